```python
import math
import jax, jax.numpy as jnp
from jax import lax
import numpy as np

D_MODEL = 4096
BATCH = 8
SEQ = 4096
DEPTH = 1

MIX_WIDTH = D_MODEL
A_HEADS = 16
NOPE_DIM = 128
ROPE_DIM = 64
V_DIM = 128
A_WIDTH = A_HEADS * V_DIM
Q_LORA = 1024
KV_LORA = 512
ROPE_THETA = 10000.0
Q_BLOCK = 128
G_WIDTH = MIX_WIDTH - A_WIDTH
G_HEADS = 16
G_HEAD_DIM = G_WIDTH // G_HEADS
CHUNK = 128
D_FF = ((8 * D_MODEL // 3 + 255) // 256) * 256
IN_WIDTH = Q_LORA + KV_LORA + ROPE_DIM + 2 * G_WIDTH
EPS = 1e-6

kernel_name = "hybrid_mla_gmlp_sandwich_block"


def rms_norm(x, g):
    xf = x.astype(jnp.float32)
    y = xf * lax.rsqrt(jnp.mean(xf * xf, axis=-1, keepdims=True) + EPS)
    return (y * g.astype(jnp.float32)).astype(x.dtype)


def layer_norm(x, g, b):
    xf = x.astype(jnp.float32)
    mu = jnp.mean(xf, axis=-1, keepdims=True)
    xc = xf - mu
    y = xc * lax.rsqrt(jnp.mean(xc * xc, axis=-1, keepdims=True) + EPS)
    return (y * g.astype(jnp.float32) + b.astype(jnp.float32)).astype(x.dtype)


def rope_tables(positions, dtype):
    inv_freq = 1.0 / (ROPE_THETA ** (jnp.arange(0, ROPE_DIM, 2, dtype=jnp.float32) / ROPE_DIM))
    ang = positions.astype(jnp.float32)[..., None] * inv_freq
    return jnp.cos(ang).astype(dtype), jnp.sin(ang).astype(dtype)


def apply_rope(t, cos, sin):
    half = t.shape[-1] // 2
    t1, t2 = t[..., :half], t[..., half:]
    return jnp.concatenate([t1 * cos - t2 * sin, t2 * cos + t1 * sin], axis=-1)


def mla_attention(q_nope, q_rope, k_nope, k_rope, v):
    B, S, H, _ = q_nope.shape
    nb = S // Q_BLOCK
    scale = 1.0 / math.sqrt(NOPE_DIM + ROPE_DIM)
    qn = q_nope.reshape(B, nb, Q_BLOCK, H, NOPE_DIM).transpose(1, 0, 2, 3, 4)
    qr = q_rope.reshape(B, nb, Q_BLOCK, H, ROPE_DIM).transpose(1, 0, 2, 3, 4)

    def block(args):
        qn_b, qr_b = args
        s = (jnp.einsum('bqhd,bkhd->bhqk', qn_b, k_nope, preferred_element_type=jnp.float32)
             + jnp.einsum('bqhr,bkr->bhqk', qr_b, k_rope, preferred_element_type=jnp.float32))
        p = jax.nn.softmax(s * scale, axis=-1).astype(v.dtype)
        return jnp.einsum('bhqk,bkhd->bqhd', p, v)

    out = lax.map(block, (qn, qr))
    return out.transpose(1, 0, 2, 3, 4).reshape(B, S, H * V_DIM)


def spatial_gating(u, v, v_ln_g, v_ln_b, w_s, b_s):
    B, S, _ = v.shape
    v = layer_norm(v, v_ln_g, v_ln_b)
    vc = v.reshape(B, S // CHUNK, CHUNK, G_HEADS, G_HEAD_DIM)
    s = jnp.einsum('gpq,bcqgd->bcpgd', w_s, vc) + jnp.transpose(b_s)[None, None, :, :, None]
    return u * s.reshape(B, S, G_WIDTH)


def _fwd_setup_inputs(seed: int = 0) -> dict:
    key = jax.random.key(seed)
    ks = jax.random.split(key, 24)
    f32 = jnp.float32
    L = DEPTH

    def nrm(k, shape, fan_in):
        return jax.random.normal(k, shape, f32) * (fan_in ** -0.5)

    def gain(k, n):
        return 1.0 + 0.02 * jax.random.normal(k, (L, n), f32)

    x = jax.random.normal(ks[0], (BATCH, SEQ, D_MODEL), f32)
    offs = jax.random.randint(ks[1], (BATCH, 1), 0, SEQ, dtype=jnp.int32)
    positions = jnp.arange(SEQ, dtype=jnp.int32)[None, :] + offs
    return {
        "x": x,
        "positions": positions,
        "pre_mix_norm": gain(ks[2], D_MODEL),
        "w_in": nrm(ks[3], (L, D_MODEL, IN_WIDTH), D_MODEL),
        "q_norm": gain(ks[4], Q_LORA),
        "kv_norm": gain(ks[5], KV_LORA),
        "w_uq": nrm(ks[6], (L, Q_LORA, A_HEADS * (NOPE_DIM + ROPE_DIM)), Q_LORA),
        "w_ukv": nrm(ks[7], (L, KV_LORA, A_HEADS * (NOPE_DIM + V_DIM)), KV_LORA),
        "v_ln_gain": gain(ks[8], G_WIDTH),
        "v_ln_bias": 0.02 * jax.random.normal(ks[9], (L, G_WIDTH), f32),
        "w_spatial": nrm(ks[10], (L, G_HEADS, CHUNK, CHUNK), CHUNK),
        "b_spatial": 1.0 + 0.1 * jax.random.normal(ks[11], (L, G_HEADS, CHUNK), f32),
        "attn_out_norm": gain(ks[12], A_WIDTH),
        "gmlp_out_norm": gain(ks[13], G_WIDTH),
        "w_out": nrm(ks[14], (L, MIX_WIDTH, D_MODEL), MIX_WIDTH),
        "post_mix_norm": gain(ks[15], D_MODEL),
        "pre_ffn_norm": gain(ks[16], D_MODEL),
        "w_gate": nrm(ks[17], (L, D_MODEL, D_FF), D_MODEL),
        "w_up": nrm(ks[18], (L, D_MODEL, D_FF), D_MODEL),
        "w_down": nrm(ks[19], (L, D_FF, D_MODEL), D_FF),
        "post_ffn_norm": gain(ks[20], D_MODEL),
    }


def _fwd_reference(x, positions, pre_mix_norm, w_in, q_norm, kv_norm, w_uq, w_ukv,
              v_ln_gain, v_ln_bias, w_spatial, b_spatial, attn_out_norm, gmlp_out_norm,
              w_out, post_mix_norm, pre_ffn_norm, w_gate, w_up, w_down, post_ffn_norm):
    B, S, _ = x.shape
    cos, sin = rope_tables(positions, x.dtype)
    splits = np.cumsum([Q_LORA, KV_LORA, ROPE_DIM, G_WIDTH]).tolist()

    for l in range(DEPTH):
        xn = rms_norm(x, pre_mix_norm[l])
        proj = jnp.einsum('bsd,de->bse', xn, w_in[l])
        q_c, kv_c, k_rope, u, v = jnp.split(proj, splits, axis=-1)

        q = jnp.einsum('bsr,re->bse', rms_norm(q_c, q_norm[l]), w_uq[l])
        q = q.reshape(B, S, A_HEADS, NOPE_DIM + ROPE_DIM)
        q_nope = q[..., :NOPE_DIM]
        q_rope = apply_rope(q[..., NOPE_DIM:], cos[:, :, None, :], sin[:, :, None, :])
        kv = jnp.einsum('bsr,re->bse', rms_norm(kv_c, kv_norm[l]), w_ukv[l])
        kv = kv.reshape(B, S, A_HEADS, NOPE_DIM + V_DIM)
        k_nope, v_a = kv[..., :NOPE_DIM], kv[..., NOPE_DIM:]
        k_rope = apply_rope(k_rope, cos, sin)
        a_out = mla_attention(q_nope, q_rope, k_nope, k_rope, v_a)

        u = jax.nn.gelu(u)
        v = jax.nn.gelu(v)
        g_out = spatial_gating(u, v, v_ln_gain[l], v_ln_bias[l], w_spatial[l], b_spatial[l])

        mixed = jnp.concatenate([rms_norm(a_out, attn_out_norm[l]),
                                 rms_norm(g_out, gmlp_out_norm[l])], axis=-1)
        mix_out = jnp.einsum('bse,ed->bsd', mixed, w_out[l])
        x = x + rms_norm(mix_out, post_mix_norm[l])

        hn = rms_norm(x, pre_ffn_norm[l])
        gate = jnp.einsum('bsd,df->bsf', hn, w_gate[l])
        up = jnp.einsum('bsd,df->bsf', hn, w_up[l])
        ffn = jnp.einsum('bsf,fd->bsd', jax.nn.silu(gate) * up, w_down[l])
        x = x + rms_norm(ffn, post_ffn_norm[l])

    return x


import jax as _jax
import jax.numpy as _jnp

TWIN_FORMAT = 'train_step'
FWD_PARAMS = ['x', 'positions', 'pre_mix_norm', 'w_in', 'q_norm', 'kv_norm', 'w_uq', 'w_ukv', 'v_ln_gain', 'v_ln_bias', 'w_spatial', 'b_spatial', 'attn_out_norm', 'gmlp_out_norm', 'w_out', 'post_mix_norm', 'pre_ffn_norm', 'w_gate', 'w_up', 'w_down', 'post_ffn_norm']
TWIN_WEIGHTS = ['pre_mix_norm', 'w_in', 'q_norm', 'kv_norm', 'w_uq', 'w_ukv', 'v_ln_gain', 'v_ln_bias', 'w_spatial', 'b_spatial', 'attn_out_norm', 'gmlp_out_norm', 'w_out', 'post_mix_norm', 'pre_ffn_norm', 'w_gate', 'w_up', 'w_down', 'post_ffn_norm']
TWIN_DIFF_INPUT = 'x'
TWIN_INPUTS = ['x', 'positions', 'pre_mix_norm', 'w_in', 'q_norm', 'kv_norm', 'w_uq', 'w_ukv', 'v_ln_gain', 'v_ln_bias', 'w_spatial', 'b_spatial', 'attn_out_norm', 'gmlp_out_norm', 'w_out', 'post_mix_norm', 'pre_ffn_norm', 'w_gate', 'w_up', 'w_down', 'post_ffn_norm', 'loss_target', 'm_pre_mix_norm', 'm_w_in', 'm_q_norm', 'm_kv_norm', 'm_w_uq', 'm_w_ukv', 'm_v_ln_gain', 'm_v_ln_bias', 'm_w_spatial', 'm_b_spatial', 'm_attn_out_norm', 'm_gmlp_out_norm', 'm_w_out', 'm_post_mix_norm', 'm_pre_ffn_norm', 'm_w_gate', 'm_w_up', 'm_w_down', 'm_post_ffn_norm', 'v_pre_mix_norm', 'v_w_in', 'v_q_norm', 'v_kv_norm', 'v_w_uq', 'v_w_ukv', 'v_v_ln_gain', 'v_v_ln_bias', 'v_w_spatial', 'v_b_spatial', 'v_attn_out_norm', 'v_gmlp_out_norm', 'v_w_out', 'v_post_mix_norm', 'v_pre_ffn_norm', 'v_w_gate', 'v_w_up', 'v_w_down', 'v_post_ffn_norm']
TWIN_OUTPUTS = ['loss', 'grad_x', 'grad_pre_mix_norm', 'grad_w_in', 'grad_q_norm', 'grad_kv_norm', 'grad_w_uq', 'grad_w_ukv', 'grad_v_ln_gain', 'grad_v_ln_bias', 'grad_w_spatial', 'grad_b_spatial', 'grad_attn_out_norm', 'grad_gmlp_out_norm', 'grad_w_out', 'grad_post_mix_norm', 'grad_pre_ffn_norm', 'grad_w_gate', 'grad_w_up', 'grad_w_down', 'grad_post_ffn_norm', 'delta_pre_mix_norm', 'delta_w_in', 'delta_q_norm', 'delta_kv_norm', 'delta_w_uq', 'delta_w_ukv', 'delta_v_ln_gain', 'delta_v_ln_bias', 'delta_w_spatial', 'delta_b_spatial', 'delta_attn_out_norm', 'delta_gmlp_out_norm', 'delta_w_out', 'delta_post_mix_norm', 'delta_pre_ffn_norm', 'delta_w_gate', 'delta_w_up', 'delta_w_down', 'delta_post_ffn_norm', 'new_m_pre_mix_norm', 'new_m_w_in', 'new_m_q_norm', 'new_m_kv_norm', 'new_m_w_uq', 'new_m_w_ukv', 'new_m_v_ln_gain', 'new_m_v_ln_bias', 'new_m_w_spatial', 'new_m_b_spatial', 'new_m_attn_out_norm', 'new_m_gmlp_out_norm', 'new_m_w_out', 'new_m_post_mix_norm', 'new_m_pre_ffn_norm', 'new_m_w_gate', 'new_m_w_up', 'new_m_w_down', 'new_m_post_ffn_norm', 'new_v_pre_mix_norm', 'new_v_w_in', 'new_v_q_norm', 'new_v_kv_norm', 'new_v_w_uq', 'new_v_w_ukv', 'new_v_v_ln_gain', 'new_v_v_ln_bias', 'new_v_w_spatial', 'new_v_b_spatial', 'new_v_attn_out_norm', 'new_v_gmlp_out_norm', 'new_v_w_out', 'new_v_post_mix_norm', 'new_v_pre_ffn_norm', 'new_v_w_gate', 'new_v_w_up', 'new_v_w_down', 'new_v_post_ffn_norm']
TWIN_LEAF_KINDS = {'loss': 'loss', 'grad_x': 'grad_x', 'grad_pre_mix_norm': 'grad_w', 'grad_w_in': 'grad_w', 'grad_q_norm': 'grad_w', 'grad_kv_norm': 'grad_w', 'grad_w_uq': 'grad_w', 'grad_w_ukv': 'grad_w', 'grad_v_ln_gain': 'grad_w', 'grad_v_ln_bias': 'grad_w', 'grad_w_spatial': 'grad_w', 'grad_b_spatial': 'grad_w', 'grad_attn_out_norm': 'grad_w', 'grad_gmlp_out_norm': 'grad_w', 'grad_w_out': 'grad_w', 'grad_post_mix_norm': 'grad_w', 'grad_pre_ffn_norm': 'grad_w', 'grad_w_gate': 'grad_w', 'grad_w_up': 'grad_w', 'grad_w_down': 'grad_w', 'grad_post_ffn_norm': 'grad_w', 'delta_pre_mix_norm': 'delta_w', 'delta_w_in': 'delta_w', 'delta_q_norm': 'delta_w', 'delta_kv_norm': 'delta_w', 'delta_w_uq': 'delta_w', 'delta_w_ukv': 'delta_w', 'delta_v_ln_gain': 'delta_w', 'delta_v_ln_bias': 'delta_w', 'delta_w_spatial': 'delta_w', 'delta_b_spatial': 'delta_w', 'delta_attn_out_norm': 'delta_w', 'delta_gmlp_out_norm': 'delta_w', 'delta_w_out': 'delta_w', 'delta_post_mix_norm': 'delta_w', 'delta_pre_ffn_norm': 'delta_w', 'delta_w_gate': 'delta_w', 'delta_w_up': 'delta_w', 'delta_w_down': 'delta_w', 'delta_post_ffn_norm': 'delta_w', 'new_m_pre_mix_norm': 'new_m', 'new_m_w_in': 'new_m', 'new_m_q_norm': 'new_m', 'new_m_kv_norm': 'new_m', 'new_m_w_uq': 'new_m', 'new_m_w_ukv': 'new_m', 'new_m_v_ln_gain': 'new_m', 'new_m_v_ln_bias': 'new_m', 'new_m_w_spatial': 'new_m', 'new_m_b_spatial': 'new_m', 'new_m_attn_out_norm': 'new_m', 'new_m_gmlp_out_norm': 'new_m', 'new_m_w_out': 'new_m', 'new_m_post_mix_norm': 'new_m', 'new_m_pre_ffn_norm': 'new_m', 'new_m_w_gate': 'new_m', 'new_m_w_up': 'new_m', 'new_m_w_down': 'new_m', 'new_m_post_ffn_norm': 'new_m', 'new_v_pre_mix_norm': 'new_v', 'new_v_w_in': 'new_v', 'new_v_q_norm': 'new_v', 'new_v_kv_norm': 'new_v', 'new_v_w_uq': 'new_v', 'new_v_w_ukv': 'new_v', 'new_v_v_ln_gain': 'new_v', 'new_v_v_ln_bias': 'new_v', 'new_v_w_spatial': 'new_v', 'new_v_b_spatial': 'new_v', 'new_v_attn_out_norm': 'new_v', 'new_v_gmlp_out_norm': 'new_v', 'new_v_w_out': 'new_v', 'new_v_post_mix_norm': 'new_v', 'new_v_pre_ffn_norm': 'new_v', 'new_v_w_gate': 'new_v', 'new_v_w_up': 'new_v', 'new_v_w_down': 'new_v', 'new_v_post_ffn_norm': 'new_v'}


def _forward(args):
    return _fwd_reference(*[args[k] for k in FWD_PARAMS])


def _output_shape():
    out = _jax.eval_shape(lambda: _forward(_fwd_setup_inputs(0)))
    return out.shape, out.dtype

N_MICROBATCH = 1
ADAM_LR = 0.001
ADAM_B1 = 0.9
ADAM_B2 = 0.999
ADAM_EPS = 1e-08
ADAM_WD = 0.01
ADAM_STEP = 10
PER_EXAMPLE_BATCH_AXIS = {'x': 0, 'positions': 0, 'loss_target': 0}
SHARED_INPUTS = []
_WEIGHT_DTYPES = {'pre_mix_norm': _jnp.float32, 'w_in': _jnp.float32, 'q_norm': _jnp.float32, 'kv_norm': _jnp.float32, 'w_uq': _jnp.float32, 'w_ukv': _jnp.float32, 'v_ln_gain': _jnp.float32, 'v_ln_bias': _jnp.float32, 'w_spatial': _jnp.float32, 'b_spatial': _jnp.float32, 'attn_out_norm': _jnp.float32, 'gmlp_out_norm': _jnp.float32, 'w_out': _jnp.float32, 'post_mix_norm': _jnp.float32, 'pre_ffn_norm': _jnp.float32, 'w_gate': _jnp.float32, 'w_up': _jnp.float32, 'w_down': _jnp.float32, 'post_ffn_norm': _jnp.float32}
MOMENT_SCALE = {'pre_mix_norm': 1.963411e-01, 'w_in': 1.680473e-01, 'q_norm': 1.871443e-01, 'kv_norm': 6.435479e-01, 'w_uq': 1.015437e-01, 'w_ukv': 1.829129e-01, 'v_ln_gain': 8.123167e-02, 'v_ln_bias': 7.764455e-02, 'w_spatial': 7.727772e-02, 'b_spatial': 7.944263e-02, 'attn_out_norm': 2.419275e-01, 'gmlp_out_norm': 2.093388e-01, 'w_out': 2.178013e-01, 'post_mix_norm': 8.020602e+00, 'pre_ffn_norm': 1.719182e-01, 'w_gate': 6.252440e-02, 'w_up': 9.218202e-02, 'w_down': 1.488759e-01, 'post_ffn_norm': 8.011796e+00}


def _to_microbatches(a, axis):
    t = _jnp.moveaxis(a, axis, 0)
    t = t.reshape((N_MICROBATCH, t.shape[0] // N_MICROBATCH) + t.shape[1:])
    return _jnp.moveaxis(t, 1, axis + 1)


def setup_inputs(seed: int = 0) -> dict:
    inp = _fwd_setup_inputs(seed)
    key = _jax.random.fold_in(_jax.random.key(seed), 7919)
    shape, _ = _output_shape()
    out = dict(inp)
    out["loss_target"] = _jax.random.normal(_jax.random.fold_in(key, 0), shape, _jnp.float32)
    for i, name in enumerate(TWIN_WEIGHTS):
        w = inp[name].astype(_jnp.float32)
        if MOMENT_SCALE is None:
            s = _jnp.sqrt(_jnp.mean(_jnp.square(w)) + 1e-30)
        else:
            s = MOMENT_SCALE[name]
        km, kv = _jax.random.split(_jax.random.fold_in(key, i + 1))
        out[name] = w
        out["m_" + name] = s * _jax.random.normal(km, w.shape, _jnp.float32)
        out["v_" + name] = (s * s) * _jax.random.uniform(kv, w.shape, _jnp.float32, 0.5, 1.5)
    if N_MICROBATCH > 1:
        for name, axis in PER_EXAMPLE_BATCH_AXIS.items():
            out[name] = _to_microbatches(out[name], axis)
    return {'x': out['x'], 'positions': out['positions'], 'pre_mix_norm': out['pre_mix_norm'], 'w_in': out['w_in'], 'q_norm': out['q_norm'], 'kv_norm': out['kv_norm'], 'w_uq': out['w_uq'], 'w_ukv': out['w_ukv'], 'v_ln_gain': out['v_ln_gain'], 'v_ln_bias': out['v_ln_bias'], 'w_spatial': out['w_spatial'], 'b_spatial': out['b_spatial'], 'attn_out_norm': out['attn_out_norm'], 'gmlp_out_norm': out['gmlp_out_norm'], 'w_out': out['w_out'], 'post_mix_norm': out['post_mix_norm'], 'pre_ffn_norm': out['pre_ffn_norm'], 'w_gate': out['w_gate'], 'w_up': out['w_up'], 'w_down': out['w_down'], 'post_ffn_norm': out['post_ffn_norm'], 'loss_target': out['loss_target'], 'm_pre_mix_norm': out['m_pre_mix_norm'], 'm_w_in': out['m_w_in'], 'm_q_norm': out['m_q_norm'], 'm_kv_norm': out['m_kv_norm'], 'm_w_uq': out['m_w_uq'], 'm_w_ukv': out['m_w_ukv'], 'm_v_ln_gain': out['m_v_ln_gain'], 'm_v_ln_bias': out['m_v_ln_bias'], 'm_w_spatial': out['m_w_spatial'], 'm_b_spatial': out['m_b_spatial'], 'm_attn_out_norm': out['m_attn_out_norm'], 'm_gmlp_out_norm': out['m_gmlp_out_norm'], 'm_w_out': out['m_w_out'], 'm_post_mix_norm': out['m_post_mix_norm'], 'm_pre_ffn_norm': out['m_pre_ffn_norm'], 'm_w_gate': out['m_w_gate'], 'm_w_up': out['m_w_up'], 'm_w_down': out['m_w_down'], 'm_post_ffn_norm': out['m_post_ffn_norm'], 'v_pre_mix_norm': out['v_pre_mix_norm'], 'v_w_in': out['v_w_in'], 'v_q_norm': out['v_q_norm'], 'v_kv_norm': out['v_kv_norm'], 'v_w_uq': out['v_w_uq'], 'v_w_ukv': out['v_w_ukv'], 'v_v_ln_gain': out['v_v_ln_gain'], 'v_v_ln_bias': out['v_v_ln_bias'], 'v_w_spatial': out['v_w_spatial'], 'v_b_spatial': out['v_b_spatial'], 'v_attn_out_norm': out['v_attn_out_norm'], 'v_gmlp_out_norm': out['v_gmlp_out_norm'], 'v_w_out': out['v_w_out'], 'v_post_mix_norm': out['v_post_mix_norm'], 'v_pre_ffn_norm': out['v_pre_ffn_norm'], 'v_w_gate': out['v_w_gate'], 'v_w_up': out['v_w_up'], 'v_w_down': out['v_w_down'], 'v_post_ffn_norm': out['v_post_ffn_norm']}


def _loss(weights, diff, rest, loss_target):
    with _jax.named_scope("forward"):
        args = {**rest, TWIN_DIFF_INPUT: diff, **{k: w.astype(_WEIGHT_DTYPES[k]) for k, w in weights.items()}}
        y = _forward(args)
    with _jax.named_scope("loss_head"):
        err = _jnp.square(y.astype(_jnp.float32) - loss_target)
        return 0.5 * _jnp.sum(_jnp.mean(err, axis=-1)) if err.ndim else 0.5 * err


def _adamw(w, g, m, v):
    m = ADAM_B1 * m + (1.0 - ADAM_B1) * g
    v = ADAM_B2 * v + (1.0 - ADAM_B2) * _jnp.square(g)
    m_hat = m / (1.0 - ADAM_B1 ** ADAM_STEP)
    v_hat = v / (1.0 - ADAM_B2 ** ADAM_STEP)
    delta = -ADAM_LR * (m_hat / (_jnp.sqrt(v_hat) + ADAM_EPS) + ADAM_WD * w)
    return delta, m, v


def reference(x, positions, pre_mix_norm, w_in, q_norm, kv_norm, w_uq, w_ukv, v_ln_gain, v_ln_bias, w_spatial, b_spatial, attn_out_norm, gmlp_out_norm, w_out, post_mix_norm, pre_ffn_norm, w_gate, w_up, w_down, post_ffn_norm, loss_target, m_pre_mix_norm, m_w_in, m_q_norm, m_kv_norm, m_w_uq, m_w_ukv, m_v_ln_gain, m_v_ln_bias, m_w_spatial, m_b_spatial, m_attn_out_norm, m_gmlp_out_norm, m_w_out, m_post_mix_norm, m_pre_ffn_norm, m_w_gate, m_w_up, m_w_down, m_post_ffn_norm, v_pre_mix_norm, v_w_in, v_q_norm, v_kv_norm, v_w_uq, v_w_ukv, v_v_ln_gain, v_v_ln_bias, v_w_spatial, v_b_spatial, v_attn_out_norm, v_gmlp_out_norm, v_w_out, v_post_mix_norm, v_pre_ffn_norm, v_w_gate, v_w_up, v_w_down, v_post_ffn_norm):
    given = dict(x=x, positions=positions, pre_mix_norm=pre_mix_norm, w_in=w_in, q_norm=q_norm, kv_norm=kv_norm, w_uq=w_uq, w_ukv=w_ukv, v_ln_gain=v_ln_gain, v_ln_bias=v_ln_bias, w_spatial=w_spatial, b_spatial=b_spatial, attn_out_norm=attn_out_norm, gmlp_out_norm=gmlp_out_norm, w_out=w_out, post_mix_norm=post_mix_norm, pre_ffn_norm=pre_ffn_norm, w_gate=w_gate, w_up=w_up, w_down=w_down, post_ffn_norm=post_ffn_norm, loss_target=loss_target, m_pre_mix_norm=m_pre_mix_norm, m_w_in=m_w_in, m_q_norm=m_q_norm, m_kv_norm=m_kv_norm, m_w_uq=m_w_uq, m_w_ukv=m_w_ukv, m_v_ln_gain=m_v_ln_gain, m_v_ln_bias=m_v_ln_bias, m_w_spatial=m_w_spatial, m_b_spatial=m_b_spatial, m_attn_out_norm=m_attn_out_norm, m_gmlp_out_norm=m_gmlp_out_norm, m_w_out=m_w_out, m_post_mix_norm=m_post_mix_norm, m_pre_ffn_norm=m_pre_ffn_norm, m_w_gate=m_w_gate, m_w_up=m_w_up, m_w_down=m_w_down, m_post_ffn_norm=m_post_ffn_norm, v_pre_mix_norm=v_pre_mix_norm, v_w_in=v_w_in, v_q_norm=v_q_norm, v_kv_norm=v_kv_norm, v_w_uq=v_w_uq, v_w_ukv=v_w_ukv, v_v_ln_gain=v_v_ln_gain, v_v_ln_bias=v_v_ln_bias, v_w_spatial=v_w_spatial, v_b_spatial=v_b_spatial, v_attn_out_norm=v_attn_out_norm, v_gmlp_out_norm=v_gmlp_out_norm, v_w_out=v_w_out, v_post_mix_norm=v_post_mix_norm, v_pre_ffn_norm=v_pre_ffn_norm, v_w_gate=v_w_gate, v_w_up=v_w_up, v_w_down=v_w_down, v_post_ffn_norm=v_post_ffn_norm)
    weights = {n: given[n] for n in TWIN_WEIGHTS}
    shared = {n: given[n] for n in SHARED_INPUTS}
    per_example = {n: given[n] for n in ['x', 'positions']}
    grad_fn = _jax.value_and_grad(_loss, argnums=(0, 1))

    def one_microbatch(ex, loss_target):
        ex = dict(ex)
        diff = ex.pop(TWIN_DIFF_INPUT)
        return grad_fn(weights, diff, {**shared, **ex}, loss_target)

    if N_MICROBATCH == 1:
        loss, (grad_w, grad_x) = one_microbatch(per_example, given["loss_target"])
    else:
        def body(carry, xs):
            loss_sum, grad_sum = carry
            l_k, (gw_k, gx_k) = one_microbatch(xs[0], xs[1])
            with _jax.named_scope("update"):
                return (loss_sum + l_k, _jax.tree.map(_jnp.add, grad_sum, gw_k)), gx_k

        init = (_jnp.zeros((), _jnp.float32), _jax.tree.map(_jnp.zeros_like, weights))
        (loss, grad_w), grad_x = _jax.lax.scan(body, init, (per_example, given["loss_target"]))
    with _jax.named_scope("update"):
        delta_w, new_m, new_v = {}, {}, {}
        for n in TWIN_WEIGHTS:
            delta_w[n], new_m[n], new_v[n] = _adamw(weights[n], grad_w[n], given["m_" + n], given["v_" + n])
    return (loss, grad_x, *[grad_w[n] for n in TWIN_WEIGHTS], *[delta_w[n] for n in TWIN_WEIGHTS],
            *[new_m[n] for n in TWIN_WEIGHTS], *[new_v[n] for n in TWIN_WEIGHTS])
```

```python
import functools
import math

import jax
import jax.numpy as jnp
from jax import lax
from jax.experimental import pallas as pl
from jax.experimental.pallas import tpu as pltpu

F32 = jnp.float32
BF16 = jnp.bfloat16
MESH = pl.DeviceIdType.MESH

NOPE_DIM = 128
ROPE_DIM = 64
ROPE_HALF = ROPE_DIM // 2
V_DIM = 128
HEAD_PAD = 256
G_HEAD_DIM = 128
CHUNK = 128
ROPE_THETA = 10000.0
EPS = 1e-6
ADAM_LR = 0.001
ADAM_B1 = 0.9
ADAM_B2 = 0.999
ADAM_EPS = 1e-08
ADAM_WD = 0.01
ADAM_STEP = 10

LANES = 128
MATMUL_TILE = 1024
VMEM_LIMIT_BYTES = 48 * 1024 * 1024

NN = (((1,), (0,)), ((), ()))
NT = (((1,), (1,)), ((), ()))
TN = (((0,), (0,)), ((), ()))


def _params(semantics):
    return pltpu.CompilerParams(dimension_semantics=semantics, vmem_limit_bytes=VMEM_LIMIT_BYTES)


def _tile(n, cap=MATMUL_TILE):
    if n <= cap:
        return n
    t = cap - cap % LANES
    while n % t:
        t -= LANES
    assert t > 0, n
    return t


def _round_up(n, m):
    return (n + m - 1) // m * m


def _matmul(a, b, dims, out_dtype, name, addend=None):
    if dims is NN:
        (m, k), (k2, n) = a.shape, b.shape
    elif dims is NT:
        (m, k), (n, k2) = a.shape, b.shape
    else:
        (k, m), (k2, n) = a.shape, b.shape
    assert k == k2, (a.shape, b.shape, name)
    tm, tn, tk = _tile(m), _tile(n), _tile(k)
    nk = k // tk

    def body(*refs):
        if addend is None:
            a_ref, b_ref, o_ref, acc_ref = refs
        else:
            a_ref, b_ref, c_ref, o_ref, acc_ref = refs
        kk = pl.program_id(2)

        @pl.when(kk == 0)
        def _():
            acc_ref[...] = jnp.zeros_like(acc_ref)

        acc_ref[...] += lax.dot_general(a_ref[...], b_ref[...], dims, preferred_element_type=F32)

        @pl.when(kk == nk - 1)
        def _():
            r = acc_ref[...]
            if addend is not None:
                r = r + c_ref[...]
            o_ref[...] = r.astype(o_ref.dtype)

    if dims is TN:
        a_spec = pl.BlockSpec((tk, tm), lambda i, j, kk: (kk, i))
    else:
        a_spec = pl.BlockSpec((tm, tk), lambda i, j, kk: (i, kk))
    if dims is NT:
        b_spec = pl.BlockSpec((tn, tk), lambda i, j, kk: (j, kk))
    else:
        b_spec = pl.BlockSpec((tk, tn), lambda i, j, kk: (kk, j))
    o_spec = pl.BlockSpec((tm, tn), lambda i, j, kk: (i, j))
    in_specs = [a_spec, b_spec] + ([o_spec] if addend is not None else [])
    args = (a, b) + ((addend,) if addend is not None else ())
    return pl.pallas_call(
        body, name=name, grid=(m // tm, n // tn, nk), in_specs=in_specs, out_specs=o_spec,
        out_shape=jax.ShapeDtypeStruct((m, n), out_dtype),
        scratch_shapes=[pltpu.VMEM((tm, tn), F32)],
        compiler_params=_params(("parallel", "parallel", "arbitrary")),
    )(*args)


def _row_call(body, name, rows, tr, row_ins, par_ins, row_outs, acc_outs):
    def col(i, cb):
        return (i, cb)

    def whole(i, nd):
        return (0,) * nd

    in_specs = [pl.BlockSpec((tr, w), functools.partial(col, cb=cb)) for (_, w, cb) in row_ins]
    in_specs += [pl.BlockSpec(a.shape, functools.partial(whole, nd=a.ndim)) for a in par_ins]
    out_specs = [pl.BlockSpec((tr, w), lambda i: (i, 0)) for (w, _) in row_outs]
    out_specs += [pl.BlockSpec(s, functools.partial(whole, nd=len(s))) for (s, _) in acc_outs]
    out_shape = [jax.ShapeDtypeStruct((rows, w), dt) for (w, dt) in row_outs]
    out_shape += [jax.ShapeDtypeStruct(s, dt) for (s, dt) in acc_outs]
    return pl.pallas_call(
        body, name=name, grid=(rows // tr,), in_specs=in_specs, out_specs=out_specs, out_shape=out_shape,
        compiler_params=_params(("arbitrary",) if acc_outs else ("parallel",)),
    )(*[a for (a, _, _) in row_ins], *par_ins)


def _accumulate(ref, val):
    i = pl.program_id(0)

    @pl.when(i == 0)
    def _():
        ref[...] = val

    @pl.when(i > 0)
    def _():
        ref[...] += val


def _colsum(v):
    return jnp.sum(v, axis=0, keepdims=True)


def _rms_fwd(x, g):
    r = lax.rsqrt(jnp.mean(x * x, axis=-1, keepdims=True) + EPS)
    return x * r * g


def _rms_bwd(x, g, dy):
    r = lax.rsqrt(jnp.mean(x * x, axis=-1, keepdims=True) + EPS)
    xh = x * r
    dxh = dy * g
    dx = r * (dxh - xh * jnp.mean(dxh * xh, axis=-1, keepdims=True))
    return dx, dy * xh


_GELU_C = math.sqrt(2.0 / math.pi)
_GELU_A = 0.044715


def _gelu(x):
    return 0.5 * x * (1.0 + jnp.tanh(_GELU_C * (x + _GELU_A * (x * x * x))))


def _gelu_grad(x):
    t = jnp.tanh(_GELU_C * (x + _GELU_A * (x * x * x)))
    return 0.5 * (1.0 + t) + 0.5 * x * (1.0 - t * t) * (_GELU_C * (1.0 + 3.0 * _GELU_A * (x * x)))


def _sigmoid(x):
    return 1.0 / (1.0 + jnp.exp(-x))


def _rope_fwd(t, cos_t, sin_t):
    return t * cos_t + pltpu.roll(t, 2 * ROPE_HALF, 1) * sin_t


def _rope_bwd(dt, cos_t, sin_t):
    return dt * cos_t - pltpu.roll(dt, 2 * ROPE_HALF, 1) * sin_t


def _prenorm(x, g, tr):
    def body(x_ref, g_ref, o_ref):
        o_ref[...] = _rms_fwd(x_ref[...], g_ref[...]).astype(BF16)

    t, d = x.shape
    return _row_call(body, "prenorm", t, tr, [(x, d, 0)], [g], [(d, BF16)], [])[0]


def _qkv_prep(proj, g_q, g_kv, cos_t, sin_t, lay, tr):
    ql, kl = lay["ql"], lay["kl"]

    def body(q_ref, kv_ref, kr_ref, cos_ref, sin_ref, gq_ref, gkv_ref, qn_ref, kvn_ref, kro_ref):
        qn_ref[...] = _rms_fwd(q_ref[...], gq_ref[...]).astype(BF16)
        kvn_ref[...] = _rms_fwd(kv_ref[...], gkv_ref[...]).astype(BF16)
        kro_ref[...] = _rope_fwd(kr_ref[...], cos_ref[...], sin_ref[...]).astype(BF16)

    t = proj.shape[0]
    return _row_call(
        body, "qkv_prep", t, tr,
        [(proj, ql, lay["q_off"] // ql), (proj, kl, lay["kv_off"] // kl), (proj, LANES, lay["kr_off"] // LANES),
         (cos_t, LANES, 0), (sin_t, LANES, 0)],
        [g_q, g_kv], [(ql, BF16), (kl, BF16), (LANES, BF16)], [])


def _q_rope(q, cos_t, sin_t, heads, tr):
    def body(q_ref, cos_ref, sin_ref, o_ref):
        c, s = cos_ref[...], sin_ref[...]
        for h in range(heads):
            lo = h * HEAD_PAD
            o_ref[:, lo:lo + NOPE_DIM] = q_ref[:, lo:lo + NOPE_DIM].astype(BF16)
            o_ref[:, lo + NOPE_DIM:lo + HEAD_PAD] = _rope_fwd(q_ref[:, lo + NOPE_DIM:lo + HEAD_PAD], c, s).astype(BF16)

    t, w = q.shape
    return _row_call(body, "q_rope", t, tr, [(q, w, 0), (cos_t, LANES, 0), (sin_t, LANES, 0)], [], [(w, BF16)], [])[0]


def _softmax_rows(s):
    m = jnp.max(s, axis=-1, keepdims=True)
    p = jnp.exp(s - m)
    return p * (1.0 / jnp.sum(p, axis=-1, keepdims=True))


def _attn_tile(t):
    return 256 if t % 256 == 0 else 128


def _attn_fwd(q, kv, kr, heads):
    t = q.shape[0]
    tq = _attn_tile(t)
    scale = 1.0 / math.sqrt(NOPE_DIM + ROPE_DIM)

    def body(q_ref, kv_ref, kr_ref, o_ref, kcat):
        @pl.when(pl.program_id(1) == 0)
        def _():
            kcat[:, :NOPE_DIM] = kv_ref[:, :NOPE_DIM]
            kcat[:, NOPE_DIM:] = kr_ref[...]

        s = lax.dot_general(q_ref[...], kcat[...], NT, preferred_element_type=F32) * scale
        p = _softmax_rows(s)
        o_ref[...] = jnp.dot(p.astype(BF16), kv_ref[:, NOPE_DIM:], preferred_element_type=F32)

    return pl.pallas_call(
        body, name="attn_fwd", grid=(heads, t // tq),
        in_specs=[pl.BlockSpec((tq, HEAD_PAD), lambda h, i: (i, h)),
                  pl.BlockSpec((t, HEAD_PAD), lambda h, i: (0, h)),
                  pl.BlockSpec((t, LANES), lambda h, i: (0, 0))],
        out_specs=pl.BlockSpec((tq, V_DIM), lambda h, i: (i, h)),
        out_shape=jax.ShapeDtypeStruct((t, heads * V_DIM), F32),
        scratch_shapes=[pltpu.VMEM((t, HEAD_PAD), BF16)],
        compiler_params=_params(("arbitrary", "arbitrary")),
    )(q, kv, kr)


def _attn_bwd(q, kv, kr, d_out, cos_t, sin_t, heads):
    t = q.shape[0]
    tq = _attn_tile(t)
    nq = t // tq
    scale = 1.0 / math.sqrt(NOPE_DIM + ROPE_DIM)

    def body(q_ref, kv_ref, kr_ref, do_ref, cos_ref, sin_ref, dq_ref, dkv_ref, dkr_ref, kcat, dk_acc, dv_acc):
        h, i = pl.program_id(0), pl.program_id(1)

        @pl.when(i == 0)
        def _():
            kcat[:, :NOPE_DIM] = kv_ref[:, :NOPE_DIM]
            kcat[:, NOPE_DIM:] = kr_ref[...]
            dk_acc[...] = jnp.zeros_like(dk_acc)
            dv_acc[...] = jnp.zeros_like(dv_acc)

        @pl.when((h == 0) & (i == 0))
        def _():
            dkr_ref[...] = jnp.zeros_like(dkr_ref)

        qb, dob = q_ref[...], do_ref[...]
        s = lax.dot_general(qb, kcat[...], NT, preferred_element_type=F32) * scale
        p = _softmax_rows(s)
        dv_acc[...] += lax.dot_general(p.astype(BF16), dob, TN, preferred_element_type=F32)
        dp = lax.dot_general(dob, kv_ref[:, NOPE_DIM:], NT, preferred_element_type=F32)
        ds = (p * (dp - jnp.sum(dp * p, axis=-1, keepdims=True)) * scale).astype(BF16)
        dq = jnp.dot(ds, kcat[...], preferred_element_type=F32)
        dq_ref[:, :NOPE_DIM] = dq[:, :NOPE_DIM].astype(BF16)
        dq_ref[:, NOPE_DIM:] = _rope_bwd(dq[:, NOPE_DIM:], cos_ref[...], sin_ref[...]).astype(BF16)
        dk_acc[...] += lax.dot_general(ds, qb, TN, preferred_element_type=F32)

        @pl.when(i == nq - 1)
        def _():
            dkv_ref[:, :NOPE_DIM] = dk_acc[:, :NOPE_DIM].astype(BF16)
            dkv_ref[:, NOPE_DIM:] = dv_acc[...].astype(BF16)
            dkr_ref[...] += dk_acc[:, NOPE_DIM:]

    return pl.pallas_call(
        body, name="attn_bwd", grid=(heads, nq),
        in_specs=[pl.BlockSpec((tq, HEAD_PAD), lambda h, i: (i, h)),
                  pl.BlockSpec((t, HEAD_PAD), lambda h, i: (0, h)),
                  pl.BlockSpec((t, LANES), lambda h, i: (0, 0)),
                  pl.BlockSpec((tq, V_DIM), lambda h, i: (i, h)),
                  pl.BlockSpec((tq, LANES), lambda h, i: (i, 0)),
                  pl.BlockSpec((tq, LANES), lambda h, i: (i, 0))],
        out_specs=[pl.BlockSpec((tq, HEAD_PAD), lambda h, i: (i, h)),
                   pl.BlockSpec((t, HEAD_PAD), lambda h, i: (0, h)),
                   pl.BlockSpec((t, LANES), lambda h, i: (0, 0))],
        out_shape=[jax.ShapeDtypeStruct((t, heads * HEAD_PAD), BF16),
                   jax.ShapeDtypeStruct((t, heads * HEAD_PAD), BF16),
                   jax.ShapeDtypeStruct((t, LANES), F32)],
        scratch_shapes=[pltpu.VMEM((t, HEAD_PAD), BF16), pltpu.VMEM((t, HEAD_PAD), F32), pltpu.VMEM((t, V_DIM), F32)],
        compiler_params=_params(("arbitrary", "arbitrary")),
    )(q, kv, kr, d_out, cos_t, sin_t)


def _layer_norm_parts(x):
    mu = jnp.mean(x, axis=-1, keepdims=True)
    xc = x - mu
    r = lax.rsqrt(jnp.mean(xc * xc, axis=-1, keepdims=True) + EPS)
    return xc * r, r


def _gmlp_fwd(proj, ln_g, ln_b, w_s, b_sb, g_out_norm, lay):
    gw = lay["gw"]
    g_heads = gw // G_HEAD_DIM

    def body(u_ref, v_ref, lng_ref, lnb_ref, ws_ref, bs_ref, gn_ref, o_ref, gate_ref):
        gu = _gelu(u_ref[...])
        vh, _ = _layer_norm_parts(_gelu(v_ref[...]))
        vln = (vh * lng_ref[...] + lnb_ref[...]).astype(BF16)
        for g in range(g_heads):
            cols = slice(g * G_HEAD_DIM, (g + 1) * G_HEAD_DIM)
            s = jnp.dot(ws_ref[g], vln[:, cols], preferred_element_type=F32) + bs_ref[g]
            gate_ref[:, cols] = gu[:, cols] * s
        o_ref[...] = _rms_fwd(gate_ref[...], gn_ref[...]).astype(BF16)

    t = proj.shape[0]
    in_specs = [pl.BlockSpec((CHUNK, gw), lambda i: (i, 0)), pl.BlockSpec((CHUNK, gw), lambda i: (i, 1))]
    pars = [ln_g, ln_b, w_s, b_sb, g_out_norm]
    in_specs += [pl.BlockSpec(a.shape, functools.partial(lambda i, nd: (0,) * nd, nd=a.ndim)) for a in pars]
    return pl.pallas_call(
        body, name="gmlp_fwd", grid=(t // CHUNK,), in_specs=in_specs,
        out_specs=pl.BlockSpec((CHUNK, gw), lambda i: (i, 0)),
        out_shape=jax.ShapeDtypeStruct((t, gw), BF16),
        scratch_shapes=[pltpu.VMEM((CHUNK, gw), F32)],
        compiler_params=_params(("parallel",)),
    )(proj, proj, *pars)


def _gmlp_bwd(proj, d_mixed, ln_g, ln_b, w_s, w_st, b_sb, g_out_norm, lay):
    gw = lay["gw"]
    g_heads = gw // G_HEAD_DIM
    aw_blocks = lay["aw"] // gw

    def body(u_ref, v_ref, dm_ref, lng_ref, lnb_ref, ws_ref, wst_ref, bs_ref, gn_ref,
             du_ref, dv_ref, dgn_ref, dlng_ref, dlnb_ref, dws_ref, dbs_ref, gate_ref, s_ref, dvln_ref):
        i = pl.program_id(0)
        u, v = u_ref[...], v_ref[...]
        gu, gv = _gelu(u), _gelu(v)
        vh, r_ln = _layer_norm_parts(gv)
        vln = (vh * lng_ref[...] + lnb_ref[...]).astype(BF16)
        for g in range(g_heads):
            cols = slice(g * G_HEAD_DIM, (g + 1) * G_HEAD_DIM)
            s = jnp.dot(ws_ref[g], vln[:, cols], preferred_element_type=F32) + bs_ref[g]
            s_ref[:, cols] = s
            gate_ref[:, cols] = gu[:, cols] * s
        d_gate, dgn = _rms_bwd(gate_ref[...], gn_ref[...], dm_ref[...])
        _accumulate(dgn_ref, _colsum(dgn))
        du_ref[...] = (d_gate * s_ref[...] * _gelu_grad(u)).astype(BF16)
        d_s = d_gate * gu
        d_sb = d_s.astype(BF16)
        for g in range(g_heads):
            cols = slice(g * G_HEAD_DIM, (g + 1) * G_HEAD_DIM)
            dw = lax.dot_general(d_sb[:, cols], vln[:, cols], NT, preferred_element_type=F32)

            @pl.when(i == 0)
            def _():
                dws_ref[g] = dw
                dbs_ref[g] = d_s[:, cols]

            @pl.when(i > 0)
            def _():
                dws_ref[g] += dw
                dbs_ref[g] += d_s[:, cols]

            dvln_ref[:, cols] = jnp.dot(wst_ref[g], d_sb[:, cols], preferred_element_type=F32)
        d_vln = dvln_ref[...]
        _accumulate(dlng_ref, _colsum(d_vln * vh))
        _accumulate(dlnb_ref, _colsum(d_vln))
        d_vh = d_vln * lng_ref[...]
        d_gv = r_ln * (d_vh - jnp.mean(d_vh, axis=-1, keepdims=True)
                       - vh * jnp.mean(d_vh * vh, axis=-1, keepdims=True))
        dv_ref[...] = (d_gv * _gelu_grad(v)).astype(BF16)

    t = proj.shape[0]
    whole = lambda a: pl.BlockSpec(a.shape, functools.partial(lambda i, nd: (0,) * nd, nd=a.ndim))
    pars = [ln_g, ln_b, w_s, w_st, b_sb, g_out_norm]
    hshape = (g_heads, CHUNK, CHUNK)
    return pl.pallas_call(
        body, name="gmlp_bwd", grid=(t // CHUNK,),
        in_specs=[pl.BlockSpec((CHUNK, gw), lambda i: (i, 0)), pl.BlockSpec((CHUNK, gw), lambda i: (i, 1)),
                  pl.BlockSpec((CHUNK, gw), lambda i: (i, aw_blocks))] + [whole(a) for a in pars],
        out_specs=[pl.BlockSpec((CHUNK, gw), lambda i: (i, 0)), pl.BlockSpec((CHUNK, gw), lambda i: (i, 0)),
                   pl.BlockSpec((1, gw), lambda i: (0, 0)), pl.BlockSpec((1, gw), lambda i: (0, 0)),
                   pl.BlockSpec((1, gw), lambda i: (0, 0)),
                   pl.BlockSpec(hshape, lambda i: (0, 0, 0)), pl.BlockSpec(hshape, lambda i: (0, 0, 0))],
        out_shape=[jax.ShapeDtypeStruct((t, gw), BF16), jax.ShapeDtypeStruct((t, gw), BF16),
                   jax.ShapeDtypeStruct((1, gw), F32), jax.ShapeDtypeStruct((1, gw), F32),
                   jax.ShapeDtypeStruct((1, gw), F32),
                   jax.ShapeDtypeStruct(hshape, F32), jax.ShapeDtypeStruct(hshape, F32)],
        scratch_shapes=[pltpu.VMEM((CHUNK, gw), F32), pltpu.VMEM((CHUNK, gw), F32), pltpu.VMEM((CHUNK, gw), F32)],
        compiler_params=_params(("arbitrary",)),
    )(proj, proj, d_mixed, *pars)


def _spatial_bias_grad(dbs_wide):
    g_heads = dbs_wide.shape[0]

    def body(x_ref, o_ref):
        for g in range(g_heads):
            o_ref[g:g + 1, :] = jnp.sum(x_ref[g].T, axis=0, keepdims=True)

    return pl.pallas_call(
        body, name="spatial_bias_grad", out_shape=jax.ShapeDtypeStruct((g_heads, CHUNK), F32),
        in_specs=[pl.BlockSpec(memory_space=pltpu.VMEM)], out_specs=pl.BlockSpec(memory_space=pltpu.VMEM),
    )(dbs_wide)


def _mix_norm(a_out, gn, g_a, tr):
    aw = a_out.shape[1]
    gw = gn.shape[1]

    def body(a_ref, gn_ref, g_ref, o_ref):
        o_ref[:, :aw] = _rms_fwd(a_ref[...], g_ref[...]).astype(BF16)
        o_ref[:, aw:] = gn_ref[...]

    t = a_out.shape[0]
    return _row_call(body, "mix_norm", t, tr, [(a_out, aw, 0), (gn, gw, 0)], [g_a], [(aw + gw, BF16)], [])[0]


def _mix_norm_bwd(a_out, d_mixed, g_a, tr):
    aw = a_out.shape[1]

    def body(a_ref, dm_ref, g_ref, da_ref, dg_ref):
        dx, dg = _rms_bwd(a_ref[...], g_ref[...], dm_ref[...])
        da_ref[...] = dx.astype(BF16)
        _accumulate(dg_ref, _colsum(dg))

    t = a_out.shape[0]
    return _row_call(body, "mix_norm_bwd", t, tr, [(a_out, aw, 0), (d_mixed, aw, 0)], [g_a],
                     [(aw, BF16)], [((1, aw), F32)])


def _post_mix(x, mix_out, g_pm, g_pf, tr):
    def body(x_ref, mo_ref, gpm_ref, gpf_ref, h_ref, hn_ref):
        h = x_ref[...] + _rms_fwd(mo_ref[...], gpm_ref[...])
        h_ref[...] = h
        hn_ref[...] = _rms_fwd(h, gpf_ref[...]).astype(BF16)

    t, d = x.shape
    return _row_call(body, "post_mix", t, tr, [(x, d, 0), (mix_out, d, 0)], [g_pm, g_pf], [(d, F32), (d, BF16)], [])


def _post_mix_bwd(mix_out, h, dy, d_hn, g_pm, g_pf, tr):
    def body(mo_ref, h_ref, dy_ref, dhn_ref, gpm_ref, gpf_ref, dh_ref, dmo_ref, dgpf_ref, dgpm_ref):
        dx, dg = _rms_bwd(h_ref[...], gpf_ref[...], dhn_ref[...])
        dh = dy_ref[...] + dx
        dh_ref[...] = dh
        _accumulate(dgpf_ref, _colsum(dg))
        dmo, dg2 = _rms_bwd(mo_ref[...], gpm_ref[...], dh)
        dmo_ref[...] = dmo.astype(BF16)
        _accumulate(dgpm_ref, _colsum(dg2))

    t, d = h.shape
    return _row_call(body, "post_mix_bwd", t, tr, [(mix_out, d, 0), (h, d, 0), (dy, d, 0), (d_hn, d, 0)],
                     [g_pm, g_pf], [(d, F32), (d, BF16)], [((1, d), F32), ((1, d), F32)])


def _swiglu(gate, up):
    t, f = gate.shape
    tr, tf = _tile(t, 512), _tile(f, 2048)

    def body(g_ref, u_ref, o_ref):
        g = g_ref[...].astype(F32)
        o_ref[...] = (g * _sigmoid(g) * u_ref[...].astype(F32)).astype(BF16)

    spec = pl.BlockSpec((tr, tf), lambda i, j: (i, j))
    return pl.pallas_call(body, name="swiglu", grid=(t // tr, f // tf), in_specs=[spec, spec], out_specs=spec,
                          out_shape=jax.ShapeDtypeStruct((t, f), BF16),
                          compiler_params=_params(("parallel", "parallel")))(gate, up)


def _swiglu_bwd(gate, up, d_act):
    t, f = gate.shape
    tr, tf = _tile(t, 512), _tile(f, 2048)

    def body(g_ref, u_ref, da_ref, dg_ref, du_ref):
        g, u, da = g_ref[...].astype(F32), u_ref[...].astype(F32), da_ref[...].astype(F32)
        sg = _sigmoid(g)
        du_ref[...] = (da * (g * sg)).astype(BF16)
        dg_ref[...] = (da * u * (sg * (1.0 + g * (1.0 - sg)))).astype(BF16)

    spec = pl.BlockSpec((tr, tf), lambda i, j: (i, j))
    shape = jax.ShapeDtypeStruct((t, f), BF16)
    return pl.pallas_call(body, name="swiglu_bwd", grid=(t // tr, f // tf), in_specs=[spec, spec, spec],
                          out_specs=[spec, spec], out_shape=[shape, shape],
                          compiler_params=_params(("parallel", "parallel")))(gate, up, d_act)


def _loss_head(h, ffn, target, g_po, tr):
    t, d = h.shape

    def body(h_ref, f_ref, t_ref, g_ref, dy_ref, df_ref, dg_ref, loss_ref):
        f = f_ref[...]
        err = h_ref[...] + _rms_fwd(f, g_ref[...]) - t_ref[...]
        dy = err * (1.0 / d)
        dy_ref[...] = dy
        df, dg = _rms_bwd(f, g_ref[...], dy)
        df_ref[...] = df.astype(BF16)
        _accumulate(dg_ref, _colsum(dg))
        sq = jnp.sum(_colsum(err * err), axis=-1, keepdims=True) * (0.5 / d)
        _accumulate(loss_ref, jnp.broadcast_to(sq, (1, LANES)))

    return _row_call(body, "loss_head", t, tr, [(h, d, 0), (ffn, d, 0), (target, d, 0)], [g_po],
                     [(d, F32), (d, BF16)], [((1, d), F32), ((1, LANES), F32)])


def _qkv_bwd(proj, d_qn, d_kvn, d_kr, g_q, g_kv, cos_t, sin_t, lay, tr):
    ql, kl = lay["ql"], lay["kl"]

    def body(q_ref, kv_ref, dqn_ref, dkvn_ref, dkr_ref, cos_ref, sin_ref, gq_ref, gkv_ref,
             dq_ref, dkv_ref, dkt_ref, dgq_ref, dgkv_ref):
        dx, dg = _rms_bwd(q_ref[...], gq_ref[...], dqn_ref[...])
        dq_ref[...] = dx.astype(BF16)
        _accumulate(dgq_ref, _colsum(dg))
        dx, dg = _rms_bwd(kv_ref[...], gkv_ref[...], dkvn_ref[...])
        dkv_ref[...] = dx.astype(BF16)
        _accumulate(dgkv_ref, _colsum(dg))
        dkt_ref[...] = _rope_bwd(dkr_ref[...], cos_ref[...], sin_ref[...]).astype(BF16)

    t = proj.shape[0]
    return _row_call(
        body, "qkv_bwd", t, tr,
        [(proj, ql, lay["q_off"] // ql), (proj, kl, lay["kv_off"] // kl), (d_qn, ql, 0), (d_kvn, kl, 0),
         (d_kr, LANES, 0), (cos_t, LANES, 0), (sin_t, LANES, 0)],
        [g_q, g_kv], [(ql, BF16), (kl, BF16), (LANES, BF16)], [((1, ql), F32), ((1, kl), F32)])


def _prenorm_bwd(x, d_xn, dh, g, tr):
    def body(x_ref, dxn_ref, dh_ref, g_ref, gx_ref, dg_ref):
        dx, dg = _rms_bwd(x_ref[...], g_ref[...], dxn_ref[...])
        gx_ref[...] = dh_ref[...] + dx
        _accumulate(dg_ref, _colsum(dg))

    t, d = x.shape
    return _row_call(body, "prenorm_bwd", t, tr, [(x, d, 0), (d_xn, d, 0), (dh, d, 0)], [g],
                     [(d, F32)], [((1, d), F32)])


def _adam_rows(rows, cols):
    cap = max(8, (256 * 1024) // cols // 8 * 8)
    tr = min(rows, cap)
    while rows % tr:
        tr -= 8
    return tr


def _adamw(w, g, m, v, name):
    rows, cols = w.shape
    tr = _adam_rows(rows, cols)

    def body(w_ref, g_ref, m_ref, v_ref, d_ref, mo_ref, vo_ref):
        g = g_ref[...]
        m2 = ADAM_B1 * m_ref[...] + (1.0 - ADAM_B1) * g
        v2 = ADAM_B2 * v_ref[...] + (1.0 - ADAM_B2) * (g * g)
        m_hat = m2 / (1.0 - ADAM_B1 ** ADAM_STEP)
        v_hat = v2 / (1.0 - ADAM_B2 ** ADAM_STEP)
        d_ref[...] = -ADAM_LR * (m_hat / (jnp.sqrt(v_hat) + ADAM_EPS) + ADAM_WD * w_ref[...])
        mo_ref[...] = m2
        vo_ref[...] = v2

    spec = pl.BlockSpec((tr, cols), lambda i: (i, 0))
    shape = jax.ShapeDtypeStruct((rows, cols), F32)
    return pl.pallas_call(body, name=name, grid=(rows // tr,), in_specs=[spec] * 4, out_specs=[spec] * 3,
                          out_shape=[shape] * 3, compiler_params=_params(("parallel",)))(w, g, m, v)


def _pair_add(parts, theirs, name):
    _, n, r, c = parts.shape
    tr = _adam_rows(r, c)
    my_c = jnp.reshape(lax.axis_index("c"), (1,)).astype(jnp.int32)

    def body(c_ref, a_ref, b_ref, o_ref):
        o_ref[0] = (a_ref[0, 0].astype(F32) + b_ref[0].astype(F32)).astype(BF16)

    spec = pl.BlockSpec((1, tr, c), lambda k, i, c_ref: (k, i, 0))
    grid_spec = pltpu.PrefetchScalarGridSpec(
        num_scalar_prefetch=1, grid=(n, r // tr),
        in_specs=[pl.BlockSpec((1, 1, tr, c), lambda k, i, c_ref: (c_ref[0], k, i, 0)), spec], out_specs=spec)
    return pl.pallas_call(body, name=name, grid_spec=grid_spec, out_shape=jax.ShapeDtypeStruct((n, r, c), BF16),
                          compiler_params=_params(("parallel", "parallel")))(my_c, parts, theirs)


def _chip_sum(pair_sums, received, name):
    _, r, c = pair_sums.shape
    tr = _adam_rows(r, c)
    own = 2 * lax.axis_index("x") + lax.axis_index("y")

    def body(own_ref, p_ref, r0_ref, r1_ref, r2_ref, o_ref):
        acc = p_ref[0].astype(F32) + r0_ref[0].astype(F32)
        acc = acc + r1_ref[0].astype(F32)
        o_ref[...] = acc + r2_ref[0].astype(F32)

    def rspec(j):
        return pl.BlockSpec((1, tr, c), functools.partial(lambda i, own_ref, j: (j, i, 0), j=j))

    grid_spec = pltpu.PrefetchScalarGridSpec(
        num_scalar_prefetch=1, grid=(r // tr,),
        in_specs=[pl.BlockSpec((1, tr, c), lambda i, own_ref: (own_ref[0], i, 0)), rspec(0), rspec(1), rspec(2)],
        out_specs=pl.BlockSpec((tr, c), lambda i, own_ref: (i, 0)))
    return pl.pallas_call(body, name=name, grid_spec=grid_spec, out_shape=jax.ShapeDtypeStruct((r, c), F32),
                          compiler_params=_params(("parallel",)))(
        jnp.reshape(own, (1,)).astype(jnp.int32), pair_sums, received, received, received)


def _mesh_place():
    x, y, c = lax.axis_index("x"), lax.axis_index("y"), lax.axis_index("c")
    other_chips = [(1 - x, y), (x, 1 - y), (1 - x, 1 - y)]
    return x, y, c, other_chips


def _hbm_specs(n):
    return [pl.BlockSpec(memory_space=pltpu.HBM)] * n


def _all_gather(shards):
    n = len(shards)

    def body(*refs):
        ins, outs = refs[:n], refs[n:2 * n]
        send_sems, recv_sems, local_sems = refs[2 * n:]
        x, y, c, other_chips = _mesh_place()
        own = 2 * x + y
        sibling = (x, y, 1 - c)

        def remote(src, dst, k, to):
            return pltpu.make_async_remote_copy(src_ref=src, dst_ref=dst, send_sem=send_sems.at[k],
                                                recv_sem=recv_sems.at[k], device_id=to, device_id_type=MESH)

        local = [pltpu.make_async_copy(ins[w], outs[w].at[own], local_sems.at[w]) for w in range(n)]
        for cp in local:
            cp.start()
        first = [remote(ins[w].at[c], outs[w].at[own, c], 6 * w + j, (*chip, c))
                 for w in range(n) for j, chip in enumerate(other_chips)]
        for cp in first:
            cp.start()
        passed = []
        for w in range(n):
            for j, chip in enumerate(other_chips):
                rows = outs[w].at[2 * chip[0] + chip[1], c]
                remote(rows, rows, 6 * w + j, (*chip, c)).wait_recv()
                fwd = remote(rows, rows, 6 * w + 3 + j, sibling)
                fwd.start()
                passed.append(fwd)
        for w in range(n):
            for j, chip in enumerate(other_chips):
                rows = outs[w].at[2 * chip[0] + chip[1], 1 - c]
                remote(rows, rows, 6 * w + 3 + j, sibling).wait_recv()
        for cp in first + passed:
            cp.wait_send()
        for cp in local:
            cp.wait()

    return pl.pallas_call(
        body, name="weights_all_gather",
        out_shape=[jax.ShapeDtypeStruct((4,) + s.shape, s.dtype) for s in shards],
        in_specs=_hbm_specs(n), out_specs=_hbm_specs(n),
        scratch_shapes=[pltpu.SemaphoreType.DMA((6 * n,)), pltpu.SemaphoreType.DMA((6 * n,)),
                        pltpu.SemaphoreType.DMA((n,))],
    )(*shards)


def _sibling_exchange(parts):
    n = len(parts)

    def body(*refs):
        ins, outs = refs[:n], refs[n:2 * n]
        send_sems, recv_sems = refs[2 * n:]
        x, y, c, _ = _mesh_place()
        copies = [pltpu.make_async_remote_copy(src_ref=ins[w].at[1 - c], dst_ref=outs[w], send_sem=send_sems.at[w],
                                               recv_sem=recv_sems.at[w], device_id=(x, y, 1 - c), device_id_type=MESH)
                  for w in range(n)]
        for cp in copies:
            cp.start()
        for cp in copies:
            cp.wait()

    return pl.pallas_call(
        body, name="grads_sibling_exchange",
        out_shape=[jax.ShapeDtypeStruct(p.shape[1:], p.dtype) for p in parts],
        in_specs=_hbm_specs(n), out_specs=_hbm_specs(n),
        scratch_shapes=[pltpu.SemaphoreType.DMA((n,)), pltpu.SemaphoreType.DMA((n,))],
    )(*parts)


def _chip_scatter(pair_sums):
    n = len(pair_sums)

    def body(*refs):
        ins, outs = refs[:n], refs[n:2 * n]
        send_sems, recv_sems = refs[2 * n:]
        x, y, c, other_chips = _mesh_place()
        copies = [pltpu.make_async_remote_copy(
            src_ref=ins[w].at[2 * chip[0] + chip[1]], dst_ref=outs[w].at[j], send_sem=send_sems.at[3 * w + j],
            recv_sem=recv_sems.at[3 * w + j], device_id=(*chip, c), device_id_type=MESH)
            for w in range(n) for j, chip in enumerate(other_chips)]
        for cp in copies:
            cp.start()
        for cp in copies:
            cp.wait()

    return pl.pallas_call(
        body, name="grads_chip_scatter",
        out_shape=[jax.ShapeDtypeStruct((3,) + p.shape[1:], p.dtype) for p in pair_sums],
        in_specs=_hbm_specs(n), out_specs=_hbm_specs(n),
        scratch_shapes=[pltpu.SemaphoreType.DMA((3 * n,)), pltpu.SemaphoreType.DMA((3 * n,))],
    )(*pair_sums)


def _halves_exchange(halves):
    n = len(halves)

    def body(*refs):
        ins, outs = refs[:n], refs[n:2 * n]
        send_sems, recv_sems, local_sems = refs[2 * n:]
        x, y, c, _ = _mesh_place()
        local = [pltpu.make_async_copy(ins[w], outs[w].at[c], local_sems.at[w]) for w in range(n)]
        for cp in local:
            cp.start()
        copies = [pltpu.make_async_remote_copy(src_ref=ins[w], dst_ref=outs[w].at[c], send_sem=send_sems.at[w],
                                               recv_sem=recv_sems.at[w], device_id=(x, y, 1 - c), device_id_type=MESH)
                  for w in range(n)]
        for cp in copies:
            cp.start()
        for w in range(n):
            pltpu.make_async_remote_copy(src_ref=ins[w], dst_ref=outs[w].at[1 - c], send_sem=send_sems.at[w],
                                         recv_sem=recv_sems.at[w], device_id=(x, y, 1 - c),
                                         device_id_type=MESH).wait_recv()
        for cp in copies:
            cp.wait_send()
        for cp in local:
            cp.wait()

    return pl.pallas_call(
        body, name="grads_halves_exchange",
        out_shape=[jax.ShapeDtypeStruct((2,) + h.shape, h.dtype) for h in halves],
        in_specs=_hbm_specs(n), out_specs=_hbm_specs(n),
        scratch_shapes=[pltpu.SemaphoreType.DMA((n,)), pltpu.SemaphoreType.DMA((n,)), pltpu.SemaphoreType.DMA((n,))],
    )(*halves)


def _small_all_reduce(packed):
    rows = packed.shape[0]

    def body(in_ref, out_ref, gathered, send_sems, recv_sems):
        x, y, c, _ = _mesh_place()
        me = 4 * x + 2 * y + c
        gathered[0] = in_ref[...]
        copies = []
        for rel in range(1, 8):
            to = (x ^ (rel >> 2), y ^ ((rel >> 1) & 1), c ^ (rel & 1))
            cp = pltpu.make_async_remote_copy(src_ref=in_ref, dst_ref=gathered.at[rel], send_sem=send_sems.at[rel - 1],
                                              recv_sem=recv_sems.at[rel - 1], device_id=to, device_id_type=MESH)
            cp.start()
            copies.append(cp)
        for cp in copies:
            cp.wait()
        acc = gathered[me]
        for dev in range(1, 8):
            acc = acc + gathered[dev ^ me]
        out_ref[...] = acc

    return pl.pallas_call(
        body, name="small_all_reduce", out_shape=jax.ShapeDtypeStruct(packed.shape, F32),
        in_specs=[pl.BlockSpec(memory_space=pltpu.VMEM)], out_specs=pl.BlockSpec(memory_space=pltpu.VMEM),
        scratch_shapes=[pltpu.VMEM((8, rows, LANES), F32), pltpu.SemaphoreType.DMA((7,)), pltpu.SemaphoreType.DMA((7,))],
        compiler_params=pltpu.CompilerParams(vmem_limit_bytes=VMEM_LIMIT_BYTES),
    )(packed)


def _layout(w_in, w_uq, w_ukv, v_ln_gain, q_norm, kv_norm):
    heads = 4 * w_uq.shape[-1] // (NOPE_DIM + ROPE_DIM)
    gw = v_ln_gain.shape[-1]
    ql, kl = q_norm.shape[-1], kv_norm.shape[-1]
    lay = dict(heads=heads, gw=gw, ql=ql, kl=kl, aw=heads * V_DIM, u_off=0, v_off=gw, q_off=2 * gw,
               kv_off=2 * gw + ql, kr_off=2 * gw + ql + kl)
    lay["in_pad"] = _round_up(lay["kr_off"] + LANES, 2 * LANES if lay["kr_off"] + LANES <= 2048 else 1024)
    assert lay["q_off"] % ql == 0 and lay["kv_off"] % kl == 0 and lay["aw"] % gw == 0
    assert 4 * w_in.shape[-1] == ql + kl + ROPE_DIM + 2 * gw
    return lay


def _rope_tile(t1, t2):
    z = jnp.zeros_like(t1)
    return jnp.concatenate([t1, z, t2, z], axis=-1)


def _w_in_padded(w, lay):
    ql, kl, gw = lay["ql"], lay["kl"], lay["gw"]
    q_c, kv_c = w[:, :ql], w[:, ql:ql + kl]
    r = w[:, ql + kl:ql + kl + ROPE_DIM]
    u = w[:, ql + kl + ROPE_DIM:ql + kl + ROPE_DIM + gw]
    v = w[:, ql + kl + ROPE_DIM + gw:]
    parts = [u, v, q_c, kv_c, _rope_tile(r[:, :ROPE_HALF], r[:, ROPE_HALF:])]
    pad = lay["in_pad"] - (lay["kr_off"] + LANES)
    if pad:
        parts.append(jnp.zeros((w.shape[0], pad), w.dtype))
    return jnp.concatenate(parts, axis=1)


def _w_in_grad_unpadded(dw, lay):
    ql, kl, gw = lay["ql"], lay["kl"], lay["gw"]
    ko = lay["kr_off"]
    return jnp.concatenate([dw[:, lay["q_off"]:lay["q_off"] + ql], dw[:, lay["kv_off"]:lay["kv_off"] + kl],
                            dw[:, ko:ko + ROPE_HALF], dw[:, ko + 2 * ROPE_HALF:ko + 3 * ROPE_HALF],
                            dw[:, :gw], dw[:, gw:2 * gw]], axis=1)


def _w_uq_padded(w, heads):
    w3 = w.reshape(w.shape[0], heads, NOPE_DIM + ROPE_DIM)
    t = _rope_tile(w3[..., NOPE_DIM:NOPE_DIM + ROPE_HALF], w3[..., NOPE_DIM + ROPE_HALF:])
    return jnp.concatenate([w3[..., :NOPE_DIM], t], axis=-1).reshape(w.shape[0], heads * HEAD_PAD)


def _w_uq_grad_unpadded(dw, heads):
    d3 = dw.reshape(dw.shape[0], heads, HEAD_PAD)
    return jnp.concatenate([d3[..., :NOPE_DIM], d3[..., NOPE_DIM:NOPE_DIM + ROPE_HALF],
                            d3[..., NOPE_DIM + 2 * ROPE_HALF:NOPE_DIM + 3 * ROPE_HALF]],
                           axis=-1).reshape(dw.shape[0], heads * (NOPE_DIM + ROPE_DIM))


def _cols_gathered(g):
    return jnp.transpose(g, (1, 0, 2)).reshape(g.shape[1], 4 * g.shape[2])


def _chunks_of_cols(grad):
    r, c4 = grad.shape
    return jnp.transpose(grad.reshape(2, r // 2, 4, c4 // 4), (0, 2, 1, 3)).astype(BF16)


def _chunks_of_rows(grad):
    r4, c = grad.shape
    return jnp.transpose(grad.reshape(4, 2, r4 // 8, c), (1, 0, 2, 3)).astype(BF16)


SMALL = ["pre_mix_norm", "q_norm", "kv_norm", "v_ln_gain", "v_ln_bias", "w_spatial", "b_spatial", "attn_out_norm",
         "gmlp_out_norm", "post_mix_norm", "pre_ffn_norm", "post_ffn_norm"]
BIG = ["w_in", "w_uq", "w_ukv", "w_out", "w_gate", "w_up", "w_down"]
ROW_SHARDED = ("w_out", "w_down")
ORDER = ["pre_mix_norm", "w_in", "q_norm", "kv_norm", "w_uq", "w_ukv", "v_ln_gain", "v_ln_bias", "w_spatial",
         "b_spatial", "attn_out_norm", "gmlp_out_norm", "w_out", "post_mix_norm", "pre_ffn_norm", "w_gate", "w_up",
         "w_down", "post_ffn_norm"]


def _pack(arrays):
    flat = jnp.concatenate([a.reshape(-1) for a in arrays])
    n = flat.shape[0]
    total = _round_up(n, 8 * LANES)
    if total > n:
        flat = jnp.concatenate([flat, jnp.zeros((total - n,), F32)])
    return flat.reshape(total // LANES, LANES)


def _unpack(packed, like):
    flat = packed.reshape(-1)
    out, off = [], 0
    for a in like:
        out.append(flat[off:off + a.size].reshape(a.shape))
        off += a.size
    return out


def kernel(x, positions, pre_mix_norm, w_in, q_norm, kv_norm, w_uq, w_ukv, v_ln_gain, v_ln_bias, w_spatial, b_spatial, attn_out_norm, gmlp_out_norm, w_out, post_mix_norm, pre_ffn_norm, w_gate, w_up, w_down, post_ffn_norm, loss_target, m_pre_mix_norm, m_w_in, m_q_norm, m_kv_norm, m_w_uq, m_w_ukv, m_v_ln_gain, m_v_ln_bias, m_w_spatial, m_b_spatial, m_attn_out_norm, m_gmlp_out_norm, m_w_out, m_post_mix_norm, m_pre_ffn_norm, m_w_gate, m_w_up, m_w_down, m_post_ffn_norm, v_pre_mix_norm, v_w_in, v_q_norm, v_kv_norm, v_w_uq, v_w_ukv, v_v_ln_gain, v_v_ln_bias, v_w_spatial, v_b_spatial, v_attn_out_norm, v_gmlp_out_norm, v_w_out, v_post_mix_norm, v_pre_ffn_norm, v_w_gate, v_w_up, v_w_down, v_post_ffn_norm):
    args = dict(locals())
    weights = {n: args[n] for n in ORDER}
    m_in = {n: args["m_" + n] for n in ORDER}
    v_in = {n: args["v_" + n] for n in ORDER}

    lay = _layout(w_in, w_uq, w_ukv, v_ln_gain, q_norm, kv_norm)
    heads, gw = lay["heads"], lay["gw"]
    t, d = x.shape[1], x.shape[2]
    tr = 128 if t % 128 == 0 else t
    xs = x.reshape(t, d)
    target = loss_target.reshape(t, d)

    shards = {n: weights[n][0].astype(BF16) for n in BIG}
    gathered = _all_gather([s.reshape(2, s.shape[0] // 2, s.shape[1]) for s in shards.values()])
    full = {}
    for n, g in zip(BIG, gathered):
        g = g.reshape(4, shards[n].shape[0], shards[n].shape[1])
        full[n] = g.reshape(4 * g.shape[1], g.shape[2]) if n in ROW_SHARDED else _cols_gathered(g)
    dff = full["w_gate"].shape[1]
    dff_pad = _round_up(dff, MATMUL_TILE if dff > MATMUL_TILE else LANES)
    wb_in = _w_in_padded(full["w_in"], lay)
    wb_uq = _w_uq_padded(full["w_uq"], heads)
    wb_ukv = full["w_ukv"]
    wb_out = full["w_out"]
    wb_gate = jnp.pad(full["w_gate"], ((0, 0), (0, dff_pad - dff)))
    wb_up = jnp.pad(full["w_up"], ((0, 0), (0, dff_pad - dff)))
    wb_down = jnp.pad(full["w_down"], ((0, dff_pad - dff), (0, 0)))

    inv_freq = 1.0 / (ROPE_THETA ** (jnp.arange(0, ROPE_DIM, 2, dtype=F32) / ROPE_DIM))
    ang = positions.reshape(t).astype(F32)[:, None] * inv_freq
    cos, sin = jnp.cos(ang), jnp.sin(ang)
    cos_t = _rope_tile(cos, cos)
    sin_t = _rope_tile(-sin, sin)

    row = lambda a: a.reshape(1, -1)
    g_pre, g_q, g_kv = row(pre_mix_norm), row(q_norm), row(kv_norm)
    g_a, g_g, g_pm = row(attn_out_norm), row(gmlp_out_norm), row(post_mix_norm)
    g_pf, g_po = row(pre_ffn_norm), row(post_ffn_norm)
    ln_g, ln_b = row(v_ln_gain), row(v_ln_bias)
    ws = w_spatial[0].astype(BF16)
    ws_t = jnp.transpose(ws, (0, 2, 1))
    bs_wide = jnp.broadcast_to(b_spatial[0][:, :, None], b_spatial.shape[1:] + (G_HEAD_DIM,))

    xn = _prenorm(xs, g_pre, tr)
    proj = _matmul(xn, wb_in, NN, F32, "proj")
    qn, kvn, kr = _qkv_prep(proj, g_q, g_kv, cos_t, sin_t, lay, tr)
    q = _q_rope(_matmul(qn, wb_uq, NN, F32, "q_up"), cos_t, sin_t, heads, tr)
    kv = _matmul(kvn, wb_ukv, NN, BF16, "kv_up")
    a_out = _attn_fwd(q, kv, kr, heads)
    gn = _gmlp_fwd(proj, ln_g, ln_b, ws, bs_wide, g_g, lay)
    mixed = _mix_norm(a_out, gn, g_a, tr)
    mix_out = _matmul(mixed, wb_out, NN, F32, "mix_out")
    h, hn = _post_mix(xs, mix_out, g_pm, g_pf, tr)
    gate = _matmul(hn, wb_gate, NN, BF16, "ffn_gate")
    up = _matmul(hn, wb_up, NN, BF16, "ffn_up")
    act = _swiglu(gate, up)
    ffn = _matmul(act, wb_down, NN, F32, "ffn_down")
    dy, d_ffn, dg_po, loss_vec = _loss_head(h, ffn, target, g_po, tr)

    d_act = _matmul(d_ffn, wb_down, NT, BF16, "d_act")
    gw_down = _matmul(act, d_ffn, TN, F32, "gw_down")
    d_gate, d_up = _swiglu_bwd(gate, up, d_act)
    d_hn = _matmul(d_up, wb_up, NT, F32, "d_hn", addend=_matmul(d_gate, wb_gate, NT, F32, "d_hn_gate"))
    gw_gate = _matmul(hn, d_gate, TN, F32, "gw_gate")
    gw_up = _matmul(hn, d_up, TN, F32, "gw_up")
    dh, d_mo, dg_pf, dg_pm = _post_mix_bwd(mix_out, h, dy, d_hn, g_pm, g_pf, tr)
    d_mixed = _matmul(d_mo, wb_out, NT, F32, "d_mixed")
    gw_out = _matmul(mixed, d_mo, TN, F32, "gw_out")
    d_a, dg_a = _mix_norm_bwd(a_out, d_mixed, g_a, tr)
    d_u, d_v, dg_g, d_ln_g, d_ln_b, d_ws, d_bs_wide = _gmlp_bwd(proj, d_mixed, ln_g, ln_b, ws, ws_t, bs_wide, g_g, lay)
    d_bs = _spatial_bias_grad(d_bs_wide)
    d_q, d_kv, d_kr = _attn_bwd(q, kv, kr, d_a, cos_t, sin_t, heads)
    d_qn = _matmul(d_q, wb_uq, NT, F32, "d_qn")
    gw_uq = _matmul(qn, d_q, TN, F32, "gw_uq")
    d_kvn = _matmul(d_kv, wb_ukv, NT, F32, "d_kvn")
    gw_ukv = _matmul(kvn, d_kv, TN, F32, "gw_ukv")
    d_qc, d_kvc, d_krt, dg_q, dg_kv = _qkv_bwd(proj, d_qn, d_kvn, d_kr, g_q, g_kv, cos_t, sin_t, lay, tr)
    parts = [d_u, d_v, d_qc, d_kvc, d_krt]
    pad = lay["in_pad"] - (lay["kr_off"] + LANES)
    if pad:
        parts.append(jnp.zeros((t, pad), BF16))
    d_proj = jnp.concatenate(parts, axis=1)
    d_xn = _matmul(d_proj, wb_in, NT, F32, "d_xn")
    gw_in = _matmul(xn, d_proj, TN, F32, "gw_in")
    grad_x, dg_pre = _prenorm_bwd(xs, d_xn, dh, g_pre, tr)

    partial = {
        "w_in": _chunks_of_cols(_w_in_grad_unpadded(gw_in, lay)),
        "w_uq": _chunks_of_cols(_w_uq_grad_unpadded(gw_uq, heads)),
        "w_ukv": _chunks_of_cols(gw_ukv),
        "w_out": _chunks_of_rows(gw_out),
        "w_gate": _chunks_of_cols(gw_gate[:, :dff]),
        "w_up": _chunks_of_cols(gw_up[:, :dff]),
        "w_down": _chunks_of_rows(gw_down[:dff]),
    }
    from_sibling = _sibling_exchange([partial[n] for n in BIG])
    pair_sums = [_pair_add(partial[n], r, "pair_add_" + n) for n, r in zip(BIG, from_sibling)]
    received = _chip_scatter(pair_sums)
    halves = [_chip_sum(p, r, "chip_sum_" + n) for n, p, r in zip(BIG, pair_sums, received)]
    summed = _halves_exchange(halves)
    grads = {n: s.reshape(2 * s.shape[1], s.shape[2]) for n, s in zip(BIG, summed)}

    small_grads = {"pre_mix_norm": dg_pre, "q_norm": dg_q, "kv_norm": dg_kv, "v_ln_gain": d_ln_g, "v_ln_bias": d_ln_b,
                   "w_spatial": d_ws, "b_spatial": d_bs, "attn_out_norm": dg_a, "gmlp_out_norm": dg_g,
                   "post_mix_norm": dg_pm, "pre_ffn_norm": dg_pf, "post_ffn_norm": dg_po}
    like = [weights[n] for n in SMALL]
    reduced = _small_all_reduce(_pack([small_grads[n] for n in SMALL] + [loss_vec]))
    loss = reduced.reshape(-1)[sum(a.size for a in like)]
    small_g = _pack(_unpack(reduced, like))
    s_delta, s_m, s_v = _adamw(_pack(like), small_g, _pack([m_in[n] for n in SMALL]),
                               _pack([v_in[n] for n in SMALL]), "adamw_small")
    for n, g in zip(SMALL, _unpack(small_g, like)):
        grads[n] = g
    delta = dict(zip(SMALL, _unpack(s_delta, like)))
    new_m = dict(zip(SMALL, _unpack(s_m, like)))
    new_v = dict(zip(SMALL, _unpack(s_v, like)))

    for n in BIG:
        shape = weights[n].shape
        dl, mm, vv = _adamw(weights[n][0], grads[n], m_in[n][0], v_in[n][0], "adamw_" + n)
        delta[n], new_m[n], new_v[n] = dl.reshape(shape), mm.reshape(shape), vv.reshape(shape)
        grads[n] = grads[n].reshape(shape)

    return (loss, grad_x.reshape(x.shape), *[grads[n] for n in ORDER], *[delta[n] for n in ORDER],
            *[new_m[n] for n in ORDER], *[new_v[n] for n in ORDER])
```

```python
import functools
import math

import jax
import jax.numpy as jnp
from jax import lax
from jax.experimental import pallas as pl
from jax.experimental.pallas import tpu as pltpu

F32 = jnp.float32
BF16 = jnp.bfloat16
MESH = pl.DeviceIdType.MESH

NOPE_DIM = 128
ROPE_DIM = 64
ROPE_HALF = ROPE_DIM // 2
V_DIM = 128
HEAD_PAD = 256
G_HEAD_DIM = 128
CHUNK = 128
ROPE_THETA = 10000.0
EPS = 1e-6
ADAM_LR = 0.001
ADAM_B1 = 0.9
ADAM_B2 = 0.999
ADAM_EPS = 1e-08
ADAM_WD = 0.01
ADAM_STEP = 10

LANES = 128
MATMUL_TILE = 1024
WIDE_TILE = 1408
VMEM_LIMIT_BYTES = 48 * 1024 * 1024

NN = (((1,), (0,)), ((), ()))
NT = (((1,), (1,)), ((), ()))
TN = (((0,), (0,)), ((), ()))


def _params(semantics):
    return pltpu.CompilerParams(dimension_semantics=semantics, vmem_limit_bytes=VMEM_LIMIT_BYTES)


def _tile(n, cap=MATMUL_TILE):
    if n <= cap:
        return n
    if cap == MATMUL_TILE and n % WIDE_TILE == 0:
        return WIDE_TILE
    t = cap - cap % LANES
    while n % t:
        t -= LANES
    assert t > 0, n
    return t


def _round_up(n, m):
    return (n + m - 1) // m * m


def _matmul(a, b, dims, out_dtype, name, addend=None, out_chunks=None):
    b_shard_cols = b.shape[2] if b.ndim == 3 else None
    b_shape = (b.shape[1], 4 * b.shape[2]) if b.ndim == 3 else b.shape
    if dims is NN:
        (m, k), (k2, n) = a.shape, b_shape
    elif dims is NT:
        (m, k), (n, k2) = a.shape, b_shape
    else:
        (k, m), (k2, n) = a.shape, b_shape
    assert k == k2, (a.shape, b.shape, name)
    tm, tn, tk = _tile(m), _tile(n), _tile(k)
    if b_shard_cols is not None:
        if dims is NT:
            tk = _tile(b_shard_cols)
        else:
            tn = _tile(b_shard_cols)
    if out_chunks == "cols":
        tm, tn = _tile(m // 2), _tile(n // 4)
    elif out_chunks == "rows":
        tm = _tile(m // 8)
    nk = k // tk

    def body(*refs):
        if addend is None:
            a_ref, b_ref, o_ref, acc_ref = refs
        else:
            a_ref, b_ref, c_ref, o_ref, acc_ref = refs
        kk = pl.program_id(2)

        @pl.when(kk == 0)
        def _():
            acc_ref[...] = jnp.zeros_like(acc_ref)

        acc_ref[...] += lax.dot_general(a_ref[...], b_ref[...], dims, preferred_element_type=F32)

        @pl.when(kk == nk - 1)
        def _():
            r = acc_ref[...]
            if addend is not None:
                r = r + c_ref[...]
            o_ref[...] = r.astype(o_ref.dtype)

    if dims is TN:
        a_spec = pl.BlockSpec((tk, tm), lambda i, j, kk: (kk, i))
    else:
        a_spec = pl.BlockSpec((tm, tk), lambda i, j, kk: (i, kk))
    if b_shard_cols is None:
        if dims is NT:
            b_spec = pl.BlockSpec((tn, tk), lambda i, j, kk: (j, kk))
        else:
            b_spec = pl.BlockSpec((tk, tn), lambda i, j, kk: (kk, j))
    elif dims is NT:
        per = b_shard_cols // tk
        b_spec = pl.BlockSpec((None, tn, tk), lambda i, j, kk: (kk // per, j, kk % per))
    else:
        per = b_shard_cols // tn
        b_spec = pl.BlockSpec((None, tk, tn), lambda i, j, kk: (j // per, kk, j % per))
    if out_chunks is None:
        o_spec = pl.BlockSpec((tm, tn), lambda i, j, kk: (i, j))
        o_shape = (m, n)
    elif out_chunks == "cols":
        pi, pj = m // 2 // tm, n // 4 // tn
        o_spec = pl.BlockSpec((None, None, tm, tn), lambda i, j, kk: (i // pi, j // pj, i % pi, j % pj))
        o_shape = (2, 4, m // 2, n // 4)
    else:
        pi = m // 8 // tm
        o_spec = pl.BlockSpec((None, None, tm, tn), lambda i, j, kk: ((i // pi) % 2, i // (2 * pi), i % pi, j))
        o_shape = (2, 4, m // 8, n)
    assert addend is None or out_chunks is None
    in_specs = [a_spec, b_spec] + ([o_spec] if addend is not None else [])
    args = (a, b) + ((addend,) if addend is not None else ())
    return pl.pallas_call(
        body, name=name, grid=(m // tm, n // tn, nk), in_specs=in_specs, out_specs=o_spec,
        out_shape=jax.ShapeDtypeStruct(o_shape, out_dtype),
        scratch_shapes=[pltpu.VMEM((tm, tn), F32)],
        compiler_params=_params(("parallel", "parallel", "arbitrary")),
    )(*args)


def _row_call(body, name, rows, tr, row_ins, par_ins, row_outs, acc_outs):
    def col(i, cb):
        return (i, cb)

    def whole(i, nd):
        return (0,) * nd

    in_specs = [pl.BlockSpec((tr, w), functools.partial(col, cb=cb)) for (_, w, cb) in row_ins]
    in_specs += [pl.BlockSpec(a.shape, functools.partial(whole, nd=a.ndim)) for a in par_ins]
    out_specs = [pl.BlockSpec((tr, w), lambda i: (i, 0)) for (w, _) in row_outs]
    out_specs += [pl.BlockSpec(s, functools.partial(whole, nd=len(s))) for (s, _) in acc_outs]
    out_shape = [jax.ShapeDtypeStruct((rows, w), dt) for (w, dt) in row_outs]
    out_shape += [jax.ShapeDtypeStruct(s, dt) for (s, dt) in acc_outs]
    return pl.pallas_call(
        body, name=name, grid=(rows // tr,), in_specs=in_specs, out_specs=out_specs, out_shape=out_shape,
        compiler_params=_params(("arbitrary",) if acc_outs else ("parallel",)),
    )(*[a for (a, _, _) in row_ins], *par_ins)


def _accumulate(ref, val):
    i = pl.program_id(0)

    @pl.when(i == 0)
    def _():
        ref[...] = val

    @pl.when(i > 0)
    def _():
        ref[...] += val


def _colsum(v):
    return jnp.sum(v, axis=0, keepdims=True)


def _rms_fwd(x, g):
    r = lax.rsqrt(jnp.mean(x * x, axis=-1, keepdims=True) + EPS)
    return x * r * g


def _rms_bwd(x, g, dy):
    r = lax.rsqrt(jnp.mean(x * x, axis=-1, keepdims=True) + EPS)
    xh = x * r
    dxh = dy * g
    dx = r * (dxh - xh * jnp.mean(dxh * xh, axis=-1, keepdims=True))
    return dx, dy * xh


_GELU_C = math.sqrt(2.0 / math.pi)
_GELU_A = 0.044715


def _gelu(x):
    return 0.5 * x * (1.0 + jnp.tanh(_GELU_C * (x + _GELU_A * (x * x * x))))


def _gelu_grad(x):
    t = jnp.tanh(_GELU_C * (x + _GELU_A * (x * x * x)))
    return 0.5 * (1.0 + t) + 0.5 * x * (1.0 - t * t) * (_GELU_C * (1.0 + 3.0 * _GELU_A * (x * x)))


def _sigmoid(x):
    return 1.0 / (1.0 + jnp.exp(-x))


def _rope_fwd(t, cos_t, sin_t):
    return t * cos_t + pltpu.roll(t, 2 * ROPE_HALF, 1) * sin_t


def _rope_bwd(dt, cos_t, sin_t):
    return dt * cos_t - pltpu.roll(dt, 2 * ROPE_HALF, 1) * sin_t


def _prenorm(x, g, tr):
    def body(x_ref, g_ref, o_ref):
        o_ref[...] = _rms_fwd(x_ref[...], g_ref[...]).astype(BF16)

    t, d = x.shape
    return _row_call(body, "prenorm", t, tr, [(x, d, 0)], [g], [(d, BF16)], [])[0]


def _qkv_prep(proj, g_q, g_kv, cos_t, sin_t, lay, tr):
    ql, kl = lay["ql"], lay["kl"]

    def body(q_ref, kv_ref, kr_ref, cos_ref, sin_ref, gq_ref, gkv_ref, qn_ref, kvn_ref, kro_ref):
        qn_ref[...] = _rms_fwd(q_ref[...], gq_ref[...]).astype(BF16)
        kvn_ref[...] = _rms_fwd(kv_ref[...], gkv_ref[...]).astype(BF16)
        kro_ref[...] = _rope_fwd(kr_ref[...], cos_ref[...], sin_ref[...]).astype(BF16)

    t = proj.shape[0]
    return _row_call(
        body, "qkv_prep", t, tr,
        [(proj, ql, lay["q_off"] // ql), (proj, kl, lay["kv_off"] // kl), (proj, LANES, lay["kr_off"] // LANES),
         (cos_t, LANES, 0), (sin_t, LANES, 0)],
        [g_q, g_kv], [(ql, BF16), (kl, BF16), (LANES, BF16)], [])


def _q_rope(q, cos_t, sin_t, heads, tr):
    def body(q_ref, cos_ref, sin_ref, o_ref):
        c, s = cos_ref[...], sin_ref[...]
        for h in range(heads):
            lo = h * HEAD_PAD
            o_ref[:, lo:lo + NOPE_DIM] = q_ref[:, lo:lo + NOPE_DIM].astype(BF16)
            o_ref[:, lo + NOPE_DIM:lo + HEAD_PAD] = _rope_fwd(q_ref[:, lo + NOPE_DIM:lo + HEAD_PAD], c, s).astype(BF16)

    t, w = q.shape
    return _row_call(body, "q_rope", t, tr, [(q, w, 0), (cos_t, LANES, 0), (sin_t, LANES, 0)], [], [(w, BF16)], [])[0]


def _softmax_rows(s):
    m = jnp.max(s, axis=-1, keepdims=True)
    p = jnp.exp(s - m)
    return p * (1.0 / jnp.sum(p, axis=-1, keepdims=True))


def _attn_tile(t):
    return 256 if t % 256 == 0 else 128


def _attn_fwd(q, kv, kr, heads):
    t = q.shape[0]
    tq = _attn_tile(t)
    scale = 1.0 / math.sqrt(NOPE_DIM + ROPE_DIM)

    def body(q_ref, kv_ref, kr_ref, o_ref, kcat):
        @pl.when(pl.program_id(1) == 0)
        def _():
            kcat[:, :NOPE_DIM] = kv_ref[:, :NOPE_DIM]
            kcat[:, NOPE_DIM:] = kr_ref[...]

        s = lax.dot_general(q_ref[...], kcat[...], NT, preferred_element_type=F32) * scale
        p = _softmax_rows(s)
        o_ref[...] = jnp.dot(p.astype(BF16), kv_ref[:, NOPE_DIM:], preferred_element_type=F32)

    return pl.pallas_call(
        body, name="attn_fwd", grid=(heads, t // tq),
        in_specs=[pl.BlockSpec((tq, HEAD_PAD), lambda h, i: (i, h)),
                  pl.BlockSpec((t, HEAD_PAD), lambda h, i: (0, h)),
                  pl.BlockSpec((t, LANES), lambda h, i: (0, 0))],
        out_specs=pl.BlockSpec((tq, V_DIM), lambda h, i: (i, h)),
        out_shape=jax.ShapeDtypeStruct((t, heads * V_DIM), F32),
        scratch_shapes=[pltpu.VMEM((t, HEAD_PAD), BF16)],
        compiler_params=_params(("arbitrary", "arbitrary")),
    )(q, kv, kr)


def _attn_bwd(q, kv, kr, d_out, cos_t, sin_t, heads):
    t = q.shape[0]
    tq = _attn_tile(t)
    nq = t // tq
    scale = 1.0 / math.sqrt(NOPE_DIM + ROPE_DIM)

    def body(q_ref, kv_ref, kr_ref, do_ref, cos_ref, sin_ref, dq_ref, dkv_ref, dkr_ref, kcat, dk_acc, dv_acc):
        h, i = pl.program_id(0), pl.program_id(1)

        @pl.when(i == 0)
        def _():
            kcat[:, :NOPE_DIM] = kv_ref[:, :NOPE_DIM]
            kcat[:, NOPE_DIM:] = kr_ref[...]
            dk_acc[...] = jnp.zeros_like(dk_acc)
            dv_acc[...] = jnp.zeros_like(dv_acc)

        @pl.when((h == 0) & (i == 0))
        def _():
            dkr_ref[...] = jnp.zeros_like(dkr_ref)

        qb, dob = q_ref[...], do_ref[...]
        s = lax.dot_general(qb, kcat[...], NT, preferred_element_type=F32) * scale
        p = _softmax_rows(s)
        dv_acc[...] += lax.dot_general(p.astype(BF16), dob, TN, preferred_element_type=F32)
        dp = lax.dot_general(dob, kv_ref[:, NOPE_DIM:], NT, preferred_element_type=F32)
        ds = (p * (dp - jnp.sum(dp * p, axis=-1, keepdims=True)) * scale).astype(BF16)
        dq = jnp.dot(ds, kcat[...], preferred_element_type=F32)
        dq_ref[:, :NOPE_DIM] = dq[:, :NOPE_DIM].astype(BF16)
        dq_ref[:, NOPE_DIM:] = _rope_bwd(dq[:, NOPE_DIM:], cos_ref[...], sin_ref[...]).astype(BF16)
        dk_acc[...] += lax.dot_general(ds, qb, TN, preferred_element_type=F32)

        @pl.when(i == nq - 1)
        def _():
            dkv_ref[:, :NOPE_DIM] = dk_acc[:, :NOPE_DIM].astype(BF16)
            dkv_ref[:, NOPE_DIM:] = dv_acc[...].astype(BF16)
            dkr_ref[...] += dk_acc[:, NOPE_DIM:]

    return pl.pallas_call(
        body, name="attn_bwd", grid=(heads, nq),
        in_specs=[pl.BlockSpec((tq, HEAD_PAD), lambda h, i: (i, h)),
                  pl.BlockSpec((t, HEAD_PAD), lambda h, i: (0, h)),
                  pl.BlockSpec((t, LANES), lambda h, i: (0, 0)),
                  pl.BlockSpec((tq, V_DIM), lambda h, i: (i, h)),
                  pl.BlockSpec((tq, LANES), lambda h, i: (i, 0)),
                  pl.BlockSpec((tq, LANES), lambda h, i: (i, 0))],
        out_specs=[pl.BlockSpec((tq, HEAD_PAD), lambda h, i: (i, h)),
                   pl.BlockSpec((t, HEAD_PAD), lambda h, i: (0, h)),
                   pl.BlockSpec((t, LANES), lambda h, i: (0, 0))],
        out_shape=[jax.ShapeDtypeStruct((t, heads * HEAD_PAD), BF16),
                   jax.ShapeDtypeStruct((t, heads * HEAD_PAD), BF16),
                   jax.ShapeDtypeStruct((t, LANES), F32)],
        scratch_shapes=[pltpu.VMEM((t, HEAD_PAD), BF16), pltpu.VMEM((t, HEAD_PAD), F32), pltpu.VMEM((t, V_DIM), F32)],
        compiler_params=_params(("arbitrary", "arbitrary")),
    )(q, kv, kr, d_out, cos_t, sin_t)


def _layer_norm_parts(x):
    mu = jnp.mean(x, axis=-1, keepdims=True)
    xc = x - mu
    r = lax.rsqrt(jnp.mean(xc * xc, axis=-1, keepdims=True) + EPS)
    return xc * r, r


def _gmlp_fwd(proj, ln_g, ln_b, w_s, b_sb, g_out_norm, lay):
    gw = lay["gw"]
    g_heads = gw // G_HEAD_DIM

    def body(u_ref, v_ref, lng_ref, lnb_ref, ws_ref, bs_ref, gn_ref, o_ref, gate_ref):
        gu = _gelu(u_ref[...])
        vh, _ = _layer_norm_parts(_gelu(v_ref[...]))
        vln = (vh * lng_ref[...] + lnb_ref[...]).astype(BF16)
        for g in range(g_heads):
            cols = slice(g * G_HEAD_DIM, (g + 1) * G_HEAD_DIM)
            s = jnp.dot(ws_ref[g], vln[:, cols], preferred_element_type=F32) + bs_ref[g]
            gate_ref[:, cols] = gu[:, cols] * s
        o_ref[...] = _rms_fwd(gate_ref[...], gn_ref[...]).astype(BF16)

    t = proj.shape[0]
    in_specs = [pl.BlockSpec((CHUNK, gw), lambda i: (i, 0)), pl.BlockSpec((CHUNK, gw), lambda i: (i, 1))]
    pars = [ln_g, ln_b, w_s, b_sb, g_out_norm]
    in_specs += [pl.BlockSpec(a.shape, functools.partial(lambda i, nd: (0,) * nd, nd=a.ndim)) for a in pars]
    return pl.pallas_call(
        body, name="gmlp_fwd", grid=(t // CHUNK,), in_specs=in_specs,
        out_specs=pl.BlockSpec((CHUNK, gw), lambda i: (i, 0)),
        out_shape=jax.ShapeDtypeStruct((t, gw), BF16),
        scratch_shapes=[pltpu.VMEM((CHUNK, gw), F32)],
        compiler_params=_params(("parallel",)),
    )(proj, proj, *pars)


def _gmlp_bwd(proj, d_mixed, ln_g, ln_b, w_s, w_st, b_sb, g_out_norm, lay):
    gw = lay["gw"]
    g_heads = gw // G_HEAD_DIM
    aw_blocks = lay["aw"] // gw

    def body(u_ref, v_ref, dm_ref, lng_ref, lnb_ref, ws_ref, wst_ref, bs_ref, gn_ref,
             du_ref, dv_ref, dgn_ref, dlng_ref, dlnb_ref, dws_ref, dbs_ref, gate_ref, s_ref, dvln_ref):
        i = pl.program_id(0)
        u, v = u_ref[...], v_ref[...]
        gu, gv = _gelu(u), _gelu(v)
        vh, r_ln = _layer_norm_parts(gv)
        vln = (vh * lng_ref[...] + lnb_ref[...]).astype(BF16)
        for g in range(g_heads):
            cols = slice(g * G_HEAD_DIM, (g + 1) * G_HEAD_DIM)
            s = jnp.dot(ws_ref[g], vln[:, cols], preferred_element_type=F32) + bs_ref[g]
            s_ref[:, cols] = s
            gate_ref[:, cols] = gu[:, cols] * s
        d_gate, dgn = _rms_bwd(gate_ref[...], gn_ref[...], dm_ref[...])
        _accumulate(dgn_ref, _colsum(dgn))
        du_ref[...] = (d_gate * s_ref[...] * _gelu_grad(u)).astype(BF16)
        d_s = d_gate * gu
        d_sb = d_s.astype(BF16)
        for g in range(g_heads):
            cols = slice(g * G_HEAD_DIM, (g + 1) * G_HEAD_DIM)
            dw = lax.dot_general(d_sb[:, cols], vln[:, cols], NT, preferred_element_type=F32)

            @pl.when(i == 0)
            def _():
                dws_ref[g] = dw
                dbs_ref[g] = d_s[:, cols]

            @pl.when(i > 0)
            def _():
                dws_ref[g] += dw
                dbs_ref[g] += d_s[:, cols]

            dvln_ref[:, cols] = jnp.dot(wst_ref[g], d_sb[:, cols], preferred_element_type=F32)
        d_vln = dvln_ref[...]
        _accumulate(dlng_ref, _colsum(d_vln * vh))
        _accumulate(dlnb_ref, _colsum(d_vln))
        d_vh = d_vln * lng_ref[...]
        d_gv = r_ln * (d_vh - jnp.mean(d_vh, axis=-1, keepdims=True)
                       - vh * jnp.mean(d_vh * vh, axis=-1, keepdims=True))
        dv_ref[...] = (d_gv * _gelu_grad(v)).astype(BF16)

    t = proj.shape[0]
    whole = lambda a: pl.BlockSpec(a.shape, functools.partial(lambda i, nd: (0,) * nd, nd=a.ndim))
    pars = [ln_g, ln_b, w_s, w_st, b_sb, g_out_norm]
    hshape = (g_heads, CHUNK, CHUNK)
    return pl.pallas_call(
        body, name="gmlp_bwd", grid=(t // CHUNK,),
        in_specs=[pl.BlockSpec((CHUNK, gw), lambda i: (i, 0)), pl.BlockSpec((CHUNK, gw), lambda i: (i, 1)),
                  pl.BlockSpec((CHUNK, gw), lambda i: (i, aw_blocks))] + [whole(a) for a in pars],
        out_specs=[pl.BlockSpec((CHUNK, gw), lambda i: (i, 0)), pl.BlockSpec((CHUNK, gw), lambda i: (i, 0)),
                   pl.BlockSpec((1, gw), lambda i: (0, 0)), pl.BlockSpec((1, gw), lambda i: (0, 0)),
                   pl.BlockSpec((1, gw), lambda i: (0, 0)),
                   pl.BlockSpec(hshape, lambda i: (0, 0, 0)), pl.BlockSpec(hshape, lambda i: (0, 0, 0))],
        out_shape=[jax.ShapeDtypeStruct((t, gw), BF16), jax.ShapeDtypeStruct((t, gw), BF16),
                   jax.ShapeDtypeStruct((1, gw), F32), jax.ShapeDtypeStruct((1, gw), F32),
                   jax.ShapeDtypeStruct((1, gw), F32),
                   jax.ShapeDtypeStruct(hshape, F32), jax.ShapeDtypeStruct(hshape, F32)],
        scratch_shapes=[pltpu.VMEM((CHUNK, gw), F32), pltpu.VMEM((CHUNK, gw), F32), pltpu.VMEM((CHUNK, gw), F32)],
        compiler_params=_params(("arbitrary",)),
    )(proj, proj, d_mixed, *pars)


def _spatial_bias_grad(dbs_wide):
    g_heads = dbs_wide.shape[0]

    def body(x_ref, o_ref):
        for g in range(g_heads):
            o_ref[g:g + 1, :] = jnp.sum(x_ref[g].T, axis=0, keepdims=True)

    return pl.pallas_call(
        body, name="spatial_bias_grad", out_shape=jax.ShapeDtypeStruct((g_heads, CHUNK), F32),
        in_specs=[pl.BlockSpec(memory_space=pltpu.VMEM)], out_specs=pl.BlockSpec(memory_space=pltpu.VMEM),
    )(dbs_wide)


def _mix_norm(a_out, gn, g_a, tr):
    aw = a_out.shape[1]
    gw = gn.shape[1]

    def body(a_ref, gn_ref, g_ref, o_ref):
        o_ref[:, :aw] = _rms_fwd(a_ref[...], g_ref[...]).astype(BF16)
        o_ref[:, aw:] = gn_ref[...]

    t = a_out.shape[0]
    return _row_call(body, "mix_norm", t, tr, [(a_out, aw, 0), (gn, gw, 0)], [g_a], [(aw + gw, BF16)], [])[0]


def _mix_norm_bwd(a_out, d_mixed, g_a, tr):
    aw = a_out.shape[1]

    def body(a_ref, dm_ref, g_ref, da_ref, dg_ref):
        dx, dg = _rms_bwd(a_ref[...], g_ref[...], dm_ref[...])
        da_ref[...] = dx.astype(BF16)
        _accumulate(dg_ref, _colsum(dg))

    t = a_out.shape[0]
    return _row_call(body, "mix_norm_bwd", t, tr, [(a_out, aw, 0), (d_mixed, aw, 0)], [g_a],
                     [(aw, BF16)], [((1, aw), F32)])


def _post_mix(x, mix_out, g_pm, g_pf, tr):
    def body(x_ref, mo_ref, gpm_ref, gpf_ref, h_ref, hn_ref):
        h = x_ref[...] + _rms_fwd(mo_ref[...], gpm_ref[...])
        h_ref[...] = h
        hn_ref[...] = _rms_fwd(h, gpf_ref[...]).astype(BF16)

    t, d = x.shape
    return _row_call(body, "post_mix", t, tr, [(x, d, 0), (mix_out, d, 0)], [g_pm, g_pf], [(d, F32), (d, BF16)], [])


def _post_mix_bwd(mix_out, h, dy, d_hn, g_pm, g_pf, tr):
    def body(mo_ref, h_ref, dy_ref, dhn_ref, gpm_ref, gpf_ref, dh_ref, dmo_ref, dgpf_ref, dgpm_ref):
        dx, dg = _rms_bwd(h_ref[...], gpf_ref[...], dhn_ref[...])
        dh = dy_ref[...] + dx
        dh_ref[...] = dh
        _accumulate(dgpf_ref, _colsum(dg))
        dmo, dg2 = _rms_bwd(mo_ref[...], gpm_ref[...], dh)
        dmo_ref[...] = dmo.astype(BF16)
        _accumulate(dgpm_ref, _colsum(dg2))

    t, d = h.shape
    return _row_call(body, "post_mix_bwd", t, tr, [(mix_out, d, 0), (h, d, 0), (dy, d, 0), (d_hn, d, 0)],
                     [g_pm, g_pf], [(d, F32), (d, BF16)], [((1, d), F32), ((1, d), F32)])


def _swiglu(gate, up):
    t, f = gate.shape
    tr, tf = _tile(t, 512), _tile(f, 2048)

    def body(g_ref, u_ref, o_ref):
        g = g_ref[...].astype(F32)
        o_ref[...] = (g * _sigmoid(g) * u_ref[...].astype(F32)).astype(BF16)

    spec = pl.BlockSpec((tr, tf), lambda i, j: (i, j))
    return pl.pallas_call(body, name="swiglu", grid=(t // tr, f // tf), in_specs=[spec, spec], out_specs=spec,
                          out_shape=jax.ShapeDtypeStruct((t, f), BF16),
                          compiler_params=_params(("parallel", "parallel")))(gate, up)


def _swiglu_bwd(gate, up, d_act):
    t, f = gate.shape
    tr, tf = _tile(t, 512), _tile(f, 2048)

    def body(g_ref, u_ref, da_ref, dg_ref, du_ref):
        g, u, da = g_ref[...].astype(F32), u_ref[...].astype(F32), da_ref[...].astype(F32)
        sg = _sigmoid(g)
        du_ref[...] = (da * (g * sg)).astype(BF16)
        dg_ref[...] = (da * u * (sg * (1.0 + g * (1.0 - sg)))).astype(BF16)

    spec = pl.BlockSpec((tr, tf), lambda i, j: (i, j))
    shape = jax.ShapeDtypeStruct((t, f), BF16)
    return pl.pallas_call(body, name="swiglu_bwd", grid=(t // tr, f // tf), in_specs=[spec, spec, spec],
                          out_specs=[spec, spec], out_shape=[shape, shape],
                          compiler_params=_params(("parallel", "parallel")))(gate, up, d_act)


def _loss_head(h, ffn, target, g_po, tr):
    t, d = h.shape

    def body(h_ref, f_ref, t_ref, g_ref, dy_ref, df_ref, dg_ref, loss_ref):
        f = f_ref[...]
        err = h_ref[...] + _rms_fwd(f, g_ref[...]) - t_ref[...]
        dy = err * (1.0 / d)
        dy_ref[...] = dy
        df, dg = _rms_bwd(f, g_ref[...], dy)
        df_ref[...] = df.astype(BF16)
        _accumulate(dg_ref, _colsum(dg))
        sq = jnp.sum(_colsum(err * err), axis=-1, keepdims=True) * (0.5 / d)
        _accumulate(loss_ref, jnp.broadcast_to(sq, (1, LANES)))

    return _row_call(body, "loss_head", t, tr, [(h, d, 0), (ffn, d, 0), (target, d, 0)], [g_po],
                     [(d, F32), (d, BF16)], [((1, d), F32), ((1, LANES), F32)])


def _qkv_bwd(proj, d_qn, d_kvn, d_kr, g_q, g_kv, cos_t, sin_t, lay, tr):
    ql, kl = lay["ql"], lay["kl"]

    def body(q_ref, kv_ref, dqn_ref, dkvn_ref, dkr_ref, cos_ref, sin_ref, gq_ref, gkv_ref,
             dq_ref, dkv_ref, dkt_ref, dgq_ref, dgkv_ref):
        dx, dg = _rms_bwd(q_ref[...], gq_ref[...], dqn_ref[...])
        dq_ref[...] = dx.astype(BF16)
        _accumulate(dgq_ref, _colsum(dg))
        dx, dg = _rms_bwd(kv_ref[...], gkv_ref[...], dkvn_ref[...])
        dkv_ref[...] = dx.astype(BF16)
        _accumulate(dgkv_ref, _colsum(dg))
        dkt_ref[...] = _rope_bwd(dkr_ref[...], cos_ref[...], sin_ref[...]).astype(BF16)

    t = proj.shape[0]
    return _row_call(
        body, "qkv_bwd", t, tr,
        [(proj, ql, lay["q_off"] // ql), (proj, kl, lay["kv_off"] // kl), (d_qn, ql, 0), (d_kvn, kl, 0),
         (d_kr, LANES, 0), (cos_t, LANES, 0), (sin_t, LANES, 0)],
        [g_q, g_kv], [(ql, BF16), (kl, BF16), (LANES, BF16)], [((1, ql), F32), ((1, kl), F32)])


def _prenorm_bwd(x, d_xn, dh, g, tr):
    def body(x_ref, dxn_ref, dh_ref, g_ref, gx_ref, dg_ref):
        dx, dg = _rms_bwd(x_ref[...], g_ref[...], dxn_ref[...])
        gx_ref[...] = dh_ref[...] + dx
        _accumulate(dg_ref, _colsum(dg))

    t, d = x.shape
    return _row_call(body, "prenorm_bwd", t, tr, [(x, d, 0), (d_xn, d, 0), (dh, d, 0)], [g],
                     [(d, F32)], [((1, d), F32)])


def _adam_rows(rows, cols):
    cap = max(8, (256 * 1024) // cols // 8 * 8)
    tr = min(rows, cap)
    while rows % tr:
        tr -= 8
    return tr


def _adamw(w, g, m, v, name):
    rows, cols = w.shape
    tr = _adam_rows(rows, cols)

    def body(w_ref, g_ref, m_ref, v_ref, d_ref, mo_ref, vo_ref):
        g = g_ref[...]
        m2 = ADAM_B1 * m_ref[...] + (1.0 - ADAM_B1) * g
        v2 = ADAM_B2 * v_ref[...] + (1.0 - ADAM_B2) * (g * g)
        m_hat = m2 / (1.0 - ADAM_B1 ** ADAM_STEP)
        v_hat = v2 / (1.0 - ADAM_B2 ** ADAM_STEP)
        d_ref[...] = -ADAM_LR * (m_hat / (jnp.sqrt(v_hat) + ADAM_EPS) + ADAM_WD * w_ref[...])
        mo_ref[...] = m2
        vo_ref[...] = v2

    spec = pl.BlockSpec((tr, cols), lambda i: (i, 0))
    shape = jax.ShapeDtypeStruct((rows, cols), F32)
    return pl.pallas_call(body, name=name, grid=(rows // tr,), in_specs=[spec] * 4, out_specs=[spec] * 3,
                          out_shape=[shape] * 3, compiler_params=_params(("parallel",)))(w, g, m, v)


def _adamw_halves(w, g_mine, g_theirs, m, v, name):
    rows, cols = w.shape
    rh = g_mine.shape[0]
    tr = _adam_rows(math.gcd(rows, rh), cols)
    per_half = rh // tr
    my_c = jnp.reshape(lax.axis_index("c"), (1,)).astype(jnp.int32)

    def body(c_ref, w_ref, gm_ref, gt_ref, m_ref, v_ref, g_ref, d_ref, mo_ref, vo_ref):
        mine = (pl.program_id(0) // per_half) == c_ref[0]
        g = jnp.where(mine, gm_ref[...], gt_ref[...])
        m2 = ADAM_B1 * m_ref[...] + (1.0 - ADAM_B1) * g
        v2 = ADAM_B2 * v_ref[...] + (1.0 - ADAM_B2) * (g * g)
        m_hat = m2 / (1.0 - ADAM_B1 ** ADAM_STEP)
        v_hat = v2 / (1.0 - ADAM_B2 ** ADAM_STEP)
        g_ref[...] = g
        d_ref[...] = -ADAM_LR * (m_hat / (jnp.sqrt(v_hat) + ADAM_EPS) + ADAM_WD * w_ref[...])
        mo_ref[...] = m2
        vo_ref[...] = v2

    def half_spec(is_mine):
        def index(i, c_ref):
            used = ((i // per_half) == c_ref[0]) if is_mine else ((i // per_half) != c_ref[0])
            return (jnp.where(used, i % per_half, 0), 0)
        return pl.BlockSpec((tr, cols), index)

    spec = pl.BlockSpec((tr, cols), lambda i, c_ref: (i, 0))
    shape = jax.ShapeDtypeStruct((rows, cols), F32)
    grid_spec = pltpu.PrefetchScalarGridSpec(
        num_scalar_prefetch=1, grid=(rows // tr,),
        in_specs=[spec, half_spec(True), half_spec(False), spec, spec], out_specs=[spec] * 4)
    return pl.pallas_call(body, name=name, grid_spec=grid_spec, out_shape=[shape] * 4,
                          compiler_params=_params(("parallel",)))(my_c, w, g_mine, g_theirs, m, v)


def _place_own(gathered, shard, name):
    _, _, r, c = gathered.shape
    tr = _adam_rows(r, c)
    own = jnp.reshape(2 * lax.axis_index("x") + lax.axis_index("y"), (1,)).astype(jnp.int32)

    def body(own_ref, g_ref, s_ref, o_ref):
        o_ref[...] = s_ref[...]

    grid_spec = pltpu.PrefetchScalarGridSpec(
        num_scalar_prefetch=1, grid=(2, r // tr),
        in_specs=[pl.BlockSpec(memory_space=pl.ANY), pl.BlockSpec((None, tr, c), lambda a, i, own_ref: (a, i, 0))],
        out_specs=pl.BlockSpec((None, None, tr, c), lambda a, i, own_ref: (own_ref[0], a, i, 0)))
    return pl.pallas_call(body, name=name, grid_spec=grid_spec,
                          out_shape=jax.ShapeDtypeStruct(gathered.shape, gathered.dtype),
                          input_output_aliases={1: 0},
                          compiler_params=_params(("parallel", "parallel")))(own, gathered, shard)


def _pair_add(parts, theirs, name):
    _, n, r, c = parts.shape
    tr = _adam_rows(r, c)
    my_c = jnp.reshape(lax.axis_index("c"), (1,)).astype(jnp.int32)

    def body(c_ref, a_ref, b_ref, o_ref):
        o_ref[0] = (a_ref[0, 0].astype(F32) + b_ref[0].astype(F32)).astype(BF16)

    spec = pl.BlockSpec((1, tr, c), lambda k, i, c_ref: (k, i, 0))
    grid_spec = pltpu.PrefetchScalarGridSpec(
        num_scalar_prefetch=1, grid=(n, r // tr),
        in_specs=[pl.BlockSpec((1, 1, tr, c), lambda k, i, c_ref: (c_ref[0], k, i, 0)), spec], out_specs=spec)
    return pl.pallas_call(body, name=name, grid_spec=grid_spec, out_shape=jax.ShapeDtypeStruct((n, r, c), BF16),
                          compiler_params=_params(("parallel", "parallel")))(my_c, parts, theirs)


def _chip_sum(pair_sums, received, name):
    _, r, c = pair_sums.shape
    tr = _adam_rows(r, c)
    own = 2 * lax.axis_index("x") + lax.axis_index("y")

    def body(own_ref, p_ref, r0_ref, r1_ref, r2_ref, o_ref):
        acc = p_ref[0].astype(F32) + r0_ref[0].astype(F32)
        acc = acc + r1_ref[0].astype(F32)
        o_ref[...] = acc + r2_ref[0].astype(F32)

    def rspec(j):
        return pl.BlockSpec((1, tr, c), functools.partial(lambda i, own_ref, j: (j, i, 0), j=j))

    grid_spec = pltpu.PrefetchScalarGridSpec(
        num_scalar_prefetch=1, grid=(r // tr,),
        in_specs=[pl.BlockSpec((1, tr, c), lambda i, own_ref: (own_ref[0], i, 0)), rspec(0), rspec(1), rspec(2)],
        out_specs=pl.BlockSpec((tr, c), lambda i, own_ref: (i, 0)))
    return pl.pallas_call(body, name=name, grid_spec=grid_spec, out_shape=jax.ShapeDtypeStruct((r, c), F32),
                          compiler_params=_params(("parallel",)))(
        jnp.reshape(own, (1,)).astype(jnp.int32), pair_sums, received, received, received)


def _mesh_place():
    x, y, c = lax.axis_index("x"), lax.axis_index("y"), lax.axis_index("c")
    other_chips = [(1 - x, y), (x, 1 - y), (1 - x, 1 - y)]
    return x, y, c, other_chips


def _hbm_specs(n):
    return [pl.BlockSpec(memory_space=pltpu.HBM)] * n


def _all_gather(shards):
    n = len(shards)

    def body(*refs):
        ins, outs = refs[:n], refs[n:2 * n]
        send_sems, recv_sems = refs[2 * n:]
        x, y, c, other_chips = _mesh_place()
        own = 2 * x + y
        sibling = (x, y, 1 - c)

        def remote(src, dst, k, to):
            return pltpu.make_async_remote_copy(src_ref=src, dst_ref=dst, send_sem=send_sems.at[k],
                                                recv_sem=recv_sems.at[k], device_id=to, device_id_type=MESH)

        first = [remote(ins[w].at[c], outs[w].at[own, c], 6 * w + j, (*chip, c))
                 for w in range(n) for j, chip in enumerate(other_chips)]
        for cp in first:
            cp.start()
        passed = []
        for w in range(n):
            for j, chip in enumerate(other_chips):
                rows = outs[w].at[2 * chip[0] + chip[1], c]
                remote(rows, rows, 6 * w + j, (*chip, c)).wait_recv()
                fwd = remote(rows, rows, 6 * w + 3 + j, sibling)
                fwd.start()
                passed.append(fwd)
        for w in range(n):
            for j, chip in enumerate(other_chips):
                rows = outs[w].at[2 * chip[0] + chip[1], 1 - c]
                remote(rows, rows, 6 * w + 3 + j, sibling).wait_recv()
        for cp in first + passed:
            cp.wait_send()

    return pl.pallas_call(
        body, name="weights_all_gather",
        out_shape=[jax.ShapeDtypeStruct((4,) + s.shape, s.dtype) for s in shards],
        in_specs=_hbm_specs(n), out_specs=_hbm_specs(n),
        scratch_shapes=[pltpu.SemaphoreType.DMA((6 * n,)), pltpu.SemaphoreType.DMA((6 * n,))],
    )(*shards)


def _sibling_exchange(parts):
    n = len(parts)

    def body(*refs):
        ins, outs = refs[:n], refs[n:2 * n]
        send_sems, recv_sems = refs[2 * n:]
        x, y, c, _ = _mesh_place()
        copies = [pltpu.make_async_remote_copy(src_ref=ins[w].at[1 - c], dst_ref=outs[w], send_sem=send_sems.at[w],
                                               recv_sem=recv_sems.at[w], device_id=(x, y, 1 - c), device_id_type=MESH)
                  for w in range(n)]
        for cp in copies:
            cp.start()
        for cp in copies:
            cp.wait()

    return pl.pallas_call(
        body, name="grads_sibling_exchange",
        out_shape=[jax.ShapeDtypeStruct(p.shape[1:], p.dtype) for p in parts],
        in_specs=_hbm_specs(n), out_specs=_hbm_specs(n),
        scratch_shapes=[pltpu.SemaphoreType.DMA((n,)), pltpu.SemaphoreType.DMA((n,))],
    )(*parts)


def _chip_scatter(pair_sums):
    n = len(pair_sums)

    def body(*refs):
        ins, outs = refs[:n], refs[n:2 * n]
        send_sems, recv_sems = refs[2 * n:]
        x, y, c, other_chips = _mesh_place()
        copies = [pltpu.make_async_remote_copy(
            src_ref=ins[w].at[2 * chip[0] + chip[1]], dst_ref=outs[w].at[j], send_sem=send_sems.at[3 * w + j],
            recv_sem=recv_sems.at[3 * w + j], device_id=(*chip, c), device_id_type=MESH)
            for w in range(n) for j, chip in enumerate(other_chips)]
        for cp in copies:
            cp.start()
        for cp in copies:
            cp.wait()

    return pl.pallas_call(
        body, name="grads_chip_scatter",
        out_shape=[jax.ShapeDtypeStruct((3,) + p.shape[1:], p.dtype) for p in pair_sums],
        in_specs=_hbm_specs(n), out_specs=_hbm_specs(n),
        scratch_shapes=[pltpu.SemaphoreType.DMA((3 * n,)), pltpu.SemaphoreType.DMA((3 * n,))],
    )(*pair_sums)


def _halves_exchange(halves):
    n = len(halves)

    def body(*refs):
        ins, outs = refs[:n], refs[n:2 * n]
        send_sems, recv_sems = refs[2 * n:]
        x, y, c, _ = _mesh_place()
        copies = [pltpu.make_async_remote_copy(src_ref=ins[w], dst_ref=outs[w], send_sem=send_sems.at[w],
                                               recv_sem=recv_sems.at[w], device_id=(x, y, 1 - c), device_id_type=MESH)
                  for w in range(n)]
        for cp in copies:
            cp.start()
        for cp in copies:
            cp.wait()

    return pl.pallas_call(
        body, name="grads_halves_exchange",
        out_shape=[jax.ShapeDtypeStruct(h.shape, h.dtype) for h in halves],
        in_specs=_hbm_specs(n), out_specs=_hbm_specs(n),
        scratch_shapes=[pltpu.SemaphoreType.DMA((n,)), pltpu.SemaphoreType.DMA((n,))],
    )(*halves)


def _small_all_reduce(packed):
    rows = packed.shape[0]

    def body(in_ref, out_ref, gathered, send_sems, recv_sems):
        x, y, c, _ = _mesh_place()
        me = 4 * x + 2 * y + c
        gathered[0] = in_ref[...]
        copies = []
        for rel in range(1, 8):
            to = (x ^ (rel >> 2), y ^ ((rel >> 1) & 1), c ^ (rel & 1))
            cp = pltpu.make_async_remote_copy(src_ref=in_ref, dst_ref=gathered.at[rel], send_sem=send_sems.at[rel - 1],
                                              recv_sem=recv_sems.at[rel - 1], device_id=to, device_id_type=MESH)
            cp.start()
            copies.append(cp)
        for cp in copies:
            cp.wait()
        acc = gathered[me]
        for dev in range(1, 8):
            acc = acc + gathered[dev ^ me]
        out_ref[...] = acc

    return pl.pallas_call(
        body, name="small_all_reduce", out_shape=jax.ShapeDtypeStruct(packed.shape, F32),
        in_specs=[pl.BlockSpec(memory_space=pltpu.VMEM)], out_specs=pl.BlockSpec(memory_space=pltpu.VMEM),
        scratch_shapes=[pltpu.VMEM((8, rows, LANES), F32), pltpu.SemaphoreType.DMA((7,)), pltpu.SemaphoreType.DMA((7,))],
        compiler_params=pltpu.CompilerParams(vmem_limit_bytes=VMEM_LIMIT_BYTES),
    )(packed)


def _layout(w_in, w_uq, w_ukv, v_ln_gain, q_norm, kv_norm):
    heads = 4 * w_uq.shape[-1] // (NOPE_DIM + ROPE_DIM)
    gw = v_ln_gain.shape[-1]
    ql, kl = q_norm.shape[-1], kv_norm.shape[-1]
    lay = dict(heads=heads, gw=gw, ql=ql, kl=kl, aw=heads * V_DIM, u_off=0, v_off=gw, q_off=2 * gw,
               kv_off=2 * gw + ql, kr_off=2 * gw + ql + kl)
    lay["in_pad"] = _round_up(lay["kr_off"] + LANES, 2 * LANES if lay["kr_off"] + LANES <= 2048 else 1024)
    assert lay["q_off"] % ql == 0 and lay["kv_off"] % kl == 0 and lay["aw"] % gw == 0
    assert 4 * w_in.shape[-1] == ql + kl + ROPE_DIM + 2 * gw
    return lay


def _rope_tile(t1, t2):
    z = jnp.zeros_like(t1)
    return jnp.concatenate([t1, z, t2, z], axis=-1)


def _w_in_padded(w, lay):
    ql, kl, gw = lay["ql"], lay["kl"], lay["gw"]
    q_c, kv_c = w[:, :ql], w[:, ql:ql + kl]
    r = w[:, ql + kl:ql + kl + ROPE_DIM]
    u = w[:, ql + kl + ROPE_DIM:ql + kl + ROPE_DIM + gw]
    v = w[:, ql + kl + ROPE_DIM + gw:]
    parts = [u, v, q_c, kv_c, _rope_tile(r[:, :ROPE_HALF], r[:, ROPE_HALF:])]
    pad = lay["in_pad"] - (lay["kr_off"] + LANES)
    if pad:
        parts.append(jnp.zeros((w.shape[0], pad), w.dtype))
    return jnp.concatenate(parts, axis=1)


def _w_in_grad_unpadded(dw, lay):
    ql, kl, gw = lay["ql"], lay["kl"], lay["gw"]
    ko = lay["kr_off"]
    return jnp.concatenate([dw[:, lay["q_off"]:lay["q_off"] + ql], dw[:, lay["kv_off"]:lay["kv_off"] + kl],
                            dw[:, ko:ko + ROPE_HALF], dw[:, ko + 2 * ROPE_HALF:ko + 3 * ROPE_HALF],
                            dw[:, :gw], dw[:, gw:2 * gw]], axis=1)


def _w_uq_padded(w, heads):
    w3 = w.reshape(w.shape[0], heads, NOPE_DIM + ROPE_DIM)
    t = _rope_tile(w3[..., NOPE_DIM:NOPE_DIM + ROPE_HALF], w3[..., NOPE_DIM + ROPE_HALF:])
    return jnp.concatenate([w3[..., :NOPE_DIM], t], axis=-1).reshape(w.shape[0], heads * HEAD_PAD)


def _w_uq_grad_unpadded(dw, heads):
    d3 = dw.reshape(dw.shape[0], heads, HEAD_PAD)
    return jnp.concatenate([d3[..., :NOPE_DIM], d3[..., NOPE_DIM:NOPE_DIM + ROPE_HALF],
                            d3[..., NOPE_DIM + 2 * ROPE_HALF:NOPE_DIM + 3 * ROPE_HALF]],
                           axis=-1).reshape(dw.shape[0], heads * (NOPE_DIM + ROPE_DIM))


def _cols_gathered(g):
    return jnp.transpose(g, (1, 0, 2)).reshape(g.shape[1], 4 * g.shape[2])


def _chunks_of_cols(grad):
    r, c4 = grad.shape
    return jnp.transpose(grad.reshape(2, r // 2, 4, c4 // 4), (0, 2, 1, 3)).astype(BF16)


SMALL = ["pre_mix_norm", "q_norm", "kv_norm", "v_ln_gain", "v_ln_bias", "w_spatial", "b_spatial", "attn_out_norm",
         "gmlp_out_norm", "post_mix_norm", "pre_ffn_norm", "post_ffn_norm"]
BIG = ["w_in", "w_uq", "w_ukv", "w_out", "w_gate", "w_up", "w_down"]
ORDER = ["pre_mix_norm", "w_in", "q_norm", "kv_norm", "w_uq", "w_ukv", "v_ln_gain", "v_ln_bias", "w_spatial",
         "b_spatial", "attn_out_norm", "gmlp_out_norm", "w_out", "post_mix_norm", "pre_ffn_norm", "w_gate", "w_up",
         "w_down", "post_ffn_norm"]


def _pack(arrays):
    flat = jnp.concatenate([a.reshape(-1) for a in arrays])
    n = flat.shape[0]
    total = _round_up(n, 8 * LANES)
    if total > n:
        flat = jnp.concatenate([flat, jnp.zeros((total - n,), F32)])
    return flat.reshape(total // LANES, LANES)


def _unpack(packed, like):
    flat = packed.reshape(-1)
    out, off = [], 0
    for a in like:
        out.append(flat[off:off + a.size].reshape(a.shape))
        off += a.size
    return out


def kernel(x, positions, pre_mix_norm, w_in, q_norm, kv_norm, w_uq, w_ukv, v_ln_gain, v_ln_bias, w_spatial, b_spatial, attn_out_norm, gmlp_out_norm, w_out, post_mix_norm, pre_ffn_norm, w_gate, w_up, w_down, post_ffn_norm, loss_target, m_pre_mix_norm, m_w_in, m_q_norm, m_kv_norm, m_w_uq, m_w_ukv, m_v_ln_gain, m_v_ln_bias, m_w_spatial, m_b_spatial, m_attn_out_norm, m_gmlp_out_norm, m_w_out, m_post_mix_norm, m_pre_ffn_norm, m_w_gate, m_w_up, m_w_down, m_post_ffn_norm, v_pre_mix_norm, v_w_in, v_q_norm, v_kv_norm, v_w_uq, v_w_ukv, v_v_ln_gain, v_v_ln_bias, v_w_spatial, v_b_spatial, v_attn_out_norm, v_gmlp_out_norm, v_w_out, v_post_mix_norm, v_pre_ffn_norm, v_w_gate, v_w_up, v_w_down, v_post_ffn_norm):
    args = dict(locals())
    weights = {n: args[n] for n in ORDER}
    m_in = {n: args["m_" + n] for n in ORDER}
    v_in = {n: args["v_" + n] for n in ORDER}

    lay = _layout(w_in, w_uq, w_ukv, v_ln_gain, q_norm, kv_norm)
    heads, gw = lay["heads"], lay["gw"]
    t, d = x.shape[1], x.shape[2]
    tr = 128 if t % 128 == 0 else t
    xs = x.reshape(t, d)
    target = loss_target.reshape(t, d)

    ffs = w_gate.shape[-1]
    ffp = _round_up(ffs, LANES)
    shards = {n: weights[n][0].astype(BF16) for n in BIG}
    shards["w_gate"] = jnp.pad(shards["w_gate"], ((0, 0), (0, ffp - ffs)))
    shards["w_up"] = jnp.pad(shards["w_up"], ((0, 0), (0, ffp - ffs)))
    shards["w_down"] = jnp.pad(shards["w_down"], ((0, ffp - ffs), (0, 0)))
    halved = [shards[n].reshape(2, shards[n].shape[0] // 2, shards[n].shape[1]) for n in BIG]
    gathered = [_place_own(g, s, "place_own_" + n) for n, g, s in zip(BIG, _all_gather(halved), halved)]
    full = {n: g.reshape(4, shards[n].shape[0], shards[n].shape[1]) for n, g in zip(BIG, gathered)}
    wb_in = _w_in_padded(_cols_gathered(full["w_in"]), lay)
    wb_uq = _w_uq_padded(_cols_gathered(full["w_uq"]), heads)
    wb_ukv = _cols_gathered(full["w_ukv"])
    wb_out = full["w_out"].reshape(-1, d)
    wb_gate, wb_up = full["w_gate"], full["w_up"]
    wb_down = full["w_down"].reshape(4 * ffp, d)

    inv_freq = 1.0 / (ROPE_THETA ** (jnp.arange(0, ROPE_DIM, 2, dtype=F32) / ROPE_DIM))
    ang = positions.reshape(t).astype(F32)[:, None] * inv_freq
    cos, sin = jnp.cos(ang), jnp.sin(ang)
    cos_t = _rope_tile(cos, cos)
    sin_t = _rope_tile(-sin, sin)

    row = lambda a: a.reshape(1, -1)
    g_pre, g_q, g_kv = row(pre_mix_norm), row(q_norm), row(kv_norm)
    g_a, g_g, g_pm = row(attn_out_norm), row(gmlp_out_norm), row(post_mix_norm)
    g_pf, g_po = row(pre_ffn_norm), row(post_ffn_norm)
    ln_g, ln_b = row(v_ln_gain), row(v_ln_bias)
    ws = w_spatial[0].astype(BF16)
    ws_t = jnp.transpose(ws, (0, 2, 1))
    bs_wide = jnp.broadcast_to(b_spatial[0][:, :, None], b_spatial.shape[1:] + (G_HEAD_DIM,))

    xn = _prenorm(xs, g_pre, tr)
    proj = _matmul(xn, wb_in, NN, F32, "proj")
    qn, kvn, kr = _qkv_prep(proj, g_q, g_kv, cos_t, sin_t, lay, tr)
    q = _q_rope(_matmul(qn, wb_uq, NN, F32, "q_up"), cos_t, sin_t, heads, tr)
    kv = _matmul(kvn, wb_ukv, NN, BF16, "kv_up")
    a_out = _attn_fwd(q, kv, kr, heads)
    gn = _gmlp_fwd(proj, ln_g, ln_b, ws, bs_wide, g_g, lay)
    mixed = _mix_norm(a_out, gn, g_a, tr)
    mix_out = _matmul(mixed, wb_out, NN, F32, "mix_out")
    h, hn = _post_mix(xs, mix_out, g_pm, g_pf, tr)
    gate = _matmul(hn, wb_gate, NN, BF16, "ffn_gate")
    up = _matmul(hn, wb_up, NN, BF16, "ffn_up")
    act = _swiglu(gate, up)
    ffn = _matmul(act, wb_down, NN, F32, "ffn_down")
    dy, d_ffn, dg_po, loss_vec = _loss_head(h, ffn, target, g_po, tr)

    d_act = _matmul(d_ffn, wb_down, NT, BF16, "d_act")
    gw_down = _matmul(act, d_ffn, TN, BF16, "gw_down", out_chunks="rows")
    d_gate, d_up = _swiglu_bwd(gate, up, d_act)
    d_hn = _matmul(d_up, wb_up, NT, F32, "d_hn", addend=_matmul(d_gate, wb_gate, NT, F32, "d_hn_gate"))
    gw_gate = _matmul(hn, d_gate, TN, BF16, "gw_gate", out_chunks="cols")
    gw_up = _matmul(hn, d_up, TN, BF16, "gw_up", out_chunks="cols")
    dh, d_mo, dg_pf, dg_pm = _post_mix_bwd(mix_out, h, dy, d_hn, g_pm, g_pf, tr)
    d_mixed = _matmul(d_mo, wb_out, NT, F32, "d_mixed")
    gw_out = _matmul(mixed, d_mo, TN, BF16, "gw_out", out_chunks="rows")
    d_a, dg_a = _mix_norm_bwd(a_out, d_mixed, g_a, tr)
    d_u, d_v, dg_g, d_ln_g, d_ln_b, d_ws, d_bs_wide = _gmlp_bwd(proj, d_mixed, ln_g, ln_b, ws, ws_t, bs_wide, g_g, lay)
    d_bs = _spatial_bias_grad(d_bs_wide)
    d_q, d_kv, d_kr = _attn_bwd(q, kv, kr, d_a, cos_t, sin_t, heads)
    d_qn = _matmul(d_q, wb_uq, NT, F32, "d_qn")
    gw_uq = _matmul(qn, d_q, TN, F32, "gw_uq")
    d_kvn = _matmul(d_kv, wb_ukv, NT, F32, "d_kvn")
    gw_ukv = _matmul(kvn, d_kv, TN, F32, "gw_ukv")
    d_qc, d_kvc, d_krt, dg_q, dg_kv = _qkv_bwd(proj, d_qn, d_kvn, d_kr, g_q, g_kv, cos_t, sin_t, lay, tr)
    parts = [d_u, d_v, d_qc, d_kvc, d_krt]
    pad = lay["in_pad"] - (lay["kr_off"] + LANES)
    if pad:
        parts.append(jnp.zeros((t, pad), BF16))
    d_proj = jnp.concatenate(parts, axis=1)
    d_xn = _matmul(d_proj, wb_in, NT, F32, "d_xn")
    gw_in = _matmul(xn, d_proj, TN, F32, "gw_in")
    grad_x, dg_pre = _prenorm_bwd(xs, d_xn, dh, g_pre, tr)

    partial = {
        "w_in": _chunks_of_cols(_w_in_grad_unpadded(gw_in, lay)),
        "w_uq": _chunks_of_cols(_w_uq_grad_unpadded(gw_uq, heads)),
        "w_ukv": _chunks_of_cols(gw_ukv),
        "w_out": gw_out, "w_gate": gw_gate, "w_up": gw_up, "w_down": gw_down,
    }
    from_sibling = _sibling_exchange([partial[n] for n in BIG])
    pair_sums = [_pair_add(partial[n], r, "pair_add_" + n) for n, r in zip(BIG, from_sibling)]
    received = _chip_scatter(pair_sums)
    mine = [_chip_sum(p, r, "chip_sum_" + n) for n, p, r in zip(BIG, pair_sums, received)]
    theirs = _halves_exchange(mine)
    grads, delta, new_m, new_v = {}, {}, {}, {}
    for n, g_mine, g_theirs in zip(BIG, mine, theirs):
        shape = weights[n].shape
        cols = shape[-1]
        out = _adamw_halves(weights[n][0], g_mine[:, :cols], g_theirs[:, :cols], m_in[n][0], v_in[n][0], "adamw_" + n)
        grads[n], delta[n], new_m[n], new_v[n] = [o.reshape(shape) for o in out]

    small_grads = {"pre_mix_norm": dg_pre, "q_norm": dg_q, "kv_norm": dg_kv, "v_ln_gain": d_ln_g, "v_ln_bias": d_ln_b,
                   "w_spatial": d_ws, "b_spatial": d_bs, "attn_out_norm": dg_a, "gmlp_out_norm": dg_g,
                   "post_mix_norm": dg_pm, "pre_ffn_norm": dg_pf, "post_ffn_norm": dg_po}
    like = [weights[n] for n in SMALL]
    reduced = _small_all_reduce(_pack([small_grads[n] for n in SMALL] + [loss_vec]))
    loss = reduced.reshape(-1)[sum(a.size for a in like)]
    small_g = _pack(_unpack(reduced, like))
    s_delta, s_m, s_v = _adamw(_pack(like), small_g, _pack([m_in[n] for n in SMALL]),
                               _pack([v_in[n] for n in SMALL]), "adamw_small")
    for n, g in zip(SMALL, _unpack(small_g, like)):
        grads[n] = g
    delta.update(zip(SMALL, _unpack(s_delta, like)))
    new_m.update(zip(SMALL, _unpack(s_m, like)))
    new_v.update(zip(SMALL, _unpack(s_v, like)))

    return (loss, grad_x.reshape(x.shape), *[grads[n] for n in ORDER], *[delta[n] for n in ORDER],
            *[new_m[n] for n in ORDER], *[new_v[n] for n in ORDER])
```

```python
import functools
import math

import jax
import jax.numpy as jnp
from jax import lax
from jax.experimental import pallas as pl
from jax.experimental.pallas import tpu as pltpu

F32 = jnp.float32
BF16 = jnp.bfloat16
MESH = pl.DeviceIdType.MESH

NOPE_DIM = 128
ROPE_DIM = 64
ROPE_HALF = ROPE_DIM // 2
V_DIM = 128
HEAD_PAD = 256
G_HEAD_DIM = 128
CHUNK = 128
ROPE_THETA = 10000.0
EPS = 1e-6
ADAM_LR = 0.001
ADAM_B1 = 0.9
ADAM_B2 = 0.999
ADAM_EPS = 1e-08
ADAM_WD = 0.01
ADAM_STEP = 10

LANES = 128
MATMUL_TILE = 1024
WIDE_TILE = 1408
VMEM_LIMIT_BYTES = 48 * 1024 * 1024

NN = (((1,), (0,)), ((), ()))
NT = (((1,), (1,)), ((), ()))
TN = (((0,), (0,)), ((), ()))


def _params(semantics):
    return pltpu.CompilerParams(dimension_semantics=semantics, vmem_limit_bytes=VMEM_LIMIT_BYTES)


def _tile(n, cap=MATMUL_TILE):
    if n <= cap:
        return n
    if cap == MATMUL_TILE and n % WIDE_TILE == 0:
        return WIDE_TILE
    t = cap - cap % LANES
    while n % t:
        t -= LANES
    assert t > 0, n
    return t


def _round_up(n, m):
    return (n + m - 1) // m * m


def _matmul(a, b, dims, out_dtype, name, addend=None, out_chunks=None):
    b_shard_cols = b.shape[2] if b.ndim == 3 else None
    b_shape = (b.shape[1], 4 * b.shape[2]) if b.ndim == 3 else b.shape
    if dims is NN:
        (m, k), (k2, n) = a.shape, b_shape
    elif dims is NT:
        (m, k), (n, k2) = a.shape, b_shape
    else:
        (k, m), (k2, n) = a.shape, b_shape
    assert k == k2, (a.shape, b.shape, name)
    tm, tn, tk = _tile(m), _tile(n), _tile(k)
    if b_shard_cols is not None:
        if dims is NT:
            tk = _tile(b_shard_cols)
        else:
            tn = _tile(b_shard_cols)
    if out_chunks == "cols":
        tm, tn = _tile(m // 2), _tile(n // 4)
    elif out_chunks == "rows":
        tm = _tile(m // 8)
    nk = k // tk

    def body(*refs):
        if addend is None:
            a_ref, b_ref, o_ref, acc_ref = refs
        else:
            a_ref, b_ref, c_ref, o_ref, acc_ref = refs
        kk = pl.program_id(2)

        @pl.when(kk == 0)
        def _():
            acc_ref[...] = jnp.zeros_like(acc_ref)

        acc_ref[...] += lax.dot_general(a_ref[...], b_ref[...], dims, preferred_element_type=F32)

        @pl.when(kk == nk - 1)
        def _():
            r = acc_ref[...]
            if addend is not None:
                r = r + c_ref[...]
            o_ref[...] = r.astype(o_ref.dtype)

    if dims is TN:
        a_spec = pl.BlockSpec((tk, tm), lambda i, j, kk: (kk, i))
    else:
        a_spec = pl.BlockSpec((tm, tk), lambda i, j, kk: (i, kk))
    if b_shard_cols is None:
        if dims is NT:
            b_spec = pl.BlockSpec((tn, tk), lambda i, j, kk: (j, kk))
        else:
            b_spec = pl.BlockSpec((tk, tn), lambda i, j, kk: (kk, j))
    elif dims is NT:
        per = b_shard_cols // tk
        b_spec = pl.BlockSpec((None, tn, tk), lambda i, j, kk: (kk // per, j, kk % per))
    else:
        per = b_shard_cols // tn
        b_spec = pl.BlockSpec((None, tk, tn), lambda i, j, kk: (j // per, kk, j % per))
    if out_chunks is None:
        o_spec = pl.BlockSpec((tm, tn), lambda i, j, kk: (i, j))
        o_shape = (m, n)
    elif out_chunks == "cols":
        pi, pj = m // 2 // tm, n // 4 // tn
        o_spec = pl.BlockSpec((None, None, tm, tn), lambda i, j, kk: (i // pi, j // pj, i % pi, j % pj))
        o_shape = (2, 4, m // 2, n // 4)
    else:
        pi = m // 8 // tm
        o_spec = pl.BlockSpec((None, None, tm, tn), lambda i, j, kk: ((i // pi) % 2, i // (2 * pi), i % pi, j))
        o_shape = (2, 4, m // 8, n)
    assert addend is None or out_chunks is None
    in_specs = [a_spec, b_spec] + ([o_spec] if addend is not None else [])
    args = (a, b) + ((addend,) if addend is not None else ())
    return pl.pallas_call(
        body, name=name, grid=(m // tm, n // tn, nk), in_specs=in_specs, out_specs=o_spec,
        out_shape=jax.ShapeDtypeStruct(o_shape, out_dtype),
        scratch_shapes=[pltpu.VMEM((tm, tn), F32)],
        compiler_params=_params(("parallel", "parallel", "arbitrary")),
    )(*args)


def _row_call(body, name, rows, tr, row_ins, par_ins, row_outs, acc_outs):
    def col(i, cb):
        return (i, cb)

    def whole(i, nd):
        return (0,) * nd

    in_specs = [pl.BlockSpec((tr, w), functools.partial(col, cb=cb)) for (_, w, cb) in row_ins]
    in_specs += [pl.BlockSpec(a.shape, functools.partial(whole, nd=a.ndim)) for a in par_ins]
    out_specs = [pl.BlockSpec((tr, w), lambda i: (i, 0)) for (w, _) in row_outs]
    out_specs += [pl.BlockSpec(s, functools.partial(whole, nd=len(s))) for (s, _) in acc_outs]
    out_shape = [jax.ShapeDtypeStruct((rows, w), dt) for (w, dt) in row_outs]
    out_shape += [jax.ShapeDtypeStruct(s, dt) for (s, dt) in acc_outs]
    return pl.pallas_call(
        body, name=name, grid=(rows // tr,), in_specs=in_specs, out_specs=out_specs, out_shape=out_shape,
        compiler_params=_params(("arbitrary",) if acc_outs else ("parallel",)),
    )(*[a for (a, _, _) in row_ins], *par_ins)


def _accumulate(ref, val):
    i = pl.program_id(0)

    @pl.when(i == 0)
    def _():
        ref[...] = val

    @pl.when(i > 0)
    def _():
        ref[...] += val


def _colsum(v):
    return jnp.sum(v, axis=0, keepdims=True)


def _rms_fwd(x, g):
    r = lax.rsqrt(jnp.mean(x * x, axis=-1, keepdims=True) + EPS)
    return x * r * g


def _rms_bwd(x, g, dy):
    r = lax.rsqrt(jnp.mean(x * x, axis=-1, keepdims=True) + EPS)
    xh = x * r
    dxh = dy * g
    dx = r * (dxh - xh * jnp.mean(dxh * xh, axis=-1, keepdims=True))
    return dx, dy * xh


_GELU_C = math.sqrt(2.0 / math.pi)
_GELU_A = 0.044715


def _gelu(x):
    return 0.5 * x * (1.0 + jnp.tanh(_GELU_C * (x + _GELU_A * (x * x * x))))


def _gelu_grad(x):
    t = jnp.tanh(_GELU_C * (x + _GELU_A * (x * x * x)))
    return 0.5 * (1.0 + t) + 0.5 * x * (1.0 - t * t) * (_GELU_C * (1.0 + 3.0 * _GELU_A * (x * x)))


def _sigmoid(x):
    return 1.0 / (1.0 + jnp.exp(-x))


def _rope_fwd(t, cos_t, sin_t):
    return t * cos_t + pltpu.roll(t, 2 * ROPE_HALF, 1) * sin_t


def _rope_bwd(dt, cos_t, sin_t):
    return dt * cos_t - pltpu.roll(dt, 2 * ROPE_HALF, 1) * sin_t


def _prenorm(x, g, tr, token):
    def body(x_ref, g_ref, token_ref, o_ref):
        o_ref[...] = _rms_fwd(x_ref[...], g_ref[...]).astype(BF16)

    t, d = x.shape
    return _row_call(body, "prenorm", t, tr, [(x, d, 0)], [g, token], [(d, BF16)], [])[0]


def _qkv_prep(proj, g_q, g_kv, cos_t, sin_t, lay, tr):
    ql, kl = lay["ql"], lay["kl"]

    def body(q_ref, kv_ref, kr_ref, cos_ref, sin_ref, gq_ref, gkv_ref, qn_ref, kvn_ref, kro_ref):
        qn_ref[...] = _rms_fwd(q_ref[...], gq_ref[...]).astype(BF16)
        kvn_ref[...] = _rms_fwd(kv_ref[...], gkv_ref[...]).astype(BF16)
        kro_ref[...] = _rope_fwd(kr_ref[...], cos_ref[...], sin_ref[...]).astype(BF16)

    t = proj.shape[0]
    return _row_call(
        body, "qkv_prep", t, tr,
        [(proj, ql, lay["q_off"] // ql), (proj, kl, lay["kv_off"] // kl), (proj, LANES, lay["kr_off"] // LANES),
         (cos_t, LANES, 0), (sin_t, LANES, 0)],
        [g_q, g_kv], [(ql, BF16), (kl, BF16), (LANES, BF16)], [])


def _q_rope(q, cos_t, sin_t, heads, tr):
    def body(q_ref, cos_ref, sin_ref, o_ref):
        c, s = cos_ref[...], sin_ref[...]
        for h in range(heads):
            lo = h * HEAD_PAD
            o_ref[:, lo:lo + NOPE_DIM] = q_ref[:, lo:lo + NOPE_DIM].astype(BF16)
            o_ref[:, lo + NOPE_DIM:lo + HEAD_PAD] = _rope_fwd(q_ref[:, lo + NOPE_DIM:lo + HEAD_PAD], c, s).astype(BF16)

    t, w = q.shape
    return _row_call(body, "q_rope", t, tr, [(q, w, 0), (cos_t, LANES, 0), (sin_t, LANES, 0)], [], [(w, BF16)], [])[0]


def _softmax_rows(s):
    m = jnp.max(s, axis=-1, keepdims=True)
    p = jnp.exp(s - m)
    return p * (1.0 / jnp.sum(p, axis=-1, keepdims=True))


def _attn_tile(t):
    return 256 if t % 256 == 0 else 128


def _attn_fwd(q, kv, kr, heads):
    t = q.shape[0]
    tq = _attn_tile(t)
    scale = 1.0 / math.sqrt(NOPE_DIM + ROPE_DIM)

    def body(q_ref, kv_ref, kr_ref, o_ref, kcat):
        @pl.when(pl.program_id(1) == 0)
        def _():
            kcat[:, :NOPE_DIM] = kv_ref[:, :NOPE_DIM]
            kcat[:, NOPE_DIM:] = kr_ref[...]

        s = lax.dot_general(q_ref[...], kcat[...], NT, preferred_element_type=F32) * scale
        p = _softmax_rows(s)
        o_ref[...] = jnp.dot(p.astype(BF16), kv_ref[:, NOPE_DIM:], preferred_element_type=F32)

    return pl.pallas_call(
        body, name="attn_fwd", grid=(heads, t // tq),
        in_specs=[pl.BlockSpec((tq, HEAD_PAD), lambda h, i: (i, h)),
                  pl.BlockSpec((t, HEAD_PAD), lambda h, i: (0, h)),
                  pl.BlockSpec((t, LANES), lambda h, i: (0, 0))],
        out_specs=pl.BlockSpec((tq, V_DIM), lambda h, i: (i, h)),
        out_shape=jax.ShapeDtypeStruct((t, heads * V_DIM), F32),
        scratch_shapes=[pltpu.VMEM((t, HEAD_PAD), BF16)],
        compiler_params=_params(("arbitrary", "arbitrary")),
    )(q, kv, kr)


def _attn_bwd(q, kv, kr, d_out, cos_t, sin_t, heads):
    t = q.shape[0]
    tq = _attn_tile(t)
    nq = t // tq
    scale = 1.0 / math.sqrt(NOPE_DIM + ROPE_DIM)

    def body(q_ref, kv_ref, kr_ref, do_ref, cos_ref, sin_ref, dq_ref, dkv_ref, dkr_ref, kcat, dk_acc, dv_acc):
        h, i = pl.program_id(0), pl.program_id(1)

        @pl.when(i == 0)
        def _():
            kcat[:, :NOPE_DIM] = kv_ref[:, :NOPE_DIM]
            kcat[:, NOPE_DIM:] = kr_ref[...]
            dk_acc[...] = jnp.zeros_like(dk_acc)
            dv_acc[...] = jnp.zeros_like(dv_acc)

        @pl.when((h == 0) & (i == 0))
        def _():
            dkr_ref[...] = jnp.zeros_like(dkr_ref)

        qb, dob = q_ref[...], do_ref[...]
        s = lax.dot_general(qb, kcat[...], NT, preferred_element_type=F32) * scale
        p = _softmax_rows(s)
        dv_acc[...] += lax.dot_general(p.astype(BF16), dob, TN, preferred_element_type=F32)
        dp = lax.dot_general(dob, kv_ref[:, NOPE_DIM:], NT, preferred_element_type=F32)
        ds = (p * (dp - jnp.sum(dp * p, axis=-1, keepdims=True)) * scale).astype(BF16)
        dq = jnp.dot(ds, kcat[...], preferred_element_type=F32)
        dq_ref[:, :NOPE_DIM] = dq[:, :NOPE_DIM].astype(BF16)
        dq_ref[:, NOPE_DIM:] = _rope_bwd(dq[:, NOPE_DIM:], cos_ref[...], sin_ref[...]).astype(BF16)
        dk_acc[...] += lax.dot_general(ds, qb, TN, preferred_element_type=F32)

        @pl.when(i == nq - 1)
        def _():
            dkv_ref[:, :NOPE_DIM] = dk_acc[:, :NOPE_DIM].astype(BF16)
            dkv_ref[:, NOPE_DIM:] = dv_acc[...].astype(BF16)
            dkr_ref[...] += dk_acc[:, NOPE_DIM:]

    return pl.pallas_call(
        body, name="attn_bwd", grid=(heads, nq),
        in_specs=[pl.BlockSpec((tq, HEAD_PAD), lambda h, i: (i, h)),
                  pl.BlockSpec((t, HEAD_PAD), lambda h, i: (0, h)),
                  pl.BlockSpec((t, LANES), lambda h, i: (0, 0)),
                  pl.BlockSpec((tq, V_DIM), lambda h, i: (i, h)),
                  pl.BlockSpec((tq, LANES), lambda h, i: (i, 0)),
                  pl.BlockSpec((tq, LANES), lambda h, i: (i, 0))],
        out_specs=[pl.BlockSpec((tq, HEAD_PAD), lambda h, i: (i, h)),
                   pl.BlockSpec((t, HEAD_PAD), lambda h, i: (0, h)),
                   pl.BlockSpec((t, LANES), lambda h, i: (0, 0))],
        out_shape=[jax.ShapeDtypeStruct((t, heads * HEAD_PAD), BF16),
                   jax.ShapeDtypeStruct((t, heads * HEAD_PAD), BF16),
                   jax.ShapeDtypeStruct((t, LANES), F32)],
        scratch_shapes=[pltpu.VMEM((t, HEAD_PAD), BF16), pltpu.VMEM((t, HEAD_PAD), F32), pltpu.VMEM((t, V_DIM), F32)],
        compiler_params=_params(("arbitrary", "arbitrary")),
    )(q, kv, kr, d_out, cos_t, sin_t)


def _layer_norm_parts(x):
    mu = jnp.mean(x, axis=-1, keepdims=True)
    xc = x - mu
    r = lax.rsqrt(jnp.mean(xc * xc, axis=-1, keepdims=True) + EPS)
    return xc * r, r


def _gmlp_fwd(proj, ln_g, ln_b, w_s, b_sb, g_out_norm, lay):
    gw = lay["gw"]
    g_heads = gw // G_HEAD_DIM

    def body(u_ref, v_ref, lng_ref, lnb_ref, ws_ref, bs_ref, gn_ref, o_ref, gate_ref):
        gu = _gelu(u_ref[...])
        vh, _ = _layer_norm_parts(_gelu(v_ref[...]))
        vln = (vh * lng_ref[...] + lnb_ref[...]).astype(BF16)
        for g in range(g_heads):
            cols = slice(g * G_HEAD_DIM, (g + 1) * G_HEAD_DIM)
            s = jnp.dot(ws_ref[g], vln[:, cols], preferred_element_type=F32) + bs_ref[g]
            gate_ref[:, cols] = gu[:, cols] * s
        o_ref[...] = _rms_fwd(gate_ref[...], gn_ref[...]).astype(BF16)

    t = proj.shape[0]
    in_specs = [pl.BlockSpec((CHUNK, gw), lambda i: (i, 0)), pl.BlockSpec((CHUNK, gw), lambda i: (i, 1))]
    pars = [ln_g, ln_b, w_s, b_sb, g_out_norm]
    in_specs += [pl.BlockSpec(a.shape, functools.partial(lambda i, nd: (0,) * nd, nd=a.ndim)) for a in pars]
    return pl.pallas_call(
        body, name="gmlp_fwd", grid=(t // CHUNK,), in_specs=in_specs,
        out_specs=pl.BlockSpec((CHUNK, gw), lambda i: (i, 0)),
        out_shape=jax.ShapeDtypeStruct((t, gw), BF16),
        scratch_shapes=[pltpu.VMEM((CHUNK, gw), F32)],
        compiler_params=_params(("parallel",)),
    )(proj, proj, *pars)


def _gmlp_bwd(proj, d_mixed, ln_g, ln_b, w_s, w_st, b_sb, g_out_norm, lay):
    gw = lay["gw"]
    g_heads = gw // G_HEAD_DIM
    aw_blocks = lay["aw"] // gw

    def body(u_ref, v_ref, dm_ref, lng_ref, lnb_ref, ws_ref, wst_ref, bs_ref, gn_ref,
             du_ref, dv_ref, dgn_ref, dlng_ref, dlnb_ref, dws_ref, dbs_ref, gate_ref, s_ref, dvln_ref):
        i = pl.program_id(0)
        u, v = u_ref[...], v_ref[...]
        gu, gv = _gelu(u), _gelu(v)
        vh, r_ln = _layer_norm_parts(gv)
        vln = (vh * lng_ref[...] + lnb_ref[...]).astype(BF16)
        for g in range(g_heads):
            cols = slice(g * G_HEAD_DIM, (g + 1) * G_HEAD_DIM)
            s = jnp.dot(ws_ref[g], vln[:, cols], preferred_element_type=F32) + bs_ref[g]
            s_ref[:, cols] = s
            gate_ref[:, cols] = gu[:, cols] * s
        d_gate, dgn = _rms_bwd(gate_ref[...], gn_ref[...], dm_ref[...])
        _accumulate(dgn_ref, _colsum(dgn))
        du_ref[...] = (d_gate * s_ref[...] * _gelu_grad(u)).astype(BF16)
        d_s = d_gate * gu
        d_sb = d_s.astype(BF16)
        for g in range(g_heads):
            cols = slice(g * G_HEAD_DIM, (g + 1) * G_HEAD_DIM)
            dw = lax.dot_general(d_sb[:, cols], vln[:, cols], NT, preferred_element_type=F32)

            @pl.when(i == 0)
            def _():
                dws_ref[g] = dw
                dbs_ref[g] = d_s[:, cols]

            @pl.when(i > 0)
            def _():
                dws_ref[g] += dw
                dbs_ref[g] += d_s[:, cols]

            dvln_ref[:, cols] = jnp.dot(wst_ref[g], d_sb[:, cols], preferred_element_type=F32)
        d_vln = dvln_ref[...]
        _accumulate(dlng_ref, _colsum(d_vln * vh))
        _accumulate(dlnb_ref, _colsum(d_vln))
        d_vh = d_vln * lng_ref[...]
        d_gv = r_ln * (d_vh - jnp.mean(d_vh, axis=-1, keepdims=True)
                       - vh * jnp.mean(d_vh * vh, axis=-1, keepdims=True))
        dv_ref[...] = (d_gv * _gelu_grad(v)).astype(BF16)

    t = proj.shape[0]
    whole = lambda a: pl.BlockSpec(a.shape, functools.partial(lambda i, nd: (0,) * nd, nd=a.ndim))
    pars = [ln_g, ln_b, w_s, w_st, b_sb, g_out_norm]
    hshape = (g_heads, CHUNK, CHUNK)
    return pl.pallas_call(
        body, name="gmlp_bwd", grid=(t // CHUNK,),
        in_specs=[pl.BlockSpec((CHUNK, gw), lambda i: (i, 0)), pl.BlockSpec((CHUNK, gw), lambda i: (i, 1)),
                  pl.BlockSpec((CHUNK, gw), lambda i: (i, aw_blocks))] + [whole(a) for a in pars],
        out_specs=[pl.BlockSpec((CHUNK, gw), lambda i: (i, 0)), pl.BlockSpec((CHUNK, gw), lambda i: (i, 0)),
                   pl.BlockSpec((1, gw), lambda i: (0, 0)), pl.BlockSpec((1, gw), lambda i: (0, 0)),
                   pl.BlockSpec((1, gw), lambda i: (0, 0)),
                   pl.BlockSpec(hshape, lambda i: (0, 0, 0)), pl.BlockSpec(hshape, lambda i: (0, 0, 0))],
        out_shape=[jax.ShapeDtypeStruct((t, gw), BF16), jax.ShapeDtypeStruct((t, gw), BF16),
                   jax.ShapeDtypeStruct((1, gw), F32), jax.ShapeDtypeStruct((1, gw), F32),
                   jax.ShapeDtypeStruct((1, gw), F32),
                   jax.ShapeDtypeStruct(hshape, F32), jax.ShapeDtypeStruct(hshape, F32)],
        scratch_shapes=[pltpu.VMEM((CHUNK, gw), F32), pltpu.VMEM((CHUNK, gw), F32), pltpu.VMEM((CHUNK, gw), F32)],
        compiler_params=_params(("arbitrary",)),
    )(proj, proj, d_mixed, *pars)


def _spatial_bias_grad(dbs_wide):
    g_heads = dbs_wide.shape[0]

    def body(x_ref, o_ref):
        for g in range(g_heads):
            o_ref[g:g + 1, :] = jnp.sum(x_ref[g].T, axis=0, keepdims=True)

    return pl.pallas_call(
        body, name="spatial_bias_grad", out_shape=jax.ShapeDtypeStruct((g_heads, CHUNK), F32),
        in_specs=[pl.BlockSpec(memory_space=pltpu.VMEM)], out_specs=pl.BlockSpec(memory_space=pltpu.VMEM),
    )(dbs_wide)


def _mix_norm(a_out, gn, g_a, tr):
    aw = a_out.shape[1]
    gw = gn.shape[1]

    def body(a_ref, gn_ref, g_ref, o_ref):
        o_ref[:, :aw] = _rms_fwd(a_ref[...], g_ref[...]).astype(BF16)
        o_ref[:, aw:] = gn_ref[...]

    t = a_out.shape[0]
    return _row_call(body, "mix_norm", t, tr, [(a_out, aw, 0), (gn, gw, 0)], [g_a], [(aw + gw, BF16)], [])[0]


def _mix_norm_bwd(a_out, d_mixed, g_a, tr):
    aw = a_out.shape[1]

    def body(a_ref, dm_ref, g_ref, da_ref, dg_ref):
        dx, dg = _rms_bwd(a_ref[...], g_ref[...], dm_ref[...])
        da_ref[...] = dx.astype(BF16)
        _accumulate(dg_ref, _colsum(dg))

    t = a_out.shape[0]
    return _row_call(body, "mix_norm_bwd", t, tr, [(a_out, aw, 0), (d_mixed, aw, 0)], [g_a],
                     [(aw, BF16)], [((1, aw), F32)])


def _post_mix(x, mix_out, g_pm, g_pf, tr):
    def body(x_ref, mo_ref, gpm_ref, gpf_ref, h_ref, hn_ref):
        h = x_ref[...] + _rms_fwd(mo_ref[...], gpm_ref[...])
        h_ref[...] = h
        hn_ref[...] = _rms_fwd(h, gpf_ref[...]).astype(BF16)

    t, d = x.shape
    return _row_call(body, "post_mix", t, tr, [(x, d, 0), (mix_out, d, 0)], [g_pm, g_pf], [(d, F32), (d, BF16)], [])


def _post_mix_bwd(mix_out, h, dy, d_hn, g_pm, g_pf, tr, token):
    def body(mo_ref, h_ref, dy_ref, dhn_ref, gpm_ref, gpf_ref, token_ref, dh_ref, dmo_ref, dgpf_ref, dgpm_ref):
        dx, dg = _rms_bwd(h_ref[...], gpf_ref[...], dhn_ref[...])
        dh = dy_ref[...] + dx
        dh_ref[...] = dh
        _accumulate(dgpf_ref, _colsum(dg))
        dmo, dg2 = _rms_bwd(mo_ref[...], gpm_ref[...], dh)
        dmo_ref[...] = dmo.astype(BF16)
        _accumulate(dgpm_ref, _colsum(dg2))

    t, d = h.shape
    return _row_call(body, "post_mix_bwd", t, tr, [(mix_out, d, 0), (h, d, 0), (dy, d, 0), (d_hn, d, 0)],
                     [g_pm, g_pf, token], [(d, F32), (d, BF16)], [((1, d), F32), ((1, d), F32)])


def _swiglu(gate, up):
    t, f = gate.shape
    tr, tf = _tile(t, 512), _tile(f, 2048)

    def body(g_ref, u_ref, o_ref):
        g = g_ref[...].astype(F32)
        o_ref[...] = (g * _sigmoid(g) * u_ref[...].astype(F32)).astype(BF16)

    spec = pl.BlockSpec((tr, tf), lambda i, j: (i, j))
    return pl.pallas_call(body, name="swiglu", grid=(t // tr, f // tf), in_specs=[spec, spec], out_specs=spec,
                          out_shape=jax.ShapeDtypeStruct((t, f), BF16),
                          compiler_params=_params(("parallel", "parallel")))(gate, up)


def _swiglu_bwd(gate, up, d_act):
    t, f = gate.shape
    tr, tf = _tile(t, 512), _tile(f, 2048)

    def body(g_ref, u_ref, da_ref, dg_ref, du_ref):
        g, u, da = g_ref[...].astype(F32), u_ref[...].astype(F32), da_ref[...].astype(F32)
        sg = _sigmoid(g)
        du_ref[...] = (da * (g * sg)).astype(BF16)
        dg_ref[...] = (da * u * (sg * (1.0 + g * (1.0 - sg)))).astype(BF16)

    spec = pl.BlockSpec((tr, tf), lambda i, j: (i, j))
    shape = jax.ShapeDtypeStruct((t, f), BF16)
    return pl.pallas_call(body, name="swiglu_bwd", grid=(t // tr, f // tf), in_specs=[spec, spec, spec],
                          out_specs=[spec, spec], out_shape=[shape, shape],
                          compiler_params=_params(("parallel", "parallel")))(gate, up, d_act)


def _loss_head(h, ffn, target, g_po, tr):
    t, d = h.shape

    def body(h_ref, f_ref, t_ref, g_ref, dy_ref, df_ref, dg_ref, loss_ref):
        f = f_ref[...]
        err = h_ref[...] + _rms_fwd(f, g_ref[...]) - t_ref[...]
        dy = err * (1.0 / d)
        dy_ref[...] = dy
        df, dg = _rms_bwd(f, g_ref[...], dy)
        df_ref[...] = df.astype(BF16)
        _accumulate(dg_ref, _colsum(dg))
        sq = jnp.sum(_colsum(err * err), axis=-1, keepdims=True) * (0.5 / d)
        _accumulate(loss_ref, jnp.broadcast_to(sq, (1, LANES)))

    return _row_call(body, "loss_head", t, tr, [(h, d, 0), (ffn, d, 0), (target, d, 0)], [g_po],
                     [(d, F32), (d, BF16)], [((1, d), F32), ((1, LANES), F32)])


def _qkv_bwd(proj, d_qn, d_kvn, d_kr, g_q, g_kv, cos_t, sin_t, lay, tr):
    ql, kl = lay["ql"], lay["kl"]

    def body(q_ref, kv_ref, dqn_ref, dkvn_ref, dkr_ref, cos_ref, sin_ref, gq_ref, gkv_ref,
             dq_ref, dkv_ref, dkt_ref, dgq_ref, dgkv_ref):
        dx, dg = _rms_bwd(q_ref[...], gq_ref[...], dqn_ref[...])
        dq_ref[...] = dx.astype(BF16)
        _accumulate(dgq_ref, _colsum(dg))
        dx, dg = _rms_bwd(kv_ref[...], gkv_ref[...], dkvn_ref[...])
        dkv_ref[...] = dx.astype(BF16)
        _accumulate(dgkv_ref, _colsum(dg))
        dkt_ref[...] = _rope_bwd(dkr_ref[...], cos_ref[...], sin_ref[...]).astype(BF16)

    t = proj.shape[0]
    return _row_call(
        body, "qkv_bwd", t, tr,
        [(proj, ql, lay["q_off"] // ql), (proj, kl, lay["kv_off"] // kl), (d_qn, ql, 0), (d_kvn, kl, 0),
         (d_kr, LANES, 0), (cos_t, LANES, 0), (sin_t, LANES, 0)],
        [g_q, g_kv], [(ql, BF16), (kl, BF16), (LANES, BF16)], [((1, ql), F32), ((1, kl), F32)])


def _prenorm_bwd(x, d_xn, dh, g, tr):
    def body(x_ref, dxn_ref, dh_ref, g_ref, gx_ref, dg_ref):
        dx, dg = _rms_bwd(x_ref[...], g_ref[...], dxn_ref[...])
        gx_ref[...] = dh_ref[...] + dx
        _accumulate(dg_ref, _colsum(dg))

    t, d = x.shape
    return _row_call(body, "prenorm_bwd", t, tr, [(x, d, 0), (d_xn, d, 0), (dh, d, 0)], [g],
                     [(d, F32)], [((1, d), F32)])


def _adam_rows(rows, cols):
    cap = max(8, (256 * 1024) // cols // 8 * 8)
    tr = min(rows, cap)
    while rows % tr:
        tr -= 8
    return tr


def _adamw(w, g, m, v, name):
    rows, cols = w.shape
    tr = _adam_rows(rows, cols)

    def body(w_ref, g_ref, m_ref, v_ref, d_ref, mo_ref, vo_ref):
        g = g_ref[...]
        m2 = ADAM_B1 * m_ref[...] + (1.0 - ADAM_B1) * g
        v2 = ADAM_B2 * v_ref[...] + (1.0 - ADAM_B2) * (g * g)
        m_hat = m2 / (1.0 - ADAM_B1 ** ADAM_STEP)
        v_hat = v2 / (1.0 - ADAM_B2 ** ADAM_STEP)
        d_ref[...] = -ADAM_LR * (m_hat / (jnp.sqrt(v_hat) + ADAM_EPS) + ADAM_WD * w_ref[...])
        mo_ref[...] = m2
        vo_ref[...] = v2

    spec = pl.BlockSpec((tr, cols), lambda i: (i, 0))
    shape = jax.ShapeDtypeStruct((rows, cols), F32)
    return pl.pallas_call(body, name=name, grid=(rows // tr,), in_specs=[spec] * 4, out_specs=[spec] * 3,
                          out_shape=[shape] * 3, compiler_params=_params(("parallel",)))(w, g, m, v)


def _adamw_halves(w, g_mine, g_theirs, m, v, name):
    rows, cols = w.shape
    rh = g_mine.shape[0]
    tr = _adam_rows(math.gcd(rows, rh), cols)
    per_half = rh // tr
    my_c = jnp.reshape(lax.axis_index("c"), (1,)).astype(jnp.int32)

    def body(c_ref, w_ref, gm_ref, gt_ref, m_ref, v_ref, g_ref, d_ref, mo_ref, vo_ref):
        mine = (pl.program_id(0) // per_half) == c_ref[0]
        g = jnp.where(mine, gm_ref[...], gt_ref[...])
        m2 = ADAM_B1 * m_ref[...] + (1.0 - ADAM_B1) * g
        v2 = ADAM_B2 * v_ref[...] + (1.0 - ADAM_B2) * (g * g)
        m_hat = m2 / (1.0 - ADAM_B1 ** ADAM_STEP)
        v_hat = v2 / (1.0 - ADAM_B2 ** ADAM_STEP)
        g_ref[...] = g
        d_ref[...] = -ADAM_LR * (m_hat / (jnp.sqrt(v_hat) + ADAM_EPS) + ADAM_WD * w_ref[...])
        mo_ref[...] = m2
        vo_ref[...] = v2

    def half_spec(is_mine):
        def index(i, c_ref):
            used = ((i // per_half) == c_ref[0]) if is_mine else ((i // per_half) != c_ref[0])
            return (jnp.where(used, i % per_half, 0), 0)
        return pl.BlockSpec((tr, cols), index)

    spec = pl.BlockSpec((tr, cols), lambda i, c_ref: (i, 0))
    shape = jax.ShapeDtypeStruct((rows, cols), F32)
    grid_spec = pltpu.PrefetchScalarGridSpec(
        num_scalar_prefetch=1, grid=(rows // tr,),
        in_specs=[spec, half_spec(True), half_spec(False), spec, spec], out_specs=[spec] * 4)
    return pl.pallas_call(body, name=name, grid_spec=grid_spec, out_shape=[shape] * 4,
                          compiler_params=_params(("parallel",)))(my_c, w, g_mine, g_theirs, m, v)


def _place_own(gathered, shard, name):
    _, _, r, c = gathered.shape
    tr = _adam_rows(r, c)
    own = jnp.reshape(2 * lax.axis_index("x") + lax.axis_index("y"), (1,)).astype(jnp.int32)

    def body(own_ref, g_ref, s_ref, o_ref):
        o_ref[...] = s_ref[...]

    grid_spec = pltpu.PrefetchScalarGridSpec(
        num_scalar_prefetch=1, grid=(2, r // tr),
        in_specs=[pl.BlockSpec(memory_space=pl.ANY), pl.BlockSpec((None, tr, c), lambda a, i, own_ref: (a, i, 0))],
        out_specs=pl.BlockSpec((None, None, tr, c), lambda a, i, own_ref: (own_ref[0], a, i, 0)))
    return pl.pallas_call(body, name=name, grid_spec=grid_spec,
                          out_shape=jax.ShapeDtypeStruct(gathered.shape, gathered.dtype),
                          input_output_aliases={1: 0},
                          compiler_params=_params(("parallel", "parallel")))(own, gathered, shard)


def _pair_add(parts, theirs, name):
    _, n, r, c = parts.shape
    tr = _adam_rows(r, c)
    my_c = jnp.reshape(lax.axis_index("c"), (1,)).astype(jnp.int32)

    def body(c_ref, a_ref, b_ref, o_ref):
        o_ref[0] = (a_ref[0, 0].astype(F32) + b_ref[0].astype(F32)).astype(BF16)

    spec = pl.BlockSpec((1, tr, c), lambda k, i, c_ref: (k, i, 0))
    grid_spec = pltpu.PrefetchScalarGridSpec(
        num_scalar_prefetch=1, grid=(n, r // tr),
        in_specs=[pl.BlockSpec((1, 1, tr, c), lambda k, i, c_ref: (c_ref[0], k, i, 0)), spec], out_specs=spec)
    return pl.pallas_call(body, name=name, grid_spec=grid_spec, out_shape=jax.ShapeDtypeStruct((n, r, c), BF16),
                          compiler_params=_params(("parallel", "parallel")))(my_c, parts, theirs)


def _chip_sum(pair_sums, received, name, token=None):
    _, r, c = pair_sums.shape
    tr = _adam_rows(r, c)
    own = 2 * lax.axis_index("x") + lax.axis_index("y")

    def body(own_ref, p_ref, r0_ref, r1_ref, r2_ref, *rest):
        o_ref = rest[-1]
        acc = p_ref[0].astype(F32) + r0_ref[0].astype(F32)
        acc = acc + r1_ref[0].astype(F32)
        o_ref[...] = acc + r2_ref[0].astype(F32)

    def rspec(j):
        return pl.BlockSpec((1, tr, c), functools.partial(lambda i, own_ref, j: (j, i, 0), j=j))

    extra = [] if token is None else [token]
    grid_spec = pltpu.PrefetchScalarGridSpec(
        num_scalar_prefetch=1, grid=(r // tr,),
        in_specs=[pl.BlockSpec((1, tr, c), lambda i, own_ref: (own_ref[0], i, 0)), rspec(0), rspec(1), rspec(2)]
        + [pl.BlockSpec(memory_space=pl.ANY)] * len(extra),
        out_specs=pl.BlockSpec((tr, c), lambda i, own_ref: (i, 0)))
    return pl.pallas_call(body, name=name, grid_spec=grid_spec, out_shape=jax.ShapeDtypeStruct((r, c), F32),
                          compiler_params=_params(("parallel",)))(
        jnp.reshape(own, (1,)).astype(jnp.int32), pair_sums, received, received, received, *extra)


def _mesh_place():
    x, y, c = lax.axis_index("x"), lax.axis_index("y"), lax.axis_index("c")
    other_chips = [(1 - x, y), (x, 1 - y), (1 - x, 1 - y)]
    return x, y, c, other_chips


def _hbm_specs(n):
    return [pl.BlockSpec(memory_space=pltpu.HBM)] * n


def _all_gather(shards):
    n = len(shards)

    def body(*refs):
        ins, outs = refs[:n], refs[n:2 * n]
        send_sems, recv_sems = refs[2 * n:]
        x, y, c, other_chips = _mesh_place()
        own = 2 * x + y
        sibling = (x, y, 1 - c)

        def remote(src, dst, k, to):
            return pltpu.make_async_remote_copy(src_ref=src, dst_ref=dst, send_sem=send_sems.at[k],
                                                recv_sem=recv_sems.at[k], device_id=to, device_id_type=MESH)

        first = [remote(ins[w].at[c], outs[w].at[own, c], 6 * w + j, (*chip, c))
                 for w in range(n) for j, chip in enumerate(other_chips)]
        for cp in first:
            cp.start()
        passed = []
        for w in range(n):
            for j, chip in enumerate(other_chips):
                rows = outs[w].at[2 * chip[0] + chip[1], c]
                remote(rows, rows, 6 * w + j, (*chip, c)).wait_recv()
                fwd = remote(rows, rows, 6 * w + 3 + j, sibling)
                fwd.start()
                passed.append(fwd)
        for w in range(n):
            for j, chip in enumerate(other_chips):
                rows = outs[w].at[2 * chip[0] + chip[1], 1 - c]
                remote(rows, rows, 6 * w + 3 + j, sibling).wait_recv()
        for cp in first + passed:
            cp.wait_send()

    return pl.pallas_call(
        body, name="weights_all_gather",
        out_shape=[jax.ShapeDtypeStruct((4,) + s.shape, s.dtype) for s in shards],
        in_specs=_hbm_specs(n), out_specs=_hbm_specs(n),
        scratch_shapes=[pltpu.SemaphoreType.DMA((6 * n,)), pltpu.SemaphoreType.DMA((6 * n,))],
    )(*shards)


def _sibling_exchange(parts, name):
    n = len(parts)

    def body(*refs):
        ins, outs = refs[:n], refs[n:2 * n]
        send_sems, recv_sems = refs[2 * n:]
        x, y, c, _ = _mesh_place()
        copies = [pltpu.make_async_remote_copy(src_ref=ins[w].at[1 - c], dst_ref=outs[w], send_sem=send_sems.at[w],
                                               recv_sem=recv_sems.at[w], device_id=(x, y, 1 - c), device_id_type=MESH)
                  for w in range(n)]
        for cp in copies:
            cp.start()
        for cp in copies:
            cp.wait()

    return pl.pallas_call(
        body, name=name,
        out_shape=[jax.ShapeDtypeStruct(p.shape[1:], p.dtype) for p in parts],
        in_specs=_hbm_specs(n), out_specs=_hbm_specs(n),
        scratch_shapes=[pltpu.SemaphoreType.DMA((n,)), pltpu.SemaphoreType.DMA((n,))],
    )(*parts)


SEM_SPEC = pl.BlockSpec(memory_space=pltpu.SEMAPHORE)
DATAFLOW_EFFECT = pltpu.SideEffectType.DATAFLOW_SIDE_EFFECTING


def _ici_copy(kind, src_refs, land_refs, send_sems, recv_sems, w, j, chip, c, own, arriving):
    theirs = 2 * chip[0] + chip[1]
    if kind == "gather":
        src, dst = src_refs[w].at[c], land_refs[w].at[theirs if arriving else own, c]
    else:
        src, dst = src_refs[w].at[theirs], land_refs[w].at[j]
    return pltpu.make_async_remote_copy(src_ref=src, dst_ref=dst, send_sem=send_sems.at[3 * w + j],
                                        recv_sem=recv_sems.at[3 * w + j], device_id=(*chip, c), device_id_type=MESH)


def _ici_start(kind, srcs, name):
    n = len(srcs)
    if kind == "gather":
        lands = [lax.empty((4,) + s.shape, s.dtype) for s in srcs]
    else:
        lands = [lax.empty((3,) + s.shape[1:], s.dtype) for s in srcs]

    def body(*refs):
        src_refs, land_refs = refs[:n], refs[n:2 * n]
        send_sems, recv_sems = refs[2 * n], refs[2 * n + 1]
        token = refs[-1]
        x, y, c, other_chips = _mesh_place()
        for w in range(n):
            for j, chip in enumerate(other_chips):
                _ici_copy(kind, src_refs, land_refs, send_sems, recv_sems, w, j, chip, c, 2 * x + y, False).start()
        token[...] = jnp.zeros_like(token)

    hbm = lambda a: pltpu.with_memory_space_constraint(a, pltpu.HBM)
    out = pl.pallas_call(
        body, name=name,
        out_shape=(pltpu.SemaphoreType.DMA((3 * n,)), pltpu.SemaphoreType.DMA((3 * n,)),
                   *[pltpu.HBM(a.shape, a.dtype) for a in srcs + lands], jax.ShapeDtypeStruct((8, LANES), F32)),
        in_specs=_hbm_specs(2 * n),
        out_specs=(SEM_SPEC, SEM_SPEC, *_hbm_specs(2 * n), pl.BlockSpec(memory_space=pltpu.VMEM)),
        input_output_aliases={i: 2 + i for i in range(2 * n)},
        compiler_params=pltpu.CompilerParams(has_side_effects=DATAFLOW_EFFECT),
    )(*[hbm(a) for a in srcs + lands])
    return out[0], out[1], list(out[2:2 + n]), list(out[2 + n:2 + 2 * n]), out[-1]


def _ici_wait(kind, send_sems, recv_sems, srcs, lands, after, name):
    n = len(srcs)

    def body(*refs):
        src_refs, land_refs = refs[:n], refs[n:2 * n]
        send_ref, recv_ref = refs[2 * n], refs[2 * n + 1]
        x, y, c, other_chips = _mesh_place()
        for w in range(n):
            for j, chip in enumerate(other_chips):
                cp = _ici_copy(kind, src_refs, land_refs, send_ref, recv_ref, w, j, chip, c, 2 * x + y, True)
                cp.wait_send()
                cp.wait_recv()

    out = pl.pallas_call(
        body, name=name, out_shape=tuple(pltpu.HBM(a.shape, a.dtype) for a in srcs + lands),
        in_specs=_hbm_specs(2 * n) + [SEM_SPEC, SEM_SPEC, pl.BlockSpec(memory_space=pl.ANY)],
        out_specs=tuple(_hbm_specs(2 * n)), input_output_aliases={i: i for i in range(2 * n)},
        compiler_params=pltpu.CompilerParams(has_side_effects=DATAFLOW_EFFECT),
    )(*srcs, *lands, send_sems, recv_sems, after)
    return list(out[:n]), list(out[n:])


def _forward_halves(lands, name):
    n = len(lands)

    def body(*refs):
        ins, outs = refs[:n], refs[n:2 * n]
        send_sems, recv_sems = refs[2 * n:]
        x, y, c, other_chips = _mesh_place()
        copies = []
        for w in range(n):
            for j, chip in enumerate(other_chips):
                k = 2 * chip[0] + chip[1]
                cp = pltpu.make_async_remote_copy(
                    src_ref=ins[w].at[k, c], dst_ref=outs[w].at[k, c], send_sem=send_sems.at[3 * w + j],
                    recv_sem=recv_sems.at[3 * w + j], device_id=(x, y, 1 - c), device_id_type=MESH)
                cp.start()
                copies.append(cp)
        for w in range(n):
            for j, chip in enumerate(other_chips):
                k = 2 * chip[0] + chip[1]
                pltpu.make_async_remote_copy(
                    src_ref=ins[w].at[k, c], dst_ref=outs[w].at[k, 1 - c], send_sem=send_sems.at[3 * w + j],
                    recv_sem=recv_sems.at[3 * w + j], device_id=(x, y, 1 - c), device_id_type=MESH).wait_recv()
        for cp in copies:
            cp.wait_send()

    return pl.pallas_call(
        body, name=name, out_shape=[jax.ShapeDtypeStruct(a.shape, a.dtype) for a in lands],
        in_specs=_hbm_specs(n), out_specs=_hbm_specs(n), input_output_aliases={i: i for i in range(n)},
        scratch_shapes=[pltpu.SemaphoreType.DMA((3 * n,)), pltpu.SemaphoreType.DMA((3 * n,))],
    )(*lands)


def _halves_exchange(halves, name):
    n = len(halves)

    def body(*refs):
        ins, outs = refs[:n], refs[n:2 * n]
        send_sems, recv_sems = refs[2 * n:]
        x, y, c, _ = _mesh_place()
        copies = [pltpu.make_async_remote_copy(src_ref=ins[w], dst_ref=outs[w], send_sem=send_sems.at[w],
                                               recv_sem=recv_sems.at[w], device_id=(x, y, 1 - c), device_id_type=MESH)
                  for w in range(n)]
        for cp in copies:
            cp.start()
        for cp in copies:
            cp.wait()

    return pl.pallas_call(
        body, name=name,
        out_shape=[jax.ShapeDtypeStruct(h.shape, h.dtype) for h in halves],
        in_specs=_hbm_specs(n), out_specs=_hbm_specs(n),
        scratch_shapes=[pltpu.SemaphoreType.DMA((n,)), pltpu.SemaphoreType.DMA((n,))],
    )(*halves)


def _small_all_reduce(packed):
    rows = packed.shape[0]

    def body(in_ref, out_ref, gathered, send_sems, recv_sems):
        x, y, c, _ = _mesh_place()
        me = 4 * x + 2 * y + c
        gathered[0] = in_ref[...]
        copies = []
        for rel in range(1, 8):
            to = (x ^ (rel >> 2), y ^ ((rel >> 1) & 1), c ^ (rel & 1))
            cp = pltpu.make_async_remote_copy(src_ref=in_ref, dst_ref=gathered.at[rel], send_sem=send_sems.at[rel - 1],
                                              recv_sem=recv_sems.at[rel - 1], device_id=to, device_id_type=MESH)
            cp.start()
            copies.append(cp)
        for cp in copies:
            cp.wait()
        acc = gathered[me]
        for dev in range(1, 8):
            acc = acc + gathered[dev ^ me]
        out_ref[...] = acc

    return pl.pallas_call(
        body, name="small_all_reduce", out_shape=jax.ShapeDtypeStruct(packed.shape, F32),
        in_specs=[pl.BlockSpec(memory_space=pltpu.VMEM)], out_specs=pl.BlockSpec(memory_space=pltpu.VMEM),
        scratch_shapes=[pltpu.VMEM((8, rows, LANES), F32), pltpu.SemaphoreType.DMA((7,)), pltpu.SemaphoreType.DMA((7,))],
        compiler_params=pltpu.CompilerParams(vmem_limit_bytes=VMEM_LIMIT_BYTES),
    )(packed)


def _layout(w_in, w_uq, w_ukv, v_ln_gain, q_norm, kv_norm):
    heads = 4 * w_uq.shape[-1] // (NOPE_DIM + ROPE_DIM)
    gw = v_ln_gain.shape[-1]
    ql, kl = q_norm.shape[-1], kv_norm.shape[-1]
    lay = dict(heads=heads, gw=gw, ql=ql, kl=kl, aw=heads * V_DIM, u_off=0, v_off=gw, q_off=2 * gw,
               kv_off=2 * gw + ql, kr_off=2 * gw + ql + kl)
    lay["in_pad"] = _round_up(lay["kr_off"] + LANES, 2 * LANES if lay["kr_off"] + LANES <= 2048 else 1024)
    assert lay["q_off"] % ql == 0 and lay["kv_off"] % kl == 0 and lay["aw"] % gw == 0
    assert 4 * w_in.shape[-1] == ql + kl + ROPE_DIM + 2 * gw
    return lay


def _rope_tile(t1, t2):
    z = jnp.zeros_like(t1)
    return jnp.concatenate([t1, z, t2, z], axis=-1)


def _w_in_padded(w, lay):
    ql, kl, gw = lay["ql"], lay["kl"], lay["gw"]
    q_c, kv_c = w[:, :ql], w[:, ql:ql + kl]
    r = w[:, ql + kl:ql + kl + ROPE_DIM]
    u = w[:, ql + kl + ROPE_DIM:ql + kl + ROPE_DIM + gw]
    v = w[:, ql + kl + ROPE_DIM + gw:]
    parts = [u, v, q_c, kv_c, _rope_tile(r[:, :ROPE_HALF], r[:, ROPE_HALF:])]
    pad = lay["in_pad"] - (lay["kr_off"] + LANES)
    if pad:
        parts.append(jnp.zeros((w.shape[0], pad), w.dtype))
    return jnp.concatenate(parts, axis=1)


def _w_in_grad_unpadded(dw, lay):
    ql, kl, gw = lay["ql"], lay["kl"], lay["gw"]
    ko = lay["kr_off"]
    return jnp.concatenate([dw[:, lay["q_off"]:lay["q_off"] + ql], dw[:, lay["kv_off"]:lay["kv_off"] + kl],
                            dw[:, ko:ko + ROPE_HALF], dw[:, ko + 2 * ROPE_HALF:ko + 3 * ROPE_HALF],
                            dw[:, :gw], dw[:, gw:2 * gw]], axis=1)


def _w_uq_padded(w, heads):
    w3 = w.reshape(w.shape[0], heads, NOPE_DIM + ROPE_DIM)
    t = _rope_tile(w3[..., NOPE_DIM:NOPE_DIM + ROPE_HALF], w3[..., NOPE_DIM + ROPE_HALF:])
    return jnp.concatenate([w3[..., :NOPE_DIM], t], axis=-1).reshape(w.shape[0], heads * HEAD_PAD)


def _w_uq_grad_unpadded(dw, heads):
    d3 = dw.reshape(dw.shape[0], heads, HEAD_PAD)
    return jnp.concatenate([d3[..., :NOPE_DIM], d3[..., NOPE_DIM:NOPE_DIM + ROPE_HALF],
                            d3[..., NOPE_DIM + 2 * ROPE_HALF:NOPE_DIM + 3 * ROPE_HALF]],
                           axis=-1).reshape(dw.shape[0], heads * (NOPE_DIM + ROPE_DIM))


def _cols_gathered(g):
    return jnp.transpose(g, (1, 0, 2)).reshape(g.shape[1], 4 * g.shape[2])


def _chunks_of_cols(grad):
    r, c4 = grad.shape
    return jnp.transpose(grad.reshape(2, r // 2, 4, c4 // 4), (0, 2, 1, 3)).astype(BF16)


SMALL = ["pre_mix_norm", "q_norm", "kv_norm", "v_ln_gain", "v_ln_bias", "w_spatial", "b_spatial", "attn_out_norm",
         "gmlp_out_norm", "post_mix_norm", "pre_ffn_norm", "post_ffn_norm"]
BIG = ["w_in", "w_uq", "w_ukv", "w_out", "w_gate", "w_up", "w_down"]
GATHER_NOW = ["w_in", "w_uq", "w_ukv"]
GATHER_LATER_1 = ["w_out", "w_gate"]
GATHER_LATER_2 = ["w_up", "w_down"]
REDUCE_FIRST = ["w_gate", "w_up", "w_down"]
REDUCE_LAST = ["w_in", "w_uq", "w_ukv", "w_out"]
ORDER = ["pre_mix_norm", "w_in", "q_norm", "kv_norm", "w_uq", "w_ukv", "v_ln_gain", "v_ln_bias", "w_spatial",
         "b_spatial", "attn_out_norm", "gmlp_out_norm", "w_out", "post_mix_norm", "pre_ffn_norm", "w_gate", "w_up",
         "w_down", "post_ffn_norm"]


def _pack(arrays):
    flat = jnp.concatenate([a.reshape(-1) for a in arrays])
    n = flat.shape[0]
    total = _round_up(n, 8 * LANES)
    if total > n:
        flat = jnp.concatenate([flat, jnp.zeros((total - n,), F32)])
    return flat.reshape(total // LANES, LANES)


def _unpack(packed, like):
    flat = packed.reshape(-1)
    out, off = [], 0
    for a in like:
        out.append(flat[off:off + a.size].reshape(a.shape))
        off += a.size
    return out


def kernel(x, positions, pre_mix_norm, w_in, q_norm, kv_norm, w_uq, w_ukv, v_ln_gain, v_ln_bias, w_spatial, b_spatial, attn_out_norm, gmlp_out_norm, w_out, post_mix_norm, pre_ffn_norm, w_gate, w_up, w_down, post_ffn_norm, loss_target, m_pre_mix_norm, m_w_in, m_q_norm, m_kv_norm, m_w_uq, m_w_ukv, m_v_ln_gain, m_v_ln_bias, m_w_spatial, m_b_spatial, m_attn_out_norm, m_gmlp_out_norm, m_w_out, m_post_mix_norm, m_pre_ffn_norm, m_w_gate, m_w_up, m_w_down, m_post_ffn_norm, v_pre_mix_norm, v_w_in, v_q_norm, v_kv_norm, v_w_uq, v_w_ukv, v_v_ln_gain, v_v_ln_bias, v_w_spatial, v_b_spatial, v_attn_out_norm, v_gmlp_out_norm, v_w_out, v_post_mix_norm, v_pre_ffn_norm, v_w_gate, v_w_up, v_w_down, v_post_ffn_norm):
    args = dict(locals())
    weights = {n: args[n] for n in ORDER}
    m_in = {n: args["m_" + n] for n in ORDER}
    v_in = {n: args["v_" + n] for n in ORDER}

    lay = _layout(w_in, w_uq, w_ukv, v_ln_gain, q_norm, kv_norm)
    heads, gw = lay["heads"], lay["gw"]
    t, d = x.shape[1], x.shape[2]
    tr = 128 if t % 128 == 0 else t
    xs = x.reshape(t, d)
    target = loss_target.reshape(t, d)

    ffs = w_gate.shape[-1]
    ffp = _round_up(ffs, LANES)
    shards = {n: weights[n][0].astype(BF16) for n in BIG}
    shards["w_gate"] = jnp.pad(shards["w_gate"], ((0, 0), (0, ffp - ffs)))
    shards["w_up"] = jnp.pad(shards["w_up"], ((0, 0), (0, ffp - ffs)))
    shards["w_down"] = jnp.pad(shards["w_down"], ((0, ffp - ffs), (0, 0)))
    halved = {n: shards[n].reshape(2, shards[n].shape[0] // 2, shards[n].shape[1]) for n in BIG}
    full = {}

    def pair_sums_of(partial, names, tag):
        from_sibling = _sibling_exchange([partial[n] for n in names], "grads_sibling_exchange_" + tag)
        return [_pair_add(partial[n], r, "pair_add_" + n) for n, r in zip(names, from_sibling)]

    def place(names, lands, srcs):
        for n, g, s in zip(names, lands, srcs):
            full[n] = _place_own(g, s, "place_own_" + n).reshape((4,) + shards[n].shape)

    place(GATHER_NOW, _all_gather([halved[n] for n in GATHER_NOW]), [halved[n] for n in GATHER_NOW])
    flight_1 = _ici_start("gather", [halved[n] for n in GATHER_LATER_1], "gather_start_1")
    flight_2 = _ici_start("gather", [halved[n] for n in GATHER_LATER_2], "gather_start_2")
    wb_in = _w_in_padded(_cols_gathered(full["w_in"]), lay)
    wb_uq = _w_uq_padded(_cols_gathered(full["w_uq"]), heads)
    wb_ukv = _cols_gathered(full["w_ukv"])

    inv_freq = 1.0 / (ROPE_THETA ** (jnp.arange(0, ROPE_DIM, 2, dtype=F32) / ROPE_DIM))
    ang = positions.reshape(t).astype(F32)[:, None] * inv_freq
    cos, sin = jnp.cos(ang), jnp.sin(ang)
    cos_t = _rope_tile(cos, cos)
    sin_t = _rope_tile(-sin, sin)

    row = lambda a: a.reshape(1, -1)
    g_pre, g_q, g_kv = row(pre_mix_norm), row(q_norm), row(kv_norm)
    g_a, g_g, g_pm = row(attn_out_norm), row(gmlp_out_norm), row(post_mix_norm)
    g_pf, g_po = row(pre_ffn_norm), row(post_ffn_norm)
    ln_g, ln_b = row(v_ln_gain), row(v_ln_bias)
    ws = w_spatial[0].astype(BF16)
    ws_t = jnp.transpose(ws, (0, 2, 1))
    bs_wide = jnp.broadcast_to(b_spatial[0][:, :, None], b_spatial.shape[1:] + (G_HEAD_DIM,))

    xn = _prenorm(xs, g_pre, tr, flight_1[4] + flight_2[4])
    proj = _matmul(xn, wb_in, NN, F32, "proj")
    qn, kvn, kr = _qkv_prep(proj, g_q, g_kv, cos_t, sin_t, lay, tr)
    q = _q_rope(_matmul(qn, wb_uq, NN, F32, "q_up"), cos_t, sin_t, heads, tr)
    kv = _matmul(kvn, wb_ukv, NN, BF16, "kv_up")
    a_out = _attn_fwd(q, kv, kr, heads)
    gn = _gmlp_fwd(proj, ln_g, ln_b, ws, bs_wide, g_g, lay)
    mixed = _mix_norm(a_out, gn, g_a, tr)
    srcs, lands = _ici_wait("gather", *flight_1[:4], mixed, "gather_wait_1")
    place(GATHER_LATER_1, _forward_halves(lands, "gather_forward_1"), srcs)
    wb_out = full["w_out"].reshape(-1, d)
    wb_gate = full["w_gate"]
    mix_out = _matmul(mixed, wb_out, NN, F32, "mix_out")
    h, hn = _post_mix(xs, mix_out, g_pm, g_pf, tr)
    gate = _matmul(hn, wb_gate, NN, BF16, "ffn_gate")
    srcs, lands = _ici_wait("gather", *flight_2[:4], gate, "gather_wait_2")
    place(GATHER_LATER_2, _forward_halves(lands, "gather_forward_2"), srcs)
    wb_up = full["w_up"]
    wb_down = full["w_down"].reshape(4 * ffp, d)
    up = _matmul(hn, wb_up, NN, BF16, "ffn_up")
    act = _swiglu(gate, up)
    ffn = _matmul(act, wb_down, NN, F32, "ffn_down")
    dy, d_ffn, dg_po, loss_vec = _loss_head(h, ffn, target, g_po, tr)

    d_act = _matmul(d_ffn, wb_down, NT, BF16, "d_act")
    gw_down = _matmul(act, d_ffn, TN, BF16, "gw_down", out_chunks="rows")
    d_gate, d_up = _swiglu_bwd(gate, up, d_act)
    d_hn = _matmul(d_up, wb_up, NT, F32, "d_hn", addend=_matmul(d_gate, wb_gate, NT, F32, "d_hn_gate"))
    gw_gate = _matmul(hn, d_gate, TN, BF16, "gw_gate", out_chunks="cols")
    gw_up = _matmul(hn, d_up, TN, BF16, "gw_up", out_chunks="cols")
    pair_ffn = pair_sums_of({"w_gate": gw_gate, "w_up": gw_up, "w_down": gw_down}, REDUCE_FIRST, "ffn")
    flight_g1 = _ici_start("scatter", pair_ffn, "scatter_start_ffn")
    dh, d_mo, dg_pf, dg_pm = _post_mix_bwd(mix_out, h, dy, d_hn, g_pm, g_pf, tr, flight_g1[4])
    d_mixed = _matmul(d_mo, wb_out, NT, F32, "d_mixed")
    gw_out = _matmul(mixed, d_mo, TN, BF16, "gw_out", out_chunks="rows")
    d_a, dg_a = _mix_norm_bwd(a_out, d_mixed, g_a, tr)
    d_u, d_v, dg_g, d_ln_g, d_ln_b, d_ws, d_bs_wide = _gmlp_bwd(proj, d_mixed, ln_g, ln_b, ws, ws_t, bs_wide, g_g, lay)
    d_bs = _spatial_bias_grad(d_bs_wide)
    d_q, d_kv, d_kr = _attn_bwd(q, kv, kr, d_a, cos_t, sin_t, heads)
    d_qn = _matmul(d_q, wb_uq, NT, F32, "d_qn")
    gw_uq = _matmul(qn, d_q, TN, F32, "gw_uq")
    d_kvn = _matmul(d_kv, wb_ukv, NT, F32, "d_kvn")
    gw_ukv = _matmul(kvn, d_kv, TN, F32, "gw_ukv")
    d_qc, d_kvc, d_krt, dg_q, dg_kv = _qkv_bwd(proj, d_qn, d_kvn, d_kr, g_q, g_kv, cos_t, sin_t, lay, tr)
    parts = [d_u, d_v, d_qc, d_kvc, d_krt]
    pad = lay["in_pad"] - (lay["kr_off"] + LANES)
    if pad:
        parts.append(jnp.zeros((t, pad), BF16))
    d_proj = jnp.concatenate(parts, axis=1)
    d_xn = _matmul(d_proj, wb_in, NT, F32, "d_xn")
    gw_in = _matmul(xn, d_proj, TN, F32, "gw_in")
    grad_x, dg_pre = _prenorm_bwd(xs, d_xn, dh, g_pre, tr)

    pair_mix = pair_sums_of({"w_in": _chunks_of_cols(_w_in_grad_unpadded(gw_in, lay)),
                             "w_uq": _chunks_of_cols(_w_uq_grad_unpadded(gw_uq, heads)),
                             "w_ukv": _chunks_of_cols(gw_ukv), "w_out": gw_out}, REDUCE_LAST, "mix")
    pair_ffn, received = _ici_wait("scatter", *flight_g1[:4], pair_mix[-1], "scatter_wait_ffn")
    flight_g2 = _ici_start("scatter", pair_mix, "scatter_start_mix")
    grads, delta, new_m, new_v = {}, {}, {}, {}

    def finish(names, pair_sums, received, tag, token):
        mine = [_chip_sum(p, r, "chip_sum_" + n, token) for n, p, r in zip(names, pair_sums, received)]
        theirs = _halves_exchange(mine, "grads_halves_exchange_" + tag)
        for n, g_mine, g_theirs in zip(names, mine, theirs):
            shape = weights[n].shape
            cols = shape[-1]
            out = _adamw_halves(weights[n][0], g_mine[:, :cols], g_theirs[:, :cols], m_in[n][0], v_in[n][0],
                                "adamw_" + n)
            grads[n], delta[n], new_m[n], new_v[n] = [o.reshape(shape) for o in out]

    finish(REDUCE_FIRST, pair_ffn, received, "ffn", flight_g2[4])
    pair_mix, received = _ici_wait("scatter", *flight_g2[:4], new_v[REDUCE_FIRST[-1]], "scatter_wait_mix")
    finish(REDUCE_LAST, pair_mix, received, "mix", None)

    small_grads = {"pre_mix_norm": dg_pre, "q_norm": dg_q, "kv_norm": dg_kv, "v_ln_gain": d_ln_g, "v_ln_bias": d_ln_b,
                   "w_spatial": d_ws, "b_spatial": d_bs, "attn_out_norm": dg_a, "gmlp_out_norm": dg_g,
                   "post_mix_norm": dg_pm, "pre_ffn_norm": dg_pf, "post_ffn_norm": dg_po}
    like = [weights[n] for n in SMALL]
    reduced = _small_all_reduce(_pack([small_grads[n] for n in SMALL] + [loss_vec]))
    loss = reduced.reshape(-1)[sum(a.size for a in like)]
    small_g = _pack(_unpack(reduced, like))
    s_delta, s_m, s_v = _adamw(_pack(like), small_g, _pack([m_in[n] for n in SMALL]),
                               _pack([v_in[n] for n in SMALL]), "adamw_small")
    for n, g in zip(SMALL, _unpack(small_g, like)):
        grads[n] = g
    delta.update(zip(SMALL, _unpack(s_delta, like)))
    new_m.update(zip(SMALL, _unpack(s_m, like)))
    new_v.update(zip(SMALL, _unpack(s_v, like)))

    return (loss, grad_x.reshape(x.shape), *[grads[n] for n in ORDER], *[delta[n] for n in ORDER],
            *[new_m[n] for n in ORDER], *[new_v[n] for n in ORDER])
```

```python
import functools
import math

import jax
import jax.numpy as jnp
from jax import lax
from jax.experimental import pallas as pl
from jax.experimental.pallas import tpu as pltpu

F32 = jnp.float32
BF16 = jnp.bfloat16
MESH = pl.DeviceIdType.MESH

NOPE_DIM = 128
ROPE_DIM = 64
ROPE_HALF = ROPE_DIM // 2
V_DIM = 128
HEAD_PAD = 256
G_HEAD_DIM = 128
CHUNK = 128
ROPE_THETA = 10000.0
EPS = 1e-6
ADAM_LR = 0.001
ADAM_B1 = 0.9
ADAM_B2 = 0.999
ADAM_EPS = 1e-08
ADAM_WD = 0.01
ADAM_STEP = 10

LANES = 128
MATMUL_TILE = 1024
WIDE_TILE = 1408
VMEM_LIMIT_BYTES = 48 * 1024 * 1024

NN = (((1,), (0,)), ((), ()))
NT = (((1,), (1,)), ((), ()))
TN = (((0,), (0,)), ((), ()))


def _params(semantics):
    return pltpu.CompilerParams(dimension_semantics=semantics, vmem_limit_bytes=VMEM_LIMIT_BYTES)


def _tile(n, cap=MATMUL_TILE):
    if n <= cap:
        return n
    if cap == MATMUL_TILE and n % WIDE_TILE == 0:
        return WIDE_TILE
    t = cap - cap % LANES
    while n % t:
        t -= LANES
    assert t > 0, n
    return t


def _round_up(n, m):
    return (n + m - 1) // m * m


def _matmul(a, b, dims, out_dtype, name, addend=None, out_chunks=None):
    if dims is NN:
        (m, k), (k2, n) = a.shape, b.shape
    elif dims is NT:
        (m, k), (n, k2) = a.shape, b.shape
    else:
        (k, m), (k2, n) = a.shape, b.shape
    assert k == k2, (a.shape, b.shape, name)
    tm, tn, tk = _tile(m // 8 if out_chunks else m), _tile(n), _tile(k)
    nk = k // tk

    def body(*refs):
        if addend is None:
            a_ref, b_ref, o_ref, acc_ref = refs
        else:
            a_ref, b_ref, c_ref, o_ref, acc_ref = refs
        kk = pl.program_id(2)

        @pl.when(kk == 0)
        def _():
            acc_ref[...] = jnp.zeros_like(acc_ref)

        acc_ref[...] += lax.dot_general(a_ref[...], b_ref[...], dims, preferred_element_type=F32)

        @pl.when(kk == nk - 1)
        def _():
            r = acc_ref[...]
            if addend is not None:
                r = r + c_ref[...]
            o_ref[...] = r.astype(o_ref.dtype)

    if dims is TN:
        a_spec = pl.BlockSpec((tk, tm), lambda i, j, kk: (kk, i))
    else:
        a_spec = pl.BlockSpec((tm, tk), lambda i, j, kk: (i, kk))
    if dims is NT:
        b_spec = pl.BlockSpec((tn, tk), lambda i, j, kk: (j, kk))
    else:
        b_spec = pl.BlockSpec((tk, tn), lambda i, j, kk: (kk, j))
    if not out_chunks:
        o_spec = pl.BlockSpec((tm, tn), lambda i, j, kk: (i, j))
        o_shape = (m, n)
    else:
        pi = m // 8 // tm
        o_spec = pl.BlockSpec((None, None, tm, tn), lambda i, j, kk: ((i // pi) % 2, i // (2 * pi), i % pi, j))
        o_shape = (2, 4, m // 8, n)
    assert addend is None or out_chunks is None
    in_specs = [a_spec, b_spec] + ([o_spec] if addend is not None else [])
    args = (a, b) + ((addend,) if addend is not None else ())
    return pl.pallas_call(
        body, name=name, grid=(m // tm, n // tn, nk), in_specs=in_specs, out_specs=o_spec,
        out_shape=jax.ShapeDtypeStruct(o_shape, out_dtype),
        scratch_shapes=[pltpu.VMEM((tm, tn), F32)],
        compiler_params=_params(("parallel", "parallel", "arbitrary")),
    )(*args)


def _row_call(body, name, rows, tr, row_ins, par_ins, row_outs, acc_outs):
    def col(i, cb):
        return (i, cb)

    def whole(i, nd):
        return (0,) * nd

    in_specs = [pl.BlockSpec((tr, w), functools.partial(col, cb=cb)) for (_, w, cb) in row_ins]
    in_specs += [pl.BlockSpec(a.shape, functools.partial(whole, nd=a.ndim)) for a in par_ins]
    out_specs = [pl.BlockSpec((tr, w), lambda i: (i, 0)) for (w, _) in row_outs]
    out_specs += [pl.BlockSpec(s, functools.partial(whole, nd=len(s))) for (s, _) in acc_outs]
    out_shape = [jax.ShapeDtypeStruct((rows, w), dt) for (w, dt) in row_outs]
    out_shape += [jax.ShapeDtypeStruct(s, dt) for (s, dt) in acc_outs]
    return pl.pallas_call(
        body, name=name, grid=(rows // tr,), in_specs=in_specs, out_specs=out_specs, out_shape=out_shape,
        compiler_params=_params(("arbitrary",) if acc_outs else ("parallel",)),
    )(*[a for (a, _, _) in row_ins], *par_ins)


def _accumulate(ref, val):
    i = pl.program_id(0)

    @pl.when(i == 0)
    def _():
        ref[...] = val

    @pl.when(i > 0)
    def _():
        ref[...] += val


def _colsum(v):
    return jnp.sum(v, axis=0, keepdims=True)


def _rms_fwd(x, g):
    r = lax.rsqrt(jnp.mean(x * x, axis=-1, keepdims=True) + EPS)
    return x * r * g


def _rms_bwd(x, g, dy):
    r = lax.rsqrt(jnp.mean(x * x, axis=-1, keepdims=True) + EPS)
    xh = x * r
    dxh = dy * g
    dx = r * (dxh - xh * jnp.mean(dxh * xh, axis=-1, keepdims=True))
    return dx, dy * xh


_GELU_C = math.sqrt(2.0 / math.pi)
_GELU_A = 0.044715


def _gelu(x):
    return 0.5 * x * (1.0 + jnp.tanh(_GELU_C * (x + _GELU_A * (x * x * x))))


def _gelu_grad(x):
    t = jnp.tanh(_GELU_C * (x + _GELU_A * (x * x * x)))
    return 0.5 * (1.0 + t) + 0.5 * x * (1.0 - t * t) * (_GELU_C * (1.0 + 3.0 * _GELU_A * (x * x)))


def _sigmoid(x):
    return 1.0 / (1.0 + jnp.exp(-x))


def _rope_fwd(t, cos_t, sin_t):
    return t * cos_t + pltpu.roll(t, 2 * ROPE_HALF, 1) * sin_t


def _rope_bwd(dt, cos_t, sin_t):
    return dt * cos_t - pltpu.roll(dt, 2 * ROPE_HALF, 1) * sin_t


def _prenorm(x, g, tr, token):
    def body(x_ref, g_ref, token_ref, o_ref):
        o_ref[...] = _rms_fwd(x_ref[...], g_ref[...]).astype(BF16)

    t, d = x.shape
    return _row_call(body, "prenorm", t, tr, [(x, d, 0)], [g, token], [(d, BF16)], [])[0]


def _qkv_prep(proj, g_q, g_kv, cos_t, sin_t, lay, tr):
    ql, kl = lay["ql"], lay["kl"]

    def body(q_ref, kv_ref, kr_ref, cos_ref, sin_ref, gq_ref, gkv_ref, qn_ref, kvn_ref, kro_ref):
        qn_ref[...] = _rms_fwd(q_ref[...], gq_ref[...]).astype(BF16)
        kvn_ref[...] = _rms_fwd(kv_ref[...], gkv_ref[...]).astype(BF16)
        kro_ref[...] = _rope_fwd(kr_ref[...], cos_ref[...], sin_ref[...]).astype(BF16)

    t = proj.shape[0]
    return _row_call(
        body, "qkv_prep", t, tr,
        [(proj, ql, lay["q_off"] // ql), (proj, kl, lay["kv_off"] // kl), (proj, LANES, lay["kr_off"] // LANES),
         (cos_t, LANES, 0), (sin_t, LANES, 0)],
        [g_q, g_kv], [(ql, BF16), (kl, BF16), (LANES, BF16)], [])


def _q_rope(q, cos_t, sin_t, heads, tr):
    def body(q_ref, cos_ref, sin_ref, o_ref):
        c, s = cos_ref[...], sin_ref[...]
        for h in range(heads):
            lo = h * HEAD_PAD
            o_ref[:, lo:lo + NOPE_DIM] = q_ref[:, lo:lo + NOPE_DIM].astype(BF16)
            o_ref[:, lo + NOPE_DIM:lo + HEAD_PAD] = _rope_fwd(q_ref[:, lo + NOPE_DIM:lo + HEAD_PAD], c, s).astype(BF16)

    t, w = q.shape
    return _row_call(body, "q_rope", t, tr, [(q, w, 0), (cos_t, LANES, 0), (sin_t, LANES, 0)], [], [(w, BF16)], [])[0]


def _softmax_rows(s):
    m = jnp.max(s, axis=-1, keepdims=True)
    p = jnp.exp(s - m)
    return p * (1.0 / jnp.sum(p, axis=-1, keepdims=True))


def _attn_tile(t):
    return 256 if t % 256 == 0 else 128


def _attn_fwd(q, kv, kr, heads):
    t = q.shape[0]
    tq = _attn_tile(t)
    scale = 1.0 / math.sqrt(NOPE_DIM + ROPE_DIM)

    def body(q_ref, kv_ref, kr_ref, o_ref, kcat):
        @pl.when(pl.program_id(1) == 0)
        def _():
            kcat[:, :NOPE_DIM] = kv_ref[:, :NOPE_DIM]
            kcat[:, NOPE_DIM:] = kr_ref[...]

        s = lax.dot_general(q_ref[...], kcat[...], NT, preferred_element_type=F32) * scale
        p = _softmax_rows(s)
        o_ref[...] = jnp.dot(p.astype(BF16), kv_ref[:, NOPE_DIM:], preferred_element_type=F32)

    return pl.pallas_call(
        body, name="attn_fwd", grid=(heads, t // tq),
        in_specs=[pl.BlockSpec((tq, HEAD_PAD), lambda h, i: (i, h)),
                  pl.BlockSpec((t, HEAD_PAD), lambda h, i: (0, h)),
                  pl.BlockSpec((t, LANES), lambda h, i: (0, 0))],
        out_specs=pl.BlockSpec((tq, V_DIM), lambda h, i: (i, h)),
        out_shape=jax.ShapeDtypeStruct((t, heads * V_DIM), F32),
        scratch_shapes=[pltpu.VMEM((t, HEAD_PAD), BF16)],
        compiler_params=_params(("arbitrary", "arbitrary")),
    )(q, kv, kr)


def _attn_bwd(q, kv, kr, d_out, cos_t, sin_t, heads):
    t = q.shape[0]
    tq = _attn_tile(t)
    nq = t // tq
    scale = 1.0 / math.sqrt(NOPE_DIM + ROPE_DIM)

    def body(q_ref, kv_ref, kr_ref, do_ref, cos_ref, sin_ref, dq_ref, dkv_ref, dkr_ref, kcat, dk_acc, dv_acc):
        h, i = pl.program_id(0), pl.program_id(1)

        @pl.when(i == 0)
        def _():
            kcat[:, :NOPE_DIM] = kv_ref[:, :NOPE_DIM]
            kcat[:, NOPE_DIM:] = kr_ref[...]
            dk_acc[...] = jnp.zeros_like(dk_acc)
            dv_acc[...] = jnp.zeros_like(dv_acc)

        @pl.when((h == 0) & (i == 0))
        def _():
            dkr_ref[...] = jnp.zeros_like(dkr_ref)

        qb, dob = q_ref[...], do_ref[...]
        s = lax.dot_general(qb, kcat[...], NT, preferred_element_type=F32) * scale
        p = _softmax_rows(s)
        dv_acc[...] += lax.dot_general(p.astype(BF16), dob, TN, preferred_element_type=F32)
        dp = lax.dot_general(dob, kv_ref[:, NOPE_DIM:], NT, preferred_element_type=F32)
        ds = (p * (dp - jnp.sum(dp * p, axis=-1, keepdims=True)) * scale).astype(BF16)
        dq = jnp.dot(ds, kcat[...], preferred_element_type=F32)
        dq_ref[:, :NOPE_DIM] = dq[:, :NOPE_DIM].astype(BF16)
        dq_ref[:, NOPE_DIM:] = _rope_bwd(dq[:, NOPE_DIM:], cos_ref[...], sin_ref[...]).astype(BF16)
        dk_acc[...] += lax.dot_general(ds, qb, TN, preferred_element_type=F32)

        @pl.when(i == nq - 1)
        def _():
            dkv_ref[:, :NOPE_DIM] = dk_acc[:, :NOPE_DIM].astype(BF16)
            dkv_ref[:, NOPE_DIM:] = dv_acc[...].astype(BF16)
            dkr_ref[...] += dk_acc[:, NOPE_DIM:]

    return pl.pallas_call(
        body, name="attn_bwd", grid=(heads, nq),
        in_specs=[pl.BlockSpec((tq, HEAD_PAD), lambda h, i: (i, h)),
                  pl.BlockSpec((t, HEAD_PAD), lambda h, i: (0, h)),
                  pl.BlockSpec((t, LANES), lambda h, i: (0, 0)),
                  pl.BlockSpec((tq, V_DIM), lambda h, i: (i, h)),
                  pl.BlockSpec((tq, LANES), lambda h, i: (i, 0)),
                  pl.BlockSpec((tq, LANES), lambda h, i: (i, 0))],
        out_specs=[pl.BlockSpec((tq, HEAD_PAD), lambda h, i: (i, h)),
                   pl.BlockSpec((t, HEAD_PAD), lambda h, i: (0, h)),
                   pl.BlockSpec((t, LANES), lambda h, i: (0, 0))],
        out_shape=[jax.ShapeDtypeStruct((t, heads * HEAD_PAD), BF16),
                   jax.ShapeDtypeStruct((t, heads * HEAD_PAD), BF16),
                   jax.ShapeDtypeStruct((t, LANES), F32)],
        scratch_shapes=[pltpu.VMEM((t, HEAD_PAD), BF16), pltpu.VMEM((t, HEAD_PAD), F32), pltpu.VMEM((t, V_DIM), F32)],
        compiler_params=_params(("arbitrary", "arbitrary")),
    )(q, kv, kr, d_out, cos_t, sin_t)


def _layer_norm_parts(x):
    mu = jnp.mean(x, axis=-1, keepdims=True)
    xc = x - mu
    r = lax.rsqrt(jnp.mean(xc * xc, axis=-1, keepdims=True) + EPS)
    return xc * r, r


def _gmlp_fwd(proj, ln_g, ln_b, w_s, b_sb, g_out_norm, lay):
    gw = lay["gw"]
    g_heads = gw // G_HEAD_DIM

    def body(u_ref, v_ref, lng_ref, lnb_ref, ws_ref, bs_ref, gn_ref, o_ref, gate_ref):
        gu = _gelu(u_ref[...])
        vh, _ = _layer_norm_parts(_gelu(v_ref[...]))
        vln = (vh * lng_ref[...] + lnb_ref[...]).astype(BF16)
        for g in range(g_heads):
            cols = slice(g * G_HEAD_DIM, (g + 1) * G_HEAD_DIM)
            s = jnp.dot(ws_ref[g], vln[:, cols], preferred_element_type=F32) + bs_ref[g]
            gate_ref[:, cols] = gu[:, cols] * s
        o_ref[...] = _rms_fwd(gate_ref[...], gn_ref[...]).astype(BF16)

    t = proj.shape[0]
    in_specs = [pl.BlockSpec((CHUNK, gw), lambda i: (i, 0)), pl.BlockSpec((CHUNK, gw), lambda i: (i, 1))]
    pars = [ln_g, ln_b, w_s, b_sb, g_out_norm]
    in_specs += [pl.BlockSpec(a.shape, functools.partial(lambda i, nd: (0,) * nd, nd=a.ndim)) for a in pars]
    return pl.pallas_call(
        body, name="gmlp_fwd", grid=(t // CHUNK,), in_specs=in_specs,
        out_specs=pl.BlockSpec((CHUNK, gw), lambda i: (i, 0)),
        out_shape=jax.ShapeDtypeStruct((t, gw), BF16),
        scratch_shapes=[pltpu.VMEM((CHUNK, gw), F32)],
        compiler_params=_params(("parallel",)),
    )(proj, proj, *pars)


def _gmlp_bwd(proj, d_mixed, ln_g, ln_b, w_s, w_st, b_sb, g_out_norm, lay):
    gw = lay["gw"]
    g_heads = gw // G_HEAD_DIM
    aw_blocks = lay["aw"] // gw

    def body(u_ref, v_ref, dm_ref, lng_ref, lnb_ref, ws_ref, wst_ref, bs_ref, gn_ref,
             du_ref, dv_ref, dgn_ref, dlng_ref, dlnb_ref, dws_ref, dbs_ref, gate_ref, s_ref, dvln_ref):
        i = pl.program_id(0)
        u, v = u_ref[...], v_ref[...]
        gu, gv = _gelu(u), _gelu(v)
        vh, r_ln = _layer_norm_parts(gv)
        vln = (vh * lng_ref[...] + lnb_ref[...]).astype(BF16)
        for g in range(g_heads):
            cols = slice(g * G_HEAD_DIM, (g + 1) * G_HEAD_DIM)
            s = jnp.dot(ws_ref[g], vln[:, cols], preferred_element_type=F32) + bs_ref[g]
            s_ref[:, cols] = s
            gate_ref[:, cols] = gu[:, cols] * s
        d_gate, dgn = _rms_bwd(gate_ref[...], gn_ref[...], dm_ref[...])
        _accumulate(dgn_ref, _colsum(dgn))
        du_ref[...] = (d_gate * s_ref[...] * _gelu_grad(u)).astype(BF16)
        d_s = d_gate * gu
        d_sb = d_s.astype(BF16)
        for g in range(g_heads):
            cols = slice(g * G_HEAD_DIM, (g + 1) * G_HEAD_DIM)
            dw = lax.dot_general(d_sb[:, cols], vln[:, cols], NT, preferred_element_type=F32)

            @pl.when(i == 0)
            def _():
                dws_ref[g] = dw
                dbs_ref[g] = d_s[:, cols]

            @pl.when(i > 0)
            def _():
                dws_ref[g] += dw
                dbs_ref[g] += d_s[:, cols]

            dvln_ref[:, cols] = jnp.dot(wst_ref[g], d_sb[:, cols], preferred_element_type=F32)
        d_vln = dvln_ref[...]
        _accumulate(dlng_ref, _colsum(d_vln * vh))
        _accumulate(dlnb_ref, _colsum(d_vln))
        d_vh = d_vln * lng_ref[...]
        d_gv = r_ln * (d_vh - jnp.mean(d_vh, axis=-1, keepdims=True)
                       - vh * jnp.mean(d_vh * vh, axis=-1, keepdims=True))
        dv_ref[...] = (d_gv * _gelu_grad(v)).astype(BF16)

    t = proj.shape[0]
    whole = lambda a: pl.BlockSpec(a.shape, functools.partial(lambda i, nd: (0,) * nd, nd=a.ndim))
    pars = [ln_g, ln_b, w_s, w_st, b_sb, g_out_norm]
    hshape = (g_heads, CHUNK, CHUNK)
    return pl.pallas_call(
        body, name="gmlp_bwd", grid=(t // CHUNK,),
        in_specs=[pl.BlockSpec((CHUNK, gw), lambda i: (i, 0)), pl.BlockSpec((CHUNK, gw), lambda i: (i, 1)),
                  pl.BlockSpec((CHUNK, gw), lambda i: (i, aw_blocks))] + [whole(a) for a in pars],
        out_specs=[pl.BlockSpec((CHUNK, gw), lambda i: (i, 0)), pl.BlockSpec((CHUNK, gw), lambda i: (i, 0)),
                   pl.BlockSpec((1, gw), lambda i: (0, 0)), pl.BlockSpec((1, gw), lambda i: (0, 0)),
                   pl.BlockSpec((1, gw), lambda i: (0, 0)),
                   pl.BlockSpec(hshape, lambda i: (0, 0, 0)), pl.BlockSpec(hshape, lambda i: (0, 0, 0))],
        out_shape=[jax.ShapeDtypeStruct((t, gw), BF16), jax.ShapeDtypeStruct((t, gw), BF16),
                   jax.ShapeDtypeStruct((1, gw), F32), jax.ShapeDtypeStruct((1, gw), F32),
                   jax.ShapeDtypeStruct((1, gw), F32),
                   jax.ShapeDtypeStruct(hshape, F32), jax.ShapeDtypeStruct(hshape, F32)],
        scratch_shapes=[pltpu.VMEM((CHUNK, gw), F32), pltpu.VMEM((CHUNK, gw), F32), pltpu.VMEM((CHUNK, gw), F32)],
        compiler_params=_params(("arbitrary",)),
    )(proj, proj, d_mixed, *pars)


def _spatial_bias_grad(dbs_wide):
    g_heads = dbs_wide.shape[0]

    def body(x_ref, o_ref):
        for g in range(g_heads):
            o_ref[g:g + 1, :] = jnp.sum(x_ref[g].T, axis=0, keepdims=True)

    return pl.pallas_call(
        body, name="spatial_bias_grad", out_shape=jax.ShapeDtypeStruct((g_heads, CHUNK), F32),
        in_specs=[pl.BlockSpec(memory_space=pltpu.VMEM)], out_specs=pl.BlockSpec(memory_space=pltpu.VMEM),
    )(dbs_wide)


def _mix_norm(a_out, gn, g_a, tr):
    aw = a_out.shape[1]
    gw = gn.shape[1]

    def body(a_ref, gn_ref, g_ref, o_ref):
        o_ref[:, :aw] = _rms_fwd(a_ref[...], g_ref[...]).astype(BF16)
        o_ref[:, aw:] = gn_ref[...]

    t = a_out.shape[0]
    return _row_call(body, "mix_norm", t, tr, [(a_out, aw, 0), (gn, gw, 0)], [g_a], [(aw + gw, BF16)], [])[0]


def _mix_norm_bwd(a_out, d_mixed, g_a, tr, token):
    aw = a_out.shape[1]

    def body(a_ref, dm_ref, g_ref, token_ref, da_ref, dg_ref):
        dx, dg = _rms_bwd(a_ref[...], g_ref[...], dm_ref[...])
        da_ref[...] = dx.astype(BF16)
        _accumulate(dg_ref, _colsum(dg))

    t = a_out.shape[0]
    return _row_call(body, "mix_norm_bwd", t, tr, [(a_out, aw, 0), (d_mixed, aw, 0)], [g_a, token],
                     [(aw, BF16)], [((1, aw), F32)])


def _post_mix(x, mix_out, g_pm, g_pf, tr):
    def body(x_ref, mo_ref, gpm_ref, gpf_ref, h_ref, hn_ref):
        h = x_ref[...] + _rms_fwd(mo_ref[...], gpm_ref[...])
        h_ref[...] = h
        hn_ref[...] = _rms_fwd(h, gpf_ref[...]).astype(BF16)

    t, d = x.shape
    return _row_call(body, "post_mix", t, tr, [(x, d, 0), (mix_out, d, 0)], [g_pm, g_pf], [(d, F32), (d, BF16)], [])


def _post_mix_bwd(mix_out, h, dy, d_hn, g_pm, g_pf, tr, token):
    def body(mo_ref, h_ref, dy_ref, dhn_ref, gpm_ref, gpf_ref, token_ref, dh_ref, dmo_ref, dgpf_ref, dgpm_ref):
        dx, dg = _rms_bwd(h_ref[...], gpf_ref[...], dhn_ref[...])
        dh = dy_ref[...] + dx
        dh_ref[...] = dh
        _accumulate(dgpf_ref, _colsum(dg))
        dmo, dg2 = _rms_bwd(mo_ref[...], gpm_ref[...], dh)
        dmo_ref[...] = dmo.astype(BF16)
        _accumulate(dgpm_ref, _colsum(dg2))

    t, d = h.shape
    return _row_call(body, "post_mix_bwd", t, tr, [(mix_out, d, 0), (h, d, 0), (dy, d, 0), (d_hn, d, 0)],
                     [g_pm, g_pf, token], [(d, F32), (d, BF16)], [((1, d), F32), ((1, d), F32)])


def _swiglu(gate, up):
    t, f = gate.shape
    tr, tf = _tile(t, 512), _tile(f, 2048)

    def body(g_ref, u_ref, o_ref):
        g = g_ref[...].astype(F32)
        o_ref[...] = (g * _sigmoid(g) * u_ref[...].astype(F32)).astype(BF16)

    spec = pl.BlockSpec((tr, tf), lambda i, j: (i, j))
    return pl.pallas_call(body, name="swiglu", grid=(t // tr, f // tf), in_specs=[spec, spec], out_specs=spec,
                          out_shape=jax.ShapeDtypeStruct((t, f), BF16),
                          compiler_params=_params(("parallel", "parallel")))(gate, up)


def _swiglu_bwd(gate, up, d_act):
    t, f = gate.shape
    tr, tf = _tile(t, 512), _tile(f, 2048)

    def body(g_ref, u_ref, da_ref, dg_ref, du_ref):
        g, u, da = g_ref[...].astype(F32), u_ref[...].astype(F32), da_ref[...].astype(F32)
        sg = _sigmoid(g)
        du_ref[...] = (da * (g * sg)).astype(BF16)
        dg_ref[...] = (da * u * (sg * (1.0 + g * (1.0 - sg)))).astype(BF16)

    spec = pl.BlockSpec((tr, tf), lambda i, j: (i, j))
    shape = jax.ShapeDtypeStruct((t, f), BF16)
    return pl.pallas_call(body, name="swiglu_bwd", grid=(t // tr, f // tf), in_specs=[spec, spec, spec],
                          out_specs=[spec, spec], out_shape=[shape, shape],
                          compiler_params=_params(("parallel", "parallel")))(gate, up, d_act)


def _loss_head(h, ffn, target, g_po, tr):
    t, d = h.shape

    def body(h_ref, f_ref, t_ref, g_ref, dy_ref, df_ref, dg_ref, loss_ref):
        f = f_ref[...]
        err = h_ref[...] + _rms_fwd(f, g_ref[...]) - t_ref[...]
        dy = err * (1.0 / d)
        dy_ref[...] = dy
        df, dg = _rms_bwd(f, g_ref[...], dy)
        df_ref[...] = df.astype(BF16)
        _accumulate(dg_ref, _colsum(dg))
        sq = jnp.sum(_colsum(err * err), axis=-1, keepdims=True) * (0.5 / d)
        _accumulate(loss_ref, jnp.broadcast_to(sq, (1, LANES)))

    return _row_call(body, "loss_head", t, tr, [(h, d, 0), (ffn, d, 0), (target, d, 0)], [g_po],
                     [(d, F32), (d, BF16)], [((1, d), F32), ((1, LANES), F32)])


def _qkv_bwd(proj, d_qn, d_kvn, d_kr, g_q, g_kv, cos_t, sin_t, lay, tr):
    ql, kl = lay["ql"], lay["kl"]

    def body(q_ref, kv_ref, dqn_ref, dkvn_ref, dkr_ref, cos_ref, sin_ref, gq_ref, gkv_ref,
             dq_ref, dkv_ref, dkt_ref, dgq_ref, dgkv_ref):
        dx, dg = _rms_bwd(q_ref[...], gq_ref[...], dqn_ref[...])
        dq_ref[...] = dx.astype(BF16)
        _accumulate(dgq_ref, _colsum(dg))
        dx, dg = _rms_bwd(kv_ref[...], gkv_ref[...], dkvn_ref[...])
        dkv_ref[...] = dx.astype(BF16)
        _accumulate(dgkv_ref, _colsum(dg))
        dkt_ref[...] = _rope_bwd(dkr_ref[...], cos_ref[...], sin_ref[...]).astype(BF16)

    t = proj.shape[0]
    return _row_call(
        body, "qkv_bwd", t, tr,
        [(proj, ql, lay["q_off"] // ql), (proj, kl, lay["kv_off"] // kl), (d_qn, ql, 0), (d_kvn, kl, 0),
         (d_kr, LANES, 0), (cos_t, LANES, 0), (sin_t, LANES, 0)],
        [g_q, g_kv], [(ql, BF16), (kl, BF16), (LANES, BF16)], [((1, ql), F32), ((1, kl), F32)])


def _prenorm_bwd(x, d_xn, dh, g, tr):
    def body(x_ref, dxn_ref, dh_ref, g_ref, gx_ref, dg_ref):
        dx, dg = _rms_bwd(x_ref[...], g_ref[...], dxn_ref[...])
        gx_ref[...] = dh_ref[...] + dx
        _accumulate(dg_ref, _colsum(dg))

    t, d = x.shape
    return _row_call(body, "prenorm_bwd", t, tr, [(x, d, 0), (d_xn, d, 0), (dh, d, 0)], [g],
                     [(d, F32)], [((1, d), F32)])


def _adam_rows(rows, cols):
    cap = max(8, (256 * 1024) // cols // 8 * 8)
    tr = min(rows, cap)
    while rows % tr:
        tr -= 8
    return tr


def _adamw(w, g, m, v, name):
    rows, cols = w.shape
    tr = _adam_rows(rows, cols)

    def body(w_ref, g_ref, m_ref, v_ref, d_ref, mo_ref, vo_ref):
        g = g_ref[...]
        m2 = ADAM_B1 * m_ref[...] + (1.0 - ADAM_B1) * g
        v2 = ADAM_B2 * v_ref[...] + (1.0 - ADAM_B2) * (g * g)
        m_hat = m2 / (1.0 - ADAM_B1 ** ADAM_STEP)
        v_hat = v2 / (1.0 - ADAM_B2 ** ADAM_STEP)
        d_ref[...] = -ADAM_LR * (m_hat / (jnp.sqrt(v_hat) + ADAM_EPS) + ADAM_WD * w_ref[...])
        mo_ref[...] = m2
        vo_ref[...] = v2

    spec = pl.BlockSpec((tr, cols), lambda i: (i, 0))
    shape = jax.ShapeDtypeStruct((rows, cols), F32)
    return pl.pallas_call(body, name=name, grid=(rows // tr,), in_specs=[spec] * 4, out_specs=[spec] * 3,
                          out_shape=[shape] * 3, compiler_params=_params(("parallel",)))(w, g, m, v)


def _adamw_halves(w, g_mine, g_theirs, m, v, name):
    rows, cols = w.shape
    rh = g_mine.shape[0]
    tr = _adam_rows(math.gcd(rows, rh), cols)
    per_half = rh // tr
    my_c = jnp.reshape(lax.axis_index("c"), (1,)).astype(jnp.int32)

    def body(c_ref, w_ref, gm_ref, gt_ref, m_ref, v_ref, g_ref, d_ref, mo_ref, vo_ref):
        mine = (pl.program_id(0) // per_half) == c_ref[0]
        g = jnp.where(mine, gm_ref[...], gt_ref[...])
        m2 = ADAM_B1 * m_ref[...] + (1.0 - ADAM_B1) * g
        v2 = ADAM_B2 * v_ref[...] + (1.0 - ADAM_B2) * (g * g)
        m_hat = m2 / (1.0 - ADAM_B1 ** ADAM_STEP)
        v_hat = v2 / (1.0 - ADAM_B2 ** ADAM_STEP)
        g_ref[...] = g
        d_ref[...] = -ADAM_LR * (m_hat / (jnp.sqrt(v_hat) + ADAM_EPS) + ADAM_WD * w_ref[...])
        mo_ref[...] = m2
        vo_ref[...] = v2

    def half_spec(is_mine):
        def index(i, c_ref):
            used = ((i // per_half) == c_ref[0]) if is_mine else ((i // per_half) != c_ref[0])
            return (jnp.where(used, i % per_half, 0), 0)
        return pl.BlockSpec((tr, cols), index)

    spec = pl.BlockSpec((tr, cols), lambda i, c_ref: (i, 0))
    shape = jax.ShapeDtypeStruct((rows, cols), F32)
    grid_spec = pltpu.PrefetchScalarGridSpec(
        num_scalar_prefetch=1, grid=(rows // tr,),
        in_specs=[spec, half_spec(True), half_spec(False), spec, spec], out_specs=[spec] * 4)
    return pl.pallas_call(body, name=name, grid_spec=grid_spec, out_shape=[shape] * 4,
                          compiler_params=_params(("parallel",)))(my_c, w, g_mine, g_theirs, m, v)


def _place_own(gathered, shard, name):
    _, _, r, c = gathered.shape
    tr = _adam_rows(r, c)
    own = jnp.reshape(2 * lax.axis_index("x") + lax.axis_index("y"), (1,)).astype(jnp.int32)

    def body(own_ref, g_ref, s_ref, o_ref):
        o_ref[...] = s_ref[...]

    grid_spec = pltpu.PrefetchScalarGridSpec(
        num_scalar_prefetch=1, grid=(2, r // tr),
        in_specs=[pl.BlockSpec(memory_space=pl.ANY), pl.BlockSpec((None, tr, c), lambda a, i, own_ref: (a, i, 0))],
        out_specs=pl.BlockSpec((None, None, tr, c), lambda a, i, own_ref: (own_ref[0], a, i, 0)))
    return pl.pallas_call(body, name=name, grid_spec=grid_spec,
                          out_shape=jax.ShapeDtypeStruct(gathered.shape, gathered.dtype),
                          input_output_aliases={1: 0},
                          compiler_params=_params(("parallel", "parallel")))(own, gathered, shard)


def _pair_add(parts, theirs, name):
    _, n, r, c = parts.shape
    tr = _adam_rows(r, c)
    my_c = jnp.reshape(lax.axis_index("c"), (1,)).astype(jnp.int32)

    def body(c_ref, a_ref, b_ref, o_ref):
        o_ref[0] = (a_ref[0, 0].astype(F32) + b_ref[0].astype(F32)).astype(BF16)

    spec = pl.BlockSpec((1, tr, c), lambda k, i, c_ref: (k, i, 0))
    grid_spec = pltpu.PrefetchScalarGridSpec(
        num_scalar_prefetch=1, grid=(n, r // tr),
        in_specs=[pl.BlockSpec((1, 1, tr, c), lambda k, i, c_ref: (c_ref[0], k, i, 0)), spec], out_specs=spec)
    return pl.pallas_call(body, name=name, grid_spec=grid_spec, out_shape=jax.ShapeDtypeStruct((n, r, c), BF16),
                          compiler_params=_params(("parallel", "parallel")))(my_c, parts, theirs)


def _chip_sum(pair_sums, received, name, token=None):
    _, r, c = pair_sums.shape
    tr = _adam_rows(r, c)
    own = 2 * lax.axis_index("x") + lax.axis_index("y")

    def body(own_ref, p_ref, r0_ref, r1_ref, r2_ref, *rest):
        o_ref = rest[-1]
        acc = p_ref[0].astype(F32) + r0_ref[0].astype(F32)
        acc = acc + r1_ref[0].astype(F32)
        o_ref[...] = acc + r2_ref[0].astype(F32)

    def rspec(j):
        return pl.BlockSpec((1, tr, c), functools.partial(lambda i, own_ref, j: (j, i, 0), j=j))

    extra = [] if token is None else [token]
    grid_spec = pltpu.PrefetchScalarGridSpec(
        num_scalar_prefetch=1, grid=(r // tr,),
        in_specs=[pl.BlockSpec((1, tr, c), lambda i, own_ref: (own_ref[0], i, 0)), rspec(0), rspec(1), rspec(2)]
        + [pl.BlockSpec(memory_space=pl.ANY)] * len(extra),
        out_specs=pl.BlockSpec((tr, c), lambda i, own_ref: (i, 0)))
    return pl.pallas_call(body, name=name, grid_spec=grid_spec, out_shape=jax.ShapeDtypeStruct((r, c), F32),
                          compiler_params=_params(("parallel",)))(
        jnp.reshape(own, (1,)).astype(jnp.int32), pair_sums, received, received, received, *extra)


def _mesh_place():
    x, y, c = lax.axis_index("x"), lax.axis_index("y"), lax.axis_index("c")
    other_chips = [(1 - x, y), (x, 1 - y), (1 - x, 1 - y)]
    return x, y, c, other_chips


def _hbm_specs(n):
    return [pl.BlockSpec(memory_space=pltpu.HBM)] * n


def _all_gather(shards):
    n = len(shards)

    def body(*refs):
        ins, outs = refs[:n], refs[n:2 * n]
        send_sems, recv_sems = refs[2 * n:]
        x, y, c, other_chips = _mesh_place()
        own = 2 * x + y
        sibling = (x, y, 1 - c)

        def remote(src, dst, k, to):
            return pltpu.make_async_remote_copy(src_ref=src, dst_ref=dst, send_sem=send_sems.at[k],
                                                recv_sem=recv_sems.at[k], device_id=to, device_id_type=MESH)

        first = [remote(ins[w].at[c], outs[w].at[own, c], 6 * w + j, (*chip, c))
                 for w in range(n) for j, chip in enumerate(other_chips)]
        for cp in first:
            cp.start()
        passed = []
        for w in range(n):
            for j, chip in enumerate(other_chips):
                rows = outs[w].at[2 * chip[0] + chip[1], c]
                remote(rows, rows, 6 * w + j, (*chip, c)).wait_recv()
                fwd = remote(rows, rows, 6 * w + 3 + j, sibling)
                fwd.start()
                passed.append(fwd)
        for w in range(n):
            for j, chip in enumerate(other_chips):
                rows = outs[w].at[2 * chip[0] + chip[1], 1 - c]
                remote(rows, rows, 6 * w + 3 + j, sibling).wait_recv()
        for cp in first + passed:
            cp.wait_send()

    return pl.pallas_call(
        body, name="weights_all_gather",
        out_shape=[jax.ShapeDtypeStruct((4,) + s.shape, s.dtype) for s in shards],
        in_specs=_hbm_specs(n), out_specs=_hbm_specs(n),
        scratch_shapes=[pltpu.SemaphoreType.DMA((6 * n,)), pltpu.SemaphoreType.DMA((6 * n,))],
    )(*shards)


def _sibling_exchange(parts, name):
    n = len(parts)

    def body(*refs):
        ins, outs = refs[:n], refs[n:2 * n]
        send_sems, recv_sems = refs[2 * n:]
        x, y, c, _ = _mesh_place()
        copies = [pltpu.make_async_remote_copy(src_ref=ins[w].at[1 - c], dst_ref=outs[w], send_sem=send_sems.at[w],
                                               recv_sem=recv_sems.at[w], device_id=(x, y, 1 - c), device_id_type=MESH)
                  for w in range(n)]
        for cp in copies:
            cp.start()
        for cp in copies:
            cp.wait()

    return pl.pallas_call(
        body, name=name,
        out_shape=[jax.ShapeDtypeStruct(p.shape[1:], p.dtype) for p in parts],
        in_specs=_hbm_specs(n), out_specs=_hbm_specs(n),
        scratch_shapes=[pltpu.SemaphoreType.DMA((n,)), pltpu.SemaphoreType.DMA((n,))],
    )(*parts)


SEM_SPEC = pl.BlockSpec(memory_space=pltpu.SEMAPHORE)
DATAFLOW_EFFECT = pltpu.SideEffectType.DATAFLOW_SIDE_EFFECTING


def _ici_copy(kind, src_refs, land_refs, send_sems, recv_sems, w, j, chip, c, own, arriving):
    theirs = 2 * chip[0] + chip[1]
    if kind == "gather":
        src, dst = src_refs[w].at[c], land_refs[w].at[theirs if arriving else own, c]
    else:
        src, dst = src_refs[w].at[theirs], land_refs[w].at[j]
    return pltpu.make_async_remote_copy(src_ref=src, dst_ref=dst, send_sem=send_sems.at[3 * w + j],
                                        recv_sem=recv_sems.at[3 * w + j], device_id=(*chip, c), device_id_type=MESH)


def _ici_start(kind, srcs, name):
    n = len(srcs)
    if kind == "gather":
        lands = [lax.empty((4,) + s.shape, s.dtype) for s in srcs]
    else:
        lands = [lax.empty((3,) + s.shape[1:], s.dtype) for s in srcs]

    def body(*refs):
        src_refs, land_refs = refs[:n], refs[n:2 * n]
        send_sems, recv_sems = refs[2 * n], refs[2 * n + 1]
        token = refs[-1]
        x, y, c, other_chips = _mesh_place()
        for w in range(n):
            for j, chip in enumerate(other_chips):
                _ici_copy(kind, src_refs, land_refs, send_sems, recv_sems, w, j, chip, c, 2 * x + y, False).start()
        token[...] = jnp.zeros_like(token)

    hbm = lambda a: pltpu.with_memory_space_constraint(a, pltpu.HBM)
    out = pl.pallas_call(
        body, name=name,
        out_shape=(pltpu.SemaphoreType.DMA((3 * n,)), pltpu.SemaphoreType.DMA((3 * n,)),
                   *[pltpu.HBM(a.shape, a.dtype) for a in srcs + lands], jax.ShapeDtypeStruct((8, LANES), F32)),
        in_specs=_hbm_specs(2 * n),
        out_specs=(SEM_SPEC, SEM_SPEC, *_hbm_specs(2 * n), pl.BlockSpec(memory_space=pltpu.VMEM)),
        input_output_aliases={i: 2 + i for i in range(2 * n)},
        compiler_params=pltpu.CompilerParams(has_side_effects=DATAFLOW_EFFECT),
    )(*[hbm(a) for a in srcs + lands])
    return out[0], out[1], list(out[2:2 + n]), list(out[2 + n:2 + 2 * n]), out[-1]


def _ici_wait(kind, send_sems, recv_sems, srcs, lands, after, name):
    n = len(srcs)

    def body(*refs):
        src_refs, land_refs = refs[:n], refs[n:2 * n]
        send_ref, recv_ref = refs[2 * n], refs[2 * n + 1]
        x, y, c, other_chips = _mesh_place()
        for w in range(n):
            for j, chip in enumerate(other_chips):
                cp = _ici_copy(kind, src_refs, land_refs, send_ref, recv_ref, w, j, chip, c, 2 * x + y, True)
                cp.wait_send()
                cp.wait_recv()

    out = pl.pallas_call(
        body, name=name, out_shape=tuple(pltpu.HBM(a.shape, a.dtype) for a in srcs + lands),
        in_specs=_hbm_specs(2 * n) + [SEM_SPEC, SEM_SPEC, pl.BlockSpec(memory_space=pl.ANY)],
        out_specs=tuple(_hbm_specs(2 * n)), input_output_aliases={i: i for i in range(2 * n)},
        compiler_params=pltpu.CompilerParams(has_side_effects=DATAFLOW_EFFECT),
    )(*srcs, *lands, send_sems, recv_sems, after)
    return list(out[:n]), list(out[n:])


def _forward_halves(lands, name):
    n = len(lands)

    def body(*refs):
        ins, outs = refs[:n], refs[n:2 * n]
        send_sems, recv_sems = refs[2 * n:]
        x, y, c, other_chips = _mesh_place()
        copies = []
        for w in range(n):
            for j, chip in enumerate(other_chips):
                k = 2 * chip[0] + chip[1]
                cp = pltpu.make_async_remote_copy(
                    src_ref=ins[w].at[k, c], dst_ref=outs[w].at[k, c], send_sem=send_sems.at[3 * w + j],
                    recv_sem=recv_sems.at[3 * w + j], device_id=(x, y, 1 - c), device_id_type=MESH)
                cp.start()
                copies.append(cp)
        for w in range(n):
            for j, chip in enumerate(other_chips):
                k = 2 * chip[0] + chip[1]
                pltpu.make_async_remote_copy(
                    src_ref=ins[w].at[k, c], dst_ref=outs[w].at[k, 1 - c], send_sem=send_sems.at[3 * w + j],
                    recv_sem=recv_sems.at[3 * w + j], device_id=(x, y, 1 - c), device_id_type=MESH).wait_recv()
        for cp in copies:
            cp.wait_send()

    return pl.pallas_call(
        body, name=name, out_shape=[jax.ShapeDtypeStruct(a.shape, a.dtype) for a in lands],
        in_specs=_hbm_specs(n), out_specs=_hbm_specs(n), input_output_aliases={i: i for i in range(n)},
        scratch_shapes=[pltpu.SemaphoreType.DMA((3 * n,)), pltpu.SemaphoreType.DMA((3 * n,))],
    )(*lands)


def _halves_exchange(halves, name):
    n = len(halves)

    def body(*refs):
        ins, outs = refs[:n], refs[n:2 * n]
        send_sems, recv_sems = refs[2 * n:]
        x, y, c, _ = _mesh_place()
        copies = [pltpu.make_async_remote_copy(src_ref=ins[w], dst_ref=outs[w], send_sem=send_sems.at[w],
                                               recv_sem=recv_sems.at[w], device_id=(x, y, 1 - c), device_id_type=MESH)
                  for w in range(n)]
        for cp in copies:
            cp.start()
        for cp in copies:
            cp.wait()

    return pl.pallas_call(
        body, name=name,
        out_shape=[jax.ShapeDtypeStruct(h.shape, h.dtype) for h in halves],
        in_specs=_hbm_specs(n), out_specs=_hbm_specs(n),
        scratch_shapes=[pltpu.SemaphoreType.DMA((n,)), pltpu.SemaphoreType.DMA((n,))],
    )(*halves)


def _small_all_reduce(packed):
    rows = packed.shape[0]

    def body(in_ref, out_ref, gathered, send_sems, recv_sems):
        x, y, c, _ = _mesh_place()
        me = 4 * x + 2 * y + c
        gathered[0] = in_ref[...]
        copies = []
        for rel in range(1, 8):
            to = (x ^ (rel >> 2), y ^ ((rel >> 1) & 1), c ^ (rel & 1))
            cp = pltpu.make_async_remote_copy(src_ref=in_ref, dst_ref=gathered.at[rel], send_sem=send_sems.at[rel - 1],
                                              recv_sem=recv_sems.at[rel - 1], device_id=to, device_id_type=MESH)
            cp.start()
            copies.append(cp)
        for cp in copies:
            cp.wait()
        acc = gathered[me]
        for dev in range(1, 8):
            acc = acc + gathered[dev ^ me]
        out_ref[...] = acc

    return pl.pallas_call(
        body, name="small_all_reduce", out_shape=jax.ShapeDtypeStruct(packed.shape, F32),
        in_specs=[pl.BlockSpec(memory_space=pltpu.VMEM)], out_specs=pl.BlockSpec(memory_space=pltpu.VMEM),
        scratch_shapes=[pltpu.VMEM((8, rows, LANES), F32), pltpu.SemaphoreType.DMA((7,)), pltpu.SemaphoreType.DMA((7,))],
        compiler_params=pltpu.CompilerParams(vmem_limit_bytes=VMEM_LIMIT_BYTES),
    )(packed)


def _layout(w_in, w_uq, w_ukv, v_ln_gain, q_norm, kv_norm):
    heads = 4 * w_uq.shape[-1] // (NOPE_DIM + ROPE_DIM)
    gw = v_ln_gain.shape[-1]
    ql, kl = q_norm.shape[-1], kv_norm.shape[-1]
    lay = dict(heads=heads, gw=gw, ql=ql, kl=kl, aw=heads * V_DIM, u_off=0, v_off=gw, q_off=2 * gw,
               kv_off=2 * gw + ql, kr_off=2 * gw + ql + kl)
    lay["in_pad"] = _round_up(lay["kr_off"] + LANES, 2 * LANES if lay["kr_off"] + LANES <= 2048 else 1024)
    assert lay["q_off"] % ql == 0 and lay["kv_off"] % kl == 0 and lay["aw"] % gw == 0
    assert 4 * w_in.shape[-1] == ql + kl + ROPE_DIM + 2 * gw
    return lay


def _rope_tile(t1, t2, axis=-1):
    z = jnp.zeros_like(t1)
    return jnp.concatenate([t1, z, t2, z], axis=axis)


def _w_in_rows(gathered, shard_rows, lay):
    d = gathered.shape[-1]
    wt = gathered[:, :shard_rows].reshape(4 * shard_rows, d)
    ql, kl, gw = lay["ql"], lay["kl"], lay["gw"]
    q_c, kv_c = wt[:ql], wt[ql:ql + kl]
    r = wt[ql + kl:ql + kl + ROPE_DIM]
    u = wt[ql + kl + ROPE_DIM:ql + kl + ROPE_DIM + gw]
    v = wt[ql + kl + ROPE_DIM + gw:]
    parts = [u, v, q_c, kv_c, _rope_tile(r[:ROPE_HALF], r[ROPE_HALF:], axis=0)]
    pad = lay["in_pad"] - (lay["kr_off"] + LANES)
    if pad:
        parts.append(jnp.zeros((pad, d), wt.dtype))
    return jnp.concatenate(parts, axis=0)


def _w_in_grad_chunks(dwt, shard_rows, padded_rows, lay):
    d = dwt.shape[-1]
    ql, kl, gw = lay["ql"], lay["kl"], lay["gw"]
    ko = lay["kr_off"]
    rows = jnp.concatenate([dwt[lay["q_off"]:lay["q_off"] + ql], dwt[lay["kv_off"]:lay["kv_off"] + kl],
                            dwt[ko:ko + ROPE_HALF], dwt[ko + 2 * ROPE_HALF:ko + 3 * ROPE_HALF],
                            dwt[:gw], dwt[gw:2 * gw]], axis=0).reshape(4, shard_rows, d)
    rows = jnp.pad(rows, ((0, 0), (0, padded_rows - shard_rows), (0, 0)))
    return jnp.transpose(rows.reshape(4, 2, padded_rows // 2, d), (1, 0, 2, 3)).astype(BF16)


def _w_uq_padded(w, heads):
    w3 = w.reshape(w.shape[0], heads, NOPE_DIM + ROPE_DIM)
    t = _rope_tile(w3[..., NOPE_DIM:NOPE_DIM + ROPE_HALF], w3[..., NOPE_DIM + ROPE_HALF:])
    return jnp.concatenate([w3[..., :NOPE_DIM], t], axis=-1).reshape(w.shape[0], heads * HEAD_PAD)


def _w_uq_grad_unpadded(dw, heads):
    d3 = dw.reshape(dw.shape[0], heads, HEAD_PAD)
    return jnp.concatenate([d3[..., :NOPE_DIM], d3[..., NOPE_DIM:NOPE_DIM + ROPE_HALF],
                            d3[..., NOPE_DIM + 2 * ROPE_HALF:NOPE_DIM + 3 * ROPE_HALF]],
                           axis=-1).reshape(dw.shape[0], heads * (NOPE_DIM + ROPE_DIM))


def _cols_gathered(g):
    return jnp.transpose(g, (1, 0, 2)).reshape(g.shape[1], 4 * g.shape[2])


def _chunks_of_cols(grad):
    r, c4 = grad.shape
    return jnp.transpose(grad.reshape(2, r // 2, 4, c4 // 4), (0, 2, 1, 3)).astype(BF16)


SMALL = ["pre_mix_norm", "q_norm", "kv_norm", "v_ln_gain", "v_ln_bias", "w_spatial", "b_spatial", "attn_out_norm",
         "gmlp_out_norm", "post_mix_norm", "pre_ffn_norm", "post_ffn_norm"]
BIG = ["w_in", "w_uq", "w_ukv", "w_out", "w_gate", "w_up", "w_down"]
GATHER_NOW = ["w_in", "w_uq", "w_ukv"]
GATHER_LATER_1 = ["w_out", "w_gate"]
GATHER_LATER_2 = ["w_up", "w_down"]
REDUCE_FFN = ["w_gate", "w_up", "w_down"]
REDUCE_OUT = ["w_out"]
REDUCE_LAST = ["w_in", "w_uq", "w_ukv"]
TRANSPOSED = ("w_in", "w_gate", "w_up")
ORDER = ["pre_mix_norm", "w_in", "q_norm", "kv_norm", "w_uq", "w_ukv", "v_ln_gain", "v_ln_bias", "w_spatial",
         "b_spatial", "attn_out_norm", "gmlp_out_norm", "w_out", "post_mix_norm", "pre_ffn_norm", "w_gate", "w_up",
         "w_down", "post_ffn_norm"]


def _pack(arrays):
    flat = jnp.concatenate([a.reshape(-1) for a in arrays])
    n = flat.shape[0]
    total = _round_up(n, 8 * LANES)
    if total > n:
        flat = jnp.concatenate([flat, jnp.zeros((total - n,), F32)])
    return flat.reshape(total // LANES, LANES)


def _unpack(packed, like):
    flat = packed.reshape(-1)
    out, off = [], 0
    for a in like:
        out.append(flat[off:off + a.size].reshape(a.shape))
        off += a.size
    return out


def kernel(x, positions, pre_mix_norm, w_in, q_norm, kv_norm, w_uq, w_ukv, v_ln_gain, v_ln_bias, w_spatial, b_spatial, attn_out_norm, gmlp_out_norm, w_out, post_mix_norm, pre_ffn_norm, w_gate, w_up, w_down, post_ffn_norm, loss_target, m_pre_mix_norm, m_w_in, m_q_norm, m_kv_norm, m_w_uq, m_w_ukv, m_v_ln_gain, m_v_ln_bias, m_w_spatial, m_b_spatial, m_attn_out_norm, m_gmlp_out_norm, m_w_out, m_post_mix_norm, m_pre_ffn_norm, m_w_gate, m_w_up, m_w_down, m_post_ffn_norm, v_pre_mix_norm, v_w_in, v_q_norm, v_kv_norm, v_w_uq, v_w_ukv, v_v_ln_gain, v_v_ln_bias, v_w_spatial, v_b_spatial, v_attn_out_norm, v_gmlp_out_norm, v_w_out, v_post_mix_norm, v_pre_ffn_norm, v_w_gate, v_w_up, v_w_down, v_post_ffn_norm):
    args = dict(locals())
    weights = {n: args[n] for n in ORDER}
    m_in = {n: args["m_" + n] for n in ORDER}
    v_in = {n: args["v_" + n] for n in ORDER}

    lay = _layout(w_in, w_uq, w_ukv, v_ln_gain, q_norm, kv_norm)
    heads, gw = lay["heads"], lay["gw"]
    t, d = x.shape[1], x.shape[2]
    tr = 128 if t % 128 == 0 else t
    xs = x.reshape(t, d)
    target = loss_target.reshape(t, d)

    ffs, ins = w_gate.shape[-1], w_in.shape[-1]
    ffp, inp = _round_up(ffs, LANES), _round_up(ins, LANES)
    shards = {n: (jnp.swapaxes(weights[n][0], 0, 1) if n in TRANSPOSED else weights[n][0]).astype(BF16)
              for n in BIG}
    for n, rows in (("w_gate", ffp), ("w_up", ffp), ("w_down", ffp), ("w_in", inp)):
        shards[n] = jnp.pad(shards[n], ((0, rows - shards[n].shape[0]), (0, 0)))
    halved = {n: shards[n].reshape(2, shards[n].shape[0] // 2, shards[n].shape[1]) for n in BIG}
    full = {}

    def pair_sums_of(partial, names, tag):
        from_sibling = _sibling_exchange([partial[n] for n in names], "grads_sibling_exchange_" + tag)
        return [_pair_add(partial[n], r, "pair_add_" + n) for n, r in zip(names, from_sibling)]

    def place(names, lands, srcs):
        for n, g, s in zip(names, lands, srcs):
            full[n] = _place_own(g, s, "place_own_" + n).reshape((4,) + shards[n].shape)

    place(GATHER_NOW, _all_gather([halved[n] for n in GATHER_NOW]), [halved[n] for n in GATHER_NOW])
    flight_1 = _ici_start("gather", [halved[n] for n in GATHER_LATER_1], "gather_start_1")
    flight_2 = _ici_start("gather", [halved[n] for n in GATHER_LATER_2], "gather_start_2")
    wt_in = _w_in_rows(full["w_in"], ins, lay)
    wb_uq = _w_uq_padded(_cols_gathered(full["w_uq"]), heads)
    wb_ukv = _cols_gathered(full["w_ukv"])

    inv_freq = 1.0 / (ROPE_THETA ** (jnp.arange(0, ROPE_DIM, 2, dtype=F32) / ROPE_DIM))
    ang = positions.reshape(t).astype(F32)[:, None] * inv_freq
    cos, sin = jnp.cos(ang), jnp.sin(ang)
    cos_t = _rope_tile(cos, cos)
    sin_t = _rope_tile(-sin, sin)

    row = lambda a: a.reshape(1, -1)
    g_pre, g_q, g_kv = row(pre_mix_norm), row(q_norm), row(kv_norm)
    g_a, g_g, g_pm = row(attn_out_norm), row(gmlp_out_norm), row(post_mix_norm)
    g_pf, g_po = row(pre_ffn_norm), row(post_ffn_norm)
    ln_g, ln_b = row(v_ln_gain), row(v_ln_bias)
    ws = w_spatial[0].astype(BF16)
    ws_t = jnp.transpose(ws, (0, 2, 1))
    bs_wide = jnp.broadcast_to(b_spatial[0][:, :, None], b_spatial.shape[1:] + (G_HEAD_DIM,))

    xn = _prenorm(xs, g_pre, tr, flight_1[4] + flight_2[4])
    proj = _matmul(xn, wt_in, NT, F32, "proj")
    qn, kvn, kr = _qkv_prep(proj, g_q, g_kv, cos_t, sin_t, lay, tr)
    q = _q_rope(_matmul(qn, wb_uq, NN, F32, "q_up"), cos_t, sin_t, heads, tr)
    kv = _matmul(kvn, wb_ukv, NN, BF16, "kv_up")
    a_out = _attn_fwd(q, kv, kr, heads)
    gn = _gmlp_fwd(proj, ln_g, ln_b, ws, bs_wide, g_g, lay)
    mixed = _mix_norm(a_out, gn, g_a, tr)
    srcs, lands = _ici_wait("gather", *flight_1[:4], mixed, "gather_wait_1")
    place(GATHER_LATER_1, _forward_halves(lands, "gather_forward_1"), srcs)
    wb_out = full["w_out"].reshape(-1, d)
    wt_gate = full["w_gate"].reshape(4 * ffp, d)
    mix_out = _matmul(mixed, wb_out, NN, F32, "mix_out")
    h, hn = _post_mix(xs, mix_out, g_pm, g_pf, tr)
    gate = _matmul(hn, wt_gate, NT, BF16, "ffn_gate")
    srcs, lands = _ici_wait("gather", *flight_2[:4], gate, "gather_wait_2")
    place(GATHER_LATER_2, _forward_halves(lands, "gather_forward_2"), srcs)
    wt_up = full["w_up"].reshape(4 * ffp, d)
    wb_down = full["w_down"].reshape(4 * ffp, d)
    up = _matmul(hn, wt_up, NT, BF16, "ffn_up")
    act = _swiglu(gate, up)
    ffn = _matmul(act, wb_down, NN, F32, "ffn_down")
    dy, d_ffn, dg_po, loss_vec = _loss_head(h, ffn, target, g_po, tr)

    d_act = _matmul(d_ffn, wb_down, NT, BF16, "d_act")
    gw_down = _matmul(act, d_ffn, TN, BF16, "gw_down", out_chunks=True)
    d_gate, d_up = _swiglu_bwd(gate, up, d_act)
    d_hn = _matmul(d_up, wt_up, NN, F32, "d_hn", addend=_matmul(d_gate, wt_gate, NN, F32, "d_hn_gate"))
    gw_gate = _matmul(d_gate, hn, TN, BF16, "gw_gate", out_chunks=True)
    gw_up = _matmul(d_up, hn, TN, BF16, "gw_up", out_chunks=True)
    pair_ffn = pair_sums_of({"w_gate": gw_gate, "w_up": gw_up, "w_down": gw_down}, REDUCE_FFN, "ffn")
    flight_ffn = _ici_start("scatter", pair_ffn, "scatter_start_ffn")
    dh, d_mo, dg_pf, dg_pm = _post_mix_bwd(mix_out, h, dy, d_hn, g_pm, g_pf, tr, flight_ffn[4])
    d_mixed = _matmul(d_mo, wb_out, NT, F32, "d_mixed")
    gw_out = _matmul(mixed, d_mo, TN, BF16, "gw_out", out_chunks=True)
    pair_out = pair_sums_of({"w_out": gw_out}, REDUCE_OUT, "out")
    flight_out = _ici_start("scatter", pair_out, "scatter_start_out")
    d_a, dg_a = _mix_norm_bwd(a_out, d_mixed, g_a, tr, flight_out[4])
    d_u, d_v, dg_g, d_ln_g, d_ln_b, d_ws, d_bs_wide = _gmlp_bwd(proj, d_mixed, ln_g, ln_b, ws, ws_t, bs_wide, g_g, lay)
    d_bs = _spatial_bias_grad(d_bs_wide)
    d_q, d_kv, d_kr = _attn_bwd(q, kv, kr, d_a, cos_t, sin_t, heads)
    d_qn = _matmul(d_q, wb_uq, NT, F32, "d_qn")
    gw_uq = _matmul(qn, d_q, TN, F32, "gw_uq")
    d_kvn = _matmul(d_kv, wb_ukv, NT, F32, "d_kvn")
    gw_ukv = _matmul(kvn, d_kv, TN, F32, "gw_ukv")
    d_qc, d_kvc, d_krt, dg_q, dg_kv = _qkv_bwd(proj, d_qn, d_kvn, d_kr, g_q, g_kv, cos_t, sin_t, lay, tr)
    parts = [d_u, d_v, d_qc, d_kvc, d_krt]
    pad = lay["in_pad"] - (lay["kr_off"] + LANES)
    if pad:
        parts.append(jnp.zeros((t, pad), BF16))
    d_proj = jnp.concatenate(parts, axis=1)
    d_xn = _matmul(d_proj, wt_in, NN, F32, "d_xn")
    gw_in = _matmul(d_proj, xn, TN, F32, "gw_in")
    grad_x, dg_pre = _prenorm_bwd(xs, d_xn, dh, g_pre, tr)

    pair_mix = pair_sums_of({"w_in": _w_in_grad_chunks(gw_in, ins, inp, lay),
                             "w_uq": _chunks_of_cols(_w_uq_grad_unpadded(gw_uq, heads)),
                             "w_ukv": _chunks_of_cols(gw_ukv)}, REDUCE_LAST, "mix")
    pair_ffn, received_ffn = _ici_wait("scatter", *flight_ffn[:4], pair_mix[-1], "scatter_wait_ffn")
    pair_out, received_out = _ici_wait("scatter", *flight_out[:4], pair_mix[-1], "scatter_wait_out")
    flight_mix = _ici_start("scatter", pair_mix, "scatter_start_mix")
    grads, delta, new_m, new_v = {}, {}, {}, {}

    def finish(names, pair_sums, received, tag, token):
        mine = [_chip_sum(p, r, "chip_sum_" + n, token) for n, p, r in zip(names, pair_sums, received)]
        theirs = _halves_exchange(mine, "grads_halves_exchange_" + tag)
        for n, g_mine, g_theirs in zip(names, mine, theirs):
            shape = weights[n].shape
            if n in TRANSPOSED:
                view = lambda a: jnp.swapaxes(a[0], 0, 1)
                back = lambda o: jnp.swapaxes(o, 0, 1).reshape(shape)
            else:
                view = lambda a: a[0]
                back = lambda o: o.reshape(shape)
            out = _adamw_halves(view(weights[n]), g_mine, g_theirs, view(m_in[n]), view(v_in[n]), "adamw_" + n)
            grads[n], delta[n], new_m[n], new_v[n] = [back(o) for o in out]

    finish(REDUCE_FFN + REDUCE_OUT, pair_ffn + pair_out, received_ffn + received_out, "ffn", flight_mix[4])
    pair_mix, received = _ici_wait("scatter", *flight_mix[:4], new_v[REDUCE_OUT[-1]], "scatter_wait_mix")
    finish(REDUCE_LAST, pair_mix, received, "mix", None)

    small_grads = {"pre_mix_norm": dg_pre, "q_norm": dg_q, "kv_norm": dg_kv, "v_ln_gain": d_ln_g, "v_ln_bias": d_ln_b,
                   "w_spatial": d_ws, "b_spatial": d_bs, "attn_out_norm": dg_a, "gmlp_out_norm": dg_g,
                   "post_mix_norm": dg_pm, "pre_ffn_norm": dg_pf, "post_ffn_norm": dg_po}
    like = [weights[n] for n in SMALL]
    reduced = _small_all_reduce(_pack([small_grads[n] for n in SMALL] + [loss_vec]))
    loss = reduced.reshape(-1)[sum(a.size for a in like)]
    small_g = _pack(_unpack(reduced, like))
    s_delta, s_m, s_v = _adamw(_pack(like), small_g, _pack([m_in[n] for n in SMALL]),
                               _pack([v_in[n] for n in SMALL]), "adamw_small")
    for n, g in zip(SMALL, _unpack(small_g, like)):
        grads[n] = g
    delta.update(zip(SMALL, _unpack(s_delta, like)))
    new_m.update(zip(SMALL, _unpack(s_m, like)))
    new_v.update(zip(SMALL, _unpack(s_v, like)))

    return (loss, grad_x.reshape(x.shape), *[grads[n] for n in ORDER], *[delta[n] for n in ORDER],
            *[new_m[n] for n in ORDER], *[new_v[n] for n in ORDER])
```

```python
import functools
import math

import jax
import jax.numpy as jnp
from jax import lax
from jax.experimental import pallas as pl
from jax.experimental.pallas import tpu as pltpu

F32 = jnp.float32
BF16 = jnp.bfloat16
MESH = pl.DeviceIdType.MESH

NOPE_DIM = 128
ROPE_DIM = 64
ROPE_HALF = ROPE_DIM // 2
V_DIM = 128
HEAD_PAD = 256
G_HEAD_DIM = 128
CHUNK = 128
ROPE_THETA = 10000.0
EPS = 1e-6
ADAM_LR = 0.001
ADAM_B1 = 0.9
ADAM_B2 = 0.999
ADAM_EPS = 1e-08
ADAM_WD = 0.01
ADAM_STEP = 10

LANES = 128
MATMUL_TILE = 1024
WIDE_TILE = 1408
VMEM_LIMIT_BYTES = 48 * 1024 * 1024

NN = (((1,), (0,)), ((), ()))
NT = (((1,), (1,)), ((), ()))
TN = (((0,), (0,)), ((), ()))


def _params(semantics):
    return pltpu.CompilerParams(dimension_semantics=semantics, vmem_limit_bytes=VMEM_LIMIT_BYTES)


def _tile(n, cap=MATMUL_TILE):
    if n <= cap:
        return n
    if cap == MATMUL_TILE and n % WIDE_TILE == 0:
        return WIDE_TILE
    t = cap - cap % LANES
    while n % t:
        t -= LANES
    assert t > 0, n
    return t


def _round_up(n, m):
    return (n + m - 1) // m * m


def _matmul(a, b, dims, out_dtype, name, extras=(), epilogue=None, out_chunks=None):
    if dims is NN:
        (m, k), (k2, n) = a.shape, b.shape
    elif dims is NT:
        (m, k), (n, k2) = a.shape, b.shape
    else:
        (k, m), (k2, n) = a.shape, b.shape
    assert k == k2, (a.shape, b.shape, name)
    tm, tn, tk = _tile(m // 8 if out_chunks else m), _tile(n), _tile(k)
    if len(extras) + (len(out_dtype) if isinstance(out_dtype, tuple) else 1) > 2:
        tm = _tile(m, MATMUL_TILE // 2)
    nk = k // tk

    out_dtypes = out_dtype if isinstance(out_dtype, tuple) else (out_dtype,)
    n_extra = len(extras)

    def body(*refs):
        a_ref, b_ref = refs[:2]
        extra_refs = refs[2:2 + n_extra]
        out_refs = refs[2 + n_extra:-1]
        acc_ref = refs[-1]
        kk = pl.program_id(2)

        @pl.when(kk == 0)
        def _():
            acc_ref[...] = jnp.zeros_like(acc_ref)

        acc_ref[...] += lax.dot_general(a_ref[...], b_ref[...], dims, preferred_element_type=F32)

        @pl.when(kk == nk - 1)
        def _():
            r = acc_ref[...]
            if epilogue is not None:
                r = epilogue(r, *[e[...] for e in extra_refs])
            for o_ref, val in zip(out_refs, r if isinstance(r, tuple) else (r,)):
                o_ref[...] = val.astype(o_ref.dtype)

    if dims is TN:
        a_spec = pl.BlockSpec((tk, tm), lambda i, j, kk: (kk, i))
    else:
        a_spec = pl.BlockSpec((tm, tk), lambda i, j, kk: (i, kk))
    if dims is NT:
        b_spec = pl.BlockSpec((tn, tk), lambda i, j, kk: (j, kk))
    else:
        b_spec = pl.BlockSpec((tk, tn), lambda i, j, kk: (kk, j))
    if not out_chunks:
        o_spec = pl.BlockSpec((tm, tn), lambda i, j, kk: (i, j))
        o_shape = (m, n)
    else:
        pi = m // 8 // tm
        o_spec = pl.BlockSpec((None, None, tm, tn), lambda i, j, kk: ((i // pi) % 2, i // (2 * pi), i % pi, j))
        o_shape = (2, 4, m // 8, n)
    assert not (extras and out_chunks)
    out = pl.pallas_call(
        body, name=name, grid=(m // tm, n // tn, nk), in_specs=[a_spec, b_spec] + [o_spec] * n_extra,
        out_specs=[o_spec] * len(out_dtypes),
        out_shape=[jax.ShapeDtypeStruct(o_shape, dt) for dt in out_dtypes],
        scratch_shapes=[pltpu.VMEM((tm, tn), F32)],
        compiler_params=_params(("parallel", "parallel", "arbitrary")),
    )(a, b, *extras)
    return tuple(out) if isinstance(out_dtype, tuple) else out[0]


def _row_call(body, name, rows, tr, row_ins, par_ins, row_outs, acc_outs):
    def col(i, cb):
        return (i, cb)

    def whole(i, nd):
        return (0,) * nd

    in_specs = [pl.BlockSpec((tr, w), functools.partial(col, cb=cb)) for (_, w, cb) in row_ins]
    in_specs += [pl.BlockSpec(a.shape, functools.partial(whole, nd=a.ndim)) for a in par_ins]
    out_specs = [pl.BlockSpec((tr, w), lambda i: (i, 0)) for (w, _) in row_outs]
    out_specs += [pl.BlockSpec(s, functools.partial(whole, nd=len(s))) for (s, _) in acc_outs]
    out_shape = [jax.ShapeDtypeStruct((rows, w), dt) for (w, dt) in row_outs]
    out_shape += [jax.ShapeDtypeStruct(s, dt) for (s, dt) in acc_outs]
    return pl.pallas_call(
        body, name=name, grid=(rows // tr,), in_specs=in_specs, out_specs=out_specs, out_shape=out_shape,
        compiler_params=_params(("arbitrary",) if acc_outs else ("parallel",)),
    )(*[a for (a, _, _) in row_ins], *par_ins)


def _accumulate(ref, val):
    i = pl.program_id(0)

    @pl.when(i == 0)
    def _():
        ref[...] = val

    @pl.when(i > 0)
    def _():
        ref[...] += val


def _colsum(v):
    return jnp.sum(v, axis=0, keepdims=True)


def _rms_fwd(x, g):
    r = lax.rsqrt(jnp.mean(x * x, axis=-1, keepdims=True) + EPS)
    return x * r * g


def _rms_bwd(x, g, dy):
    r = lax.rsqrt(jnp.mean(x * x, axis=-1, keepdims=True) + EPS)
    xh = x * r
    dxh = dy * g
    dx = r * (dxh - xh * jnp.mean(dxh * xh, axis=-1, keepdims=True))
    return dx, dy * xh


_GELU_C = math.sqrt(2.0 / math.pi)
_GELU_A = 0.044715


def _gelu(x):
    return 0.5 * x * (1.0 + jnp.tanh(_GELU_C * (x + _GELU_A * (x * x * x))))


def _gelu_grad(x):
    t = jnp.tanh(_GELU_C * (x + _GELU_A * (x * x * x)))
    return 0.5 * (1.0 + t) + 0.5 * x * (1.0 - t * t) * (_GELU_C * (1.0 + 3.0 * _GELU_A * (x * x)))


def _sigmoid(x):
    return 1.0 / (1.0 + jnp.exp(-x))


def _rope_fwd(t, cos_t, sin_t):
    return t * cos_t + pltpu.roll(t, 2 * ROPE_HALF, 1) * sin_t


def _rope_bwd(dt, cos_t, sin_t):
    return dt * cos_t - pltpu.roll(dt, 2 * ROPE_HALF, 1) * sin_t


def _prenorm(x, g, tr, token):
    def body(x_ref, g_ref, token_ref, o_ref):
        o_ref[...] = _rms_fwd(x_ref[...], g_ref[...]).astype(BF16)

    t, d = x.shape
    return _row_call(body, "prenorm", t, tr, [(x, d, 0)], [g, token], [(d, BF16)], [])[0]


def _qkv_prep(proj, g_q, g_kv, cos_t, sin_t, lay, tr):
    ql, kl = lay["ql"], lay["kl"]

    def body(q_ref, kv_ref, kr_ref, cos_ref, sin_ref, gq_ref, gkv_ref, qn_ref, kvn_ref, kro_ref):
        qn_ref[...] = _rms_fwd(q_ref[...], gq_ref[...]).astype(BF16)
        kvn_ref[...] = _rms_fwd(kv_ref[...], gkv_ref[...]).astype(BF16)
        kro_ref[...] = _rope_fwd(kr_ref[...], cos_ref[...], sin_ref[...]).astype(BF16)

    t = proj.shape[0]
    return _row_call(
        body, "qkv_prep", t, tr,
        [(proj, ql, lay["q_off"] // ql), (proj, kl, lay["kv_off"] // kl), (proj, LANES, lay["kr_off"] // LANES),
         (cos_t, LANES, 0), (sin_t, LANES, 0)],
        [g_q, g_kv], [(ql, BF16), (kl, BF16), (LANES, BF16)], [])


def _q_rope(q, cos_t, sin_t, heads, tr):
    def body(q_ref, cos_ref, sin_ref, o_ref):
        c, s = cos_ref[...], sin_ref[...]
        for h in range(heads):
            lo = h * HEAD_PAD
            o_ref[:, lo:lo + NOPE_DIM] = q_ref[:, lo:lo + NOPE_DIM].astype(BF16)
            o_ref[:, lo + NOPE_DIM:lo + HEAD_PAD] = _rope_fwd(q_ref[:, lo + NOPE_DIM:lo + HEAD_PAD], c, s).astype(BF16)

    t, w = q.shape
    return _row_call(body, "q_rope", t, tr, [(q, w, 0), (cos_t, LANES, 0), (sin_t, LANES, 0)], [], [(w, BF16)], [])[0]


ATTN_SCALE = 1.0 / math.sqrt(NOPE_DIM + ROPE_DIM)
ATTN_EXP2_SCALE = ATTN_SCALE * math.log2(math.e)


def _attn_tile(t, cap):
    tq = cap
    while t % tq:
        tq //= 2
    return tq


def _attn_fwd(q, kv, kr, heads):
    t = q.shape[0]
    tq = _attn_tile(t, 256)

    def body(q_ref, kv_ref, kr_ref, o_ref, lse_ref, kcat):
        @pl.when(pl.program_id(1) == 0)
        def _():
            kcat[:, :NOPE_DIM] = kv_ref[:, :NOPE_DIM]
            kcat[:, NOPE_DIM:] = kr_ref[...]

        s = lax.dot_general(q_ref[...], kcat[...], NT, preferred_element_type=F32)
        m = jnp.max(s, axis=-1, keepdims=True)
        p = jnp.exp2((s - m) * ATTN_EXP2_SCALE)
        l = jnp.sum(p, axis=-1, keepdims=True)
        o_ref[...] = jnp.dot(p.astype(BF16), kv_ref[:, NOPE_DIM:], preferred_element_type=F32) * (1.0 / l)
        lse_ref[...] = jnp.broadcast_to(m * ATTN_EXP2_SCALE + jnp.log(l) * math.log2(math.e), (tq, V_DIM))

    out_spec = pl.BlockSpec((tq, V_DIM), lambda h, i: (i, h))
    out_shape = jax.ShapeDtypeStruct((t, heads * V_DIM), F32)
    return pl.pallas_call(
        body, name="attn_fwd", grid=(heads, t // tq),
        in_specs=[pl.BlockSpec((tq, HEAD_PAD), lambda h, i: (i, h)),
                  pl.BlockSpec((t, HEAD_PAD), lambda h, i: (0, h)),
                  pl.BlockSpec((t, LANES), lambda h, i: (0, 0))],
        out_specs=[out_spec, out_spec], out_shape=[out_shape, out_shape],
        scratch_shapes=[pltpu.VMEM((t, HEAD_PAD), BF16)],
        compiler_params=_params(("arbitrary", "arbitrary")),
    )(q, kv, kr)


def _attn_bwd(q, kv, kr, out, lse, d_out, cos_t, sin_t, heads):
    t = q.shape[0]
    tq = _attn_tile(t, 256)
    nq = t // tq

    def body(q_ref, kv_ref, kr_ref, o_ref, lse_ref, do_ref, cos_ref, sin_ref, dq_ref, dkv_ref, dkr_ref,
             kcat, dk_acc, dv_acc):
        h, i = pl.program_id(0), pl.program_id(1)

        @pl.when(i == 0)
        def _():
            kcat[:, :NOPE_DIM] = kv_ref[:, :NOPE_DIM]
            kcat[:, NOPE_DIM:] = kr_ref[...]
            dk_acc[...] = jnp.zeros_like(dk_acc)
            dv_acc[...] = jnp.zeros_like(dv_acc)

        @pl.when((h == 0) & (i == 0))
        def _():
            dkr_ref[...] = jnp.zeros_like(dkr_ref)

        qb, dob = q_ref[...], do_ref[...]
        row_term = jnp.sum(dob.astype(F32) * o_ref[...], axis=-1, keepdims=True)
        s = lax.dot_general(qb, kcat[...], NT, preferred_element_type=F32)
        dp = lax.dot_general(dob, kv_ref[:, NOPE_DIM:], NT, preferred_element_type=F32)
        p = jnp.exp2(s * ATTN_EXP2_SCALE - lse_ref[:, :1])
        ds = (p * (dp - row_term)).astype(BF16)
        dv_acc[...] += lax.dot_general(p.astype(BF16), dob, TN, preferred_element_type=F32)
        dq = jnp.dot(ds, kcat[...], preferred_element_type=F32) * ATTN_SCALE
        dq_ref[:, :NOPE_DIM] = dq[:, :NOPE_DIM].astype(BF16)
        dq_ref[:, NOPE_DIM:] = _rope_bwd(dq[:, NOPE_DIM:], cos_ref[...], sin_ref[...]).astype(BF16)
        dk_acc[...] += lax.dot_general(ds, qb, TN, preferred_element_type=F32)

        @pl.when(i == nq - 1)
        def _():
            dkv_ref[:, :NOPE_DIM] = (dk_acc[:, :NOPE_DIM] * ATTN_SCALE).astype(BF16)
            dkv_ref[:, NOPE_DIM:] = dv_acc[...].astype(BF16)
            dkr_ref[...] += dk_acc[:, NOPE_DIM:] * ATTN_SCALE

    return pl.pallas_call(
        body, name="attn_bwd", grid=(heads, nq),
        in_specs=[pl.BlockSpec((tq, HEAD_PAD), lambda h, i: (i, h)),
                  pl.BlockSpec((t, HEAD_PAD), lambda h, i: (0, h)),
                  pl.BlockSpec((t, LANES), lambda h, i: (0, 0)),
                  pl.BlockSpec((tq, V_DIM), lambda h, i: (i, h)),
                  pl.BlockSpec((tq, V_DIM), lambda h, i: (i, h)),
                  pl.BlockSpec((tq, V_DIM), lambda h, i: (i, h)),
                  pl.BlockSpec((tq, LANES), lambda h, i: (i, 0)),
                  pl.BlockSpec((tq, LANES), lambda h, i: (i, 0))],
        out_specs=[pl.BlockSpec((tq, HEAD_PAD), lambda h, i: (i, h)),
                   pl.BlockSpec((t, HEAD_PAD), lambda h, i: (0, h)),
                   pl.BlockSpec((t, LANES), lambda h, i: (0, 0))],
        out_shape=[jax.ShapeDtypeStruct((t, heads * HEAD_PAD), BF16),
                   jax.ShapeDtypeStruct((t, heads * HEAD_PAD), BF16),
                   jax.ShapeDtypeStruct((t, LANES), F32)],
        scratch_shapes=[pltpu.VMEM((t, HEAD_PAD), BF16), pltpu.VMEM((t, HEAD_PAD), F32), pltpu.VMEM((t, V_DIM), F32)],
        compiler_params=_params(("arbitrary", "arbitrary")),
    )(q, kv, kr, out, lse, d_out, cos_t, sin_t)


def _layer_norm_parts(x):
    mu = jnp.mean(x, axis=-1, keepdims=True)
    xc = x - mu
    r = lax.rsqrt(jnp.mean(xc * xc, axis=-1, keepdims=True) + EPS)
    return xc * r, r


def _gmlp_fwd(proj, ln_g, ln_b, w_s, b_sb, g_out_norm, lay):
    gw = lay["gw"]
    g_heads = gw // G_HEAD_DIM

    def body(u_ref, v_ref, lng_ref, lnb_ref, ws_ref, bs_ref, gn_ref, o_ref, gate_ref):
        gu = _gelu(u_ref[...])
        vh, _ = _layer_norm_parts(_gelu(v_ref[...]))
        vln = (vh * lng_ref[...] + lnb_ref[...]).astype(BF16)
        for g in range(g_heads):
            cols = slice(g * G_HEAD_DIM, (g + 1) * G_HEAD_DIM)
            s = jnp.dot(ws_ref[g], vln[:, cols], preferred_element_type=F32) + bs_ref[g]
            gate_ref[:, cols] = gu[:, cols] * s
        o_ref[...] = _rms_fwd(gate_ref[...], gn_ref[...]).astype(BF16)

    t = proj.shape[0]
    in_specs = [pl.BlockSpec((CHUNK, gw), lambda i: (i, 0)), pl.BlockSpec((CHUNK, gw), lambda i: (i, 1))]
    pars = [ln_g, ln_b, w_s, b_sb, g_out_norm]
    in_specs += [pl.BlockSpec(a.shape, functools.partial(lambda i, nd: (0,) * nd, nd=a.ndim)) for a in pars]
    return pl.pallas_call(
        body, name="gmlp_fwd", grid=(t // CHUNK,), in_specs=in_specs,
        out_specs=pl.BlockSpec((CHUNK, gw), lambda i: (i, 0)),
        out_shape=jax.ShapeDtypeStruct((t, gw), BF16),
        scratch_shapes=[pltpu.VMEM((CHUNK, gw), F32)],
        compiler_params=_params(("parallel",)),
    )(proj, proj, *pars)


def _gmlp_bwd(proj, d_mixed, ln_g, ln_b, w_s, w_st, b_sb, g_out_norm, lay):
    gw = lay["gw"]
    g_heads = gw // G_HEAD_DIM
    aw_blocks = lay["aw"] // gw

    def body(u_ref, v_ref, dm_ref, lng_ref, lnb_ref, ws_ref, wst_ref, bs_ref, gn_ref,
             du_ref, dv_ref, dgn_ref, dlng_ref, dlnb_ref, dws_ref, dbs_ref, gate_ref, s_ref, dvln_ref):
        i = pl.program_id(0)
        u, v = u_ref[...], v_ref[...]
        gu, gv = _gelu(u), _gelu(v)
        vh, r_ln = _layer_norm_parts(gv)
        vln = (vh * lng_ref[...] + lnb_ref[...]).astype(BF16)
        for g in range(g_heads):
            cols = slice(g * G_HEAD_DIM, (g + 1) * G_HEAD_DIM)
            s = jnp.dot(ws_ref[g], vln[:, cols], preferred_element_type=F32) + bs_ref[g]
            s_ref[:, cols] = s
            gate_ref[:, cols] = gu[:, cols] * s
        d_gate, dgn = _rms_bwd(gate_ref[...], gn_ref[...], dm_ref[...])
        _accumulate(dgn_ref, _colsum(dgn))
        du_ref[...] = (d_gate * s_ref[...] * _gelu_grad(u)).astype(BF16)
        d_s = d_gate * gu
        d_sb = d_s.astype(BF16)
        for g in range(g_heads):
            cols = slice(g * G_HEAD_DIM, (g + 1) * G_HEAD_DIM)
            dw = lax.dot_general(d_sb[:, cols], vln[:, cols], NT, preferred_element_type=F32)

            @pl.when(i == 0)
            def _():
                dws_ref[g] = dw
                dbs_ref[g] = d_s[:, cols]

            @pl.when(i > 0)
            def _():
                dws_ref[g] += dw
                dbs_ref[g] += d_s[:, cols]

            dvln_ref[:, cols] = jnp.dot(wst_ref[g], d_sb[:, cols], preferred_element_type=F32)
        d_vln = dvln_ref[...]
        _accumulate(dlng_ref, _colsum(d_vln * vh))
        _accumulate(dlnb_ref, _colsum(d_vln))
        d_vh = d_vln * lng_ref[...]
        d_gv = r_ln * (d_vh - jnp.mean(d_vh, axis=-1, keepdims=True)
                       - vh * jnp.mean(d_vh * vh, axis=-1, keepdims=True))
        dv_ref[...] = (d_gv * _gelu_grad(v)).astype(BF16)

    t = proj.shape[0]
    whole = lambda a: pl.BlockSpec(a.shape, functools.partial(lambda i, nd: (0,) * nd, nd=a.ndim))
    pars = [ln_g, ln_b, w_s, w_st, b_sb, g_out_norm]
    hshape = (g_heads, CHUNK, CHUNK)
    return pl.pallas_call(
        body, name="gmlp_bwd", grid=(t // CHUNK,),
        in_specs=[pl.BlockSpec((CHUNK, gw), lambda i: (i, 0)), pl.BlockSpec((CHUNK, gw), lambda i: (i, 1)),
                  pl.BlockSpec((CHUNK, gw), lambda i: (i, aw_blocks))] + [whole(a) for a in pars],
        out_specs=[pl.BlockSpec((CHUNK, gw), lambda i: (i, 0)), pl.BlockSpec((CHUNK, gw), lambda i: (i, 0)),
                   pl.BlockSpec((1, gw), lambda i: (0, 0)), pl.BlockSpec((1, gw), lambda i: (0, 0)),
                   pl.BlockSpec((1, gw), lambda i: (0, 0)),
                   pl.BlockSpec(hshape, lambda i: (0, 0, 0)), pl.BlockSpec(hshape, lambda i: (0, 0, 0))],
        out_shape=[jax.ShapeDtypeStruct((t, gw), BF16), jax.ShapeDtypeStruct((t, gw), BF16),
                   jax.ShapeDtypeStruct((1, gw), F32), jax.ShapeDtypeStruct((1, gw), F32),
                   jax.ShapeDtypeStruct((1, gw), F32),
                   jax.ShapeDtypeStruct(hshape, F32), jax.ShapeDtypeStruct(hshape, F32)],
        scratch_shapes=[pltpu.VMEM((CHUNK, gw), F32), pltpu.VMEM((CHUNK, gw), F32), pltpu.VMEM((CHUNK, gw), F32)],
        compiler_params=_params(("arbitrary",)),
    )(proj, proj, d_mixed, *pars)


def _spatial_bias_grad(dbs_wide):
    g_heads = dbs_wide.shape[0]

    def body(x_ref, o_ref):
        for g in range(g_heads):
            o_ref[g:g + 1, :] = jnp.sum(x_ref[g].T, axis=0, keepdims=True)

    return pl.pallas_call(
        body, name="spatial_bias_grad", out_shape=jax.ShapeDtypeStruct((g_heads, CHUNK), F32),
        in_specs=[pl.BlockSpec(memory_space=pltpu.VMEM)], out_specs=pl.BlockSpec(memory_space=pltpu.VMEM),
    )(dbs_wide)


def _mix_norm(a_out, gn, g_a, tr):
    aw = a_out.shape[1]
    gw = gn.shape[1]

    def body(a_ref, gn_ref, g_ref, o_ref):
        o_ref[:, :aw] = _rms_fwd(a_ref[...], g_ref[...]).astype(BF16)
        o_ref[:, aw:] = gn_ref[...]

    t = a_out.shape[0]
    return _row_call(body, "mix_norm", t, tr, [(a_out, aw, 0), (gn, gw, 0)], [g_a], [(aw + gw, BF16)], [])[0]


def _mix_norm_bwd(a_out, d_mixed, g_a, tr, token):
    aw = a_out.shape[1]

    def body(a_ref, dm_ref, g_ref, token_ref, da_ref, dg_ref):
        dx, dg = _rms_bwd(a_ref[...], g_ref[...], dm_ref[...])
        da_ref[...] = dx.astype(BF16)
        _accumulate(dg_ref, _colsum(dg))

    t = a_out.shape[0]
    return _row_call(body, "mix_norm_bwd", t, tr, [(a_out, aw, 0), (d_mixed, aw, 0)], [g_a, token],
                     [(aw, BF16)], [((1, aw), F32)])


def _post_mix(x, mix_out, g_pm, g_pf, tr):
    def body(x_ref, mo_ref, gpm_ref, gpf_ref, h_ref, hn_ref):
        h = x_ref[...] + _rms_fwd(mo_ref[...], gpm_ref[...])
        h_ref[...] = h
        hn_ref[...] = _rms_fwd(h, gpf_ref[...]).astype(BF16)

    t, d = x.shape
    return _row_call(body, "post_mix", t, tr, [(x, d, 0), (mix_out, d, 0)], [g_pm, g_pf], [(d, F32), (d, BF16)], [])


def _post_mix_bwd(mix_out, h, dy, d_hn, g_pm, g_pf, tr, token):
    def body(mo_ref, h_ref, dy_ref, dhn_ref, gpm_ref, gpf_ref, token_ref, dh_ref, dmo_ref, dgpf_ref, dgpm_ref):
        dx, dg = _rms_bwd(h_ref[...], gpf_ref[...], dhn_ref[...])
        dh = dy_ref[...] + dx
        dh_ref[...] = dh
        _accumulate(dgpf_ref, _colsum(dg))
        dmo, dg2 = _rms_bwd(mo_ref[...], gpm_ref[...], dh)
        dmo_ref[...] = dmo.astype(BF16)
        _accumulate(dgpm_ref, _colsum(dg2))

    t, d = h.shape
    return _row_call(body, "post_mix_bwd", t, tr, [(mix_out, d, 0), (h, d, 0), (dy, d, 0), (d_hn, d, 0)],
                     [g_pm, g_pf, token], [(d, F32), (d, BF16)], [((1, d), F32), ((1, d), F32)])


def _swiglu_of_up(up, gate):
    g = gate.astype(F32)
    return up, g * _sigmoid(g) * up


def _swiglu_grads(d_act, gate, up):
    g, u = gate.astype(F32), up.astype(F32)
    sg = _sigmoid(g)
    return d_act * u * (sg * (1.0 + g * (1.0 - sg))), d_act * (g * sg)


def _loss_head(h, ffn, target, g_po, tr):
    t, d = h.shape

    def body(h_ref, f_ref, t_ref, g_ref, dy_ref, df_ref, dg_ref, loss_ref):
        f = f_ref[...]
        err = h_ref[...] + _rms_fwd(f, g_ref[...]) - t_ref[...]
        dy = err * (1.0 / d)
        dy_ref[...] = dy
        df, dg = _rms_bwd(f, g_ref[...], dy)
        df_ref[...] = df.astype(BF16)
        _accumulate(dg_ref, _colsum(dg))
        sq = jnp.sum(_colsum(err * err), axis=-1, keepdims=True) * (0.5 / d)
        _accumulate(loss_ref, jnp.broadcast_to(sq, (1, LANES)))

    return _row_call(body, "loss_head", t, tr, [(h, d, 0), (ffn, d, 0), (target, d, 0)], [g_po],
                     [(d, F32), (d, BF16)], [((1, d), F32), ((1, LANES), F32)])


def _qkv_bwd(proj, d_qn, d_kvn, d_kr, g_q, g_kv, cos_t, sin_t, lay, tr):
    ql, kl = lay["ql"], lay["kl"]

    def body(q_ref, kv_ref, dqn_ref, dkvn_ref, dkr_ref, cos_ref, sin_ref, gq_ref, gkv_ref,
             dq_ref, dkv_ref, dkt_ref, dgq_ref, dgkv_ref):
        dx, dg = _rms_bwd(q_ref[...], gq_ref[...], dqn_ref[...])
        dq_ref[...] = dx.astype(BF16)
        _accumulate(dgq_ref, _colsum(dg))
        dx, dg = _rms_bwd(kv_ref[...], gkv_ref[...], dkvn_ref[...])
        dkv_ref[...] = dx.astype(BF16)
        _accumulate(dgkv_ref, _colsum(dg))
        dkt_ref[...] = _rope_bwd(dkr_ref[...], cos_ref[...], sin_ref[...]).astype(BF16)

    t = proj.shape[0]
    return _row_call(
        body, "qkv_bwd", t, tr,
        [(proj, ql, lay["q_off"] // ql), (proj, kl, lay["kv_off"] // kl), (d_qn, ql, 0), (d_kvn, kl, 0),
         (d_kr, LANES, 0), (cos_t, LANES, 0), (sin_t, LANES, 0)],
        [g_q, g_kv], [(ql, BF16), (kl, BF16), (LANES, BF16)], [((1, ql), F32), ((1, kl), F32)])


def _prenorm_bwd(x, d_xn, dh, g, tr):
    def body(x_ref, dxn_ref, dh_ref, g_ref, gx_ref, dg_ref):
        dx, dg = _rms_bwd(x_ref[...], g_ref[...], dxn_ref[...])
        gx_ref[...] = dh_ref[...] + dx
        _accumulate(dg_ref, _colsum(dg))

    t, d = x.shape
    return _row_call(body, "prenorm_bwd", t, tr, [(x, d, 0), (d_xn, d, 0), (dh, d, 0)], [g],
                     [(d, F32)], [((1, d), F32)])


def _adam_rows(rows, cols):
    cap = max(8, (256 * 1024) // cols // 8 * 8)
    tr = min(rows, cap)
    while rows % tr:
        tr -= 8
    return tr


def _adamw(w, g, m, v, name):
    rows, cols = w.shape
    tr = _adam_rows(rows, cols)

    def body(w_ref, g_ref, m_ref, v_ref, d_ref, mo_ref, vo_ref):
        g = g_ref[...]
        m2 = ADAM_B1 * m_ref[...] + (1.0 - ADAM_B1) * g
        v2 = ADAM_B2 * v_ref[...] + (1.0 - ADAM_B2) * (g * g)
        m_hat = m2 / (1.0 - ADAM_B1 ** ADAM_STEP)
        v_hat = v2 / (1.0 - ADAM_B2 ** ADAM_STEP)
        d_ref[...] = -ADAM_LR * (m_hat / (jnp.sqrt(v_hat) + ADAM_EPS) + ADAM_WD * w_ref[...])
        mo_ref[...] = m2
        vo_ref[...] = v2

    spec = pl.BlockSpec((tr, cols), lambda i: (i, 0))
    shape = jax.ShapeDtypeStruct((rows, cols), F32)
    return pl.pallas_call(body, name=name, grid=(rows // tr,), in_specs=[spec] * 4, out_specs=[spec] * 3,
                          out_shape=[shape] * 3, compiler_params=_params(("parallel",)))(w, g, m, v)


def _adamw_halves(w, g_mine, g_theirs, m, v, name):
    rows, cols = w.shape
    rh = g_mine.shape[0]
    tr = _adam_rows(math.gcd(rows, rh), cols)
    per_half = rh // tr
    my_c = jnp.reshape(lax.axis_index("c"), (1,)).astype(jnp.int32)

    def body(c_ref, w_ref, gm_ref, gt_ref, m_ref, v_ref, g_ref, d_ref, mo_ref, vo_ref):
        mine = (pl.program_id(0) // per_half) == c_ref[0]
        g = jnp.where(mine, gm_ref[...], gt_ref[...])
        m2 = ADAM_B1 * m_ref[...] + (1.0 - ADAM_B1) * g
        v2 = ADAM_B2 * v_ref[...] + (1.0 - ADAM_B2) * (g * g)
        m_hat = m2 / (1.0 - ADAM_B1 ** ADAM_STEP)
        v_hat = v2 / (1.0 - ADAM_B2 ** ADAM_STEP)
        g_ref[...] = g
        d_ref[...] = -ADAM_LR * (m_hat / (jnp.sqrt(v_hat) + ADAM_EPS) + ADAM_WD * w_ref[...])
        mo_ref[...] = m2
        vo_ref[...] = v2

    def half_spec(is_mine):
        def index(i, c_ref):
            used = ((i // per_half) == c_ref[0]) if is_mine else ((i // per_half) != c_ref[0])
            return (jnp.where(used, i % per_half, 0), 0)
        return pl.BlockSpec((tr, cols), index)

    spec = pl.BlockSpec((tr, cols), lambda i, c_ref: (i, 0))
    shape = jax.ShapeDtypeStruct((rows, cols), F32)
    grid_spec = pltpu.PrefetchScalarGridSpec(
        num_scalar_prefetch=1, grid=(rows // tr,),
        in_specs=[spec, half_spec(True), half_spec(False), spec, spec], out_specs=[spec] * 4)
    return pl.pallas_call(body, name=name, grid_spec=grid_spec, out_shape=[shape] * 4,
                          compiler_params=_params(("parallel",)))(my_c, w, g_mine, g_theirs, m, v)


def _pair_add(parts, theirs, name):
    _, n, r, c = parts.shape
    tr = _adam_rows(r, c)
    my_c = jnp.reshape(lax.axis_index("c"), (1,)).astype(jnp.int32)

    def body(c_ref, a_ref, b_ref, o_ref):
        o_ref[0] = (a_ref[0, 0].astype(F32) + b_ref[0].astype(F32)).astype(BF16)

    spec = pl.BlockSpec((1, tr, c), lambda k, i, c_ref: (k, i, 0))
    grid_spec = pltpu.PrefetchScalarGridSpec(
        num_scalar_prefetch=1, grid=(n, r // tr),
        in_specs=[pl.BlockSpec((1, 1, tr, c), lambda k, i, c_ref: (c_ref[0], k, i, 0)), spec], out_specs=spec)
    return pl.pallas_call(body, name=name, grid_spec=grid_spec, out_shape=jax.ShapeDtypeStruct((n, r, c), BF16),
                          compiler_params=_params(("parallel", "parallel")))(my_c, parts, theirs)


def _chip_sum(pair_sums, received, name, token=None):
    _, r, c = pair_sums.shape
    tr = _adam_rows(r, c)
    own = 2 * lax.axis_index("x") + lax.axis_index("y")

    def body(own_ref, p_ref, r0_ref, r1_ref, r2_ref, *rest):
        o_ref = rest[-1]
        acc = p_ref[0].astype(F32) + r0_ref[0].astype(F32)
        acc = acc + r1_ref[0].astype(F32)
        o_ref[...] = acc + r2_ref[0].astype(F32)

    def rspec(j):
        return pl.BlockSpec((1, tr, c), functools.partial(lambda i, own_ref, j: (j, i, 0), j=j))

    extra = [] if token is None else [token]
    grid_spec = pltpu.PrefetchScalarGridSpec(
        num_scalar_prefetch=1, grid=(r // tr,),
        in_specs=[pl.BlockSpec((1, tr, c), lambda i, own_ref: (own_ref[0], i, 0)), rspec(0), rspec(1), rspec(2)]
        + [pl.BlockSpec(memory_space=pl.ANY)] * len(extra),
        out_specs=pl.BlockSpec((tr, c), lambda i, own_ref: (i, 0)))
    return pl.pallas_call(body, name=name, grid_spec=grid_spec, out_shape=jax.ShapeDtypeStruct((r, c), F32),
                          compiler_params=_params(("parallel",)))(
        jnp.reshape(own, (1,)).astype(jnp.int32), pair_sums, received, received, received, *extra)


def _mesh_place():
    x, y, c = lax.axis_index("x"), lax.axis_index("y"), lax.axis_index("c")
    other_chips = [(1 - x, y), (x, 1 - y), (1 - x, 1 - y)]
    return x, y, c, other_chips


def _hbm_specs(n):
    return [pl.BlockSpec(memory_space=pltpu.HBM)] * n


def _all_gather(shards):
    n = len(shards)

    def body(*refs):
        ins, outs = refs[:n], refs[n:2 * n]
        send_sems, recv_sems = refs[2 * n:]
        x, y, c, other_chips = _mesh_place()
        own = 2 * x + y
        sibling = (x, y, 1 - c)

        def remote(src, dst, k, to):
            return pltpu.make_async_remote_copy(src_ref=src, dst_ref=dst, send_sem=send_sems.at[k],
                                                recv_sem=recv_sems.at[k], device_id=to, device_id_type=MESH)

        first = [remote(ins[w].at[c], outs[w].at[own, c], 7 * w + j, (*chip, c))
                 for w in range(n) for j, chip in enumerate(other_chips)]
        first += [remote(ins[w], outs[w].at[own], 7 * w + 6, sibling) for w in range(n)]
        for cp in first:
            cp.start()
        passed = []
        for w in range(n):
            for j, chip in enumerate(other_chips):
                rows = outs[w].at[2 * chip[0] + chip[1], c]
                remote(rows, rows, 7 * w + j, (*chip, c)).wait_recv()
                fwd = remote(rows, rows, 7 * w + 3 + j, sibling)
                fwd.start()
                passed.append(fwd)
        for w in range(n):
            for j, chip in enumerate(other_chips):
                rows = outs[w].at[2 * chip[0] + chip[1], 1 - c]
                remote(rows, rows, 7 * w + 3 + j, sibling).wait_recv()
            remote(ins[w], outs[w].at[own], 7 * w + 6, sibling).wait_recv()
        for cp in first + passed:
            cp.wait_send()

    return pl.pallas_call(
        body, name="weights_all_gather",
        out_shape=[jax.ShapeDtypeStruct((4,) + s.shape, s.dtype) for s in shards],
        in_specs=_hbm_specs(n), out_specs=_hbm_specs(n),
        scratch_shapes=[pltpu.SemaphoreType.DMA((7 * n,)), pltpu.SemaphoreType.DMA((7 * n,))],
    )(*shards)


def _sibling_exchange(parts, name):
    n = len(parts)

    def body(*refs):
        ins, outs = refs[:n], refs[n:2 * n]
        send_sems, recv_sems = refs[2 * n:]
        x, y, c, _ = _mesh_place()
        copies = [pltpu.make_async_remote_copy(src_ref=ins[w].at[1 - c], dst_ref=outs[w], send_sem=send_sems.at[w],
                                               recv_sem=recv_sems.at[w], device_id=(x, y, 1 - c), device_id_type=MESH)
                  for w in range(n)]
        for cp in copies:
            cp.start()
        for cp in copies:
            cp.wait()

    return pl.pallas_call(
        body, name=name,
        out_shape=[jax.ShapeDtypeStruct(p.shape[1:], p.dtype) for p in parts],
        in_specs=_hbm_specs(n), out_specs=_hbm_specs(n),
        scratch_shapes=[pltpu.SemaphoreType.DMA((n,)), pltpu.SemaphoreType.DMA((n,))],
    )(*parts)


SEM_SPEC = pl.BlockSpec(memory_space=pltpu.SEMAPHORE)
DATAFLOW_EFFECT = pltpu.SideEffectType.DATAFLOW_SIDE_EFFECTING


def _copies_per_weight(kind):
    return 4 if kind == "gather" else 3


def _flight_copies(kind, src_refs, land_refs, send_sems, recv_sems, arriving):
    x, y, c, other_chips = _mesh_place()
    own = 2 * x + y
    per = _copies_per_weight(kind)
    copies = []
    for w in range(len(src_refs)):
        def remote(src, dst, j, to):
            return pltpu.make_async_remote_copy(src_ref=src, dst_ref=dst, send_sem=send_sems.at[per * w + j],
                                                recv_sem=recv_sems.at[per * w + j], device_id=to, device_id_type=MESH)

        for j, chip in enumerate(other_chips):
            theirs = 2 * chip[0] + chip[1]
            if kind == "gather":
                src, dst = src_refs[w].at[c], land_refs[w].at[theirs if arriving else own, c]
            else:
                src, dst = src_refs[w].at[theirs], land_refs[w].at[j]
            copies.append(remote(src, dst, j, (*chip, c)))
        if kind == "gather":
            copies.append(remote(src_refs[w], land_refs[w].at[own], 3, (x, y, 1 - c)))
    return copies


def _ici_start(kind, srcs, name):
    n = len(srcs)
    if kind == "gather":
        lands = [lax.empty((4,) + s.shape, s.dtype) for s in srcs]
    else:
        lands = [lax.empty((3,) + s.shape[1:], s.dtype) for s in srcs]

    def body(*refs):
        src_refs, land_refs = refs[:n], refs[n:2 * n]
        send_sems, recv_sems = refs[2 * n], refs[2 * n + 1]
        token = refs[-1]
        for cp in _flight_copies(kind, src_refs, land_refs, send_sems, recv_sems, False):
            cp.start()
        token[...] = jnp.zeros_like(token)

    hbm = lambda a: pltpu.with_memory_space_constraint(a, pltpu.HBM)
    n_sems = _copies_per_weight(kind) * n
    out = pl.pallas_call(
        body, name=name,
        out_shape=(pltpu.SemaphoreType.DMA((n_sems,)), pltpu.SemaphoreType.DMA((n_sems,)),
                   *[pltpu.HBM(a.shape, a.dtype) for a in srcs + lands], jax.ShapeDtypeStruct((8, LANES), F32)),
        in_specs=_hbm_specs(2 * n),
        out_specs=(SEM_SPEC, SEM_SPEC, *_hbm_specs(2 * n), pl.BlockSpec(memory_space=pltpu.VMEM)),
        input_output_aliases={i: 2 + i for i in range(2 * n)},
        compiler_params=pltpu.CompilerParams(has_side_effects=DATAFLOW_EFFECT),
    )(*[hbm(a) for a in srcs + lands])
    return out[0], out[1], list(out[2:2 + n]), list(out[2 + n:2 + 2 * n]), out[-1]


def _ici_wait(kind, send_sems, recv_sems, srcs, lands, after, name):
    n = len(srcs)

    def body(*refs):
        src_refs, land_refs = refs[:n], refs[n:2 * n]
        send_ref, recv_ref = refs[2 * n], refs[2 * n + 1]
        for cp in _flight_copies(kind, src_refs, land_refs, send_ref, recv_ref, True):
            cp.wait_send()
            cp.wait_recv()

    out = pl.pallas_call(
        body, name=name, out_shape=tuple(pltpu.HBM(a.shape, a.dtype) for a in srcs + lands),
        in_specs=_hbm_specs(2 * n) + [SEM_SPEC, SEM_SPEC, pl.BlockSpec(memory_space=pl.ANY)],
        out_specs=tuple(_hbm_specs(2 * n)), input_output_aliases={i: i for i in range(2 * n)},
        compiler_params=pltpu.CompilerParams(has_side_effects=DATAFLOW_EFFECT),
    )(*srcs, *lands, send_sems, recv_sems, after)
    return list(out[:n]), list(out[n:])


def _forward_halves(lands, name):
    n = len(lands)

    def body(*refs):
        ins, outs = refs[:n], refs[n:2 * n]
        send_sems, recv_sems = refs[2 * n:]
        x, y, c, other_chips = _mesh_place()
        copies = []
        for w in range(n):
            for j, chip in enumerate(other_chips):
                k = 2 * chip[0] + chip[1]
                cp = pltpu.make_async_remote_copy(
                    src_ref=ins[w].at[k, c], dst_ref=outs[w].at[k, c], send_sem=send_sems.at[3 * w + j],
                    recv_sem=recv_sems.at[3 * w + j], device_id=(x, y, 1 - c), device_id_type=MESH)
                cp.start()
                copies.append(cp)
        for w in range(n):
            for j, chip in enumerate(other_chips):
                k = 2 * chip[0] + chip[1]
                pltpu.make_async_remote_copy(
                    src_ref=ins[w].at[k, c], dst_ref=outs[w].at[k, 1 - c], send_sem=send_sems.at[3 * w + j],
                    recv_sem=recv_sems.at[3 * w + j], device_id=(x, y, 1 - c), device_id_type=MESH).wait_recv()
        for cp in copies:
            cp.wait_send()

    return pl.pallas_call(
        body, name=name, out_shape=[jax.ShapeDtypeStruct(a.shape, a.dtype) for a in lands],
        in_specs=_hbm_specs(n), out_specs=_hbm_specs(n), input_output_aliases={i: i for i in range(n)},
        scratch_shapes=[pltpu.SemaphoreType.DMA((3 * n,)), pltpu.SemaphoreType.DMA((3 * n,))],
    )(*lands)


def _halves_exchange(halves, name):
    n = len(halves)

    def body(*refs):
        ins, outs = refs[:n], refs[n:2 * n]
        send_sems, recv_sems = refs[2 * n:]
        x, y, c, _ = _mesh_place()
        copies = [pltpu.make_async_remote_copy(src_ref=ins[w], dst_ref=outs[w], send_sem=send_sems.at[w],
                                               recv_sem=recv_sems.at[w], device_id=(x, y, 1 - c), device_id_type=MESH)
                  for w in range(n)]
        for cp in copies:
            cp.start()
        for cp in copies:
            cp.wait()

    return pl.pallas_call(
        body, name=name,
        out_shape=[jax.ShapeDtypeStruct(h.shape, h.dtype) for h in halves],
        in_specs=_hbm_specs(n), out_specs=_hbm_specs(n),
        scratch_shapes=[pltpu.SemaphoreType.DMA((n,)), pltpu.SemaphoreType.DMA((n,))],
    )(*halves)


def _small_all_reduce(packed):
    rows = packed.shape[0]

    def body(in_ref, out_ref, gathered, send_sems, recv_sems):
        x, y, c, _ = _mesh_place()
        me = 4 * x + 2 * y + c
        gathered[0] = in_ref[...]
        copies = []
        for rel in range(1, 8):
            to = (x ^ (rel >> 2), y ^ ((rel >> 1) & 1), c ^ (rel & 1))
            cp = pltpu.make_async_remote_copy(src_ref=in_ref, dst_ref=gathered.at[rel], send_sem=send_sems.at[rel - 1],
                                              recv_sem=recv_sems.at[rel - 1], device_id=to, device_id_type=MESH)
            cp.start()
            copies.append(cp)
        for cp in copies:
            cp.wait()
        acc = gathered[me]
        for dev in range(1, 8):
            acc = acc + gathered[dev ^ me]
        out_ref[...] = acc

    return pl.pallas_call(
        body, name="small_all_reduce", out_shape=jax.ShapeDtypeStruct(packed.shape, F32),
        in_specs=[pl.BlockSpec(memory_space=pltpu.VMEM)], out_specs=pl.BlockSpec(memory_space=pltpu.VMEM),
        scratch_shapes=[pltpu.VMEM((8, rows, LANES), F32), pltpu.SemaphoreType.DMA((7,)), pltpu.SemaphoreType.DMA((7,))],
        compiler_params=pltpu.CompilerParams(vmem_limit_bytes=VMEM_LIMIT_BYTES),
    )(packed)


def _layout(w_in, w_uq, w_ukv, v_ln_gain, q_norm, kv_norm):
    heads = 4 * w_uq.shape[-1] // (NOPE_DIM + ROPE_DIM)
    gw = v_ln_gain.shape[-1]
    ql, kl = q_norm.shape[-1], kv_norm.shape[-1]
    lay = dict(heads=heads, gw=gw, ql=ql, kl=kl, aw=heads * V_DIM, u_off=0, v_off=gw, q_off=2 * gw,
               kv_off=2 * gw + ql, kr_off=2 * gw + ql + kl)
    lay["in_pad"] = _round_up(lay["kr_off"] + LANES, 2 * LANES if lay["kr_off"] + LANES <= 2048 else 1024)
    assert lay["q_off"] % ql == 0 and lay["kv_off"] % kl == 0 and lay["aw"] % gw == 0
    assert 4 * w_in.shape[-1] == ql + kl + ROPE_DIM + 2 * gw
    return lay


def _rope_tile(t1, t2, axis=-1):
    z = jnp.zeros_like(t1)
    return jnp.concatenate([t1, z, t2, z], axis=axis)


def _w_in_rows(gathered, shard_rows, lay):
    d = gathered.shape[-1]
    wt = gathered[:, :shard_rows].reshape(4 * shard_rows, d)
    ql, kl, gw = lay["ql"], lay["kl"], lay["gw"]
    q_c, kv_c = wt[:ql], wt[ql:ql + kl]
    r = wt[ql + kl:ql + kl + ROPE_DIM]
    u = wt[ql + kl + ROPE_DIM:ql + kl + ROPE_DIM + gw]
    v = wt[ql + kl + ROPE_DIM + gw:]
    parts = [u, v, q_c, kv_c, _rope_tile(r[:ROPE_HALF], r[ROPE_HALF:], axis=0)]
    pad = lay["in_pad"] - (lay["kr_off"] + LANES)
    if pad:
        parts.append(jnp.zeros((pad, d), wt.dtype))
    return jnp.concatenate(parts, axis=0)


def _w_in_grad_chunks(dwt, shard_rows, padded_rows, lay):
    d = dwt.shape[-1]
    ql, kl, gw = lay["ql"], lay["kl"], lay["gw"]
    ko = lay["kr_off"]
    rows = jnp.concatenate([dwt[lay["q_off"]:lay["q_off"] + ql], dwt[lay["kv_off"]:lay["kv_off"] + kl],
                            dwt[ko:ko + ROPE_HALF], dwt[ko + 2 * ROPE_HALF:ko + 3 * ROPE_HALF],
                            dwt[:gw], dwt[gw:2 * gw]], axis=0).reshape(4, shard_rows, d)
    rows = jnp.pad(rows, ((0, 0), (0, padded_rows - shard_rows), (0, 0)))
    return jnp.transpose(rows.reshape(4, 2, padded_rows // 2, d), (1, 0, 2, 3)).astype(BF16)


def _w_uq_padded(w, heads):
    w3 = w.reshape(w.shape[0], heads, NOPE_DIM + ROPE_DIM)
    t = _rope_tile(w3[..., NOPE_DIM:NOPE_DIM + ROPE_HALF], w3[..., NOPE_DIM + ROPE_HALF:])
    return jnp.concatenate([w3[..., :NOPE_DIM], t], axis=-1).reshape(w.shape[0], heads * HEAD_PAD)


def _w_uq_grad_unpadded(dw, heads):
    d3 = dw.reshape(dw.shape[0], heads, HEAD_PAD)
    return jnp.concatenate([d3[..., :NOPE_DIM], d3[..., NOPE_DIM:NOPE_DIM + ROPE_HALF],
                            d3[..., NOPE_DIM + 2 * ROPE_HALF:NOPE_DIM + 3 * ROPE_HALF]],
                           axis=-1).reshape(dw.shape[0], heads * (NOPE_DIM + ROPE_DIM))


def _cols_gathered(g):
    return jnp.transpose(g, (1, 0, 2)).reshape(g.shape[1], 4 * g.shape[2])


def _chunks_of_cols(grad):
    r, c4 = grad.shape
    return jnp.transpose(grad.reshape(2, r // 2, 4, c4 // 4), (0, 2, 1, 3)).astype(BF16)


SMALL = ["pre_mix_norm", "q_norm", "kv_norm", "v_ln_gain", "v_ln_bias", "w_spatial", "b_spatial", "attn_out_norm",
         "gmlp_out_norm", "post_mix_norm", "pre_ffn_norm", "post_ffn_norm"]
BIG = ["w_in", "w_uq", "w_ukv", "w_out", "w_gate", "w_up", "w_down"]
GATHER_NOW = ["w_in", "w_uq", "w_ukv"]
GATHER_LATER_1 = ["w_out", "w_gate"]
GATHER_LATER_2 = ["w_up", "w_down"]
REDUCE_FFN = ["w_gate", "w_up", "w_down"]
REDUCE_OUT = ["w_out"]
REDUCE_LAST = ["w_in", "w_uq", "w_ukv"]
TRANSPOSED = ("w_in", "w_gate", "w_up")
ORDER = ["pre_mix_norm", "w_in", "q_norm", "kv_norm", "w_uq", "w_ukv", "v_ln_gain", "v_ln_bias", "w_spatial",
         "b_spatial", "attn_out_norm", "gmlp_out_norm", "w_out", "post_mix_norm", "pre_ffn_norm", "w_gate", "w_up",
         "w_down", "post_ffn_norm"]


def _pack(arrays):
    flat = jnp.concatenate([a.reshape(-1) for a in arrays])
    n = flat.shape[0]
    total = _round_up(n, 8 * LANES)
    if total > n:
        flat = jnp.concatenate([flat, jnp.zeros((total - n,), F32)])
    return flat.reshape(total // LANES, LANES)


def _unpack(packed, like):
    flat = packed.reshape(-1)
    out, off = [], 0
    for a in like:
        out.append(flat[off:off + a.size].reshape(a.shape))
        off += a.size
    return out


def kernel(x, positions, pre_mix_norm, w_in, q_norm, kv_norm, w_uq, w_ukv, v_ln_gain, v_ln_bias, w_spatial, b_spatial, attn_out_norm, gmlp_out_norm, w_out, post_mix_norm, pre_ffn_norm, w_gate, w_up, w_down, post_ffn_norm, loss_target, m_pre_mix_norm, m_w_in, m_q_norm, m_kv_norm, m_w_uq, m_w_ukv, m_v_ln_gain, m_v_ln_bias, m_w_spatial, m_b_spatial, m_attn_out_norm, m_gmlp_out_norm, m_w_out, m_post_mix_norm, m_pre_ffn_norm, m_w_gate, m_w_up, m_w_down, m_post_ffn_norm, v_pre_mix_norm, v_w_in, v_q_norm, v_kv_norm, v_w_uq, v_w_ukv, v_v_ln_gain, v_v_ln_bias, v_w_spatial, v_b_spatial, v_attn_out_norm, v_gmlp_out_norm, v_w_out, v_post_mix_norm, v_pre_ffn_norm, v_w_gate, v_w_up, v_w_down, v_post_ffn_norm):
    args = dict(locals())
    weights = {n: args[n] for n in ORDER}
    m_in = {n: args["m_" + n] for n in ORDER}
    v_in = {n: args["v_" + n] for n in ORDER}

    lay = _layout(w_in, w_uq, w_ukv, v_ln_gain, q_norm, kv_norm)
    heads, gw = lay["heads"], lay["gw"]
    t, d = x.shape[1], x.shape[2]
    tr = 128 if t % 128 == 0 else t
    xs = x.reshape(t, d)
    target = loss_target.reshape(t, d)

    ffs, ins = w_gate.shape[-1], w_in.shape[-1]
    ffp, inp = _round_up(ffs, LANES), _round_up(ins, LANES)
    shards = {n: (jnp.swapaxes(weights[n][0], 0, 1) if n in TRANSPOSED else weights[n][0]).astype(BF16)
              for n in BIG}
    for n, rows in (("w_gate", ffp), ("w_up", ffp), ("w_down", ffp), ("w_in", inp)):
        shards[n] = jnp.pad(shards[n], ((0, rows - shards[n].shape[0]), (0, 0)))
    halved = {n: shards[n].reshape(2, shards[n].shape[0] // 2, shards[n].shape[1]) for n in BIG}
    full = {}

    def pair_sums_of(partial, names, tag):
        from_sibling = _sibling_exchange([partial[n] for n in names], "grads_sibling_exchange_" + tag)
        return [_pair_add(partial[n], r, "pair_add_" + n) for n, r in zip(names, from_sibling)]

    def place(names, lands):
        for n, g in zip(names, lands):
            full[n] = g.reshape((4,) + shards[n].shape)

    place(GATHER_NOW, _all_gather([halved[n] for n in GATHER_NOW]))
    flight_1 = _ici_start("gather", [halved[n] for n in GATHER_LATER_1], "gather_start_1")
    flight_2 = _ici_start("gather", [halved[n] for n in GATHER_LATER_2], "gather_start_2")
    wt_in = _w_in_rows(full["w_in"], ins, lay)
    wb_uq = _w_uq_padded(_cols_gathered(full["w_uq"]), heads)
    wb_ukv = _cols_gathered(full["w_ukv"])

    inv_freq = 1.0 / (ROPE_THETA ** (jnp.arange(0, ROPE_DIM, 2, dtype=F32) / ROPE_DIM))
    ang = positions.reshape(t).astype(F32)[:, None] * inv_freq
    cos, sin = jnp.cos(ang), jnp.sin(ang)
    cos_t = _rope_tile(cos, cos)
    sin_t = _rope_tile(-sin, sin)

    row = lambda a: a.reshape(1, -1)
    g_pre, g_q, g_kv = row(pre_mix_norm), row(q_norm), row(kv_norm)
    g_a, g_g, g_pm = row(attn_out_norm), row(gmlp_out_norm), row(post_mix_norm)
    g_pf, g_po = row(pre_ffn_norm), row(post_ffn_norm)
    ln_g, ln_b = row(v_ln_gain), row(v_ln_bias)
    ws = w_spatial[0].astype(BF16)
    ws_t = jnp.transpose(ws, (0, 2, 1))
    bs_wide = jnp.broadcast_to(b_spatial[0][:, :, None], b_spatial.shape[1:] + (G_HEAD_DIM,))

    xn = _prenorm(xs, g_pre, tr, flight_1[4] + flight_2[4])
    proj = _matmul(xn, wt_in, NT, F32, "proj")
    qn, kvn, kr = _qkv_prep(proj, g_q, g_kv, cos_t, sin_t, lay, tr)
    q = _q_rope(_matmul(qn, wb_uq, NN, F32, "q_up"), cos_t, sin_t, heads, tr)
    kv = _matmul(kvn, wb_ukv, NN, BF16, "kv_up")
    a_out, a_lse = _attn_fwd(q, kv, kr, heads)
    gn = _gmlp_fwd(proj, ln_g, ln_b, ws, bs_wide, g_g, lay)
    mixed = _mix_norm(a_out, gn, g_a, tr)
    _, lands = _ici_wait("gather", *flight_1[:4], mixed, "gather_wait_1")
    place(GATHER_LATER_1, _forward_halves(lands, "gather_forward_1"))
    wb_out = full["w_out"].reshape(-1, d)
    wt_gate = full["w_gate"].reshape(4 * ffp, d)
    mix_out = _matmul(mixed, wb_out, NN, F32, "mix_out")
    h, hn = _post_mix(xs, mix_out, g_pm, g_pf, tr)
    gate = _matmul(hn, wt_gate, NT, BF16, "ffn_gate")
    _, lands = _ici_wait("gather", *flight_2[:4], gate, "gather_wait_2")
    place(GATHER_LATER_2, _forward_halves(lands, "gather_forward_2"))
    wt_up = full["w_up"].reshape(4 * ffp, d)
    wb_down = full["w_down"].reshape(4 * ffp, d)
    up, act = _matmul(hn, wt_up, NT, (BF16, BF16), "ffn_up", extras=[gate], epilogue=_swiglu_of_up)
    ffn = _matmul(act, wb_down, NN, F32, "ffn_down")
    dy, d_ffn, dg_po, loss_vec = _loss_head(h, ffn, target, g_po, tr)

    d_gate, d_up = _matmul(d_ffn, wb_down, NT, (BF16, BF16), "d_act", extras=[gate, up], epilogue=_swiglu_grads)
    gw_down = _matmul(act, d_ffn, TN, BF16, "gw_down", out_chunks=True)
    d_hn = _matmul(d_up, wt_up, NN, F32, "d_hn", extras=[_matmul(d_gate, wt_gate, NN, F32, "d_hn_gate")],
                   epilogue=lambda acc, partial: acc + partial)
    gw_gate = _matmul(d_gate, hn, TN, BF16, "gw_gate", out_chunks=True)
    gw_up = _matmul(d_up, hn, TN, BF16, "gw_up", out_chunks=True)
    pair_ffn = pair_sums_of({"w_gate": gw_gate, "w_up": gw_up, "w_down": gw_down}, REDUCE_FFN, "ffn")
    flight_ffn = _ici_start("scatter", pair_ffn, "scatter_start_ffn")
    dh, d_mo, dg_pf, dg_pm = _post_mix_bwd(mix_out, h, dy, d_hn, g_pm, g_pf, tr, flight_ffn[4])
    d_mixed = _matmul(d_mo, wb_out, NT, F32, "d_mixed")
    gw_out = _matmul(mixed, d_mo, TN, BF16, "gw_out", out_chunks=True)
    pair_out = pair_sums_of({"w_out": gw_out}, REDUCE_OUT, "out")
    flight_out = _ici_start("scatter", pair_out, "scatter_start_out")
    d_a, dg_a = _mix_norm_bwd(a_out, d_mixed, g_a, tr, flight_out[4])
    d_u, d_v, dg_g, d_ln_g, d_ln_b, d_ws, d_bs_wide = _gmlp_bwd(proj, d_mixed, ln_g, ln_b, ws, ws_t, bs_wide, g_g, lay)
    d_bs = _spatial_bias_grad(d_bs_wide)
    d_q, d_kv, d_kr = _attn_bwd(q, kv, kr, a_out, a_lse, d_a, cos_t, sin_t, heads)
    d_qn = _matmul(d_q, wb_uq, NT, F32, "d_qn")
    gw_uq = _matmul(qn, d_q, TN, F32, "gw_uq")
    d_kvn = _matmul(d_kv, wb_ukv, NT, F32, "d_kvn")
    gw_ukv = _matmul(kvn, d_kv, TN, F32, "gw_ukv")
    d_qc, d_kvc, d_krt, dg_q, dg_kv = _qkv_bwd(proj, d_qn, d_kvn, d_kr, g_q, g_kv, cos_t, sin_t, lay, tr)
    parts = [d_u, d_v, d_qc, d_kvc, d_krt]
    pad = lay["in_pad"] - (lay["kr_off"] + LANES)
    if pad:
        parts.append(jnp.zeros((t, pad), BF16))
    d_proj = jnp.concatenate(parts, axis=1)
    d_xn = _matmul(d_proj, wt_in, NN, F32, "d_xn")
    gw_in = _matmul(d_proj, xn, TN, F32, "gw_in")
    grad_x, dg_pre = _prenorm_bwd(xs, d_xn, dh, g_pre, tr)

    pair_mix = pair_sums_of({"w_in": _w_in_grad_chunks(gw_in, ins, inp, lay),
                             "w_uq": _chunks_of_cols(_w_uq_grad_unpadded(gw_uq, heads)),
                             "w_ukv": _chunks_of_cols(gw_ukv)}, REDUCE_LAST, "mix")
    pair_ffn, received_ffn = _ici_wait("scatter", *flight_ffn[:4], pair_mix[-1], "scatter_wait_ffn")
    pair_out, received_out = _ici_wait("scatter", *flight_out[:4], pair_mix[-1], "scatter_wait_out")
    flight_mix = _ici_start("scatter", pair_mix, "scatter_start_mix")
    grads, delta, new_m, new_v = {}, {}, {}, {}

    def finish(names, pair_sums, received, tag, token):
        mine = [_chip_sum(p, r, "chip_sum_" + n, token) for n, p, r in zip(names, pair_sums, received)]
        theirs = _halves_exchange(mine, "grads_halves_exchange_" + tag)
        for n, g_mine, g_theirs in zip(names, mine, theirs):
            shape = weights[n].shape
            if n in TRANSPOSED:
                view = lambda a: jnp.swapaxes(a[0], 0, 1)
                back = lambda o: jnp.swapaxes(o, 0, 1).reshape(shape)
            else:
                view = lambda a: a[0]
                back = lambda o: o.reshape(shape)
            out = _adamw_halves(view(weights[n]), g_mine, g_theirs, view(m_in[n]), view(v_in[n]), "adamw_" + n)
            grads[n], delta[n], new_m[n], new_v[n] = [back(o) for o in out]

    finish(REDUCE_FFN + REDUCE_OUT, pair_ffn + pair_out, received_ffn + received_out, "ffn", flight_mix[4])
    pair_mix, received = _ici_wait("scatter", *flight_mix[:4], new_v[REDUCE_OUT[-1]], "scatter_wait_mix")
    finish(REDUCE_LAST, pair_mix, received, "mix", None)

    small_grads = {"pre_mix_norm": dg_pre, "q_norm": dg_q, "kv_norm": dg_kv, "v_ln_gain": d_ln_g, "v_ln_bias": d_ln_b,
                   "w_spatial": d_ws, "b_spatial": d_bs, "attn_out_norm": dg_a, "gmlp_out_norm": dg_g,
                   "post_mix_norm": dg_pm, "pre_ffn_norm": dg_pf, "post_ffn_norm": dg_po}
    like = [weights[n] for n in SMALL]
    reduced = _small_all_reduce(_pack([small_grads[n] for n in SMALL] + [loss_vec]))
    loss = reduced.reshape(-1)[sum(a.size for a in like)]
    small_g = _pack(_unpack(reduced, like))
    s_delta, s_m, s_v = _adamw(_pack(like), small_g, _pack([m_in[n] for n in SMALL]),
                               _pack([v_in[n] for n in SMALL]), "adamw_small")
    for n, g in zip(SMALL, _unpack(small_g, like)):
        grads[n] = g
    delta.update(zip(SMALL, _unpack(s_delta, like)))
    new_m.update(zip(SMALL, _unpack(s_m, like)))
    new_v.update(zip(SMALL, _unpack(s_v, like)))

    return (loss, grad_x.reshape(x.shape), *[grads[n] for n in ORDER], *[delta[n] for n in ORDER],
            *[new_m[n] for n in ORDER], *[new_v[n] for n in ORDER])
```

```python
import functools
import math

import jax
import jax.numpy as jnp
from jax import lax
from jax.experimental import pallas as pl
from jax.experimental.pallas import tpu as pltpu

F32 = jnp.float32
BF16 = jnp.bfloat16
MESH = pl.DeviceIdType.MESH

NOPE_DIM = 128
ROPE_DIM = 64
ROPE_HALF = ROPE_DIM // 2
V_DIM = 128
HEAD_PAD = 256
G_HEAD_DIM = 128
CHUNK = 128
ROPE_THETA = 10000.0
EPS = 1e-6
ADAM_LR = 0.001
ADAM_B1 = 0.9
ADAM_B2 = 0.999
ADAM_EPS = 1e-08
ADAM_WD = 0.01
ADAM_STEP = 10

LANES = 128
MATMUL_TILE = 1024
WIDE_TILE = 1408
VMEM_LIMIT_BYTES = 48 * 1024 * 1024

NN = (((1,), (0,)), ((), ()))
NT = (((1,), (1,)), ((), ()))
TN = (((0,), (0,)), ((), ()))


def _params(semantics):
    return pltpu.CompilerParams(dimension_semantics=semantics, vmem_limit_bytes=VMEM_LIMIT_BYTES)


def _tile(n, cap=MATMUL_TILE):
    if n <= cap:
        return n
    if cap == MATMUL_TILE and n % WIDE_TILE == 0:
        return WIDE_TILE
    t = cap - cap % LANES
    while n % t:
        t -= LANES
    assert t > 0, n
    return t


def _round_up(n, m):
    return (n + m - 1) // m * m


def _matmul(a, b, dims, out_dtype, name, extras=(), epilogue=None, out_chunks=None, token=None):
    if dims is NN:
        (m, k), (k2, n) = a.shape, b.shape
    elif dims is NT:
        (m, k), (n, k2) = a.shape, b.shape
    else:
        (k, m), (k2, n) = a.shape, b.shape
    assert k == k2, (a.shape, b.shape, name)
    tm, tn, tk = _tile(m // 8 if out_chunks else m), _tile(n), _tile(k)
    if len(extras) + (len(out_dtype) if isinstance(out_dtype, tuple) else 1) > 2:
        tm = _tile(m, MATMUL_TILE // 2)
    nk = k // tk

    out_dtypes = out_dtype if isinstance(out_dtype, tuple) else (out_dtype,)
    n_extra = len(extras)

    def body(*refs):
        a_ref, b_ref = refs[:2]
        extra_refs = refs[2:2 + n_extra]
        out_refs = refs[2 + n_extra + (token is not None):-1]
        acc_ref = refs[-1]
        kk = pl.program_id(2)

        @pl.when(kk == 0)
        def _():
            acc_ref[...] = jnp.zeros_like(acc_ref)

        acc_ref[...] += lax.dot_general(a_ref[...], b_ref[...], dims, preferred_element_type=F32)

        @pl.when(kk == nk - 1)
        def _():
            r = acc_ref[...]
            if epilogue is not None:
                r = epilogue(r, *[e[...] for e in extra_refs])
            for o_ref, val in zip(out_refs, r if isinstance(r, tuple) else (r,)):
                o_ref[...] = val.astype(o_ref.dtype)

    if dims is TN:
        a_spec = pl.BlockSpec((tk, tm), lambda i, j, kk: (kk, i))
    else:
        a_spec = pl.BlockSpec((tm, tk), lambda i, j, kk: (i, kk))
    if dims is NT:
        b_spec = pl.BlockSpec((tn, tk), lambda i, j, kk: (j, kk))
    else:
        b_spec = pl.BlockSpec((tk, tn), lambda i, j, kk: (kk, j))
    if not out_chunks:
        o_spec = pl.BlockSpec((tm, tn), lambda i, j, kk: (i, j))
        o_shape = (m, n)
    else:
        pi = m // 8 // tm
        o_spec = pl.BlockSpec((None, None, tm, tn), lambda i, j, kk: ((i // pi) % 2, i // (2 * pi), i % pi, j))
        o_shape = (2, 4, m // 8, n)
    assert not (extras and out_chunks)
    tokens = [] if token is None else [token]
    out = pl.pallas_call(
        body, name=name, grid=(m // tm, n // tn, nk),
        in_specs=[a_spec, b_spec] + [o_spec] * n_extra + [pl.BlockSpec(memory_space=pl.ANY)] * len(tokens),
        out_specs=[o_spec] * len(out_dtypes),
        out_shape=[jax.ShapeDtypeStruct(o_shape, dt) for dt in out_dtypes],
        scratch_shapes=[pltpu.VMEM((tm, tn), F32)],
        compiler_params=_params(("parallel", "parallel", "arbitrary")),
    )(a, b, *extras, *tokens)
    return tuple(out) if isinstance(out_dtype, tuple) else out[0]


def _row_call(body, name, rows, tr, row_ins, par_ins, row_outs, acc_outs):
    def col(i, cb):
        return (i, cb)

    def whole(i, nd):
        return (0,) * nd

    in_specs = [pl.BlockSpec((tr, w), functools.partial(col, cb=cb)) for (_, w, cb) in row_ins]
    in_specs += [pl.BlockSpec(a.shape, functools.partial(whole, nd=a.ndim)) for a in par_ins]
    out_specs = [pl.BlockSpec((tr, w), lambda i: (i, 0)) for (w, _) in row_outs]
    out_specs += [pl.BlockSpec(s, functools.partial(whole, nd=len(s))) for (s, _) in acc_outs]
    out_shape = [jax.ShapeDtypeStruct((rows, w), dt) for (w, dt) in row_outs]
    out_shape += [jax.ShapeDtypeStruct(s, dt) for (s, dt) in acc_outs]
    return pl.pallas_call(
        body, name=name, grid=(rows // tr,), in_specs=in_specs, out_specs=out_specs, out_shape=out_shape,
        compiler_params=_params(("arbitrary",) if acc_outs else ("parallel",)),
    )(*[a for (a, _, _) in row_ins], *par_ins)


def _accumulate(ref, val):
    i = pl.program_id(0)

    @pl.when(i == 0)
    def _():
        ref[...] = val

    @pl.when(i > 0)
    def _():
        ref[...] += val


def _colsum(v):
    return jnp.sum(v, axis=0, keepdims=True)


def _rms_fwd(x, g):
    r = lax.rsqrt(jnp.mean(x * x, axis=-1, keepdims=True) + EPS)
    return x * r * g


def _rms_bwd(x, g, dy):
    r = lax.rsqrt(jnp.mean(x * x, axis=-1, keepdims=True) + EPS)
    xh = x * r
    dxh = dy * g
    dx = r * (dxh - xh * jnp.mean(dxh * xh, axis=-1, keepdims=True))
    return dx, dy * xh


_GELU_C = math.sqrt(2.0 / math.pi)
_GELU_A = 0.044715


def _gelu(x):
    return 0.5 * x * (1.0 + jnp.tanh(_GELU_C * (x + _GELU_A * (x * x * x))))


def _gelu_grad(x):
    t = jnp.tanh(_GELU_C * (x + _GELU_A * (x * x * x)))
    return 0.5 * (1.0 + t) + 0.5 * x * (1.0 - t * t) * (_GELU_C * (1.0 + 3.0 * _GELU_A * (x * x)))


def _sigmoid(x):
    return 1.0 / (1.0 + jnp.exp(-x))


def _rope_fwd(t, cos_t, sin_t):
    return t * cos_t + pltpu.roll(t, 2 * ROPE_HALF, 1) * sin_t


def _rope_bwd(dt, cos_t, sin_t):
    return dt * cos_t - pltpu.roll(dt, 2 * ROPE_HALF, 1) * sin_t


def _prenorm(x, g, tr, token):
    def body(x_ref, g_ref, token_ref, o_ref):
        o_ref[...] = _rms_fwd(x_ref[...], g_ref[...]).astype(BF16)

    t, d = x.shape
    return _row_call(body, "prenorm", t, tr, [(x, d, 0)], [g, token], [(d, BF16)], [])[0]


def _qkv_prep(proj, g_q, g_kv, cos_t, sin_t, lay, tr):
    ql, kl = lay["ql"], lay["kl"]

    def body(q_ref, kv_ref, kr_ref, cos_ref, sin_ref, gq_ref, gkv_ref, qn_ref, kvn_ref, kro_ref):
        qn_ref[...] = _rms_fwd(q_ref[...], gq_ref[...]).astype(BF16)
        kvn_ref[...] = _rms_fwd(kv_ref[...], gkv_ref[...]).astype(BF16)
        kro_ref[...] = _rope_fwd(kr_ref[...], cos_ref[...], sin_ref[...]).astype(BF16)

    t = proj.shape[0]
    return _row_call(
        body, "qkv_prep", t, tr,
        [(proj, ql, lay["q_off"] // ql), (proj, kl, lay["kv_off"] // kl), (proj, LANES, lay["kr_off"] // LANES),
         (cos_t, LANES, 0), (sin_t, LANES, 0)],
        [g_q, g_kv], [(ql, BF16), (kl, BF16), (LANES, BF16)], [])


def _q_rope(q, cos_t, sin_t, heads, tr):
    def body(q_ref, cos_ref, sin_ref, o_ref):
        c, s = cos_ref[...], sin_ref[...]
        for h in range(heads):
            lo = h * HEAD_PAD
            o_ref[:, lo:lo + NOPE_DIM] = q_ref[:, lo:lo + NOPE_DIM].astype(BF16)
            o_ref[:, lo + NOPE_DIM:lo + HEAD_PAD] = _rope_fwd(q_ref[:, lo + NOPE_DIM:lo + HEAD_PAD], c, s).astype(BF16)

    t, w = q.shape
    return _row_call(body, "q_rope", t, tr, [(q, w, 0), (cos_t, LANES, 0), (sin_t, LANES, 0)], [], [(w, BF16)], [])[0]


ATTN_SCALE = 1.0 / math.sqrt(NOPE_DIM + ROPE_DIM)
ATTN_EXP2_SCALE = ATTN_SCALE * math.log2(math.e)


def _attn_tile(t, cap):
    tq = cap
    while t % tq:
        tq //= 2
    return tq


def _attn_fwd(q, kv, kr, heads):
    t = q.shape[0]
    tq = _attn_tile(t, 256)

    def body(q_ref, kv_ref, kr_ref, o_ref, lse_ref, kcat):
        @pl.when(pl.program_id(1) == 0)
        def _():
            kcat[:, :NOPE_DIM] = kv_ref[:, :NOPE_DIM]
            kcat[:, NOPE_DIM:] = kr_ref[...]

        s = lax.dot_general(q_ref[...], kcat[...], NT, preferred_element_type=F32)
        m = jnp.max(s, axis=-1, keepdims=True)
        p = jnp.exp2((s - m) * ATTN_EXP2_SCALE)
        l = jnp.sum(p, axis=-1, keepdims=True)
        o_ref[...] = jnp.dot(p.astype(BF16), kv_ref[:, NOPE_DIM:], preferred_element_type=F32) * (1.0 / l)
        lse_ref[...] = jnp.broadcast_to(m * ATTN_EXP2_SCALE + jnp.log(l) * math.log2(math.e), (tq, V_DIM))

    out_spec = pl.BlockSpec((tq, V_DIM), lambda h, i: (i, h))
    out_shape = jax.ShapeDtypeStruct((t, heads * V_DIM), F32)
    return pl.pallas_call(
        body, name="attn_fwd", grid=(heads, t // tq),
        in_specs=[pl.BlockSpec((tq, HEAD_PAD), lambda h, i: (i, h)),
                  pl.BlockSpec((t, HEAD_PAD), lambda h, i: (0, h)),
                  pl.BlockSpec((t, LANES), lambda h, i: (0, 0))],
        out_specs=[out_spec, out_spec], out_shape=[out_shape, out_shape],
        scratch_shapes=[pltpu.VMEM((t, HEAD_PAD), BF16)],
        compiler_params=_params(("arbitrary", "arbitrary")),
    )(q, kv, kr)


def _attn_bwd(q, kv, kr, out, lse, d_out, cos_t, sin_t, heads):
    t = q.shape[0]
    tq = _attn_tile(t, 256)
    nq = t // tq

    def body(q_ref, kv_ref, kr_ref, o_ref, lse_ref, do_ref, cos_ref, sin_ref, dq_ref, dkv_ref, dkr_ref,
             kcat, dk_acc, dv_acc):
        h, i = pl.program_id(0), pl.program_id(1)

        @pl.when(i == 0)
        def _():
            kcat[:, :NOPE_DIM] = kv_ref[:, :NOPE_DIM]
            kcat[:, NOPE_DIM:] = kr_ref[...]
            dk_acc[...] = jnp.zeros_like(dk_acc)
            dv_acc[...] = jnp.zeros_like(dv_acc)

        @pl.when((h == 0) & (i == 0))
        def _():
            dkr_ref[...] = jnp.zeros_like(dkr_ref)

        qb, dob = q_ref[...], do_ref[...]
        row_term = jnp.sum(dob.astype(F32) * o_ref[...], axis=-1, keepdims=True)
        s = lax.dot_general(qb, kcat[...], NT, preferred_element_type=F32)
        dp = lax.dot_general(dob, kv_ref[:, NOPE_DIM:], NT, preferred_element_type=F32)
        p = jnp.exp2(s * ATTN_EXP2_SCALE - lse_ref[:, :1])
        ds = (p * (dp - row_term)).astype(BF16)
        dv_acc[...] += lax.dot_general(p.astype(BF16), dob, TN, preferred_element_type=F32)
        dq = jnp.dot(ds, kcat[...], preferred_element_type=F32) * ATTN_SCALE
        dq_ref[:, :NOPE_DIM] = dq[:, :NOPE_DIM].astype(BF16)
        dq_ref[:, NOPE_DIM:] = _rope_bwd(dq[:, NOPE_DIM:], cos_ref[...], sin_ref[...]).astype(BF16)
        dk_acc[...] += lax.dot_general(ds, qb, TN, preferred_element_type=F32)

        @pl.when(i == nq - 1)
        def _():
            dkv_ref[:, :NOPE_DIM] = (dk_acc[:, :NOPE_DIM] * ATTN_SCALE).astype(BF16)
            dkv_ref[:, NOPE_DIM:] = dv_acc[...].astype(BF16)
            dkr_ref[...] += dk_acc[:, NOPE_DIM:] * ATTN_SCALE

    return pl.pallas_call(
        body, name="attn_bwd", grid=(heads, nq),
        in_specs=[pl.BlockSpec((tq, HEAD_PAD), lambda h, i: (i, h)),
                  pl.BlockSpec((t, HEAD_PAD), lambda h, i: (0, h)),
                  pl.BlockSpec((t, LANES), lambda h, i: (0, 0)),
                  pl.BlockSpec((tq, V_DIM), lambda h, i: (i, h)),
                  pl.BlockSpec((tq, V_DIM), lambda h, i: (i, h)),
                  pl.BlockSpec((tq, V_DIM), lambda h, i: (i, h)),
                  pl.BlockSpec((tq, LANES), lambda h, i: (i, 0)),
                  pl.BlockSpec((tq, LANES), lambda h, i: (i, 0))],
        out_specs=[pl.BlockSpec((tq, HEAD_PAD), lambda h, i: (i, h)),
                   pl.BlockSpec((t, HEAD_PAD), lambda h, i: (0, h)),
                   pl.BlockSpec((t, LANES), lambda h, i: (0, 0))],
        out_shape=[jax.ShapeDtypeStruct((t, heads * HEAD_PAD), BF16),
                   jax.ShapeDtypeStruct((t, heads * HEAD_PAD), BF16),
                   jax.ShapeDtypeStruct((t, LANES), F32)],
        scratch_shapes=[pltpu.VMEM((t, HEAD_PAD), BF16), pltpu.VMEM((t, HEAD_PAD), F32), pltpu.VMEM((t, V_DIM), F32)],
        compiler_params=_params(("arbitrary", "arbitrary")),
    )(q, kv, kr, out, lse, d_out, cos_t, sin_t)


def _layer_norm_parts(x):
    mu = jnp.mean(x, axis=-1, keepdims=True)
    xc = x - mu
    r = lax.rsqrt(jnp.mean(xc * xc, axis=-1, keepdims=True) + EPS)
    return xc * r, r


def _gmlp_fwd(proj, ln_g, ln_b, w_s, b_sb, g_out_norm, lay, token):
    gw = lay["gw"]
    g_heads = gw // G_HEAD_DIM

    def body(u_ref, v_ref, lng_ref, lnb_ref, ws_ref, bs_ref, gn_ref, token_ref, o_ref, gate_ref):
        gu = _gelu(u_ref[...])
        vh, _ = _layer_norm_parts(_gelu(v_ref[...]))
        vln = (vh * lng_ref[...] + lnb_ref[...]).astype(BF16)
        for g in range(g_heads):
            cols = slice(g * G_HEAD_DIM, (g + 1) * G_HEAD_DIM)
            s = jnp.dot(ws_ref[g], vln[:, cols], preferred_element_type=F32) + bs_ref[g]
            gate_ref[:, cols] = gu[:, cols] * s
        o_ref[...] = _rms_fwd(gate_ref[...], gn_ref[...]).astype(BF16)

    t = proj.shape[0]
    in_specs = [pl.BlockSpec((CHUNK, gw), lambda i: (i, 0)), pl.BlockSpec((CHUNK, gw), lambda i: (i, 1))]
    pars = [ln_g, ln_b, w_s, b_sb, g_out_norm, token]
    in_specs += [pl.BlockSpec(a.shape, functools.partial(lambda i, nd: (0,) * nd, nd=a.ndim)) for a in pars]
    return pl.pallas_call(
        body, name="gmlp_fwd", grid=(t // CHUNK,), in_specs=in_specs,
        out_specs=pl.BlockSpec((CHUNK, gw), lambda i: (i, 0)),
        out_shape=jax.ShapeDtypeStruct((t, gw), BF16),
        scratch_shapes=[pltpu.VMEM((CHUNK, gw), F32)],
        compiler_params=_params(("parallel",)),
    )(proj, proj, *pars)


def _gmlp_bwd(proj, d_mixed, ln_g, ln_b, w_s, w_st, b_sb, g_out_norm, lay):
    gw = lay["gw"]
    g_heads = gw // G_HEAD_DIM
    aw_blocks = lay["aw"] // gw

    def body(u_ref, v_ref, dm_ref, lng_ref, lnb_ref, ws_ref, wst_ref, bs_ref, gn_ref,
             du_ref, dv_ref, dgn_ref, dlng_ref, dlnb_ref, dws_ref, dbs_ref, gate_ref, s_ref, dvln_ref):
        i = pl.program_id(0)
        u, v = u_ref[...], v_ref[...]
        gu, gv = _gelu(u), _gelu(v)
        vh, r_ln = _layer_norm_parts(gv)
        vln = (vh * lng_ref[...] + lnb_ref[...]).astype(BF16)
        for g in range(g_heads):
            cols = slice(g * G_HEAD_DIM, (g + 1) * G_HEAD_DIM)
            s = jnp.dot(ws_ref[g], vln[:, cols], preferred_element_type=F32) + bs_ref[g]
            s_ref[:, cols] = s
            gate_ref[:, cols] = gu[:, cols] * s
        d_gate, dgn = _rms_bwd(gate_ref[...], gn_ref[...], dm_ref[...])
        _accumulate(dgn_ref, _colsum(dgn))
        du_ref[...] = (d_gate * s_ref[...] * _gelu_grad(u)).astype(BF16)
        d_s = d_gate * gu
        d_sb = d_s.astype(BF16)
        for g in range(g_heads):
            cols = slice(g * G_HEAD_DIM, (g + 1) * G_HEAD_DIM)
            dw = lax.dot_general(d_sb[:, cols], vln[:, cols], NT, preferred_element_type=F32)

            @pl.when(i == 0)
            def _():
                dws_ref[g] = dw
                dbs_ref[g] = d_s[:, cols]

            @pl.when(i > 0)
            def _():
                dws_ref[g] += dw
                dbs_ref[g] += d_s[:, cols]

            dvln_ref[:, cols] = jnp.dot(wst_ref[g], d_sb[:, cols], preferred_element_type=F32)
        d_vln = dvln_ref[...]
        _accumulate(dlng_ref, _colsum(d_vln * vh))
        _accumulate(dlnb_ref, _colsum(d_vln))
        d_vh = d_vln * lng_ref[...]
        d_gv = r_ln * (d_vh - jnp.mean(d_vh, axis=-1, keepdims=True)
                       - vh * jnp.mean(d_vh * vh, axis=-1, keepdims=True))
        dv_ref[...] = (d_gv * _gelu_grad(v)).astype(BF16)

    t = proj.shape[0]
    whole = lambda a: pl.BlockSpec(a.shape, functools.partial(lambda i, nd: (0,) * nd, nd=a.ndim))
    pars = [ln_g, ln_b, w_s, w_st, b_sb, g_out_norm]
    hshape = (g_heads, CHUNK, CHUNK)
    return pl.pallas_call(
        body, name="gmlp_bwd", grid=(t // CHUNK,),
        in_specs=[pl.BlockSpec((CHUNK, gw), lambda i: (i, 0)), pl.BlockSpec((CHUNK, gw), lambda i: (i, 1)),
                  pl.BlockSpec((CHUNK, gw), lambda i: (i, aw_blocks))] + [whole(a) for a in pars],
        out_specs=[pl.BlockSpec((CHUNK, gw), lambda i: (i, 0)), pl.BlockSpec((CHUNK, gw), lambda i: (i, 0)),
                   pl.BlockSpec((1, gw), lambda i: (0, 0)), pl.BlockSpec((1, gw), lambda i: (0, 0)),
                   pl.BlockSpec((1, gw), lambda i: (0, 0)),
                   pl.BlockSpec(hshape, lambda i: (0, 0, 0)), pl.BlockSpec(hshape, lambda i: (0, 0, 0))],
        out_shape=[jax.ShapeDtypeStruct((t, gw), BF16), jax.ShapeDtypeStruct((t, gw), BF16),
                   jax.ShapeDtypeStruct((1, gw), F32), jax.ShapeDtypeStruct((1, gw), F32),
                   jax.ShapeDtypeStruct((1, gw), F32),
                   jax.ShapeDtypeStruct(hshape, F32), jax.ShapeDtypeStruct(hshape, F32)],
        scratch_shapes=[pltpu.VMEM((CHUNK, gw), F32), pltpu.VMEM((CHUNK, gw), F32), pltpu.VMEM((CHUNK, gw), F32)],
        compiler_params=_params(("arbitrary",)),
    )(proj, proj, d_mixed, *pars)


def _spatial_bias_grad(dbs_wide):
    g_heads = dbs_wide.shape[0]

    def body(x_ref, o_ref):
        for g in range(g_heads):
            o_ref[g:g + 1, :] = jnp.sum(x_ref[g].T, axis=0, keepdims=True)

    return pl.pallas_call(
        body, name="spatial_bias_grad", out_shape=jax.ShapeDtypeStruct((g_heads, CHUNK), F32),
        in_specs=[pl.BlockSpec(memory_space=pltpu.VMEM)], out_specs=pl.BlockSpec(memory_space=pltpu.VMEM),
    )(dbs_wide)


def _mix_norm(a_out, gn, g_a, tr):
    aw = a_out.shape[1]
    gw = gn.shape[1]

    def body(a_ref, gn_ref, g_ref, o_ref):
        o_ref[:, :aw] = _rms_fwd(a_ref[...], g_ref[...]).astype(BF16)
        o_ref[:, aw:] = gn_ref[...]

    t = a_out.shape[0]
    return _row_call(body, "mix_norm", t, tr, [(a_out, aw, 0), (gn, gw, 0)], [g_a], [(aw + gw, BF16)], [])[0]


def _mix_norm_bwd(a_out, d_mixed, g_a, tr, token):
    aw = a_out.shape[1]

    def body(a_ref, dm_ref, g_ref, token_ref, da_ref, dg_ref):
        dx, dg = _rms_bwd(a_ref[...], g_ref[...], dm_ref[...])
        da_ref[...] = dx.astype(BF16)
        _accumulate(dg_ref, _colsum(dg))

    t = a_out.shape[0]
    return _row_call(body, "mix_norm_bwd", t, tr, [(a_out, aw, 0), (d_mixed, aw, 0)], [g_a, token],
                     [(aw, BF16)], [((1, aw), F32)])


def _post_mix(x, mix_out, g_pm, g_pf, tr):
    def body(x_ref, mo_ref, gpm_ref, gpf_ref, h_ref, hn_ref):
        h = x_ref[...] + _rms_fwd(mo_ref[...], gpm_ref[...])
        h_ref[...] = h
        hn_ref[...] = _rms_fwd(h, gpf_ref[...]).astype(BF16)

    t, d = x.shape
    return _row_call(body, "post_mix", t, tr, [(x, d, 0), (mix_out, d, 0)], [g_pm, g_pf], [(d, F32), (d, BF16)], [])


def _post_mix_bwd(mix_out, h, dy, d_hn, g_pm, g_pf, tr, token):
    def body(mo_ref, h_ref, dy_ref, dhn_ref, gpm_ref, gpf_ref, token_ref, dh_ref, dmo_ref, dgpf_ref, dgpm_ref):
        dx, dg = _rms_bwd(h_ref[...], gpf_ref[...], dhn_ref[...])
        dh = dy_ref[...] + dx
        dh_ref[...] = dh
        _accumulate(dgpf_ref, _colsum(dg))
        dmo, dg2 = _rms_bwd(mo_ref[...], gpm_ref[...], dh)
        dmo_ref[...] = dmo.astype(BF16)
        _accumulate(dgpm_ref, _colsum(dg2))

    t, d = h.shape
    return _row_call(body, "post_mix_bwd", t, tr, [(mix_out, d, 0), (h, d, 0), (dy, d, 0), (d_hn, d, 0)],
                     [g_pm, g_pf, token], [(d, F32), (d, BF16)], [((1, d), F32), ((1, d), F32)])


def _swiglu(gate, up):
    t, f = gate.shape
    tr, tf = _tile(t, 512), _tile(f, 2048)

    def body(g_ref, u_ref, o_ref):
        g = g_ref[...].astype(F32)
        o_ref[...] = (g * _sigmoid(g) * u_ref[...].astype(F32)).astype(BF16)

    spec = pl.BlockSpec((tr, tf), lambda i, j: (i, j))
    return pl.pallas_call(body, name="swiglu", grid=(t // tr, f // tf), in_specs=[spec, spec], out_specs=spec,
                          out_shape=jax.ShapeDtypeStruct((t, f), BF16),
                          compiler_params=_params(("parallel", "parallel")))(gate, up)


def _swiglu_bwd(gate, up, d_act):
    t, f = gate.shape
    tr, tf = _tile(t, 512), _tile(f, 2048)

    def body(g_ref, u_ref, da_ref, dg_ref, du_ref):
        g, u, da = g_ref[...].astype(F32), u_ref[...].astype(F32), da_ref[...].astype(F32)
        sg = _sigmoid(g)
        du_ref[...] = (da * (g * sg)).astype(BF16)
        dg_ref[...] = (da * u * (sg * (1.0 + g * (1.0 - sg)))).astype(BF16)

    spec = pl.BlockSpec((tr, tf), lambda i, j: (i, j))
    shape = jax.ShapeDtypeStruct((t, f), BF16)
    return pl.pallas_call(body, name="swiglu_bwd", grid=(t // tr, f // tf), in_specs=[spec, spec, spec],
                          out_specs=[spec, spec], out_shape=[shape, shape],
                          compiler_params=_params(("parallel", "parallel")))(gate, up, d_act)


def _loss_head(h, ffn, target, g_po, tr):
    t, d = h.shape

    def body(h_ref, f_ref, t_ref, g_ref, dy_ref, df_ref, dg_ref, loss_ref):
        f = f_ref[...]
        err = h_ref[...] + _rms_fwd(f, g_ref[...]) - t_ref[...]
        dy = err * (1.0 / d)
        dy_ref[...] = dy
        df, dg = _rms_bwd(f, g_ref[...], dy)
        df_ref[...] = df.astype(BF16)
        _accumulate(dg_ref, _colsum(dg))
        sq = jnp.sum(_colsum(err * err), axis=-1, keepdims=True) * (0.5 / d)
        _accumulate(loss_ref, jnp.broadcast_to(sq, (1, LANES)))

    return _row_call(body, "loss_head", t, tr, [(h, d, 0), (ffn, d, 0), (target, d, 0)], [g_po],
                     [(d, F32), (d, BF16)], [((1, d), F32), ((1, LANES), F32)])


def _qkv_bwd(proj, d_qn, d_kvn, d_kr, g_q, g_kv, cos_t, sin_t, lay, tr):
    ql, kl = lay["ql"], lay["kl"]

    def body(q_ref, kv_ref, dqn_ref, dkvn_ref, dkr_ref, cos_ref, sin_ref, gq_ref, gkv_ref,
             dq_ref, dkv_ref, dkt_ref, dgq_ref, dgkv_ref):
        dx, dg = _rms_bwd(q_ref[...], gq_ref[...], dqn_ref[...])
        dq_ref[...] = dx.astype(BF16)
        _accumulate(dgq_ref, _colsum(dg))
        dx, dg = _rms_bwd(kv_ref[...], gkv_ref[...], dkvn_ref[...])
        dkv_ref[...] = dx.astype(BF16)
        _accumulate(dgkv_ref, _colsum(dg))
        dkt_ref[...] = _rope_bwd(dkr_ref[...], cos_ref[...], sin_ref[...]).astype(BF16)

    t = proj.shape[0]
    return _row_call(
        body, "qkv_bwd", t, tr,
        [(proj, ql, lay["q_off"] // ql), (proj, kl, lay["kv_off"] // kl), (d_qn, ql, 0), (d_kvn, kl, 0),
         (d_kr, LANES, 0), (cos_t, LANES, 0), (sin_t, LANES, 0)],
        [g_q, g_kv], [(ql, BF16), (kl, BF16), (LANES, BF16)], [((1, ql), F32), ((1, kl), F32)])


def _prenorm_bwd(x, d_xn, dh, g, tr):
    def body(x_ref, dxn_ref, dh_ref, g_ref, gx_ref, dg_ref):
        dx, dg = _rms_bwd(x_ref[...], g_ref[...], dxn_ref[...])
        gx_ref[...] = dh_ref[...] + dx
        _accumulate(dg_ref, _colsum(dg))

    t, d = x.shape
    return _row_call(body, "prenorm_bwd", t, tr, [(x, d, 0), (d_xn, d, 0), (dh, d, 0)], [g],
                     [(d, F32)], [((1, d), F32)])


def _adam_rows(rows, cols):
    cap = max(8, (256 * 1024) // cols // 8 * 8)
    tr = min(rows, cap)
    while rows % tr:
        tr -= 8
    return tr


def _adamw(w, g, m, v, name):
    rows, cols = w.shape
    tr = _adam_rows(rows, cols)

    def body(w_ref, g_ref, m_ref, v_ref, d_ref, mo_ref, vo_ref):
        g = g_ref[...]
        m2 = ADAM_B1 * m_ref[...] + (1.0 - ADAM_B1) * g
        v2 = ADAM_B2 * v_ref[...] + (1.0 - ADAM_B2) * (g * g)
        m_hat = m2 / (1.0 - ADAM_B1 ** ADAM_STEP)
        v_hat = v2 / (1.0 - ADAM_B2 ** ADAM_STEP)
        d_ref[...] = -ADAM_LR * (m_hat / (jnp.sqrt(v_hat) + ADAM_EPS) + ADAM_WD * w_ref[...])
        mo_ref[...] = m2
        vo_ref[...] = v2

    spec = pl.BlockSpec((tr, cols), lambda i: (i, 0))
    shape = jax.ShapeDtypeStruct((rows, cols), F32)
    return pl.pallas_call(body, name=name, grid=(rows // tr,), in_specs=[spec] * 4, out_specs=[spec] * 3,
                          out_shape=[shape] * 3, compiler_params=_params(("parallel",)))(w, g, m, v)


def _adamw_halves(w, g_mine, g_theirs, m, v, name):
    rows, cols = w.shape
    rh = g_mine.shape[0]
    tr = _adam_rows(math.gcd(rows, rh), cols)
    per_half = rh // tr
    my_c = jnp.reshape(lax.axis_index("c"), (1,)).astype(jnp.int32)

    def body(c_ref, w_ref, gm_ref, gt_ref, m_ref, v_ref, g_ref, d_ref, mo_ref, vo_ref):
        mine = (pl.program_id(0) // per_half) == c_ref[0]
        g = jnp.where(mine, gm_ref[...], gt_ref[...])
        m2 = ADAM_B1 * m_ref[...] + (1.0 - ADAM_B1) * g
        v2 = ADAM_B2 * v_ref[...] + (1.0 - ADAM_B2) * (g * g)
        m_hat = m2 / (1.0 - ADAM_B1 ** ADAM_STEP)
        v_hat = v2 / (1.0 - ADAM_B2 ** ADAM_STEP)
        g_ref[...] = g
        d_ref[...] = -ADAM_LR * (m_hat / (jnp.sqrt(v_hat) + ADAM_EPS) + ADAM_WD * w_ref[...])
        mo_ref[...] = m2
        vo_ref[...] = v2

    def half_spec(is_mine):
        def index(i, c_ref):
            used = ((i // per_half) == c_ref[0]) if is_mine else ((i // per_half) != c_ref[0])
            return (jnp.where(used, i % per_half, 0), 0)
        return pl.BlockSpec((tr, cols), index)

    spec = pl.BlockSpec((tr, cols), lambda i, c_ref: (i, 0))
    shape = jax.ShapeDtypeStruct((rows, cols), F32)
    grid_spec = pltpu.PrefetchScalarGridSpec(
        num_scalar_prefetch=1, grid=(rows // tr,),
        in_specs=[spec, half_spec(True), half_spec(False), spec, spec], out_specs=[spec] * 4)
    return pl.pallas_call(body, name=name, grid_spec=grid_spec, out_shape=[shape] * 4,
                          compiler_params=_params(("parallel",)))(my_c, w, g_mine, g_theirs, m, v)


def _pair_add(parts, theirs, name):
    _, n, r, c = parts.shape
    tr = _adam_rows(r, c)
    my_c = jnp.reshape(lax.axis_index("c"), (1,)).astype(jnp.int32)

    def body(c_ref, a_ref, b_ref, o_ref):
        o_ref[0] = (a_ref[0, 0].astype(F32) + b_ref[0].astype(F32)).astype(BF16)

    spec = pl.BlockSpec((1, tr, c), lambda k, i, c_ref: (k, i, 0))
    grid_spec = pltpu.PrefetchScalarGridSpec(
        num_scalar_prefetch=1, grid=(n, r // tr),
        in_specs=[pl.BlockSpec((1, 1, tr, c), lambda k, i, c_ref: (c_ref[0], k, i, 0)), spec], out_specs=spec)
    return pl.pallas_call(body, name=name, grid_spec=grid_spec, out_shape=jax.ShapeDtypeStruct((n, r, c), BF16),
                          compiler_params=_params(("parallel", "parallel")))(my_c, parts, theirs)


def _chip_sum(pair_sums, received, name, token=None):
    _, r, c = pair_sums.shape
    tr = _adam_rows(r, c)
    own = 2 * lax.axis_index("x") + lax.axis_index("y")

    def body(own_ref, p_ref, r0_ref, r1_ref, r2_ref, *rest):
        o_ref = rest[-1]
        acc = p_ref[0].astype(F32) + r0_ref[0].astype(F32)
        acc = acc + r1_ref[0].astype(F32)
        o_ref[...] = acc + r2_ref[0].astype(F32)

    def rspec(j):
        return pl.BlockSpec((1, tr, c), functools.partial(lambda i, own_ref, j: (j, i, 0), j=j))

    extra = [] if token is None else [token]
    grid_spec = pltpu.PrefetchScalarGridSpec(
        num_scalar_prefetch=1, grid=(r // tr,),
        in_specs=[pl.BlockSpec((1, tr, c), lambda i, own_ref: (own_ref[0], i, 0)), rspec(0), rspec(1), rspec(2)]
        + [pl.BlockSpec(memory_space=pl.ANY)] * len(extra),
        out_specs=pl.BlockSpec((tr, c), lambda i, own_ref: (i, 0)))
    return pl.pallas_call(body, name=name, grid_spec=grid_spec, out_shape=jax.ShapeDtypeStruct((r, c), F32),
                          compiler_params=_params(("parallel",)))(
        jnp.reshape(own, (1,)).astype(jnp.int32), pair_sums, received, received, received, *extra)


def _mesh_place():
    x, y, c = lax.axis_index("x"), lax.axis_index("y"), lax.axis_index("c")
    other_chips = [(1 - x, y), (x, 1 - y), (1 - x, 1 - y)]
    return x, y, c, other_chips


def _hbm_specs(n):
    return [pl.BlockSpec(memory_space=pltpu.HBM)] * n


def _all_gather(shards):
    n = len(shards)

    def body(*refs):
        ins, outs = refs[:n], refs[n:2 * n]
        send_sems, recv_sems = refs[2 * n:]
        x, y, c, other_chips = _mesh_place()
        own = 2 * x + y
        sibling = (x, y, 1 - c)

        def remote(src, dst, k, to):
            return pltpu.make_async_remote_copy(src_ref=src, dst_ref=dst, send_sem=send_sems.at[k],
                                                recv_sem=recv_sems.at[k], device_id=to, device_id_type=MESH)

        first = [remote(ins[w].at[c], outs[w].at[own, c], 7 * w + j, (*chip, c))
                 for w in range(n) for j, chip in enumerate(other_chips)]
        first += [remote(ins[w], outs[w].at[own], 7 * w + 6, sibling) for w in range(n)]
        for cp in first:
            cp.start()
        passed = []
        for w in range(n):
            for j, chip in enumerate(other_chips):
                rows = outs[w].at[2 * chip[0] + chip[1], c]
                remote(rows, rows, 7 * w + j, (*chip, c)).wait_recv()
                fwd = remote(rows, rows, 7 * w + 3 + j, sibling)
                fwd.start()
                passed.append(fwd)
        for w in range(n):
            for j, chip in enumerate(other_chips):
                rows = outs[w].at[2 * chip[0] + chip[1], 1 - c]
                remote(rows, rows, 7 * w + 3 + j, sibling).wait_recv()
            remote(ins[w], outs[w].at[own], 7 * w + 6, sibling).wait_recv()
        for cp in first + passed:
            cp.wait_send()

    return pl.pallas_call(
        body, name="weights_all_gather",
        out_shape=[jax.ShapeDtypeStruct((4,) + s.shape, s.dtype) for s in shards],
        in_specs=_hbm_specs(n), out_specs=_hbm_specs(n),
        scratch_shapes=[pltpu.SemaphoreType.DMA((7 * n,)), pltpu.SemaphoreType.DMA((7 * n,))],
    )(*shards)


def _sibling_exchange(parts, name):
    n = len(parts)

    def body(*refs):
        ins, outs = refs[:n], refs[n:2 * n]
        send_sems, recv_sems = refs[2 * n:]
        x, y, c, _ = _mesh_place()
        copies = [pltpu.make_async_remote_copy(src_ref=ins[w].at[1 - c], dst_ref=outs[w], send_sem=send_sems.at[w],
                                               recv_sem=recv_sems.at[w], device_id=(x, y, 1 - c), device_id_type=MESH)
                  for w in range(n)]
        for cp in copies:
            cp.start()
        for cp in copies:
            cp.wait()

    return pl.pallas_call(
        body, name=name,
        out_shape=[jax.ShapeDtypeStruct(p.shape[1:], p.dtype) for p in parts],
        in_specs=_hbm_specs(n), out_specs=_hbm_specs(n),
        scratch_shapes=[pltpu.SemaphoreType.DMA((n,)), pltpu.SemaphoreType.DMA((n,))],
    )(*parts)


SEM_SPEC = pl.BlockSpec(memory_space=pltpu.SEMAPHORE)
DATAFLOW_EFFECT = pltpu.SideEffectType.DATAFLOW_SIDE_EFFECTING


def _copies_per_weight(kind):
    return {"gather": 4, "scatter": 3, "sibling": 1, "forward": 3}[kind]


def _flight_copies(kind, src_refs, land_refs, send_sems, recv_sems, arriving):
    x, y, c, other_chips = _mesh_place()
    own = 2 * x + y
    sibling = (x, y, 1 - c)
    per = _copies_per_weight(kind)
    copies = []
    for w in range(len(src_refs)):
        def remote(src, dst, j, to):
            return pltpu.make_async_remote_copy(src_ref=src, dst_ref=dst, send_sem=send_sems.at[per * w + j],
                                                recv_sem=recv_sems.at[per * w + j], device_id=to, device_id_type=MESH)

        if kind == "sibling":
            copies.append(remote(src_refs[w].at[1 - c], land_refs[w], 0, sibling))
            continue
        for j, chip in enumerate(other_chips):
            theirs = 2 * chip[0] + chip[1]
            if kind == "gather":
                copies.append(remote(src_refs[w].at[c], land_refs[w].at[theirs if arriving else own, c], j, (*chip, c)))
            elif kind == "forward":
                copies.append(remote(src_refs[w].at[theirs, c], src_refs[w].at[theirs, (1 - c) if arriving else c],
                                     j, sibling))
            else:
                copies.append(remote(src_refs[w].at[theirs], land_refs[w].at[j], j, (*chip, c)))
        if kind == "gather":
            copies.append(remote(src_refs[w], land_refs[w].at[own], 3, sibling))
    return copies


def _ici_start(kind, srcs, name):
    n = len(srcs)
    if kind == "gather":
        lands = [lax.empty((4,) + s.shape, s.dtype) for s in srcs]
    elif kind == "scatter":
        lands = [lax.empty((3,) + s.shape[1:], s.dtype) for s in srcs]
    elif kind == "sibling":
        lands = [lax.empty(s.shape[1:], s.dtype) for s in srcs]
    else:
        lands = []
    nb = n + len(lands)

    def body(*refs):
        src_refs, land_refs = refs[:n], refs[n:nb]
        send_sems, recv_sems = refs[nb], refs[nb + 1]
        token = refs[-1]
        for cp in _flight_copies(kind, src_refs, land_refs, send_sems, recv_sems, False):
            cp.start()
        token[...] = jnp.zeros_like(token)

    hbm = lambda a: pltpu.with_memory_space_constraint(a, pltpu.HBM)
    n_sems = _copies_per_weight(kind) * n
    out = pl.pallas_call(
        body, name=name,
        out_shape=(pltpu.SemaphoreType.DMA((n_sems,)), pltpu.SemaphoreType.DMA((n_sems,)),
                   *[pltpu.HBM(a.shape, a.dtype) for a in srcs + lands], jax.ShapeDtypeStruct((8, LANES), F32)),
        in_specs=_hbm_specs(nb),
        out_specs=(SEM_SPEC, SEM_SPEC, *_hbm_specs(nb), pl.BlockSpec(memory_space=pltpu.VMEM)),
        input_output_aliases={i: 2 + i for i in range(nb)},
        compiler_params=pltpu.CompilerParams(has_side_effects=DATAFLOW_EFFECT),
    )(*[hbm(a) for a in srcs + lands])
    return out[0], out[1], list(out[2:2 + n]), list(out[2 + n:2 + nb]), out[-1]


def _ici_wait(kind, send_sems, recv_sems, srcs, lands, after, name):
    n = len(srcs)
    nb = n + len(lands)

    def body(*refs):
        src_refs, land_refs = refs[:n], refs[n:nb]
        send_ref, recv_ref = refs[nb], refs[nb + 1]
        for cp in _flight_copies(kind, src_refs, land_refs, send_ref, recv_ref, True):
            cp.wait_send()
            cp.wait_recv()

    out = pl.pallas_call(
        body, name=name, out_shape=tuple(pltpu.HBM(a.shape, a.dtype) for a in srcs + lands),
        in_specs=_hbm_specs(nb) + [SEM_SPEC, SEM_SPEC, pl.BlockSpec(memory_space=pl.ANY)],
        out_specs=tuple(_hbm_specs(nb)), input_output_aliases={i: i for i in range(nb)},
        compiler_params=pltpu.CompilerParams(has_side_effects=DATAFLOW_EFFECT),
    )(*srcs, *lands, send_sems, recv_sems, after)
    return list(out[:n]), list(out[n:])


def _halves_exchange(halves, name):
    n = len(halves)

    def body(*refs):
        ins, outs = refs[:n], refs[n:2 * n]
        send_sems, recv_sems = refs[2 * n:]
        x, y, c, _ = _mesh_place()
        copies = [pltpu.make_async_remote_copy(src_ref=ins[w], dst_ref=outs[w], send_sem=send_sems.at[w],
                                               recv_sem=recv_sems.at[w], device_id=(x, y, 1 - c), device_id_type=MESH)
                  for w in range(n)]
        for cp in copies:
            cp.start()
        for cp in copies:
            cp.wait()

    return pl.pallas_call(
        body, name=name,
        out_shape=[jax.ShapeDtypeStruct(h.shape, h.dtype) for h in halves],
        in_specs=_hbm_specs(n), out_specs=_hbm_specs(n),
        scratch_shapes=[pltpu.SemaphoreType.DMA((n,)), pltpu.SemaphoreType.DMA((n,))],
    )(*halves)


def _small_all_reduce(packed):
    rows = packed.shape[0]

    def body(in_ref, out_ref, gathered, send_sems, recv_sems):
        x, y, c, _ = _mesh_place()
        me = 4 * x + 2 * y + c
        gathered[0] = in_ref[...]
        copies = []
        for rel in range(1, 8):
            to = (x ^ (rel >> 2), y ^ ((rel >> 1) & 1), c ^ (rel & 1))
            cp = pltpu.make_async_remote_copy(src_ref=in_ref, dst_ref=gathered.at[rel], send_sem=send_sems.at[rel - 1],
                                              recv_sem=recv_sems.at[rel - 1], device_id=to, device_id_type=MESH)
            cp.start()
            copies.append(cp)
        for cp in copies:
            cp.wait()
        acc = gathered[me]
        for dev in range(1, 8):
            acc = acc + gathered[dev ^ me]
        out_ref[...] = acc

    return pl.pallas_call(
        body, name="small_all_reduce", out_shape=jax.ShapeDtypeStruct(packed.shape, F32),
        in_specs=[pl.BlockSpec(memory_space=pltpu.VMEM)], out_specs=pl.BlockSpec(memory_space=pltpu.VMEM),
        scratch_shapes=[pltpu.VMEM((8, rows, LANES), F32), pltpu.SemaphoreType.DMA((7,)), pltpu.SemaphoreType.DMA((7,))],
        compiler_params=pltpu.CompilerParams(vmem_limit_bytes=VMEM_LIMIT_BYTES),
    )(packed)


def _layout(w_in, w_uq, w_ukv, v_ln_gain, q_norm, kv_norm):
    heads = 4 * w_uq.shape[-1] // (NOPE_DIM + ROPE_DIM)
    gw = v_ln_gain.shape[-1]
    ql, kl = q_norm.shape[-1], kv_norm.shape[-1]
    lay = dict(heads=heads, gw=gw, ql=ql, kl=kl, aw=heads * V_DIM, u_off=0, v_off=gw, q_off=2 * gw,
               kv_off=2 * gw + ql, kr_off=2 * gw + ql + kl)
    lay["in_pad"] = _round_up(lay["kr_off"] + LANES, 2 * LANES if lay["kr_off"] + LANES <= 2048 else 1024)
    assert lay["q_off"] % ql == 0 and lay["kv_off"] % kl == 0 and lay["aw"] % gw == 0
    assert 4 * w_in.shape[-1] == ql + kl + ROPE_DIM + 2 * gw
    return lay


def _rope_tile(t1, t2, axis=-1):
    z = jnp.zeros_like(t1)
    return jnp.concatenate([t1, z, t2, z], axis=axis)


def _w_in_rows(gathered, shard_rows, lay):
    d = gathered.shape[-1]
    wt = gathered[:, :shard_rows].reshape(4 * shard_rows, d)
    ql, kl, gw = lay["ql"], lay["kl"], lay["gw"]
    q_c, kv_c = wt[:ql], wt[ql:ql + kl]
    r = wt[ql + kl:ql + kl + ROPE_DIM]
    u = wt[ql + kl + ROPE_DIM:ql + kl + ROPE_DIM + gw]
    v = wt[ql + kl + ROPE_DIM + gw:]
    parts = [u, v, q_c, kv_c, _rope_tile(r[:ROPE_HALF], r[ROPE_HALF:], axis=0)]
    pad = lay["in_pad"] - (lay["kr_off"] + LANES)
    if pad:
        parts.append(jnp.zeros((pad, d), wt.dtype))
    return jnp.concatenate(parts, axis=0)


def _w_in_grad_chunks(dwt, shard_rows, padded_rows, lay):
    d = dwt.shape[-1]
    ql, kl, gw = lay["ql"], lay["kl"], lay["gw"]
    ko = lay["kr_off"]
    rows = jnp.concatenate([dwt[lay["q_off"]:lay["q_off"] + ql], dwt[lay["kv_off"]:lay["kv_off"] + kl],
                            dwt[ko:ko + ROPE_HALF], dwt[ko + 2 * ROPE_HALF:ko + 3 * ROPE_HALF],
                            dwt[:gw], dwt[gw:2 * gw]], axis=0).reshape(4, shard_rows, d)
    rows = jnp.pad(rows, ((0, 0), (0, padded_rows - shard_rows), (0, 0)))
    return jnp.transpose(rows.reshape(4, 2, padded_rows // 2, d), (1, 0, 2, 3)).astype(BF16)


def _w_uq_padded(w, heads):
    w3 = w.reshape(w.shape[0], heads, NOPE_DIM + ROPE_DIM)
    t = _rope_tile(w3[..., NOPE_DIM:NOPE_DIM + ROPE_HALF], w3[..., NOPE_DIM + ROPE_HALF:])
    return jnp.concatenate([w3[..., :NOPE_DIM], t], axis=-1).reshape(w.shape[0], heads * HEAD_PAD)


def _w_uq_grad_unpadded(dw, heads):
    d3 = dw.reshape(dw.shape[0], heads, HEAD_PAD)
    return jnp.concatenate([d3[..., :NOPE_DIM], d3[..., NOPE_DIM:NOPE_DIM + ROPE_HALF],
                            d3[..., NOPE_DIM + 2 * ROPE_HALF:NOPE_DIM + 3 * ROPE_HALF]],
                           axis=-1).reshape(dw.shape[0], heads * (NOPE_DIM + ROPE_DIM))


def _cols_gathered(g):
    return jnp.transpose(g, (1, 0, 2)).reshape(g.shape[1], 4 * g.shape[2])


def _chunks_of_cols(grad):
    r, c4 = grad.shape
    return jnp.transpose(grad.reshape(2, r // 2, 4, c4 // 4), (0, 2, 1, 3)).astype(BF16)


SMALL = ["pre_mix_norm", "q_norm", "kv_norm", "v_ln_gain", "v_ln_bias", "w_spatial", "b_spatial", "attn_out_norm",
         "gmlp_out_norm", "post_mix_norm", "pre_ffn_norm", "post_ffn_norm"]
BIG = ["w_in", "w_uq", "w_ukv", "w_out", "w_gate", "w_up", "w_down"]
GATHER_NOW = ["w_in", "w_uq", "w_ukv"]
GATHER_LATER_1 = ["w_out", "w_gate", "w_up"]
GATHER_LATER_2 = ["w_down"]
REDUCE_FFN = ["w_gate", "w_up", "w_down"]
REDUCE_OUT = ["w_out"]
REDUCE_LAST = ["w_in", "w_uq", "w_ukv"]
TRANSPOSED = ("w_in", "w_gate", "w_up")
ORDER = ["pre_mix_norm", "w_in", "q_norm", "kv_norm", "w_uq", "w_ukv", "v_ln_gain", "v_ln_bias", "w_spatial",
         "b_spatial", "attn_out_norm", "gmlp_out_norm", "w_out", "post_mix_norm", "pre_ffn_norm", "w_gate", "w_up",
         "w_down", "post_ffn_norm"]


def _pack(arrays):
    flat = jnp.concatenate([a.reshape(-1) for a in arrays])
    n = flat.shape[0]
    total = _round_up(n, 8 * LANES)
    if total > n:
        flat = jnp.concatenate([flat, jnp.zeros((total - n,), F32)])
    return flat.reshape(total // LANES, LANES)


def _unpack(packed, like):
    flat = packed.reshape(-1)
    out, off = [], 0
    for a in like:
        out.append(flat[off:off + a.size].reshape(a.shape))
        off += a.size
    return out


def kernel(x, positions, pre_mix_norm, w_in, q_norm, kv_norm, w_uq, w_ukv, v_ln_gain, v_ln_bias, w_spatial, b_spatial, attn_out_norm, gmlp_out_norm, w_out, post_mix_norm, pre_ffn_norm, w_gate, w_up, w_down, post_ffn_norm, loss_target, m_pre_mix_norm, m_w_in, m_q_norm, m_kv_norm, m_w_uq, m_w_ukv, m_v_ln_gain, m_v_ln_bias, m_w_spatial, m_b_spatial, m_attn_out_norm, m_gmlp_out_norm, m_w_out, m_post_mix_norm, m_pre_ffn_norm, m_w_gate, m_w_up, m_w_down, m_post_ffn_norm, v_pre_mix_norm, v_w_in, v_q_norm, v_kv_norm, v_w_uq, v_w_ukv, v_v_ln_gain, v_v_ln_bias, v_w_spatial, v_b_spatial, v_attn_out_norm, v_gmlp_out_norm, v_w_out, v_post_mix_norm, v_pre_ffn_norm, v_w_gate, v_w_up, v_w_down, v_post_ffn_norm):
    args = dict(locals())
    weights = {n: args[n] for n in ORDER}
    m_in = {n: args["m_" + n] for n in ORDER}
    v_in = {n: args["v_" + n] for n in ORDER}

    lay = _layout(w_in, w_uq, w_ukv, v_ln_gain, q_norm, kv_norm)
    heads, gw = lay["heads"], lay["gw"]
    t, d = x.shape[1], x.shape[2]
    tr = 128 if t % 128 == 0 else t
    xs = x.reshape(t, d)
    target = loss_target.reshape(t, d)

    ffs, ins = w_gate.shape[-1], w_in.shape[-1]
    ffp, inp = _round_up(ffs, LANES), _round_up(ins, LANES)
    shards = {n: (jnp.swapaxes(weights[n][0], 0, 1) if n in TRANSPOSED else weights[n][0]).astype(BF16)
              for n in BIG}
    for n, rows in (("w_gate", ffp), ("w_up", ffp), ("w_down", ffp), ("w_in", inp)):
        shards[n] = jnp.pad(shards[n], ((0, rows - shards[n].shape[0]), (0, 0)))
    halved = {n: shards[n].reshape(2, shards[n].shape[0] // 2, shards[n].shape[1]) for n in BIG}
    full = {}

    def pair_sums_of(partial, names, tag):
        from_sibling = _sibling_exchange([partial[n] for n in names], "grads_sibling_exchange_" + tag)
        return [_pair_add(partial[n], r, "pair_add_" + n) for n, r in zip(names, from_sibling)]

    def place(names, lands):
        for n, g in zip(names, lands):
            full[n] = g.reshape((4,) + shards[n].shape)

    place(GATHER_NOW, _all_gather([halved[n] for n in GATHER_NOW]))
    flight_1 = _ici_start("gather", [halved[n] for n in GATHER_LATER_1], "gather_start_1")
    flight_2 = _ici_start("gather", [halved[n] for n in GATHER_LATER_2], "gather_start_2")
    wt_in = _w_in_rows(full["w_in"], ins, lay)
    wb_uq = _w_uq_padded(_cols_gathered(full["w_uq"]), heads)
    wb_ukv = _cols_gathered(full["w_ukv"])

    inv_freq = 1.0 / (ROPE_THETA ** (jnp.arange(0, ROPE_DIM, 2, dtype=F32) / ROPE_DIM))
    ang = positions.reshape(t).astype(F32)[:, None] * inv_freq
    cos, sin = jnp.cos(ang), jnp.sin(ang)
    cos_t = _rope_tile(cos, cos)
    sin_t = _rope_tile(-sin, sin)

    row = lambda a: a.reshape(1, -1)
    g_pre, g_q, g_kv = row(pre_mix_norm), row(q_norm), row(kv_norm)
    g_a, g_g, g_pm = row(attn_out_norm), row(gmlp_out_norm), row(post_mix_norm)
    g_pf, g_po = row(pre_ffn_norm), row(post_ffn_norm)
    ln_g, ln_b = row(v_ln_gain), row(v_ln_bias)
    ws = w_spatial[0].astype(BF16)
    ws_t = jnp.transpose(ws, (0, 2, 1))
    bs_wide = jnp.broadcast_to(b_spatial[0][:, :, None], b_spatial.shape[1:] + (G_HEAD_DIM,))

    xn = _prenorm(xs, g_pre, tr, flight_1[4] + flight_2[4])
    proj = _matmul(xn, wt_in, NT, F32, "proj")
    qn, kvn, kr = _qkv_prep(proj, g_q, g_kv, cos_t, sin_t, lay, tr)
    q = _q_rope(_matmul(qn, wb_uq, NN, F32, "q_up"), cos_t, sin_t, heads, tr)
    kv = _matmul(kvn, wb_ukv, NN, BF16, "kv_up")
    a_out, a_lse = _attn_fwd(q, kv, kr, heads)
    _, lands = _ici_wait("gather", *flight_1[:4], a_out, "gather_wait_1")
    pass_out = _ici_start("forward", lands[:1], "forward_start_out")
    pass_gu = _ici_start("forward", lands[1:], "forward_start_gate_up")
    gn = _gmlp_fwd(proj, ln_g, ln_b, ws, bs_wide, g_g, lay, pass_out[4] + pass_gu[4])
    mixed = _mix_norm(a_out, gn, g_a, tr)
    place(["w_out"], _ici_wait("forward", *pass_out[:4], mixed, "forward_wait_out")[0])
    wb_out = full["w_out"].reshape(-1, d)
    mix_out = _matmul(mixed, wb_out, NN, F32, "mix_out")
    h, hn = _post_mix(xs, mix_out, g_pm, g_pf, tr)
    place(["w_gate", "w_up"], _ici_wait("forward", *pass_gu[:4], hn, "forward_wait_gate_up")[0])
    wt_gate = full["w_gate"].reshape(4 * ffp, d)
    wt_up = full["w_up"].reshape(4 * ffp, d)
    gate = _matmul(hn, wt_gate, NT, BF16, "ffn_gate")
    _, lands = _ici_wait("gather", *flight_2[:4], gate, "gather_wait_2")
    pass_down = _ici_start("forward", lands, "forward_start_down")
    up = _matmul(hn, wt_up, NT, BF16, "ffn_up", token=pass_down[4])
    act = _swiglu(gate, up)
    place(["w_down"], _ici_wait("forward", *pass_down[:4], act, "forward_wait_down")[0])
    wb_down = full["w_down"].reshape(4 * ffp, d)
    ffn = _matmul(act, wb_down, NN, F32, "ffn_down")
    dy, d_ffn, dg_po, loss_vec = _loss_head(h, ffn, target, g_po, tr)

    d_act = _matmul(d_ffn, wb_down, NT, BF16, "d_act")
    d_gate, d_up = _swiglu_bwd(gate, up, d_act)
    partial_ffn = [_matmul(d_gate, hn, TN, BF16, "gw_gate", out_chunks=True),
                   _matmul(d_up, hn, TN, BF16, "gw_up", out_chunks=True),
                   _matmul(act, d_ffn, TN, BF16, "gw_down", out_chunks=True)]
    swap_ffn = _ici_start("sibling", partial_ffn, "sibling_start_ffn")
    d_hn = _matmul(d_up, wt_up, NN, F32, "d_hn",
                   extras=[_matmul(d_gate, wt_gate, NN, F32, "d_hn_gate", token=swap_ffn[4])],
                   epilogue=lambda acc, partial: acc + partial)
    partial_ffn, from_sibling = _ici_wait("sibling", *swap_ffn[:4], d_hn, "sibling_wait_ffn")
    pair_ffn = [_pair_add(p, r, "pair_add_" + n) for n, p, r in zip(REDUCE_FFN, partial_ffn, from_sibling)]
    flight_ffn = _ici_start("scatter", pair_ffn, "scatter_start_ffn")
    dh, d_mo, dg_pf, dg_pm = _post_mix_bwd(mix_out, h, dy, d_hn, g_pm, g_pf, tr, flight_ffn[4])
    d_mixed = _matmul(d_mo, wb_out, NT, F32, "d_mixed")
    gw_out = _matmul(mixed, d_mo, TN, BF16, "gw_out", out_chunks=True)
    pair_out = pair_sums_of({"w_out": gw_out}, REDUCE_OUT, "out")
    flight_out = _ici_start("scatter", pair_out, "scatter_start_out")
    d_a, dg_a = _mix_norm_bwd(a_out, d_mixed, g_a, tr, flight_out[4])
    d_u, d_v, dg_g, d_ln_g, d_ln_b, d_ws, d_bs_wide = _gmlp_bwd(proj, d_mixed, ln_g, ln_b, ws, ws_t, bs_wide, g_g, lay)
    d_bs = _spatial_bias_grad(d_bs_wide)
    d_q, d_kv, d_kr = _attn_bwd(q, kv, kr, a_out, a_lse, d_a, cos_t, sin_t, heads)
    d_qn = _matmul(d_q, wb_uq, NT, F32, "d_qn")
    gw_uq = _matmul(qn, d_q, TN, F32, "gw_uq")
    d_kvn = _matmul(d_kv, wb_ukv, NT, F32, "d_kvn")
    gw_ukv = _matmul(kvn, d_kv, TN, F32, "gw_ukv")
    d_qc, d_kvc, d_krt, dg_q, dg_kv = _qkv_bwd(proj, d_qn, d_kvn, d_kr, g_q, g_kv, cos_t, sin_t, lay, tr)
    parts = [d_u, d_v, d_qc, d_kvc, d_krt]
    pad = lay["in_pad"] - (lay["kr_off"] + LANES)
    if pad:
        parts.append(jnp.zeros((t, pad), BF16))
    d_proj = jnp.concatenate(parts, axis=1)
    d_xn = _matmul(d_proj, wt_in, NN, F32, "d_xn")
    gw_in = _matmul(d_proj, xn, TN, F32, "gw_in")
    grad_x, dg_pre = _prenorm_bwd(xs, d_xn, dh, g_pre, tr)

    pair_mix = pair_sums_of({"w_in": _w_in_grad_chunks(gw_in, ins, inp, lay),
                             "w_uq": _chunks_of_cols(_w_uq_grad_unpadded(gw_uq, heads)),
                             "w_ukv": _chunks_of_cols(gw_ukv)}, REDUCE_LAST, "mix")
    pair_ffn, received_ffn = _ici_wait("scatter", *flight_ffn[:4], pair_mix[-1], "scatter_wait_ffn")
    pair_out, received_out = _ici_wait("scatter", *flight_out[:4], pair_mix[-1], "scatter_wait_out")
    flight_mix = _ici_start("scatter", pair_mix, "scatter_start_mix")
    grads, delta, new_m, new_v = {}, {}, {}, {}

    def finish(names, pair_sums, received, tag, token):
        mine = [_chip_sum(p, r, "chip_sum_" + n, token) for n, p, r in zip(names, pair_sums, received)]
        theirs = _halves_exchange(mine, "grads_halves_exchange_" + tag)
        for n, g_mine, g_theirs in zip(names, mine, theirs):
            shape = weights[n].shape
            if n in TRANSPOSED:
                view = lambda a: jnp.swapaxes(a[0], 0, 1)
                back = lambda o: jnp.swapaxes(o, 0, 1).reshape(shape)
            else:
                view = lambda a: a[0]
                back = lambda o: o.reshape(shape)
            out = _adamw_halves(view(weights[n]), g_mine, g_theirs, view(m_in[n]), view(v_in[n]), "adamw_" + n)
            grads[n], delta[n], new_m[n], new_v[n] = [back(o) for o in out]

    finish(REDUCE_FFN + REDUCE_OUT, pair_ffn + pair_out, received_ffn + received_out, "ffn", flight_mix[4])
    pair_mix, received = _ici_wait("scatter", *flight_mix[:4], new_v[REDUCE_OUT[-1]], "scatter_wait_mix")
    finish(REDUCE_LAST, pair_mix, received, "mix", None)

    small_grads = {"pre_mix_norm": dg_pre, "q_norm": dg_q, "kv_norm": dg_kv, "v_ln_gain": d_ln_g, "v_ln_bias": d_ln_b,
                   "w_spatial": d_ws, "b_spatial": d_bs, "attn_out_norm": dg_a, "gmlp_out_norm": dg_g,
                   "post_mix_norm": dg_pm, "pre_ffn_norm": dg_pf, "post_ffn_norm": dg_po}
    like = [weights[n] for n in SMALL]
    reduced = _small_all_reduce(_pack([small_grads[n] for n in SMALL] + [loss_vec]))
    loss = reduced.reshape(-1)[sum(a.size for a in like)]
    small_g = _pack(_unpack(reduced, like))
    s_delta, s_m, s_v = _adamw(_pack(like), small_g, _pack([m_in[n] for n in SMALL]),
                               _pack([v_in[n] for n in SMALL]), "adamw_small")
    for n, g in zip(SMALL, _unpack(small_g, like)):
        grads[n] = g
    delta.update(zip(SMALL, _unpack(s_delta, like)))
    new_m.update(zip(SMALL, _unpack(s_m, like)))
    new_v.update(zip(SMALL, _unpack(s_v, like)))

    return (loss, grad_x.reshape(x.shape), *[grads[n] for n in ORDER], *[delta[n] for n in ORDER],
            *[new_m[n] for n in ORDER], *[new_v[n] for n in ORDER])
```

```python
import functools
import math

import jax
import jax.numpy as jnp
from jax import lax
from jax.experimental import pallas as pl
from jax.experimental.pallas import tpu as pltpu

F32 = jnp.float32
BF16 = jnp.bfloat16
MESH = pl.DeviceIdType.MESH

NOPE_DIM = 128
ROPE_DIM = 64
ROPE_HALF = ROPE_DIM // 2
V_DIM = 128
HEAD_PAD = 256
G_HEAD_DIM = 128
CHUNK = 128
ROPE_THETA = 10000.0
EPS = 1e-6
ADAM_LR = 0.001
ADAM_B1 = 0.9
ADAM_B2 = 0.999
ADAM_EPS = 1e-08
ADAM_WD = 0.01
ADAM_STEP = 10

LANES = 128
MATMUL_TILE = 1024
WIDE_TILE = 1408
VMEM_LIMIT_BYTES = 48 * 1024 * 1024

NN = (((1,), (0,)), ((), ()))
NT = (((1,), (1,)), ((), ()))
TN = (((0,), (0,)), ((), ()))


def _params(semantics):
    return pltpu.CompilerParams(dimension_semantics=semantics, vmem_limit_bytes=VMEM_LIMIT_BYTES)


def _tile(n, cap=MATMUL_TILE):
    if n <= cap:
        return n
    if cap == MATMUL_TILE and n % WIDE_TILE == 0:
        return WIDE_TILE
    t = cap - cap % LANES
    while n % t:
        t -= LANES
    assert t > 0, n
    return t


def _round_up(n, m):
    return (n + m - 1) // m * m


def _matmul(a, b, dims, out_dtype, name, extras=(), epilogue=None, out_chunks=None, token=None):
    if dims is NN:
        (m, k), (k2, n) = a.shape, b.shape
    elif dims is NT:
        (m, k), (n, k2) = a.shape, b.shape
    else:
        (k, m), (k2, n) = a.shape, b.shape
    assert k == k2, (a.shape, b.shape, name)
    tm, tn, tk = _tile(m // 8 if out_chunks else m), _tile(n), _tile(k)
    if len(extras) + (len(out_dtype) if isinstance(out_dtype, tuple) else 1) > 2:
        tm = _tile(m, MATMUL_TILE // 2)
    nk = k // tk

    out_dtypes = out_dtype if isinstance(out_dtype, tuple) else (out_dtype,)
    n_extra = len(extras)

    def body(*refs):
        a_ref, b_ref = refs[:2]
        extra_refs = refs[2:2 + n_extra]
        out_refs = refs[2 + n_extra + (token is not None):-1]
        acc_ref = refs[-1]
        kk = pl.program_id(2)

        @pl.when(kk == 0)
        def _():
            acc_ref[...] = jnp.zeros_like(acc_ref)

        acc_ref[...] += lax.dot_general(a_ref[...], b_ref[...], dims, preferred_element_type=F32)

        @pl.when(kk == nk - 1)
        def _():
            r = acc_ref[...]
            if epilogue is not None:
                r = epilogue(r, *[e[...] for e in extra_refs])
            for o_ref, val in zip(out_refs, r if isinstance(r, tuple) else (r,)):
                o_ref[...] = val.astype(o_ref.dtype)

    if dims is TN:
        a_spec = pl.BlockSpec((tk, tm), lambda i, j, kk: (kk, i))
    else:
        a_spec = pl.BlockSpec((tm, tk), lambda i, j, kk: (i, kk))
    if dims is NT:
        b_spec = pl.BlockSpec((tn, tk), lambda i, j, kk: (j, kk))
    else:
        b_spec = pl.BlockSpec((tk, tn), lambda i, j, kk: (kk, j))
    if not out_chunks:
        o_spec = pl.BlockSpec((tm, tn), lambda i, j, kk: (i, j))
        o_shape = (m, n)
    else:
        pi = m // 8 // tm
        o_spec = pl.BlockSpec((None, None, tm, tn), lambda i, j, kk: ((i // pi) % 2, i // (2 * pi), i % pi, j))
        o_shape = (2, 4, m // 8, n)
    assert not (extras and out_chunks)
    tokens = [] if token is None else [token]
    out = pl.pallas_call(
        body, name=name, grid=(m // tm, n // tn, nk),
        in_specs=[a_spec, b_spec] + [o_spec] * n_extra + [pl.BlockSpec(memory_space=pl.ANY)] * len(tokens),
        out_specs=[o_spec] * len(out_dtypes),
        out_shape=[jax.ShapeDtypeStruct(o_shape, dt) for dt in out_dtypes],
        scratch_shapes=[pltpu.VMEM((tm, tn), F32)],
        compiler_params=_params(("parallel", "parallel", "arbitrary")),
    )(a, b, *extras, *tokens)
    return tuple(out) if isinstance(out_dtype, tuple) else out[0]


def _row_call(body, name, rows, tr, row_ins, par_ins, row_outs, acc_outs):
    def col(i, cb):
        return (i, cb)

    def whole(i, nd):
        return (0,) * nd

    in_specs = [pl.BlockSpec((tr, w), functools.partial(col, cb=cb)) for (_, w, cb) in row_ins]
    in_specs += [pl.BlockSpec(a.shape, functools.partial(whole, nd=a.ndim)) for a in par_ins]
    out_specs = [pl.BlockSpec((tr, w), lambda i: (i, 0)) for (w, _) in row_outs]
    out_specs += [pl.BlockSpec(s, functools.partial(whole, nd=len(s))) for (s, _) in acc_outs]
    out_shape = [jax.ShapeDtypeStruct((rows, w), dt) for (w, dt) in row_outs]
    out_shape += [jax.ShapeDtypeStruct(s, dt) for (s, dt) in acc_outs]
    return pl.pallas_call(
        body, name=name, grid=(rows // tr,), in_specs=in_specs, out_specs=out_specs, out_shape=out_shape,
        compiler_params=_params(("arbitrary",) if acc_outs else ("parallel",)),
    )(*[a for (a, _, _) in row_ins], *par_ins)


def _accumulate(ref, val):
    i = pl.program_id(0)

    @pl.when(i == 0)
    def _():
        ref[...] = val

    @pl.when(i > 0)
    def _():
        ref[...] += val


def _colsum(v):
    return jnp.sum(v, axis=0, keepdims=True)


def _rms_fwd(x, g):
    r = lax.rsqrt(jnp.mean(x * x, axis=-1, keepdims=True) + EPS)
    return x * r * g


def _rms_bwd(x, g, dy):
    r = lax.rsqrt(jnp.mean(x * x, axis=-1, keepdims=True) + EPS)
    xh = x * r
    dxh = dy * g
    dx = r * (dxh - xh * jnp.mean(dxh * xh, axis=-1, keepdims=True))
    return dx, dy * xh


_GELU_C = math.sqrt(2.0 / math.pi)
_GELU_A = 0.044715


def _gelu(x):
    return 0.5 * x * (1.0 + jnp.tanh(_GELU_C * (x + _GELU_A * (x * x * x))))


def _gelu_grad(x):
    t = jnp.tanh(_GELU_C * (x + _GELU_A * (x * x * x)))
    return 0.5 * (1.0 + t) + 0.5 * x * (1.0 - t * t) * (_GELU_C * (1.0 + 3.0 * _GELU_A * (x * x)))


def _sigmoid(x):
    return 1.0 / (1.0 + jnp.exp(-x))


def _rope_fwd(t, cos_t, sin_t):
    return t * cos_t + pltpu.roll(t, 2 * ROPE_HALF, 1) * sin_t


def _rope_bwd(dt, cos_t, sin_t):
    return dt * cos_t - pltpu.roll(dt, 2 * ROPE_HALF, 1) * sin_t


def _prenorm(x, g, tr, token):
    def body(x_ref, g_ref, token_ref, o_ref):
        o_ref[...] = _rms_fwd(x_ref[...], g_ref[...]).astype(BF16)

    t, d = x.shape
    return _row_call(body, "prenorm", t, tr, [(x, d, 0)], [g, token], [(d, BF16)], [])[0]


def _qkv_prep(proj, g_q, g_kv, cos_t, sin_t, lay, tr):
    ql, kl = lay["ql"], lay["kl"]

    def body(q_ref, kv_ref, kr_ref, cos_ref, sin_ref, gq_ref, gkv_ref, qn_ref, kvn_ref, kro_ref):
        qn_ref[...] = _rms_fwd(q_ref[...], gq_ref[...]).astype(BF16)
        kvn_ref[...] = _rms_fwd(kv_ref[...], gkv_ref[...]).astype(BF16)
        kro_ref[...] = _rope_fwd(kr_ref[...], cos_ref[...], sin_ref[...]).astype(BF16)

    t = proj.shape[0]
    return _row_call(
        body, "qkv_prep", t, tr,
        [(proj, ql, lay["q_off"] // ql), (proj, kl, lay["kv_off"] // kl), (proj, LANES, lay["kr_off"] // LANES),
         (cos_t, LANES, 0), (sin_t, LANES, 0)],
        [g_q, g_kv], [(ql, BF16), (kl, BF16), (LANES, BF16)], [])


def _q_rope(q, cos_t, sin_t, heads, tr):
    def body(q_ref, cos_ref, sin_ref, o_ref):
        c, s = cos_ref[...], sin_ref[...]
        for h in range(heads):
            lo = h * HEAD_PAD
            o_ref[:, lo:lo + NOPE_DIM] = q_ref[:, lo:lo + NOPE_DIM].astype(BF16)
            o_ref[:, lo + NOPE_DIM:lo + HEAD_PAD] = _rope_fwd(q_ref[:, lo + NOPE_DIM:lo + HEAD_PAD], c, s).astype(BF16)

    t, w = q.shape
    return _row_call(body, "q_rope", t, tr, [(q, w, 0), (cos_t, LANES, 0), (sin_t, LANES, 0)], [], [(w, BF16)], [])[0]


ATTN_SCALE = 1.0 / math.sqrt(NOPE_DIM + ROPE_DIM)
ATTN_EXP2_SCALE = ATTN_SCALE * math.log2(math.e)


def _attn_tile(t, cap):
    tq = cap
    while t % tq:
        tq //= 2
    return tq


def _attn_fwd(q, kv, kr, heads):
    t = q.shape[0]
    tq = _attn_tile(t, 256)

    def body(q_ref, kv_ref, kr_ref, o_ref, lse_ref, kcat):
        @pl.when(pl.program_id(1) == 0)
        def _():
            kcat[:, :NOPE_DIM] = kv_ref[:, :NOPE_DIM]
            kcat[:, NOPE_DIM:] = kr_ref[...]

        s = lax.dot_general(q_ref[...], kcat[...], NT, preferred_element_type=F32)
        m = jnp.max(s, axis=-1, keepdims=True)
        p = jnp.exp2((s - m) * ATTN_EXP2_SCALE)
        l = jnp.sum(p, axis=-1, keepdims=True)
        o_ref[...] = jnp.dot(p.astype(BF16), kv_ref[:, NOPE_DIM:], preferred_element_type=F32) * (1.0 / l)
        lse_ref[...] = jnp.broadcast_to(m * ATTN_EXP2_SCALE + jnp.log(l) * math.log2(math.e), (tq, V_DIM))

    out_spec = pl.BlockSpec((tq, V_DIM), lambda h, i: (i, h))
    out_shape = jax.ShapeDtypeStruct((t, heads * V_DIM), F32)
    return pl.pallas_call(
        body, name="attn_fwd", grid=(heads, t // tq),
        in_specs=[pl.BlockSpec((tq, HEAD_PAD), lambda h, i: (i, h)),
                  pl.BlockSpec((t, HEAD_PAD), lambda h, i: (0, h)),
                  pl.BlockSpec((t, LANES), lambda h, i: (0, 0))],
        out_specs=[out_spec, out_spec], out_shape=[out_shape, out_shape],
        scratch_shapes=[pltpu.VMEM((t, HEAD_PAD), BF16)],
        compiler_params=_params(("arbitrary", "arbitrary")),
    )(q, kv, kr)


def _attn_bwd(q, kv, kr, out, lse, d_out, cos_t, sin_t, heads):
    t = q.shape[0]
    tq = _attn_tile(t, 256)
    nq = t // tq

    def body(q_ref, kv_ref, kr_ref, o_ref, lse_ref, do_ref, cos_ref, sin_ref, dq_ref, dkv_ref, dkr_ref,
             kcat, dk_acc, dv_acc):
        h, i = pl.program_id(0), pl.program_id(1)

        @pl.when(i == 0)
        def _():
            kcat[:, :NOPE_DIM] = kv_ref[:, :NOPE_DIM]
            kcat[:, NOPE_DIM:] = kr_ref[...]
            dk_acc[...] = jnp.zeros_like(dk_acc)
            dv_acc[...] = jnp.zeros_like(dv_acc)

        @pl.when((h == 0) & (i == 0))
        def _():
            dkr_ref[...] = jnp.zeros_like(dkr_ref)

        qb, dob = q_ref[...], do_ref[...]
        row_term = jnp.sum(dob.astype(F32) * o_ref[...], axis=-1, keepdims=True)
        s = lax.dot_general(qb, kcat[...], NT, preferred_element_type=F32)
        dp = lax.dot_general(dob, kv_ref[:, NOPE_DIM:], NT, preferred_element_type=F32)
        p = jnp.exp2(s * ATTN_EXP2_SCALE - lse_ref[:, :1])
        ds = (p * (dp - row_term)).astype(BF16)
        dv_acc[...] += lax.dot_general(p.astype(BF16), dob, TN, preferred_element_type=F32)
        dq = jnp.dot(ds, kcat[...], preferred_element_type=F32) * ATTN_SCALE
        dq_ref[:, :NOPE_DIM] = dq[:, :NOPE_DIM].astype(BF16)
        dq_ref[:, NOPE_DIM:] = _rope_bwd(dq[:, NOPE_DIM:], cos_ref[...], sin_ref[...]).astype(BF16)
        dk_acc[...] += lax.dot_general(ds, qb, TN, preferred_element_type=F32)

        @pl.when(i == nq - 1)
        def _():
            dkv_ref[:, :NOPE_DIM] = (dk_acc[:, :NOPE_DIM] * ATTN_SCALE).astype(BF16)
            dkv_ref[:, NOPE_DIM:] = dv_acc[...].astype(BF16)
            dkr_ref[...] += dk_acc[:, NOPE_DIM:] * ATTN_SCALE

    return pl.pallas_call(
        body, name="attn_bwd", grid=(heads, nq),
        in_specs=[pl.BlockSpec((tq, HEAD_PAD), lambda h, i: (i, h)),
                  pl.BlockSpec((t, HEAD_PAD), lambda h, i: (0, h)),
                  pl.BlockSpec((t, LANES), lambda h, i: (0, 0)),
                  pl.BlockSpec((tq, V_DIM), lambda h, i: (i, h)),
                  pl.BlockSpec((tq, V_DIM), lambda h, i: (i, h)),
                  pl.BlockSpec((tq, V_DIM), lambda h, i: (i, h)),
                  pl.BlockSpec((tq, LANES), lambda h, i: (i, 0)),
                  pl.BlockSpec((tq, LANES), lambda h, i: (i, 0))],
        out_specs=[pl.BlockSpec((tq, HEAD_PAD), lambda h, i: (i, h)),
                   pl.BlockSpec((t, HEAD_PAD), lambda h, i: (0, h)),
                   pl.BlockSpec((t, LANES), lambda h, i: (0, 0))],
        out_shape=[jax.ShapeDtypeStruct((t, heads * HEAD_PAD), BF16),
                   jax.ShapeDtypeStruct((t, heads * HEAD_PAD), BF16),
                   jax.ShapeDtypeStruct((t, LANES), F32)],
        scratch_shapes=[pltpu.VMEM((t, HEAD_PAD), BF16), pltpu.VMEM((t, HEAD_PAD), F32), pltpu.VMEM((t, V_DIM), F32)],
        compiler_params=_params(("arbitrary", "arbitrary")),
    )(q, kv, kr, out, lse, d_out, cos_t, sin_t)


def _layer_norm_parts(x):
    mu = jnp.mean(x, axis=-1, keepdims=True)
    xc = x - mu
    r = lax.rsqrt(jnp.mean(xc * xc, axis=-1, keepdims=True) + EPS)
    return xc * r, r


def _gmlp_fwd(proj, ln_g, ln_b, w_s, b_sb, g_out_norm, lay):
    gw = lay["gw"]
    g_heads = gw // G_HEAD_DIM

    def body(u_ref, v_ref, lng_ref, lnb_ref, ws_ref, bs_ref, gn_ref, o_ref, gate_ref):
        gu = _gelu(u_ref[...])
        vh, _ = _layer_norm_parts(_gelu(v_ref[...]))
        vln = (vh * lng_ref[...] + lnb_ref[...]).astype(BF16)
        for g in range(g_heads):
            cols = slice(g * G_HEAD_DIM, (g + 1) * G_HEAD_DIM)
            s = jnp.dot(ws_ref[g], vln[:, cols], preferred_element_type=F32) + bs_ref[g]
            gate_ref[:, cols] = gu[:, cols] * s
        o_ref[...] = _rms_fwd(gate_ref[...], gn_ref[...]).astype(BF16)

    t = proj.shape[0]
    in_specs = [pl.BlockSpec((CHUNK, gw), lambda i: (i, 0)), pl.BlockSpec((CHUNK, gw), lambda i: (i, 1))]
    pars = [ln_g, ln_b, w_s, b_sb, g_out_norm]
    in_specs += [pl.BlockSpec(a.shape, functools.partial(lambda i, nd: (0,) * nd, nd=a.ndim)) for a in pars]
    return pl.pallas_call(
        body, name="gmlp_fwd", grid=(t // CHUNK,), in_specs=in_specs,
        out_specs=pl.BlockSpec((CHUNK, gw), lambda i: (i, 0)),
        out_shape=jax.ShapeDtypeStruct((t, gw), BF16),
        scratch_shapes=[pltpu.VMEM((CHUNK, gw), F32)],
        compiler_params=_params(("parallel",)),
    )(proj, proj, *pars)


def _gmlp_bwd(proj, d_mixed, ln_g, ln_b, w_s, w_st, b_sb, g_out_norm, lay):
    gw = lay["gw"]
    g_heads = gw // G_HEAD_DIM
    aw_blocks = lay["aw"] // gw

    def body(u_ref, v_ref, dm_ref, lng_ref, lnb_ref, ws_ref, wst_ref, bs_ref, gn_ref,
             du_ref, dv_ref, dgn_ref, dlng_ref, dlnb_ref, dws_ref, dbs_ref, gate_ref, s_ref, dvln_ref):
        i = pl.program_id(0)
        u, v = u_ref[...], v_ref[...]
        gu, gv = _gelu(u), _gelu(v)
        vh, r_ln = _layer_norm_parts(gv)
        vln = (vh * lng_ref[...] + lnb_ref[...]).astype(BF16)
        for g in range(g_heads):
            cols = slice(g * G_HEAD_DIM, (g + 1) * G_HEAD_DIM)
            s = jnp.dot(ws_ref[g], vln[:, cols], preferred_element_type=F32) + bs_ref[g]
            s_ref[:, cols] = s
            gate_ref[:, cols] = gu[:, cols] * s
        d_gate, dgn = _rms_bwd(gate_ref[...], gn_ref[...], dm_ref[...])
        _accumulate(dgn_ref, _colsum(dgn))
        du_ref[...] = (d_gate * s_ref[...] * _gelu_grad(u)).astype(BF16)
        d_s = d_gate * gu
        d_sb = d_s.astype(BF16)
        for g in range(g_heads):
            cols = slice(g * G_HEAD_DIM, (g + 1) * G_HEAD_DIM)
            dw = lax.dot_general(d_sb[:, cols], vln[:, cols], NT, preferred_element_type=F32)

            @pl.when(i == 0)
            def _():
                dws_ref[g] = dw
                dbs_ref[g] = d_s[:, cols]

            @pl.when(i > 0)
            def _():
                dws_ref[g] += dw
                dbs_ref[g] += d_s[:, cols]

            dvln_ref[:, cols] = jnp.dot(wst_ref[g], d_sb[:, cols], preferred_element_type=F32)
        d_vln = dvln_ref[...]
        _accumulate(dlng_ref, _colsum(d_vln * vh))
        _accumulate(dlnb_ref, _colsum(d_vln))
        d_vh = d_vln * lng_ref[...]
        d_gv = r_ln * (d_vh - jnp.mean(d_vh, axis=-1, keepdims=True)
                       - vh * jnp.mean(d_vh * vh, axis=-1, keepdims=True))
        dv_ref[...] = (d_gv * _gelu_grad(v)).astype(BF16)

    t = proj.shape[0]
    whole = lambda a: pl.BlockSpec(a.shape, functools.partial(lambda i, nd: (0,) * nd, nd=a.ndim))
    pars = [ln_g, ln_b, w_s, w_st, b_sb, g_out_norm]
    hshape = (g_heads, CHUNK, CHUNK)
    return pl.pallas_call(
        body, name="gmlp_bwd", grid=(t // CHUNK,),
        in_specs=[pl.BlockSpec((CHUNK, gw), lambda i: (i, 0)), pl.BlockSpec((CHUNK, gw), lambda i: (i, 1)),
                  pl.BlockSpec((CHUNK, gw), lambda i: (i, aw_blocks))] + [whole(a) for a in pars],
        out_specs=[pl.BlockSpec((CHUNK, gw), lambda i: (i, 0)), pl.BlockSpec((CHUNK, gw), lambda i: (i, 0)),
                   pl.BlockSpec((1, gw), lambda i: (0, 0)), pl.BlockSpec((1, gw), lambda i: (0, 0)),
                   pl.BlockSpec((1, gw), lambda i: (0, 0)),
                   pl.BlockSpec(hshape, lambda i: (0, 0, 0)), pl.BlockSpec(hshape, lambda i: (0, 0, 0))],
        out_shape=[jax.ShapeDtypeStruct((t, gw), BF16), jax.ShapeDtypeStruct((t, gw), BF16),
                   jax.ShapeDtypeStruct((1, gw), F32), jax.ShapeDtypeStruct((1, gw), F32),
                   jax.ShapeDtypeStruct((1, gw), F32),
                   jax.ShapeDtypeStruct(hshape, F32), jax.ShapeDtypeStruct(hshape, F32)],
        scratch_shapes=[pltpu.VMEM((CHUNK, gw), F32), pltpu.VMEM((CHUNK, gw), F32), pltpu.VMEM((CHUNK, gw), F32)],
        compiler_params=_params(("arbitrary",)),
    )(proj, proj, d_mixed, *pars)


def _spatial_bias_grad(dbs_wide):
    g_heads = dbs_wide.shape[0]

    def body(x_ref, o_ref):
        for g in range(g_heads):
            o_ref[g:g + 1, :] = jnp.sum(x_ref[g].T, axis=0, keepdims=True)

    return pl.pallas_call(
        body, name="spatial_bias_grad", out_shape=jax.ShapeDtypeStruct((g_heads, CHUNK), F32),
        in_specs=[pl.BlockSpec(memory_space=pltpu.VMEM)], out_specs=pl.BlockSpec(memory_space=pltpu.VMEM),
    )(dbs_wide)


def _mix_norm(a_out, gn, g_a, tr):
    aw = a_out.shape[1]
    gw = gn.shape[1]

    def body(a_ref, gn_ref, g_ref, o_ref):
        o_ref[:, :aw] = _rms_fwd(a_ref[...], g_ref[...]).astype(BF16)
        o_ref[:, aw:] = gn_ref[...]

    t = a_out.shape[0]
    return _row_call(body, "mix_norm", t, tr, [(a_out, aw, 0), (gn, gw, 0)], [g_a], [(aw + gw, BF16)], [])[0]


def _mix_norm_bwd(a_out, d_mixed, g_a, tr, token):
    aw = a_out.shape[1]

    def body(a_ref, dm_ref, g_ref, token_ref, da_ref, dg_ref):
        dx, dg = _rms_bwd(a_ref[...], g_ref[...], dm_ref[...])
        da_ref[...] = dx.astype(BF16)
        _accumulate(dg_ref, _colsum(dg))

    t = a_out.shape[0]
    return _row_call(body, "mix_norm_bwd", t, tr, [(a_out, aw, 0), (d_mixed, aw, 0)], [g_a, token],
                     [(aw, BF16)], [((1, aw), F32)])


def _post_mix(x, mix_out, g_pm, g_pf, tr):
    def body(x_ref, mo_ref, gpm_ref, gpf_ref, h_ref, hn_ref):
        h = x_ref[...] + _rms_fwd(mo_ref[...], gpm_ref[...])
        h_ref[...] = h
        hn_ref[...] = _rms_fwd(h, gpf_ref[...]).astype(BF16)

    t, d = x.shape
    return _row_call(body, "post_mix", t, tr, [(x, d, 0), (mix_out, d, 0)], [g_pm, g_pf], [(d, F32), (d, BF16)], [])


def _post_mix_bwd(mix_out, h, dy, d_hn, g_pm, g_pf, tr, token):
    def body(mo_ref, h_ref, dy_ref, dhn_ref, gpm_ref, gpf_ref, token_ref, dh_ref, dmo_ref, dgpf_ref, dgpm_ref):
        dx, dg = _rms_bwd(h_ref[...], gpf_ref[...], dhn_ref[...])
        dh = dy_ref[...] + dx
        dh_ref[...] = dh
        _accumulate(dgpf_ref, _colsum(dg))
        dmo, dg2 = _rms_bwd(mo_ref[...], gpm_ref[...], dh)
        dmo_ref[...] = dmo.astype(BF16)
        _accumulate(dgpm_ref, _colsum(dg2))

    t, d = h.shape
    return _row_call(body, "post_mix_bwd", t, tr, [(mix_out, d, 0), (h, d, 0), (dy, d, 0), (d_hn, d, 0)],
                     [g_pm, g_pf, token], [(d, F32), (d, BF16)], [((1, d), F32), ((1, d), F32)])


def _swiglu(gate, up):
    t, f = gate.shape
    tr, tf = _tile(t, 512), _tile(f, 2048)

    def body(g_ref, u_ref, o_ref):
        g = g_ref[...].astype(F32)
        o_ref[...] = (g * _sigmoid(g) * u_ref[...].astype(F32)).astype(BF16)

    spec = pl.BlockSpec((tr, tf), lambda i, j: (i, j))
    return pl.pallas_call(body, name="swiglu", grid=(t // tr, f // tf), in_specs=[spec, spec], out_specs=spec,
                          out_shape=jax.ShapeDtypeStruct((t, f), BF16),
                          compiler_params=_params(("parallel", "parallel")))(gate, up)


def _swiglu_bwd(gate, up, d_act):
    t, f = gate.shape
    tr, tf = _tile(t, 512), _tile(f, 2048)

    def body(g_ref, u_ref, da_ref, dg_ref, du_ref):
        g, u, da = g_ref[...].astype(F32), u_ref[...].astype(F32), da_ref[...].astype(F32)
        sg = _sigmoid(g)
        du_ref[...] = (da * (g * sg)).astype(BF16)
        dg_ref[...] = (da * u * (sg * (1.0 + g * (1.0 - sg)))).astype(BF16)

    spec = pl.BlockSpec((tr, tf), lambda i, j: (i, j))
    shape = jax.ShapeDtypeStruct((t, f), BF16)
    return pl.pallas_call(body, name="swiglu_bwd", grid=(t // tr, f // tf), in_specs=[spec, spec, spec],
                          out_specs=[spec, spec], out_shape=[shape, shape],
                          compiler_params=_params(("parallel", "parallel")))(gate, up, d_act)


def _loss_head(h, ffn, target, g_po, tr):
    t, d = h.shape

    def body(h_ref, f_ref, t_ref, g_ref, dy_ref, df_ref, dg_ref, loss_ref):
        f = f_ref[...]
        err = h_ref[...] + _rms_fwd(f, g_ref[...]) - t_ref[...]
        dy = err * (1.0 / d)
        dy_ref[...] = dy
        df, dg = _rms_bwd(f, g_ref[...], dy)
        df_ref[...] = df.astype(BF16)
        _accumulate(dg_ref, _colsum(dg))
        sq = jnp.sum(_colsum(err * err), axis=-1, keepdims=True) * (0.5 / d)
        _accumulate(loss_ref, jnp.broadcast_to(sq, (1, LANES)))

    return _row_call(body, "loss_head", t, tr, [(h, d, 0), (ffn, d, 0), (target, d, 0)], [g_po],
                     [(d, F32), (d, BF16)], [((1, d), F32), ((1, LANES), F32)])


def _qkv_bwd(proj, d_qn, d_kvn, d_kr, g_q, g_kv, cos_t, sin_t, lay, tr):
    ql, kl = lay["ql"], lay["kl"]

    def body(q_ref, kv_ref, dqn_ref, dkvn_ref, dkr_ref, cos_ref, sin_ref, gq_ref, gkv_ref,
             dq_ref, dkv_ref, dkt_ref, dgq_ref, dgkv_ref):
        dx, dg = _rms_bwd(q_ref[...], gq_ref[...], dqn_ref[...])
        dq_ref[...] = dx.astype(BF16)
        _accumulate(dgq_ref, _colsum(dg))
        dx, dg = _rms_bwd(kv_ref[...], gkv_ref[...], dkvn_ref[...])
        dkv_ref[...] = dx.astype(BF16)
        _accumulate(dgkv_ref, _colsum(dg))
        dkt_ref[...] = _rope_bwd(dkr_ref[...], cos_ref[...], sin_ref[...]).astype(BF16)

    t = proj.shape[0]
    return _row_call(
        body, "qkv_bwd", t, tr,
        [(proj, ql, lay["q_off"] // ql), (proj, kl, lay["kv_off"] // kl), (d_qn, ql, 0), (d_kvn, kl, 0),
         (d_kr, LANES, 0), (cos_t, LANES, 0), (sin_t, LANES, 0)],
        [g_q, g_kv], [(ql, BF16), (kl, BF16), (LANES, BF16)], [((1, ql), F32), ((1, kl), F32)])


def _prenorm_bwd(x, d_xn, dh, g, tr):
    def body(x_ref, dxn_ref, dh_ref, g_ref, gx_ref, dg_ref):
        dx, dg = _rms_bwd(x_ref[...], g_ref[...], dxn_ref[...])
        gx_ref[...] = dh_ref[...] + dx
        _accumulate(dg_ref, _colsum(dg))

    t, d = x.shape
    return _row_call(body, "prenorm_bwd", t, tr, [(x, d, 0), (d_xn, d, 0), (dh, d, 0)], [g],
                     [(d, F32)], [((1, d), F32)])


def _adam_rows(rows, cols):
    cap = max(8, (256 * 1024) // cols // 8 * 8)
    tr = min(rows, cap)
    while rows % tr:
        tr -= 8
    return tr


def _adamw(w, g, m, v, name):
    rows, cols = w.shape
    tr = _adam_rows(rows, cols)

    def body(w_ref, g_ref, m_ref, v_ref, d_ref, mo_ref, vo_ref):
        g = g_ref[...]
        m2 = ADAM_B1 * m_ref[...] + (1.0 - ADAM_B1) * g
        v2 = ADAM_B2 * v_ref[...] + (1.0 - ADAM_B2) * (g * g)
        m_hat = m2 / (1.0 - ADAM_B1 ** ADAM_STEP)
        v_hat = v2 / (1.0 - ADAM_B2 ** ADAM_STEP)
        d_ref[...] = -ADAM_LR * (m_hat / (jnp.sqrt(v_hat) + ADAM_EPS) + ADAM_WD * w_ref[...])
        mo_ref[...] = m2
        vo_ref[...] = v2

    spec = pl.BlockSpec((tr, cols), lambda i: (i, 0))
    shape = jax.ShapeDtypeStruct((rows, cols), F32)
    return pl.pallas_call(body, name=name, grid=(rows // tr,), in_specs=[spec] * 4, out_specs=[spec] * 3,
                          out_shape=[shape] * 3, compiler_params=_params(("parallel",)))(w, g, m, v)


def _adamw_halves(w, g_mine, g_theirs, m, v, name):
    rows, cols = w.shape
    rh = g_mine.shape[0]
    tr = _adam_rows(math.gcd(rows, rh), cols)
    per_half = rh // tr
    my_c = jnp.reshape(lax.axis_index("c"), (1,)).astype(jnp.int32)

    def body(c_ref, w_ref, gm_ref, gt_ref, m_ref, v_ref, g_ref, d_ref, mo_ref, vo_ref):
        mine = (pl.program_id(0) // per_half) == c_ref[0]
        g = jnp.where(mine, gm_ref[...], gt_ref[...])
        m2 = ADAM_B1 * m_ref[...] + (1.0 - ADAM_B1) * g
        v2 = ADAM_B2 * v_ref[...] + (1.0 - ADAM_B2) * (g * g)
        m_hat = m2 / (1.0 - ADAM_B1 ** ADAM_STEP)
        v_hat = v2 / (1.0 - ADAM_B2 ** ADAM_STEP)
        g_ref[...] = g
        d_ref[...] = -ADAM_LR * (m_hat / (jnp.sqrt(v_hat) + ADAM_EPS) + ADAM_WD * w_ref[...])
        mo_ref[...] = m2
        vo_ref[...] = v2

    def half_spec(is_mine):
        def index(i, c_ref):
            used = ((i // per_half) == c_ref[0]) if is_mine else ((i // per_half) != c_ref[0])
            return (jnp.where(used, i % per_half, 0), 0)
        return pl.BlockSpec((tr, cols), index)

    spec = pl.BlockSpec((tr, cols), lambda i, c_ref: (i, 0))
    shape = jax.ShapeDtypeStruct((rows, cols), F32)
    grid_spec = pltpu.PrefetchScalarGridSpec(
        num_scalar_prefetch=1, grid=(rows // tr,),
        in_specs=[spec, half_spec(True), half_spec(False), spec, spec], out_specs=[spec] * 4)
    return pl.pallas_call(body, name=name, grid_spec=grid_spec, out_shape=[shape] * 4,
                          compiler_params=_params(("parallel",)))(my_c, w, g_mine, g_theirs, m, v)


def _pair_add(parts, theirs, name):
    _, n, r, c = parts.shape
    tr = _adam_rows(r, c)
    my_c = jnp.reshape(lax.axis_index("c"), (1,)).astype(jnp.int32)

    def body(c_ref, a_ref, b_ref, o_ref):
        o_ref[0] = (a_ref[0, 0].astype(F32) + b_ref[0].astype(F32)).astype(BF16)

    spec = pl.BlockSpec((1, tr, c), lambda k, i, c_ref: (k, i, 0))
    grid_spec = pltpu.PrefetchScalarGridSpec(
        num_scalar_prefetch=1, grid=(n, r // tr),
        in_specs=[pl.BlockSpec((1, 1, tr, c), lambda k, i, c_ref: (c_ref[0], k, i, 0)), spec], out_specs=spec)
    return pl.pallas_call(body, name=name, grid_spec=grid_spec, out_shape=jax.ShapeDtypeStruct((n, r, c), BF16),
                          compiler_params=_params(("parallel", "parallel")))(my_c, parts, theirs)


def _chip_sum(pair_sums, received, name, token=None):
    _, r, c = pair_sums.shape
    tr = _adam_rows(r, c)
    own = 2 * lax.axis_index("x") + lax.axis_index("y")

    def body(own_ref, p_ref, r0_ref, r1_ref, r2_ref, *rest):
        o_ref = rest[-1]
        acc = p_ref[0].astype(F32) + r0_ref[0].astype(F32)
        acc = acc + r1_ref[0].astype(F32)
        o_ref[...] = acc + r2_ref[0].astype(F32)

    def rspec(j):
        return pl.BlockSpec((1, tr, c), functools.partial(lambda i, own_ref, j: (j, i, 0), j=j))

    extra = [] if token is None else [token]
    grid_spec = pltpu.PrefetchScalarGridSpec(
        num_scalar_prefetch=1, grid=(r // tr,),
        in_specs=[pl.BlockSpec((1, tr, c), lambda i, own_ref: (own_ref[0], i, 0)), rspec(0), rspec(1), rspec(2)]
        + [pl.BlockSpec(memory_space=pl.ANY)] * len(extra),
        out_specs=pl.BlockSpec((tr, c), lambda i, own_ref: (i, 0)))
    return pl.pallas_call(body, name=name, grid_spec=grid_spec, out_shape=jax.ShapeDtypeStruct((r, c), F32),
                          compiler_params=_params(("parallel",)))(
        jnp.reshape(own, (1,)).astype(jnp.int32), pair_sums, received, received, received, *extra)


def _mesh_place():
    x, y, c = lax.axis_index("x"), lax.axis_index("y"), lax.axis_index("c")
    other_chips = [(1 - x, y), (x, 1 - y), (1 - x, 1 - y)]
    return x, y, c, other_chips


def _hbm_specs(n):
    return [pl.BlockSpec(memory_space=pltpu.HBM)] * n


def _all_gather(shards):
    n = len(shards)

    def body(*refs):
        ins, outs = refs[:n], refs[n:2 * n]
        send_sems, recv_sems = refs[2 * n:]
        x, y, c, other_chips = _mesh_place()
        own = 2 * x + y
        sibling = (x, y, 1 - c)

        def remote(src, dst, k, to):
            return pltpu.make_async_remote_copy(src_ref=src, dst_ref=dst, send_sem=send_sems.at[k],
                                                recv_sem=recv_sems.at[k], device_id=to, device_id_type=MESH)

        first = [remote(ins[w].at[c], outs[w].at[own, c], 7 * w + j, (*chip, c))
                 for w in range(n) for j, chip in enumerate(other_chips)]
        first += [remote(ins[w], outs[w].at[own], 7 * w + 6, sibling) for w in range(n)]
        for cp in first:
            cp.start()
        passed = []
        for w in range(n):
            for j, chip in enumerate(other_chips):
                rows = outs[w].at[2 * chip[0] + chip[1], c]
                remote(rows, rows, 7 * w + j, (*chip, c)).wait_recv()
                fwd = remote(rows, rows, 7 * w + 3 + j, sibling)
                fwd.start()
                passed.append(fwd)
        for w in range(n):
            for j, chip in enumerate(other_chips):
                rows = outs[w].at[2 * chip[0] + chip[1], 1 - c]
                remote(rows, rows, 7 * w + 3 + j, sibling).wait_recv()
            remote(ins[w], outs[w].at[own], 7 * w + 6, sibling).wait_recv()
        for cp in first + passed:
            cp.wait_send()

    return pl.pallas_call(
        body, name="weights_all_gather",
        out_shape=[jax.ShapeDtypeStruct((4,) + s.shape, s.dtype) for s in shards],
        in_specs=_hbm_specs(n), out_specs=_hbm_specs(n),
        scratch_shapes=[pltpu.SemaphoreType.DMA((7 * n,)), pltpu.SemaphoreType.DMA((7 * n,))],
    )(*shards)


def _sibling_exchange(parts, name):
    n = len(parts)

    def body(*refs):
        ins, outs = refs[:n], refs[n:2 * n]
        send_sems, recv_sems = refs[2 * n:]
        x, y, c, _ = _mesh_place()
        copies = [pltpu.make_async_remote_copy(src_ref=ins[w].at[1 - c], dst_ref=outs[w], send_sem=send_sems.at[w],
                                               recv_sem=recv_sems.at[w], device_id=(x, y, 1 - c), device_id_type=MESH)
                  for w in range(n)]
        for cp in copies:
            cp.start()
        for cp in copies:
            cp.wait()

    return pl.pallas_call(
        body, name=name,
        out_shape=[jax.ShapeDtypeStruct(p.shape[1:], p.dtype) for p in parts],
        in_specs=_hbm_specs(n), out_specs=_hbm_specs(n),
        scratch_shapes=[pltpu.SemaphoreType.DMA((n,)), pltpu.SemaphoreType.DMA((n,))],
    )(*parts)


SEM_SPEC = pl.BlockSpec(memory_space=pltpu.SEMAPHORE)
DATAFLOW_EFFECT = pltpu.SideEffectType.DATAFLOW_SIDE_EFFECTING


def _copies_per_weight(kind):
    return {"gather": 4, "scatter": 3, "sibling": 1, "forward": 3}[kind]


def _flight_copies(kind, src_refs, land_refs, send_sems, recv_sems, arriving):
    x, y, c, other_chips = _mesh_place()
    own = 2 * x + y
    sibling = (x, y, 1 - c)
    per = _copies_per_weight(kind)
    copies = []
    for w in range(len(src_refs)):
        def remote(src, dst, j, to):
            return pltpu.make_async_remote_copy(src_ref=src, dst_ref=dst, send_sem=send_sems.at[per * w + j],
                                                recv_sem=recv_sems.at[per * w + j], device_id=to, device_id_type=MESH)

        if kind == "sibling":
            copies.append(remote(src_refs[w].at[1 - c], land_refs[w], 0, sibling))
            continue
        for j, chip in enumerate(other_chips):
            theirs = 2 * chip[0] + chip[1]
            if kind == "gather":
                copies.append(remote(src_refs[w].at[c], land_refs[w].at[theirs if arriving else own, c], j, (*chip, c)))
            elif kind == "forward":
                copies.append(remote(src_refs[w].at[theirs, c], src_refs[w].at[theirs, (1 - c) if arriving else c],
                                     j, sibling))
            else:
                copies.append(remote(src_refs[w].at[theirs], land_refs[w].at[j], j, (*chip, c)))
        if kind == "gather":
            copies.append(remote(src_refs[w], land_refs[w].at[own], 3, sibling))
    return copies


def _ici_start(kind, srcs, name):
    n = len(srcs)
    if kind == "gather":
        lands = [lax.empty((4,) + s.shape, s.dtype) for s in srcs]
    elif kind == "scatter":
        lands = [lax.empty((3,) + s.shape[1:], s.dtype) for s in srcs]
    elif kind == "sibling":
        lands = [lax.empty(s.shape[1:], s.dtype) for s in srcs]
    else:
        lands = []
    nb = n + len(lands)

    def body(*refs):
        src_refs, land_refs = refs[:n], refs[n:nb]
        send_sems, recv_sems = refs[nb], refs[nb + 1]
        token = refs[-1]
        for cp in _flight_copies(kind, src_refs, land_refs, send_sems, recv_sems, False):
            cp.start()
        token[...] = jnp.zeros_like(token)

    hbm = lambda a: pltpu.with_memory_space_constraint(a, pltpu.HBM)
    n_sems = _copies_per_weight(kind) * n
    out = pl.pallas_call(
        body, name=name,
        out_shape=(pltpu.SemaphoreType.DMA((n_sems,)), pltpu.SemaphoreType.DMA((n_sems,)),
                   *[pltpu.HBM(a.shape, a.dtype) for a in srcs + lands], jax.ShapeDtypeStruct((8, LANES), F32)),
        in_specs=_hbm_specs(nb),
        out_specs=(SEM_SPEC, SEM_SPEC, *_hbm_specs(nb), pl.BlockSpec(memory_space=pltpu.VMEM)),
        input_output_aliases={i: 2 + i for i in range(nb)},
        compiler_params=pltpu.CompilerParams(has_side_effects=DATAFLOW_EFFECT),
    )(*[hbm(a) for a in srcs + lands])
    return out[0], out[1], list(out[2:2 + n]), list(out[2 + n:2 + nb]), out[-1]


def _ici_wait(kind, send_sems, recv_sems, srcs, lands, after, name):
    n = len(srcs)
    nb = n + len(lands)

    def body(*refs):
        src_refs, land_refs = refs[:n], refs[n:nb]
        send_ref, recv_ref = refs[nb], refs[nb + 1]
        for cp in _flight_copies(kind, src_refs, land_refs, send_ref, recv_ref, True):
            cp.wait_send()
            cp.wait_recv()

    out = pl.pallas_call(
        body, name=name, out_shape=tuple(pltpu.HBM(a.shape, a.dtype) for a in srcs + lands),
        in_specs=_hbm_specs(nb) + [SEM_SPEC, SEM_SPEC, pl.BlockSpec(memory_space=pl.ANY)],
        out_specs=tuple(_hbm_specs(nb)), input_output_aliases={i: i for i in range(nb)},
        compiler_params=pltpu.CompilerParams(has_side_effects=DATAFLOW_EFFECT),
    )(*srcs, *lands, send_sems, recv_sems, after)
    return list(out[:n]), list(out[n:])


def _halves_exchange(halves, name):
    n = len(halves)

    def body(*refs):
        ins, outs = refs[:n], refs[n:2 * n]
        send_sems, recv_sems = refs[2 * n:]
        x, y, c, _ = _mesh_place()
        copies = [pltpu.make_async_remote_copy(src_ref=ins[w], dst_ref=outs[w], send_sem=send_sems.at[w],
                                               recv_sem=recv_sems.at[w], device_id=(x, y, 1 - c), device_id_type=MESH)
                  for w in range(n)]
        for cp in copies:
            cp.start()
        for cp in copies:
            cp.wait()

    return pl.pallas_call(
        body, name=name,
        out_shape=[jax.ShapeDtypeStruct(h.shape, h.dtype) for h in halves],
        in_specs=_hbm_specs(n), out_specs=_hbm_specs(n),
        scratch_shapes=[pltpu.SemaphoreType.DMA((n,)), pltpu.SemaphoreType.DMA((n,))],
    )(*halves)


def _small_all_reduce(packed):
    rows = packed.shape[0]

    def body(in_ref, out_ref, gathered, send_sems, recv_sems):
        x, y, c, _ = _mesh_place()
        me = 4 * x + 2 * y + c
        gathered[0] = in_ref[...]
        copies = []
        for rel in range(1, 8):
            to = (x ^ (rel >> 2), y ^ ((rel >> 1) & 1), c ^ (rel & 1))
            cp = pltpu.make_async_remote_copy(src_ref=in_ref, dst_ref=gathered.at[rel], send_sem=send_sems.at[rel - 1],
                                              recv_sem=recv_sems.at[rel - 1], device_id=to, device_id_type=MESH)
            cp.start()
            copies.append(cp)
        for cp in copies:
            cp.wait()
        acc = gathered[me]
        for dev in range(1, 8):
            acc = acc + gathered[dev ^ me]
        out_ref[...] = acc

    return pl.pallas_call(
        body, name="small_all_reduce", out_shape=jax.ShapeDtypeStruct(packed.shape, F32),
        in_specs=[pl.BlockSpec(memory_space=pltpu.VMEM)], out_specs=pl.BlockSpec(memory_space=pltpu.VMEM),
        scratch_shapes=[pltpu.VMEM((8, rows, LANES), F32), pltpu.SemaphoreType.DMA((7,)), pltpu.SemaphoreType.DMA((7,))],
        compiler_params=pltpu.CompilerParams(vmem_limit_bytes=VMEM_LIMIT_BYTES),
    )(packed)


def _layout(w_in, w_uq, w_ukv, v_ln_gain, q_norm, kv_norm):
    heads = 4 * w_uq.shape[-1] // (NOPE_DIM + ROPE_DIM)
    gw = v_ln_gain.shape[-1]
    ql, kl = q_norm.shape[-1], kv_norm.shape[-1]
    lay = dict(heads=heads, gw=gw, ql=ql, kl=kl, aw=heads * V_DIM, u_off=0, v_off=gw, q_off=2 * gw,
               kv_off=2 * gw + ql, kr_off=2 * gw + ql + kl)
    lay["in_pad"] = _round_up(lay["kr_off"] + LANES, 2 * LANES if lay["kr_off"] + LANES <= 2048 else 1024)
    assert lay["q_off"] % ql == 0 and lay["kv_off"] % kl == 0 and lay["aw"] % gw == 0
    assert 4 * w_in.shape[-1] == ql + kl + ROPE_DIM + 2 * gw
    return lay


def _rope_tile(t1, t2, axis=-1):
    z = jnp.zeros_like(t1)
    return jnp.concatenate([t1, z, t2, z], axis=axis)


def _w_in_rows(gathered, shard_rows, lay):
    d = gathered.shape[-1]
    wt = gathered[:, :shard_rows].reshape(4 * shard_rows, d)
    ql, kl, gw = lay["ql"], lay["kl"], lay["gw"]
    q_c, kv_c = wt[:ql], wt[ql:ql + kl]
    r = wt[ql + kl:ql + kl + ROPE_DIM]
    u = wt[ql + kl + ROPE_DIM:ql + kl + ROPE_DIM + gw]
    v = wt[ql + kl + ROPE_DIM + gw:]
    parts = [u, v, q_c, kv_c, _rope_tile(r[:ROPE_HALF], r[ROPE_HALF:], axis=0)]
    pad = lay["in_pad"] - (lay["kr_off"] + LANES)
    if pad:
        parts.append(jnp.zeros((pad, d), wt.dtype))
    return jnp.concatenate(parts, axis=0)


def _w_in_grad_chunks(dwt, shard_rows, padded_rows, lay):
    d = dwt.shape[-1]
    ql, kl, gw = lay["ql"], lay["kl"], lay["gw"]
    ko = lay["kr_off"]
    rows = jnp.concatenate([dwt[lay["q_off"]:lay["q_off"] + ql], dwt[lay["kv_off"]:lay["kv_off"] + kl],
                            dwt[ko:ko + ROPE_HALF], dwt[ko + 2 * ROPE_HALF:ko + 3 * ROPE_HALF],
                            dwt[:gw], dwt[gw:2 * gw]], axis=0).reshape(4, shard_rows, d)
    rows = jnp.pad(rows, ((0, 0), (0, padded_rows - shard_rows), (0, 0)))
    return jnp.transpose(rows.reshape(4, 2, padded_rows // 2, d), (1, 0, 2, 3)).astype(BF16)


def _w_uq_padded(w, heads):
    w3 = w.reshape(w.shape[0], heads, NOPE_DIM + ROPE_DIM)
    t = _rope_tile(w3[..., NOPE_DIM:NOPE_DIM + ROPE_HALF], w3[..., NOPE_DIM + ROPE_HALF:])
    return jnp.concatenate([w3[..., :NOPE_DIM], t], axis=-1).reshape(w.shape[0], heads * HEAD_PAD)


def _w_uq_grad_unpadded(dw, heads):
    d3 = dw.reshape(dw.shape[0], heads, HEAD_PAD)
    return jnp.concatenate([d3[..., :NOPE_DIM], d3[..., NOPE_DIM:NOPE_DIM + ROPE_HALF],
                            d3[..., NOPE_DIM + 2 * ROPE_HALF:NOPE_DIM + 3 * ROPE_HALF]],
                           axis=-1).reshape(dw.shape[0], heads * (NOPE_DIM + ROPE_DIM))


def _cols_gathered(g):
    return jnp.transpose(g, (1, 0, 2)).reshape(g.shape[1], 4 * g.shape[2])


def _chunks_of_cols(grad):
    r, c4 = grad.shape
    return jnp.transpose(grad.reshape(2, r // 2, 4, c4 // 4), (0, 2, 1, 3)).astype(BF16)


SMALL = ["pre_mix_norm", "q_norm", "kv_norm", "v_ln_gain", "v_ln_bias", "w_spatial", "b_spatial", "attn_out_norm",
         "gmlp_out_norm", "post_mix_norm", "pre_ffn_norm", "post_ffn_norm"]
BIG = ["w_in", "w_uq", "w_ukv", "w_out", "w_gate", "w_up", "w_down"]
GATHER_NOW = ["w_in", "w_uq", "w_ukv"]
GATHER_LATER_1 = ["w_out", "w_gate"]
GATHER_LATER_2 = ["w_up", "w_down"]
REDUCE_FFN = ["w_gate", "w_up", "w_down"]
REDUCE_OUT = ["w_out"]
REDUCE_LAST = ["w_in", "w_uq", "w_ukv"]
TRANSPOSED = ("w_in", "w_gate", "w_up")
ORDER = ["pre_mix_norm", "w_in", "q_norm", "kv_norm", "w_uq", "w_ukv", "v_ln_gain", "v_ln_bias", "w_spatial",
         "b_spatial", "attn_out_norm", "gmlp_out_norm", "w_out", "post_mix_norm", "pre_ffn_norm", "w_gate", "w_up",
         "w_down", "post_ffn_norm"]


def _pack(arrays):
    flat = jnp.concatenate([a.reshape(-1) for a in arrays])
    n = flat.shape[0]
    total = _round_up(n, 8 * LANES)
    if total > n:
        flat = jnp.concatenate([flat, jnp.zeros((total - n,), F32)])
    return flat.reshape(total // LANES, LANES)


def _unpack(packed, like):
    flat = packed.reshape(-1)
    out, off = [], 0
    for a in like:
        out.append(flat[off:off + a.size].reshape(a.shape))
        off += a.size
    return out


def kernel(x, positions, pre_mix_norm, w_in, q_norm, kv_norm, w_uq, w_ukv, v_ln_gain, v_ln_bias, w_spatial, b_spatial, attn_out_norm, gmlp_out_norm, w_out, post_mix_norm, pre_ffn_norm, w_gate, w_up, w_down, post_ffn_norm, loss_target, m_pre_mix_norm, m_w_in, m_q_norm, m_kv_norm, m_w_uq, m_w_ukv, m_v_ln_gain, m_v_ln_bias, m_w_spatial, m_b_spatial, m_attn_out_norm, m_gmlp_out_norm, m_w_out, m_post_mix_norm, m_pre_ffn_norm, m_w_gate, m_w_up, m_w_down, m_post_ffn_norm, v_pre_mix_norm, v_w_in, v_q_norm, v_kv_norm, v_w_uq, v_w_ukv, v_v_ln_gain, v_v_ln_bias, v_w_spatial, v_b_spatial, v_attn_out_norm, v_gmlp_out_norm, v_w_out, v_post_mix_norm, v_pre_ffn_norm, v_w_gate, v_w_up, v_w_down, v_post_ffn_norm):
    args = dict(locals())
    weights = {n: args[n] for n in ORDER}
    m_in = {n: args["m_" + n] for n in ORDER}
    v_in = {n: args["v_" + n] for n in ORDER}

    lay = _layout(w_in, w_uq, w_ukv, v_ln_gain, q_norm, kv_norm)
    heads, gw = lay["heads"], lay["gw"]
    t, d = x.shape[1], x.shape[2]
    tr = 128 if t % 128 == 0 else t
    xs = x.reshape(t, d)
    target = loss_target.reshape(t, d)

    ffs, ins = w_gate.shape[-1], w_in.shape[-1]
    ffp, inp = _round_up(ffs, LANES), _round_up(ins, LANES)
    shards = {n: (jnp.swapaxes(weights[n][0], 0, 1) if n in TRANSPOSED else weights[n][0]).astype(BF16)
              for n in BIG}
    for n, rows in (("w_gate", ffp), ("w_up", ffp), ("w_down", ffp), ("w_in", inp)):
        shards[n] = jnp.pad(shards[n], ((0, rows - shards[n].shape[0]), (0, 0)))
    halved = {n: shards[n].reshape(2, shards[n].shape[0] // 2, shards[n].shape[1]) for n in BIG}
    full = {}

    def pair_sums_of(partial, names, tag):
        from_sibling = _sibling_exchange([partial[n] for n in names], "grads_sibling_exchange_" + tag)
        return [_pair_add(partial[n], r, "pair_add_" + n) for n, r in zip(names, from_sibling)]

    def place(names, lands):
        for n, g in zip(names, lands):
            full[n] = g.reshape((4,) + shards[n].shape)

    place(GATHER_NOW, _all_gather([halved[n] for n in GATHER_NOW]))
    flight_1 = _ici_start("gather", [halved[n] for n in GATHER_LATER_1], "gather_start_1")
    flight_2 = _ici_start("gather", [halved[n] for n in GATHER_LATER_2], "gather_start_2")
    wt_in = _w_in_rows(full["w_in"], ins, lay)
    wb_uq = _w_uq_padded(_cols_gathered(full["w_uq"]), heads)
    wb_ukv = _cols_gathered(full["w_ukv"])

    inv_freq = 1.0 / (ROPE_THETA ** (jnp.arange(0, ROPE_DIM, 2, dtype=F32) / ROPE_DIM))
    ang = positions.reshape(t).astype(F32)[:, None] * inv_freq
    cos, sin = jnp.cos(ang), jnp.sin(ang)
    cos_t = _rope_tile(cos, cos)
    sin_t = _rope_tile(-sin, sin)

    row = lambda a: a.reshape(1, -1)
    g_pre, g_q, g_kv = row(pre_mix_norm), row(q_norm), row(kv_norm)
    g_a, g_g, g_pm = row(attn_out_norm), row(gmlp_out_norm), row(post_mix_norm)
    g_pf, g_po = row(pre_ffn_norm), row(post_ffn_norm)
    ln_g, ln_b = row(v_ln_gain), row(v_ln_bias)
    ws = w_spatial[0].astype(BF16)
    ws_t = jnp.transpose(ws, (0, 2, 1))
    bs_wide = jnp.broadcast_to(b_spatial[0][:, :, None], b_spatial.shape[1:] + (G_HEAD_DIM,))

    xn = _prenorm(xs, g_pre, tr, flight_1[4] + flight_2[4])
    proj = _matmul(xn, wt_in, NT, F32, "proj")
    qn, kvn, kr = _qkv_prep(proj, g_q, g_kv, cos_t, sin_t, lay, tr)
    q = _q_rope(_matmul(qn, wb_uq, NN, F32, "q_up"), cos_t, sin_t, heads, tr)
    kv = _matmul(kvn, wb_ukv, NN, BF16, "kv_up")
    a_out, a_lse = _attn_fwd(q, kv, kr, heads)
    gn = _gmlp_fwd(proj, ln_g, ln_b, ws, bs_wide, g_g, lay)
    mixed = _mix_norm(a_out, gn, g_a, tr)
    _, lands = _ici_wait("gather", *flight_1[:4], mixed, "gather_wait_1")
    pass_out = _ici_start("forward", lands[:1], "forward_start_out")
    pass_gate = _ici_start("forward", lands[1:], "forward_start_gate")
    place(["w_out"], _ici_wait("forward", *pass_out[:4], mixed, "forward_wait_out")[0])
    wb_out = full["w_out"].reshape(-1, d)
    mix_out = _matmul(mixed, wb_out, NN, F32, "mix_out", token=pass_gate[4])
    h, hn = _post_mix(xs, mix_out, g_pm, g_pf, tr)
    place(["w_gate"], _ici_wait("forward", *pass_gate[:4], hn, "forward_wait_gate")[0])
    wt_gate = full["w_gate"].reshape(4 * ffp, d)
    gate = _matmul(hn, wt_gate, NT, BF16, "ffn_gate")
    _, lands = _ici_wait("gather", *flight_2[:4], gate, "gather_wait_2")
    pass_up = _ici_start("forward", lands[:1], "forward_start_up")
    pass_down = _ici_start("forward", lands[1:], "forward_start_down")
    place(["w_up"], _ici_wait("forward", *pass_up[:4], gate, "forward_wait_up")[0])
    wt_up = full["w_up"].reshape(4 * ffp, d)
    up = _matmul(hn, wt_up, NT, BF16, "ffn_up", token=pass_down[4])
    act = _swiglu(gate, up)
    place(["w_down"], _ici_wait("forward", *pass_down[:4], act, "forward_wait_down")[0])
    wb_down = full["w_down"].reshape(4 * ffp, d)
    ffn = _matmul(act, wb_down, NN, F32, "ffn_down")
    dy, d_ffn, dg_po, loss_vec = _loss_head(h, ffn, target, g_po, tr)

    d_act = _matmul(d_ffn, wb_down, NT, BF16, "d_act")
    d_gate, d_up = _swiglu_bwd(gate, up, d_act)
    partial_ffn = [_matmul(d_gate, hn, TN, BF16, "gw_gate", out_chunks=True),
                   _matmul(d_up, hn, TN, BF16, "gw_up", out_chunks=True),
                   _matmul(act, d_ffn, TN, BF16, "gw_down", out_chunks=True)]
    swap_ffn = _ici_start("sibling", partial_ffn, "sibling_start_ffn")
    d_hn = _matmul(d_up, wt_up, NN, F32, "d_hn",
                   extras=[_matmul(d_gate, wt_gate, NN, F32, "d_hn_gate", token=swap_ffn[4])],
                   epilogue=lambda acc, partial: acc + partial)
    partial_ffn, from_sibling = _ici_wait("sibling", *swap_ffn[:4], d_hn, "sibling_wait_ffn")
    pair_ffn = [_pair_add(p, r, "pair_add_" + n) for n, p, r in zip(REDUCE_FFN, partial_ffn, from_sibling)]
    flight_ffn = _ici_start("scatter", pair_ffn, "scatter_start_ffn")
    dh, d_mo, dg_pf, dg_pm = _post_mix_bwd(mix_out, h, dy, d_hn, g_pm, g_pf, tr, flight_ffn[4])
    d_mixed = _matmul(d_mo, wb_out, NT, F32, "d_mixed")
    gw_out = _matmul(mixed, d_mo, TN, BF16, "gw_out", out_chunks=True)
    pair_out = pair_sums_of({"w_out": gw_out}, REDUCE_OUT, "out")
    flight_out = _ici_start("scatter", pair_out, "scatter_start_out")
    d_a, dg_a = _mix_norm_bwd(a_out, d_mixed, g_a, tr, flight_out[4])
    d_u, d_v, dg_g, d_ln_g, d_ln_b, d_ws, d_bs_wide = _gmlp_bwd(proj, d_mixed, ln_g, ln_b, ws, ws_t, bs_wide, g_g, lay)
    d_bs = _spatial_bias_grad(d_bs_wide)
    d_q, d_kv, d_kr = _attn_bwd(q, kv, kr, a_out, a_lse, d_a, cos_t, sin_t, heads)
    d_qn = _matmul(d_q, wb_uq, NT, F32, "d_qn")
    gw_uq = _matmul(qn, d_q, TN, F32, "gw_uq")
    d_kvn = _matmul(d_kv, wb_ukv, NT, F32, "d_kvn")
    gw_ukv = _matmul(kvn, d_kv, TN, F32, "gw_ukv")
    d_qc, d_kvc, d_krt, dg_q, dg_kv = _qkv_bwd(proj, d_qn, d_kvn, d_kr, g_q, g_kv, cos_t, sin_t, lay, tr)
    parts = [d_u, d_v, d_qc, d_kvc, d_krt]
    pad = lay["in_pad"] - (lay["kr_off"] + LANES)
    if pad:
        parts.append(jnp.zeros((t, pad), BF16))
    d_proj = jnp.concatenate(parts, axis=1)
    d_xn = _matmul(d_proj, wt_in, NN, F32, "d_xn")
    gw_in = _matmul(d_proj, xn, TN, F32, "gw_in")
    grad_x, dg_pre = _prenorm_bwd(xs, d_xn, dh, g_pre, tr)

    pair_mix = pair_sums_of({"w_in": _w_in_grad_chunks(gw_in, ins, inp, lay),
                             "w_uq": _chunks_of_cols(_w_uq_grad_unpadded(gw_uq, heads)),
                             "w_ukv": _chunks_of_cols(gw_ukv)}, REDUCE_LAST, "mix")
    pair_ffn, received_ffn = _ici_wait("scatter", *flight_ffn[:4], pair_mix[-1], "scatter_wait_ffn")
    pair_out, received_out = _ici_wait("scatter", *flight_out[:4], pair_mix[-1], "scatter_wait_out")
    flight_mix = _ici_start("scatter", pair_mix, "scatter_start_mix")
    grads, delta, new_m, new_v = {}, {}, {}, {}

    def finish(names, pair_sums, received, tag, token):
        mine = [_chip_sum(p, r, "chip_sum_" + n, token) for n, p, r in zip(names, pair_sums, received)]
        theirs = _halves_exchange(mine, "grads_halves_exchange_" + tag)
        for n, g_mine, g_theirs in zip(names, mine, theirs):
            shape = weights[n].shape
            if n in TRANSPOSED:
                view = lambda a: jnp.swapaxes(a[0], 0, 1)
                back = lambda o: jnp.swapaxes(o, 0, 1).reshape(shape)
            else:
                view = lambda a: a[0]
                back = lambda o: o.reshape(shape)
            out = _adamw_halves(view(weights[n]), g_mine, g_theirs, view(m_in[n]), view(v_in[n]), "adamw_" + n)
            grads[n], delta[n], new_m[n], new_v[n] = [back(o) for o in out]

    finish(REDUCE_FFN + REDUCE_OUT, pair_ffn + pair_out, received_ffn + received_out, "ffn", flight_mix[4])
    pair_mix, received = _ici_wait("scatter", *flight_mix[:4], new_v[REDUCE_OUT[-1]], "scatter_wait_mix")
    finish(REDUCE_LAST, pair_mix, received, "mix", None)

    small_grads = {"pre_mix_norm": dg_pre, "q_norm": dg_q, "kv_norm": dg_kv, "v_ln_gain": d_ln_g, "v_ln_bias": d_ln_b,
                   "w_spatial": d_ws, "b_spatial": d_bs, "attn_out_norm": dg_a, "gmlp_out_norm": dg_g,
                   "post_mix_norm": dg_pm, "pre_ffn_norm": dg_pf, "post_ffn_norm": dg_po}
    like = [weights[n] for n in SMALL]
    reduced = _small_all_reduce(_pack([small_grads[n] for n in SMALL] + [loss_vec]))
    loss = reduced.reshape(-1)[sum(a.size for a in like)]
    small_g = _pack(_unpack(reduced, like))
    s_delta, s_m, s_v = _adamw(_pack(like), small_g, _pack([m_in[n] for n in SMALL]),
                               _pack([v_in[n] for n in SMALL]), "adamw_small")
    for n, g in zip(SMALL, _unpack(small_g, like)):
        grads[n] = g
    delta.update(zip(SMALL, _unpack(s_delta, like)))
    new_m.update(zip(SMALL, _unpack(s_m, like)))
    new_v.update(zip(SMALL, _unpack(s_v, like)))

    return (loss, grad_x.reshape(x.shape), *[grads[n] for n in ORDER], *[delta[n] for n in ORDER],
            *[new_m[n] for n in ORDER], *[new_v[n] for n in ORDER])
```

```python
import functools
import math

import jax
import jax.numpy as jnp
from jax import lax
from jax.experimental import pallas as pl
from jax.experimental.pallas import tpu as pltpu

F32 = jnp.float32
BF16 = jnp.bfloat16
MESH = pl.DeviceIdType.MESH

NOPE_DIM = 128
ROPE_DIM = 64
ROPE_HALF = ROPE_DIM // 2
V_DIM = 128
HEAD_PAD = 256
G_HEAD_DIM = 128
CHUNK = 128
ROPE_THETA = 10000.0
EPS = 1e-6
ADAM_LR = 0.001
ADAM_B1 = 0.9
ADAM_B2 = 0.999
ADAM_EPS = 1e-08
ADAM_WD = 0.01
ADAM_STEP = 10

LANES = 128
MATMUL_TILE = 1024
WIDE_TILE = 1408
VMEM_LIMIT_BYTES = 48 * 1024 * 1024

NN = (((1,), (0,)), ((), ()))
NT = (((1,), (1,)), ((), ()))
TN = (((0,), (0,)), ((), ()))


def _params(semantics):
    return pltpu.CompilerParams(dimension_semantics=semantics, vmem_limit_bytes=VMEM_LIMIT_BYTES)


def _tile(n, cap=MATMUL_TILE):
    if n <= cap:
        return n
    if cap == MATMUL_TILE and n % WIDE_TILE == 0:
        return WIDE_TILE
    t = cap - cap % LANES
    while n % t:
        t -= LANES
    assert t > 0, n
    return t


def _round_up(n, m):
    return (n + m - 1) // m * m


def _matmul(a, b, dims, out_dtype, name, extras=(), epilogue=None, out_chunks=None, token=None):
    if dims is NN:
        (m, k), (k2, n) = a.shape, b.shape
    elif dims is NT:
        (m, k), (n, k2) = a.shape, b.shape
    else:
        (k, m), (k2, n) = a.shape, b.shape
    assert k == k2, (a.shape, b.shape, name)
    tm, tn, tk = _tile(m // 8 if out_chunks else m), _tile(n), _tile(k)
    if len(extras) + (len(out_dtype) if isinstance(out_dtype, tuple) else 1) > 2:
        tm = _tile(m, MATMUL_TILE // 2)
    nk = k // tk

    out_dtypes = out_dtype if isinstance(out_dtype, tuple) else (out_dtype,)
    n_extra = len(extras)

    def body(*refs):
        a_ref, b_ref = refs[:2]
        extra_refs = refs[2:2 + n_extra]
        out_refs = refs[2 + n_extra + (token is not None):-1]
        acc_ref = refs[-1]
        kk = pl.program_id(2)

        @pl.when(kk == 0)
        def _():
            acc_ref[...] = jnp.zeros_like(acc_ref)

        acc_ref[...] += lax.dot_general(a_ref[...], b_ref[...], dims, preferred_element_type=F32)

        @pl.when(kk == nk - 1)
        def _():
            r = acc_ref[...]
            if epilogue is not None:
                r = epilogue(r, *[e[...] for e in extra_refs])
            for o_ref, val in zip(out_refs, r if isinstance(r, tuple) else (r,)):
                o_ref[...] = val.astype(o_ref.dtype)

    if dims is TN:
        a_spec = pl.BlockSpec((tk, tm), lambda i, j, kk: (kk, i))
    else:
        a_spec = pl.BlockSpec((tm, tk), lambda i, j, kk: (i, kk))
    if dims is NT:
        b_spec = pl.BlockSpec((tn, tk), lambda i, j, kk: (j, kk))
    else:
        b_spec = pl.BlockSpec((tk, tn), lambda i, j, kk: (kk, j))
    if not out_chunks:
        o_spec = pl.BlockSpec((tm, tn), lambda i, j, kk: (i, j))
        o_shape = (m, n)
    else:
        pi = m // 8 // tm
        o_spec = pl.BlockSpec((None, None, tm, tn), lambda i, j, kk: ((i // pi) % 2, i // (2 * pi), i % pi, j))
        o_shape = (2, 4, m // 8, n)
    assert not (extras and out_chunks)
    tokens = [] if token is None else [token]
    out = pl.pallas_call(
        body, name=name, grid=(m // tm, n // tn, nk),
        in_specs=[a_spec, b_spec] + [o_spec] * n_extra + [pl.BlockSpec(memory_space=pl.ANY)] * len(tokens),
        out_specs=[o_spec] * len(out_dtypes),
        out_shape=[jax.ShapeDtypeStruct(o_shape, dt) for dt in out_dtypes],
        scratch_shapes=[pltpu.VMEM((tm, tn), F32)],
        compiler_params=_params(("parallel", "parallel", "arbitrary")),
    )(a, b, *extras, *tokens)
    return tuple(out) if isinstance(out_dtype, tuple) else out[0]


def _row_call(body, name, rows, tr, row_ins, par_ins, row_outs, acc_outs):
    def col(i, cb):
        return (i, cb)

    def whole(i, nd):
        return (0,) * nd

    in_specs = [pl.BlockSpec((tr, w), functools.partial(col, cb=cb)) for (_, w, cb) in row_ins]
    in_specs += [pl.BlockSpec(a.shape, functools.partial(whole, nd=a.ndim)) for a in par_ins]
    out_specs = [pl.BlockSpec((tr, w), lambda i: (i, 0)) for (w, _) in row_outs]
    out_specs += [pl.BlockSpec(s, functools.partial(whole, nd=len(s))) for (s, _) in acc_outs]
    out_shape = [jax.ShapeDtypeStruct((rows, w), dt) for (w, dt) in row_outs]
    out_shape += [jax.ShapeDtypeStruct(s, dt) for (s, dt) in acc_outs]
    return pl.pallas_call(
        body, name=name, grid=(rows // tr,), in_specs=in_specs, out_specs=out_specs, out_shape=out_shape,
        compiler_params=_params(("arbitrary",) if acc_outs else ("parallel",)),
    )(*[a for (a, _, _) in row_ins], *par_ins)


def _accumulate(ref, val):
    i = pl.program_id(0)

    @pl.when(i == 0)
    def _():
        ref[...] = val

    @pl.when(i > 0)
    def _():
        ref[...] += val


def _colsum(v):
    return jnp.sum(v, axis=0, keepdims=True)


def _rms_fwd(x, g):
    r = lax.rsqrt(jnp.mean(x * x, axis=-1, keepdims=True) + EPS)
    return x * r * g


def _rms_bwd(x, g, dy):
    r = lax.rsqrt(jnp.mean(x * x, axis=-1, keepdims=True) + EPS)
    xh = x * r
    dxh = dy * g
    dx = r * (dxh - xh * jnp.mean(dxh * xh, axis=-1, keepdims=True))
    return dx, dy * xh


_GELU_C = math.sqrt(2.0 / math.pi)
_GELU_A = 0.044715


def _gelu(x):
    return 0.5 * x * (1.0 + jnp.tanh(_GELU_C * (x + _GELU_A * (x * x * x))))


def _gelu_grad(x):
    t = jnp.tanh(_GELU_C * (x + _GELU_A * (x * x * x)))
    return 0.5 * (1.0 + t) + 0.5 * x * (1.0 - t * t) * (_GELU_C * (1.0 + 3.0 * _GELU_A * (x * x)))


def _sigmoid(x):
    return 1.0 / (1.0 + jnp.exp(-x))


def _rope_fwd(t, cos_t, sin_t):
    return t * cos_t + pltpu.roll(t, 2 * ROPE_HALF, 1) * sin_t


def _rope_bwd(dt, cos_t, sin_t):
    return dt * cos_t - pltpu.roll(dt, 2 * ROPE_HALF, 1) * sin_t


def _prenorm(x, g, tr, token):
    def body(x_ref, g_ref, token_ref, o_ref):
        o_ref[...] = _rms_fwd(x_ref[...], g_ref[...]).astype(BF16)

    t, d = x.shape
    return _row_call(body, "prenorm", t, tr, [(x, d, 0)], [g, token], [(d, BF16)], [])[0]


def _qkv_prep(proj, g_q, g_kv, cos_t, sin_t, lay, tr):
    ql, kl = lay["ql"], lay["kl"]

    def body(q_ref, kv_ref, kr_ref, cos_ref, sin_ref, gq_ref, gkv_ref, qn_ref, kvn_ref, kro_ref):
        qn_ref[...] = _rms_fwd(q_ref[...], gq_ref[...]).astype(BF16)
        kvn_ref[...] = _rms_fwd(kv_ref[...], gkv_ref[...]).astype(BF16)
        kro_ref[...] = _rope_fwd(kr_ref[...], cos_ref[...], sin_ref[...]).astype(BF16)

    t = proj.shape[0]
    return _row_call(
        body, "qkv_prep", t, tr,
        [(proj, ql, lay["q_off"] // ql), (proj, kl, lay["kv_off"] // kl), (proj, LANES, lay["kr_off"] // LANES),
         (cos_t, LANES, 0), (sin_t, LANES, 0)],
        [g_q, g_kv], [(ql, BF16), (kl, BF16), (LANES, BF16)], [])


def _q_rope(q, cos_t, sin_t, heads, tr):
    def body(q_ref, cos_ref, sin_ref, o_ref):
        c, s = cos_ref[...], sin_ref[...]
        for h in range(heads):
            lo = h * HEAD_PAD
            o_ref[:, lo:lo + NOPE_DIM] = q_ref[:, lo:lo + NOPE_DIM].astype(BF16)
            o_ref[:, lo + NOPE_DIM:lo + HEAD_PAD] = _rope_fwd(q_ref[:, lo + NOPE_DIM:lo + HEAD_PAD], c, s).astype(BF16)

    t, w = q.shape
    return _row_call(body, "q_rope", t, tr, [(q, w, 0), (cos_t, LANES, 0), (sin_t, LANES, 0)], [], [(w, BF16)], [])[0]


ATTN_SCALE = 1.0 / math.sqrt(NOPE_DIM + ROPE_DIM)
ATTN_EXP2_SCALE = ATTN_SCALE * math.log2(math.e)


def _attn_tile(t, cap):
    tq = cap
    while t % tq:
        tq //= 2
    return tq


def _attn_fwd(q, kv, kr, heads):
    t = q.shape[0]
    tq = _attn_tile(t, 256)

    def body(q_ref, kv_ref, kr_ref, o_ref, lse_ref, kcat):
        @pl.when(pl.program_id(1) == 0)
        def _():
            kcat[:, :NOPE_DIM] = kv_ref[:, :NOPE_DIM]
            kcat[:, NOPE_DIM:] = kr_ref[...]

        s = lax.dot_general(q_ref[...], kcat[...], NT, preferred_element_type=F32)
        m = jnp.max(s, axis=-1, keepdims=True)
        p = jnp.exp2((s - m) * ATTN_EXP2_SCALE)
        l = jnp.sum(p, axis=-1, keepdims=True)
        o_ref[...] = jnp.dot(p.astype(BF16), kv_ref[:, NOPE_DIM:], preferred_element_type=F32) * (1.0 / l)
        lse_ref[...] = jnp.broadcast_to(m * ATTN_EXP2_SCALE + jnp.log(l) * math.log2(math.e), (tq, V_DIM))

    out_spec = pl.BlockSpec((tq, V_DIM), lambda h, i: (i, h))
    out_shape = jax.ShapeDtypeStruct((t, heads * V_DIM), F32)
    return pl.pallas_call(
        body, name="attn_fwd", grid=(heads, t // tq),
        in_specs=[pl.BlockSpec((tq, HEAD_PAD), lambda h, i: (i, h)),
                  pl.BlockSpec((t, HEAD_PAD), lambda h, i: (0, h)),
                  pl.BlockSpec((t, LANES), lambda h, i: (0, 0))],
        out_specs=[out_spec, out_spec], out_shape=[out_shape, out_shape],
        scratch_shapes=[pltpu.VMEM((t, HEAD_PAD), BF16)],
        compiler_params=_params(("arbitrary", "arbitrary")),
    )(q, kv, kr)


def _attn_bwd(q, kv, kr, out, lse, d_out, cos_t, sin_t, heads):
    t = q.shape[0]
    tq = _attn_tile(t, 256)
    nq = t // tq

    def body(q_ref, kv_ref, kr_ref, o_ref, lse_ref, do_ref, cos_ref, sin_ref, dq_ref, dkv_ref, dkr_ref,
             kcat, dk_acc, dv_acc):
        h, i = pl.program_id(0), pl.program_id(1)

        @pl.when(i == 0)
        def _():
            kcat[:, :NOPE_DIM] = kv_ref[:, :NOPE_DIM]
            kcat[:, NOPE_DIM:] = kr_ref[...]
            dk_acc[...] = jnp.zeros_like(dk_acc)
            dv_acc[...] = jnp.zeros_like(dv_acc)

        @pl.when((h == 0) & (i == 0))
        def _():
            dkr_ref[...] = jnp.zeros_like(dkr_ref)

        qb, dob = q_ref[...], do_ref[...]
        row_term = jnp.sum(dob.astype(F32) * o_ref[...], axis=-1, keepdims=True)
        s = lax.dot_general(qb, kcat[...], NT, preferred_element_type=F32)
        dp = lax.dot_general(dob, kv_ref[:, NOPE_DIM:], NT, preferred_element_type=F32)
        p = jnp.exp2(s * ATTN_EXP2_SCALE - lse_ref[:, :1])
        ds = (p * (dp - row_term)).astype(BF16)
        dv_acc[...] += lax.dot_general(p.astype(BF16), dob, TN, preferred_element_type=F32)
        dq = jnp.dot(ds, kcat[...], preferred_element_type=F32) * ATTN_SCALE
        dq_ref[:, :NOPE_DIM] = dq[:, :NOPE_DIM].astype(BF16)
        dq_ref[:, NOPE_DIM:] = _rope_bwd(dq[:, NOPE_DIM:], cos_ref[...], sin_ref[...]).astype(BF16)
        dk_acc[...] += lax.dot_general(ds, qb, TN, preferred_element_type=F32)

        @pl.when(i == nq - 1)
        def _():
            dkv_ref[:, :NOPE_DIM] = (dk_acc[:, :NOPE_DIM] * ATTN_SCALE).astype(BF16)
            dkv_ref[:, NOPE_DIM:] = dv_acc[...].astype(BF16)
            dkr_ref[...] += dk_acc[:, NOPE_DIM:] * ATTN_SCALE

    return pl.pallas_call(
        body, name="attn_bwd", grid=(heads, nq),
        in_specs=[pl.BlockSpec((tq, HEAD_PAD), lambda h, i: (i, h)),
                  pl.BlockSpec((t, HEAD_PAD), lambda h, i: (0, h)),
                  pl.BlockSpec((t, LANES), lambda h, i: (0, 0)),
                  pl.BlockSpec((tq, V_DIM), lambda h, i: (i, h)),
                  pl.BlockSpec((tq, V_DIM), lambda h, i: (i, h)),
                  pl.BlockSpec((tq, V_DIM), lambda h, i: (i, h)),
                  pl.BlockSpec((tq, LANES), lambda h, i: (i, 0)),
                  pl.BlockSpec((tq, LANES), lambda h, i: (i, 0))],
        out_specs=[pl.BlockSpec((tq, HEAD_PAD), lambda h, i: (i, h)),
                   pl.BlockSpec((t, HEAD_PAD), lambda h, i: (0, h)),
                   pl.BlockSpec((t, LANES), lambda h, i: (0, 0))],
        out_shape=[jax.ShapeDtypeStruct((t, heads * HEAD_PAD), BF16),
                   jax.ShapeDtypeStruct((t, heads * HEAD_PAD), BF16),
                   jax.ShapeDtypeStruct((t, LANES), F32)],
        scratch_shapes=[pltpu.VMEM((t, HEAD_PAD), BF16), pltpu.VMEM((t, HEAD_PAD), F32), pltpu.VMEM((t, V_DIM), F32)],
        compiler_params=_params(("arbitrary", "arbitrary")),
    )(q, kv, kr, out, lse, d_out, cos_t, sin_t)


def _layer_norm_parts(x):
    mu = jnp.mean(x, axis=-1, keepdims=True)
    xc = x - mu
    r = lax.rsqrt(jnp.mean(xc * xc, axis=-1, keepdims=True) + EPS)
    return xc * r, r


def _gmlp_fwd(proj, ln_g, ln_b, w_s, b_sb, g_out_norm, lay):
    gw = lay["gw"]
    g_heads = gw // G_HEAD_DIM

    def body(u_ref, v_ref, lng_ref, lnb_ref, ws_ref, bs_ref, gn_ref, o_ref, gate_ref):
        gu = _gelu(u_ref[...])
        vh, _ = _layer_norm_parts(_gelu(v_ref[...]))
        vln = (vh * lng_ref[...] + lnb_ref[...]).astype(BF16)
        for g in range(g_heads):
            cols = slice(g * G_HEAD_DIM, (g + 1) * G_HEAD_DIM)
            s = jnp.dot(ws_ref[g], vln[:, cols], preferred_element_type=F32) + bs_ref[g]
            gate_ref[:, cols] = gu[:, cols] * s
        o_ref[...] = _rms_fwd(gate_ref[...], gn_ref[...]).astype(BF16)

    t = proj.shape[0]
    in_specs = [pl.BlockSpec((CHUNK, gw), lambda i: (i, 0)), pl.BlockSpec((CHUNK, gw), lambda i: (i, 1))]
    pars = [ln_g, ln_b, w_s, b_sb, g_out_norm]
    in_specs += [pl.BlockSpec(a.shape, functools.partial(lambda i, nd: (0,) * nd, nd=a.ndim)) for a in pars]
    return pl.pallas_call(
        body, name="gmlp_fwd", grid=(t // CHUNK,), in_specs=in_specs,
        out_specs=pl.BlockSpec((CHUNK, gw), lambda i: (i, 0)),
        out_shape=jax.ShapeDtypeStruct((t, gw), BF16),
        scratch_shapes=[pltpu.VMEM((CHUNK, gw), F32)],
        compiler_params=_params(("parallel",)),
    )(proj, proj, *pars)


def _gmlp_bwd(proj, d_mixed, ln_g, ln_b, w_s, w_st, b_sb, g_out_norm, lay):
    gw = lay["gw"]
    g_heads = gw // G_HEAD_DIM
    aw_blocks = lay["aw"] // gw

    def body(u_ref, v_ref, dm_ref, lng_ref, lnb_ref, ws_ref, wst_ref, bs_ref, gn_ref,
             du_ref, dv_ref, dgn_ref, dlng_ref, dlnb_ref, dws_ref, dbs_ref, gate_ref, s_ref, dvln_ref):
        i = pl.program_id(0)
        u, v = u_ref[...], v_ref[...]
        gu, gv = _gelu(u), _gelu(v)
        vh, r_ln = _layer_norm_parts(gv)
        vln = (vh * lng_ref[...] + lnb_ref[...]).astype(BF16)
        for g in range(g_heads):
            cols = slice(g * G_HEAD_DIM, (g + 1) * G_HEAD_DIM)
            s = jnp.dot(ws_ref[g], vln[:, cols], preferred_element_type=F32) + bs_ref[g]
            s_ref[:, cols] = s
            gate_ref[:, cols] = gu[:, cols] * s
        d_gate, dgn = _rms_bwd(gate_ref[...], gn_ref[...], dm_ref[...])
        _accumulate(dgn_ref, _colsum(dgn))
        du_ref[...] = (d_gate * s_ref[...] * _gelu_grad(u)).astype(BF16)
        d_s = d_gate * gu
        d_sb = d_s.astype(BF16)
        for g in range(g_heads):
            cols = slice(g * G_HEAD_DIM, (g + 1) * G_HEAD_DIM)
            dw = lax.dot_general(d_sb[:, cols], vln[:, cols], NT, preferred_element_type=F32)

            @pl.when(i == 0)
            def _():
                dws_ref[g] = dw
                dbs_ref[g] = d_s[:, cols]

            @pl.when(i > 0)
            def _():
                dws_ref[g] += dw
                dbs_ref[g] += d_s[:, cols]

            dvln_ref[:, cols] = jnp.dot(wst_ref[g], d_sb[:, cols], preferred_element_type=F32)
        d_vln = dvln_ref[...]
        _accumulate(dlng_ref, _colsum(d_vln * vh))
        _accumulate(dlnb_ref, _colsum(d_vln))
        d_vh = d_vln * lng_ref[...]
        d_gv = r_ln * (d_vh - jnp.mean(d_vh, axis=-1, keepdims=True)
                       - vh * jnp.mean(d_vh * vh, axis=-1, keepdims=True))
        dv_ref[...] = (d_gv * _gelu_grad(v)).astype(BF16)

    t = proj.shape[0]
    whole = lambda a: pl.BlockSpec(a.shape, functools.partial(lambda i, nd: (0,) * nd, nd=a.ndim))
    pars = [ln_g, ln_b, w_s, w_st, b_sb, g_out_norm]
    hshape = (g_heads, CHUNK, CHUNK)
    return pl.pallas_call(
        body, name="gmlp_bwd", grid=(t // CHUNK,),
        in_specs=[pl.BlockSpec((CHUNK, gw), lambda i: (i, 0)), pl.BlockSpec((CHUNK, gw), lambda i: (i, 1)),
                  pl.BlockSpec((CHUNK, gw), lambda i: (i, aw_blocks))] + [whole(a) for a in pars],
        out_specs=[pl.BlockSpec((CHUNK, gw), lambda i: (i, 0)), pl.BlockSpec((CHUNK, gw), lambda i: (i, 0)),
                   pl.BlockSpec((1, gw), lambda i: (0, 0)), pl.BlockSpec((1, gw), lambda i: (0, 0)),
                   pl.BlockSpec((1, gw), lambda i: (0, 0)),
                   pl.BlockSpec(hshape, lambda i: (0, 0, 0)), pl.BlockSpec(hshape, lambda i: (0, 0, 0))],
        out_shape=[jax.ShapeDtypeStruct((t, gw), BF16), jax.ShapeDtypeStruct((t, gw), BF16),
                   jax.ShapeDtypeStruct((1, gw), F32), jax.ShapeDtypeStruct((1, gw), F32),
                   jax.ShapeDtypeStruct((1, gw), F32),
                   jax.ShapeDtypeStruct(hshape, F32), jax.ShapeDtypeStruct(hshape, F32)],
        scratch_shapes=[pltpu.VMEM((CHUNK, gw), F32), pltpu.VMEM((CHUNK, gw), F32), pltpu.VMEM((CHUNK, gw), F32)],
        compiler_params=_params(("arbitrary",)),
    )(proj, proj, d_mixed, *pars)


def _spatial_bias_grad(dbs_wide):
    g_heads = dbs_wide.shape[0]

    def body(x_ref, o_ref):
        for g in range(g_heads):
            o_ref[g:g + 1, :] = jnp.sum(x_ref[g].T, axis=0, keepdims=True)

    return pl.pallas_call(
        body, name="spatial_bias_grad", out_shape=jax.ShapeDtypeStruct((g_heads, CHUNK), F32),
        in_specs=[pl.BlockSpec(memory_space=pltpu.VMEM)], out_specs=pl.BlockSpec(memory_space=pltpu.VMEM),
    )(dbs_wide)


def _mix_norm(a_out, gn, g_a, tr):
    aw = a_out.shape[1]
    gw = gn.shape[1]

    def body(a_ref, gn_ref, g_ref, o_ref):
        o_ref[:, :aw] = _rms_fwd(a_ref[...], g_ref[...]).astype(BF16)
        o_ref[:, aw:] = gn_ref[...]

    t = a_out.shape[0]
    return _row_call(body, "mix_norm", t, tr, [(a_out, aw, 0), (gn, gw, 0)], [g_a], [(aw + gw, BF16)], [])[0]


def _mix_norm_bwd(a_out, d_mixed, g_a, tr, token):
    aw = a_out.shape[1]

    def body(a_ref, dm_ref, g_ref, token_ref, da_ref, dg_ref):
        dx, dg = _rms_bwd(a_ref[...], g_ref[...], dm_ref[...])
        da_ref[...] = dx.astype(BF16)
        _accumulate(dg_ref, _colsum(dg))

    t = a_out.shape[0]
    return _row_call(body, "mix_norm_bwd", t, tr, [(a_out, aw, 0), (d_mixed, aw, 0)], [g_a, token],
                     [(aw, BF16)], [((1, aw), F32)])


def _post_mix(x, mix_out, g_pm, g_pf, tr):
    def body(x_ref, mo_ref, gpm_ref, gpf_ref, h_ref, hn_ref):
        h = x_ref[...] + _rms_fwd(mo_ref[...], gpm_ref[...])
        h_ref[...] = h
        hn_ref[...] = _rms_fwd(h, gpf_ref[...]).astype(BF16)

    t, d = x.shape
    return _row_call(body, "post_mix", t, tr, [(x, d, 0), (mix_out, d, 0)], [g_pm, g_pf], [(d, F32), (d, BF16)], [])


def _post_mix_bwd(mix_out, h, dy, d_hn, g_pm, g_pf, tr, token):
    def body(mo_ref, h_ref, dy_ref, dhn_ref, gpm_ref, gpf_ref, token_ref, dh_ref, dmo_ref, dgpf_ref, dgpm_ref):
        dx, dg = _rms_bwd(h_ref[...], gpf_ref[...], dhn_ref[...])
        dh = dy_ref[...] + dx
        dh_ref[...] = dh
        _accumulate(dgpf_ref, _colsum(dg))
        dmo, dg2 = _rms_bwd(mo_ref[...], gpm_ref[...], dh)
        dmo_ref[...] = dmo.astype(BF16)
        _accumulate(dgpm_ref, _colsum(dg2))

    t, d = h.shape
    return _row_call(body, "post_mix_bwd", t, tr, [(mix_out, d, 0), (h, d, 0), (dy, d, 0), (d_hn, d, 0)],
                     [g_pm, g_pf, token], [(d, F32), (d, BF16)], [((1, d), F32), ((1, d), F32)])


def _swiglu(gate, up):
    t, f = gate.shape
    tr, tf = _tile(t, 512), _tile(f, 2048)

    def body(g_ref, u_ref, o_ref):
        g = g_ref[...].astype(F32)
        o_ref[...] = (g * _sigmoid(g) * u_ref[...].astype(F32)).astype(BF16)

    spec = pl.BlockSpec((tr, tf), lambda i, j: (i, j))
    return pl.pallas_call(body, name="swiglu", grid=(t // tr, f // tf), in_specs=[spec, spec], out_specs=spec,
                          out_shape=jax.ShapeDtypeStruct((t, f), BF16),
                          compiler_params=_params(("parallel", "parallel")))(gate, up)


def _swiglu_bwd(gate, up, d_act):
    t, f = gate.shape
    tr, tf = _tile(t, 512), _tile(f, 2048)

    def body(g_ref, u_ref, da_ref, dg_ref, du_ref):
        g, u, da = g_ref[...].astype(F32), u_ref[...].astype(F32), da_ref[...].astype(F32)
        sg = _sigmoid(g)
        du_ref[...] = (da * (g * sg)).astype(BF16)
        dg_ref[...] = (da * u * (sg * (1.0 + g * (1.0 - sg)))).astype(BF16)

    spec = pl.BlockSpec((tr, tf), lambda i, j: (i, j))
    shape = jax.ShapeDtypeStruct((t, f), BF16)
    return pl.pallas_call(body, name="swiglu_bwd", grid=(t // tr, f // tf), in_specs=[spec, spec, spec],
                          out_specs=[spec, spec], out_shape=[shape, shape],
                          compiler_params=_params(("parallel", "parallel")))(gate, up, d_act)


def _loss_head(h, ffn, target, g_po, tr):
    t, d = h.shape

    def body(h_ref, f_ref, t_ref, g_ref, dy_ref, df_ref, dg_ref, loss_ref):
        f = f_ref[...]
        err = h_ref[...] + _rms_fwd(f, g_ref[...]) - t_ref[...]
        dy = err * (1.0 / d)
        dy_ref[...] = dy
        df, dg = _rms_bwd(f, g_ref[...], dy)
        df_ref[...] = df.astype(BF16)
        _accumulate(dg_ref, _colsum(dg))
        sq = jnp.sum(_colsum(err * err), axis=-1, keepdims=True) * (0.5 / d)
        _accumulate(loss_ref, jnp.broadcast_to(sq, (1, LANES)))

    return _row_call(body, "loss_head", t, tr, [(h, d, 0), (ffn, d, 0), (target, d, 0)], [g_po],
                     [(d, F32), (d, BF16)], [((1, d), F32), ((1, LANES), F32)])


def _qkv_bwd(proj, d_qn, d_kvn, d_kr, g_q, g_kv, cos_t, sin_t, lay, tr):
    ql, kl = lay["ql"], lay["kl"]

    def body(q_ref, kv_ref, dqn_ref, dkvn_ref, dkr_ref, cos_ref, sin_ref, gq_ref, gkv_ref,
             dq_ref, dkv_ref, dkt_ref, dgq_ref, dgkv_ref):
        dx, dg = _rms_bwd(q_ref[...], gq_ref[...], dqn_ref[...])
        dq_ref[...] = dx.astype(BF16)
        _accumulate(dgq_ref, _colsum(dg))
        dx, dg = _rms_bwd(kv_ref[...], gkv_ref[...], dkvn_ref[...])
        dkv_ref[...] = dx.astype(BF16)
        _accumulate(dgkv_ref, _colsum(dg))
        dkt_ref[...] = _rope_bwd(dkr_ref[...], cos_ref[...], sin_ref[...]).astype(BF16)

    t = proj.shape[0]
    return _row_call(
        body, "qkv_bwd", t, tr,
        [(proj, ql, lay["q_off"] // ql), (proj, kl, lay["kv_off"] // kl), (d_qn, ql, 0), (d_kvn, kl, 0),
         (d_kr, LANES, 0), (cos_t, LANES, 0), (sin_t, LANES, 0)],
        [g_q, g_kv], [(ql, BF16), (kl, BF16), (LANES, BF16)], [((1, ql), F32), ((1, kl), F32)])


def _prenorm_bwd(x, d_xn, dh, g, tr):
    def body(x_ref, dxn_ref, dh_ref, g_ref, gx_ref, dg_ref):
        dx, dg = _rms_bwd(x_ref[...], g_ref[...], dxn_ref[...])
        gx_ref[...] = dh_ref[...] + dx
        _accumulate(dg_ref, _colsum(dg))

    t, d = x.shape
    return _row_call(body, "prenorm_bwd", t, tr, [(x, d, 0), (d_xn, d, 0), (dh, d, 0)], [g],
                     [(d, F32)], [((1, d), F32)])


def _adam_rows(rows, cols):
    cap = max(8, (256 * 1024) // cols // 8 * 8)
    tr = min(rows, cap)
    while rows % tr:
        tr -= 8
    return tr


def _adamw(w, g, m, v, name):
    rows, cols = w.shape
    tr = _adam_rows(rows, cols)

    def body(w_ref, g_ref, m_ref, v_ref, d_ref, mo_ref, vo_ref):
        g = g_ref[...]
        m2 = ADAM_B1 * m_ref[...] + (1.0 - ADAM_B1) * g
        v2 = ADAM_B2 * v_ref[...] + (1.0 - ADAM_B2) * (g * g)
        m_hat = m2 / (1.0 - ADAM_B1 ** ADAM_STEP)
        v_hat = v2 / (1.0 - ADAM_B2 ** ADAM_STEP)
        d_ref[...] = -ADAM_LR * (m_hat / (jnp.sqrt(v_hat) + ADAM_EPS) + ADAM_WD * w_ref[...])
        mo_ref[...] = m2
        vo_ref[...] = v2

    spec = pl.BlockSpec((tr, cols), lambda i: (i, 0))
    shape = jax.ShapeDtypeStruct((rows, cols), F32)
    return pl.pallas_call(body, name=name, grid=(rows // tr,), in_specs=[spec] * 4, out_specs=[spec] * 3,
                          out_shape=[shape] * 3, compiler_params=_params(("parallel",)))(w, g, m, v)


def _adamw_halves(w, g_mine, g_theirs, m, v, name):
    rows, cols = w.shape
    rh = g_mine.shape[0]
    tr = _adam_rows(math.gcd(rows, rh), cols)
    per_half = rh // tr
    my_c = jnp.reshape(lax.axis_index("c"), (1,)).astype(jnp.int32)

    def body(c_ref, w_ref, gm_ref, gt_ref, m_ref, v_ref, g_ref, d_ref, mo_ref, vo_ref):
        mine = (pl.program_id(0) // per_half) == c_ref[0]
        g = jnp.where(mine, gm_ref[...], gt_ref[...])
        m2 = ADAM_B1 * m_ref[...] + (1.0 - ADAM_B1) * g
        v2 = ADAM_B2 * v_ref[...] + (1.0 - ADAM_B2) * (g * g)
        m_hat = m2 / (1.0 - ADAM_B1 ** ADAM_STEP)
        v_hat = v2 / (1.0 - ADAM_B2 ** ADAM_STEP)
        g_ref[...] = g
        d_ref[...] = -ADAM_LR * (m_hat / (jnp.sqrt(v_hat) + ADAM_EPS) + ADAM_WD * w_ref[...])
        mo_ref[...] = m2
        vo_ref[...] = v2

    def half_spec(is_mine):
        def index(i, c_ref):
            used = ((i // per_half) == c_ref[0]) if is_mine else ((i // per_half) != c_ref[0])
            return (jnp.where(used, i % per_half, 0), 0)
        return pl.BlockSpec((tr, cols), index)

    spec = pl.BlockSpec((tr, cols), lambda i, c_ref: (i, 0))
    shape = jax.ShapeDtypeStruct((rows, cols), F32)
    grid_spec = pltpu.PrefetchScalarGridSpec(
        num_scalar_prefetch=1, grid=(rows // tr,),
        in_specs=[spec, half_spec(True), half_spec(False), spec, spec], out_specs=[spec] * 4)
    return pl.pallas_call(body, name=name, grid_spec=grid_spec, out_shape=[shape] * 4,
                          compiler_params=_params(("parallel",)))(my_c, w, g_mine, g_theirs, m, v)


def _pair_add(parts, theirs, name):
    _, n, r, c = parts.shape
    tr = _adam_rows(r, c)
    my_c = jnp.reshape(lax.axis_index("c"), (1,)).astype(jnp.int32)

    def body(c_ref, a_ref, b_ref, o_ref):
        o_ref[0] = (a_ref[0, 0].astype(F32) + b_ref[0].astype(F32)).astype(BF16)

    spec = pl.BlockSpec((1, tr, c), lambda k, i, c_ref: (k, i, 0))
    grid_spec = pltpu.PrefetchScalarGridSpec(
        num_scalar_prefetch=1, grid=(n, r // tr),
        in_specs=[pl.BlockSpec((1, 1, tr, c), lambda k, i, c_ref: (c_ref[0], k, i, 0)), spec], out_specs=spec)
    return pl.pallas_call(body, name=name, grid_spec=grid_spec, out_shape=jax.ShapeDtypeStruct((n, r, c), BF16),
                          compiler_params=_params(("parallel", "parallel")))(my_c, parts, theirs)


def _chip_sum(pair_sums, received, name, token=None):
    _, r, c = pair_sums.shape
    tr = _adam_rows(r, c)
    own = 2 * lax.axis_index("x") + lax.axis_index("y")

    def body(own_ref, p_ref, r0_ref, r1_ref, r2_ref, *rest):
        o_ref = rest[-1]
        acc = p_ref[0].astype(F32) + r0_ref[0].astype(F32)
        acc = acc + r1_ref[0].astype(F32)
        o_ref[...] = acc + r2_ref[0].astype(F32)

    def rspec(j):
        return pl.BlockSpec((1, tr, c), functools.partial(lambda i, own_ref, j: (j, i, 0), j=j))

    extra = [] if token is None else [token]
    grid_spec = pltpu.PrefetchScalarGridSpec(
        num_scalar_prefetch=1, grid=(r // tr,),
        in_specs=[pl.BlockSpec((1, tr, c), lambda i, own_ref: (own_ref[0], i, 0)), rspec(0), rspec(1), rspec(2)]
        + [pl.BlockSpec(memory_space=pl.ANY)] * len(extra),
        out_specs=pl.BlockSpec((tr, c), lambda i, own_ref: (i, 0)))
    return pl.pallas_call(body, name=name, grid_spec=grid_spec, out_shape=jax.ShapeDtypeStruct((r, c), F32),
                          compiler_params=_params(("parallel",)))(
        jnp.reshape(own, (1,)).astype(jnp.int32), pair_sums, received, received, received, *extra)


def _mesh_place():
    x, y, c = lax.axis_index("x"), lax.axis_index("y"), lax.axis_index("c")
    other_chips = [(1 - x, y), (x, 1 - y), (1 - x, 1 - y)]
    return x, y, c, other_chips


def _hbm_specs(n):
    return [pl.BlockSpec(memory_space=pltpu.HBM)] * n


def _sibling_exchange(parts, name):
    n = len(parts)

    def body(*refs):
        ins, outs = refs[:n], refs[n:2 * n]
        send_sems, recv_sems = refs[2 * n:]
        x, y, c, _ = _mesh_place()
        copies = [pltpu.make_async_remote_copy(src_ref=ins[w].at[1 - c], dst_ref=outs[w], send_sem=send_sems.at[w],
                                               recv_sem=recv_sems.at[w], device_id=(x, y, 1 - c), device_id_type=MESH)
                  for w in range(n)]
        for cp in copies:
            cp.start()
        for cp in copies:
            cp.wait()

    return pl.pallas_call(
        body, name=name,
        out_shape=[jax.ShapeDtypeStruct(p.shape[1:], p.dtype) for p in parts],
        in_specs=_hbm_specs(n), out_specs=_hbm_specs(n),
        scratch_shapes=[pltpu.SemaphoreType.DMA((n,)), pltpu.SemaphoreType.DMA((n,))],
    )(*parts)


SEM_SPEC = pl.BlockSpec(memory_space=pltpu.SEMAPHORE)
DATAFLOW_EFFECT = pltpu.SideEffectType.DATAFLOW_SIDE_EFFECTING


def _copies_per_weight(kind):
    return {"gather": 4, "scatter": 3, "sibling": 1, "forward": 3}[kind]


def _flight_copies(kind, src_refs, land_refs, send_sems, recv_sems, arriving):
    x, y, c, other_chips = _mesh_place()
    own = 2 * x + y
    sibling = (x, y, 1 - c)
    per = _copies_per_weight(kind)
    copies = []
    for w in range(len(src_refs)):
        def remote(src, dst, j, to):
            return pltpu.make_async_remote_copy(src_ref=src, dst_ref=dst, send_sem=send_sems.at[per * w + j],
                                                recv_sem=recv_sems.at[per * w + j], device_id=to, device_id_type=MESH)

        if kind == "sibling":
            copies.append(remote(src_refs[w].at[1 - c], land_refs[w], 0, sibling))
            continue
        for j, chip in enumerate(other_chips):
            theirs = 2 * chip[0] + chip[1]
            if kind == "gather":
                copies.append(remote(src_refs[w].at[c], land_refs[w].at[theirs if arriving else own, c], j, (*chip, c)))
            elif kind == "forward":
                copies.append(remote(src_refs[w].at[theirs, c], src_refs[w].at[theirs, (1 - c) if arriving else c],
                                     j, sibling))
            else:
                copies.append(remote(src_refs[w].at[theirs], land_refs[w].at[j], j, (*chip, c)))
        if kind == "gather":
            copies.append(remote(src_refs[w], land_refs[w].at[own], 3, sibling))
    return copies


def _ici_start(kind, srcs, name, after=None):
    n = len(srcs)
    if kind == "gather":
        lands = [lax.empty((4,) + s.shape, s.dtype) for s in srcs]
    elif kind == "scatter":
        lands = [lax.empty((3,) + s.shape[1:], s.dtype) for s in srcs]
    elif kind == "sibling":
        lands = [lax.empty(s.shape[1:], s.dtype) for s in srcs]
    else:
        lands = []
    nb = n + len(lands)
    afters = [] if after is None else [after]

    def body(*refs):
        src_refs, land_refs = refs[:n], refs[n:nb]
        send_sems, recv_sems = refs[nb + len(afters)], refs[nb + len(afters) + 1]
        token = refs[-1]
        for cp in _flight_copies(kind, src_refs, land_refs, send_sems, recv_sems, False):
            cp.start()
        token[...] = jnp.zeros_like(token)

    hbm = lambda a: pltpu.with_memory_space_constraint(a, pltpu.HBM)
    n_sems = _copies_per_weight(kind) * n
    out = pl.pallas_call(
        body, name=name,
        out_shape=(pltpu.SemaphoreType.DMA((n_sems,)), pltpu.SemaphoreType.DMA((n_sems,)),
                   *[pltpu.HBM(a.shape, a.dtype) for a in srcs + lands], jax.ShapeDtypeStruct((8, LANES), F32)),
        in_specs=_hbm_specs(nb) + [pl.BlockSpec(memory_space=pl.ANY)] * len(afters),
        out_specs=(SEM_SPEC, SEM_SPEC, *_hbm_specs(nb), pl.BlockSpec(memory_space=pltpu.VMEM)),
        input_output_aliases={i: 2 + i for i in range(nb)},
        compiler_params=pltpu.CompilerParams(has_side_effects=DATAFLOW_EFFECT),
    )(*[hbm(a) for a in srcs + lands], *afters)
    return out[0], out[1], list(out[2:2 + n]), list(out[2 + n:2 + nb]), out[-1]


def _ici_wait(kind, send_sems, recv_sems, srcs, lands, after, name):
    n = len(srcs)
    nb = n + len(lands)

    def body(*refs):
        src_refs, land_refs = refs[:n], refs[n:nb]
        send_ref, recv_ref = refs[nb], refs[nb + 1]
        for cp in _flight_copies(kind, src_refs, land_refs, send_ref, recv_ref, True):
            cp.wait_send()
            cp.wait_recv()

    out = pl.pallas_call(
        body, name=name, out_shape=tuple(pltpu.HBM(a.shape, a.dtype) for a in srcs + lands),
        in_specs=_hbm_specs(nb) + [SEM_SPEC, SEM_SPEC, pl.BlockSpec(memory_space=pl.ANY)],
        out_specs=tuple(_hbm_specs(nb)), input_output_aliases={i: i for i in range(nb)},
        compiler_params=pltpu.CompilerParams(has_side_effects=DATAFLOW_EFFECT),
    )(*srcs, *lands, send_sems, recv_sems, after)
    return list(out[:n]), list(out[n:])


def _halves_exchange(halves, name):
    n = len(halves)

    def body(*refs):
        ins, outs = refs[:n], refs[n:2 * n]
        send_sems, recv_sems = refs[2 * n:]
        x, y, c, _ = _mesh_place()
        copies = [pltpu.make_async_remote_copy(src_ref=ins[w], dst_ref=outs[w], send_sem=send_sems.at[w],
                                               recv_sem=recv_sems.at[w], device_id=(x, y, 1 - c), device_id_type=MESH)
                  for w in range(n)]
        for cp in copies:
            cp.start()
        for cp in copies:
            cp.wait()

    return pl.pallas_call(
        body, name=name,
        out_shape=[jax.ShapeDtypeStruct(h.shape, h.dtype) for h in halves],
        in_specs=_hbm_specs(n), out_specs=_hbm_specs(n),
        scratch_shapes=[pltpu.SemaphoreType.DMA((n,)), pltpu.SemaphoreType.DMA((n,))],
    )(*halves)


def _small_all_reduce(packed):
    rows = packed.shape[0]

    def body(in_ref, out_ref, gathered, send_sems, recv_sems):
        x, y, c, _ = _mesh_place()
        me = 4 * x + 2 * y + c
        gathered[0] = in_ref[...]
        copies = []
        for rel in range(1, 8):
            to = (x ^ (rel >> 2), y ^ ((rel >> 1) & 1), c ^ (rel & 1))
            cp = pltpu.make_async_remote_copy(src_ref=in_ref, dst_ref=gathered.at[rel], send_sem=send_sems.at[rel - 1],
                                              recv_sem=recv_sems.at[rel - 1], device_id=to, device_id_type=MESH)
            cp.start()
            copies.append(cp)
        for cp in copies:
            cp.wait()
        acc = gathered[me]
        for dev in range(1, 8):
            acc = acc + gathered[dev ^ me]
        out_ref[...] = acc

    return pl.pallas_call(
        body, name="small_all_reduce", out_shape=jax.ShapeDtypeStruct(packed.shape, F32),
        in_specs=[pl.BlockSpec(memory_space=pltpu.VMEM)], out_specs=pl.BlockSpec(memory_space=pltpu.VMEM),
        scratch_shapes=[pltpu.VMEM((8, rows, LANES), F32), pltpu.SemaphoreType.DMA((7,)), pltpu.SemaphoreType.DMA((7,))],
        compiler_params=pltpu.CompilerParams(vmem_limit_bytes=VMEM_LIMIT_BYTES),
    )(packed)


def _layout(w_in, w_uq, w_ukv, v_ln_gain, q_norm, kv_norm):
    heads = 4 * w_uq.shape[-1] // (NOPE_DIM + ROPE_DIM)
    gw = v_ln_gain.shape[-1]
    ql, kl = q_norm.shape[-1], kv_norm.shape[-1]
    lay = dict(heads=heads, gw=gw, ql=ql, kl=kl, aw=heads * V_DIM, u_off=0, v_off=gw, q_off=2 * gw,
               kv_off=2 * gw + ql, kr_off=2 * gw + ql + kl)
    lay["in_pad"] = _round_up(lay["kr_off"] + LANES, 2 * LANES if lay["kr_off"] + LANES <= 2048 else 1024)
    assert lay["q_off"] % ql == 0 and lay["kv_off"] % kl == 0 and lay["aw"] % gw == 0
    assert 4 * w_in.shape[-1] == ql + kl + ROPE_DIM + 2 * gw
    return lay


def _rope_tile(t1, t2, axis=-1):
    z = jnp.zeros_like(t1)
    return jnp.concatenate([t1, z, t2, z], axis=axis)


def _w_in_rows(gathered, shard_rows, lay):
    d = gathered.shape[-1]
    wt = gathered[:, :shard_rows].reshape(4 * shard_rows, d)
    ql, kl, gw = lay["ql"], lay["kl"], lay["gw"]
    q_c, kv_c = wt[:ql], wt[ql:ql + kl]
    r = wt[ql + kl:ql + kl + ROPE_DIM]
    u = wt[ql + kl + ROPE_DIM:ql + kl + ROPE_DIM + gw]
    v = wt[ql + kl + ROPE_DIM + gw:]
    parts = [u, v, q_c, kv_c, _rope_tile(r[:ROPE_HALF], r[ROPE_HALF:], axis=0)]
    pad = lay["in_pad"] - (lay["kr_off"] + LANES)
    if pad:
        parts.append(jnp.zeros((pad, d), wt.dtype))
    return jnp.concatenate(parts, axis=0)


def _w_in_grad_chunks(dwt, shard_rows, padded_rows, lay):
    d = dwt.shape[-1]
    ql, kl, gw = lay["ql"], lay["kl"], lay["gw"]
    ko = lay["kr_off"]
    rows = jnp.concatenate([dwt[lay["q_off"]:lay["q_off"] + ql], dwt[lay["kv_off"]:lay["kv_off"] + kl],
                            dwt[ko:ko + ROPE_HALF], dwt[ko + 2 * ROPE_HALF:ko + 3 * ROPE_HALF],
                            dwt[:gw], dwt[gw:2 * gw]], axis=0).reshape(4, shard_rows, d)
    rows = jnp.pad(rows, ((0, 0), (0, padded_rows - shard_rows), (0, 0)))
    return jnp.transpose(rows.reshape(4, 2, padded_rows // 2, d), (1, 0, 2, 3)).astype(BF16)


def _w_uq_padded(w, heads):
    w3 = w.reshape(w.shape[0], heads, NOPE_DIM + ROPE_DIM)
    t = _rope_tile(w3[..., NOPE_DIM:NOPE_DIM + ROPE_HALF], w3[..., NOPE_DIM + ROPE_HALF:])
    return jnp.concatenate([w3[..., :NOPE_DIM], t], axis=-1).reshape(w.shape[0], heads * HEAD_PAD)


def _w_uq_grad_unpadded(dw, heads):
    d3 = dw.reshape(dw.shape[0], heads, HEAD_PAD)
    return jnp.concatenate([d3[..., :NOPE_DIM], d3[..., NOPE_DIM:NOPE_DIM + ROPE_HALF],
                            d3[..., NOPE_DIM + 2 * ROPE_HALF:NOPE_DIM + 3 * ROPE_HALF]],
                           axis=-1).reshape(dw.shape[0], heads * (NOPE_DIM + ROPE_DIM))


def _cols_gathered(g):
    return jnp.transpose(g, (1, 0, 2)).reshape(g.shape[1], 4 * g.shape[2])


def _chunks_of_cols(grad):
    r, c4 = grad.shape
    return jnp.transpose(grad.reshape(2, r // 2, 4, c4 // 4), (0, 2, 1, 3)).astype(BF16)


SMALL = ["pre_mix_norm", "q_norm", "kv_norm", "v_ln_gain", "v_ln_bias", "w_spatial", "b_spatial", "attn_out_norm",
         "gmlp_out_norm", "post_mix_norm", "pre_ffn_norm", "post_ffn_norm"]
BIG = ["w_in", "w_uq", "w_ukv", "w_out", "w_gate", "w_up", "w_down"]
GATHER_NOW = ["w_in", "w_uq", "w_ukv"]
GATHER_LATER_1 = ["w_out", "w_gate"]
GATHER_LATER_2 = ["w_up", "w_down"]
REDUCE_FFN = ["w_gate", "w_up", "w_down"]
REDUCE_OUT = ["w_out"]
REDUCE_LAST = ["w_in", "w_uq", "w_ukv"]
TRANSPOSED = ("w_in", "w_gate", "w_up")
ORDER = ["pre_mix_norm", "w_in", "q_norm", "kv_norm", "w_uq", "w_ukv", "v_ln_gain", "v_ln_bias", "w_spatial",
         "b_spatial", "attn_out_norm", "gmlp_out_norm", "w_out", "post_mix_norm", "pre_ffn_norm", "w_gate", "w_up",
         "w_down", "post_ffn_norm"]


def _pack(arrays):
    flat = jnp.concatenate([a.reshape(-1) for a in arrays])
    n = flat.shape[0]
    total = _round_up(n, 8 * LANES)
    if total > n:
        flat = jnp.concatenate([flat, jnp.zeros((total - n,), F32)])
    return flat.reshape(total // LANES, LANES)


def _unpack(packed, like):
    flat = packed.reshape(-1)
    out, off = [], 0
    for a in like:
        out.append(flat[off:off + a.size].reshape(a.shape))
        off += a.size
    return out


def kernel(x, positions, pre_mix_norm, w_in, q_norm, kv_norm, w_uq, w_ukv, v_ln_gain, v_ln_bias, w_spatial, b_spatial, attn_out_norm, gmlp_out_norm, w_out, post_mix_norm, pre_ffn_norm, w_gate, w_up, w_down, post_ffn_norm, loss_target, m_pre_mix_norm, m_w_in, m_q_norm, m_kv_norm, m_w_uq, m_w_ukv, m_v_ln_gain, m_v_ln_bias, m_w_spatial, m_b_spatial, m_attn_out_norm, m_gmlp_out_norm, m_w_out, m_post_mix_norm, m_pre_ffn_norm, m_w_gate, m_w_up, m_w_down, m_post_ffn_norm, v_pre_mix_norm, v_w_in, v_q_norm, v_kv_norm, v_w_uq, v_w_ukv, v_v_ln_gain, v_v_ln_bias, v_w_spatial, v_b_spatial, v_attn_out_norm, v_gmlp_out_norm, v_w_out, v_post_mix_norm, v_pre_ffn_norm, v_w_gate, v_w_up, v_w_down, v_post_ffn_norm):
    args = dict(locals())
    weights = {n: args[n] for n in ORDER}
    m_in = {n: args["m_" + n] for n in ORDER}
    v_in = {n: args["v_" + n] for n in ORDER}

    lay = _layout(w_in, w_uq, w_ukv, v_ln_gain, q_norm, kv_norm)
    heads, gw = lay["heads"], lay["gw"]
    t, d = x.shape[1], x.shape[2]
    tr = 128 if t % 128 == 0 else t
    xs = x.reshape(t, d)
    target = loss_target.reshape(t, d)

    ffs, ins = w_gate.shape[-1], w_in.shape[-1]
    ffp, inp = _round_up(ffs, LANES), _round_up(ins, LANES)
    shards = {n: (jnp.swapaxes(weights[n][0], 0, 1) if n in TRANSPOSED else weights[n][0]).astype(BF16)
              for n in BIG}
    for n, rows in (("w_gate", ffp), ("w_up", ffp), ("w_down", ffp), ("w_in", inp)):
        shards[n] = jnp.pad(shards[n], ((0, rows - shards[n].shape[0]), (0, 0)))
    halved = {n: shards[n].reshape(2, shards[n].shape[0] // 2, shards[n].shape[1]) for n in BIG}
    full = {}

    def pair_sums_of(partial, names, tag):
        from_sibling = _sibling_exchange([partial[n] for n in names], "grads_sibling_exchange_" + tag)
        return [_pair_add(partial[n], r, "pair_add_" + n) for n, r in zip(names, from_sibling)]

    def place(names, lands):
        for n, g in zip(names, lands):
            full[n] = g.reshape((4,) + shards[n].shape)

    flight_0 = _ici_start("gather", [halved[n] for n in GATHER_NOW], "gather_start_0")
    flight_1 = _ici_start("gather", [halved[n] for n in GATHER_LATER_1], "gather_start_1", after=flight_0[4])
    flight_2 = _ici_start("gather", [halved[n] for n in GATHER_LATER_2], "gather_start_2", after=flight_1[4])
    _, lands = _ici_wait("gather", *flight_0[:4], flight_2[4], "gather_wait_0")
    pass_now = _ici_start("forward", lands, "forward_start_now")
    place(GATHER_NOW, _ici_wait("forward", *pass_now[:4], pass_now[4], "forward_wait_now")[0])
    wt_in = _w_in_rows(full["w_in"], ins, lay)
    wb_uq = _w_uq_padded(_cols_gathered(full["w_uq"]), heads)
    wb_ukv = _cols_gathered(full["w_ukv"])

    inv_freq = 1.0 / (ROPE_THETA ** (jnp.arange(0, ROPE_DIM, 2, dtype=F32) / ROPE_DIM))
    ang = positions.reshape(t).astype(F32)[:, None] * inv_freq
    cos, sin = jnp.cos(ang), jnp.sin(ang)
    cos_t = _rope_tile(cos, cos)
    sin_t = _rope_tile(-sin, sin)

    row = lambda a: a.reshape(1, -1)
    g_pre, g_q, g_kv = row(pre_mix_norm), row(q_norm), row(kv_norm)
    g_a, g_g, g_pm = row(attn_out_norm), row(gmlp_out_norm), row(post_mix_norm)
    g_pf, g_po = row(pre_ffn_norm), row(post_ffn_norm)
    ln_g, ln_b = row(v_ln_gain), row(v_ln_bias)
    ws = w_spatial[0].astype(BF16)
    ws_t = jnp.transpose(ws, (0, 2, 1))
    bs_wide = jnp.broadcast_to(b_spatial[0][:, :, None], b_spatial.shape[1:] + (G_HEAD_DIM,))

    xn = _prenorm(xs, g_pre, tr, flight_1[4] + flight_2[4])
    proj = _matmul(xn, wt_in, NT, F32, "proj")
    qn, kvn, kr = _qkv_prep(proj, g_q, g_kv, cos_t, sin_t, lay, tr)
    q = _q_rope(_matmul(qn, wb_uq, NN, F32, "q_up"), cos_t, sin_t, heads, tr)
    kv = _matmul(kvn, wb_ukv, NN, BF16, "kv_up")
    a_out, a_lse = _attn_fwd(q, kv, kr, heads)
    gn = _gmlp_fwd(proj, ln_g, ln_b, ws, bs_wide, g_g, lay)
    mixed = _mix_norm(a_out, gn, g_a, tr)
    _, lands = _ici_wait("gather", *flight_1[:4], mixed, "gather_wait_1")
    pass_out = _ici_start("forward", lands[:1], "forward_start_out")
    pass_gate = _ici_start("forward", lands[1:], "forward_start_gate")
    place(["w_out"], _ici_wait("forward", *pass_out[:4], mixed, "forward_wait_out")[0])
    wb_out = full["w_out"].reshape(-1, d)
    mix_out = _matmul(mixed, wb_out, NN, F32, "mix_out", token=pass_gate[4])
    h, hn = _post_mix(xs, mix_out, g_pm, g_pf, tr)
    place(["w_gate"], _ici_wait("forward", *pass_gate[:4], hn, "forward_wait_gate")[0])
    wt_gate = full["w_gate"].reshape(4 * ffp, d)
    gate = _matmul(hn, wt_gate, NT, BF16, "ffn_gate")
    _, lands = _ici_wait("gather", *flight_2[:4], gate, "gather_wait_2")
    pass_up = _ici_start("forward", lands[:1], "forward_start_up")
    pass_down = _ici_start("forward", lands[1:], "forward_start_down")
    place(["w_up"], _ici_wait("forward", *pass_up[:4], gate, "forward_wait_up")[0])
    wt_up = full["w_up"].reshape(4 * ffp, d)
    up = _matmul(hn, wt_up, NT, BF16, "ffn_up", token=pass_down[4])
    act = _swiglu(gate, up)
    place(["w_down"], _ici_wait("forward", *pass_down[:4], act, "forward_wait_down")[0])
    wb_down = full["w_down"].reshape(4 * ffp, d)
    ffn = _matmul(act, wb_down, NN, F32, "ffn_down")
    dy, d_ffn, dg_po, loss_vec = _loss_head(h, ffn, target, g_po, tr)

    d_act = _matmul(d_ffn, wb_down, NT, BF16, "d_act")
    d_gate, d_up = _swiglu_bwd(gate, up, d_act)
    partial_ffn = [_matmul(d_gate, hn, TN, BF16, "gw_gate", out_chunks=True),
                   _matmul(d_up, hn, TN, BF16, "gw_up", out_chunks=True),
                   _matmul(act, d_ffn, TN, BF16, "gw_down", out_chunks=True)]
    swap_ffn = _ici_start("sibling", partial_ffn, "sibling_start_ffn")
    d_hn = _matmul(d_up, wt_up, NN, F32, "d_hn",
                   extras=[_matmul(d_gate, wt_gate, NN, F32, "d_hn_gate", token=swap_ffn[4])],
                   epilogue=lambda acc, partial: acc + partial)
    partial_ffn, from_sibling = _ici_wait("sibling", *swap_ffn[:4], d_hn, "sibling_wait_ffn")
    pair_ffn = [_pair_add(p, r, "pair_add_" + n) for n, p, r in zip(REDUCE_FFN, partial_ffn, from_sibling)]
    flight_ffn = _ici_start("scatter", pair_ffn, "scatter_start_ffn")
    dh, d_mo, dg_pf, dg_pm = _post_mix_bwd(mix_out, h, dy, d_hn, g_pm, g_pf, tr, flight_ffn[4])
    d_mixed = _matmul(d_mo, wb_out, NT, F32, "d_mixed")
    gw_out = _matmul(mixed, d_mo, TN, BF16, "gw_out", out_chunks=True)
    pair_out = pair_sums_of({"w_out": gw_out}, REDUCE_OUT, "out")
    flight_out = _ici_start("scatter", pair_out, "scatter_start_out")
    d_a, dg_a = _mix_norm_bwd(a_out, d_mixed, g_a, tr, flight_out[4])
    d_u, d_v, dg_g, d_ln_g, d_ln_b, d_ws, d_bs_wide = _gmlp_bwd(proj, d_mixed, ln_g, ln_b, ws, ws_t, bs_wide, g_g, lay)
    d_bs = _spatial_bias_grad(d_bs_wide)
    d_q, d_kv, d_kr = _attn_bwd(q, kv, kr, a_out, a_lse, d_a, cos_t, sin_t, heads)
    d_qn = _matmul(d_q, wb_uq, NT, F32, "d_qn")
    gw_uq = _matmul(qn, d_q, TN, F32, "gw_uq")
    d_kvn = _matmul(d_kv, wb_ukv, NT, F32, "d_kvn")
    gw_ukv = _matmul(kvn, d_kv, TN, F32, "gw_ukv")
    d_qc, d_kvc, d_krt, dg_q, dg_kv = _qkv_bwd(proj, d_qn, d_kvn, d_kr, g_q, g_kv, cos_t, sin_t, lay, tr)
    parts = [d_u, d_v, d_qc, d_kvc, d_krt]
    pad = lay["in_pad"] - (lay["kr_off"] + LANES)
    if pad:
        parts.append(jnp.zeros((t, pad), BF16))
    d_proj = jnp.concatenate(parts, axis=1)
    d_xn = _matmul(d_proj, wt_in, NN, F32, "d_xn")
    gw_in = _matmul(d_proj, xn, TN, F32, "gw_in")
    grad_x, dg_pre = _prenorm_bwd(xs, d_xn, dh, g_pre, tr)

    pair_mix = pair_sums_of({"w_in": _w_in_grad_chunks(gw_in, ins, inp, lay),
                             "w_uq": _chunks_of_cols(_w_uq_grad_unpadded(gw_uq, heads)),
                             "w_ukv": _chunks_of_cols(gw_ukv)}, REDUCE_LAST, "mix")
    pair_ffn, received_ffn = _ici_wait("scatter", *flight_ffn[:4], pair_mix[-1], "scatter_wait_ffn")
    pair_out, received_out = _ici_wait("scatter", *flight_out[:4], pair_mix[-1], "scatter_wait_out")
    flight_mix = _ici_start("scatter", pair_mix, "scatter_start_mix")
    grads, delta, new_m, new_v = {}, {}, {}, {}

    def finish(names, pair_sums, received, tag, token):
        mine = [_chip_sum(p, r, "chip_sum_" + n, token) for n, p, r in zip(names, pair_sums, received)]
        theirs = _halves_exchange(mine, "grads_halves_exchange_" + tag)
        for n, g_mine, g_theirs in zip(names, mine, theirs):
            shape = weights[n].shape
            if n in TRANSPOSED:
                view = lambda a: jnp.swapaxes(a[0], 0, 1)
                back = lambda o: jnp.swapaxes(o, 0, 1).reshape(shape)
            else:
                view = lambda a: a[0]
                back = lambda o: o.reshape(shape)
            out = _adamw_halves(view(weights[n]), g_mine, g_theirs, view(m_in[n]), view(v_in[n]), "adamw_" + n)
            grads[n], delta[n], new_m[n], new_v[n] = [back(o) for o in out]

    finish(REDUCE_FFN + REDUCE_OUT, pair_ffn + pair_out, received_ffn + received_out, "ffn", flight_mix[4])
    pair_mix, received = _ici_wait("scatter", *flight_mix[:4], new_v[REDUCE_OUT[-1]], "scatter_wait_mix")
    finish(REDUCE_LAST, pair_mix, received, "mix", None)

    small_grads = {"pre_mix_norm": dg_pre, "q_norm": dg_q, "kv_norm": dg_kv, "v_ln_gain": d_ln_g, "v_ln_bias": d_ln_b,
                   "w_spatial": d_ws, "b_spatial": d_bs, "attn_out_norm": dg_a, "gmlp_out_norm": dg_g,
                   "post_mix_norm": dg_pm, "pre_ffn_norm": dg_pf, "post_ffn_norm": dg_po}
    like = [weights[n] for n in SMALL]
    reduced = _small_all_reduce(_pack([small_grads[n] for n in SMALL] + [loss_vec]))
    loss = reduced.reshape(-1)[sum(a.size for a in like)]
    small_g = _pack(_unpack(reduced, like))
    s_delta, s_m, s_v = _adamw(_pack(like), small_g, _pack([m_in[n] for n in SMALL]),
                               _pack([v_in[n] for n in SMALL]), "adamw_small")
    for n, g in zip(SMALL, _unpack(small_g, like)):
        grads[n] = g
    delta.update(zip(SMALL, _unpack(s_delta, like)))
    new_m.update(zip(SMALL, _unpack(s_m, like)))
    new_v.update(zip(SMALL, _unpack(s_v, like)))

    return (loss, grad_x.reshape(x.shape), *[grads[n] for n in ORDER], *[delta[n] for n in ORDER],
            *[new_m[n] for n in ORDER], *[new_v[n] for n in ORDER])
```

```python
import functools
import math

import jax
import jax.numpy as jnp
from jax import lax
from jax.experimental import pallas as pl
from jax.experimental.pallas import tpu as pltpu

F32 = jnp.float32
BF16 = jnp.bfloat16
MESH = pl.DeviceIdType.MESH

NOPE_DIM = 128
ROPE_DIM = 64
ROPE_HALF = ROPE_DIM // 2
V_DIM = 128
HEAD_PAD = 256
G_HEAD_DIM = 128
CHUNK = 128
ROPE_THETA = 10000.0
EPS = 1e-6
ADAM_LR = 0.001
ADAM_B1 = 0.9
ADAM_B2 = 0.999
ADAM_EPS = 1e-08
ADAM_WD = 0.01
ADAM_STEP = 10

LANES = 128
MATMUL_TILE = 1024
WIDE_TILE = 1408
VMEM_LIMIT_BYTES = 48 * 1024 * 1024

NN = (((1,), (0,)), ((), ()))
NT = (((1,), (1,)), ((), ()))
TN = (((0,), (0,)), ((), ()))


def _params(semantics):
    return pltpu.CompilerParams(dimension_semantics=semantics, vmem_limit_bytes=VMEM_LIMIT_BYTES)


def _tile(n, cap=MATMUL_TILE):
    if n <= cap:
        return n
    if cap == MATMUL_TILE and n % WIDE_TILE == 0:
        return WIDE_TILE
    t = cap - cap % LANES
    while n % t:
        t -= LANES
    assert t > 0, n
    return t


def _round_up(n, m):
    return (n + m - 1) // m * m


def _matmul(a, b, dims, out_dtype, name, extras=(), epilogue=None, out_chunks=None, token=None):
    if dims is NN:
        (m, k), (k2, n) = a.shape, b.shape
    elif dims is NT:
        (m, k), (n, k2) = a.shape, b.shape
    else:
        (k, m), (k2, n) = a.shape, b.shape
    assert k == k2, (a.shape, b.shape, name)
    tm, tn, tk = _tile(m // 8 if out_chunks else m), _tile(n), _tile(k, 2 * MATMUL_TILE)
    if len(extras) + (len(out_dtype) if isinstance(out_dtype, tuple) else 1) > 2:
        tm = _tile(m, MATMUL_TILE // 2)
    nk = k // tk

    out_dtypes = out_dtype if isinstance(out_dtype, tuple) else (out_dtype,)
    n_extra = len(extras)

    def body(*refs):
        a_ref, b_ref = refs[:2]
        extra_refs = refs[2:2 + n_extra]
        out_refs = refs[2 + n_extra + (token is not None):-1]
        acc_ref = refs[-1]
        kk = pl.program_id(2)

        @pl.when(kk == 0)
        def _():
            acc_ref[...] = jnp.zeros_like(acc_ref)

        acc_ref[...] += lax.dot_general(a_ref[...], b_ref[...], dims, preferred_element_type=F32)

        @pl.when(kk == nk - 1)
        def _():
            r = acc_ref[...]
            if epilogue is not None:
                r = epilogue(r, *[e[...] for e in extra_refs])
            for o_ref, val in zip(out_refs, r if isinstance(r, tuple) else (r,)):
                o_ref[...] = val.astype(o_ref.dtype)

    if dims is TN:
        a_spec = pl.BlockSpec((tk, tm), lambda i, j, kk: (kk, i))
    else:
        a_spec = pl.BlockSpec((tm, tk), lambda i, j, kk: (i, kk))
    if dims is NT:
        b_spec = pl.BlockSpec((tn, tk), lambda i, j, kk: (j, kk))
    else:
        b_spec = pl.BlockSpec((tk, tn), lambda i, j, kk: (kk, j))
    if not out_chunks:
        o_spec = pl.BlockSpec((tm, tn), lambda i, j, kk: (i, j))
        o_shape = (m, n)
    else:
        pi = m // 8 // tm
        o_spec = pl.BlockSpec((None, None, tm, tn), lambda i, j, kk: ((i // pi) % 2, i // (2 * pi), i % pi, j))
        o_shape = (2, 4, m // 8, n)
    assert not (extras and out_chunks)
    tokens = [] if token is None else [token]
    out = pl.pallas_call(
        body, name=name, grid=(m // tm, n // tn, nk),
        in_specs=[a_spec, b_spec] + [o_spec] * n_extra + [pl.BlockSpec(memory_space=pl.ANY)] * len(tokens),
        out_specs=[o_spec] * len(out_dtypes),
        out_shape=[jax.ShapeDtypeStruct(o_shape, dt) for dt in out_dtypes],
        scratch_shapes=[pltpu.VMEM((tm, tn), F32)],
        compiler_params=_params(("parallel", "parallel", "arbitrary")),
    )(a, b, *extras, *tokens)
    return tuple(out) if isinstance(out_dtype, tuple) else out[0]


def _row_call(body, name, rows, tr, row_ins, par_ins, row_outs, acc_outs):
    def col(i, cb):
        return (i, cb)

    def whole(i, nd):
        return (0,) * nd

    in_specs = [pl.BlockSpec((tr, w), functools.partial(col, cb=cb)) for (_, w, cb) in row_ins]
    in_specs += [pl.BlockSpec(a.shape, functools.partial(whole, nd=a.ndim)) for a in par_ins]
    out_specs = [pl.BlockSpec((tr, w), lambda i: (i, 0)) for (w, _) in row_outs]
    out_specs += [pl.BlockSpec(s, functools.partial(whole, nd=len(s))) for (s, _) in acc_outs]
    out_shape = [jax.ShapeDtypeStruct((rows, w), dt) for (w, dt) in row_outs]
    out_shape += [jax.ShapeDtypeStruct(s, dt) for (s, dt) in acc_outs]
    return pl.pallas_call(
        body, name=name, grid=(rows // tr,), in_specs=in_specs, out_specs=out_specs, out_shape=out_shape,
        compiler_params=_params(("arbitrary",) if acc_outs else ("parallel",)),
    )(*[a for (a, _, _) in row_ins], *par_ins)


def _accumulate(ref, val):
    i = pl.program_id(0)

    @pl.when(i == 0)
    def _():
        ref[...] = val

    @pl.when(i > 0)
    def _():
        ref[...] += val


def _colsum(v):
    return jnp.sum(v, axis=0, keepdims=True)


def _rms_fwd(x, g):
    r = lax.rsqrt(jnp.mean(x * x, axis=-1, keepdims=True) + EPS)
    return x * r * g


def _rms_bwd(x, g, dy):
    r = lax.rsqrt(jnp.mean(x * x, axis=-1, keepdims=True) + EPS)
    xh = x * r
    dxh = dy * g
    dx = r * (dxh - xh * jnp.mean(dxh * xh, axis=-1, keepdims=True))
    return dx, dy * xh


_GELU_C = math.sqrt(2.0 / math.pi)
_GELU_A = 0.044715


def _gelu(x):
    return 0.5 * x * (1.0 + jnp.tanh(_GELU_C * (x + _GELU_A * (x * x * x))))


def _gelu_grad(x):
    t = jnp.tanh(_GELU_C * (x + _GELU_A * (x * x * x)))
    return 0.5 * (1.0 + t) + 0.5 * x * (1.0 - t * t) * (_GELU_C * (1.0 + 3.0 * _GELU_A * (x * x)))


def _sigmoid(x):
    return 1.0 / (1.0 + jnp.exp(-x))


def _rope_fwd(t, cos_t, sin_t):
    return t * cos_t + pltpu.roll(t, 2 * ROPE_HALF, 1) * sin_t


def _rope_bwd(dt, cos_t, sin_t):
    return dt * cos_t - pltpu.roll(dt, 2 * ROPE_HALF, 1) * sin_t


def _prenorm(x, g, tr, token):
    def body(x_ref, g_ref, token_ref, o_ref):
        o_ref[...] = _rms_fwd(x_ref[...], g_ref[...]).astype(BF16)

    t, d = x.shape
    return _row_call(body, "prenorm", t, tr, [(x, d, 0)], [g, token], [(d, BF16)], [])[0]


def _qkv_prep(proj, g_q, g_kv, cos_t, sin_t, lay, tr):
    ql, kl = lay["ql"], lay["kl"]

    def body(q_ref, kv_ref, kr_ref, cos_ref, sin_ref, gq_ref, gkv_ref, qn_ref, kvn_ref, kro_ref):
        qn_ref[...] = _rms_fwd(q_ref[...], gq_ref[...]).astype(BF16)
        kvn_ref[...] = _rms_fwd(kv_ref[...], gkv_ref[...]).astype(BF16)
        kro_ref[...] = _rope_fwd(kr_ref[...], cos_ref[...], sin_ref[...]).astype(BF16)

    t = proj.shape[0]
    return _row_call(
        body, "qkv_prep", t, tr,
        [(proj, ql, lay["q_off"] // ql), (proj, kl, lay["kv_off"] // kl), (proj, LANES, lay["kr_off"] // LANES),
         (cos_t, LANES, 0), (sin_t, LANES, 0)],
        [g_q, g_kv], [(ql, BF16), (kl, BF16), (LANES, BF16)], [])


def _q_rope(q, cos_t, sin_t, heads, tr):
    def body(q_ref, cos_ref, sin_ref, o_ref):
        c, s = cos_ref[...], sin_ref[...]
        for h in range(heads):
            lo = h * HEAD_PAD
            o_ref[:, lo:lo + NOPE_DIM] = q_ref[:, lo:lo + NOPE_DIM].astype(BF16)
            o_ref[:, lo + NOPE_DIM:lo + HEAD_PAD] = _rope_fwd(q_ref[:, lo + NOPE_DIM:lo + HEAD_PAD], c, s).astype(BF16)

    t, w = q.shape
    return _row_call(body, "q_rope", t, tr, [(q, w, 0), (cos_t, LANES, 0), (sin_t, LANES, 0)], [], [(w, BF16)], [])[0]


ATTN_SCALE = 1.0 / math.sqrt(NOPE_DIM + ROPE_DIM)
ATTN_EXP2_SCALE = ATTN_SCALE * math.log2(math.e)


def _attn_tile(t, cap):
    tq = cap
    while t % tq:
        tq //= 2
    return tq


def _attn_fwd(q, kv, kr, heads):
    t = q.shape[0]
    tq = _attn_tile(t, 256)

    def body(q_ref, kv_ref, kr_ref, o_ref, lse_ref, kcat):
        @pl.when(pl.program_id(1) == 0)
        def _():
            kcat[:, :NOPE_DIM] = kv_ref[:, :NOPE_DIM]
            kcat[:, NOPE_DIM:] = kr_ref[...]

        s = lax.dot_general(q_ref[...], kcat[...], NT, preferred_element_type=F32)
        m = jnp.max(s, axis=-1, keepdims=True)
        p = jnp.exp2((s - m) * ATTN_EXP2_SCALE)
        l = jnp.sum(p, axis=-1, keepdims=True)
        o_ref[...] = jnp.dot(p.astype(BF16), kv_ref[:, NOPE_DIM:], preferred_element_type=F32) * (1.0 / l)
        lse_ref[...] = jnp.broadcast_to(m * ATTN_EXP2_SCALE + jnp.log(l) * math.log2(math.e), (tq, V_DIM))

    out_spec = pl.BlockSpec((tq, V_DIM), lambda h, i: (i, h))
    out_shape = jax.ShapeDtypeStruct((t, heads * V_DIM), F32)
    return pl.pallas_call(
        body, name="attn_fwd", grid=(heads, t // tq),
        in_specs=[pl.BlockSpec((tq, HEAD_PAD), lambda h, i: (i, h)),
                  pl.BlockSpec((t, HEAD_PAD), lambda h, i: (0, h)),
                  pl.BlockSpec((t, LANES), lambda h, i: (0, 0))],
        out_specs=[out_spec, out_spec], out_shape=[out_shape, out_shape],
        scratch_shapes=[pltpu.VMEM((t, HEAD_PAD), BF16)],
        compiler_params=_params(("arbitrary", "arbitrary")),
    )(q, kv, kr)


def _attn_bwd(q, kv, kr, out, lse, d_out, cos_t, sin_t, heads):
    t = q.shape[0]
    tq = _attn_tile(t, 256)
    nq = t // tq

    def body(q_ref, kv_ref, kr_ref, o_ref, lse_ref, do_ref, cos_ref, sin_ref, dq_ref, dkv_ref, dkr_ref,
             kcat, dk_acc, dv_acc):
        h, i = pl.program_id(0), pl.program_id(1)

        @pl.when(i == 0)
        def _():
            kcat[:, :NOPE_DIM] = kv_ref[:, :NOPE_DIM]
            kcat[:, NOPE_DIM:] = kr_ref[...]
            dk_acc[...] = jnp.zeros_like(dk_acc)
            dv_acc[...] = jnp.zeros_like(dv_acc)

        @pl.when((h == 0) & (i == 0))
        def _():
            dkr_ref[...] = jnp.zeros_like(dkr_ref)

        qb, dob = q_ref[...], do_ref[...]
        row_term = jnp.sum(dob.astype(F32) * o_ref[...], axis=-1, keepdims=True)
        s = lax.dot_general(qb, kcat[...], NT, preferred_element_type=F32)
        dp = lax.dot_general(dob, kv_ref[:, NOPE_DIM:], NT, preferred_element_type=F32)
        p = jnp.exp2(s * ATTN_EXP2_SCALE - lse_ref[:, :1])
        ds = (p * (dp - row_term)).astype(BF16)
        dv_acc[...] += lax.dot_general(p.astype(BF16), dob, TN, preferred_element_type=F32)
        dq = jnp.dot(ds, kcat[...], preferred_element_type=F32) * ATTN_SCALE
        dq_ref[:, :NOPE_DIM] = dq[:, :NOPE_DIM].astype(BF16)
        dq_ref[:, NOPE_DIM:] = _rope_bwd(dq[:, NOPE_DIM:], cos_ref[...], sin_ref[...]).astype(BF16)
        dk_acc[...] += lax.dot_general(ds, qb, TN, preferred_element_type=F32)

        @pl.when(i == nq - 1)
        def _():
            dkv_ref[:, :NOPE_DIM] = (dk_acc[:, :NOPE_DIM] * ATTN_SCALE).astype(BF16)
            dkv_ref[:, NOPE_DIM:] = dv_acc[...].astype(BF16)
            dkr_ref[...] += dk_acc[:, NOPE_DIM:] * ATTN_SCALE

    return pl.pallas_call(
        body, name="attn_bwd", grid=(heads, nq),
        in_specs=[pl.BlockSpec((tq, HEAD_PAD), lambda h, i: (i, h)),
                  pl.BlockSpec((t, HEAD_PAD), lambda h, i: (0, h)),
                  pl.BlockSpec((t, LANES), lambda h, i: (0, 0)),
                  pl.BlockSpec((tq, V_DIM), lambda h, i: (i, h)),
                  pl.BlockSpec((tq, V_DIM), lambda h, i: (i, h)),
                  pl.BlockSpec((tq, V_DIM), lambda h, i: (i, h)),
                  pl.BlockSpec((tq, LANES), lambda h, i: (i, 0)),
                  pl.BlockSpec((tq, LANES), lambda h, i: (i, 0))],
        out_specs=[pl.BlockSpec((tq, HEAD_PAD), lambda h, i: (i, h)),
                   pl.BlockSpec((t, HEAD_PAD), lambda h, i: (0, h)),
                   pl.BlockSpec((t, LANES), lambda h, i: (0, 0))],
        out_shape=[jax.ShapeDtypeStruct((t, heads * HEAD_PAD), BF16),
                   jax.ShapeDtypeStruct((t, heads * HEAD_PAD), BF16),
                   jax.ShapeDtypeStruct((t, LANES), F32)],
        scratch_shapes=[pltpu.VMEM((t, HEAD_PAD), BF16), pltpu.VMEM((t, HEAD_PAD), F32), pltpu.VMEM((t, V_DIM), F32)],
        compiler_params=_params(("arbitrary", "arbitrary")),
    )(q, kv, kr, out, lse, d_out, cos_t, sin_t)


def _layer_norm_parts(x):
    mu = jnp.mean(x, axis=-1, keepdims=True)
    xc = x - mu
    r = lax.rsqrt(jnp.mean(xc * xc, axis=-1, keepdims=True) + EPS)
    return xc * r, r


def _gmlp_fwd(proj, ln_g, ln_b, w_s, b_sb, g_out_norm, lay):
    gw = lay["gw"]
    g_heads = gw // G_HEAD_DIM

    def body(u_ref, v_ref, lng_ref, lnb_ref, ws_ref, bs_ref, gn_ref, o_ref, gate_ref):
        gu = _gelu(u_ref[...])
        vh, _ = _layer_norm_parts(_gelu(v_ref[...]))
        vln = (vh * lng_ref[...] + lnb_ref[...]).astype(BF16)
        for g in range(g_heads):
            cols = slice(g * G_HEAD_DIM, (g + 1) * G_HEAD_DIM)
            s = jnp.dot(ws_ref[g], vln[:, cols], preferred_element_type=F32) + bs_ref[g]
            gate_ref[:, cols] = gu[:, cols] * s
        o_ref[...] = _rms_fwd(gate_ref[...], gn_ref[...]).astype(BF16)

    t = proj.shape[0]
    in_specs = [pl.BlockSpec((CHUNK, gw), lambda i: (i, 0)), pl.BlockSpec((CHUNK, gw), lambda i: (i, 1))]
    pars = [ln_g, ln_b, w_s, b_sb, g_out_norm]
    in_specs += [pl.BlockSpec(a.shape, functools.partial(lambda i, nd: (0,) * nd, nd=a.ndim)) for a in pars]
    return pl.pallas_call(
        body, name="gmlp_fwd", grid=(t // CHUNK,), in_specs=in_specs,
        out_specs=pl.BlockSpec((CHUNK, gw), lambda i: (i, 0)),
        out_shape=jax.ShapeDtypeStruct((t, gw), BF16),
        scratch_shapes=[pltpu.VMEM((CHUNK, gw), F32)],
        compiler_params=_params(("parallel",)),
    )(proj, proj, *pars)


def _gmlp_bwd(proj, d_mixed, ln_g, ln_b, w_s, w_st, b_sb, g_out_norm, lay):
    gw = lay["gw"]
    g_heads = gw // G_HEAD_DIM
    aw_blocks = lay["aw"] // gw

    def body(u_ref, v_ref, dm_ref, lng_ref, lnb_ref, ws_ref, wst_ref, bs_ref, gn_ref,
             du_ref, dv_ref, dgn_ref, dlng_ref, dlnb_ref, dws_ref, dbs_ref, gate_ref, s_ref, dvln_ref):
        i = pl.program_id(0)
        u, v = u_ref[...], v_ref[...]
        gu, gv = _gelu(u), _gelu(v)
        vh, r_ln = _layer_norm_parts(gv)
        vln = (vh * lng_ref[...] + lnb_ref[...]).astype(BF16)
        for g in range(g_heads):
            cols = slice(g * G_HEAD_DIM, (g + 1) * G_HEAD_DIM)
            s = jnp.dot(ws_ref[g], vln[:, cols], preferred_element_type=F32) + bs_ref[g]
            s_ref[:, cols] = s
            gate_ref[:, cols] = gu[:, cols] * s
        d_gate, dgn = _rms_bwd(gate_ref[...], gn_ref[...], dm_ref[...])
        _accumulate(dgn_ref, _colsum(dgn))
        du_ref[...] = (d_gate * s_ref[...] * _gelu_grad(u)).astype(BF16)
        d_s = d_gate * gu
        d_sb = d_s.astype(BF16)
        for g in range(g_heads):
            cols = slice(g * G_HEAD_DIM, (g + 1) * G_HEAD_DIM)
            dw = lax.dot_general(d_sb[:, cols], vln[:, cols], NT, preferred_element_type=F32)

            @pl.when(i == 0)
            def _():
                dws_ref[g] = dw
                dbs_ref[g] = d_s[:, cols]

            @pl.when(i > 0)
            def _():
                dws_ref[g] += dw
                dbs_ref[g] += d_s[:, cols]

            dvln_ref[:, cols] = jnp.dot(wst_ref[g], d_sb[:, cols], preferred_element_type=F32)
        d_vln = dvln_ref[...]
        _accumulate(dlng_ref, _colsum(d_vln * vh))
        _accumulate(dlnb_ref, _colsum(d_vln))
        d_vh = d_vln * lng_ref[...]
        d_gv = r_ln * (d_vh - jnp.mean(d_vh, axis=-1, keepdims=True)
                       - vh * jnp.mean(d_vh * vh, axis=-1, keepdims=True))
        dv_ref[...] = (d_gv * _gelu_grad(v)).astype(BF16)

    t = proj.shape[0]
    whole = lambda a: pl.BlockSpec(a.shape, functools.partial(lambda i, nd: (0,) * nd, nd=a.ndim))
    pars = [ln_g, ln_b, w_s, w_st, b_sb, g_out_norm]
    hshape = (g_heads, CHUNK, CHUNK)
    return pl.pallas_call(
        body, name="gmlp_bwd", grid=(t // CHUNK,),
        in_specs=[pl.BlockSpec((CHUNK, gw), lambda i: (i, 0)), pl.BlockSpec((CHUNK, gw), lambda i: (i, 1)),
                  pl.BlockSpec((CHUNK, gw), lambda i: (i, aw_blocks))] + [whole(a) for a in pars],
        out_specs=[pl.BlockSpec((CHUNK, gw), lambda i: (i, 0)), pl.BlockSpec((CHUNK, gw), lambda i: (i, 0)),
                   pl.BlockSpec((1, gw), lambda i: (0, 0)), pl.BlockSpec((1, gw), lambda i: (0, 0)),
                   pl.BlockSpec((1, gw), lambda i: (0, 0)),
                   pl.BlockSpec(hshape, lambda i: (0, 0, 0)), pl.BlockSpec(hshape, lambda i: (0, 0, 0))],
        out_shape=[jax.ShapeDtypeStruct((t, gw), BF16), jax.ShapeDtypeStruct((t, gw), BF16),
                   jax.ShapeDtypeStruct((1, gw), F32), jax.ShapeDtypeStruct((1, gw), F32),
                   jax.ShapeDtypeStruct((1, gw), F32),
                   jax.ShapeDtypeStruct(hshape, F32), jax.ShapeDtypeStruct(hshape, F32)],
        scratch_shapes=[pltpu.VMEM((CHUNK, gw), F32), pltpu.VMEM((CHUNK, gw), F32), pltpu.VMEM((CHUNK, gw), F32)],
        compiler_params=_params(("arbitrary",)),
    )(proj, proj, d_mixed, *pars)


def _spatial_bias_grad(dbs_wide):
    g_heads = dbs_wide.shape[0]

    def body(x_ref, o_ref):
        for g in range(g_heads):
            o_ref[g:g + 1, :] = jnp.sum(x_ref[g].T, axis=0, keepdims=True)

    return pl.pallas_call(
        body, name="spatial_bias_grad", out_shape=jax.ShapeDtypeStruct((g_heads, CHUNK), F32),
        in_specs=[pl.BlockSpec(memory_space=pltpu.VMEM)], out_specs=pl.BlockSpec(memory_space=pltpu.VMEM),
    )(dbs_wide)


def _mix_norm(a_out, gn, g_a, tr, token):
    aw = a_out.shape[1]
    gw = gn.shape[1]

    def body(a_ref, gn_ref, g_ref, token_ref, o_ref):
        o_ref[:, :aw] = _rms_fwd(a_ref[...], g_ref[...]).astype(BF16)
        o_ref[:, aw:] = gn_ref[...]

    t = a_out.shape[0]
    return _row_call(body, "mix_norm", t, tr, [(a_out, aw, 0), (gn, gw, 0)], [g_a, token], [(aw + gw, BF16)], [])[0]


def _mix_norm_bwd(a_out, d_mixed, g_a, tr, token):
    aw = a_out.shape[1]

    def body(a_ref, dm_ref, g_ref, token_ref, da_ref, dg_ref):
        dx, dg = _rms_bwd(a_ref[...], g_ref[...], dm_ref[...])
        da_ref[...] = dx.astype(BF16)
        _accumulate(dg_ref, _colsum(dg))

    t = a_out.shape[0]
    return _row_call(body, "mix_norm_bwd", t, tr, [(a_out, aw, 0), (d_mixed, aw, 0)], [g_a, token],
                     [(aw, BF16)], [((1, aw), F32)])


def _post_mix(x, mix_out, g_pm, g_pf, tr):
    def body(x_ref, mo_ref, gpm_ref, gpf_ref, h_ref, hn_ref):
        h = x_ref[...] + _rms_fwd(mo_ref[...], gpm_ref[...])
        h_ref[...] = h
        hn_ref[...] = _rms_fwd(h, gpf_ref[...]).astype(BF16)

    t, d = x.shape
    return _row_call(body, "post_mix", t, tr, [(x, d, 0), (mix_out, d, 0)], [g_pm, g_pf], [(d, F32), (d, BF16)], [])


def _post_mix_bwd(mix_out, h, dy, d_hn, g_pm, g_pf, tr, token):
    def body(mo_ref, h_ref, dy_ref, dhn_ref, gpm_ref, gpf_ref, token_ref, dh_ref, dmo_ref, dgpf_ref, dgpm_ref):
        dx, dg = _rms_bwd(h_ref[...], gpf_ref[...], dhn_ref[...])
        dh = dy_ref[...] + dx
        dh_ref[...] = dh
        _accumulate(dgpf_ref, _colsum(dg))
        dmo, dg2 = _rms_bwd(mo_ref[...], gpm_ref[...], dh)
        dmo_ref[...] = dmo.astype(BF16)
        _accumulate(dgpm_ref, _colsum(dg2))

    t, d = h.shape
    return _row_call(body, "post_mix_bwd", t, tr, [(mix_out, d, 0), (h, d, 0), (dy, d, 0), (d_hn, d, 0)],
                     [g_pm, g_pf, token], [(d, F32), (d, BF16)], [((1, d), F32), ((1, d), F32)])


def _swiglu(gate, up):
    t, f = gate.shape
    tr, tf = _tile(t, 512), _tile(f, 2048)

    def body(g_ref, u_ref, o_ref):
        g = g_ref[...].astype(F32)
        o_ref[...] = (g * _sigmoid(g) * u_ref[...].astype(F32)).astype(BF16)

    spec = pl.BlockSpec((tr, tf), lambda i, j: (i, j))
    return pl.pallas_call(body, name="swiglu", grid=(t // tr, f // tf), in_specs=[spec, spec], out_specs=spec,
                          out_shape=jax.ShapeDtypeStruct((t, f), BF16),
                          compiler_params=_params(("parallel", "parallel")))(gate, up)


def _swiglu_bwd(gate, up, d_act):
    t, f = gate.shape
    tr, tf = _tile(t, 512), _tile(f, 2048)

    def body(g_ref, u_ref, da_ref, dg_ref, du_ref):
        g, u, da = g_ref[...].astype(F32), u_ref[...].astype(F32), da_ref[...].astype(F32)
        sg = _sigmoid(g)
        du_ref[...] = (da * (g * sg)).astype(BF16)
        dg_ref[...] = (da * u * (sg * (1.0 + g * (1.0 - sg)))).astype(BF16)

    spec = pl.BlockSpec((tr, tf), lambda i, j: (i, j))
    shape = jax.ShapeDtypeStruct((t, f), BF16)
    return pl.pallas_call(body, name="swiglu_bwd", grid=(t // tr, f // tf), in_specs=[spec, spec, spec],
                          out_specs=[spec, spec], out_shape=[shape, shape],
                          compiler_params=_params(("parallel", "parallel")))(gate, up, d_act)


def _loss_head(h, ffn, target, g_po, tr):
    t, d = h.shape

    def body(h_ref, f_ref, t_ref, g_ref, dy_ref, df_ref, dg_ref, loss_ref):
        f = f_ref[...]
        err = h_ref[...] + _rms_fwd(f, g_ref[...]) - t_ref[...]
        dy = err * (1.0 / d)
        dy_ref[...] = dy
        df, dg = _rms_bwd(f, g_ref[...], dy)
        df_ref[...] = df.astype(BF16)
        _accumulate(dg_ref, _colsum(dg))
        sq = jnp.sum(_colsum(err * err), axis=-1, keepdims=True) * (0.5 / d)
        _accumulate(loss_ref, jnp.broadcast_to(sq, (1, LANES)))

    return _row_call(body, "loss_head", t, tr, [(h, d, 0), (ffn, d, 0), (target, d, 0)], [g_po],
                     [(d, F32), (d, BF16)], [((1, d), F32), ((1, LANES), F32)])


def _qkv_bwd(proj, d_qn, d_kvn, d_kr, g_q, g_kv, cos_t, sin_t, lay, tr):
    ql, kl = lay["ql"], lay["kl"]

    def body(q_ref, kv_ref, dqn_ref, dkvn_ref, dkr_ref, cos_ref, sin_ref, gq_ref, gkv_ref,
             dq_ref, dkv_ref, dkt_ref, dgq_ref, dgkv_ref):
        dx, dg = _rms_bwd(q_ref[...], gq_ref[...], dqn_ref[...])
        dq_ref[...] = dx.astype(BF16)
        _accumulate(dgq_ref, _colsum(dg))
        dx, dg = _rms_bwd(kv_ref[...], gkv_ref[...], dkvn_ref[...])
        dkv_ref[...] = dx.astype(BF16)
        _accumulate(dgkv_ref, _colsum(dg))
        dkt_ref[...] = _rope_bwd(dkr_ref[...], cos_ref[...], sin_ref[...]).astype(BF16)

    t = proj.shape[0]
    return _row_call(
        body, "qkv_bwd", t, tr,
        [(proj, ql, lay["q_off"] // ql), (proj, kl, lay["kv_off"] // kl), (d_qn, ql, 0), (d_kvn, kl, 0),
         (d_kr, LANES, 0), (cos_t, LANES, 0), (sin_t, LANES, 0)],
        [g_q, g_kv], [(ql, BF16), (kl, BF16), (LANES, BF16)], [((1, ql), F32), ((1, kl), F32)])


def _prenorm_bwd(x, d_xn, dh, g, tr):
    def body(x_ref, dxn_ref, dh_ref, g_ref, gx_ref, dg_ref):
        dx, dg = _rms_bwd(x_ref[...], g_ref[...], dxn_ref[...])
        gx_ref[...] = dh_ref[...] + dx
        _accumulate(dg_ref, _colsum(dg))

    t, d = x.shape
    return _row_call(body, "prenorm_bwd", t, tr, [(x, d, 0), (d_xn, d, 0), (dh, d, 0)], [g],
                     [(d, F32)], [((1, d), F32)])


def _adam_rows(rows, cols, block_elements=256 * 1024):
    cap = max(8, block_elements // cols // 8 * 8)
    tr = min(rows, cap)
    while rows % tr:
        tr -= 8
    return tr


def _adamw(w, g, m, v, name):
    rows, cols = w.shape
    tr = _adam_rows(rows, cols)

    def body(w_ref, g_ref, m_ref, v_ref, d_ref, mo_ref, vo_ref):
        g = g_ref[...]
        m2 = ADAM_B1 * m_ref[...] + (1.0 - ADAM_B1) * g
        v2 = ADAM_B2 * v_ref[...] + (1.0 - ADAM_B2) * (g * g)
        m_hat = m2 / (1.0 - ADAM_B1 ** ADAM_STEP)
        v_hat = v2 / (1.0 - ADAM_B2 ** ADAM_STEP)
        d_ref[...] = -ADAM_LR * (m_hat / (jnp.sqrt(v_hat) + ADAM_EPS) + ADAM_WD * w_ref[...])
        mo_ref[...] = m2
        vo_ref[...] = v2

    spec = pl.BlockSpec((tr, cols), lambda i: (i, 0))
    shape = jax.ShapeDtypeStruct((rows, cols), F32)
    return pl.pallas_call(body, name=name, grid=(rows // tr,), in_specs=[spec] * 4, out_specs=[spec] * 3,
                          out_shape=[shape] * 3, compiler_params=_params(("parallel",)))(w, g, m, v)


def _adamw_halves(w, g_mine, g_theirs, m, v, name):
    rows, cols = w.shape
    rh = g_mine.shape[0]
    tr = _adam_rows(math.gcd(rows, rh), cols)
    per_half = rh // tr
    my_c = jnp.reshape(lax.axis_index("c"), (1,)).astype(jnp.int32)

    def body(c_ref, w_ref, gm_ref, gt_ref, m_ref, v_ref, g_ref, d_ref, mo_ref, vo_ref):
        mine = (pl.program_id(0) // per_half) == c_ref[0]
        g = jnp.where(mine, gm_ref[...], gt_ref[...])
        m2 = ADAM_B1 * m_ref[...] + (1.0 - ADAM_B1) * g
        v2 = ADAM_B2 * v_ref[...] + (1.0 - ADAM_B2) * (g * g)
        m_hat = m2 / (1.0 - ADAM_B1 ** ADAM_STEP)
        v_hat = v2 / (1.0 - ADAM_B2 ** ADAM_STEP)
        g_ref[...] = g
        d_ref[...] = -ADAM_LR * (m_hat / (jnp.sqrt(v_hat) + ADAM_EPS) + ADAM_WD * w_ref[...])
        mo_ref[...] = m2
        vo_ref[...] = v2

    def half_spec(is_mine):
        def index(i, c_ref):
            used = ((i // per_half) == c_ref[0]) if is_mine else ((i // per_half) != c_ref[0])
            return (jnp.where(used, i % per_half, 0), 0)
        return pl.BlockSpec((tr, cols), index)

    spec = pl.BlockSpec((tr, cols), lambda i, c_ref: (i, 0))
    shape = jax.ShapeDtypeStruct((rows, cols), F32)
    grid_spec = pltpu.PrefetchScalarGridSpec(
        num_scalar_prefetch=1, grid=(rows // tr,),
        in_specs=[spec, half_spec(True), half_spec(False), spec, spec], out_specs=[spec] * 4)
    return pl.pallas_call(body, name=name, grid_spec=grid_spec, out_shape=[shape] * 4,
                          compiler_params=_params(("parallel",)))(my_c, w, g_mine, g_theirs, m, v)


def _pair_add(parts, theirs, name):
    _, n, r, c = parts.shape
    tr = _adam_rows(r, c, 1024 * 1024)
    my_c = jnp.reshape(lax.axis_index("c"), (1,)).astype(jnp.int32)

    def body(c_ref, a_ref, b_ref, o_ref):
        o_ref[0] = (a_ref[0, 0].astype(F32) + b_ref[0].astype(F32)).astype(BF16)

    spec = pl.BlockSpec((1, tr, c), lambda k, i, c_ref: (k, i, 0))
    grid_spec = pltpu.PrefetchScalarGridSpec(
        num_scalar_prefetch=1, grid=(n, r // tr),
        in_specs=[pl.BlockSpec((1, 1, tr, c), lambda k, i, c_ref: (c_ref[0], k, i, 0)), spec], out_specs=spec)
    return pl.pallas_call(body, name=name, grid_spec=grid_spec, out_shape=jax.ShapeDtypeStruct((n, r, c), BF16),
                          compiler_params=_params(("parallel", "parallel")))(my_c, parts, theirs)


def _chip_sum(pair_sums, received, name, token=None):
    _, r, c = pair_sums.shape
    tr = _adam_rows(r, c)
    own = 2 * lax.axis_index("x") + lax.axis_index("y")

    def body(own_ref, p_ref, r0_ref, r1_ref, r2_ref, *rest):
        o_ref = rest[-1]
        acc = p_ref[0].astype(F32) + r0_ref[0].astype(F32)
        acc = acc + r1_ref[0].astype(F32)
        o_ref[...] = acc + r2_ref[0].astype(F32)

    def rspec(j):
        return pl.BlockSpec((1, tr, c), functools.partial(lambda i, own_ref, j: (j, i, 0), j=j))

    extra = [] if token is None else [token]
    grid_spec = pltpu.PrefetchScalarGridSpec(
        num_scalar_prefetch=1, grid=(r // tr,),
        in_specs=[pl.BlockSpec((1, tr, c), lambda i, own_ref: (own_ref[0], i, 0)), rspec(0), rspec(1), rspec(2)]
        + [pl.BlockSpec(memory_space=pl.ANY)] * len(extra),
        out_specs=pl.BlockSpec((tr, c), lambda i, own_ref: (i, 0)))
    return pl.pallas_call(body, name=name, grid_spec=grid_spec, out_shape=jax.ShapeDtypeStruct((r, c), F32),
                          compiler_params=_params(("parallel",)))(
        jnp.reshape(own, (1,)).astype(jnp.int32), pair_sums, received, received, received, *extra)


def _mesh_place():
    x, y, c = lax.axis_index("x"), lax.axis_index("y"), lax.axis_index("c")
    other_chips = [(1 - x, y), (x, 1 - y), (1 - x, 1 - y)]
    return x, y, c, other_chips


def _hbm_specs(n):
    return [pl.BlockSpec(memory_space=pltpu.HBM)] * n


def _sibling_exchange(parts, name):
    n = len(parts)

    def body(*refs):
        ins, outs = refs[:n], refs[n:2 * n]
        send_sems, recv_sems = refs[2 * n:]
        x, y, c, _ = _mesh_place()
        copies = [pltpu.make_async_remote_copy(src_ref=ins[w].at[1 - c], dst_ref=outs[w], send_sem=send_sems.at[w],
                                               recv_sem=recv_sems.at[w], device_id=(x, y, 1 - c), device_id_type=MESH)
                  for w in range(n)]
        for cp in copies:
            cp.start()
        for cp in copies:
            cp.wait()

    return pl.pallas_call(
        body, name=name,
        out_shape=[jax.ShapeDtypeStruct(p.shape[1:], p.dtype) for p in parts],
        in_specs=_hbm_specs(n), out_specs=_hbm_specs(n),
        scratch_shapes=[pltpu.SemaphoreType.DMA((n,)), pltpu.SemaphoreType.DMA((n,))],
    )(*parts)


SEM_SPEC = pl.BlockSpec(memory_space=pltpu.SEMAPHORE)
DATAFLOW_EFFECT = pltpu.SideEffectType.DATAFLOW_SIDE_EFFECTING


def _copies_per_weight(kind):
    return {"gather": 4, "scatter": 3, "sibling": 1, "forward": 3}[kind]


def _flight_copies(kind, src_refs, land_refs, send_sems, recv_sems, arriving):
    x, y, c, other_chips = _mesh_place()
    own = 2 * x + y
    sibling = (x, y, 1 - c)
    per = _copies_per_weight(kind)
    copies = []
    for w in range(len(src_refs)):
        def remote(src, dst, j, to):
            return pltpu.make_async_remote_copy(src_ref=src, dst_ref=dst, send_sem=send_sems.at[per * w + j],
                                                recv_sem=recv_sems.at[per * w + j], device_id=to, device_id_type=MESH)

        if kind == "sibling":
            copies.append(remote(src_refs[w].at[1 - c], land_refs[w], 0, sibling))
            continue
        for j, chip in enumerate(other_chips):
            theirs = 2 * chip[0] + chip[1]
            if kind == "gather":
                copies.append(remote(src_refs[w].at[c], land_refs[w].at[theirs if arriving else own, c], j, (*chip, c)))
            elif kind == "forward":
                copies.append(remote(src_refs[w].at[theirs, c], src_refs[w].at[theirs, (1 - c) if arriving else c],
                                     j, sibling))
            else:
                copies.append(remote(src_refs[w].at[theirs], land_refs[w].at[j], j, (*chip, c)))
        if kind == "gather":
            copies.append(remote(src_refs[w], land_refs[w].at[own], 3, sibling))
    return copies


def _ici_start(kind, srcs, name, after=None):
    n = len(srcs)
    if kind == "gather":
        lands = [lax.empty((4,) + s.shape, s.dtype) for s in srcs]
    elif kind == "scatter":
        lands = [lax.empty((3,) + s.shape[1:], s.dtype) for s in srcs]
    elif kind == "sibling":
        lands = [lax.empty(s.shape[1:], s.dtype) for s in srcs]
    else:
        lands = []
    nb = n + len(lands)
    afters = [] if after is None else [after]

    def body(*refs):
        src_refs, land_refs = refs[:n], refs[n:nb]
        send_sems, recv_sems = refs[nb + len(afters)], refs[nb + len(afters) + 1]
        token = refs[-1]
        for cp in _flight_copies(kind, src_refs, land_refs, send_sems, recv_sems, False):
            cp.start()
        token[...] = jnp.zeros_like(token)

    hbm = lambda a: pltpu.with_memory_space_constraint(a, pltpu.HBM)
    n_sems = _copies_per_weight(kind) * n
    out = pl.pallas_call(
        body, name=name,
        out_shape=(pltpu.SemaphoreType.DMA((n_sems,)), pltpu.SemaphoreType.DMA((n_sems,)),
                   *[pltpu.HBM(a.shape, a.dtype) for a in srcs + lands], jax.ShapeDtypeStruct((8, LANES), F32)),
        in_specs=_hbm_specs(nb) + [pl.BlockSpec(memory_space=pl.ANY)] * len(afters),
        out_specs=(SEM_SPEC, SEM_SPEC, *_hbm_specs(nb), pl.BlockSpec(memory_space=pltpu.VMEM)),
        input_output_aliases={i: 2 + i for i in range(nb)},
        compiler_params=pltpu.CompilerParams(has_side_effects=DATAFLOW_EFFECT),
    )(*[hbm(a) for a in srcs + lands], *afters)
    return out[0], out[1], list(out[2:2 + n]), list(out[2 + n:2 + nb]), out[-1]


def _ici_wait(kind, send_sems, recv_sems, srcs, lands, after, name):
    n = len(srcs)
    nb = n + len(lands)

    def body(*refs):
        src_refs, land_refs = refs[:n], refs[n:nb]
        send_ref, recv_ref = refs[nb], refs[nb + 1]
        for cp in _flight_copies(kind, src_refs, land_refs, send_ref, recv_ref, True):
            cp.wait_send()
            cp.wait_recv()

    out = pl.pallas_call(
        body, name=name, out_shape=tuple(pltpu.HBM(a.shape, a.dtype) for a in srcs + lands),
        in_specs=_hbm_specs(nb) + [SEM_SPEC, SEM_SPEC, pl.BlockSpec(memory_space=pl.ANY)],
        out_specs=tuple(_hbm_specs(nb)), input_output_aliases={i: i for i in range(nb)},
        compiler_params=pltpu.CompilerParams(has_side_effects=DATAFLOW_EFFECT),
    )(*srcs, *lands, send_sems, recv_sems, after)
    return list(out[:n]), list(out[n:])


def _halves_exchange(halves, name):
    n = len(halves)

    def body(*refs):
        ins, outs = refs[:n], refs[n:2 * n]
        send_sems, recv_sems = refs[2 * n:]
        x, y, c, _ = _mesh_place()
        copies = [pltpu.make_async_remote_copy(src_ref=ins[w], dst_ref=outs[w], send_sem=send_sems.at[w],
                                               recv_sem=recv_sems.at[w], device_id=(x, y, 1 - c), device_id_type=MESH)
                  for w in range(n)]
        for cp in copies:
            cp.start()
        for cp in copies:
            cp.wait()

    return pl.pallas_call(
        body, name=name,
        out_shape=[jax.ShapeDtypeStruct(h.shape, h.dtype) for h in halves],
        in_specs=_hbm_specs(n), out_specs=_hbm_specs(n),
        scratch_shapes=[pltpu.SemaphoreType.DMA((n,)), pltpu.SemaphoreType.DMA((n,))],
    )(*halves)


def _small_all_reduce(packed):
    rows = packed.shape[0]

    def body(in_ref, out_ref, gathered, send_sems, recv_sems):
        x, y, c, _ = _mesh_place()
        me = 4 * x + 2 * y + c
        gathered[0] = in_ref[...]
        copies = []
        for rel in range(1, 8):
            to = (x ^ (rel >> 2), y ^ ((rel >> 1) & 1), c ^ (rel & 1))
            cp = pltpu.make_async_remote_copy(src_ref=in_ref, dst_ref=gathered.at[rel], send_sem=send_sems.at[rel - 1],
                                              recv_sem=recv_sems.at[rel - 1], device_id=to, device_id_type=MESH)
            cp.start()
            copies.append(cp)
        for cp in copies:
            cp.wait()
        acc = gathered[me]
        for dev in range(1, 8):
            acc = acc + gathered[dev ^ me]
        out_ref[...] = acc

    return pl.pallas_call(
        body, name="small_all_reduce", out_shape=jax.ShapeDtypeStruct(packed.shape, F32),
        in_specs=[pl.BlockSpec(memory_space=pltpu.VMEM)], out_specs=pl.BlockSpec(memory_space=pltpu.VMEM),
        scratch_shapes=[pltpu.VMEM((8, rows, LANES), F32), pltpu.SemaphoreType.DMA((7,)), pltpu.SemaphoreType.DMA((7,))],
        compiler_params=pltpu.CompilerParams(vmem_limit_bytes=VMEM_LIMIT_BYTES),
    )(packed)


def _layout(w_in, w_uq, w_ukv, v_ln_gain, q_norm, kv_norm):
    heads = 4 * w_uq.shape[-1] // (NOPE_DIM + ROPE_DIM)
    gw = v_ln_gain.shape[-1]
    ql, kl = q_norm.shape[-1], kv_norm.shape[-1]
    lay = dict(heads=heads, gw=gw, ql=ql, kl=kl, aw=heads * V_DIM, u_off=0, v_off=gw, q_off=2 * gw,
               kv_off=2 * gw + ql, kr_off=2 * gw + ql + kl)
    lay["in_pad"] = _round_up(lay["kr_off"] + LANES, 2 * LANES if lay["kr_off"] + LANES <= 2048 else 1024)
    assert lay["q_off"] % ql == 0 and lay["kv_off"] % kl == 0 and lay["aw"] % gw == 0
    assert 4 * w_in.shape[-1] == ql + kl + ROPE_DIM + 2 * gw
    return lay


def _rope_tile(t1, t2, axis=-1):
    z = jnp.zeros_like(t1)
    return jnp.concatenate([t1, z, t2, z], axis=axis)


def _w_in_rows(gathered, shard_rows, lay):
    d = gathered.shape[-1]
    wt = gathered[:, :shard_rows].reshape(4 * shard_rows, d)
    ql, kl, gw = lay["ql"], lay["kl"], lay["gw"]
    q_c, kv_c = wt[:ql], wt[ql:ql + kl]
    r = wt[ql + kl:ql + kl + ROPE_DIM]
    u = wt[ql + kl + ROPE_DIM:ql + kl + ROPE_DIM + gw]
    v = wt[ql + kl + ROPE_DIM + gw:]
    parts = [u, v, q_c, kv_c, _rope_tile(r[:ROPE_HALF], r[ROPE_HALF:], axis=0)]
    pad = lay["in_pad"] - (lay["kr_off"] + LANES)
    if pad:
        parts.append(jnp.zeros((pad, d), wt.dtype))
    return jnp.concatenate(parts, axis=0)


def _w_in_grad_chunks(dwt, shard_rows, padded_rows, lay):
    d = dwt.shape[-1]
    ql, kl, gw = lay["ql"], lay["kl"], lay["gw"]
    ko = lay["kr_off"]
    rows = jnp.concatenate([dwt[lay["q_off"]:lay["q_off"] + ql], dwt[lay["kv_off"]:lay["kv_off"] + kl],
                            dwt[ko:ko + ROPE_HALF], dwt[ko + 2 * ROPE_HALF:ko + 3 * ROPE_HALF],
                            dwt[:gw], dwt[gw:2 * gw]], axis=0).reshape(4, shard_rows, d)
    rows = jnp.pad(rows, ((0, 0), (0, padded_rows - shard_rows), (0, 0)))
    return jnp.transpose(rows.reshape(4, 2, padded_rows // 2, d), (1, 0, 2, 3)).astype(BF16)


def _w_uq_padded(w, heads):
    w3 = w.reshape(w.shape[0], heads, NOPE_DIM + ROPE_DIM)
    t = _rope_tile(w3[..., NOPE_DIM:NOPE_DIM + ROPE_HALF], w3[..., NOPE_DIM + ROPE_HALF:])
    return jnp.concatenate([w3[..., :NOPE_DIM], t], axis=-1).reshape(w.shape[0], heads * HEAD_PAD)


def _w_uq_grad_unpadded(dw, heads):
    d3 = dw.reshape(dw.shape[0], heads, HEAD_PAD)
    return jnp.concatenate([d3[..., :NOPE_DIM], d3[..., NOPE_DIM:NOPE_DIM + ROPE_HALF],
                            d3[..., NOPE_DIM + 2 * ROPE_HALF:NOPE_DIM + 3 * ROPE_HALF]],
                           axis=-1).reshape(dw.shape[0], heads * (NOPE_DIM + ROPE_DIM))


def _cols_gathered(g):
    return jnp.transpose(g, (1, 0, 2)).reshape(g.shape[1], 4 * g.shape[2])


def _chunks_of_cols(grad):
    r, c4 = grad.shape
    return jnp.transpose(grad.reshape(2, r // 2, 4, c4 // 4), (0, 2, 1, 3)).astype(BF16)


SMALL = ["pre_mix_norm", "q_norm", "kv_norm", "v_ln_gain", "v_ln_bias", "w_spatial", "b_spatial", "attn_out_norm",
         "gmlp_out_norm", "post_mix_norm", "pre_ffn_norm", "post_ffn_norm"]
BIG = ["w_in", "w_uq", "w_ukv", "w_out", "w_gate", "w_up", "w_down"]
GATHER_NOW = ["w_in", "w_uq", "w_ukv"]
GATHER_LATER = ["w_out", "w_gate", "w_up", "w_down"]
REDUCE_FFN = ["w_gate", "w_up", "w_down"]
REDUCE_OUT = ["w_out"]
REDUCE_LAST = ["w_in", "w_uq", "w_ukv"]
TRANSPOSED = ("w_in", "w_gate", "w_up")
ORDER = ["pre_mix_norm", "w_in", "q_norm", "kv_norm", "w_uq", "w_ukv", "v_ln_gain", "v_ln_bias", "w_spatial",
         "b_spatial", "attn_out_norm", "gmlp_out_norm", "w_out", "post_mix_norm", "pre_ffn_norm", "w_gate", "w_up",
         "w_down", "post_ffn_norm"]


def _pack(arrays):
    flat = jnp.concatenate([a.reshape(-1) for a in arrays])
    n = flat.shape[0]
    total = _round_up(n, 8 * LANES)
    if total > n:
        flat = jnp.concatenate([flat, jnp.zeros((total - n,), F32)])
    return flat.reshape(total // LANES, LANES)


def _unpack(packed, like):
    flat = packed.reshape(-1)
    out, off = [], 0
    for a in like:
        out.append(flat[off:off + a.size].reshape(a.shape))
        off += a.size
    return out


def kernel(x, positions, pre_mix_norm, w_in, q_norm, kv_norm, w_uq, w_ukv, v_ln_gain, v_ln_bias, w_spatial, b_spatial, attn_out_norm, gmlp_out_norm, w_out, post_mix_norm, pre_ffn_norm, w_gate, w_up, w_down, post_ffn_norm, loss_target, m_pre_mix_norm, m_w_in, m_q_norm, m_kv_norm, m_w_uq, m_w_ukv, m_v_ln_gain, m_v_ln_bias, m_w_spatial, m_b_spatial, m_attn_out_norm, m_gmlp_out_norm, m_w_out, m_post_mix_norm, m_pre_ffn_norm, m_w_gate, m_w_up, m_w_down, m_post_ffn_norm, v_pre_mix_norm, v_w_in, v_q_norm, v_kv_norm, v_w_uq, v_w_ukv, v_v_ln_gain, v_v_ln_bias, v_w_spatial, v_b_spatial, v_attn_out_norm, v_gmlp_out_norm, v_w_out, v_post_mix_norm, v_pre_ffn_norm, v_w_gate, v_w_up, v_w_down, v_post_ffn_norm):
    args = dict(locals())
    weights = {n: args[n] for n in ORDER}
    m_in = {n: args["m_" + n] for n in ORDER}
    v_in = {n: args["v_" + n] for n in ORDER}

    lay = _layout(w_in, w_uq, w_ukv, v_ln_gain, q_norm, kv_norm)
    heads, gw = lay["heads"], lay["gw"]
    t, d = x.shape[1], x.shape[2]
    tr = 128 if t % 128 == 0 else t
    xs = x.reshape(t, d)
    target = loss_target.reshape(t, d)

    ffs, ins = w_gate.shape[-1], w_in.shape[-1]
    ffp, inp = _round_up(ffs, LANES), _round_up(ins, LANES)
    shards = {n: (jnp.swapaxes(weights[n][0], 0, 1) if n in TRANSPOSED else weights[n][0]).astype(BF16)
              for n in BIG}
    for n, rows in (("w_gate", ffp), ("w_up", ffp), ("w_down", ffp), ("w_in", inp)):
        shards[n] = jnp.pad(shards[n], ((0, rows - shards[n].shape[0]), (0, 0)))
    halved = {n: shards[n].reshape(2, shards[n].shape[0] // 2, shards[n].shape[1]) for n in BIG}
    full = {}

    def pair_sums_of(partial, names, tag):
        from_sibling = _sibling_exchange([partial[n] for n in names], "grads_sibling_exchange_" + tag)
        return [_pair_add(partial[n], r, "pair_add_" + n) for n, r in zip(names, from_sibling)]

    def place(names, lands):
        for n, g in zip(names, lands):
            full[n] = g.reshape((4,) + shards[n].shape)

    flight_0 = _ici_start("gather", [halved[n] for n in GATHER_NOW], "gather_start_0")
    flights, last_token = {}, flight_0[4]
    for n in GATHER_LATER:
        flights[n] = _ici_start("gather", [halved[n]], "gather_start_" + n, after=last_token)
        last_token = flights[n][4]
    _, lands = _ici_wait("gather", *flight_0[:4], last_token, "gather_wait_0")
    pass_now = _ici_start("forward", lands, "forward_start_now")
    place(GATHER_NOW, _ici_wait("forward", *pass_now[:4], pass_now[4], "forward_wait_now")[0])
    wt_in = _w_in_rows(full["w_in"], ins, lay)
    wb_uq = _w_uq_padded(_cols_gathered(full["w_uq"]), heads)
    wb_ukv = _cols_gathered(full["w_ukv"])

    inv_freq = 1.0 / (ROPE_THETA ** (jnp.arange(0, ROPE_DIM, 2, dtype=F32) / ROPE_DIM))
    ang = positions.reshape(t).astype(F32)[:, None] * inv_freq
    cos, sin = jnp.cos(ang), jnp.sin(ang)
    cos_t = _rope_tile(cos, cos)
    sin_t = _rope_tile(-sin, sin)

    row = lambda a: a.reshape(1, -1)
    g_pre, g_q, g_kv = row(pre_mix_norm), row(q_norm), row(kv_norm)
    g_a, g_g, g_pm = row(attn_out_norm), row(gmlp_out_norm), row(post_mix_norm)
    g_pf, g_po = row(pre_ffn_norm), row(post_ffn_norm)
    ln_g, ln_b = row(v_ln_gain), row(v_ln_bias)
    ws = w_spatial[0].astype(BF16)
    ws_t = jnp.transpose(ws, (0, 2, 1))
    bs_wide = jnp.broadcast_to(b_spatial[0][:, :, None], b_spatial.shape[1:] + (G_HEAD_DIM,))

    xn = _prenorm(xs, g_pre, tr, last_token)
    proj = _matmul(xn, wt_in, NT, F32, "proj")
    qn, kvn, kr = _qkv_prep(proj, g_q, g_kv, cos_t, sin_t, lay, tr)
    q = _q_rope(_matmul(qn, wb_uq, NN, F32, "q_up"), cos_t, sin_t, heads, tr)
    kv = _matmul(kvn, wb_ukv, NN, BF16, "kv_up")
    a_out, a_lse = _attn_fwd(q, kv, kr, heads)
    def arrive(n, after):
        _, lands = _ici_wait("gather", *flights[n][:4], after, "gather_wait_" + n)
        return _ici_start("forward", lands, "forward_start_" + n)

    def settle(n, passing, after):
        place([n], _ici_wait("forward", *passing[:4], after, "forward_wait_" + n)[0])

    pass_out = arrive("w_out", a_out)
    gn = _gmlp_fwd(proj, ln_g, ln_b, ws, bs_wide, g_g, lay)
    mixed = _mix_norm(a_out, gn, g_a, tr, pass_out[4])
    settle("w_out", pass_out, mixed)
    pass_gate = arrive("w_gate", mixed)
    wb_out = full["w_out"].reshape(-1, d)
    mix_out = _matmul(mixed, wb_out, NN, F32, "mix_out", token=pass_gate[4])
    h, hn = _post_mix(xs, mix_out, g_pm, g_pf, tr)
    settle("w_gate", pass_gate, hn)
    pass_up = arrive("w_up", hn)
    wt_gate = full["w_gate"].reshape(4 * ffp, d)
    gate = _matmul(hn, wt_gate, NT, BF16, "ffn_gate", token=pass_up[4])
    settle("w_up", pass_up, gate)
    pass_down = arrive("w_down", gate)
    wt_up = full["w_up"].reshape(4 * ffp, d)
    up = _matmul(hn, wt_up, NT, BF16, "ffn_up", token=pass_down[4])
    act = _swiglu(gate, up)
    settle("w_down", pass_down, act)
    wb_down = full["w_down"].reshape(4 * ffp, d)
    ffn = _matmul(act, wb_down, NN, F32, "ffn_down")
    dy, d_ffn, dg_po, loss_vec = _loss_head(h, ffn, target, g_po, tr)

    d_act = _matmul(d_ffn, wb_down, NT, BF16, "d_act")
    d_gate, d_up = _swiglu_bwd(gate, up, d_act)
    partial_ffn = [_matmul(d_gate, hn, TN, BF16, "gw_gate", out_chunks=True),
                   _matmul(d_up, hn, TN, BF16, "gw_up", out_chunks=True),
                   _matmul(act, d_ffn, TN, BF16, "gw_down", out_chunks=True)]
    swap_ffn = _ici_start("sibling", partial_ffn, "sibling_start_ffn")
    d_hn = _matmul(d_up, wt_up, NN, F32, "d_hn",
                   extras=[_matmul(d_gate, wt_gate, NN, F32, "d_hn_gate", token=swap_ffn[4])],
                   epilogue=lambda acc, partial: acc + partial)
    partial_ffn, from_sibling = _ici_wait("sibling", *swap_ffn[:4], d_hn, "sibling_wait_ffn")
    pair_ffn = [_pair_add(p, r, "pair_add_" + n) for n, p, r in zip(REDUCE_FFN, partial_ffn, from_sibling)]
    flight_ffn = _ici_start("scatter", pair_ffn, "scatter_start_ffn")
    dh, d_mo, dg_pf, dg_pm = _post_mix_bwd(mix_out, h, dy, d_hn, g_pm, g_pf, tr, flight_ffn[4])
    d_mixed = _matmul(d_mo, wb_out, NT, F32, "d_mixed")
    gw_out = _matmul(mixed, d_mo, TN, BF16, "gw_out", out_chunks=True)
    pair_out = pair_sums_of({"w_out": gw_out}, REDUCE_OUT, "out")
    flight_out = _ici_start("scatter", pair_out, "scatter_start_out")
    d_a, dg_a = _mix_norm_bwd(a_out, d_mixed, g_a, tr, flight_out[4])
    d_u, d_v, dg_g, d_ln_g, d_ln_b, d_ws, d_bs_wide = _gmlp_bwd(proj, d_mixed, ln_g, ln_b, ws, ws_t, bs_wide, g_g, lay)
    d_bs = _spatial_bias_grad(d_bs_wide)
    d_q, d_kv, d_kr = _attn_bwd(q, kv, kr, a_out, a_lse, d_a, cos_t, sin_t, heads)
    d_qn = _matmul(d_q, wb_uq, NT, F32, "d_qn")
    gw_uq = _matmul(qn, d_q, TN, F32, "gw_uq")
    d_kvn = _matmul(d_kv, wb_ukv, NT, F32, "d_kvn")
    gw_ukv = _matmul(kvn, d_kv, TN, F32, "gw_ukv")
    d_qc, d_kvc, d_krt, dg_q, dg_kv = _qkv_bwd(proj, d_qn, d_kvn, d_kr, g_q, g_kv, cos_t, sin_t, lay, tr)
    parts = [d_u, d_v, d_qc, d_kvc, d_krt]
    pad = lay["in_pad"] - (lay["kr_off"] + LANES)
    if pad:
        parts.append(jnp.zeros((t, pad), BF16))
    d_proj = jnp.concatenate(parts, axis=1)
    d_xn = _matmul(d_proj, wt_in, NN, F32, "d_xn")
    gw_in = _matmul(d_proj, xn, TN, F32, "gw_in")
    grad_x, dg_pre = _prenorm_bwd(xs, d_xn, dh, g_pre, tr)

    pair_mix = pair_sums_of({"w_in": _w_in_grad_chunks(gw_in, ins, inp, lay),
                             "w_uq": _chunks_of_cols(_w_uq_grad_unpadded(gw_uq, heads)),
                             "w_ukv": _chunks_of_cols(gw_ukv)}, REDUCE_LAST, "mix")
    pair_ffn, received_ffn = _ici_wait("scatter", *flight_ffn[:4], pair_mix[-1], "scatter_wait_ffn")
    pair_out, received_out = _ici_wait("scatter", *flight_out[:4], pair_mix[-1], "scatter_wait_out")
    flight_mix = _ici_start("scatter", pair_mix, "scatter_start_mix")
    grads, delta, new_m, new_v = {}, {}, {}, {}

    def finish(names, pair_sums, received, tag, token):
        mine = [_chip_sum(p, r, "chip_sum_" + n, token) for n, p, r in zip(names, pair_sums, received)]
        theirs = _halves_exchange(mine, "grads_halves_exchange_" + tag)
        for n, g_mine, g_theirs in zip(names, mine, theirs):
            shape = weights[n].shape
            if n in TRANSPOSED:
                view = lambda a: jnp.swapaxes(a[0], 0, 1)
                back = lambda o: jnp.swapaxes(o, 0, 1).reshape(shape)
            else:
                view = lambda a: a[0]
                back = lambda o: o.reshape(shape)
            out = _adamw_halves(view(weights[n]), g_mine, g_theirs, view(m_in[n]), view(v_in[n]), "adamw_" + n)
            grads[n], delta[n], new_m[n], new_v[n] = [back(o) for o in out]

    finish(REDUCE_FFN + REDUCE_OUT, pair_ffn + pair_out, received_ffn + received_out, "ffn", flight_mix[4])
    pair_mix, received = _ici_wait("scatter", *flight_mix[:4], new_v[REDUCE_OUT[-1]], "scatter_wait_mix")
    finish(REDUCE_LAST, pair_mix, received, "mix", None)

    small_grads = {"pre_mix_norm": dg_pre, "q_norm": dg_q, "kv_norm": dg_kv, "v_ln_gain": d_ln_g, "v_ln_bias": d_ln_b,
                   "w_spatial": d_ws, "b_spatial": d_bs, "attn_out_norm": dg_a, "gmlp_out_norm": dg_g,
                   "post_mix_norm": dg_pm, "pre_ffn_norm": dg_pf, "post_ffn_norm": dg_po}
    like = [weights[n] for n in SMALL]
    reduced = _small_all_reduce(_pack([small_grads[n] for n in SMALL] + [loss_vec]))
    loss = reduced.reshape(-1)[sum(a.size for a in like)]
    small_g = _pack(_unpack(reduced, like))
    s_delta, s_m, s_v = _adamw(_pack(like), small_g, _pack([m_in[n] for n in SMALL]),
                               _pack([v_in[n] for n in SMALL]), "adamw_small")
    for n, g in zip(SMALL, _unpack(small_g, like)):
        grads[n] = g
    delta.update(zip(SMALL, _unpack(s_delta, like)))
    new_m.update(zip(SMALL, _unpack(s_m, like)))
    new_v.update(zip(SMALL, _unpack(s_v, like)))

    return (loss, grad_x.reshape(x.shape), *[grads[n] for n in ORDER], *[delta[n] for n in ORDER],
            *[new_m[n] for n in ORDER], *[new_v[n] for n in ORDER])
```

```python
import functools
import math

import jax
import jax.numpy as jnp
from jax import lax
from jax.experimental import pallas as pl
from jax.experimental.pallas import tpu as pltpu

F32 = jnp.float32
BF16 = jnp.bfloat16
MESH = pl.DeviceIdType.MESH

NOPE_DIM = 128
ROPE_DIM = 64
ROPE_HALF = ROPE_DIM // 2
V_DIM = 128
HEAD_PAD = 256
G_HEAD_DIM = 128
CHUNK = 128
ROPE_THETA = 10000.0
EPS = 1e-6
ADAM_LR = 0.001
ADAM_B1 = 0.9
ADAM_B2 = 0.999
ADAM_EPS = 1e-08
ADAM_WD = 0.01
ADAM_STEP = 10

LANES = 128
MATMUL_TILE = 1024
WIDE_TILE = 1408
VMEM_LIMIT_BYTES = 48 * 1024 * 1024

NN = (((1,), (0,)), ((), ()))
NT = (((1,), (1,)), ((), ()))
TN = (((0,), (0,)), ((), ()))


def _params(semantics):
    return pltpu.CompilerParams(dimension_semantics=semantics, vmem_limit_bytes=VMEM_LIMIT_BYTES)


def _tile(n, cap=MATMUL_TILE):
    if n <= cap:
        return n
    if cap == MATMUL_TILE and n % WIDE_TILE == 0:
        return WIDE_TILE
    t = cap - cap % LANES
    while n % t:
        t -= LANES
    assert t > 0, n
    return t


def _round_up(n, m):
    return (n + m - 1) // m * m


def _matmul(a, b, dims, out_dtype, name, extras=(), epilogue=None, out_chunks=None, token=None):
    if dims is NN:
        (m, k), (k2, n) = a.shape, b.shape
    elif dims is NT:
        (m, k), (n, k2) = a.shape, b.shape
    else:
        (k, m), (k2, n) = a.shape, b.shape
    assert k == k2, (a.shape, b.shape, name)
    tm, tn, tk = _tile(m // 8 if out_chunks else m), _tile(n), _tile(k, 2 * MATMUL_TILE)
    if len(extras) + (len(out_dtype) if isinstance(out_dtype, tuple) else 1) > 2:
        tm = _tile(m, MATMUL_TILE // 2)
    if not extras and k % (2 * WIDE_TILE) == 0:
        tk = 2 * WIDE_TILE
    nk = k // tk

    out_dtypes = out_dtype if isinstance(out_dtype, tuple) else (out_dtype,)
    n_extra = len(extras)

    def body(*refs):
        a_ref, b_ref = refs[:2]
        extra_refs = refs[2:2 + n_extra]
        out_refs = refs[2 + n_extra + (token is not None):-1]
        acc_ref = refs[-1]
        kk = pl.program_id(2)

        @pl.when(kk == 0)
        def _():
            acc_ref[...] = jnp.zeros_like(acc_ref)

        acc_ref[...] += lax.dot_general(a_ref[...], b_ref[...], dims, preferred_element_type=F32)

        @pl.when(kk == nk - 1)
        def _():
            r = acc_ref[...]
            if epilogue is not None:
                r = epilogue(r, *[e[...] for e in extra_refs])
            for o_ref, val in zip(out_refs, r if isinstance(r, tuple) else (r,)):
                o_ref[...] = val.astype(o_ref.dtype)

    if dims is TN:
        a_spec = pl.BlockSpec((tk, tm), lambda i, j, kk: (kk, i))
    else:
        a_spec = pl.BlockSpec((tm, tk), lambda i, j, kk: (i, kk))
    if dims is NT:
        b_spec = pl.BlockSpec((tn, tk), lambda i, j, kk: (j, kk))
    else:
        b_spec = pl.BlockSpec((tk, tn), lambda i, j, kk: (kk, j))
    if not out_chunks:
        o_spec = pl.BlockSpec((tm, tn), lambda i, j, kk: (i, j))
        o_shape = (m, n)
    else:
        pi = m // 8 // tm
        o_spec = pl.BlockSpec((None, None, tm, tn), lambda i, j, kk: ((i // pi) % 2, i // (2 * pi), i % pi, j))
        o_shape = (2, 4, m // 8, n)
    assert not (extras and out_chunks)
    tokens = [] if token is None else [token]
    out = pl.pallas_call(
        body, name=name, grid=(m // tm, n // tn, nk),
        in_specs=[a_spec, b_spec] + [o_spec] * n_extra + [pl.BlockSpec(memory_space=pl.ANY)] * len(tokens),
        out_specs=[o_spec] * len(out_dtypes),
        out_shape=[jax.ShapeDtypeStruct(o_shape, dt) for dt in out_dtypes],
        scratch_shapes=[pltpu.VMEM((tm, tn), F32)],
        compiler_params=_params(("parallel", "parallel", "arbitrary")),
    )(a, b, *extras, *tokens)
    return tuple(out) if isinstance(out_dtype, tuple) else out[0]


def _row_call(body, name, rows, tr, row_ins, par_ins, row_outs, acc_outs):
    def col(i, cb):
        return (i, cb)

    def whole(i, nd):
        return (0,) * nd

    in_specs = [pl.BlockSpec((tr, w), functools.partial(col, cb=cb)) for (_, w, cb) in row_ins]
    in_specs += [pl.BlockSpec(a.shape, functools.partial(whole, nd=a.ndim)) for a in par_ins]
    out_specs = [pl.BlockSpec((tr, w), lambda i: (i, 0)) for (w, _) in row_outs]
    out_specs += [pl.BlockSpec(s, functools.partial(whole, nd=len(s))) for (s, _) in acc_outs]
    out_shape = [jax.ShapeDtypeStruct((rows, w), dt) for (w, dt) in row_outs]
    out_shape += [jax.ShapeDtypeStruct(s, dt) for (s, dt) in acc_outs]
    return pl.pallas_call(
        body, name=name, grid=(rows // tr,), in_specs=in_specs, out_specs=out_specs, out_shape=out_shape,
        compiler_params=_params(("arbitrary",) if acc_outs else ("parallel",)),
    )(*[a for (a, _, _) in row_ins], *par_ins)


def _accumulate(ref, val):
    i = pl.program_id(0)

    @pl.when(i == 0)
    def _():
        ref[...] = val

    @pl.when(i > 0)
    def _():
        ref[...] += val


def _colsum(v):
    return jnp.sum(v, axis=0, keepdims=True)


def _rms_fwd(x, g):
    r = lax.rsqrt(jnp.mean(x * x, axis=-1, keepdims=True) + EPS)
    return x * r * g


def _rms_bwd(x, g, dy):
    r = lax.rsqrt(jnp.mean(x * x, axis=-1, keepdims=True) + EPS)
    xh = x * r
    dxh = dy * g
    dx = r * (dxh - xh * jnp.mean(dxh * xh, axis=-1, keepdims=True))
    return dx, dy * xh


_GELU_C = math.sqrt(2.0 / math.pi)
_GELU_A = 0.044715


def _gelu(x):
    return 0.5 * x * (1.0 + jnp.tanh(_GELU_C * (x + _GELU_A * (x * x * x))))


def _gelu_and_grad(x):
    t = jnp.tanh(_GELU_C * (x + _GELU_A * (x * x * x)))
    return (0.5 * x * (1.0 + t),
            0.5 * (1.0 + t) + 0.5 * x * (1.0 - t * t) * (_GELU_C * (1.0 + 3.0 * _GELU_A * (x * x))))


def _sigmoid(x):
    return 1.0 / (1.0 + jnp.exp(-x))


def _rope_fwd(t, cos_t, sin_t):
    return t * cos_t + pltpu.roll(t, 2 * ROPE_HALF, 1) * sin_t


def _rope_bwd(dt, cos_t, sin_t):
    return dt * cos_t - pltpu.roll(dt, 2 * ROPE_HALF, 1) * sin_t


def _prenorm(x, g, tr, token):
    def body(x_ref, g_ref, token_ref, o_ref):
        o_ref[...] = _rms_fwd(x_ref[...], g_ref[...]).astype(BF16)

    t, d = x.shape
    return _row_call(body, "prenorm", t, tr, [(x, d, 0)], [g, token], [(d, BF16)], [])[0]


def _qkv_prep(proj, g_q, g_kv, cos_t, sin_t, lay, tr):
    ql, kl = lay["ql"], lay["kl"]

    def body(q_ref, kv_ref, kr_ref, cos_ref, sin_ref, gq_ref, gkv_ref, qn_ref, kvn_ref, kro_ref):
        qn_ref[...] = _rms_fwd(q_ref[...], gq_ref[...]).astype(BF16)
        kvn_ref[...] = _rms_fwd(kv_ref[...], gkv_ref[...]).astype(BF16)
        kro_ref[...] = _rope_fwd(kr_ref[...], cos_ref[...], sin_ref[...]).astype(BF16)

    t = proj.shape[0]
    return _row_call(
        body, "qkv_prep", t, tr,
        [(proj, ql, lay["q_off"] // ql), (proj, kl, lay["kv_off"] // kl), (proj, LANES, lay["kr_off"] // LANES),
         (cos_t, LANES, 0), (sin_t, LANES, 0)],
        [g_q, g_kv], [(ql, BF16), (kl, BF16), (LANES, BF16)], [])


def _q_rope(q, cos_t, sin_t, heads, tr):
    def body(q_ref, cos_ref, sin_ref, o_ref):
        c, s = cos_ref[...], sin_ref[...]
        for h in range(heads):
            lo = h * HEAD_PAD
            o_ref[:, lo:lo + NOPE_DIM] = q_ref[:, lo:lo + NOPE_DIM].astype(BF16)
            o_ref[:, lo + NOPE_DIM:lo + HEAD_PAD] = _rope_fwd(q_ref[:, lo + NOPE_DIM:lo + HEAD_PAD], c, s).astype(BF16)

    t, w = q.shape
    return _row_call(body, "q_rope", t, tr, [(q, w, 0), (cos_t, LANES, 0), (sin_t, LANES, 0)], [], [(w, BF16)], [])[0]


ATTN_SCALE = 1.0 / math.sqrt(NOPE_DIM + ROPE_DIM)
ATTN_EXP2_SCALE = ATTN_SCALE * math.log2(math.e)


def _attn_tile(t, cap):
    tq = cap
    while t % tq:
        tq //= 2
    return tq


def _attn_fwd(q, kv, kr, heads):
    t = q.shape[0]
    tq = _attn_tile(t, 256)

    def body(q_ref, kv_ref, kr_ref, o_ref, lse_ref, kcat):
        @pl.when(pl.program_id(1) == 0)
        def _():
            kcat[:, :NOPE_DIM] = kv_ref[:, :NOPE_DIM]
            kcat[:, NOPE_DIM:] = kr_ref[...]

        s = lax.dot_general(q_ref[...], kcat[...], NT, preferred_element_type=F32)
        m = jnp.max(s, axis=-1, keepdims=True)
        p = jnp.exp2((s - m) * ATTN_EXP2_SCALE)
        l = jnp.sum(p, axis=-1, keepdims=True)
        o_ref[...] = jnp.dot(p.astype(BF16), kv_ref[:, NOPE_DIM:], preferred_element_type=F32) * (1.0 / l)
        lse_ref[...] = jnp.broadcast_to(m * ATTN_EXP2_SCALE + jnp.log(l) * math.log2(math.e), (tq, V_DIM))

    out_spec = pl.BlockSpec((tq, V_DIM), lambda h, i: (i, h))
    out_shape = jax.ShapeDtypeStruct((t, heads * V_DIM), F32)
    return pl.pallas_call(
        body, name="attn_fwd", grid=(heads, t // tq),
        in_specs=[pl.BlockSpec((tq, HEAD_PAD), lambda h, i: (i, h)),
                  pl.BlockSpec((t, HEAD_PAD), lambda h, i: (0, h)),
                  pl.BlockSpec((t, LANES), lambda h, i: (0, 0))],
        out_specs=[out_spec, out_spec], out_shape=[out_shape, out_shape],
        scratch_shapes=[pltpu.VMEM((t, HEAD_PAD), BF16)],
        compiler_params=_params(("arbitrary", "arbitrary")),
    )(q, kv, kr)


def _attn_bwd(q, kv, kr, out, lse, d_out, cos_t, sin_t, heads):
    t = q.shape[0]
    tq = _attn_tile(t, 256)
    nq = t // tq

    def body(q_ref, kv_ref, kr_ref, o_ref, lse_ref, do_ref, cos_ref, sin_ref, dq_ref, dkv_ref, dkr_ref,
             kcat, dk_acc, dv_acc):
        h, i = pl.program_id(0), pl.program_id(1)

        @pl.when(i == 0)
        def _():
            kcat[:, :NOPE_DIM] = kv_ref[:, :NOPE_DIM]
            kcat[:, NOPE_DIM:] = kr_ref[...]
            dk_acc[...] = jnp.zeros_like(dk_acc)
            dv_acc[...] = jnp.zeros_like(dv_acc)

        @pl.when((h == 0) & (i == 0))
        def _():
            dkr_ref[...] = jnp.zeros_like(dkr_ref)

        qb, dob = q_ref[...], do_ref[...]
        row_term = jnp.sum(dob.astype(F32) * o_ref[...], axis=-1, keepdims=True)
        s = lax.dot_general(qb, kcat[...], NT, preferred_element_type=F32)
        dp = lax.dot_general(dob, kv_ref[:, NOPE_DIM:], NT, preferred_element_type=F32)
        p = jnp.exp2(s * ATTN_EXP2_SCALE - lse_ref[:, :1])
        ds = (p * (dp - row_term)).astype(BF16)
        dv_acc[...] += lax.dot_general(p.astype(BF16), dob, TN, preferred_element_type=F32)
        dq = jnp.dot(ds, kcat[...], preferred_element_type=F32) * ATTN_SCALE
        dq_ref[:, :NOPE_DIM] = dq[:, :NOPE_DIM].astype(BF16)
        dq_ref[:, NOPE_DIM:] = _rope_bwd(dq[:, NOPE_DIM:], cos_ref[...], sin_ref[...]).astype(BF16)
        dk_acc[...] += lax.dot_general(ds, qb, TN, preferred_element_type=F32)

        @pl.when(i == nq - 1)
        def _():
            dkv_ref[:, :NOPE_DIM] = (dk_acc[:, :NOPE_DIM] * ATTN_SCALE).astype(BF16)
            dkv_ref[:, NOPE_DIM:] = dv_acc[...].astype(BF16)
            dkr_ref[...] += dk_acc[:, NOPE_DIM:] * ATTN_SCALE

    return pl.pallas_call(
        body, name="attn_bwd", grid=(heads, nq),
        in_specs=[pl.BlockSpec((tq, HEAD_PAD), lambda h, i: (i, h)),
                  pl.BlockSpec((t, HEAD_PAD), lambda h, i: (0, h)),
                  pl.BlockSpec((t, LANES), lambda h, i: (0, 0)),
                  pl.BlockSpec((tq, V_DIM), lambda h, i: (i, h)),
                  pl.BlockSpec((tq, V_DIM), lambda h, i: (i, h)),
                  pl.BlockSpec((tq, V_DIM), lambda h, i: (i, h)),
                  pl.BlockSpec((tq, LANES), lambda h, i: (i, 0)),
                  pl.BlockSpec((tq, LANES), lambda h, i: (i, 0))],
        out_specs=[pl.BlockSpec((tq, HEAD_PAD), lambda h, i: (i, h)),
                   pl.BlockSpec((t, HEAD_PAD), lambda h, i: (0, h)),
                   pl.BlockSpec((t, LANES), lambda h, i: (0, 0))],
        out_shape=[jax.ShapeDtypeStruct((t, heads * HEAD_PAD), BF16),
                   jax.ShapeDtypeStruct((t, heads * HEAD_PAD), BF16),
                   jax.ShapeDtypeStruct((t, LANES), F32)],
        scratch_shapes=[pltpu.VMEM((t, HEAD_PAD), BF16), pltpu.VMEM((t, HEAD_PAD), F32), pltpu.VMEM((t, V_DIM), F32)],
        compiler_params=_params(("arbitrary", "arbitrary")),
    )(q, kv, kr, out, lse, d_out, cos_t, sin_t)


def _layer_norm_parts(x):
    mu = jnp.mean(x, axis=-1, keepdims=True)
    xc = x - mu
    r = lax.rsqrt(jnp.mean(xc * xc, axis=-1, keepdims=True) + EPS)
    return xc * r, r


def _gmlp_fwd(proj, ln_g, ln_b, w_s, b_sb, g_out_norm, lay):
    gw = lay["gw"]
    g_heads = gw // G_HEAD_DIM

    def body(u_ref, v_ref, lng_ref, lnb_ref, ws_ref, bs_ref, gn_ref, o_ref, gate_ref):
        gu = _gelu(u_ref[...])
        vh, _ = _layer_norm_parts(_gelu(v_ref[...]))
        vln = (vh * lng_ref[...] + lnb_ref[...]).astype(BF16)
        for g in range(g_heads):
            cols = slice(g * G_HEAD_DIM, (g + 1) * G_HEAD_DIM)
            s = jnp.dot(ws_ref[g], vln[:, cols], preferred_element_type=F32) + bs_ref[g]
            gate_ref[:, cols] = gu[:, cols] * s
        o_ref[...] = _rms_fwd(gate_ref[...], gn_ref[...]).astype(BF16)

    t = proj.shape[0]
    in_specs = [pl.BlockSpec((CHUNK, gw), lambda i: (i, 0)), pl.BlockSpec((CHUNK, gw), lambda i: (i, 1))]
    pars = [ln_g, ln_b, w_s, b_sb, g_out_norm]
    in_specs += [pl.BlockSpec(a.shape, functools.partial(lambda i, nd: (0,) * nd, nd=a.ndim)) for a in pars]
    return pl.pallas_call(
        body, name="gmlp_fwd", grid=(t // CHUNK,), in_specs=in_specs,
        out_specs=pl.BlockSpec((CHUNK, gw), lambda i: (i, 0)),
        out_shape=jax.ShapeDtypeStruct((t, gw), BF16),
        scratch_shapes=[pltpu.VMEM((CHUNK, gw), F32)],
        compiler_params=_params(("parallel",)),
    )(proj, proj, *pars)


def _gmlp_bwd(proj, d_mixed, ln_g, ln_b, w_s, w_st, b_sb, g_out_norm, lay):
    gw = lay["gw"]
    g_heads = gw // G_HEAD_DIM
    aw_blocks = lay["aw"] // gw

    def body(u_ref, v_ref, dm_ref, lng_ref, lnb_ref, ws_ref, wst_ref, bs_ref, gn_ref,
             du_ref, dv_ref, dgn_ref, dlng_ref, dlnb_ref, dws_ref, dbs_ref, gate_ref, s_ref, dvln_ref):
        i = pl.program_id(0)
        u, v = u_ref[...], v_ref[...]
        (gu, gelu_du), (gv, gelu_dv) = _gelu_and_grad(u), _gelu_and_grad(v)
        vh, r_ln = _layer_norm_parts(gv)
        vln = (vh * lng_ref[...] + lnb_ref[...]).astype(BF16)
        for g in range(g_heads):
            cols = slice(g * G_HEAD_DIM, (g + 1) * G_HEAD_DIM)
            s = jnp.dot(ws_ref[g], vln[:, cols], preferred_element_type=F32) + bs_ref[g]
            s_ref[:, cols] = s
            gate_ref[:, cols] = gu[:, cols] * s
        d_gate, dgn = _rms_bwd(gate_ref[...], gn_ref[...], dm_ref[...])
        _accumulate(dgn_ref, _colsum(dgn))
        du_ref[...] = (d_gate * s_ref[...] * gelu_du).astype(BF16)
        d_s = d_gate * gu
        d_sb = d_s.astype(BF16)
        for g in range(g_heads):
            cols = slice(g * G_HEAD_DIM, (g + 1) * G_HEAD_DIM)
            dw = lax.dot_general(d_sb[:, cols], vln[:, cols], NT, preferred_element_type=F32)

            @pl.when(i == 0)
            def _():
                dws_ref[g] = dw
                dbs_ref[g] = d_s[:, cols]

            @pl.when(i > 0)
            def _():
                dws_ref[g] += dw
                dbs_ref[g] += d_s[:, cols]

            dvln_ref[:, cols] = jnp.dot(wst_ref[g], d_sb[:, cols], preferred_element_type=F32)
        d_vln = dvln_ref[...]
        _accumulate(dlng_ref, _colsum(d_vln * vh))
        _accumulate(dlnb_ref, _colsum(d_vln))
        d_vh = d_vln * lng_ref[...]
        d_gv = r_ln * (d_vh - jnp.mean(d_vh, axis=-1, keepdims=True)
                       - vh * jnp.mean(d_vh * vh, axis=-1, keepdims=True))
        dv_ref[...] = (d_gv * gelu_dv).astype(BF16)

    t = proj.shape[0]
    whole = lambda a: pl.BlockSpec(a.shape, functools.partial(lambda i, nd: (0,) * nd, nd=a.ndim))
    pars = [ln_g, ln_b, w_s, w_st, b_sb, g_out_norm]
    hshape = (g_heads, CHUNK, CHUNK)
    return pl.pallas_call(
        body, name="gmlp_bwd", grid=(t // CHUNK,),
        in_specs=[pl.BlockSpec((CHUNK, gw), lambda i: (i, 0)), pl.BlockSpec((CHUNK, gw), lambda i: (i, 1)),
                  pl.BlockSpec((CHUNK, gw), lambda i: (i, aw_blocks))] + [whole(a) for a in pars],
        out_specs=[pl.BlockSpec((CHUNK, gw), lambda i: (i, 0)), pl.BlockSpec((CHUNK, gw), lambda i: (i, 0)),
                   pl.BlockSpec((1, gw), lambda i: (0, 0)), pl.BlockSpec((1, gw), lambda i: (0, 0)),
                   pl.BlockSpec((1, gw), lambda i: (0, 0)),
                   pl.BlockSpec(hshape, lambda i: (0, 0, 0)), pl.BlockSpec(hshape, lambda i: (0, 0, 0))],
        out_shape=[jax.ShapeDtypeStruct((t, gw), BF16), jax.ShapeDtypeStruct((t, gw), BF16),
                   jax.ShapeDtypeStruct((1, gw), F32), jax.ShapeDtypeStruct((1, gw), F32),
                   jax.ShapeDtypeStruct((1, gw), F32),
                   jax.ShapeDtypeStruct(hshape, F32), jax.ShapeDtypeStruct(hshape, F32)],
        scratch_shapes=[pltpu.VMEM((CHUNK, gw), F32), pltpu.VMEM((CHUNK, gw), F32), pltpu.VMEM((CHUNK, gw), F32)],
        compiler_params=_params(("arbitrary",)),
    )(proj, proj, d_mixed, *pars)


def _spatial_bias_grad(dbs_wide):
    g_heads = dbs_wide.shape[0]

    def body(x_ref, o_ref):
        for g in range(g_heads):
            o_ref[g:g + 1, :] = jnp.sum(x_ref[g].T, axis=0, keepdims=True)

    return pl.pallas_call(
        body, name="spatial_bias_grad", out_shape=jax.ShapeDtypeStruct((g_heads, CHUNK), F32),
        in_specs=[pl.BlockSpec(memory_space=pltpu.VMEM)], out_specs=pl.BlockSpec(memory_space=pltpu.VMEM),
    )(dbs_wide)


def _mix_norm(a_out, gn, g_a, tr, token):
    aw = a_out.shape[1]
    gw = gn.shape[1]

    def body(a_ref, gn_ref, g_ref, token_ref, o_ref):
        o_ref[:, :aw] = _rms_fwd(a_ref[...], g_ref[...]).astype(BF16)
        o_ref[:, aw:] = gn_ref[...]

    t = a_out.shape[0]
    return _row_call(body, "mix_norm", t, tr, [(a_out, aw, 0), (gn, gw, 0)], [g_a, token], [(aw + gw, BF16)], [])[0]


def _mix_norm_bwd(a_out, d_mixed, g_a, tr, token):
    aw = a_out.shape[1]

    def body(a_ref, dm_ref, g_ref, token_ref, da_ref, dg_ref):
        dx, dg = _rms_bwd(a_ref[...], g_ref[...], dm_ref[...])
        da_ref[...] = dx.astype(BF16)
        _accumulate(dg_ref, _colsum(dg))

    t = a_out.shape[0]
    return _row_call(body, "mix_norm_bwd", t, tr, [(a_out, aw, 0), (d_mixed, aw, 0)], [g_a, token],
                     [(aw, BF16)], [((1, aw), F32)])


def _post_mix(x, mix_out, g_pm, g_pf, tr):
    def body(x_ref, mo_ref, gpm_ref, gpf_ref, h_ref, hn_ref):
        h = x_ref[...] + _rms_fwd(mo_ref[...], gpm_ref[...])
        h_ref[...] = h
        hn_ref[...] = _rms_fwd(h, gpf_ref[...]).astype(BF16)

    t, d = x.shape
    return _row_call(body, "post_mix", t, tr, [(x, d, 0), (mix_out, d, 0)], [g_pm, g_pf], [(d, F32), (d, BF16)], [])


def _post_mix_bwd(mix_out, h, dy, d_hn, g_pm, g_pf, tr, token):
    def body(mo_ref, h_ref, dy_ref, dhn_ref, gpm_ref, gpf_ref, token_ref, dh_ref, dmo_ref, dgpf_ref, dgpm_ref):
        dx, dg = _rms_bwd(h_ref[...], gpf_ref[...], dhn_ref[...])
        dh = dy_ref[...] + dx
        dh_ref[...] = dh
        _accumulate(dgpf_ref, _colsum(dg))
        dmo, dg2 = _rms_bwd(mo_ref[...], gpm_ref[...], dh)
        dmo_ref[...] = dmo.astype(BF16)
        _accumulate(dgpm_ref, _colsum(dg2))

    t, d = h.shape
    return _row_call(body, "post_mix_bwd", t, tr, [(mix_out, d, 0), (h, d, 0), (dy, d, 0), (d_hn, d, 0)],
                     [g_pm, g_pf, token], [(d, F32), (d, BF16)], [((1, d), F32), ((1, d), F32)])


def _swiglu(gate, up):
    t, f = gate.shape
    tr, tf = _tile(t, 512), _tile(f, 2048)

    def body(g_ref, u_ref, o_ref):
        g = g_ref[...].astype(F32)
        o_ref[...] = (g * _sigmoid(g) * u_ref[...].astype(F32)).astype(BF16)

    spec = pl.BlockSpec((tr, tf), lambda i, j: (i, j))
    return pl.pallas_call(body, name="swiglu", grid=(t // tr, f // tf), in_specs=[spec, spec], out_specs=spec,
                          out_shape=jax.ShapeDtypeStruct((t, f), BF16),
                          compiler_params=_params(("parallel", "parallel")))(gate, up)


def _swiglu_bwd(gate, up, d_act):
    t, f = gate.shape
    tr, tf = _tile(t, 512), _tile(f, 2048)

    def body(g_ref, u_ref, da_ref, dg_ref, du_ref):
        g, u, da = g_ref[...].astype(F32), u_ref[...].astype(F32), da_ref[...].astype(F32)
        sg = _sigmoid(g)
        du_ref[...] = (da * (g * sg)).astype(BF16)
        dg_ref[...] = (da * u * (sg * (1.0 + g * (1.0 - sg)))).astype(BF16)

    spec = pl.BlockSpec((tr, tf), lambda i, j: (i, j))
    shape = jax.ShapeDtypeStruct((t, f), BF16)
    return pl.pallas_call(body, name="swiglu_bwd", grid=(t // tr, f // tf), in_specs=[spec, spec, spec],
                          out_specs=[spec, spec], out_shape=[shape, shape],
                          compiler_params=_params(("parallel", "parallel")))(gate, up, d_act)


def _loss_head(h, ffn, target, g_po, tr):
    t, d = h.shape

    def body(h_ref, f_ref, t_ref, g_ref, dy_ref, df_ref, dg_ref, loss_ref):
        f = f_ref[...]
        err = h_ref[...] + _rms_fwd(f, g_ref[...]) - t_ref[...]
        dy = err * (1.0 / d)
        dy_ref[...] = dy
        df, dg = _rms_bwd(f, g_ref[...], dy)
        df_ref[...] = df.astype(BF16)
        _accumulate(dg_ref, _colsum(dg))
        sq = jnp.sum(_colsum(err * err), axis=-1, keepdims=True) * (0.5 / d)
        _accumulate(loss_ref, jnp.broadcast_to(sq, (1, LANES)))

    return _row_call(body, "loss_head", t, tr, [(h, d, 0), (ffn, d, 0), (target, d, 0)], [g_po],
                     [(d, F32), (d, BF16)], [((1, d), F32), ((1, LANES), F32)])


def _qkv_bwd(proj, d_qn, d_kvn, d_kr, g_q, g_kv, cos_t, sin_t, lay, tr):
    ql, kl = lay["ql"], lay["kl"]

    def body(q_ref, kv_ref, dqn_ref, dkvn_ref, dkr_ref, cos_ref, sin_ref, gq_ref, gkv_ref,
             dq_ref, dkv_ref, dkt_ref, dgq_ref, dgkv_ref):
        dx, dg = _rms_bwd(q_ref[...], gq_ref[...], dqn_ref[...])
        dq_ref[...] = dx.astype(BF16)
        _accumulate(dgq_ref, _colsum(dg))
        dx, dg = _rms_bwd(kv_ref[...], gkv_ref[...], dkvn_ref[...])
        dkv_ref[...] = dx.astype(BF16)
        _accumulate(dgkv_ref, _colsum(dg))
        dkt_ref[...] = _rope_bwd(dkr_ref[...], cos_ref[...], sin_ref[...]).astype(BF16)

    t = proj.shape[0]
    return _row_call(
        body, "qkv_bwd", t, tr,
        [(proj, ql, lay["q_off"] // ql), (proj, kl, lay["kv_off"] // kl), (d_qn, ql, 0), (d_kvn, kl, 0),
         (d_kr, LANES, 0), (cos_t, LANES, 0), (sin_t, LANES, 0)],
        [g_q, g_kv], [(ql, BF16), (kl, BF16), (LANES, BF16)], [((1, ql), F32), ((1, kl), F32)])


def _prenorm_bwd(x, d_xn, dh, g, tr):
    def body(x_ref, dxn_ref, dh_ref, g_ref, gx_ref, dg_ref):
        dx, dg = _rms_bwd(x_ref[...], g_ref[...], dxn_ref[...])
        gx_ref[...] = dh_ref[...] + dx
        _accumulate(dg_ref, _colsum(dg))

    t, d = x.shape
    return _row_call(body, "prenorm_bwd", t, tr, [(x, d, 0), (d_xn, d, 0), (dh, d, 0)], [g],
                     [(d, F32)], [((1, d), F32)])


def _adam_rows(rows, cols, block_elements=256 * 1024):
    cap = max(8, block_elements // cols // 8 * 8)
    tr = min(rows, cap)
    while rows % tr:
        tr -= 8
    return tr


def _adamw(w, g, m, v, name):
    rows, cols = w.shape
    tr = _adam_rows(rows, cols)

    def body(w_ref, g_ref, m_ref, v_ref, d_ref, mo_ref, vo_ref):
        g = g_ref[...]
        m2 = ADAM_B1 * m_ref[...] + (1.0 - ADAM_B1) * g
        v2 = ADAM_B2 * v_ref[...] + (1.0 - ADAM_B2) * (g * g)
        m_hat = m2 / (1.0 - ADAM_B1 ** ADAM_STEP)
        v_hat = v2 / (1.0 - ADAM_B2 ** ADAM_STEP)
        d_ref[...] = -ADAM_LR * (m_hat / (jnp.sqrt(v_hat) + ADAM_EPS) + ADAM_WD * w_ref[...])
        mo_ref[...] = m2
        vo_ref[...] = v2

    spec = pl.BlockSpec((tr, cols), lambda i: (i, 0))
    shape = jax.ShapeDtypeStruct((rows, cols), F32)
    return pl.pallas_call(body, name=name, grid=(rows // tr,), in_specs=[spec] * 4, out_specs=[spec] * 3,
                          out_shape=[shape] * 3, compiler_params=_params(("parallel",)))(w, g, m, v)


def _adamw_halves(w, g_mine, g_theirs, m, v, name):
    rows, cols = w.shape
    rh = g_mine.shape[0]
    tr = _adam_rows(math.gcd(rows, rh), cols)
    per_half = rh // tr
    my_c = jnp.reshape(lax.axis_index("c"), (1,)).astype(jnp.int32)

    def body(c_ref, w_ref, gm_ref, gt_ref, m_ref, v_ref, g_ref, d_ref, mo_ref, vo_ref):
        mine = (pl.program_id(0) // per_half) == c_ref[0]
        g = jnp.where(mine, gm_ref[...], gt_ref[...])
        m2 = ADAM_B1 * m_ref[...] + (1.0 - ADAM_B1) * g
        v2 = ADAM_B2 * v_ref[...] + (1.0 - ADAM_B2) * (g * g)
        m_hat = m2 / (1.0 - ADAM_B1 ** ADAM_STEP)
        v_hat = v2 / (1.0 - ADAM_B2 ** ADAM_STEP)
        g_ref[...] = g
        d_ref[...] = -ADAM_LR * (m_hat / (jnp.sqrt(v_hat) + ADAM_EPS) + ADAM_WD * w_ref[...])
        mo_ref[...] = m2
        vo_ref[...] = v2

    def half_spec(is_mine):
        def index(i, c_ref):
            used = ((i // per_half) == c_ref[0]) if is_mine else ((i // per_half) != c_ref[0])
            return (jnp.where(used, i % per_half, 0), 0)
        return pl.BlockSpec((tr, cols), index)

    spec = pl.BlockSpec((tr, cols), lambda i, c_ref: (i, 0))
    shape = jax.ShapeDtypeStruct((rows, cols), F32)
    grid_spec = pltpu.PrefetchScalarGridSpec(
        num_scalar_prefetch=1, grid=(rows // tr,),
        in_specs=[spec, half_spec(True), half_spec(False), spec, spec], out_specs=[spec] * 4)
    return pl.pallas_call(body, name=name, grid_spec=grid_spec, out_shape=[shape] * 4,
                          compiler_params=_params(("parallel",)))(my_c, w, g_mine, g_theirs, m, v)


def _pair_add(parts, theirs, name):
    _, n, r, c = parts.shape
    tr = _adam_rows(r, c, 1024 * 1024)
    my_c = jnp.reshape(lax.axis_index("c"), (1,)).astype(jnp.int32)

    def body(c_ref, a_ref, b_ref, o_ref):
        o_ref[0] = (a_ref[0, 0].astype(F32) + b_ref[0].astype(F32)).astype(BF16)

    spec = pl.BlockSpec((1, tr, c), lambda k, i, c_ref: (k, i, 0))
    grid_spec = pltpu.PrefetchScalarGridSpec(
        num_scalar_prefetch=1, grid=(n, r // tr),
        in_specs=[pl.BlockSpec((1, 1, tr, c), lambda k, i, c_ref: (c_ref[0], k, i, 0)), spec], out_specs=spec)
    return pl.pallas_call(body, name=name, grid_spec=grid_spec, out_shape=jax.ShapeDtypeStruct((n, r, c), BF16),
                          compiler_params=_params(("parallel", "parallel")))(my_c, parts, theirs)


def _chip_sum(pair_sums, received, name, token=None):
    _, r, c = pair_sums.shape
    tr = _adam_rows(r, c)
    own = 2 * lax.axis_index("x") + lax.axis_index("y")

    def body(own_ref, p_ref, r0_ref, r1_ref, r2_ref, *rest):
        o_ref = rest[-1]
        acc = p_ref[0].astype(F32) + r0_ref[0].astype(F32)
        acc = acc + r1_ref[0].astype(F32)
        o_ref[...] = acc + r2_ref[0].astype(F32)

    def rspec(j):
        return pl.BlockSpec((1, tr, c), functools.partial(lambda i, own_ref, j: (j, i, 0), j=j))

    extra = [] if token is None else [token]
    grid_spec = pltpu.PrefetchScalarGridSpec(
        num_scalar_prefetch=1, grid=(r // tr,),
        in_specs=[pl.BlockSpec((1, tr, c), lambda i, own_ref: (own_ref[0], i, 0)), rspec(0), rspec(1), rspec(2)]
        + [pl.BlockSpec(memory_space=pl.ANY)] * len(extra),
        out_specs=pl.BlockSpec((tr, c), lambda i, own_ref: (i, 0)))
    return pl.pallas_call(body, name=name, grid_spec=grid_spec, out_shape=jax.ShapeDtypeStruct((r, c), F32),
                          compiler_params=_params(("parallel",)))(
        jnp.reshape(own, (1,)).astype(jnp.int32), pair_sums, received, received, received, *extra)


def _mesh_place():
    x, y, c = lax.axis_index("x"), lax.axis_index("y"), lax.axis_index("c")
    other_chips = [(1 - x, y), (x, 1 - y), (1 - x, 1 - y)]
    return x, y, c, other_chips


def _hbm_specs(n):
    return [pl.BlockSpec(memory_space=pltpu.HBM)] * n


def _sibling_exchange(parts, name):
    n = len(parts)

    def body(*refs):
        ins, outs = refs[:n], refs[n:2 * n]
        send_sems, recv_sems = refs[2 * n:]
        x, y, c, _ = _mesh_place()
        copies = [pltpu.make_async_remote_copy(src_ref=ins[w].at[1 - c], dst_ref=outs[w], send_sem=send_sems.at[w],
                                               recv_sem=recv_sems.at[w], device_id=(x, y, 1 - c), device_id_type=MESH)
                  for w in range(n)]
        for cp in copies:
            cp.start()
        for cp in copies:
            cp.wait()

    return pl.pallas_call(
        body, name=name,
        out_shape=[jax.ShapeDtypeStruct(p.shape[1:], p.dtype) for p in parts],
        in_specs=_hbm_specs(n), out_specs=_hbm_specs(n),
        scratch_shapes=[pltpu.SemaphoreType.DMA((n,)), pltpu.SemaphoreType.DMA((n,))],
    )(*parts)


SEM_SPEC = pl.BlockSpec(memory_space=pltpu.SEMAPHORE)
DATAFLOW_EFFECT = pltpu.SideEffectType.DATAFLOW_SIDE_EFFECTING


def _copies_per_weight(kind):
    return {"gather": 4, "scatter": 3, "sibling": 1, "forward": 3, "swap": 1}[kind]


def _flight_copies(kind, src_refs, land_refs, send_sems, recv_sems, arriving):
    x, y, c, other_chips = _mesh_place()
    own = 2 * x + y
    sibling = (x, y, 1 - c)
    per = _copies_per_weight(kind)
    copies = []
    for w in range(len(src_refs)):
        def remote(src, dst, j, to):
            return pltpu.make_async_remote_copy(src_ref=src, dst_ref=dst, send_sem=send_sems.at[per * w + j],
                                                recv_sem=recv_sems.at[per * w + j], device_id=to, device_id_type=MESH)

        if kind == "sibling":
            copies.append(remote(src_refs[w].at[1 - c], land_refs[w], 0, sibling))
            continue
        if kind == "swap":
            copies.append(remote(src_refs[w], land_refs[w], 0, sibling))
            continue
        for j, chip in enumerate(other_chips):
            theirs = 2 * chip[0] + chip[1]
            if kind == "gather":
                copies.append(remote(src_refs[w].at[c], land_refs[w].at[theirs if arriving else own, c], j, (*chip, c)))
            elif kind == "forward":
                copies.append(remote(src_refs[w].at[theirs, c], src_refs[w].at[theirs, (1 - c) if arriving else c],
                                     j, sibling))
            else:
                copies.append(remote(src_refs[w].at[theirs], land_refs[w].at[j], j, (*chip, c)))
        if kind == "gather":
            copies.append(remote(src_refs[w], land_refs[w].at[own], 3, sibling))
    return copies


def _ici_start(kind, srcs, name, after=None):
    n = len(srcs)
    if kind == "gather":
        lands = [lax.empty((4,) + s.shape, s.dtype) for s in srcs]
    elif kind == "scatter":
        lands = [lax.empty((3,) + s.shape[1:], s.dtype) for s in srcs]
    elif kind == "sibling":
        lands = [lax.empty(s.shape[1:], s.dtype) for s in srcs]
    elif kind == "swap":
        lands = [lax.empty(s.shape, s.dtype) for s in srcs]
    else:
        lands = []
    nb = n + len(lands)
    afters = [] if after is None else [after]

    def body(*refs):
        src_refs, land_refs = refs[:n], refs[n:nb]
        send_sems, recv_sems = refs[nb + len(afters)], refs[nb + len(afters) + 1]
        token = refs[-1]
        for cp in _flight_copies(kind, src_refs, land_refs, send_sems, recv_sems, False):
            cp.start()
        token[...] = jnp.zeros_like(token)

    hbm = lambda a: pltpu.with_memory_space_constraint(a, pltpu.HBM)
    n_sems = _copies_per_weight(kind) * n
    out = pl.pallas_call(
        body, name=name,
        out_shape=(pltpu.SemaphoreType.DMA((n_sems,)), pltpu.SemaphoreType.DMA((n_sems,)),
                   *[pltpu.HBM(a.shape, a.dtype) for a in srcs + lands], jax.ShapeDtypeStruct((8, LANES), F32)),
        in_specs=_hbm_specs(nb) + [pl.BlockSpec(memory_space=pl.ANY)] * len(afters),
        out_specs=(SEM_SPEC, SEM_SPEC, *_hbm_specs(nb), pl.BlockSpec(memory_space=pltpu.VMEM)),
        input_output_aliases={i: 2 + i for i in range(nb)},
        compiler_params=pltpu.CompilerParams(has_side_effects=DATAFLOW_EFFECT),
    )(*[hbm(a) for a in srcs + lands], *afters)
    return out[0], out[1], list(out[2:2 + n]), list(out[2 + n:2 + nb]), out[-1]


def _ici_wait(kind, send_sems, recv_sems, srcs, lands, after, name):
    n = len(srcs)
    nb = n + len(lands)

    def body(*refs):
        src_refs, land_refs = refs[:n], refs[n:nb]
        send_ref, recv_ref = refs[nb], refs[nb + 1]
        for cp in _flight_copies(kind, src_refs, land_refs, send_ref, recv_ref, True):
            cp.wait_send()
            cp.wait_recv()

    out = pl.pallas_call(
        body, name=name, out_shape=tuple(pltpu.HBM(a.shape, a.dtype) for a in srcs + lands),
        in_specs=_hbm_specs(nb) + [SEM_SPEC, SEM_SPEC, pl.BlockSpec(memory_space=pl.ANY)],
        out_specs=tuple(_hbm_specs(nb)), input_output_aliases={i: i for i in range(nb)},
        compiler_params=pltpu.CompilerParams(has_side_effects=DATAFLOW_EFFECT),
    )(*srcs, *lands, send_sems, recv_sems, after)
    return list(out[:n]), list(out[n:])


def _halves_exchange(halves, name):
    n = len(halves)

    def body(*refs):
        ins, outs = refs[:n], refs[n:2 * n]
        send_sems, recv_sems = refs[2 * n:]
        x, y, c, _ = _mesh_place()
        copies = [pltpu.make_async_remote_copy(src_ref=ins[w], dst_ref=outs[w], send_sem=send_sems.at[w],
                                               recv_sem=recv_sems.at[w], device_id=(x, y, 1 - c), device_id_type=MESH)
                  for w in range(n)]
        for cp in copies:
            cp.start()
        for cp in copies:
            cp.wait()

    return pl.pallas_call(
        body, name=name,
        out_shape=[jax.ShapeDtypeStruct(h.shape, h.dtype) for h in halves],
        in_specs=_hbm_specs(n), out_specs=_hbm_specs(n),
        scratch_shapes=[pltpu.SemaphoreType.DMA((n,)), pltpu.SemaphoreType.DMA((n,))],
    )(*halves)


def _small_all_reduce(packed):
    rows = packed.shape[0]

    def body(in_ref, out_ref, gathered, send_sems, recv_sems):
        x, y, c, _ = _mesh_place()
        me = 4 * x + 2 * y + c
        gathered[0] = in_ref[...]
        copies = []
        for rel in range(1, 8):
            to = (x ^ (rel >> 2), y ^ ((rel >> 1) & 1), c ^ (rel & 1))
            cp = pltpu.make_async_remote_copy(src_ref=in_ref, dst_ref=gathered.at[rel], send_sem=send_sems.at[rel - 1],
                                              recv_sem=recv_sems.at[rel - 1], device_id=to, device_id_type=MESH)
            cp.start()
            copies.append(cp)
        for cp in copies:
            cp.wait()
        acc = gathered[me]
        for dev in range(1, 8):
            acc = acc + gathered[dev ^ me]
        out_ref[...] = acc

    return pl.pallas_call(
        body, name="small_all_reduce", out_shape=jax.ShapeDtypeStruct(packed.shape, F32),
        in_specs=[pl.BlockSpec(memory_space=pltpu.VMEM)], out_specs=pl.BlockSpec(memory_space=pltpu.VMEM),
        scratch_shapes=[pltpu.VMEM((8, rows, LANES), F32), pltpu.SemaphoreType.DMA((7,)), pltpu.SemaphoreType.DMA((7,))],
        compiler_params=pltpu.CompilerParams(vmem_limit_bytes=VMEM_LIMIT_BYTES),
    )(packed)


def _layout(w_in, w_uq, w_ukv, v_ln_gain, q_norm, kv_norm):
    heads = 4 * w_uq.shape[-1] // (NOPE_DIM + ROPE_DIM)
    gw = v_ln_gain.shape[-1]
    ql, kl = q_norm.shape[-1], kv_norm.shape[-1]
    lay = dict(heads=heads, gw=gw, ql=ql, kl=kl, aw=heads * V_DIM, u_off=0, v_off=gw, q_off=2 * gw,
               kv_off=2 * gw + ql, kr_off=2 * gw + ql + kl)
    lay["in_pad"] = _round_up(lay["kr_off"] + LANES, 2 * LANES if lay["kr_off"] + LANES <= 2048 else 1024)
    assert lay["q_off"] % ql == 0 and lay["kv_off"] % kl == 0 and lay["aw"] % gw == 0
    assert 4 * w_in.shape[-1] == ql + kl + ROPE_DIM + 2 * gw
    return lay


def _rope_tile(t1, t2, axis=-1):
    z = jnp.zeros_like(t1)
    return jnp.concatenate([t1, z, t2, z], axis=axis)


def _w_in_rows(gathered, shard_rows, lay):
    d = gathered.shape[-1]
    wt = gathered[:, :shard_rows].reshape(4 * shard_rows, d)
    ql, kl, gw = lay["ql"], lay["kl"], lay["gw"]
    q_c, kv_c = wt[:ql], wt[ql:ql + kl]
    r = wt[ql + kl:ql + kl + ROPE_DIM]
    u = wt[ql + kl + ROPE_DIM:ql + kl + ROPE_DIM + gw]
    v = wt[ql + kl + ROPE_DIM + gw:]
    parts = [u, v, q_c, kv_c, _rope_tile(r[:ROPE_HALF], r[ROPE_HALF:], axis=0)]
    pad = lay["in_pad"] - (lay["kr_off"] + LANES)
    if pad:
        parts.append(jnp.zeros((pad, d), wt.dtype))
    return jnp.concatenate(parts, axis=0)


def _w_in_grad_chunks(dwt, shard_rows, padded_rows, lay):
    d = dwt.shape[-1]
    ql, kl, gw = lay["ql"], lay["kl"], lay["gw"]
    ko = lay["kr_off"]
    rows = jnp.concatenate([dwt[lay["q_off"]:lay["q_off"] + ql], dwt[lay["kv_off"]:lay["kv_off"] + kl],
                            dwt[ko:ko + ROPE_HALF], dwt[ko + 2 * ROPE_HALF:ko + 3 * ROPE_HALF],
                            dwt[:gw], dwt[gw:2 * gw]], axis=0).reshape(4, shard_rows, d)
    rows = jnp.pad(rows, ((0, 0), (0, padded_rows - shard_rows), (0, 0)))
    return jnp.transpose(rows.reshape(4, 2, padded_rows // 2, d), (1, 0, 2, 3)).astype(BF16)


def _w_uq_padded(w, heads):
    w3 = w.reshape(w.shape[0], heads, NOPE_DIM + ROPE_DIM)
    t = _rope_tile(w3[..., NOPE_DIM:NOPE_DIM + ROPE_HALF], w3[..., NOPE_DIM + ROPE_HALF:])
    return jnp.concatenate([w3[..., :NOPE_DIM], t], axis=-1).reshape(w.shape[0], heads * HEAD_PAD)


def _w_uq_grad_unpadded(dw, heads):
    d3 = dw.reshape(dw.shape[0], heads, HEAD_PAD)
    return jnp.concatenate([d3[..., :NOPE_DIM], d3[..., NOPE_DIM:NOPE_DIM + ROPE_HALF],
                            d3[..., NOPE_DIM + 2 * ROPE_HALF:NOPE_DIM + 3 * ROPE_HALF]],
                           axis=-1).reshape(dw.shape[0], heads * (NOPE_DIM + ROPE_DIM))


def _cols_gathered(g):
    return jnp.transpose(g, (1, 0, 2)).reshape(g.shape[1], 4 * g.shape[2])


def _chunks_of_cols(grad):
    r, c4 = grad.shape
    return jnp.transpose(grad.reshape(2, r // 2, 4, c4 // 4), (0, 2, 1, 3)).astype(BF16)


SMALL = ["pre_mix_norm", "q_norm", "kv_norm", "v_ln_gain", "v_ln_bias", "w_spatial", "b_spatial", "attn_out_norm",
         "gmlp_out_norm", "post_mix_norm", "pre_ffn_norm", "post_ffn_norm"]
BIG = ["w_in", "w_uq", "w_ukv", "w_out", "w_gate", "w_up", "w_down"]
GATHER_NOW = ["w_in", "w_uq", "w_ukv"]
GATHER_LATER = ["w_out", "w_gate", "w_up", "w_down"]
REDUCE_FFN = ["w_gate", "w_up", "w_down"]
REDUCE_OUT = ["w_out"]
REDUCE_LAST = ["w_in", "w_uq", "w_ukv"]
TRANSPOSED = ("w_in", "w_gate", "w_up")
ORDER = ["pre_mix_norm", "w_in", "q_norm", "kv_norm", "w_uq", "w_ukv", "v_ln_gain", "v_ln_bias", "w_spatial",
         "b_spatial", "attn_out_norm", "gmlp_out_norm", "w_out", "post_mix_norm", "pre_ffn_norm", "w_gate", "w_up",
         "w_down", "post_ffn_norm"]


def _pack(arrays):
    flat = jnp.concatenate([a.reshape(-1) for a in arrays])
    n = flat.shape[0]
    total = _round_up(n, 8 * LANES)
    if total > n:
        flat = jnp.concatenate([flat, jnp.zeros((total - n,), F32)])
    return flat.reshape(total // LANES, LANES)


def _unpack(packed, like):
    flat = packed.reshape(-1)
    out, off = [], 0
    for a in like:
        out.append(flat[off:off + a.size].reshape(a.shape))
        off += a.size
    return out


def kernel(x, positions, pre_mix_norm, w_in, q_norm, kv_norm, w_uq, w_ukv, v_ln_gain, v_ln_bias, w_spatial, b_spatial, attn_out_norm, gmlp_out_norm, w_out, post_mix_norm, pre_ffn_norm, w_gate, w_up, w_down, post_ffn_norm, loss_target, m_pre_mix_norm, m_w_in, m_q_norm, m_kv_norm, m_w_uq, m_w_ukv, m_v_ln_gain, m_v_ln_bias, m_w_spatial, m_b_spatial, m_attn_out_norm, m_gmlp_out_norm, m_w_out, m_post_mix_norm, m_pre_ffn_norm, m_w_gate, m_w_up, m_w_down, m_post_ffn_norm, v_pre_mix_norm, v_w_in, v_q_norm, v_kv_norm, v_w_uq, v_w_ukv, v_v_ln_gain, v_v_ln_bias, v_w_spatial, v_b_spatial, v_attn_out_norm, v_gmlp_out_norm, v_w_out, v_post_mix_norm, v_pre_ffn_norm, v_w_gate, v_w_up, v_w_down, v_post_ffn_norm):
    args = dict(locals())
    weights = {n: args[n] for n in ORDER}
    m_in = {n: args["m_" + n] for n in ORDER}
    v_in = {n: args["v_" + n] for n in ORDER}

    lay = _layout(w_in, w_uq, w_ukv, v_ln_gain, q_norm, kv_norm)
    heads, gw = lay["heads"], lay["gw"]
    t, d = x.shape[1], x.shape[2]
    tr = 128 if t % 128 == 0 else t
    xs = x.reshape(t, d)
    target = loss_target.reshape(t, d)

    ffs, ins = w_gate.shape[-1], w_in.shape[-1]
    ffp, inp = _round_up(ffs, LANES), _round_up(ins, LANES)
    shards = {n: (jnp.swapaxes(weights[n][0], 0, 1) if n in TRANSPOSED else weights[n][0]).astype(BF16)
              for n in BIG}
    for n, rows in (("w_gate", ffp), ("w_up", ffp), ("w_down", ffp), ("w_in", inp)):
        shards[n] = jnp.pad(shards[n], ((0, rows - shards[n].shape[0]), (0, 0)))
    halved = {n: shards[n].reshape(2, shards[n].shape[0] // 2, shards[n].shape[1]) for n in BIG}
    full = {}

    def pair_sums_of(partial, names, tag):
        from_sibling = _sibling_exchange([partial[n] for n in names], "grads_sibling_exchange_" + tag)
        return [_pair_add(partial[n], r, "pair_add_" + n) for n, r in zip(names, from_sibling)]

    def place(names, lands):
        for n, g in zip(names, lands):
            full[n] = g.reshape((4,) + shards[n].shape)

    flight_0 = _ici_start("gather", [halved[n] for n in GATHER_NOW], "gather_start_0")
    flights, last_token = {}, flight_0[4]
    for n in GATHER_LATER:
        flights[n] = _ici_start("gather", [halved[n]], "gather_start_" + n, after=last_token)
        last_token = flights[n][4]
    _, lands = _ici_wait("gather", *flight_0[:4], last_token, "gather_wait_0")
    pass_now = _ici_start("forward", lands, "forward_start_now")
    place(GATHER_NOW, _ici_wait("forward", *pass_now[:4], pass_now[4], "forward_wait_now")[0])
    wt_in = _w_in_rows(full["w_in"], ins, lay)
    wb_uq = _w_uq_padded(_cols_gathered(full["w_uq"]), heads)
    wb_ukv = _cols_gathered(full["w_ukv"])

    inv_freq = 1.0 / (ROPE_THETA ** (jnp.arange(0, ROPE_DIM, 2, dtype=F32) / ROPE_DIM))
    ang = positions.reshape(t).astype(F32)[:, None] * inv_freq
    cos, sin = jnp.cos(ang), jnp.sin(ang)
    cos_t = _rope_tile(cos, cos)
    sin_t = _rope_tile(-sin, sin)

    row = lambda a: a.reshape(1, -1)
    g_pre, g_q, g_kv = row(pre_mix_norm), row(q_norm), row(kv_norm)
    g_a, g_g, g_pm = row(attn_out_norm), row(gmlp_out_norm), row(post_mix_norm)
    g_pf, g_po = row(pre_ffn_norm), row(post_ffn_norm)
    ln_g, ln_b = row(v_ln_gain), row(v_ln_bias)
    ws = w_spatial[0].astype(BF16)
    ws_t = jnp.transpose(ws, (0, 2, 1))
    bs_wide = jnp.broadcast_to(b_spatial[0][:, :, None], b_spatial.shape[1:] + (G_HEAD_DIM,))

    xn = _prenorm(xs, g_pre, tr, last_token)
    proj = _matmul(xn, wt_in, NT, F32, "proj")
    qn, kvn, kr = _qkv_prep(proj, g_q, g_kv, cos_t, sin_t, lay, tr)
    q = _q_rope(_matmul(qn, wb_uq, NN, F32, "q_up"), cos_t, sin_t, heads, tr)
    kv = _matmul(kvn, wb_ukv, NN, BF16, "kv_up")
    a_out, a_lse = _attn_fwd(q, kv, kr, heads)
    def arrive(n, after):
        _, lands = _ici_wait("gather", *flights[n][:4], after, "gather_wait_" + n)
        return _ici_start("forward", lands, "forward_start_" + n)

    def settle(n, passing, after):
        place([n], _ici_wait("forward", *passing[:4], after, "forward_wait_" + n)[0])

    pass_out = arrive("w_out", a_out)
    gn = _gmlp_fwd(proj, ln_g, ln_b, ws, bs_wide, g_g, lay)
    mixed = _mix_norm(a_out, gn, g_a, tr, pass_out[4])
    settle("w_out", pass_out, mixed)
    pass_gate = arrive("w_gate", mixed)
    wb_out = full["w_out"].reshape(-1, d)
    mix_out = _matmul(mixed, wb_out, NN, F32, "mix_out", token=pass_gate[4])
    h, hn = _post_mix(xs, mix_out, g_pm, g_pf, tr)
    settle("w_gate", pass_gate, hn)
    pass_up = arrive("w_up", hn)
    wt_gate = full["w_gate"].reshape(4 * ffp, d)
    gate = _matmul(hn, wt_gate, NT, BF16, "ffn_gate", token=pass_up[4])
    settle("w_up", pass_up, gate)
    pass_down = arrive("w_down", gate)
    wt_up = full["w_up"].reshape(4 * ffp, d)
    up = _matmul(hn, wt_up, NT, BF16, "ffn_up", token=pass_down[4])
    act = _swiglu(gate, up)
    settle("w_down", pass_down, act)
    wb_down = full["w_down"].reshape(4 * ffp, d)
    ffn = _matmul(act, wb_down, NN, F32, "ffn_down")
    dy, d_ffn, dg_po, loss_vec = _loss_head(h, ffn, target, g_po, tr)

    d_act = _matmul(d_ffn, wb_down, NT, BF16, "d_act")
    d_gate, d_up = _swiglu_bwd(gate, up, d_act)
    partial_ffn = [_matmul(d_gate, hn, TN, BF16, "gw_gate", out_chunks=True),
                   _matmul(d_up, hn, TN, BF16, "gw_up", out_chunks=True),
                   _matmul(act, d_ffn, TN, BF16, "gw_down", out_chunks=True)]
    swap_ffn = _ici_start("sibling", partial_ffn, "sibling_start_ffn")
    d_hn = _matmul(d_up, wt_up, NN, F32, "d_hn",
                   extras=[_matmul(d_gate, wt_gate, NN, F32, "d_hn_gate", token=swap_ffn[4])],
                   epilogue=lambda acc, partial: acc + partial)
    partial_ffn, from_sibling = _ici_wait("sibling", *swap_ffn[:4], d_hn, "sibling_wait_ffn")
    pair_ffn = [_pair_add(p, r, "pair_add_" + n) for n, p, r in zip(REDUCE_FFN, partial_ffn, from_sibling)]
    flight_ffn = _ici_start("scatter", pair_ffn, "scatter_start_ffn")
    dh, d_mo, dg_pf, dg_pm = _post_mix_bwd(mix_out, h, dy, d_hn, g_pm, g_pf, tr, flight_ffn[4])
    d_mixed = _matmul(d_mo, wb_out, NT, F32, "d_mixed")
    gw_out = _matmul(mixed, d_mo, TN, BF16, "gw_out", out_chunks=True)
    pair_out = pair_sums_of({"w_out": gw_out}, REDUCE_OUT, "out")
    flight_out = _ici_start("scatter", pair_out, "scatter_start_out")
    d_a, dg_a = _mix_norm_bwd(a_out, d_mixed, g_a, tr, flight_out[4])
    d_u, d_v, dg_g, d_ln_g, d_ln_b, d_ws, d_bs_wide = _gmlp_bwd(proj, d_mixed, ln_g, ln_b, ws, ws_t, bs_wide, g_g, lay)
    d_bs = _spatial_bias_grad(d_bs_wide)
    d_q, d_kv, d_kr = _attn_bwd(q, kv, kr, a_out, a_lse, d_a, cos_t, sin_t, heads)
    d_qn = _matmul(d_q, wb_uq, NT, F32, "d_qn")
    gw_uq = _matmul(qn, d_q, TN, F32, "gw_uq")
    d_kvn = _matmul(d_kv, wb_ukv, NT, F32, "d_kvn")
    gw_ukv = _matmul(kvn, d_kv, TN, F32, "gw_ukv")
    d_qc, d_kvc, d_krt, dg_q, dg_kv = _qkv_bwd(proj, d_qn, d_kvn, d_kr, g_q, g_kv, cos_t, sin_t, lay, tr)
    parts = [d_u, d_v, d_qc, d_kvc, d_krt]
    pad = lay["in_pad"] - (lay["kr_off"] + LANES)
    if pad:
        parts.append(jnp.zeros((t, pad), BF16))
    d_proj = jnp.concatenate(parts, axis=1)
    d_xn = _matmul(d_proj, wt_in, NN, F32, "d_xn")
    gw_in = _matmul(d_proj, xn, TN, F32, "gw_in")
    grad_x, dg_pre = _prenorm_bwd(xs, d_xn, dh, g_pre, tr)

    pair_mix = pair_sums_of({"w_in": _w_in_grad_chunks(gw_in, ins, inp, lay),
                             "w_uq": _chunks_of_cols(_w_uq_grad_unpadded(gw_uq, heads)),
                             "w_ukv": _chunks_of_cols(gw_ukv)}, REDUCE_LAST, "mix")
    pair_ffn, received_ffn = _ici_wait("scatter", *flight_ffn[:4], pair_mix[-1], "scatter_wait_ffn")
    pair_out, received_out = _ici_wait("scatter", *flight_out[:4], pair_mix[-1], "scatter_wait_out")
    flight_mix = _ici_start("scatter", pair_mix, "scatter_start_mix")
    grads, delta, new_m, new_v = {}, {}, {}, {}

    def finish(names, pair_sums, received, tag, token, in_flight):
        mine, swaps = [], []
        for n, p, r in zip(names, pair_sums, received):
            mine.append(_chip_sum(p, r, "chip_sum_" + n, token))
            if in_flight:
                swaps.append(_ici_start("swap", [mine[-1]], "swap_start_" + n))
                token = swaps[-1][4]
        if not in_flight:
            theirs = _halves_exchange(mine, "grads_halves_exchange_" + tag)
        previous = mine[-1]
        for i, n in enumerate(names):
            if in_flight:
                (g_mine,), (g_theirs,) = _ici_wait("swap", *swaps[i][:4], previous, "swap_wait_" + n)
            else:
                g_mine, g_theirs = mine[i], theirs[i]
            shape = weights[n].shape
            if n in TRANSPOSED:
                view = lambda a: jnp.swapaxes(a[0], 0, 1)
                back = lambda o: jnp.swapaxes(o, 0, 1).reshape(shape)
            else:
                view = lambda a: a[0]
                back = lambda o: o.reshape(shape)
            out = _adamw_halves(view(weights[n]), g_mine, g_theirs, view(m_in[n]), view(v_in[n]), "adamw_" + n)
            grads[n], delta[n], new_m[n], new_v[n] = [back(o) for o in out]
            previous = out[-1]

    finish(REDUCE_FFN + REDUCE_OUT, pair_ffn + pair_out, received_ffn + received_out, "ffn", flight_mix[4], True)
    pair_mix, received = _ici_wait("scatter", *flight_mix[:4], new_v[REDUCE_OUT[-1]], "scatter_wait_mix")
    finish(REDUCE_LAST, pair_mix, received, "mix", None, False)

    small_grads = {"pre_mix_norm": dg_pre, "q_norm": dg_q, "kv_norm": dg_kv, "v_ln_gain": d_ln_g, "v_ln_bias": d_ln_b,
                   "w_spatial": d_ws, "b_spatial": d_bs, "attn_out_norm": dg_a, "gmlp_out_norm": dg_g,
                   "post_mix_norm": dg_pm, "pre_ffn_norm": dg_pf, "post_ffn_norm": dg_po}
    like = [weights[n] for n in SMALL]
    reduced = _small_all_reduce(_pack([small_grads[n] for n in SMALL] + [loss_vec]))
    loss = reduced.reshape(-1)[sum(a.size for a in like)]
    small_g = _pack(_unpack(reduced, like))
    s_delta, s_m, s_v = _adamw(_pack(like), small_g, _pack([m_in[n] for n in SMALL]),
                               _pack([v_in[n] for n in SMALL]), "adamw_small")
    for n, g in zip(SMALL, _unpack(small_g, like)):
        grads[n] = g
    delta.update(zip(SMALL, _unpack(s_delta, like)))
    new_m.update(zip(SMALL, _unpack(s_m, like)))
    new_v.update(zip(SMALL, _unpack(s_v, like)))

    return (loss, grad_x.reshape(x.shape), *[grads[n] for n in ORDER], *[delta[n] for n in ORDER],
            *[new_m[n] for n in ORDER], *[new_v[n] for n in ORDER])
```

```python
import functools
import math

import jax
import jax.numpy as jnp
from jax import lax
from jax.experimental import pallas as pl
from jax.experimental.pallas import tpu as pltpu

F32 = jnp.float32
BF16 = jnp.bfloat16
MESH = pl.DeviceIdType.MESH

NOPE_DIM = 128
ROPE_DIM = 64
ROPE_HALF = ROPE_DIM // 2
V_DIM = 128
HEAD_PAD = 256
G_HEAD_DIM = 128
CHUNK = 128
ROPE_THETA = 10000.0
EPS = 1e-6
ADAM_LR = 0.001
ADAM_B1 = 0.9
ADAM_B2 = 0.999
ADAM_EPS = 1e-08
ADAM_WD = 0.01
ADAM_STEP = 10

LANES = 128
MATMUL_TILE = 1024
WIDE_TILE = 1408
VMEM_LIMIT_BYTES = 48 * 1024 * 1024

NN = (((1,), (0,)), ((), ()))
NT = (((1,), (1,)), ((), ()))
TN = (((0,), (0,)), ((), ()))


def _params(semantics):
    return pltpu.CompilerParams(dimension_semantics=semantics, vmem_limit_bytes=VMEM_LIMIT_BYTES)


def _tile(n, cap=MATMUL_TILE):
    if n <= cap:
        return n
    if cap == MATMUL_TILE and n % WIDE_TILE == 0:
        return WIDE_TILE
    t = cap - cap % LANES
    while n % t:
        t -= LANES
    assert t > 0, n
    return t


def _round_up(n, m):
    return (n + m - 1) // m * m


def _matmul(a, b, dims, out_dtype, name, extras=(), epilogue=None, out_chunks=None, token=None):
    if dims is NN:
        (m, k), (k2, n) = a.shape, b.shape
    elif dims is NT:
        (m, k), (n, k2) = a.shape, b.shape
    else:
        (k, m), (k2, n) = a.shape, b.shape
    assert k == k2, (a.shape, b.shape, name)
    tm, tn, tk = _tile(m // 8 if out_chunks else m), _tile(n), _tile(k, 2 * MATMUL_TILE)
    if len(extras) + (len(out_dtype) if isinstance(out_dtype, tuple) else 1) > 2:
        tm = _tile(m, MATMUL_TILE // 2)
    if not extras and k % (2 * WIDE_TILE) == 0:
        tk = 2 * WIDE_TILE
    nk = k // tk

    out_dtypes = out_dtype if isinstance(out_dtype, tuple) else (out_dtype,)
    n_extra = len(extras)

    def body(*refs):
        a_ref, b_ref = refs[:2]
        extra_refs = refs[2:2 + n_extra]
        out_refs = refs[2 + n_extra + (token is not None):-1]
        acc_ref = refs[-1]
        kk = pl.program_id(2)

        @pl.when(kk == 0)
        def _():
            acc_ref[...] = jnp.zeros_like(acc_ref)

        acc_ref[...] += lax.dot_general(a_ref[...], b_ref[...], dims, preferred_element_type=F32)

        @pl.when(kk == nk - 1)
        def _():
            r = acc_ref[...]
            if epilogue is not None:
                r = epilogue(r, *[e[...] for e in extra_refs])
            for o_ref, val in zip(out_refs, r if isinstance(r, tuple) else (r,)):
                o_ref[...] = val.astype(o_ref.dtype)

    if dims is TN:
        a_spec = pl.BlockSpec((tk, tm), lambda i, j, kk: (kk, i))
    else:
        a_spec = pl.BlockSpec((tm, tk), lambda i, j, kk: (i, kk))
    if dims is NT:
        b_spec = pl.BlockSpec((tn, tk), lambda i, j, kk: (j, kk))
    else:
        b_spec = pl.BlockSpec((tk, tn), lambda i, j, kk: (kk, j))
    if not out_chunks:
        o_spec = pl.BlockSpec((tm, tn), lambda i, j, kk: (i, j))
        o_shape = (m, n)
    else:
        pi = m // 8 // tm
        o_spec = pl.BlockSpec((None, None, tm, tn), lambda i, j, kk: ((i // pi) % 2, i // (2 * pi), i % pi, j))
        o_shape = (2, 4, m // 8, n)
    assert not (extras and out_chunks)
    tokens = [] if token is None else [token]
    out = pl.pallas_call(
        body, name=name, grid=(m // tm, n // tn, nk),
        in_specs=[a_spec, b_spec] + [o_spec] * n_extra + [pl.BlockSpec(memory_space=pl.ANY)] * len(tokens),
        out_specs=[o_spec] * len(out_dtypes),
        out_shape=[jax.ShapeDtypeStruct(o_shape, dt) for dt in out_dtypes],
        scratch_shapes=[pltpu.VMEM((tm, tn), F32)],
        compiler_params=_params(("parallel", "parallel", "arbitrary")),
    )(a, b, *extras, *tokens)
    return tuple(out) if isinstance(out_dtype, tuple) else out[0]


def _row_call(body, name, rows, tr, row_ins, par_ins, row_outs, acc_outs):
    def col(i, cb):
        return (i, cb)

    def whole(i, nd):
        return (0,) * nd

    in_specs = [pl.BlockSpec((tr, w), functools.partial(col, cb=cb)) for (_, w, cb) in row_ins]
    in_specs += [pl.BlockSpec(a.shape, functools.partial(whole, nd=a.ndim)) for a in par_ins]
    out_specs = [pl.BlockSpec((tr, w), lambda i: (i, 0)) for (w, _) in row_outs]
    out_specs += [pl.BlockSpec(s, functools.partial(whole, nd=len(s))) for (s, _) in acc_outs]
    out_shape = [jax.ShapeDtypeStruct((rows, w), dt) for (w, dt) in row_outs]
    out_shape += [jax.ShapeDtypeStruct(s, dt) for (s, dt) in acc_outs]
    return pl.pallas_call(
        body, name=name, grid=(rows // tr,), in_specs=in_specs, out_specs=out_specs, out_shape=out_shape,
        compiler_params=_params(("arbitrary",) if acc_outs else ("parallel",)),
    )(*[a for (a, _, _) in row_ins], *par_ins)


def _accumulate(ref, val):
    i = pl.program_id(0)

    @pl.when(i == 0)
    def _():
        ref[...] = val

    @pl.when(i > 0)
    def _():
        ref[...] += val


def _colsum(v):
    return jnp.sum(v, axis=0, keepdims=True)


def _rms_fwd(x, g):
    r = lax.rsqrt(jnp.mean(x * x, axis=-1, keepdims=True) + EPS)
    return x * r * g


def _rms_bwd(x, g, dy):
    r = lax.rsqrt(jnp.mean(x * x, axis=-1, keepdims=True) + EPS)
    xh = x * r
    dxh = dy * g
    dx = r * (dxh - xh * jnp.mean(dxh * xh, axis=-1, keepdims=True))
    return dx, dy * xh


_GELU_C = math.sqrt(2.0 / math.pi)
_GELU_A = 0.044715


def _gelu(x):
    return 0.5 * x * (1.0 + jnp.tanh(_GELU_C * (x + _GELU_A * (x * x * x))))


def _gelu_and_grad(x):
    t = jnp.tanh(_GELU_C * (x + _GELU_A * (x * x * x)))
    return (0.5 * x * (1.0 + t),
            0.5 * (1.0 + t) + 0.5 * x * (1.0 - t * t) * (_GELU_C * (1.0 + 3.0 * _GELU_A * (x * x))))


def _sigmoid(x):
    return 1.0 / (1.0 + jnp.exp(-x))


def _rope_fwd(t, cos_t, sin_t):
    return t * cos_t + pltpu.roll(t, 2 * ROPE_HALF, 1) * sin_t


def _rope_bwd(dt, cos_t, sin_t):
    return dt * cos_t - pltpu.roll(dt, 2 * ROPE_HALF, 1) * sin_t


def _prenorm(x, g, tr, token):
    def body(x_ref, g_ref, token_ref, o_ref):
        o_ref[...] = _rms_fwd(x_ref[...], g_ref[...]).astype(BF16)

    t, d = x.shape
    return _row_call(body, "prenorm", t, tr, [(x, d, 0)], [g, token], [(d, BF16)], [])[0]


def _qkv_prep(proj, g_q, g_kv, cos_t, sin_t, lay, tr):
    ql, kl = lay["ql"], lay["kl"]

    def body(q_ref, kv_ref, kr_ref, cos_ref, sin_ref, gq_ref, gkv_ref, qn_ref, kvn_ref, kro_ref):
        qn_ref[...] = _rms_fwd(q_ref[...], gq_ref[...]).astype(BF16)
        kvn_ref[...] = _rms_fwd(kv_ref[...], gkv_ref[...]).astype(BF16)
        kro_ref[...] = _rope_fwd(kr_ref[...], cos_ref[...], sin_ref[...]).astype(BF16)

    t = proj.shape[0]
    return _row_call(
        body, "qkv_prep", t, tr,
        [(proj, ql, lay["q_off"] // ql), (proj, kl, lay["kv_off"] // kl), (proj, LANES, lay["kr_off"] // LANES),
         (cos_t, LANES, 0), (sin_t, LANES, 0)],
        [g_q, g_kv], [(ql, BF16), (kl, BF16), (LANES, BF16)], [])


def _q_rope(q, cos_t, sin_t, heads, tr):
    def body(q_ref, cos_ref, sin_ref, o_ref):
        c, s = cos_ref[...], sin_ref[...]
        for h in range(heads):
            lo = h * HEAD_PAD
            o_ref[:, lo:lo + NOPE_DIM] = q_ref[:, lo:lo + NOPE_DIM].astype(BF16)
            o_ref[:, lo + NOPE_DIM:lo + HEAD_PAD] = _rope_fwd(q_ref[:, lo + NOPE_DIM:lo + HEAD_PAD], c, s).astype(BF16)

    t, w = q.shape
    return _row_call(body, "q_rope", t, tr, [(q, w, 0), (cos_t, LANES, 0), (sin_t, LANES, 0)], [], [(w, BF16)], [])[0]


ATTN_SCALE = 1.0 / math.sqrt(NOPE_DIM + ROPE_DIM)
ATTN_EXP2_SCALE = ATTN_SCALE * math.log2(math.e)


def _attn_tile(t, cap):
    tq = cap
    while t % tq:
        tq //= 2
    return tq


def _attn_fwd(q, kv, kr, heads):
    t = q.shape[0]
    tq = _attn_tile(t, 256)

    def body(q_ref, kv_ref, kr_ref, o_ref, lse_ref, kcat):
        @pl.when(pl.program_id(1) == 0)
        def _():
            kcat[:, :NOPE_DIM] = kv_ref[:, :NOPE_DIM]
            kcat[:, NOPE_DIM:] = kr_ref[...]

        s = lax.dot_general(q_ref[...], kcat[...], NT, preferred_element_type=F32)
        m = jnp.max(s, axis=-1, keepdims=True)
        p = jnp.exp2((s - m) * ATTN_EXP2_SCALE)
        l = jnp.sum(p, axis=-1, keepdims=True)
        o_ref[...] = jnp.dot(p.astype(BF16), kv_ref[:, NOPE_DIM:], preferred_element_type=F32) * (1.0 / l)
        lse_ref[...] = jnp.broadcast_to(m * ATTN_EXP2_SCALE + jnp.log(l) * math.log2(math.e), (tq, V_DIM))

    out_spec = pl.BlockSpec((tq, V_DIM), lambda h, i: (i, h))
    out_shape = jax.ShapeDtypeStruct((t, heads * V_DIM), F32)
    return pl.pallas_call(
        body, name="attn_fwd", grid=(heads, t // tq),
        in_specs=[pl.BlockSpec((tq, HEAD_PAD), lambda h, i: (i, h)),
                  pl.BlockSpec((t, HEAD_PAD), lambda h, i: (0, h)),
                  pl.BlockSpec((t, LANES), lambda h, i: (0, 0))],
        out_specs=[out_spec, out_spec], out_shape=[out_shape, out_shape],
        scratch_shapes=[pltpu.VMEM((t, HEAD_PAD), BF16)],
        compiler_params=_params(("arbitrary", "arbitrary")),
    )(q, kv, kr)


def _attn_bwd(q, kv, kr, out, lse, d_out, cos_t, sin_t, heads):
    t = q.shape[0]
    tq = _attn_tile(t, 256)
    nq = t // tq

    def body(q_ref, kv_ref, kr_ref, o_ref, lse_ref, do_ref, cos_ref, sin_ref, dq_ref, dkv_ref, dkr_ref,
             kcat, dk_acc, dv_acc):
        h, i = pl.program_id(0), pl.program_id(1)

        @pl.when(i == 0)
        def _():
            kcat[:, :NOPE_DIM] = kv_ref[:, :NOPE_DIM]
            kcat[:, NOPE_DIM:] = kr_ref[...]
            dk_acc[...] = jnp.zeros_like(dk_acc)
            dv_acc[...] = jnp.zeros_like(dv_acc)

        @pl.when((h == 0) & (i == 0))
        def _():
            dkr_ref[...] = jnp.zeros_like(dkr_ref)

        qb, dob = q_ref[...], do_ref[...]
        row_term = jnp.sum(dob.astype(F32) * o_ref[...], axis=-1, keepdims=True)
        s = lax.dot_general(qb, kcat[...], NT, preferred_element_type=F32)
        dp = lax.dot_general(dob, kv_ref[:, NOPE_DIM:], NT, preferred_element_type=F32)
        p = jnp.exp2(s * ATTN_EXP2_SCALE - lse_ref[:, :1])
        ds = (p * (dp - row_term)).astype(BF16)
        dv_acc[...] += lax.dot_general(p.astype(BF16), dob, TN, preferred_element_type=F32)
        dq = jnp.dot(ds, kcat[...], preferred_element_type=F32) * ATTN_SCALE
        dq_ref[:, :NOPE_DIM] = dq[:, :NOPE_DIM].astype(BF16)
        dq_ref[:, NOPE_DIM:] = _rope_bwd(dq[:, NOPE_DIM:], cos_ref[...], sin_ref[...]).astype(BF16)
        dk_acc[...] += lax.dot_general(ds, qb, TN, preferred_element_type=F32)

        @pl.when(i == nq - 1)
        def _():
            dkv_ref[:, :NOPE_DIM] = (dk_acc[:, :NOPE_DIM] * ATTN_SCALE).astype(BF16)
            dkv_ref[:, NOPE_DIM:] = dv_acc[...].astype(BF16)
            dkr_ref[...] += dk_acc[:, NOPE_DIM:] * ATTN_SCALE

    return pl.pallas_call(
        body, name="attn_bwd", grid=(heads, nq),
        in_specs=[pl.BlockSpec((tq, HEAD_PAD), lambda h, i: (i, h)),
                  pl.BlockSpec((t, HEAD_PAD), lambda h, i: (0, h)),
                  pl.BlockSpec((t, LANES), lambda h, i: (0, 0)),
                  pl.BlockSpec((tq, V_DIM), lambda h, i: (i, h)),
                  pl.BlockSpec((tq, V_DIM), lambda h, i: (i, h)),
                  pl.BlockSpec((tq, V_DIM), lambda h, i: (i, h)),
                  pl.BlockSpec((tq, LANES), lambda h, i: (i, 0)),
                  pl.BlockSpec((tq, LANES), lambda h, i: (i, 0))],
        out_specs=[pl.BlockSpec((tq, HEAD_PAD), lambda h, i: (i, h)),
                   pl.BlockSpec((t, HEAD_PAD), lambda h, i: (0, h)),
                   pl.BlockSpec((t, LANES), lambda h, i: (0, 0))],
        out_shape=[jax.ShapeDtypeStruct((t, heads * HEAD_PAD), BF16),
                   jax.ShapeDtypeStruct((t, heads * HEAD_PAD), BF16),
                   jax.ShapeDtypeStruct((t, LANES), F32)],
        scratch_shapes=[pltpu.VMEM((t, HEAD_PAD), BF16), pltpu.VMEM((t, HEAD_PAD), F32), pltpu.VMEM((t, V_DIM), F32)],
        compiler_params=_params(("arbitrary", "arbitrary")),
    )(q, kv, kr, out, lse, d_out, cos_t, sin_t)


def _layer_norm_parts(x):
    mu = jnp.mean(x, axis=-1, keepdims=True)
    xc = x - mu
    r = lax.rsqrt(jnp.mean(xc * xc, axis=-1, keepdims=True) + EPS)
    return xc * r, r


def _gmlp_fwd(proj, ln_g, ln_b, w_s, b_sb, g_out_norm, lay):
    gw = lay["gw"]
    g_heads = gw // G_HEAD_DIM

    def body(u_ref, v_ref, lng_ref, lnb_ref, ws_ref, bs_ref, gn_ref, o_ref, gate_ref):
        gu = _gelu(u_ref[...])
        vh, _ = _layer_norm_parts(_gelu(v_ref[...]))
        vln = (vh * lng_ref[...] + lnb_ref[...]).astype(BF16)
        for g in range(g_heads):
            cols = slice(g * G_HEAD_DIM, (g + 1) * G_HEAD_DIM)
            s = jnp.dot(ws_ref[g], vln[:, cols], preferred_element_type=F32) + bs_ref[g]
            gate_ref[:, cols] = gu[:, cols] * s
        o_ref[...] = _rms_fwd(gate_ref[...], gn_ref[...]).astype(BF16)

    t = proj.shape[0]
    in_specs = [pl.BlockSpec((CHUNK, gw), lambda i: (i, 0)), pl.BlockSpec((CHUNK, gw), lambda i: (i, 1))]
    pars = [ln_g, ln_b, w_s, b_sb, g_out_norm]
    in_specs += [pl.BlockSpec(a.shape, functools.partial(lambda i, nd: (0,) * nd, nd=a.ndim)) for a in pars]
    return pl.pallas_call(
        body, name="gmlp_fwd", grid=(t // CHUNK,), in_specs=in_specs,
        out_specs=pl.BlockSpec((CHUNK, gw), lambda i: (i, 0)),
        out_shape=jax.ShapeDtypeStruct((t, gw), BF16),
        scratch_shapes=[pltpu.VMEM((CHUNK, gw), F32)],
        compiler_params=_params(("parallel",)),
    )(proj, proj, *pars)


def _gmlp_bwd(proj, d_mixed, ln_g, ln_b, w_s, w_st, b_sb, g_out_norm, lay):
    gw = lay["gw"]
    g_heads = gw // G_HEAD_DIM
    aw_blocks = lay["aw"] // gw

    def body(u_ref, v_ref, dm_ref, lng_ref, lnb_ref, ws_ref, wst_ref, bs_ref, gn_ref,
             du_ref, dv_ref, dgn_ref, dlng_ref, dlnb_ref, dws_ref, dbs_ref, gate_ref, s_ref, dvln_ref):
        i = pl.program_id(0)
        u, v = u_ref[...], v_ref[...]
        (gu, gelu_du), (gv, gelu_dv) = _gelu_and_grad(u), _gelu_and_grad(v)
        vh, r_ln = _layer_norm_parts(gv)
        vln = (vh * lng_ref[...] + lnb_ref[...]).astype(BF16)
        for g in range(g_heads):
            cols = slice(g * G_HEAD_DIM, (g + 1) * G_HEAD_DIM)
            s = jnp.dot(ws_ref[g], vln[:, cols], preferred_element_type=F32) + bs_ref[g]
            s_ref[:, cols] = s
            gate_ref[:, cols] = gu[:, cols] * s
        d_gate, dgn = _rms_bwd(gate_ref[...], gn_ref[...], dm_ref[...])
        _accumulate(dgn_ref, _colsum(dgn))
        du_ref[...] = (d_gate * s_ref[...] * gelu_du).astype(BF16)
        d_s = d_gate * gu
        d_sb = d_s.astype(BF16)
        for g in range(g_heads):
            cols = slice(g * G_HEAD_DIM, (g + 1) * G_HEAD_DIM)
            dw = lax.dot_general(d_sb[:, cols], vln[:, cols], NT, preferred_element_type=F32)

            @pl.when(i == 0)
            def _():
                dws_ref[g] = dw
                dbs_ref[g] = d_s[:, cols]

            @pl.when(i > 0)
            def _():
                dws_ref[g] += dw
                dbs_ref[g] += d_s[:, cols]

            dvln_ref[:, cols] = jnp.dot(wst_ref[g], d_sb[:, cols], preferred_element_type=F32)
        d_vln = dvln_ref[...]
        _accumulate(dlng_ref, _colsum(d_vln * vh))
        _accumulate(dlnb_ref, _colsum(d_vln))
        d_vh = d_vln * lng_ref[...]
        d_gv = r_ln * (d_vh - jnp.mean(d_vh, axis=-1, keepdims=True)
                       - vh * jnp.mean(d_vh * vh, axis=-1, keepdims=True))
        dv_ref[...] = (d_gv * gelu_dv).astype(BF16)

    t = proj.shape[0]
    whole = lambda a: pl.BlockSpec(a.shape, functools.partial(lambda i, nd: (0,) * nd, nd=a.ndim))
    pars = [ln_g, ln_b, w_s, w_st, b_sb, g_out_norm]
    hshape = (g_heads, CHUNK, CHUNK)
    return pl.pallas_call(
        body, name="gmlp_bwd", grid=(t // CHUNK,),
        in_specs=[pl.BlockSpec((CHUNK, gw), lambda i: (i, 0)), pl.BlockSpec((CHUNK, gw), lambda i: (i, 1)),
                  pl.BlockSpec((CHUNK, gw), lambda i: (i, aw_blocks))] + [whole(a) for a in pars],
        out_specs=[pl.BlockSpec((CHUNK, gw), lambda i: (i, 0)), pl.BlockSpec((CHUNK, gw), lambda i: (i, 0)),
                   pl.BlockSpec((1, gw), lambda i: (0, 0)), pl.BlockSpec((1, gw), lambda i: (0, 0)),
                   pl.BlockSpec((1, gw), lambda i: (0, 0)),
                   pl.BlockSpec(hshape, lambda i: (0, 0, 0)), pl.BlockSpec(hshape, lambda i: (0, 0, 0))],
        out_shape=[jax.ShapeDtypeStruct((t, gw), BF16), jax.ShapeDtypeStruct((t, gw), BF16),
                   jax.ShapeDtypeStruct((1, gw), F32), jax.ShapeDtypeStruct((1, gw), F32),
                   jax.ShapeDtypeStruct((1, gw), F32),
                   jax.ShapeDtypeStruct(hshape, F32), jax.ShapeDtypeStruct(hshape, F32)],
        scratch_shapes=[pltpu.VMEM((CHUNK, gw), F32), pltpu.VMEM((CHUNK, gw), F32), pltpu.VMEM((CHUNK, gw), F32)],
        compiler_params=_params(("arbitrary",)),
    )(proj, proj, d_mixed, *pars)


def _spatial_bias_grad(dbs_wide):
    g_heads = dbs_wide.shape[0]

    def body(x_ref, o_ref):
        for g in range(g_heads):
            o_ref[g:g + 1, :] = jnp.sum(x_ref[g].T, axis=0, keepdims=True)

    return pl.pallas_call(
        body, name="spatial_bias_grad", out_shape=jax.ShapeDtypeStruct((g_heads, CHUNK), F32),
        in_specs=[pl.BlockSpec(memory_space=pltpu.VMEM)], out_specs=pl.BlockSpec(memory_space=pltpu.VMEM),
    )(dbs_wide)


def _mix_norm(a_out, gn, g_a, tr, token):
    aw = a_out.shape[1]
    gw = gn.shape[1]

    def body(a_ref, gn_ref, g_ref, token_ref, o_ref):
        o_ref[:, :aw] = _rms_fwd(a_ref[...], g_ref[...]).astype(BF16)
        o_ref[:, aw:] = gn_ref[...]

    t = a_out.shape[0]
    return _row_call(body, "mix_norm", t, tr, [(a_out, aw, 0), (gn, gw, 0)], [g_a, token], [(aw + gw, BF16)], [])[0]


def _mix_norm_bwd(a_out, d_mixed, g_a, tr, token):
    aw = a_out.shape[1]

    def body(a_ref, dm_ref, g_ref, token_ref, da_ref, dg_ref):
        dx, dg = _rms_bwd(a_ref[...], g_ref[...], dm_ref[...])
        da_ref[...] = dx.astype(BF16)
        _accumulate(dg_ref, _colsum(dg))

    t = a_out.shape[0]
    return _row_call(body, "mix_norm_bwd", t, tr, [(a_out, aw, 0), (d_mixed, aw, 0)], [g_a, token],
                     [(aw, BF16)], [((1, aw), F32)])


def _post_mix(x, mix_out, g_pm, g_pf, tr):
    def body(x_ref, mo_ref, gpm_ref, gpf_ref, h_ref, hn_ref):
        h = x_ref[...] + _rms_fwd(mo_ref[...], gpm_ref[...])
        h_ref[...] = h
        hn_ref[...] = _rms_fwd(h, gpf_ref[...]).astype(BF16)

    t, d = x.shape
    return _row_call(body, "post_mix", t, tr, [(x, d, 0), (mix_out, d, 0)], [g_pm, g_pf], [(d, F32), (d, BF16)], [])


def _post_mix_bwd(mix_out, h, dy, d_hn, g_pm, g_pf, tr, token):
    def body(mo_ref, h_ref, dy_ref, dhn_ref, gpm_ref, gpf_ref, token_ref, dh_ref, dmo_ref, dgpf_ref, dgpm_ref):
        dx, dg = _rms_bwd(h_ref[...], gpf_ref[...], dhn_ref[...])
        dh = dy_ref[...] + dx
        dh_ref[...] = dh
        _accumulate(dgpf_ref, _colsum(dg))
        dmo, dg2 = _rms_bwd(mo_ref[...], gpm_ref[...], dh)
        dmo_ref[...] = dmo.astype(BF16)
        _accumulate(dgpm_ref, _colsum(dg2))

    t, d = h.shape
    return _row_call(body, "post_mix_bwd", t, tr, [(mix_out, d, 0), (h, d, 0), (dy, d, 0), (d_hn, d, 0)],
                     [g_pm, g_pf, token], [(d, F32), (d, BF16)], [((1, d), F32), ((1, d), F32)])


def _swiglu(gate, up):
    t, f = gate.shape
    tr, tf = _tile(t, 512), _tile(f, 2048)

    def body(g_ref, u_ref, o_ref):
        g = g_ref[...].astype(F32)
        o_ref[...] = (g * _sigmoid(g) * u_ref[...].astype(F32)).astype(BF16)

    spec = pl.BlockSpec((tr, tf), lambda i, j: (i, j))
    return pl.pallas_call(body, name="swiglu", grid=(t // tr, f // tf), in_specs=[spec, spec], out_specs=spec,
                          out_shape=jax.ShapeDtypeStruct((t, f), BF16),
                          compiler_params=_params(("parallel", "parallel")))(gate, up)


def _swiglu_bwd(gate, up, d_act):
    t, f = gate.shape
    tr, tf = _tile(t, 512), _tile(f, 2048)

    def body(g_ref, u_ref, da_ref, dg_ref, du_ref):
        g, u, da = g_ref[...].astype(F32), u_ref[...].astype(F32), da_ref[...].astype(F32)
        sg = _sigmoid(g)
        du_ref[...] = (da * (g * sg)).astype(BF16)
        dg_ref[...] = (da * u * (sg * (1.0 + g * (1.0 - sg)))).astype(BF16)

    spec = pl.BlockSpec((tr, tf), lambda i, j: (i, j))
    shape = jax.ShapeDtypeStruct((t, f), BF16)
    return pl.pallas_call(body, name="swiglu_bwd", grid=(t // tr, f // tf), in_specs=[spec, spec, spec],
                          out_specs=[spec, spec], out_shape=[shape, shape],
                          compiler_params=_params(("parallel", "parallel")))(gate, up, d_act)


def _loss_head(h, ffn, target, g_po, tr):
    t, d = h.shape

    def body(h_ref, f_ref, t_ref, g_ref, dy_ref, df_ref, dg_ref, loss_ref):
        f = f_ref[...]
        err = h_ref[...] + _rms_fwd(f, g_ref[...]) - t_ref[...]
        dy = err * (1.0 / d)
        dy_ref[...] = dy
        df, dg = _rms_bwd(f, g_ref[...], dy)
        df_ref[...] = df.astype(BF16)
        _accumulate(dg_ref, _colsum(dg))
        sq = jnp.sum(_colsum(err * err), axis=-1, keepdims=True) * (0.5 / d)
        _accumulate(loss_ref, jnp.broadcast_to(sq, (1, LANES)))

    return _row_call(body, "loss_head", t, tr, [(h, d, 0), (ffn, d, 0), (target, d, 0)], [g_po],
                     [(d, F32), (d, BF16)], [((1, d), F32), ((1, LANES), F32)])


def _qkv_bwd(proj, d_qn, d_kvn, d_kr, g_q, g_kv, cos_t, sin_t, lay, tr):
    ql, kl = lay["ql"], lay["kl"]

    def body(q_ref, kv_ref, dqn_ref, dkvn_ref, dkr_ref, cos_ref, sin_ref, gq_ref, gkv_ref,
             dq_ref, dkv_ref, dkt_ref, dgq_ref, dgkv_ref):
        dx, dg = _rms_bwd(q_ref[...], gq_ref[...], dqn_ref[...])
        dq_ref[...] = dx.astype(BF16)
        _accumulate(dgq_ref, _colsum(dg))
        dx, dg = _rms_bwd(kv_ref[...], gkv_ref[...], dkvn_ref[...])
        dkv_ref[...] = dx.astype(BF16)
        _accumulate(dgkv_ref, _colsum(dg))
        dkt_ref[...] = _rope_bwd(dkr_ref[...], cos_ref[...], sin_ref[...]).astype(BF16)

    t = proj.shape[0]
    return _row_call(
        body, "qkv_bwd", t, tr,
        [(proj, ql, lay["q_off"] // ql), (proj, kl, lay["kv_off"] // kl), (d_qn, ql, 0), (d_kvn, kl, 0),
         (d_kr, LANES, 0), (cos_t, LANES, 0), (sin_t, LANES, 0)],
        [g_q, g_kv], [(ql, BF16), (kl, BF16), (LANES, BF16)], [((1, ql), F32), ((1, kl), F32)])


def _prenorm_bwd(x, d_xn, dh, g, tr):
    def body(x_ref, dxn_ref, dh_ref, g_ref, gx_ref, dg_ref):
        dx, dg = _rms_bwd(x_ref[...], g_ref[...], dxn_ref[...])
        gx_ref[...] = dh_ref[...] + dx
        _accumulate(dg_ref, _colsum(dg))

    t, d = x.shape
    return _row_call(body, "prenorm_bwd", t, tr, [(x, d, 0), (d_xn, d, 0), (dh, d, 0)], [g],
                     [(d, F32)], [((1, d), F32)])


def _adam_rows(rows, cols, block_elements=256 * 1024):
    cap = max(8, block_elements // cols // 8 * 8)
    tr = min(rows, cap)
    while rows % tr:
        tr -= 8
    return tr


def _adamw(w, g, m, v, name):
    rows, cols = w.shape
    tr = _adam_rows(rows, cols)

    def body(w_ref, g_ref, m_ref, v_ref, d_ref, mo_ref, vo_ref):
        g = g_ref[...]
        m2 = ADAM_B1 * m_ref[...] + (1.0 - ADAM_B1) * g
        v2 = ADAM_B2 * v_ref[...] + (1.0 - ADAM_B2) * (g * g)
        m_hat = m2 / (1.0 - ADAM_B1 ** ADAM_STEP)
        v_hat = v2 / (1.0 - ADAM_B2 ** ADAM_STEP)
        d_ref[...] = -ADAM_LR * (m_hat / (jnp.sqrt(v_hat) + ADAM_EPS) + ADAM_WD * w_ref[...])
        mo_ref[...] = m2
        vo_ref[...] = v2

    spec = pl.BlockSpec((tr, cols), lambda i: (i, 0))
    shape = jax.ShapeDtypeStruct((rows, cols), F32)
    return pl.pallas_call(body, name=name, grid=(rows // tr,), in_specs=[spec] * 4, out_specs=[spec] * 3,
                          out_shape=[shape] * 3, compiler_params=_params(("parallel",)))(w, g, m, v)


def _adamw_halves(w, g_mine, g_theirs, m, v, name):
    rows, cols = w.shape
    rh = g_mine.shape[0]
    tr = _adam_rows(math.gcd(rows, rh), cols)
    per_half = rh // tr
    my_c = jnp.reshape(lax.axis_index("c"), (1,)).astype(jnp.int32)

    def body(c_ref, w_ref, gm_ref, gt_ref, m_ref, v_ref, g_ref, d_ref, mo_ref, vo_ref):
        mine = (pl.program_id(0) // per_half) == c_ref[0]
        g = jnp.where(mine, gm_ref[...], gt_ref[...])
        m2 = ADAM_B1 * m_ref[...] + (1.0 - ADAM_B1) * g
        v2 = ADAM_B2 * v_ref[...] + (1.0 - ADAM_B2) * (g * g)
        m_hat = m2 / (1.0 - ADAM_B1 ** ADAM_STEP)
        v_hat = v2 / (1.0 - ADAM_B2 ** ADAM_STEP)
        g_ref[...] = g
        d_ref[...] = -ADAM_LR * (m_hat / (jnp.sqrt(v_hat) + ADAM_EPS) + ADAM_WD * w_ref[...])
        mo_ref[...] = m2
        vo_ref[...] = v2

    def half_spec(is_mine):
        def index(i, c_ref):
            used = ((i // per_half) == c_ref[0]) if is_mine else ((i // per_half) != c_ref[0])
            return (jnp.where(used, i % per_half, 0), 0)
        return pl.BlockSpec((tr, cols), index)

    spec = pl.BlockSpec((tr, cols), lambda i, c_ref: (i, 0))
    shape = jax.ShapeDtypeStruct((rows, cols), F32)
    grid_spec = pltpu.PrefetchScalarGridSpec(
        num_scalar_prefetch=1, grid=(rows // tr,),
        in_specs=[spec, half_spec(True), half_spec(False), spec, spec], out_specs=[spec] * 4)
    return pl.pallas_call(body, name=name, grid_spec=grid_spec, out_shape=[shape] * 4,
                          compiler_params=_params(("parallel",)))(my_c, w, g_mine, g_theirs, m, v)


def _pair_add(parts, theirs, name):
    _, n, r, c = parts.shape
    tr = _adam_rows(r, c, 1024 * 1024)
    my_c = jnp.reshape(lax.axis_index("c"), (1,)).astype(jnp.int32)

    def body(c_ref, a_ref, b_ref, o_ref):
        o_ref[0] = (a_ref[0, 0].astype(F32) + b_ref[0].astype(F32)).astype(BF16)

    spec = pl.BlockSpec((1, tr, c), lambda k, i, c_ref: (k, i, 0))
    grid_spec = pltpu.PrefetchScalarGridSpec(
        num_scalar_prefetch=1, grid=(n, r // tr),
        in_specs=[pl.BlockSpec((1, 1, tr, c), lambda k, i, c_ref: (c_ref[0], k, i, 0)), spec], out_specs=spec)
    return pl.pallas_call(body, name=name, grid_spec=grid_spec, out_shape=jax.ShapeDtypeStruct((n, r, c), BF16),
                          compiler_params=_params(("parallel", "parallel")))(my_c, parts, theirs)


def _chip_sum(pair_sums, received, name, token=None):
    _, r, c = pair_sums.shape
    tr = _adam_rows(r, c)
    own = 2 * lax.axis_index("x") + lax.axis_index("y")

    def body(own_ref, p_ref, r0_ref, r1_ref, r2_ref, *rest):
        o_ref = rest[-1]
        acc = p_ref[0].astype(F32) + r0_ref[0].astype(F32)
        acc = acc + r1_ref[0].astype(F32)
        o_ref[...] = acc + r2_ref[0].astype(F32)

    def rspec(j):
        return pl.BlockSpec((1, tr, c), functools.partial(lambda i, own_ref, j: (j, i, 0), j=j))

    extra = [] if token is None else [token]
    grid_spec = pltpu.PrefetchScalarGridSpec(
        num_scalar_prefetch=1, grid=(r // tr,),
        in_specs=[pl.BlockSpec((1, tr, c), lambda i, own_ref: (own_ref[0], i, 0)), rspec(0), rspec(1), rspec(2)]
        + [pl.BlockSpec(memory_space=pl.ANY)] * len(extra),
        out_specs=pl.BlockSpec((tr, c), lambda i, own_ref: (i, 0)))
    return pl.pallas_call(body, name=name, grid_spec=grid_spec, out_shape=jax.ShapeDtypeStruct((r, c), F32),
                          compiler_params=_params(("parallel",)))(
        jnp.reshape(own, (1,)).astype(jnp.int32), pair_sums, received, received, received, *extra)


def _mesh_place():
    x, y, c = lax.axis_index("x"), lax.axis_index("y"), lax.axis_index("c")
    other_chips = [(1 - x, y), (x, 1 - y), (1 - x, 1 - y)]
    return x, y, c, other_chips


def _hbm_specs(n):
    return [pl.BlockSpec(memory_space=pltpu.HBM)] * n


def _sibling_exchange(parts, name):
    n = len(parts)

    def body(*refs):
        ins, outs = refs[:n], refs[n:2 * n]
        send_sems, recv_sems = refs[2 * n:]
        x, y, c, _ = _mesh_place()
        copies = [pltpu.make_async_remote_copy(src_ref=ins[w].at[1 - c], dst_ref=outs[w], send_sem=send_sems.at[w],
                                               recv_sem=recv_sems.at[w], device_id=(x, y, 1 - c), device_id_type=MESH)
                  for w in range(n)]
        for cp in copies:
            cp.start()
        for cp in copies:
            cp.wait()

    return pl.pallas_call(
        body, name=name,
        out_shape=[jax.ShapeDtypeStruct(p.shape[1:], p.dtype) for p in parts],
        in_specs=_hbm_specs(n), out_specs=_hbm_specs(n),
        scratch_shapes=[pltpu.SemaphoreType.DMA((n,)), pltpu.SemaphoreType.DMA((n,))],
    )(*parts)


SEM_SPEC = pl.BlockSpec(memory_space=pltpu.SEMAPHORE)
DATAFLOW_EFFECT = pltpu.SideEffectType.DATAFLOW_SIDE_EFFECTING


def _copies_per_weight(kind):
    return {"gather": 4, "scatter": 3, "sibling": 1, "forward": 3, "swap": 1}[kind]


def _flight_copies(kind, src_refs, land_refs, send_sems, recv_sems, arriving):
    x, y, c, other_chips = _mesh_place()
    own = 2 * x + y
    sibling = (x, y, 1 - c)
    per = _copies_per_weight(kind)
    copies = []
    for w in range(len(src_refs)):
        def remote(src, dst, j, to):
            return pltpu.make_async_remote_copy(src_ref=src, dst_ref=dst, send_sem=send_sems.at[per * w + j],
                                                recv_sem=recv_sems.at[per * w + j], device_id=to, device_id_type=MESH)

        if kind == "sibling":
            copies.append(remote(src_refs[w].at[1 - c], land_refs[w], 0, sibling))
            continue
        if kind == "swap":
            copies.append(remote(src_refs[w], land_refs[w], 0, sibling))
            continue
        for j, chip in enumerate(other_chips):
            theirs = 2 * chip[0] + chip[1]
            if kind == "gather":
                copies.append(remote(src_refs[w].at[c], land_refs[w].at[theirs if arriving else own, c], j, (*chip, c)))
            elif kind == "forward":
                copies.append(remote(src_refs[w].at[theirs, c], src_refs[w].at[theirs, (1 - c) if arriving else c],
                                     j, sibling))
            else:
                copies.append(remote(src_refs[w].at[theirs], land_refs[w].at[j], j, (*chip, c)))
        if kind == "gather":
            copies.append(remote(src_refs[w], land_refs[w].at[own], 3, sibling))
    return copies


def _ici_start(kind, srcs, name, after=None):
    n = len(srcs)
    if kind == "gather":
        lands = [lax.empty((4,) + s.shape, s.dtype) for s in srcs]
    elif kind == "scatter":
        lands = [lax.empty((3,) + s.shape[1:], s.dtype) for s in srcs]
    elif kind == "sibling":
        lands = [lax.empty(s.shape[1:], s.dtype) for s in srcs]
    elif kind == "swap":
        lands = [lax.empty(s.shape, s.dtype) for s in srcs]
    else:
        lands = []
    nb = n + len(lands)
    afters = [] if after is None else [after]

    def body(*refs):
        src_refs, land_refs = refs[:n], refs[n:nb]
        send_sems, recv_sems = refs[nb + len(afters)], refs[nb + len(afters) + 1]
        token = refs[-1]
        for cp in _flight_copies(kind, src_refs, land_refs, send_sems, recv_sems, False):
            cp.start()
        token[...] = jnp.zeros_like(token)

    hbm = lambda a: pltpu.with_memory_space_constraint(a, pltpu.HBM)
    n_sems = _copies_per_weight(kind) * n
    out = pl.pallas_call(
        body, name=name,
        out_shape=(pltpu.SemaphoreType.DMA((n_sems,)), pltpu.SemaphoreType.DMA((n_sems,)),
                   *[pltpu.HBM(a.shape, a.dtype) for a in srcs + lands], jax.ShapeDtypeStruct((8, LANES), F32)),
        in_specs=_hbm_specs(nb) + [pl.BlockSpec(memory_space=pl.ANY)] * len(afters),
        out_specs=(SEM_SPEC, SEM_SPEC, *_hbm_specs(nb), pl.BlockSpec(memory_space=pltpu.VMEM)),
        input_output_aliases={i: 2 + i for i in range(nb)},
        compiler_params=pltpu.CompilerParams(has_side_effects=DATAFLOW_EFFECT),
    )(*[hbm(a) for a in srcs + lands], *afters)
    return out[0], out[1], list(out[2:2 + n]), list(out[2 + n:2 + nb]), out[-1]


def _ici_wait(kind, send_sems, recv_sems, srcs, lands, after, name):
    n = len(srcs)
    nb = n + len(lands)

    def body(*refs):
        src_refs, land_refs = refs[:n], refs[n:nb]
        send_ref, recv_ref = refs[nb], refs[nb + 1]
        for cp in _flight_copies(kind, src_refs, land_refs, send_ref, recv_ref, True):
            cp.wait_send()
            cp.wait_recv()

    out = pl.pallas_call(
        body, name=name, out_shape=tuple(pltpu.HBM(a.shape, a.dtype) for a in srcs + lands),
        in_specs=_hbm_specs(nb) + [SEM_SPEC, SEM_SPEC, pl.BlockSpec(memory_space=pl.ANY)],
        out_specs=tuple(_hbm_specs(nb)), input_output_aliases={i: i for i in range(nb)},
        compiler_params=pltpu.CompilerParams(has_side_effects=DATAFLOW_EFFECT),
    )(*srcs, *lands, send_sems, recv_sems, after)
    return list(out[:n]), list(out[n:])


def _halves_exchange(halves, name):
    n = len(halves)

    def body(*refs):
        ins, outs = refs[:n], refs[n:2 * n]
        send_sems, recv_sems = refs[2 * n:]
        x, y, c, _ = _mesh_place()
        copies = [pltpu.make_async_remote_copy(src_ref=ins[w], dst_ref=outs[w], send_sem=send_sems.at[w],
                                               recv_sem=recv_sems.at[w], device_id=(x, y, 1 - c), device_id_type=MESH)
                  for w in range(n)]
        for cp in copies:
            cp.start()
        for cp in copies:
            cp.wait()

    return pl.pallas_call(
        body, name=name,
        out_shape=[jax.ShapeDtypeStruct(h.shape, h.dtype) for h in halves],
        in_specs=_hbm_specs(n), out_specs=_hbm_specs(n),
        scratch_shapes=[pltpu.SemaphoreType.DMA((n,)), pltpu.SemaphoreType.DMA((n,))],
    )(*halves)


def _small_all_reduce(packed):
    rows = packed.shape[0]

    def body(in_ref, out_ref, from_sibling, pair_sums, send_sems, recv_sems):
        x, y, c, other_chips = _mesh_place()
        own = 2 * x + y

        def remote(src, dst, k, to):
            return pltpu.make_async_remote_copy(src_ref=src, dst_ref=dst, send_sem=send_sems.at[k],
                                                recv_sem=recv_sems.at[k], device_id=to, device_id_type=MESH)

        swap = remote(in_ref, from_sibling, 0, (x, y, 1 - c))
        swap.start()
        swap.wait()
        mine, theirs = in_ref[...], from_sibling[...]
        pair_sums[own] = jnp.where(c == 0, mine, theirs) + jnp.where(c == 0, theirs, mine)
        copies = [remote(pair_sums.at[own], pair_sums.at[own], 1 + j, (*chip, c)) for j, chip in enumerate(other_chips)]
        for cp in copies:
            cp.start()
        for j, chip in enumerate(other_chips):
            landing = pair_sums.at[2 * chip[0] + chip[1]]
            remote(landing, landing, 1 + j, (*chip, c)).wait_recv()
        for cp in copies:
            cp.wait_send()
        out_ref[...] = ((pair_sums[0] + pair_sums[1]) + pair_sums[2]) + pair_sums[3]

    return pl.pallas_call(
        body, name="small_all_reduce", out_shape=jax.ShapeDtypeStruct(packed.shape, F32),
        in_specs=[pl.BlockSpec(memory_space=pltpu.VMEM)], out_specs=pl.BlockSpec(memory_space=pltpu.VMEM),
        scratch_shapes=[pltpu.VMEM((rows, LANES), F32), pltpu.VMEM((4, rows, LANES), F32),
                        pltpu.SemaphoreType.DMA((4,)), pltpu.SemaphoreType.DMA((4,))],
        compiler_params=pltpu.CompilerParams(vmem_limit_bytes=VMEM_LIMIT_BYTES),
    )(packed)


def _layout(w_in, w_uq, w_ukv, v_ln_gain, q_norm, kv_norm):
    heads = 4 * w_uq.shape[-1] // (NOPE_DIM + ROPE_DIM)
    gw = v_ln_gain.shape[-1]
    ql, kl = q_norm.shape[-1], kv_norm.shape[-1]
    lay = dict(heads=heads, gw=gw, ql=ql, kl=kl, aw=heads * V_DIM, u_off=0, v_off=gw, q_off=2 * gw,
               kv_off=2 * gw + ql, kr_off=2 * gw + ql + kl)
    lay["in_pad"] = _round_up(lay["kr_off"] + LANES, 2 * LANES if lay["kr_off"] + LANES <= 2048 else 1024)
    assert lay["q_off"] % ql == 0 and lay["kv_off"] % kl == 0 and lay["aw"] % gw == 0
    assert 4 * w_in.shape[-1] == ql + kl + ROPE_DIM + 2 * gw
    return lay


def _rope_tile(t1, t2, axis=-1):
    z = jnp.zeros_like(t1)
    return jnp.concatenate([t1, z, t2, z], axis=axis)


def _w_in_rows(gathered, shard_rows, lay):
    d = gathered.shape[-1]
    wt = gathered[:, :shard_rows].reshape(4 * shard_rows, d)
    ql, kl, gw = lay["ql"], lay["kl"], lay["gw"]
    q_c, kv_c = wt[:ql], wt[ql:ql + kl]
    r = wt[ql + kl:ql + kl + ROPE_DIM]
    u = wt[ql + kl + ROPE_DIM:ql + kl + ROPE_DIM + gw]
    v = wt[ql + kl + ROPE_DIM + gw:]
    parts = [u, v, q_c, kv_c, _rope_tile(r[:ROPE_HALF], r[ROPE_HALF:], axis=0)]
    pad = lay["in_pad"] - (lay["kr_off"] + LANES)
    if pad:
        parts.append(jnp.zeros((pad, d), wt.dtype))
    return jnp.concatenate(parts, axis=0)


def _w_in_grad_chunks(dwt, shard_rows, padded_rows, lay):
    d = dwt.shape[-1]
    ql, kl, gw = lay["ql"], lay["kl"], lay["gw"]
    ko = lay["kr_off"]
    rows = jnp.concatenate([dwt[lay["q_off"]:lay["q_off"] + ql], dwt[lay["kv_off"]:lay["kv_off"] + kl],
                            dwt[ko:ko + ROPE_HALF], dwt[ko + 2 * ROPE_HALF:ko + 3 * ROPE_HALF],
                            dwt[:gw], dwt[gw:2 * gw]], axis=0).reshape(4, shard_rows, d)
    rows = jnp.pad(rows, ((0, 0), (0, padded_rows - shard_rows), (0, 0)))
    return jnp.transpose(rows.reshape(4, 2, padded_rows // 2, d), (1, 0, 2, 3)).astype(BF16)


def _w_uq_padded(w, heads):
    w3 = w.reshape(w.shape[0], heads, NOPE_DIM + ROPE_DIM)
    t = _rope_tile(w3[..., NOPE_DIM:NOPE_DIM + ROPE_HALF], w3[..., NOPE_DIM + ROPE_HALF:])
    return jnp.concatenate([w3[..., :NOPE_DIM], t], axis=-1).reshape(w.shape[0], heads * HEAD_PAD)


def _w_uq_grad_unpadded(dw, heads):
    d3 = dw.reshape(dw.shape[0], heads, HEAD_PAD)
    return jnp.concatenate([d3[..., :NOPE_DIM], d3[..., NOPE_DIM:NOPE_DIM + ROPE_HALF],
                            d3[..., NOPE_DIM + 2 * ROPE_HALF:NOPE_DIM + 3 * ROPE_HALF]],
                           axis=-1).reshape(dw.shape[0], heads * (NOPE_DIM + ROPE_DIM))


def _cols_gathered(g):
    return jnp.transpose(g, (1, 0, 2)).reshape(g.shape[1], 4 * g.shape[2])


def _chunks_of_cols(grad):
    r, c4 = grad.shape
    return jnp.transpose(grad.reshape(2, r // 2, 4, c4 // 4), (0, 2, 1, 3)).astype(BF16)


SMALL = ["pre_mix_norm", "q_norm", "kv_norm", "v_ln_gain", "v_ln_bias", "w_spatial", "b_spatial", "attn_out_norm",
         "gmlp_out_norm", "post_mix_norm", "pre_ffn_norm", "post_ffn_norm"]
BIG = ["w_in", "w_uq", "w_ukv", "w_out", "w_gate", "w_up", "w_down"]
GATHER_NOW = ["w_in", "w_uq", "w_ukv"]
GATHER_LATER = ["w_out", "w_gate", "w_up", "w_down"]
REDUCE_FFN = ["w_gate", "w_up", "w_down"]
REDUCE_OUT = ["w_out"]
REDUCE_LAST = ["w_in", "w_uq", "w_ukv"]
TRANSPOSED = ("w_in", "w_gate", "w_up")
ORDER = ["pre_mix_norm", "w_in", "q_norm", "kv_norm", "w_uq", "w_ukv", "v_ln_gain", "v_ln_bias", "w_spatial",
         "b_spatial", "attn_out_norm", "gmlp_out_norm", "w_out", "post_mix_norm", "pre_ffn_norm", "w_gate", "w_up",
         "w_down", "post_ffn_norm"]


def _pack(arrays):
    flat = jnp.concatenate([a.reshape(-1) for a in arrays])
    n = flat.shape[0]
    total = _round_up(n, 8 * LANES)
    if total > n:
        flat = jnp.concatenate([flat, jnp.zeros((total - n,), F32)])
    return flat.reshape(total // LANES, LANES)


def _unpack(packed, like):
    flat = packed.reshape(-1)
    out, off = [], 0
    for a in like:
        out.append(flat[off:off + a.size].reshape(a.shape))
        off += a.size
    return out


def kernel(x, positions, pre_mix_norm, w_in, q_norm, kv_norm, w_uq, w_ukv, v_ln_gain, v_ln_bias, w_spatial, b_spatial, attn_out_norm, gmlp_out_norm, w_out, post_mix_norm, pre_ffn_norm, w_gate, w_up, w_down, post_ffn_norm, loss_target, m_pre_mix_norm, m_w_in, m_q_norm, m_kv_norm, m_w_uq, m_w_ukv, m_v_ln_gain, m_v_ln_bias, m_w_spatial, m_b_spatial, m_attn_out_norm, m_gmlp_out_norm, m_w_out, m_post_mix_norm, m_pre_ffn_norm, m_w_gate, m_w_up, m_w_down, m_post_ffn_norm, v_pre_mix_norm, v_w_in, v_q_norm, v_kv_norm, v_w_uq, v_w_ukv, v_v_ln_gain, v_v_ln_bias, v_w_spatial, v_b_spatial, v_attn_out_norm, v_gmlp_out_norm, v_w_out, v_post_mix_norm, v_pre_ffn_norm, v_w_gate, v_w_up, v_w_down, v_post_ffn_norm):
    args = dict(locals())
    weights = {n: args[n] for n in ORDER}
    m_in = {n: args["m_" + n] for n in ORDER}
    v_in = {n: args["v_" + n] for n in ORDER}

    lay = _layout(w_in, w_uq, w_ukv, v_ln_gain, q_norm, kv_norm)
    heads, gw = lay["heads"], lay["gw"]
    t, d = x.shape[1], x.shape[2]
    tr = 128 if t % 128 == 0 else t
    xs = x.reshape(t, d)
    target = loss_target.reshape(t, d)

    ffs, ins = w_gate.shape[-1], w_in.shape[-1]
    ffp, inp = _round_up(ffs, LANES), _round_up(ins, LANES)
    shards = {n: (jnp.swapaxes(weights[n][0], 0, 1) if n in TRANSPOSED else weights[n][0]).astype(BF16)
              for n in BIG}
    for n, rows in (("w_gate", ffp), ("w_up", ffp), ("w_down", ffp), ("w_in", inp)):
        shards[n] = jnp.pad(shards[n], ((0, rows - shards[n].shape[0]), (0, 0)))
    halved = {n: shards[n].reshape(2, shards[n].shape[0] // 2, shards[n].shape[1]) for n in BIG}
    full = {}

    def pair_sums_of(partial, names, tag):
        from_sibling = _sibling_exchange([partial[n] for n in names], "grads_sibling_exchange_" + tag)
        return [_pair_add(partial[n], r, "pair_add_" + n) for n, r in zip(names, from_sibling)]

    def place(names, lands):
        for n, g in zip(names, lands):
            full[n] = g.reshape((4,) + shards[n].shape)

    flight_0 = _ici_start("gather", [halved[n] for n in GATHER_NOW], "gather_start_0")
    flights, last_token = {}, flight_0[4]
    for n in GATHER_LATER:
        flights[n] = _ici_start("gather", [halved[n]], "gather_start_" + n, after=last_token)
        last_token = flights[n][4]
    xn = _prenorm(xs, pre_mix_norm.reshape(1, -1), tr, last_token)
    _, lands = _ici_wait("gather", *flight_0[:4], xn, "gather_wait_0")
    pass_now = _ici_start("forward", lands, "forward_start_now")
    place(GATHER_NOW, _ici_wait("forward", *pass_now[:4], pass_now[4], "forward_wait_now")[0])
    wt_in = _w_in_rows(full["w_in"], ins, lay)
    wb_uq = _w_uq_padded(_cols_gathered(full["w_uq"]), heads)
    wb_ukv = _cols_gathered(full["w_ukv"])

    inv_freq = 1.0 / (ROPE_THETA ** (jnp.arange(0, ROPE_DIM, 2, dtype=F32) / ROPE_DIM))
    ang = positions.reshape(t).astype(F32)[:, None] * inv_freq
    cos, sin = jnp.cos(ang), jnp.sin(ang)
    cos_t = _rope_tile(cos, cos)
    sin_t = _rope_tile(-sin, sin)

    row = lambda a: a.reshape(1, -1)
    g_pre, g_q, g_kv = row(pre_mix_norm), row(q_norm), row(kv_norm)
    g_a, g_g, g_pm = row(attn_out_norm), row(gmlp_out_norm), row(post_mix_norm)
    g_pf, g_po = row(pre_ffn_norm), row(post_ffn_norm)
    ln_g, ln_b = row(v_ln_gain), row(v_ln_bias)
    ws = w_spatial[0].astype(BF16)
    ws_t = jnp.transpose(ws, (0, 2, 1))
    bs_wide = jnp.broadcast_to(b_spatial[0][:, :, None], b_spatial.shape[1:] + (G_HEAD_DIM,))

    proj = _matmul(xn, wt_in, NT, F32, "proj")
    qn, kvn, kr = _qkv_prep(proj, g_q, g_kv, cos_t, sin_t, lay, tr)
    q = _q_rope(_matmul(qn, wb_uq, NN, F32, "q_up"), cos_t, sin_t, heads, tr)
    kv = _matmul(kvn, wb_ukv, NN, BF16, "kv_up")
    a_out, a_lse = _attn_fwd(q, kv, kr, heads)
    def arrive(n, after):
        _, lands = _ici_wait("gather", *flights[n][:4], after, "gather_wait_" + n)
        return _ici_start("forward", lands, "forward_start_" + n)

    def settle(n, passing, after):
        place([n], _ici_wait("forward", *passing[:4], after, "forward_wait_" + n)[0])

    pass_out = arrive("w_out", a_out)
    gn = _gmlp_fwd(proj, ln_g, ln_b, ws, bs_wide, g_g, lay)
    mixed = _mix_norm(a_out, gn, g_a, tr, pass_out[4])
    settle("w_out", pass_out, mixed)
    pass_gate = arrive("w_gate", mixed)
    wb_out = full["w_out"].reshape(-1, d)
    mix_out = _matmul(mixed, wb_out, NN, F32, "mix_out", token=pass_gate[4])
    h, hn = _post_mix(xs, mix_out, g_pm, g_pf, tr)
    settle("w_gate", pass_gate, hn)
    pass_up = arrive("w_up", hn)
    wt_gate = full["w_gate"].reshape(4 * ffp, d)
    gate = _matmul(hn, wt_gate, NT, BF16, "ffn_gate", token=pass_up[4])
    settle("w_up", pass_up, gate)
    pass_down = arrive("w_down", gate)
    wt_up = full["w_up"].reshape(4 * ffp, d)
    up = _matmul(hn, wt_up, NT, BF16, "ffn_up", token=pass_down[4])
    act = _swiglu(gate, up)
    settle("w_down", pass_down, act)
    wb_down = full["w_down"].reshape(4 * ffp, d)
    ffn = _matmul(act, wb_down, NN, F32, "ffn_down")
    dy, d_ffn, dg_po, loss_vec = _loss_head(h, ffn, target, g_po, tr)

    d_act = _matmul(d_ffn, wb_down, NT, BF16, "d_act")
    d_gate, d_up = _swiglu_bwd(gate, up, d_act)
    partial_ffn = [_matmul(d_gate, hn, TN, BF16, "gw_gate", out_chunks=True),
                   _matmul(d_up, hn, TN, BF16, "gw_up", out_chunks=True),
                   _matmul(act, d_ffn, TN, BF16, "gw_down", out_chunks=True)]
    swap_ffn = _ici_start("sibling", partial_ffn, "sibling_start_ffn")
    d_hn = _matmul(d_up, wt_up, NN, F32, "d_hn",
                   extras=[_matmul(d_gate, wt_gate, NN, F32, "d_hn_gate", token=swap_ffn[4])],
                   epilogue=lambda acc, partial: acc + partial)
    partial_ffn, from_sibling = _ici_wait("sibling", *swap_ffn[:4], d_hn, "sibling_wait_ffn")
    pair_ffn = [_pair_add(p, r, "pair_add_" + n) for n, p, r in zip(REDUCE_FFN, partial_ffn, from_sibling)]
    flight_ffn = _ici_start("scatter", pair_ffn, "scatter_start_ffn")
    dh, d_mo, dg_pf, dg_pm = _post_mix_bwd(mix_out, h, dy, d_hn, g_pm, g_pf, tr, flight_ffn[4])
    d_mixed = _matmul(d_mo, wb_out, NT, F32, "d_mixed")
    gw_out = _matmul(mixed, d_mo, TN, BF16, "gw_out", out_chunks=True)
    pair_out = pair_sums_of({"w_out": gw_out}, REDUCE_OUT, "out")
    flight_out = _ici_start("scatter", pair_out, "scatter_start_out")
    d_a, dg_a = _mix_norm_bwd(a_out, d_mixed, g_a, tr, flight_out[4])
    d_u, d_v, dg_g, d_ln_g, d_ln_b, d_ws, d_bs_wide = _gmlp_bwd(proj, d_mixed, ln_g, ln_b, ws, ws_t, bs_wide, g_g, lay)
    d_bs = _spatial_bias_grad(d_bs_wide)
    d_q, d_kv, d_kr = _attn_bwd(q, kv, kr, a_out, a_lse, d_a, cos_t, sin_t, heads)
    d_qn = _matmul(d_q, wb_uq, NT, F32, "d_qn")
    gw_uq = _matmul(qn, d_q, TN, BF16, "gw_uq")
    d_kvn = _matmul(d_kv, wb_ukv, NT, F32, "d_kvn")
    gw_ukv = _matmul(kvn, d_kv, TN, BF16, "gw_ukv")
    d_qc, d_kvc, d_krt, dg_q, dg_kv = _qkv_bwd(proj, d_qn, d_kvn, d_kr, g_q, g_kv, cos_t, sin_t, lay, tr)
    parts = [d_u, d_v, d_qc, d_kvc, d_krt]
    pad = lay["in_pad"] - (lay["kr_off"] + LANES)
    if pad:
        parts.append(jnp.zeros((t, pad), BF16))
    d_proj = jnp.concatenate(parts, axis=1)
    d_xn = _matmul(d_proj, wt_in, NN, F32, "d_xn")
    gw_in = _matmul(d_proj, xn, TN, BF16, "gw_in")
    grad_x, dg_pre = _prenorm_bwd(xs, d_xn, dh, g_pre, tr)

    pair_mix = pair_sums_of({"w_in": _w_in_grad_chunks(gw_in, ins, inp, lay),
                             "w_uq": _chunks_of_cols(_w_uq_grad_unpadded(gw_uq, heads)),
                             "w_ukv": _chunks_of_cols(gw_ukv)}, REDUCE_LAST, "mix")
    pair_ffn, received_ffn = _ici_wait("scatter", *flight_ffn[:4], pair_mix[-1], "scatter_wait_ffn")
    pair_out, received_out = _ici_wait("scatter", *flight_out[:4], pair_mix[-1], "scatter_wait_out")
    flight_mix = _ici_start("scatter", pair_mix, "scatter_start_mix")
    grads, delta, new_m, new_v = {}, {}, {}, {}

    def finish(names, pair_sums, received, tag, token, in_flight):
        mine, swaps = [], []
        for n, p, r in zip(names, pair_sums, received):
            mine.append(_chip_sum(p, r, "chip_sum_" + n, token))
            if in_flight:
                swaps.append(_ici_start("swap", [mine[-1]], "swap_start_" + n))
                token = swaps[-1][4]
        if not in_flight:
            theirs = _halves_exchange(mine, "grads_halves_exchange_" + tag)
        previous = mine[-1]
        for i, n in enumerate(names):
            if in_flight:
                (g_mine,), (g_theirs,) = _ici_wait("swap", *swaps[i][:4], previous, "swap_wait_" + n)
            else:
                g_mine, g_theirs = mine[i], theirs[i]
            shape = weights[n].shape
            if n in TRANSPOSED:
                view = lambda a: jnp.swapaxes(a[0], 0, 1)
                back = lambda o: jnp.swapaxes(o, 0, 1).reshape(shape)
            else:
                view = lambda a: a[0]
                back = lambda o: o.reshape(shape)
            out = _adamw_halves(view(weights[n]), g_mine, g_theirs, view(m_in[n]), view(v_in[n]), "adamw_" + n)
            grads[n], delta[n], new_m[n], new_v[n] = [back(o) for o in out]
            previous = out[-1]

    finish(REDUCE_FFN + REDUCE_OUT, pair_ffn + pair_out, received_ffn + received_out, "ffn", flight_mix[4], True)
    pair_mix, received = _ici_wait("scatter", *flight_mix[:4], new_v[REDUCE_OUT[-1]], "scatter_wait_mix")
    finish(REDUCE_LAST, pair_mix, received, "mix", None, False)

    small_grads = {"pre_mix_norm": dg_pre, "q_norm": dg_q, "kv_norm": dg_kv, "v_ln_gain": d_ln_g, "v_ln_bias": d_ln_b,
                   "w_spatial": d_ws, "b_spatial": d_bs, "attn_out_norm": dg_a, "gmlp_out_norm": dg_g,
                   "post_mix_norm": dg_pm, "pre_ffn_norm": dg_pf, "post_ffn_norm": dg_po}
    like = [weights[n] for n in SMALL]
    reduced = _small_all_reduce(_pack([small_grads[n] for n in SMALL] + [loss_vec]))
    loss = reduced.reshape(-1)[sum(a.size for a in like)]
    small_g = _pack(_unpack(reduced, like))
    s_delta, s_m, s_v = _adamw(_pack(like), small_g, _pack([m_in[n] for n in SMALL]),
                               _pack([v_in[n] for n in SMALL]), "adamw_small")
    for n, g in zip(SMALL, _unpack(small_g, like)):
        grads[n] = g
    delta.update(zip(SMALL, _unpack(s_delta, like)))
    new_m.update(zip(SMALL, _unpack(s_m, like)))
    new_v.update(zip(SMALL, _unpack(s_v, like)))

    return (loss, grad_x.reshape(x.shape), *[grads[n] for n in ORDER], *[delta[n] for n in ORDER],
            *[new_m[n] for n in ORDER], *[new_v[n] for n in ORDER])
```

```python
import functools
import math

import jax
import jax.numpy as jnp
from jax import lax
from jax.experimental import pallas as pl
from jax.experimental.pallas import tpu as pltpu

F32 = jnp.float32
BF16 = jnp.bfloat16
MESH = pl.DeviceIdType.MESH

NOPE_DIM = 128
ROPE_DIM = 64
ROPE_HALF = ROPE_DIM // 2
V_DIM = 128
HEAD_PAD = 256
G_HEAD_DIM = 128
CHUNK = 128
ROPE_THETA = 10000.0
EPS = 1e-6
ADAM_LR = 0.001
ADAM_B1 = 0.9
ADAM_B2 = 0.999
ADAM_EPS = 1e-08
ADAM_WD = 0.01
ADAM_STEP = 10

LANES = 128
MATMUL_TILE = 1024
WIDE_TILE = 1408
VMEM_LIMIT_BYTES = 48 * 1024 * 1024

NN = (((1,), (0,)), ((), ()))
NT = (((1,), (1,)), ((), ()))
TN = (((0,), (0,)), ((), ()))


def _params(semantics):
    return pltpu.CompilerParams(dimension_semantics=semantics, vmem_limit_bytes=VMEM_LIMIT_BYTES)


def _tile(n, cap=MATMUL_TILE):
    if n <= cap:
        return n
    if cap == MATMUL_TILE and n % WIDE_TILE == 0:
        return WIDE_TILE
    t = cap - cap % LANES
    while n % t:
        t -= LANES
    assert t > 0, n
    return t


def _round_up(n, m):
    return (n + m - 1) // m * m


def _matmul(a, b, dims, out_dtype, name, extras=(), epilogue=None, out_chunks=None, token=None):
    if dims is NN:
        (m, k), (k2, n) = a.shape, b.shape
    elif dims is NT:
        (m, k), (n, k2) = a.shape, b.shape
    else:
        (k, m), (k2, n) = a.shape, b.shape
    assert k == k2, (a.shape, b.shape, name)
    tm, tn, tk = _tile(m // 8 if out_chunks else m), _tile(n), _tile(k, 2 * MATMUL_TILE)
    if len(extras) + (len(out_dtype) if isinstance(out_dtype, tuple) else 1) > 2:
        tm = _tile(m, MATMUL_TILE // 2)
    if not extras and k % (2 * WIDE_TILE) == 0:
        tk = 2 * WIDE_TILE
    nk = k // tk

    out_dtypes = out_dtype if isinstance(out_dtype, tuple) else (out_dtype,)
    n_extra = len(extras)

    def body(*refs):
        a_ref, b_ref = refs[:2]
        extra_refs = refs[2:2 + n_extra]
        out_refs = refs[2 + n_extra + (token is not None):-1]
        acc_ref = refs[-1]
        kk = pl.program_id(2)

        @pl.when(kk == 0)
        def _():
            acc_ref[...] = jnp.zeros_like(acc_ref)

        acc_ref[...] += lax.dot_general(a_ref[...], b_ref[...], dims, preferred_element_type=F32)

        @pl.when(kk == nk - 1)
        def _():
            r = acc_ref[...]
            if epilogue is not None:
                r = epilogue(r, *[e[...] for e in extra_refs])
            for o_ref, val in zip(out_refs, r if isinstance(r, tuple) else (r,)):
                o_ref[...] = val.astype(o_ref.dtype)

    if dims is TN:
        a_spec = pl.BlockSpec((tk, tm), lambda i, j, kk: (kk, i))
    else:
        a_spec = pl.BlockSpec((tm, tk), lambda i, j, kk: (i, kk))
    if dims is NT:
        b_spec = pl.BlockSpec((tn, tk), lambda i, j, kk: (j, kk))
    else:
        b_spec = pl.BlockSpec((tk, tn), lambda i, j, kk: (kk, j))
    if not out_chunks:
        o_spec = pl.BlockSpec((tm, tn), lambda i, j, kk: (i, j))
        o_shape = (m, n)
    else:
        pi = m // 8 // tm
        o_spec = pl.BlockSpec((None, None, tm, tn), lambda i, j, kk: ((i // pi) % 2, i // (2 * pi), i % pi, j))
        o_shape = (2, 4, m // 8, n)
    assert not (extras and out_chunks)
    tokens = [] if token is None else [token]
    out = pl.pallas_call(
        body, name=name, grid=(m // tm, n // tn, nk),
        in_specs=[a_spec, b_spec] + [o_spec] * n_extra + [pl.BlockSpec(memory_space=pl.ANY)] * len(tokens),
        out_specs=[o_spec] * len(out_dtypes),
        out_shape=[jax.ShapeDtypeStruct(o_shape, dt) for dt in out_dtypes],
        scratch_shapes=[pltpu.VMEM((tm, tn), F32)],
        compiler_params=_params(("parallel", "parallel", "arbitrary")),
    )(a, b, *extras, *tokens)
    return tuple(out) if isinstance(out_dtype, tuple) else out[0]


def _row_call(body, name, rows, tr, row_ins, par_ins, row_outs, acc_outs):
    def col(i, cb):
        return (i, cb)

    def whole(i, nd):
        return (0,) * nd

    in_specs = [pl.BlockSpec((tr, w), functools.partial(col, cb=cb)) for (_, w, cb) in row_ins]
    in_specs += [pl.BlockSpec(a.shape, functools.partial(whole, nd=a.ndim)) for a in par_ins]
    out_specs = [pl.BlockSpec((tr, w), lambda i: (i, 0)) for (w, _) in row_outs]
    out_specs += [pl.BlockSpec(s, functools.partial(whole, nd=len(s))) for (s, _) in acc_outs]
    out_shape = [jax.ShapeDtypeStruct((rows, w), dt) for (w, dt) in row_outs]
    out_shape += [jax.ShapeDtypeStruct(s, dt) for (s, dt) in acc_outs]
    return pl.pallas_call(
        body, name=name, grid=(rows // tr,), in_specs=in_specs, out_specs=out_specs, out_shape=out_shape,
        compiler_params=_params(("arbitrary",) if acc_outs else ("parallel",)),
    )(*[a for (a, _, _) in row_ins], *par_ins)


def _accumulate(ref, val):
    i = pl.program_id(0)

    @pl.when(i == 0)
    def _():
        ref[...] = val

    @pl.when(i > 0)
    def _():
        ref[...] += val


def _colsum(v):
    return jnp.sum(v, axis=0, keepdims=True)


def _rms_fwd(x, g):
    r = lax.rsqrt(jnp.mean(x * x, axis=-1, keepdims=True) + EPS)
    return x * r * g


def _rms_bwd(x, g, dy):
    r = lax.rsqrt(jnp.mean(x * x, axis=-1, keepdims=True) + EPS)
    xh = x * r
    dxh = dy * g
    dx = r * (dxh - xh * jnp.mean(dxh * xh, axis=-1, keepdims=True))
    return dx, dy * xh


_GELU_C = math.sqrt(2.0 / math.pi)
_GELU_A = 0.044715


def _gelu(x):
    return 0.5 * x * (1.0 + jnp.tanh(_GELU_C * (x + _GELU_A * (x * x * x))))


def _gelu_and_grad(x):
    t = jnp.tanh(_GELU_C * (x + _GELU_A * (x * x * x)))
    return (0.5 * x * (1.0 + t),
            0.5 * (1.0 + t) + 0.5 * x * (1.0 - t * t) * (_GELU_C * (1.0 + 3.0 * _GELU_A * (x * x))))


def _sigmoid(x):
    return 1.0 / (1.0 + jnp.exp(-x))


def _rope_fwd(t, cos_t, sin_t):
    return t * cos_t + pltpu.roll(t, 2 * ROPE_HALF, 1) * sin_t


def _rope_bwd(dt, cos_t, sin_t):
    return dt * cos_t - pltpu.roll(dt, 2 * ROPE_HALF, 1) * sin_t


def _prenorm(x, g, tr, token):
    def body(x_ref, g_ref, token_ref, o_ref):
        o_ref[...] = _rms_fwd(x_ref[...], g_ref[...]).astype(BF16)

    t, d = x.shape
    return _row_call(body, "prenorm", t, tr, [(x, d, 0)], [g, token], [(d, BF16)], [])[0]


def _qkv_prep(proj, g_q, g_kv, cos_t, sin_t, lay, tr, token):
    ql, kl = lay["ql"], lay["kl"]

    def body(q_ref, kv_ref, kr_ref, cos_ref, sin_ref, gq_ref, gkv_ref, token_ref, qn_ref, kvn_ref, kro_ref):
        qn_ref[...] = _rms_fwd(q_ref[...], gq_ref[...]).astype(BF16)
        kvn_ref[...] = _rms_fwd(kv_ref[...], gkv_ref[...]).astype(BF16)
        kro_ref[...] = _rope_fwd(kr_ref[...], cos_ref[...], sin_ref[...]).astype(BF16)

    t = proj.shape[0]
    return _row_call(
        body, "qkv_prep", t, tr,
        [(proj, ql, lay["q_off"] // ql), (proj, kl, lay["kv_off"] // kl), (proj, LANES, lay["kr_off"] // LANES),
         (cos_t, LANES, 0), (sin_t, LANES, 0)],
        [g_q, g_kv, token], [(ql, BF16), (kl, BF16), (LANES, BF16)], [])


def _q_rope(q, cos_t, sin_t, heads, tr):
    def body(q_ref, cos_ref, sin_ref, o_ref):
        c, s = cos_ref[...], sin_ref[...]
        for h in range(heads):
            lo = h * HEAD_PAD
            o_ref[:, lo:lo + NOPE_DIM] = q_ref[:, lo:lo + NOPE_DIM].astype(BF16)
            o_ref[:, lo + NOPE_DIM:lo + HEAD_PAD] = _rope_fwd(q_ref[:, lo + NOPE_DIM:lo + HEAD_PAD], c, s).astype(BF16)

    t, w = q.shape
    return _row_call(body, "q_rope", t, tr, [(q, w, 0), (cos_t, LANES, 0), (sin_t, LANES, 0)], [], [(w, BF16)], [])[0]


ATTN_SCALE = 1.0 / math.sqrt(NOPE_DIM + ROPE_DIM)
ATTN_EXP2_SCALE = ATTN_SCALE * math.log2(math.e)


def _attn_tile(t, cap):
    tq = cap
    while t % tq:
        tq //= 2
    return tq


def _attn_fwd(q, kv, kr, heads):
    t = q.shape[0]
    tq = _attn_tile(t, 256)

    def body(q_ref, kv_ref, kr_ref, o_ref, lse_ref, kcat):
        @pl.when(pl.program_id(1) == 0)
        def _():
            kcat[:, :NOPE_DIM] = kv_ref[:, :NOPE_DIM]
            kcat[:, NOPE_DIM:] = kr_ref[...]

        s = lax.dot_general(q_ref[...], kcat[...], NT, preferred_element_type=F32)
        m = jnp.max(s, axis=-1, keepdims=True)
        p = jnp.exp2((s - m) * ATTN_EXP2_SCALE)
        l = jnp.sum(p, axis=-1, keepdims=True)
        o_ref[...] = jnp.dot(p.astype(BF16), kv_ref[:, NOPE_DIM:], preferred_element_type=F32) * (1.0 / l)
        lse_ref[...] = jnp.broadcast_to(m * ATTN_EXP2_SCALE + jnp.log(l) * math.log2(math.e), (tq, V_DIM))

    out_spec = pl.BlockSpec((tq, V_DIM), lambda h, i: (i, h))
    out_shape = jax.ShapeDtypeStruct((t, heads * V_DIM), F32)
    return pl.pallas_call(
        body, name="attn_fwd", grid=(heads, t // tq),
        in_specs=[pl.BlockSpec((tq, HEAD_PAD), lambda h, i: (i, h)),
                  pl.BlockSpec((t, HEAD_PAD), lambda h, i: (0, h)),
                  pl.BlockSpec((t, LANES), lambda h, i: (0, 0))],
        out_specs=[out_spec, out_spec], out_shape=[out_shape, out_shape],
        scratch_shapes=[pltpu.VMEM((t, HEAD_PAD), BF16)],
        compiler_params=_params(("arbitrary", "arbitrary")),
    )(q, kv, kr)


def _attn_bwd(q, kv, kr, out, lse, d_out, cos_t, sin_t, heads):
    t = q.shape[0]
    tq = _attn_tile(t, 256)
    nq = t // tq

    def body(q_ref, kv_ref, kr_ref, o_ref, lse_ref, do_ref, cos_ref, sin_ref, dq_ref, dkv_ref, dkr_ref,
             kcat, dk_acc, dv_acc):
        h, i = pl.program_id(0), pl.program_id(1)

        @pl.when(i == 0)
        def _():
            kcat[:, :NOPE_DIM] = kv_ref[:, :NOPE_DIM]
            kcat[:, NOPE_DIM:] = kr_ref[...]
            dk_acc[...] = jnp.zeros_like(dk_acc)
            dv_acc[...] = jnp.zeros_like(dv_acc)

        @pl.when((h == 0) & (i == 0))
        def _():
            dkr_ref[...] = jnp.zeros_like(dkr_ref)

        qb, dob = q_ref[...], do_ref[...]
        row_term = jnp.sum(dob.astype(F32) * o_ref[...], axis=-1, keepdims=True)
        s = lax.dot_general(qb, kcat[...], NT, preferred_element_type=F32)
        dp = lax.dot_general(dob, kv_ref[:, NOPE_DIM:], NT, preferred_element_type=F32)
        p = jnp.exp2(s * ATTN_EXP2_SCALE - lse_ref[:, :1])
        ds = (p * (dp - row_term)).astype(BF16)
        dv_acc[...] += lax.dot_general(p.astype(BF16), dob, TN, preferred_element_type=F32)
        dq = jnp.dot(ds, kcat[...], preferred_element_type=F32) * ATTN_SCALE
        dq_ref[:, :NOPE_DIM] = dq[:, :NOPE_DIM].astype(BF16)
        dq_ref[:, NOPE_DIM:] = _rope_bwd(dq[:, NOPE_DIM:], cos_ref[...], sin_ref[...]).astype(BF16)
        dk_acc[...] += lax.dot_general(ds, qb, TN, preferred_element_type=F32)

        @pl.when(i == nq - 1)
        def _():
            dkv_ref[:, :NOPE_DIM] = (dk_acc[:, :NOPE_DIM] * ATTN_SCALE).astype(BF16)
            dkv_ref[:, NOPE_DIM:] = dv_acc[...].astype(BF16)
            dkr_ref[...] += dk_acc[:, NOPE_DIM:] * ATTN_SCALE

    return pl.pallas_call(
        body, name="attn_bwd", grid=(heads, nq),
        in_specs=[pl.BlockSpec((tq, HEAD_PAD), lambda h, i: (i, h)),
                  pl.BlockSpec((t, HEAD_PAD), lambda h, i: (0, h)),
                  pl.BlockSpec((t, LANES), lambda h, i: (0, 0)),
                  pl.BlockSpec((tq, V_DIM), lambda h, i: (i, h)),
                  pl.BlockSpec((tq, V_DIM), lambda h, i: (i, h)),
                  pl.BlockSpec((tq, V_DIM), lambda h, i: (i, h)),
                  pl.BlockSpec((tq, LANES), lambda h, i: (i, 0)),
                  pl.BlockSpec((tq, LANES), lambda h, i: (i, 0))],
        out_specs=[pl.BlockSpec((tq, HEAD_PAD), lambda h, i: (i, h)),
                   pl.BlockSpec((t, HEAD_PAD), lambda h, i: (0, h)),
                   pl.BlockSpec((t, LANES), lambda h, i: (0, 0))],
        out_shape=[jax.ShapeDtypeStruct((t, heads * HEAD_PAD), BF16),
                   jax.ShapeDtypeStruct((t, heads * HEAD_PAD), BF16),
                   jax.ShapeDtypeStruct((t, LANES), F32)],
        scratch_shapes=[pltpu.VMEM((t, HEAD_PAD), BF16), pltpu.VMEM((t, HEAD_PAD), F32), pltpu.VMEM((t, V_DIM), F32)],
        compiler_params=_params(("arbitrary", "arbitrary")),
    )(q, kv, kr, out, lse, d_out, cos_t, sin_t)


def _layer_norm_parts(x):
    mu = jnp.mean(x, axis=-1, keepdims=True)
    xc = x - mu
    r = lax.rsqrt(jnp.mean(xc * xc, axis=-1, keepdims=True) + EPS)
    return xc * r, r


def _gmlp_fwd(proj, ln_g, ln_b, w_s, b_sb, g_out_norm, lay):
    gw = lay["gw"]
    g_heads = gw // G_HEAD_DIM

    def body(u_ref, v_ref, lng_ref, lnb_ref, ws_ref, bs_ref, gn_ref, o_ref, gate_ref):
        gu = _gelu(u_ref[...])
        vh, _ = _layer_norm_parts(_gelu(v_ref[...]))
        vln = (vh * lng_ref[...] + lnb_ref[...]).astype(BF16)
        for g in range(g_heads):
            cols = slice(g * G_HEAD_DIM, (g + 1) * G_HEAD_DIM)
            s = jnp.dot(ws_ref[g], vln[:, cols], preferred_element_type=F32) + bs_ref[g]
            gate_ref[:, cols] = gu[:, cols] * s
        o_ref[...] = _rms_fwd(gate_ref[...], gn_ref[...]).astype(BF16)

    t = proj.shape[0]
    in_specs = [pl.BlockSpec((CHUNK, gw), lambda i: (i, 0)), pl.BlockSpec((CHUNK, gw), lambda i: (i, 1))]
    pars = [ln_g, ln_b, w_s, b_sb, g_out_norm]
    in_specs += [pl.BlockSpec(a.shape, functools.partial(lambda i, nd: (0,) * nd, nd=a.ndim)) for a in pars]
    return pl.pallas_call(
        body, name="gmlp_fwd", grid=(t // CHUNK,), in_specs=in_specs,
        out_specs=pl.BlockSpec((CHUNK, gw), lambda i: (i, 0)),
        out_shape=jax.ShapeDtypeStruct((t, gw), BF16),
        scratch_shapes=[pltpu.VMEM((CHUNK, gw), F32)],
        compiler_params=_params(("parallel",)),
    )(proj, proj, *pars)


def _gmlp_bwd(proj, d_mixed, ln_g, ln_b, w_s, w_st, b_sb, g_out_norm, lay):
    gw = lay["gw"]
    g_heads = gw // G_HEAD_DIM
    aw_blocks = lay["aw"] // gw

    def body(u_ref, v_ref, dm_ref, lng_ref, lnb_ref, ws_ref, wst_ref, bs_ref, gn_ref,
             du_ref, dv_ref, dgn_ref, dlng_ref, dlnb_ref, dws_ref, dbs_ref, gate_ref, s_ref, dvln_ref):
        i = pl.program_id(0)
        u, v = u_ref[...], v_ref[...]
        (gu, gelu_du), (gv, gelu_dv) = _gelu_and_grad(u), _gelu_and_grad(v)
        vh, r_ln = _layer_norm_parts(gv)
        vln = (vh * lng_ref[...] + lnb_ref[...]).astype(BF16)
        for g in range(g_heads):
            cols = slice(g * G_HEAD_DIM, (g + 1) * G_HEAD_DIM)
            s = jnp.dot(ws_ref[g], vln[:, cols], preferred_element_type=F32) + bs_ref[g]
            s_ref[:, cols] = s
            gate_ref[:, cols] = gu[:, cols] * s
        d_gate, dgn = _rms_bwd(gate_ref[...], gn_ref[...], dm_ref[...])
        _accumulate(dgn_ref, _colsum(dgn))
        du_ref[...] = (d_gate * s_ref[...] * gelu_du).astype(BF16)
        d_s = d_gate * gu
        d_sb = d_s.astype(BF16)
        for g in range(g_heads):
            cols = slice(g * G_HEAD_DIM, (g + 1) * G_HEAD_DIM)
            dw = lax.dot_general(d_sb[:, cols], vln[:, cols], NT, preferred_element_type=F32)

            @pl.when(i == 0)
            def _():
                dws_ref[g] = dw
                dbs_ref[g] = d_s[:, cols]

            @pl.when(i > 0)
            def _():
                dws_ref[g] += dw
                dbs_ref[g] += d_s[:, cols]

            dvln_ref[:, cols] = jnp.dot(wst_ref[g], d_sb[:, cols], preferred_element_type=F32)
        d_vln = dvln_ref[...]
        _accumulate(dlng_ref, _colsum(d_vln * vh))
        _accumulate(dlnb_ref, _colsum(d_vln))
        d_vh = d_vln * lng_ref[...]
        d_gv = r_ln * (d_vh - jnp.mean(d_vh, axis=-1, keepdims=True)
                       - vh * jnp.mean(d_vh * vh, axis=-1, keepdims=True))
        dv_ref[...] = (d_gv * gelu_dv).astype(BF16)

    t = proj.shape[0]
    whole = lambda a: pl.BlockSpec(a.shape, functools.partial(lambda i, nd: (0,) * nd, nd=a.ndim))
    pars = [ln_g, ln_b, w_s, w_st, b_sb, g_out_norm]
    hshape = (g_heads, CHUNK, CHUNK)
    return pl.pallas_call(
        body, name="gmlp_bwd", grid=(t // CHUNK,),
        in_specs=[pl.BlockSpec((CHUNK, gw), lambda i: (i, 0)), pl.BlockSpec((CHUNK, gw), lambda i: (i, 1)),
                  pl.BlockSpec((CHUNK, gw), lambda i: (i, aw_blocks))] + [whole(a) for a in pars],
        out_specs=[pl.BlockSpec((CHUNK, gw), lambda i: (i, 0)), pl.BlockSpec((CHUNK, gw), lambda i: (i, 0)),
                   pl.BlockSpec((1, gw), lambda i: (0, 0)), pl.BlockSpec((1, gw), lambda i: (0, 0)),
                   pl.BlockSpec((1, gw), lambda i: (0, 0)),
                   pl.BlockSpec(hshape, lambda i: (0, 0, 0)), pl.BlockSpec(hshape, lambda i: (0, 0, 0))],
        out_shape=[jax.ShapeDtypeStruct((t, gw), BF16), jax.ShapeDtypeStruct((t, gw), BF16),
                   jax.ShapeDtypeStruct((1, gw), F32), jax.ShapeDtypeStruct((1, gw), F32),
                   jax.ShapeDtypeStruct((1, gw), F32),
                   jax.ShapeDtypeStruct(hshape, F32), jax.ShapeDtypeStruct(hshape, F32)],
        scratch_shapes=[pltpu.VMEM((CHUNK, gw), F32), pltpu.VMEM((CHUNK, gw), F32), pltpu.VMEM((CHUNK, gw), F32)],
        compiler_params=_params(("arbitrary",)),
    )(proj, proj, d_mixed, *pars)


def _spatial_bias_grad(dbs_wide):
    g_heads = dbs_wide.shape[0]

    def body(x_ref, o_ref):
        for g in range(g_heads):
            o_ref[g:g + 1, :] = jnp.sum(x_ref[g].T, axis=0, keepdims=True)

    return pl.pallas_call(
        body, name="spatial_bias_grad", out_shape=jax.ShapeDtypeStruct((g_heads, CHUNK), F32),
        in_specs=[pl.BlockSpec(memory_space=pltpu.VMEM)], out_specs=pl.BlockSpec(memory_space=pltpu.VMEM),
    )(dbs_wide)


def _mix_norm(a_out, gn, g_a, tr, token):
    aw = a_out.shape[1]
    gw = gn.shape[1]

    def body(a_ref, gn_ref, g_ref, token_ref, o_ref):
        o_ref[:, :aw] = _rms_fwd(a_ref[...], g_ref[...]).astype(BF16)
        o_ref[:, aw:] = gn_ref[...]

    t = a_out.shape[0]
    return _row_call(body, "mix_norm", t, tr, [(a_out, aw, 0), (gn, gw, 0)], [g_a, token], [(aw + gw, BF16)], [])[0]


def _mix_norm_bwd(a_out, d_mixed, g_a, tr, token):
    aw = a_out.shape[1]

    def body(a_ref, dm_ref, g_ref, token_ref, da_ref, dg_ref):
        dx, dg = _rms_bwd(a_ref[...], g_ref[...], dm_ref[...])
        da_ref[...] = dx.astype(BF16)
        _accumulate(dg_ref, _colsum(dg))

    t = a_out.shape[0]
    return _row_call(body, "mix_norm_bwd", t, tr, [(a_out, aw, 0), (d_mixed, aw, 0)], [g_a, token],
                     [(aw, BF16)], [((1, aw), F32)])


def _post_mix(x, mix_out, g_pm, g_pf, tr):
    def body(x_ref, mo_ref, gpm_ref, gpf_ref, h_ref, hn_ref):
        h = x_ref[...] + _rms_fwd(mo_ref[...], gpm_ref[...])
        h_ref[...] = h
        hn_ref[...] = _rms_fwd(h, gpf_ref[...]).astype(BF16)

    t, d = x.shape
    return _row_call(body, "post_mix", t, tr, [(x, d, 0), (mix_out, d, 0)], [g_pm, g_pf], [(d, F32), (d, BF16)], [])


def _post_mix_bwd(mix_out, h, dy, d_hn, g_pm, g_pf, tr, token):
    def body(mo_ref, h_ref, dy_ref, dhn_ref, gpm_ref, gpf_ref, token_ref, dh_ref, dmo_ref, dgpf_ref, dgpm_ref):
        dx, dg = _rms_bwd(h_ref[...], gpf_ref[...], dhn_ref[...])
        dh = dy_ref[...] + dx
        dh_ref[...] = dh
        _accumulate(dgpf_ref, _colsum(dg))
        dmo, dg2 = _rms_bwd(mo_ref[...], gpm_ref[...], dh)
        dmo_ref[...] = dmo.astype(BF16)
        _accumulate(dgpm_ref, _colsum(dg2))

    t, d = h.shape
    return _row_call(body, "post_mix_bwd", t, tr, [(mix_out, d, 0), (h, d, 0), (dy, d, 0), (d_hn, d, 0)],
                     [g_pm, g_pf, token], [(d, F32), (d, BF16)], [((1, d), F32), ((1, d), F32)])


def _swiglu(gate, up):
    t, f = gate.shape
    tr, tf = _tile(t, 512), _tile(f, 2048)

    def body(g_ref, u_ref, o_ref):
        g = g_ref[...].astype(F32)
        o_ref[...] = (g * _sigmoid(g) * u_ref[...].astype(F32)).astype(BF16)

    spec = pl.BlockSpec((tr, tf), lambda i, j: (i, j))
    return pl.pallas_call(body, name="swiglu", grid=(t // tr, f // tf), in_specs=[spec, spec], out_specs=spec,
                          out_shape=jax.ShapeDtypeStruct((t, f), BF16),
                          compiler_params=_params(("parallel", "parallel")))(gate, up)


def _swiglu_bwd(gate, up, d_act):
    t, f = gate.shape
    tr, tf = _tile(t, 512), _tile(f, 2048)

    def body(g_ref, u_ref, da_ref, dg_ref, du_ref):
        g, u, da = g_ref[...].astype(F32), u_ref[...].astype(F32), da_ref[...].astype(F32)
        sg = _sigmoid(g)
        du_ref[...] = (da * (g * sg)).astype(BF16)
        dg_ref[...] = (da * u * (sg * (1.0 + g * (1.0 - sg)))).astype(BF16)

    spec = pl.BlockSpec((tr, tf), lambda i, j: (i, j))
    shape = jax.ShapeDtypeStruct((t, f), BF16)
    return pl.pallas_call(body, name="swiglu_bwd", grid=(t // tr, f // tf), in_specs=[spec, spec, spec],
                          out_specs=[spec, spec], out_shape=[shape, shape],
                          compiler_params=_params(("parallel", "parallel")))(gate, up, d_act)


def _loss_head(h, ffn, target, g_po, tr):
    t, d = h.shape

    def body(h_ref, f_ref, t_ref, g_ref, dy_ref, df_ref, dg_ref, loss_ref):
        f = f_ref[...]
        err = h_ref[...] + _rms_fwd(f, g_ref[...]) - t_ref[...]
        dy = err * (1.0 / d)
        dy_ref[...] = dy
        df, dg = _rms_bwd(f, g_ref[...], dy)
        df_ref[...] = df.astype(BF16)
        _accumulate(dg_ref, _colsum(dg))
        sq = jnp.sum(_colsum(err * err), axis=-1, keepdims=True) * (0.5 / d)
        _accumulate(loss_ref, jnp.broadcast_to(sq, (1, LANES)))

    return _row_call(body, "loss_head", t, tr, [(h, d, 0), (ffn, d, 0), (target, d, 0)], [g_po],
                     [(d, F32), (d, BF16)], [((1, d), F32), ((1, LANES), F32)])


def _qkv_bwd(proj, d_qn, d_kvn, d_kr, g_q, g_kv, cos_t, sin_t, lay, tr):
    ql, kl = lay["ql"], lay["kl"]

    def body(q_ref, kv_ref, dqn_ref, dkvn_ref, dkr_ref, cos_ref, sin_ref, gq_ref, gkv_ref,
             dq_ref, dkv_ref, dkt_ref, dgq_ref, dgkv_ref):
        dx, dg = _rms_bwd(q_ref[...], gq_ref[...], dqn_ref[...])
        dq_ref[...] = dx.astype(BF16)
        _accumulate(dgq_ref, _colsum(dg))
        dx, dg = _rms_bwd(kv_ref[...], gkv_ref[...], dkvn_ref[...])
        dkv_ref[...] = dx.astype(BF16)
        _accumulate(dgkv_ref, _colsum(dg))
        dkt_ref[...] = _rope_bwd(dkr_ref[...], cos_ref[...], sin_ref[...]).astype(BF16)

    t = proj.shape[0]
    return _row_call(
        body, "qkv_bwd", t, tr,
        [(proj, ql, lay["q_off"] // ql), (proj, kl, lay["kv_off"] // kl), (d_qn, ql, 0), (d_kvn, kl, 0),
         (d_kr, LANES, 0), (cos_t, LANES, 0), (sin_t, LANES, 0)],
        [g_q, g_kv], [(ql, BF16), (kl, BF16), (LANES, BF16)], [((1, ql), F32), ((1, kl), F32)])


def _prenorm_bwd(x, d_xn, dh, g, tr):
    def body(x_ref, dxn_ref, dh_ref, g_ref, gx_ref, dg_ref):
        dx, dg = _rms_bwd(x_ref[...], g_ref[...], dxn_ref[...])
        gx_ref[...] = dh_ref[...] + dx
        _accumulate(dg_ref, _colsum(dg))

    t, d = x.shape
    return _row_call(body, "prenorm_bwd", t, tr, [(x, d, 0), (d_xn, d, 0), (dh, d, 0)], [g],
                     [(d, F32)], [((1, d), F32)])


def _adam_rows(rows, cols, block_elements=256 * 1024):
    cap = max(8, block_elements // cols // 8 * 8)
    tr = min(rows, cap)
    while rows % tr:
        tr -= 8
    return tr


def _adamw(w, g, m, v, name):
    rows, cols = w.shape
    tr = _adam_rows(rows, cols)

    def body(w_ref, g_ref, m_ref, v_ref, d_ref, mo_ref, vo_ref):
        g = g_ref[...]
        m2 = ADAM_B1 * m_ref[...] + (1.0 - ADAM_B1) * g
        v2 = ADAM_B2 * v_ref[...] + (1.0 - ADAM_B2) * (g * g)
        m_hat = m2 / (1.0 - ADAM_B1 ** ADAM_STEP)
        v_hat = v2 / (1.0 - ADAM_B2 ** ADAM_STEP)
        d_ref[...] = -ADAM_LR * (m_hat / (jnp.sqrt(v_hat) + ADAM_EPS) + ADAM_WD * w_ref[...])
        mo_ref[...] = m2
        vo_ref[...] = v2

    spec = pl.BlockSpec((tr, cols), lambda i: (i, 0))
    shape = jax.ShapeDtypeStruct((rows, cols), F32)
    return pl.pallas_call(body, name=name, grid=(rows // tr,), in_specs=[spec] * 4, out_specs=[spec] * 3,
                          out_shape=[shape] * 3, compiler_params=_params(("parallel",)))(w, g, m, v)


def _adamw_halves(w, g_mine, g_theirs, m, v, name):
    rows, cols = w.shape
    rh = g_mine.shape[0]
    tr = _adam_rows(math.gcd(rows, rh), cols)
    per_half = rh // tr
    my_c = jnp.reshape(lax.axis_index("c"), (1,)).astype(jnp.int32)

    def body(c_ref, w_ref, gm_ref, gt_ref, m_ref, v_ref, g_ref, d_ref, mo_ref, vo_ref):
        mine = (pl.program_id(0) // per_half) == c_ref[0]
        g = jnp.where(mine, gm_ref[...], gt_ref[...])
        m2 = ADAM_B1 * m_ref[...] + (1.0 - ADAM_B1) * g
        v2 = ADAM_B2 * v_ref[...] + (1.0 - ADAM_B2) * (g * g)
        m_hat = m2 / (1.0 - ADAM_B1 ** ADAM_STEP)
        v_hat = v2 / (1.0 - ADAM_B2 ** ADAM_STEP)
        g_ref[...] = g
        d_ref[...] = -ADAM_LR * (m_hat / (jnp.sqrt(v_hat) + ADAM_EPS) + ADAM_WD * w_ref[...])
        mo_ref[...] = m2
        vo_ref[...] = v2

    def half_spec(is_mine):
        def index(i, c_ref):
            used = ((i // per_half) == c_ref[0]) if is_mine else ((i // per_half) != c_ref[0])
            return (jnp.where(used, i % per_half, 0), 0)
        return pl.BlockSpec((tr, cols), index)

    spec = pl.BlockSpec((tr, cols), lambda i, c_ref: (i, 0))
    shape = jax.ShapeDtypeStruct((rows, cols), F32)
    grid_spec = pltpu.PrefetchScalarGridSpec(
        num_scalar_prefetch=1, grid=(rows // tr,),
        in_specs=[spec, half_spec(True), half_spec(False), spec, spec], out_specs=[spec] * 4)
    return pl.pallas_call(body, name=name, grid_spec=grid_spec, out_shape=[shape] * 4,
                          compiler_params=_params(("parallel",)))(my_c, w, g_mine, g_theirs, m, v)


def _pair_add(parts, theirs, name):
    _, n, r, c = parts.shape
    tr = _adam_rows(r, c, 1024 * 1024)
    my_c = jnp.reshape(lax.axis_index("c"), (1,)).astype(jnp.int32)

    def body(c_ref, a_ref, b_ref, o_ref):
        o_ref[0] = (a_ref[0, 0].astype(F32) + b_ref[0].astype(F32)).astype(BF16)

    spec = pl.BlockSpec((1, tr, c), lambda k, i, c_ref: (k, i, 0))
    grid_spec = pltpu.PrefetchScalarGridSpec(
        num_scalar_prefetch=1, grid=(n, r // tr),
        in_specs=[pl.BlockSpec((1, 1, tr, c), lambda k, i, c_ref: (c_ref[0], k, i, 0)), spec], out_specs=spec)
    return pl.pallas_call(body, name=name, grid_spec=grid_spec, out_shape=jax.ShapeDtypeStruct((n, r, c), BF16),
                          compiler_params=_params(("parallel", "parallel")))(my_c, parts, theirs)


def _chip_sum(pair_sums, received, name, token=None):
    _, r, c = pair_sums.shape
    tr = _adam_rows(r, c)
    own = 2 * lax.axis_index("x") + lax.axis_index("y")

    def body(own_ref, p_ref, r0_ref, r1_ref, r2_ref, *rest):
        o_ref = rest[-1]
        acc = p_ref[0].astype(F32) + r0_ref[0].astype(F32)
        acc = acc + r1_ref[0].astype(F32)
        o_ref[...] = acc + r2_ref[0].astype(F32)

    def rspec(j):
        return pl.BlockSpec((1, tr, c), functools.partial(lambda i, own_ref, j: (j, i, 0), j=j))

    extra = [] if token is None else [token]
    grid_spec = pltpu.PrefetchScalarGridSpec(
        num_scalar_prefetch=1, grid=(r // tr,),
        in_specs=[pl.BlockSpec((1, tr, c), lambda i, own_ref: (own_ref[0], i, 0)), rspec(0), rspec(1), rspec(2)]
        + [pl.BlockSpec(memory_space=pl.ANY)] * len(extra),
        out_specs=pl.BlockSpec((tr, c), lambda i, own_ref: (i, 0)))
    return pl.pallas_call(body, name=name, grid_spec=grid_spec, out_shape=jax.ShapeDtypeStruct((r, c), F32),
                          compiler_params=_params(("parallel",)))(
        jnp.reshape(own, (1,)).astype(jnp.int32), pair_sums, received, received, received, *extra)


def _mesh_place():
    x, y, c = lax.axis_index("x"), lax.axis_index("y"), lax.axis_index("c")
    other_chips = [(1 - x, y), (x, 1 - y), (1 - x, 1 - y)]
    return x, y, c, other_chips


def _hbm_specs(n):
    return [pl.BlockSpec(memory_space=pltpu.HBM)] * n


def _sibling_exchange(parts, name):
    n = len(parts)

    def body(*refs):
        ins, outs = refs[:n], refs[n:2 * n]
        send_sems, recv_sems = refs[2 * n:]
        x, y, c, _ = _mesh_place()
        copies = [pltpu.make_async_remote_copy(src_ref=ins[w].at[1 - c], dst_ref=outs[w], send_sem=send_sems.at[w],
                                               recv_sem=recv_sems.at[w], device_id=(x, y, 1 - c), device_id_type=MESH)
                  for w in range(n)]
        for cp in copies:
            cp.start()
        for cp in copies:
            cp.wait()

    return pl.pallas_call(
        body, name=name,
        out_shape=[jax.ShapeDtypeStruct(p.shape[1:], p.dtype) for p in parts],
        in_specs=_hbm_specs(n), out_specs=_hbm_specs(n),
        scratch_shapes=[pltpu.SemaphoreType.DMA((n,)), pltpu.SemaphoreType.DMA((n,))],
    )(*parts)


SEM_SPEC = pl.BlockSpec(memory_space=pltpu.SEMAPHORE)
DATAFLOW_EFFECT = pltpu.SideEffectType.DATAFLOW_SIDE_EFFECTING


def _copies_per_weight(kind):
    return {"gather": 4, "scatter": 3, "sibling": 1, "forward": 3, "swap": 1}[kind]


def _flight_copies(kind, src_refs, land_refs, send_sems, recv_sems, arriving):
    x, y, c, other_chips = _mesh_place()
    own = 2 * x + y
    sibling = (x, y, 1 - c)
    per = _copies_per_weight(kind)
    copies = []
    for w in range(len(src_refs)):
        def remote(src, dst, j, to):
            return pltpu.make_async_remote_copy(src_ref=src, dst_ref=dst, send_sem=send_sems.at[per * w + j],
                                                recv_sem=recv_sems.at[per * w + j], device_id=to, device_id_type=MESH)

        if kind == "sibling":
            copies.append(remote(src_refs[w].at[1 - c], land_refs[w], 0, sibling))
            continue
        if kind == "swap":
            copies.append(remote(src_refs[w], land_refs[w], 0, sibling))
            continue
        for j, chip in enumerate(other_chips):
            theirs = 2 * chip[0] + chip[1]
            if kind == "gather":
                copies.append(remote(src_refs[w].at[c], land_refs[w].at[theirs if arriving else own, c], j, (*chip, c)))
            elif kind == "forward":
                copies.append(remote(src_refs[w].at[theirs, c], src_refs[w].at[theirs, (1 - c) if arriving else c],
                                     j, sibling))
            else:
                copies.append(remote(src_refs[w].at[theirs], land_refs[w].at[j], j, (*chip, c)))
        if kind == "gather":
            copies.append(remote(src_refs[w], land_refs[w].at[own], 3, sibling))
    return copies


def _ici_start(kind, srcs, name, after=None):
    n = len(srcs)
    if kind == "gather":
        lands = [lax.empty((4,) + s.shape, s.dtype) for s in srcs]
    elif kind == "scatter":
        lands = [lax.empty((3,) + s.shape[1:], s.dtype) for s in srcs]
    elif kind == "sibling":
        lands = [lax.empty(s.shape[1:], s.dtype) for s in srcs]
    elif kind == "swap":
        lands = [lax.empty(s.shape, s.dtype) for s in srcs]
    else:
        lands = []
    nb = n + len(lands)
    afters = [] if after is None else [after]

    def body(*refs):
        src_refs, land_refs = refs[:n], refs[n:nb]
        send_sems, recv_sems = refs[nb + len(afters)], refs[nb + len(afters) + 1]
        token = refs[-1]
        for cp in _flight_copies(kind, src_refs, land_refs, send_sems, recv_sems, False):
            cp.start()
        token[...] = jnp.zeros_like(token)

    hbm = lambda a: pltpu.with_memory_space_constraint(a, pltpu.HBM)
    n_sems = _copies_per_weight(kind) * n
    out = pl.pallas_call(
        body, name=name,
        out_shape=(pltpu.SemaphoreType.DMA((n_sems,)), pltpu.SemaphoreType.DMA((n_sems,)),
                   *[pltpu.HBM(a.shape, a.dtype) for a in srcs + lands], jax.ShapeDtypeStruct((8, LANES), F32)),
        in_specs=_hbm_specs(nb) + [pl.BlockSpec(memory_space=pl.ANY)] * len(afters),
        out_specs=(SEM_SPEC, SEM_SPEC, *_hbm_specs(nb), pl.BlockSpec(memory_space=pltpu.VMEM)),
        input_output_aliases={i: 2 + i for i in range(nb)},
        compiler_params=pltpu.CompilerParams(has_side_effects=DATAFLOW_EFFECT),
    )(*[hbm(a) for a in srcs + lands], *afters)
    return out[0], out[1], list(out[2:2 + n]), list(out[2 + n:2 + nb]), out[-1]


def _ici_wait(kind, send_sems, recv_sems, srcs, lands, after, name):
    n = len(srcs)
    nb = n + len(lands)

    def body(*refs):
        src_refs, land_refs = refs[:n], refs[n:nb]
        send_ref, recv_ref = refs[nb], refs[nb + 1]
        for cp in _flight_copies(kind, src_refs, land_refs, send_ref, recv_ref, True):
            cp.wait_send()
            cp.wait_recv()

    out = pl.pallas_call(
        body, name=name, out_shape=tuple(pltpu.HBM(a.shape, a.dtype) for a in srcs + lands),
        in_specs=_hbm_specs(nb) + [SEM_SPEC, SEM_SPEC, pl.BlockSpec(memory_space=pl.ANY)],
        out_specs=tuple(_hbm_specs(nb)), input_output_aliases={i: i for i in range(nb)},
        compiler_params=pltpu.CompilerParams(has_side_effects=DATAFLOW_EFFECT),
    )(*srcs, *lands, send_sems, recv_sems, after)
    return list(out[:n]), list(out[n:])


def _halves_exchange(halves, name):
    n = len(halves)

    def body(*refs):
        ins, outs = refs[:n], refs[n:2 * n]
        send_sems, recv_sems = refs[2 * n:]
        x, y, c, _ = _mesh_place()
        copies = [pltpu.make_async_remote_copy(src_ref=ins[w], dst_ref=outs[w], send_sem=send_sems.at[w],
                                               recv_sem=recv_sems.at[w], device_id=(x, y, 1 - c), device_id_type=MESH)
                  for w in range(n)]
        for cp in copies:
            cp.start()
        for cp in copies:
            cp.wait()

    return pl.pallas_call(
        body, name=name,
        out_shape=[jax.ShapeDtypeStruct(h.shape, h.dtype) for h in halves],
        in_specs=_hbm_specs(n), out_specs=_hbm_specs(n),
        scratch_shapes=[pltpu.SemaphoreType.DMA((n,)), pltpu.SemaphoreType.DMA((n,))],
    )(*halves)


def _small_all_reduce(packed):
    rows = packed.shape[0]

    def body(in_ref, out_ref, from_sibling, pair_sums, send_sems, recv_sems):
        x, y, c, other_chips = _mesh_place()
        own = 2 * x + y

        def remote(src, dst, k, to):
            return pltpu.make_async_remote_copy(src_ref=src, dst_ref=dst, send_sem=send_sems.at[k],
                                                recv_sem=recv_sems.at[k], device_id=to, device_id_type=MESH)

        swap = remote(in_ref, from_sibling, 0, (x, y, 1 - c))
        swap.start()
        swap.wait()
        mine, theirs = in_ref[...], from_sibling[...]
        pair_sums[own] = jnp.where(c == 0, mine, theirs) + jnp.where(c == 0, theirs, mine)
        copies = [remote(pair_sums.at[own], pair_sums.at[own], 1 + j, (*chip, c)) for j, chip in enumerate(other_chips)]
        for cp in copies:
            cp.start()
        for j, chip in enumerate(other_chips):
            landing = pair_sums.at[2 * chip[0] + chip[1]]
            remote(landing, landing, 1 + j, (*chip, c)).wait_recv()
        for cp in copies:
            cp.wait_send()
        out_ref[...] = ((pair_sums[0] + pair_sums[1]) + pair_sums[2]) + pair_sums[3]

    return pl.pallas_call(
        body, name="small_all_reduce", out_shape=jax.ShapeDtypeStruct(packed.shape, F32),
        in_specs=[pl.BlockSpec(memory_space=pltpu.VMEM)], out_specs=pl.BlockSpec(memory_space=pltpu.VMEM),
        scratch_shapes=[pltpu.VMEM((rows, LANES), F32), pltpu.VMEM((4, rows, LANES), F32),
                        pltpu.SemaphoreType.DMA((4,)), pltpu.SemaphoreType.DMA((4,))],
        compiler_params=pltpu.CompilerParams(vmem_limit_bytes=VMEM_LIMIT_BYTES),
    )(packed)


def _layout(w_in, w_uq, w_ukv, v_ln_gain, q_norm, kv_norm):
    heads = 4 * w_uq.shape[-1] // (NOPE_DIM + ROPE_DIM)
    gw = v_ln_gain.shape[-1]
    ql, kl = q_norm.shape[-1], kv_norm.shape[-1]
    lay = dict(heads=heads, gw=gw, ql=ql, kl=kl, aw=heads * V_DIM, u_off=0, v_off=gw, q_off=2 * gw,
               kv_off=2 * gw + ql, kr_off=2 * gw + ql + kl)
    lay["in_pad"] = _round_up(lay["kr_off"] + LANES, 2 * LANES if lay["kr_off"] + LANES <= 2048 else 1024)
    assert lay["q_off"] % ql == 0 and lay["kv_off"] % kl == 0 and lay["aw"] % gw == 0
    assert 4 * w_in.shape[-1] == ql + kl + ROPE_DIM + 2 * gw
    return lay


def _rope_tile(t1, t2, axis=-1):
    z = jnp.zeros_like(t1)
    return jnp.concatenate([t1, z, t2, z], axis=axis)


def _w_in_rows(gathered, shard_rows, lay):
    d = gathered.shape[-1]
    wt = gathered[:, :shard_rows].reshape(4 * shard_rows, d)
    ql, kl, gw = lay["ql"], lay["kl"], lay["gw"]
    q_c, kv_c = wt[:ql], wt[ql:ql + kl]
    r = wt[ql + kl:ql + kl + ROPE_DIM]
    u = wt[ql + kl + ROPE_DIM:ql + kl + ROPE_DIM + gw]
    v = wt[ql + kl + ROPE_DIM + gw:]
    parts = [u, v, q_c, kv_c, _rope_tile(r[:ROPE_HALF], r[ROPE_HALF:], axis=0)]
    pad = lay["in_pad"] - (lay["kr_off"] + LANES)
    if pad:
        parts.append(jnp.zeros((pad, d), wt.dtype))
    return jnp.concatenate(parts, axis=0)


def _w_in_grad_chunks(dwt, shard_rows, padded_rows, lay):
    d = dwt.shape[-1]
    ql, kl, gw = lay["ql"], lay["kl"], lay["gw"]
    ko = lay["kr_off"]
    rows = jnp.concatenate([dwt[lay["q_off"]:lay["q_off"] + ql], dwt[lay["kv_off"]:lay["kv_off"] + kl],
                            dwt[ko:ko + ROPE_HALF], dwt[ko + 2 * ROPE_HALF:ko + 3 * ROPE_HALF],
                            dwt[:gw], dwt[gw:2 * gw]], axis=0).reshape(4, shard_rows, d)
    rows = jnp.pad(rows, ((0, 0), (0, padded_rows - shard_rows), (0, 0)))
    return jnp.transpose(rows.reshape(4, 2, padded_rows // 2, d), (1, 0, 2, 3)).astype(BF16)


def _w_uq_padded(w, heads):
    w3 = w.reshape(w.shape[0], heads, NOPE_DIM + ROPE_DIM)
    t = _rope_tile(w3[..., NOPE_DIM:NOPE_DIM + ROPE_HALF], w3[..., NOPE_DIM + ROPE_HALF:])
    return jnp.concatenate([w3[..., :NOPE_DIM], t], axis=-1).reshape(w.shape[0], heads * HEAD_PAD)


def _w_uq_grad_unpadded(dw, heads):
    d3 = dw.reshape(dw.shape[0], heads, HEAD_PAD)
    return jnp.concatenate([d3[..., :NOPE_DIM], d3[..., NOPE_DIM:NOPE_DIM + ROPE_HALF],
                            d3[..., NOPE_DIM + 2 * ROPE_HALF:NOPE_DIM + 3 * ROPE_HALF]],
                           axis=-1).reshape(dw.shape[0], heads * (NOPE_DIM + ROPE_DIM))


def _cols_gathered(g):
    return jnp.transpose(g, (1, 0, 2)).reshape(g.shape[1], 4 * g.shape[2])


def _chunks_of_cols(grad):
    r, c4 = grad.shape
    return jnp.transpose(grad.reshape(2, r // 2, 4, c4 // 4), (0, 2, 1, 3)).astype(BF16)


SMALL = ["pre_mix_norm", "q_norm", "kv_norm", "v_ln_gain", "v_ln_bias", "w_spatial", "b_spatial", "attn_out_norm",
         "gmlp_out_norm", "post_mix_norm", "pre_ffn_norm", "post_ffn_norm"]
BIG = ["w_in", "w_uq", "w_ukv", "w_out", "w_gate", "w_up", "w_down"]
GATHER_NOW = ["w_in"]
GATHER_SOON = ["w_uq", "w_ukv"]
GATHER_LATER = ["w_out", "w_gate", "w_up", "w_down"]
REDUCE_FFN = ["w_gate", "w_up", "w_down"]
REDUCE_OUT = ["w_out"]
REDUCE_LAST = ["w_in", "w_uq", "w_ukv"]
TRANSPOSED = ("w_in", "w_gate", "w_up")
ORDER = ["pre_mix_norm", "w_in", "q_norm", "kv_norm", "w_uq", "w_ukv", "v_ln_gain", "v_ln_bias", "w_spatial",
         "b_spatial", "attn_out_norm", "gmlp_out_norm", "w_out", "post_mix_norm", "pre_ffn_norm", "w_gate", "w_up",
         "w_down", "post_ffn_norm"]


def _pack(arrays):
    flat = jnp.concatenate([a.reshape(-1) for a in arrays])
    n = flat.shape[0]
    total = _round_up(n, 8 * LANES)
    if total > n:
        flat = jnp.concatenate([flat, jnp.zeros((total - n,), F32)])
    return flat.reshape(total // LANES, LANES)


def _unpack(packed, like):
    flat = packed.reshape(-1)
    out, off = [], 0
    for a in like:
        out.append(flat[off:off + a.size].reshape(a.shape))
        off += a.size
    return out


def kernel(x, positions, pre_mix_norm, w_in, q_norm, kv_norm, w_uq, w_ukv, v_ln_gain, v_ln_bias, w_spatial, b_spatial, attn_out_norm, gmlp_out_norm, w_out, post_mix_norm, pre_ffn_norm, w_gate, w_up, w_down, post_ffn_norm, loss_target, m_pre_mix_norm, m_w_in, m_q_norm, m_kv_norm, m_w_uq, m_w_ukv, m_v_ln_gain, m_v_ln_bias, m_w_spatial, m_b_spatial, m_attn_out_norm, m_gmlp_out_norm, m_w_out, m_post_mix_norm, m_pre_ffn_norm, m_w_gate, m_w_up, m_w_down, m_post_ffn_norm, v_pre_mix_norm, v_w_in, v_q_norm, v_kv_norm, v_w_uq, v_w_ukv, v_v_ln_gain, v_v_ln_bias, v_w_spatial, v_b_spatial, v_attn_out_norm, v_gmlp_out_norm, v_w_out, v_post_mix_norm, v_pre_ffn_norm, v_w_gate, v_w_up, v_w_down, v_post_ffn_norm):
    args = dict(locals())
    weights = {n: args[n] for n in ORDER}
    m_in = {n: args["m_" + n] for n in ORDER}
    v_in = {n: args["v_" + n] for n in ORDER}

    lay = _layout(w_in, w_uq, w_ukv, v_ln_gain, q_norm, kv_norm)
    heads, gw = lay["heads"], lay["gw"]
    t, d = x.shape[1], x.shape[2]
    tr = 128 if t % 128 == 0 else t
    xs = x.reshape(t, d)
    target = loss_target.reshape(t, d)

    ffs, ins = w_gate.shape[-1], w_in.shape[-1]
    ffp, inp = _round_up(ffs, LANES), _round_up(ins, LANES)
    shards = {n: (jnp.swapaxes(weights[n][0], 0, 1) if n in TRANSPOSED else weights[n][0]).astype(BF16)
              for n in BIG}
    for n, rows in (("w_gate", ffp), ("w_up", ffp), ("w_down", ffp), ("w_in", inp)):
        shards[n] = jnp.pad(shards[n], ((0, rows - shards[n].shape[0]), (0, 0)))
    halved = {n: shards[n].reshape(2, shards[n].shape[0] // 2, shards[n].shape[1]) for n in BIG}
    full = {}

    def pair_sums_of(partial, names, tag):
        from_sibling = _sibling_exchange([partial[n] for n in names], "grads_sibling_exchange_" + tag)
        return [_pair_add(partial[n], r, "pair_add_" + n) for n, r in zip(names, from_sibling)]

    def place(names, lands):
        for n, g in zip(names, lands):
            full[n] = g.reshape((4,) + shards[n].shape)

    flight_0 = _ici_start("gather", [halved[n] for n in GATHER_NOW], "gather_start_0")
    flight_soon = _ici_start("gather", [halved[n] for n in GATHER_SOON], "gather_start_soon", after=flight_0[4])
    flights, last_token = {}, flight_soon[4]
    for n in GATHER_LATER:
        flights[n] = _ici_start("gather", [halved[n]], "gather_start_" + n, after=last_token)
        last_token = flights[n][4]
    xn = _prenorm(xs, pre_mix_norm.reshape(1, -1), tr, last_token)
    _, lands = _ici_wait("gather", *flight_0[:4], xn, "gather_wait_0")
    pass_now = _ici_start("forward", lands, "forward_start_now")
    place(GATHER_NOW, _ici_wait("forward", *pass_now[:4], pass_now[4], "forward_wait_now")[0])
    wt_in = _w_in_rows(full["w_in"], ins, lay)

    inv_freq = 1.0 / (ROPE_THETA ** (jnp.arange(0, ROPE_DIM, 2, dtype=F32) / ROPE_DIM))
    ang = positions.reshape(t).astype(F32)[:, None] * inv_freq
    cos, sin = jnp.cos(ang), jnp.sin(ang)
    cos_t = _rope_tile(cos, cos)
    sin_t = _rope_tile(-sin, sin)

    row = lambda a: a.reshape(1, -1)
    g_pre, g_q, g_kv = row(pre_mix_norm), row(q_norm), row(kv_norm)
    g_a, g_g, g_pm = row(attn_out_norm), row(gmlp_out_norm), row(post_mix_norm)
    g_pf, g_po = row(pre_ffn_norm), row(post_ffn_norm)
    ln_g, ln_b = row(v_ln_gain), row(v_ln_bias)
    ws = w_spatial[0].astype(BF16)
    ws_t = jnp.transpose(ws, (0, 2, 1))
    bs_wide = jnp.broadcast_to(b_spatial[0][:, :, None], b_spatial.shape[1:] + (G_HEAD_DIM,))

    proj = _matmul(xn, wt_in, NT, F32, "proj")
    _, lands = _ici_wait("gather", *flight_soon[:4], proj, "gather_wait_soon")
    pass_soon = _ici_start("forward", lands, "forward_start_soon")
    qn, kvn, kr = _qkv_prep(proj, g_q, g_kv, cos_t, sin_t, lay, tr, pass_soon[4])
    place(GATHER_SOON, _ici_wait("forward", *pass_soon[:4], qn, "forward_wait_soon")[0])
    wb_uq = _w_uq_padded(_cols_gathered(full["w_uq"]), heads)
    wb_ukv = _cols_gathered(full["w_ukv"])
    q = _q_rope(_matmul(qn, wb_uq, NN, F32, "q_up"), cos_t, sin_t, heads, tr)
    kv = _matmul(kvn, wb_ukv, NN, BF16, "kv_up")
    a_out, a_lse = _attn_fwd(q, kv, kr, heads)
    def arrive(n, after):
        _, lands = _ici_wait("gather", *flights[n][:4], after, "gather_wait_" + n)
        return _ici_start("forward", lands, "forward_start_" + n)

    def settle(n, passing, after):
        place([n], _ici_wait("forward", *passing[:4], after, "forward_wait_" + n)[0])

    pass_out = arrive("w_out", a_out)
    gn = _gmlp_fwd(proj, ln_g, ln_b, ws, bs_wide, g_g, lay)
    mixed = _mix_norm(a_out, gn, g_a, tr, pass_out[4])
    settle("w_out", pass_out, mixed)
    pass_gate = arrive("w_gate", mixed)
    wb_out = full["w_out"].reshape(-1, d)
    mix_out = _matmul(mixed, wb_out, NN, F32, "mix_out", token=pass_gate[4])
    h, hn = _post_mix(xs, mix_out, g_pm, g_pf, tr)
    settle("w_gate", pass_gate, hn)
    pass_up = arrive("w_up", hn)
    wt_gate = full["w_gate"].reshape(4 * ffp, d)
    gate = _matmul(hn, wt_gate, NT, BF16, "ffn_gate", token=pass_up[4])
    settle("w_up", pass_up, gate)
    pass_down = arrive("w_down", gate)
    wt_up = full["w_up"].reshape(4 * ffp, d)
    up = _matmul(hn, wt_up, NT, BF16, "ffn_up", token=pass_down[4])
    act = _swiglu(gate, up)
    settle("w_down", pass_down, act)
    wb_down = full["w_down"].reshape(4 * ffp, d)
    ffn = _matmul(act, wb_down, NN, F32, "ffn_down")
    dy, d_ffn, dg_po, loss_vec = _loss_head(h, ffn, target, g_po, tr)

    d_act = _matmul(d_ffn, wb_down, NT, BF16, "d_act")
    d_gate, d_up = _swiglu_bwd(gate, up, d_act)
    partial_ffn = [_matmul(d_gate, hn, TN, BF16, "gw_gate", out_chunks=True),
                   _matmul(d_up, hn, TN, BF16, "gw_up", out_chunks=True),
                   _matmul(act, d_ffn, TN, BF16, "gw_down", out_chunks=True)]
    swap_ffn = _ici_start("sibling", partial_ffn, "sibling_start_ffn")
    d_hn = _matmul(d_up, wt_up, NN, F32, "d_hn",
                   extras=[_matmul(d_gate, wt_gate, NN, F32, "d_hn_gate", token=swap_ffn[4])],
                   epilogue=lambda acc, partial: acc + partial)
    partial_ffn, from_sibling = _ici_wait("sibling", *swap_ffn[:4], d_hn, "sibling_wait_ffn")
    pair_ffn = [_pair_add(p, r, "pair_add_" + n) for n, p, r in zip(REDUCE_FFN, partial_ffn, from_sibling)]
    flight_ffn = _ici_start("scatter", pair_ffn, "scatter_start_ffn")
    dh, d_mo, dg_pf, dg_pm = _post_mix_bwd(mix_out, h, dy, d_hn, g_pm, g_pf, tr, flight_ffn[4])
    d_mixed = _matmul(d_mo, wb_out, NT, F32, "d_mixed")
    gw_out = _matmul(mixed, d_mo, TN, BF16, "gw_out", out_chunks=True)
    pair_out = pair_sums_of({"w_out": gw_out}, REDUCE_OUT, "out")
    flight_out = _ici_start("scatter", pair_out, "scatter_start_out")
    d_a, dg_a = _mix_norm_bwd(a_out, d_mixed, g_a, tr, flight_out[4])
    d_u, d_v, dg_g, d_ln_g, d_ln_b, d_ws, d_bs_wide = _gmlp_bwd(proj, d_mixed, ln_g, ln_b, ws, ws_t, bs_wide, g_g, lay)
    d_bs = _spatial_bias_grad(d_bs_wide)
    d_q, d_kv, d_kr = _attn_bwd(q, kv, kr, a_out, a_lse, d_a, cos_t, sin_t, heads)
    d_qn = _matmul(d_q, wb_uq, NT, F32, "d_qn")
    gw_uq = _matmul(qn, d_q, TN, BF16, "gw_uq")
    d_kvn = _matmul(d_kv, wb_ukv, NT, F32, "d_kvn")
    gw_ukv = _matmul(kvn, d_kv, TN, BF16, "gw_ukv")
    d_qc, d_kvc, d_krt, dg_q, dg_kv = _qkv_bwd(proj, d_qn, d_kvn, d_kr, g_q, g_kv, cos_t, sin_t, lay, tr)
    parts = [d_u, d_v, d_qc, d_kvc, d_krt]
    pad = lay["in_pad"] - (lay["kr_off"] + LANES)
    if pad:
        parts.append(jnp.zeros((t, pad), BF16))
    d_proj = jnp.concatenate(parts, axis=1)
    d_xn = _matmul(d_proj, wt_in, NN, F32, "d_xn")
    gw_in = _matmul(d_proj, xn, TN, BF16, "gw_in")
    grad_x, dg_pre = _prenorm_bwd(xs, d_xn, dh, g_pre, tr)

    pair_mix = pair_sums_of({"w_in": _w_in_grad_chunks(gw_in, ins, inp, lay),
                             "w_uq": _chunks_of_cols(_w_uq_grad_unpadded(gw_uq, heads)),
                             "w_ukv": _chunks_of_cols(gw_ukv)}, REDUCE_LAST, "mix")
    pair_ffn, received_ffn = _ici_wait("scatter", *flight_ffn[:4], pair_mix[-1], "scatter_wait_ffn")
    pair_out, received_out = _ici_wait("scatter", *flight_out[:4], pair_mix[-1], "scatter_wait_out")
    flight_mix = _ici_start("scatter", pair_mix, "scatter_start_mix")
    grads, delta, new_m, new_v = {}, {}, {}, {}

    def finish(names, pair_sums, received, tag, token, in_flight):
        mine, swaps = [], []
        for n, p, r in zip(names, pair_sums, received):
            mine.append(_chip_sum(p, r, "chip_sum_" + n, token))
            if in_flight:
                swaps.append(_ici_start("swap", [mine[-1]], "swap_start_" + n))
                token = swaps[-1][4]
        if not in_flight:
            theirs = _halves_exchange(mine, "grads_halves_exchange_" + tag)
        previous = mine[-1]
        for i, n in enumerate(names):
            if in_flight:
                (g_mine,), (g_theirs,) = _ici_wait("swap", *swaps[i][:4], previous, "swap_wait_" + n)
            else:
                g_mine, g_theirs = mine[i], theirs[i]
            shape = weights[n].shape
            if n in TRANSPOSED:
                view = lambda a: jnp.swapaxes(a[0], 0, 1)
                back = lambda o: jnp.swapaxes(o, 0, 1).reshape(shape)
            else:
                view = lambda a: a[0]
                back = lambda o: o.reshape(shape)
            out = _adamw_halves(view(weights[n]), g_mine, g_theirs, view(m_in[n]), view(v_in[n]), "adamw_" + n)
            grads[n], delta[n], new_m[n], new_v[n] = [back(o) for o in out]
            previous = out[-1]

    finish(REDUCE_FFN + REDUCE_OUT, pair_ffn + pair_out, received_ffn + received_out, "ffn", flight_mix[4], True)
    pair_mix, received = _ici_wait("scatter", *flight_mix[:4], new_v[REDUCE_OUT[-1]], "scatter_wait_mix")
    finish(REDUCE_LAST, pair_mix, received, "mix", None, False)

    small_grads = {"pre_mix_norm": dg_pre, "q_norm": dg_q, "kv_norm": dg_kv, "v_ln_gain": d_ln_g, "v_ln_bias": d_ln_b,
                   "w_spatial": d_ws, "b_spatial": d_bs, "attn_out_norm": dg_a, "gmlp_out_norm": dg_g,
                   "post_mix_norm": dg_pm, "pre_ffn_norm": dg_pf, "post_ffn_norm": dg_po}
    like = [weights[n] for n in SMALL]
    reduced = _small_all_reduce(_pack([small_grads[n] for n in SMALL] + [loss_vec]))
    loss = reduced.reshape(-1)[sum(a.size for a in like)]
    small_g = _pack(_unpack(reduced, like))
    s_delta, s_m, s_v = _adamw(_pack(like), small_g, _pack([m_in[n] for n in SMALL]),
                               _pack([v_in[n] for n in SMALL]), "adamw_small")
    for n, g in zip(SMALL, _unpack(small_g, like)):
        grads[n] = g
    delta.update(zip(SMALL, _unpack(s_delta, like)))
    new_m.update(zip(SMALL, _unpack(s_m, like)))
    new_v.update(zip(SMALL, _unpack(s_v, like)))

    return (loss, grad_x.reshape(x.shape), *[grads[n] for n in ORDER], *[delta[n] for n in ORDER],
            *[new_m[n] for n in ORDER], *[new_v[n] for n in ORDER])
```

```python
import functools
import math

import jax
import jax.numpy as jnp
from jax import lax
from jax.experimental import pallas as pl
from jax.experimental.pallas import tpu as pltpu

F32 = jnp.float32
BF16 = jnp.bfloat16
MESH = pl.DeviceIdType.MESH

NOPE_DIM = 128
ROPE_DIM = 64
ROPE_HALF = ROPE_DIM // 2
V_DIM = 128
HEAD_PAD = 256
G_HEAD_DIM = 128
CHUNK = 128
ROPE_THETA = 10000.0
EPS = 1e-6
ADAM_LR = 0.001
ADAM_B1 = 0.9
ADAM_B2 = 0.999
ADAM_EPS = 1e-08
ADAM_WD = 0.01
ADAM_STEP = 10

LANES = 128
MATMUL_TILE = 1024
WIDE_TILE = 1408
VMEM_LIMIT_BYTES = 48 * 1024 * 1024

NN = (((1,), (0,)), ((), ()))
NT = (((1,), (1,)), ((), ()))
TN = (((0,), (0,)), ((), ()))


def _params(semantics):
    return pltpu.CompilerParams(dimension_semantics=semantics, vmem_limit_bytes=VMEM_LIMIT_BYTES)


def _tile(n, cap=MATMUL_TILE):
    if n <= cap:
        return n
    if cap == MATMUL_TILE and n % WIDE_TILE == 0:
        return WIDE_TILE
    t = cap - cap % LANES
    while n % t:
        t -= LANES
    assert t > 0, n
    return t


def _round_up(n, m):
    return (n + m - 1) // m * m


def _matmul(a, b, dims, out_dtype, name, extras=(), epilogue=None, out_chunks=None, token=None):
    if dims is NN:
        (m, k), (k2, n) = a.shape, b.shape
    elif dims is NT:
        (m, k), (n, k2) = a.shape, b.shape
    else:
        (k, m), (k2, n) = a.shape, b.shape
    assert k == k2, (a.shape, b.shape, name)
    tm, tn, tk = _tile(m // 8 if out_chunks else m), _tile(n), _tile(k, 2 * MATMUL_TILE)
    if len(extras) + (len(out_dtype) if isinstance(out_dtype, tuple) else 1) > 2:
        tm = _tile(m, MATMUL_TILE // 2)
    if not extras and k % (2 * WIDE_TILE) == 0:
        tk = 2 * WIDE_TILE
    nk = k // tk

    out_dtypes = out_dtype if isinstance(out_dtype, tuple) else (out_dtype,)
    n_extra = len(extras)

    def body(*refs):
        a_ref, b_ref = refs[:2]
        extra_refs = refs[2:2 + n_extra]
        out_refs = refs[2 + n_extra + (token is not None):-1]
        acc_ref = refs[-1]
        kk = pl.program_id(2)

        @pl.when(kk == 0)
        def _():
            acc_ref[...] = jnp.zeros_like(acc_ref)

        acc_ref[...] += lax.dot_general(a_ref[...], b_ref[...], dims, preferred_element_type=F32)

        @pl.when(kk == nk - 1)
        def _():
            r = acc_ref[...]
            if epilogue is not None:
                r = epilogue(r, *[e[...] for e in extra_refs])
            for o_ref, val in zip(out_refs, r if isinstance(r, tuple) else (r,)):
                o_ref[...] = val.astype(o_ref.dtype)

    if dims is TN:
        a_spec = pl.BlockSpec((tk, tm), lambda i, j, kk: (kk, i))
    else:
        a_spec = pl.BlockSpec((tm, tk), lambda i, j, kk: (i, kk))
    if dims is NT:
        b_spec = pl.BlockSpec((tn, tk), lambda i, j, kk: (j, kk))
    else:
        b_spec = pl.BlockSpec((tk, tn), lambda i, j, kk: (kk, j))
    if not out_chunks:
        o_spec = pl.BlockSpec((tm, tn), lambda i, j, kk: (i, j))
        o_shape = (m, n)
    else:
        pi = m // 8 // tm
        o_spec = pl.BlockSpec((None, None, tm, tn), lambda i, j, kk: ((i // pi) % 2, i // (2 * pi), i % pi, j))
        o_shape = (2, 4, m // 8, n)
    assert not (extras and out_chunks)
    tokens = [] if token is None else [token]
    out = pl.pallas_call(
        body, name=name, grid=(m // tm, n // tn, nk),
        in_specs=[a_spec, b_spec] + [o_spec] * n_extra + [pl.BlockSpec(memory_space=pl.ANY)] * len(tokens),
        out_specs=[o_spec] * len(out_dtypes),
        out_shape=[jax.ShapeDtypeStruct(o_shape, dt) for dt in out_dtypes],
        scratch_shapes=[pltpu.VMEM((tm, tn), F32)],
        compiler_params=_params(("parallel", "parallel", "arbitrary")),
    )(a, b, *extras, *tokens)
    return tuple(out) if isinstance(out_dtype, tuple) else out[0]


def _row_call(body, name, rows, tr, row_ins, par_ins, row_outs, acc_outs):
    def col(i, cb):
        return (i, cb)

    def whole(i, nd):
        return (0,) * nd

    in_specs = [pl.BlockSpec((tr, w), functools.partial(col, cb=cb)) for (_, w, cb) in row_ins]
    in_specs += [pl.BlockSpec(a.shape, functools.partial(whole, nd=a.ndim)) for a in par_ins]
    out_specs = [pl.BlockSpec((tr, w), lambda i: (i, 0)) for (w, _) in row_outs]
    out_specs += [pl.BlockSpec(s, functools.partial(whole, nd=len(s))) for (s, _) in acc_outs]
    out_shape = [jax.ShapeDtypeStruct((rows, w), dt) for (w, dt) in row_outs]
    out_shape += [jax.ShapeDtypeStruct(s, dt) for (s, dt) in acc_outs]
    return pl.pallas_call(
        body, name=name, grid=(rows // tr,), in_specs=in_specs, out_specs=out_specs, out_shape=out_shape,
        compiler_params=_params(("arbitrary",) if acc_outs else ("parallel",)),
    )(*[a for (a, _, _) in row_ins], *par_ins)


def _accumulate(ref, val):
    i = pl.program_id(0)

    @pl.when(i == 0)
    def _():
        ref[...] = val

    @pl.when(i > 0)
    def _():
        ref[...] += val


def _colsum(v):
    return jnp.sum(v, axis=0, keepdims=True)


def _rms_fwd(x, g):
    r = lax.rsqrt(jnp.mean(x * x, axis=-1, keepdims=True) + EPS)
    return x * r * g


def _rms_bwd(x, g, dy):
    r = lax.rsqrt(jnp.mean(x * x, axis=-1, keepdims=True) + EPS)
    xh = x * r
    dxh = dy * g
    dx = r * (dxh - xh * jnp.mean(dxh * xh, axis=-1, keepdims=True))
    return dx, dy * xh


_GELU_C = math.sqrt(2.0 / math.pi)
_GELU_A = 0.044715


def _gelu(x):
    return 0.5 * x * (1.0 + jnp.tanh(_GELU_C * (x + _GELU_A * (x * x * x))))


def _gelu_and_grad(x):
    t = jnp.tanh(_GELU_C * (x + _GELU_A * (x * x * x)))
    return (0.5 * x * (1.0 + t),
            0.5 * (1.0 + t) + 0.5 * x * (1.0 - t * t) * (_GELU_C * (1.0 + 3.0 * _GELU_A * (x * x))))


def _sigmoid(x):
    return 1.0 / (1.0 + jnp.exp(-x))


def _rope_fwd(t, cos_t, sin_t):
    return t * cos_t + pltpu.roll(t, 2 * ROPE_HALF, 1) * sin_t


def _rope_bwd(dt, cos_t, sin_t):
    return dt * cos_t - pltpu.roll(dt, 2 * ROPE_HALF, 1) * sin_t


def _prenorm(x, g, tr, token):
    def body(x_ref, g_ref, token_ref, o_ref):
        o_ref[...] = _rms_fwd(x_ref[...], g_ref[...]).astype(BF16)

    t, d = x.shape
    return _row_call(body, "prenorm", t, tr, [(x, d, 0)], [g, token], [(d, BF16)], [])[0]


def _qkv_prep(proj, g_q, g_kv, cos_t, sin_t, lay, tr, token):
    ql, kl = lay["ql"], lay["kl"]

    def body(q_ref, kv_ref, kr_ref, cos_ref, sin_ref, gq_ref, gkv_ref, token_ref, qn_ref, kvn_ref, kro_ref):
        qn_ref[...] = _rms_fwd(q_ref[...], gq_ref[...]).astype(BF16)
        kvn_ref[...] = _rms_fwd(kv_ref[...], gkv_ref[...]).astype(BF16)
        kro_ref[...] = _rope_fwd(kr_ref[...], cos_ref[...], sin_ref[...]).astype(BF16)

    t = proj.shape[0]
    return _row_call(
        body, "qkv_prep", t, tr,
        [(proj, ql, lay["q_off"] // ql), (proj, kl, lay["kv_off"] // kl), (proj, LANES, lay["kr_off"] // LANES),
         (cos_t, LANES, 0), (sin_t, LANES, 0)],
        [g_q, g_kv, token], [(ql, BF16), (kl, BF16), (LANES, BF16)], [])


def _q_rope(q, cos_t, sin_t, heads, tr):
    def body(q_ref, cos_ref, sin_ref, o_ref):
        c, s = cos_ref[...], sin_ref[...]
        for h in range(heads):
            lo = h * HEAD_PAD
            o_ref[:, lo:lo + NOPE_DIM] = q_ref[:, lo:lo + NOPE_DIM].astype(BF16)
            o_ref[:, lo + NOPE_DIM:lo + HEAD_PAD] = _rope_fwd(q_ref[:, lo + NOPE_DIM:lo + HEAD_PAD], c, s).astype(BF16)

    t, w = q.shape
    return _row_call(body, "q_rope", t, tr, [(q, w, 0), (cos_t, LANES, 0), (sin_t, LANES, 0)], [], [(w, BF16)], [])[0]


ATTN_SCALE = 1.0 / math.sqrt(NOPE_DIM + ROPE_DIM)
ATTN_EXP2_SCALE = ATTN_SCALE * math.log2(math.e)


def _attn_tile(t, cap):
    tq = cap
    while t % tq:
        tq //= 2
    return tq


def _attn_fwd(q, kv, kr, heads):
    t = q.shape[0]
    tq = _attn_tile(t, 256)

    def body(q_ref, kv_ref, kr_ref, o_ref, lse_ref, kcat):
        @pl.when(pl.program_id(1) == 0)
        def _():
            kcat[:, :NOPE_DIM] = kv_ref[:, :NOPE_DIM]
            kcat[:, NOPE_DIM:] = kr_ref[...]

        s = lax.dot_general(q_ref[...], kcat[...], NT, preferred_element_type=F32)
        m = jnp.max(s, axis=-1, keepdims=True)
        p = jnp.exp2((s - m) * ATTN_EXP2_SCALE)
        l = jnp.sum(p, axis=-1, keepdims=True)
        o_ref[...] = jnp.dot(p.astype(BF16), kv_ref[:, NOPE_DIM:], preferred_element_type=F32) * (1.0 / l)
        lse_ref[...] = jnp.broadcast_to(m * ATTN_EXP2_SCALE + jnp.log(l) * math.log2(math.e), (tq, V_DIM))

    out_spec = pl.BlockSpec((tq, V_DIM), lambda h, i: (i, h))
    out_shape = jax.ShapeDtypeStruct((t, heads * V_DIM), F32)
    return pl.pallas_call(
        body, name="attn_fwd", grid=(heads, t // tq),
        in_specs=[pl.BlockSpec((tq, HEAD_PAD), lambda h, i: (i, h)),
                  pl.BlockSpec((t, HEAD_PAD), lambda h, i: (0, h)),
                  pl.BlockSpec((t, LANES), lambda h, i: (0, 0))],
        out_specs=[out_spec, out_spec], out_shape=[out_shape, out_shape],
        scratch_shapes=[pltpu.VMEM((t, HEAD_PAD), BF16)],
        compiler_params=_params(("arbitrary", "arbitrary")),
    )(q, kv, kr)


def _attn_bwd(q, kv, kr, out, lse, d_out, cos_t, sin_t, heads):
    t = q.shape[0]
    tq = _attn_tile(t, 256)
    nq = t // tq

    def body(q_ref, kv_ref, kr_ref, o_ref, lse_ref, do_ref, cos_ref, sin_ref, dq_ref, dkv_ref, dkr_ref,
             kcat, dk_acc, dv_acc):
        h, i = pl.program_id(0), pl.program_id(1)

        @pl.when(i == 0)
        def _():
            kcat[:, :NOPE_DIM] = kv_ref[:, :NOPE_DIM]
            kcat[:, NOPE_DIM:] = kr_ref[...]
            dk_acc[...] = jnp.zeros_like(dk_acc)
            dv_acc[...] = jnp.zeros_like(dv_acc)

        @pl.when((h == 0) & (i == 0))
        def _():
            dkr_ref[...] = jnp.zeros_like(dkr_ref)

        qb, dob = q_ref[...], do_ref[...]
        row_term = jnp.sum(dob.astype(F32) * o_ref[...], axis=-1, keepdims=True)
        s = lax.dot_general(qb, kcat[...], NT, preferred_element_type=F32)
        dp = lax.dot_general(dob, kv_ref[:, NOPE_DIM:], NT, preferred_element_type=F32)
        p = jnp.exp2(s * ATTN_EXP2_SCALE - lse_ref[:, :1])
        ds = (p * (dp - row_term)).astype(BF16)
        dv_acc[...] += lax.dot_general(p.astype(BF16), dob, TN, preferred_element_type=F32)
        dq = jnp.dot(ds, kcat[...], preferred_element_type=F32) * ATTN_SCALE
        dq_ref[:, :NOPE_DIM] = dq[:, :NOPE_DIM].astype(BF16)
        dq_ref[:, NOPE_DIM:] = _rope_bwd(dq[:, NOPE_DIM:], cos_ref[...], sin_ref[...]).astype(BF16)
        dk_acc[...] += lax.dot_general(ds, qb, TN, preferred_element_type=F32)

        @pl.when(i == nq - 1)
        def _():
            dkv_ref[:, :NOPE_DIM] = (dk_acc[:, :NOPE_DIM] * ATTN_SCALE).astype(BF16)
            dkv_ref[:, NOPE_DIM:] = dv_acc[...].astype(BF16)
            dkr_ref[...] += dk_acc[:, NOPE_DIM:] * ATTN_SCALE

    return pl.pallas_call(
        body, name="attn_bwd", grid=(heads, nq),
        in_specs=[pl.BlockSpec((tq, HEAD_PAD), lambda h, i: (i, h)),
                  pl.BlockSpec((t, HEAD_PAD), lambda h, i: (0, h)),
                  pl.BlockSpec((t, LANES), lambda h, i: (0, 0)),
                  pl.BlockSpec((tq, V_DIM), lambda h, i: (i, h)),
                  pl.BlockSpec((tq, V_DIM), lambda h, i: (i, h)),
                  pl.BlockSpec((tq, V_DIM), lambda h, i: (i, h)),
                  pl.BlockSpec((tq, LANES), lambda h, i: (i, 0)),
                  pl.BlockSpec((tq, LANES), lambda h, i: (i, 0))],
        out_specs=[pl.BlockSpec((tq, HEAD_PAD), lambda h, i: (i, h)),
                   pl.BlockSpec((t, HEAD_PAD), lambda h, i: (0, h)),
                   pl.BlockSpec((t, LANES), lambda h, i: (0, 0))],
        out_shape=[jax.ShapeDtypeStruct((t, heads * HEAD_PAD), BF16),
                   jax.ShapeDtypeStruct((t, heads * HEAD_PAD), BF16),
                   jax.ShapeDtypeStruct((t, LANES), F32)],
        scratch_shapes=[pltpu.VMEM((t, HEAD_PAD), BF16), pltpu.VMEM((t, HEAD_PAD), F32), pltpu.VMEM((t, V_DIM), F32)],
        compiler_params=_params(("arbitrary", "arbitrary")),
    )(q, kv, kr, out, lse, d_out, cos_t, sin_t)


def _layer_norm_parts(x):
    mu = jnp.mean(x, axis=-1, keepdims=True)
    xc = x - mu
    r = lax.rsqrt(jnp.mean(xc * xc, axis=-1, keepdims=True) + EPS)
    return xc * r, r


def _gmlp_fwd(proj, ln_g, ln_b, w_s, b_sb, g_out_norm, lay):
    gw = lay["gw"]
    g_heads = gw // G_HEAD_DIM

    def body(u_ref, v_ref, lng_ref, lnb_ref, ws_ref, bs_ref, gn_ref, o_ref, gate_ref):
        gu = _gelu(u_ref[...])
        vh, _ = _layer_norm_parts(_gelu(v_ref[...]))
        vln = (vh * lng_ref[...] + lnb_ref[...]).astype(BF16)
        for g in range(g_heads):
            cols = slice(g * G_HEAD_DIM, (g + 1) * G_HEAD_DIM)
            s = jnp.dot(ws_ref[g], vln[:, cols], preferred_element_type=F32) + bs_ref[g]
            gate_ref[:, cols] = gu[:, cols] * s
        o_ref[...] = _rms_fwd(gate_ref[...], gn_ref[...]).astype(BF16)

    t = proj.shape[0]
    in_specs = [pl.BlockSpec((CHUNK, gw), lambda i: (i, 0)), pl.BlockSpec((CHUNK, gw), lambda i: (i, 1))]
    pars = [ln_g, ln_b, w_s, b_sb, g_out_norm]
    in_specs += [pl.BlockSpec(a.shape, functools.partial(lambda i, nd: (0,) * nd, nd=a.ndim)) for a in pars]
    return pl.pallas_call(
        body, name="gmlp_fwd", grid=(t // CHUNK,), in_specs=in_specs,
        out_specs=pl.BlockSpec((CHUNK, gw), lambda i: (i, 0)),
        out_shape=jax.ShapeDtypeStruct((t, gw), BF16),
        scratch_shapes=[pltpu.VMEM((CHUNK, gw), F32)],
        compiler_params=_params(("parallel",)),
    )(proj, proj, *pars)


def _gmlp_bwd(proj, d_mixed, ln_g, ln_b, w_s, w_st, b_sb, g_out_norm, lay):
    gw = lay["gw"]
    g_heads = gw // G_HEAD_DIM
    aw_blocks = lay["aw"] // gw

    def body(u_ref, v_ref, dm_ref, lng_ref, lnb_ref, ws_ref, wst_ref, bs_ref, gn_ref,
             du_ref, dv_ref, dgn_ref, dlng_ref, dlnb_ref, dws_ref, dbs_ref, gate_ref, s_ref, dvln_ref):
        i = pl.program_id(0)
        u, v = u_ref[...], v_ref[...]
        (gu, gelu_du), (gv, gelu_dv) = _gelu_and_grad(u), _gelu_and_grad(v)
        vh, r_ln = _layer_norm_parts(gv)
        vln = (vh * lng_ref[...] + lnb_ref[...]).astype(BF16)
        for g in range(g_heads):
            cols = slice(g * G_HEAD_DIM, (g + 1) * G_HEAD_DIM)
            s = jnp.dot(ws_ref[g], vln[:, cols], preferred_element_type=F32) + bs_ref[g]
            s_ref[:, cols] = s
            gate_ref[:, cols] = gu[:, cols] * s
        d_gate, dgn = _rms_bwd(gate_ref[...], gn_ref[...], dm_ref[...])
        _accumulate(dgn_ref, _colsum(dgn))
        du_ref[...] = (d_gate * s_ref[...] * gelu_du).astype(BF16)
        d_s = d_gate * gu
        d_sb = d_s.astype(BF16)
        for g in range(g_heads):
            cols = slice(g * G_HEAD_DIM, (g + 1) * G_HEAD_DIM)
            dw = lax.dot_general(d_sb[:, cols], vln[:, cols], NT, preferred_element_type=F32)

            @pl.when(i == 0)
            def _():
                dws_ref[g] = dw
                dbs_ref[g] = d_s[:, cols]

            @pl.when(i > 0)
            def _():
                dws_ref[g] += dw
                dbs_ref[g] += d_s[:, cols]

            dvln_ref[:, cols] = jnp.dot(wst_ref[g], d_sb[:, cols], preferred_element_type=F32)
        d_vln = dvln_ref[...]
        _accumulate(dlng_ref, _colsum(d_vln * vh))
        _accumulate(dlnb_ref, _colsum(d_vln))
        d_vh = d_vln * lng_ref[...]
        d_gv = r_ln * (d_vh - jnp.mean(d_vh, axis=-1, keepdims=True)
                       - vh * jnp.mean(d_vh * vh, axis=-1, keepdims=True))
        dv_ref[...] = (d_gv * gelu_dv).astype(BF16)

    t = proj.shape[0]
    whole = lambda a: pl.BlockSpec(a.shape, functools.partial(lambda i, nd: (0,) * nd, nd=a.ndim))
    pars = [ln_g, ln_b, w_s, w_st, b_sb, g_out_norm]
    hshape = (g_heads, CHUNK, CHUNK)
    return pl.pallas_call(
        body, name="gmlp_bwd", grid=(t // CHUNK,),
        in_specs=[pl.BlockSpec((CHUNK, gw), lambda i: (i, 0)), pl.BlockSpec((CHUNK, gw), lambda i: (i, 1)),
                  pl.BlockSpec((CHUNK, gw), lambda i: (i, aw_blocks))] + [whole(a) for a in pars],
        out_specs=[pl.BlockSpec((CHUNK, gw), lambda i: (i, 0)), pl.BlockSpec((CHUNK, gw), lambda i: (i, 0)),
                   pl.BlockSpec((1, gw), lambda i: (0, 0)), pl.BlockSpec((1, gw), lambda i: (0, 0)),
                   pl.BlockSpec((1, gw), lambda i: (0, 0)),
                   pl.BlockSpec(hshape, lambda i: (0, 0, 0)), pl.BlockSpec(hshape, lambda i: (0, 0, 0))],
        out_shape=[jax.ShapeDtypeStruct((t, gw), BF16), jax.ShapeDtypeStruct((t, gw), BF16),
                   jax.ShapeDtypeStruct((1, gw), F32), jax.ShapeDtypeStruct((1, gw), F32),
                   jax.ShapeDtypeStruct((1, gw), F32),
                   jax.ShapeDtypeStruct(hshape, F32), jax.ShapeDtypeStruct(hshape, F32)],
        scratch_shapes=[pltpu.VMEM((CHUNK, gw), F32), pltpu.VMEM((CHUNK, gw), F32), pltpu.VMEM((CHUNK, gw), F32)],
        compiler_params=_params(("arbitrary",)),
    )(proj, proj, d_mixed, *pars)


def _spatial_bias_grad(dbs_wide):
    g_heads = dbs_wide.shape[0]

    def body(x_ref, o_ref):
        for g in range(g_heads):
            o_ref[g:g + 1, :] = jnp.sum(x_ref[g].T, axis=0, keepdims=True)

    return pl.pallas_call(
        body, name="spatial_bias_grad", out_shape=jax.ShapeDtypeStruct((g_heads, CHUNK), F32),
        in_specs=[pl.BlockSpec(memory_space=pltpu.VMEM)], out_specs=pl.BlockSpec(memory_space=pltpu.VMEM),
    )(dbs_wide)


def _mix_norm(a_out, gn, g_a, tr, token):
    aw = a_out.shape[1]
    gw = gn.shape[1]

    def body(a_ref, gn_ref, g_ref, token_ref, o_ref):
        o_ref[:, :aw] = _rms_fwd(a_ref[...], g_ref[...]).astype(BF16)
        o_ref[:, aw:] = gn_ref[...]

    t = a_out.shape[0]
    return _row_call(body, "mix_norm", t, tr, [(a_out, aw, 0), (gn, gw, 0)], [g_a, token], [(aw + gw, BF16)], [])[0]


def _mix_norm_bwd(a_out, d_mixed, g_a, tr, token):
    aw = a_out.shape[1]

    def body(a_ref, dm_ref, g_ref, token_ref, da_ref, dg_ref):
        dx, dg = _rms_bwd(a_ref[...], g_ref[...], dm_ref[...])
        da_ref[...] = dx.astype(BF16)
        _accumulate(dg_ref, _colsum(dg))

    t = a_out.shape[0]
    return _row_call(body, "mix_norm_bwd", t, tr, [(a_out, aw, 0), (d_mixed, aw, 0)], [g_a, token],
                     [(aw, BF16)], [((1, aw), F32)])


def _post_mix(x, mix_out, g_pm, g_pf, tr):
    def body(x_ref, mo_ref, gpm_ref, gpf_ref, h_ref, hn_ref):
        h = x_ref[...] + _rms_fwd(mo_ref[...], gpm_ref[...])
        h_ref[...] = h
        hn_ref[...] = _rms_fwd(h, gpf_ref[...]).astype(BF16)

    t, d = x.shape
    return _row_call(body, "post_mix", t, tr, [(x, d, 0), (mix_out, d, 0)], [g_pm, g_pf], [(d, F32), (d, BF16)], [])


def _post_mix_bwd(mix_out, h, dy, d_hn, g_pm, g_pf, tr, token):
    def body(mo_ref, h_ref, dy_ref, dhn_ref, gpm_ref, gpf_ref, token_ref, dh_ref, dmo_ref, dgpf_ref, dgpm_ref):
        dx, dg = _rms_bwd(h_ref[...], gpf_ref[...], dhn_ref[...])
        dh = dy_ref[...] + dx
        dh_ref[...] = dh
        _accumulate(dgpf_ref, _colsum(dg))
        dmo, dg2 = _rms_bwd(mo_ref[...], gpm_ref[...], dh)
        dmo_ref[...] = dmo.astype(BF16)
        _accumulate(dgpm_ref, _colsum(dg2))

    t, d = h.shape
    return _row_call(body, "post_mix_bwd", t, tr, [(mix_out, d, 0), (h, d, 0), (dy, d, 0), (d_hn, d, 0)],
                     [g_pm, g_pf, token], [(d, F32), (d, BF16)], [((1, d), F32), ((1, d), F32)])


def _swiglu(gate, up):
    t, f = gate.shape
    tr, tf = _tile(t, 512), _tile(f, 2048)

    def body(g_ref, u_ref, o_ref):
        g = g_ref[...].astype(F32)
        o_ref[...] = (g * _sigmoid(g) * u_ref[...].astype(F32)).astype(BF16)

    spec = pl.BlockSpec((tr, tf), lambda i, j: (i, j))
    return pl.pallas_call(body, name="swiglu", grid=(t // tr, f // tf), in_specs=[spec, spec], out_specs=spec,
                          out_shape=jax.ShapeDtypeStruct((t, f), BF16),
                          compiler_params=_params(("parallel", "parallel")))(gate, up)


def _swiglu_bwd(gate, up, d_act):
    t, f = gate.shape
    tr, tf = _tile(t, 512), _tile(f, 2048)

    def body(g_ref, u_ref, da_ref, dg_ref, du_ref):
        g, u, da = g_ref[...].astype(F32), u_ref[...].astype(F32), da_ref[...].astype(F32)
        sg = _sigmoid(g)
        du_ref[...] = (da * (g * sg)).astype(BF16)
        dg_ref[...] = (da * u * (sg * (1.0 + g * (1.0 - sg)))).astype(BF16)

    spec = pl.BlockSpec((tr, tf), lambda i, j: (i, j))
    shape = jax.ShapeDtypeStruct((t, f), BF16)
    return pl.pallas_call(body, name="swiglu_bwd", grid=(t // tr, f // tf), in_specs=[spec, spec, spec],
                          out_specs=[spec, spec], out_shape=[shape, shape],
                          compiler_params=_params(("parallel", "parallel")))(gate, up, d_act)


def _loss_head(h, ffn, target, g_po, tr):
    t, d = h.shape

    def body(h_ref, f_ref, t_ref, g_ref, dy_ref, df_ref, dg_ref, loss_ref):
        f = f_ref[...]
        err = h_ref[...] + _rms_fwd(f, g_ref[...]) - t_ref[...]
        dy = err * (1.0 / d)
        dy_ref[...] = dy
        df, dg = _rms_bwd(f, g_ref[...], dy)
        df_ref[...] = df.astype(BF16)
        _accumulate(dg_ref, _colsum(dg))
        sq = jnp.sum(_colsum(err * err), axis=-1, keepdims=True) * (0.5 / d)
        _accumulate(loss_ref, jnp.broadcast_to(sq, (1, LANES)))

    return _row_call(body, "loss_head", t, tr, [(h, d, 0), (ffn, d, 0), (target, d, 0)], [g_po],
                     [(d, F32), (d, BF16)], [((1, d), F32), ((1, LANES), F32)])


def _qkv_bwd(proj, d_qn, d_kvn, d_kr, g_q, g_kv, cos_t, sin_t, lay, tr):
    ql, kl = lay["ql"], lay["kl"]

    def body(q_ref, kv_ref, dqn_ref, dkvn_ref, dkr_ref, cos_ref, sin_ref, gq_ref, gkv_ref,
             dq_ref, dkv_ref, dkt_ref, dgq_ref, dgkv_ref):
        dx, dg = _rms_bwd(q_ref[...], gq_ref[...], dqn_ref[...])
        dq_ref[...] = dx.astype(BF16)
        _accumulate(dgq_ref, _colsum(dg))
        dx, dg = _rms_bwd(kv_ref[...], gkv_ref[...], dkvn_ref[...])
        dkv_ref[...] = dx.astype(BF16)
        _accumulate(dgkv_ref, _colsum(dg))
        dkt_ref[...] = _rope_bwd(dkr_ref[...], cos_ref[...], sin_ref[...]).astype(BF16)

    t = proj.shape[0]
    return _row_call(
        body, "qkv_bwd", t, tr,
        [(proj, ql, lay["q_off"] // ql), (proj, kl, lay["kv_off"] // kl), (d_qn, ql, 0), (d_kvn, kl, 0),
         (d_kr, LANES, 0), (cos_t, LANES, 0), (sin_t, LANES, 0)],
        [g_q, g_kv], [(ql, BF16), (kl, BF16), (LANES, BF16)], [((1, ql), F32), ((1, kl), F32)])


def _prenorm_bwd(x, d_xn, dh, g, tr):
    def body(x_ref, dxn_ref, dh_ref, g_ref, gx_ref, dg_ref):
        dx, dg = _rms_bwd(x_ref[...], g_ref[...], dxn_ref[...])
        gx_ref[...] = dh_ref[...] + dx
        _accumulate(dg_ref, _colsum(dg))

    t, d = x.shape
    return _row_call(body, "prenorm_bwd", t, tr, [(x, d, 0), (d_xn, d, 0), (dh, d, 0)], [g],
                     [(d, F32)], [((1, d), F32)])


def _adam_rows(rows, cols, block_elements=256 * 1024):
    cap = max(8, block_elements // cols // 8 * 8)
    tr = min(rows, cap)
    while rows % tr:
        tr -= 8
    return tr


def _adamw(w, g, m, v, name):
    rows, cols = w.shape
    tr = _adam_rows(rows, cols)

    def body(w_ref, g_ref, m_ref, v_ref, d_ref, mo_ref, vo_ref):
        g = g_ref[...]
        m2 = ADAM_B1 * m_ref[...] + (1.0 - ADAM_B1) * g
        v2 = ADAM_B2 * v_ref[...] + (1.0 - ADAM_B2) * (g * g)
        m_hat = m2 / (1.0 - ADAM_B1 ** ADAM_STEP)
        v_hat = v2 / (1.0 - ADAM_B2 ** ADAM_STEP)
        d_ref[...] = -ADAM_LR * (m_hat / (jnp.sqrt(v_hat) + ADAM_EPS) + ADAM_WD * w_ref[...])
        mo_ref[...] = m2
        vo_ref[...] = v2

    spec = pl.BlockSpec((tr, cols), lambda i: (i, 0))
    shape = jax.ShapeDtypeStruct((rows, cols), F32)
    return pl.pallas_call(body, name=name, grid=(rows // tr,), in_specs=[spec] * 4, out_specs=[spec] * 3,
                          out_shape=[shape] * 3, compiler_params=_params(("parallel",)))(w, g, m, v)


def _adamw_halves(w, g_mine, g_theirs, m, v, name):
    rows, cols = w.shape
    rh = g_mine.shape[0]
    tr = _adam_rows(math.gcd(rows, rh), cols)
    per_half = rh // tr
    my_c = jnp.reshape(lax.axis_index("c"), (1,)).astype(jnp.int32)

    def body(c_ref, w_ref, gm_ref, gt_ref, m_ref, v_ref, g_ref, d_ref, mo_ref, vo_ref):
        mine = (pl.program_id(0) // per_half) == c_ref[0]
        g = jnp.where(mine, gm_ref[...], gt_ref[...])
        m2 = ADAM_B1 * m_ref[...] + (1.0 - ADAM_B1) * g
        v2 = ADAM_B2 * v_ref[...] + (1.0 - ADAM_B2) * (g * g)
        m_hat = m2 / (1.0 - ADAM_B1 ** ADAM_STEP)
        v_hat = v2 / (1.0 - ADAM_B2 ** ADAM_STEP)
        g_ref[...] = g
        d_ref[...] = -ADAM_LR * (m_hat / (jnp.sqrt(v_hat) + ADAM_EPS) + ADAM_WD * w_ref[...])
        mo_ref[...] = m2
        vo_ref[...] = v2

    def half_spec(is_mine):
        def index(i, c_ref):
            used = ((i // per_half) == c_ref[0]) if is_mine else ((i // per_half) != c_ref[0])
            return (jnp.where(used, i % per_half, 0), 0)
        return pl.BlockSpec((tr, cols), index)

    spec = pl.BlockSpec((tr, cols), lambda i, c_ref: (i, 0))
    shape = jax.ShapeDtypeStruct((rows, cols), F32)
    grid_spec = pltpu.PrefetchScalarGridSpec(
        num_scalar_prefetch=1, grid=(rows // tr,),
        in_specs=[spec, half_spec(True), half_spec(False), spec, spec], out_specs=[spec] * 4)
    return pl.pallas_call(body, name=name, grid_spec=grid_spec, out_shape=[shape] * 4,
                          compiler_params=_params(("parallel",)))(my_c, w, g_mine, g_theirs, m, v)


def _pair_add(parts, theirs, name):
    _, n, r, c = parts.shape
    tr = _adam_rows(r, c, 1024 * 1024)
    my_c = jnp.reshape(lax.axis_index("c"), (1,)).astype(jnp.int32)

    def body(c_ref, a_ref, b_ref, o_ref):
        o_ref[0] = (a_ref[0, 0].astype(F32) + b_ref[0].astype(F32)).astype(BF16)

    spec = pl.BlockSpec((1, tr, c), lambda k, i, c_ref: (k, i, 0))
    grid_spec = pltpu.PrefetchScalarGridSpec(
        num_scalar_prefetch=1, grid=(n, r // tr),
        in_specs=[pl.BlockSpec((1, 1, tr, c), lambda k, i, c_ref: (c_ref[0], k, i, 0)), spec], out_specs=spec)
    return pl.pallas_call(body, name=name, grid_spec=grid_spec, out_shape=jax.ShapeDtypeStruct((n, r, c), BF16),
                          compiler_params=_params(("parallel", "parallel")))(my_c, parts, theirs)


def _chip_sum(pair_sums, received, name, token=None):
    _, r, c = pair_sums.shape
    tr = _adam_rows(r, c)
    own = 2 * lax.axis_index("x") + lax.axis_index("y")

    def body(own_ref, p_ref, r0_ref, r1_ref, r2_ref, *rest):
        o_ref = rest[-1]
        acc = p_ref[0].astype(F32) + r0_ref[0].astype(F32)
        acc = acc + r1_ref[0].astype(F32)
        o_ref[...] = acc + r2_ref[0].astype(F32)

    def rspec(j):
        return pl.BlockSpec((1, tr, c), functools.partial(lambda i, own_ref, j: (j, i, 0), j=j))

    extra = [] if token is None else [token]
    grid_spec = pltpu.PrefetchScalarGridSpec(
        num_scalar_prefetch=1, grid=(r // tr,),
        in_specs=[pl.BlockSpec((1, tr, c), lambda i, own_ref: (own_ref[0], i, 0)), rspec(0), rspec(1), rspec(2)]
        + [pl.BlockSpec(memory_space=pl.ANY)] * len(extra),
        out_specs=pl.BlockSpec((tr, c), lambda i, own_ref: (i, 0)))
    return pl.pallas_call(body, name=name, grid_spec=grid_spec, out_shape=jax.ShapeDtypeStruct((r, c), F32),
                          compiler_params=_params(("parallel",)))(
        jnp.reshape(own, (1,)).astype(jnp.int32), pair_sums, received, received, received, *extra)


def _mesh_place():
    x, y, c = lax.axis_index("x"), lax.axis_index("y"), lax.axis_index("c")
    other_chips = [(1 - x, y), (x, 1 - y), (1 - x, 1 - y)]
    return x, y, c, other_chips


def _hbm_specs(n):
    return [pl.BlockSpec(memory_space=pltpu.HBM)] * n


def _sibling_exchange(parts, name):
    n = len(parts)

    def body(*refs):
        ins, outs = refs[:n], refs[n:2 * n]
        send_sems, recv_sems = refs[2 * n:]
        x, y, c, _ = _mesh_place()
        copies = [pltpu.make_async_remote_copy(src_ref=ins[w].at[1 - c], dst_ref=outs[w], send_sem=send_sems.at[w],
                                               recv_sem=recv_sems.at[w], device_id=(x, y, 1 - c), device_id_type=MESH)
                  for w in range(n)]
        for cp in copies:
            cp.start()
        for cp in copies:
            cp.wait()

    return pl.pallas_call(
        body, name=name,
        out_shape=[jax.ShapeDtypeStruct(p.shape[1:], p.dtype) for p in parts],
        in_specs=_hbm_specs(n), out_specs=_hbm_specs(n),
        scratch_shapes=[pltpu.SemaphoreType.DMA((n,)), pltpu.SemaphoreType.DMA((n,))],
    )(*parts)


SEM_SPEC = pl.BlockSpec(memory_space=pltpu.SEMAPHORE)
DATAFLOW_EFFECT = pltpu.SideEffectType.DATAFLOW_SIDE_EFFECTING


def _copies_per_weight(kind):
    return {"gather": 4, "scatter": 3, "sibling": 1, "forward": 3, "swap": 1}[kind]


def _flight_copies(kind, src_refs, land_refs, send_sems, recv_sems, arriving):
    x, y, c, other_chips = _mesh_place()
    own = 2 * x + y
    sibling = (x, y, 1 - c)
    per = _copies_per_weight(kind)
    copies = []
    for w in range(len(src_refs)):
        def remote(src, dst, j, to):
            return pltpu.make_async_remote_copy(src_ref=src, dst_ref=dst, send_sem=send_sems.at[per * w + j],
                                                recv_sem=recv_sems.at[per * w + j], device_id=to, device_id_type=MESH)

        if kind == "sibling":
            copies.append(remote(src_refs[w].at[1 - c], land_refs[w], 0, sibling))
            continue
        if kind == "swap":
            copies.append(remote(src_refs[w], land_refs[w], 0, sibling))
            continue
        for j, chip in enumerate(other_chips):
            theirs = 2 * chip[0] + chip[1]
            if kind == "gather":
                copies.append(remote(src_refs[w].at[c], land_refs[w].at[theirs if arriving else own, c], j, (*chip, c)))
            elif kind == "forward":
                copies.append(remote(src_refs[w].at[theirs, c], src_refs[w].at[theirs, (1 - c) if arriving else c],
                                     j, sibling))
            else:
                copies.append(remote(src_refs[w].at[theirs], land_refs[w].at[j], j, (*chip, c)))
        if kind == "gather":
            copies.append(remote(src_refs[w], land_refs[w].at[own], 3, sibling))
    return copies


def _ici_start(kind, srcs, name, after=None):
    n = len(srcs)
    if kind == "gather":
        lands = [lax.empty((4,) + s.shape, s.dtype) for s in srcs]
    elif kind == "scatter":
        lands = [lax.empty((3,) + s.shape[1:], s.dtype) for s in srcs]
    elif kind == "sibling":
        lands = [lax.empty(s.shape[1:], s.dtype) for s in srcs]
    elif kind == "swap":
        lands = [lax.empty(s.shape, s.dtype) for s in srcs]
    else:
        lands = []
    nb = n + len(lands)
    afters = [] if after is None else [after]

    def body(*refs):
        src_refs, land_refs = refs[:n], refs[n:nb]
        send_sems, recv_sems = refs[nb + len(afters)], refs[nb + len(afters) + 1]
        token = refs[-1]
        for cp in _flight_copies(kind, src_refs, land_refs, send_sems, recv_sems, False):
            cp.start()
        token[...] = jnp.zeros_like(token)

    hbm = lambda a: pltpu.with_memory_space_constraint(a, pltpu.HBM)
    n_sems = _copies_per_weight(kind) * n
    out = pl.pallas_call(
        body, name=name,
        out_shape=(pltpu.SemaphoreType.DMA((n_sems,)), pltpu.SemaphoreType.DMA((n_sems,)),
                   *[pltpu.HBM(a.shape, a.dtype) for a in srcs + lands], jax.ShapeDtypeStruct((8, LANES), F32)),
        in_specs=_hbm_specs(nb) + [pl.BlockSpec(memory_space=pl.ANY)] * len(afters),
        out_specs=(SEM_SPEC, SEM_SPEC, *_hbm_specs(nb), pl.BlockSpec(memory_space=pltpu.VMEM)),
        input_output_aliases={i: 2 + i for i in range(nb)},
        compiler_params=pltpu.CompilerParams(has_side_effects=DATAFLOW_EFFECT),
    )(*[hbm(a) for a in srcs + lands], *afters)
    return out[0], out[1], list(out[2:2 + n]), list(out[2 + n:2 + nb]), out[-1]


def _ici_wait(kind, send_sems, recv_sems, srcs, lands, after, name):
    n = len(srcs)
    nb = n + len(lands)

    def body(*refs):
        src_refs, land_refs = refs[:n], refs[n:nb]
        send_ref, recv_ref = refs[nb], refs[nb + 1]
        for cp in _flight_copies(kind, src_refs, land_refs, send_ref, recv_ref, True):
            cp.wait_send()
            cp.wait_recv()

    out = pl.pallas_call(
        body, name=name, out_shape=tuple(pltpu.HBM(a.shape, a.dtype) for a in srcs + lands),
        in_specs=_hbm_specs(nb) + [SEM_SPEC, SEM_SPEC, pl.BlockSpec(memory_space=pl.ANY)],
        out_specs=tuple(_hbm_specs(nb)), input_output_aliases={i: i for i in range(nb)},
        compiler_params=pltpu.CompilerParams(has_side_effects=DATAFLOW_EFFECT),
    )(*srcs, *lands, send_sems, recv_sems, after)
    return list(out[:n]), list(out[n:])


def _halves_exchange(halves, name):
    n = len(halves)

    def body(*refs):
        ins, outs = refs[:n], refs[n:2 * n]
        send_sems, recv_sems = refs[2 * n:]
        x, y, c, _ = _mesh_place()
        copies = [pltpu.make_async_remote_copy(src_ref=ins[w], dst_ref=outs[w], send_sem=send_sems.at[w],
                                               recv_sem=recv_sems.at[w], device_id=(x, y, 1 - c), device_id_type=MESH)
                  for w in range(n)]
        for cp in copies:
            cp.start()
        for cp in copies:
            cp.wait()

    return pl.pallas_call(
        body, name=name,
        out_shape=[jax.ShapeDtypeStruct(h.shape, h.dtype) for h in halves],
        in_specs=_hbm_specs(n), out_specs=_hbm_specs(n),
        scratch_shapes=[pltpu.SemaphoreType.DMA((n,)), pltpu.SemaphoreType.DMA((n,))],
    )(*halves)


def _small_all_reduce(packed):
    rows = packed.shape[0]

    def body(in_ref, out_ref, from_sibling, pair_sums, send_sems, recv_sems):
        x, y, c, other_chips = _mesh_place()
        own = 2 * x + y

        def remote(src, dst, k, to):
            return pltpu.make_async_remote_copy(src_ref=src, dst_ref=dst, send_sem=send_sems.at[k],
                                                recv_sem=recv_sems.at[k], device_id=to, device_id_type=MESH)

        swap = remote(in_ref, from_sibling, 0, (x, y, 1 - c))
        swap.start()
        swap.wait()
        mine, theirs = in_ref[...], from_sibling[...]
        pair_sums[own] = jnp.where(c == 0, mine, theirs) + jnp.where(c == 0, theirs, mine)
        copies = [remote(pair_sums.at[own], pair_sums.at[own], 1 + j, (*chip, c)) for j, chip in enumerate(other_chips)]
        for cp in copies:
            cp.start()
        for j, chip in enumerate(other_chips):
            landing = pair_sums.at[2 * chip[0] + chip[1]]
            remote(landing, landing, 1 + j, (*chip, c)).wait_recv()
        for cp in copies:
            cp.wait_send()
        out_ref[...] = ((pair_sums[0] + pair_sums[1]) + pair_sums[2]) + pair_sums[3]

    return pl.pallas_call(
        body, name="small_all_reduce", out_shape=jax.ShapeDtypeStruct(packed.shape, F32),
        in_specs=[pl.BlockSpec(memory_space=pltpu.VMEM)], out_specs=pl.BlockSpec(memory_space=pltpu.VMEM),
        scratch_shapes=[pltpu.VMEM((rows, LANES), F32), pltpu.VMEM((4, rows, LANES), F32),
                        pltpu.SemaphoreType.DMA((4,)), pltpu.SemaphoreType.DMA((4,))],
        compiler_params=pltpu.CompilerParams(vmem_limit_bytes=VMEM_LIMIT_BYTES),
    )(packed)


def _layout(w_in, w_uq, w_ukv, v_ln_gain, q_norm, kv_norm):
    heads = 4 * w_uq.shape[-1] // (NOPE_DIM + ROPE_DIM)
    gw = v_ln_gain.shape[-1]
    ql, kl = q_norm.shape[-1], kv_norm.shape[-1]
    lay = dict(heads=heads, gw=gw, ql=ql, kl=kl, aw=heads * V_DIM, u_off=0, v_off=gw, q_off=2 * gw,
               kv_off=2 * gw + ql, kr_off=2 * gw + ql + kl)
    lay["in_pad"] = _round_up(lay["kr_off"] + LANES, 2 * LANES if lay["kr_off"] + LANES <= 2048 else 1024)
    assert lay["q_off"] % ql == 0 and lay["kv_off"] % kl == 0 and lay["aw"] % gw == 0
    assert 4 * w_in.shape[-1] == ql + kl + ROPE_DIM + 2 * gw
    return lay


def _rope_tile(t1, t2, axis=-1):
    z = jnp.zeros_like(t1)
    return jnp.concatenate([t1, z, t2, z], axis=axis)


def _w_in_rows(gathered, shard_rows, lay):
    d = gathered.shape[-1]
    wt = gathered[:, :shard_rows].reshape(4 * shard_rows, d)
    ql, kl, gw = lay["ql"], lay["kl"], lay["gw"]
    q_c, kv_c = wt[:ql], wt[ql:ql + kl]
    r = wt[ql + kl:ql + kl + ROPE_DIM]
    u = wt[ql + kl + ROPE_DIM:ql + kl + ROPE_DIM + gw]
    v = wt[ql + kl + ROPE_DIM + gw:]
    parts = [u, v, q_c, kv_c, _rope_tile(r[:ROPE_HALF], r[ROPE_HALF:], axis=0)]
    pad = lay["in_pad"] - (lay["kr_off"] + LANES)
    if pad:
        parts.append(jnp.zeros((pad, d), wt.dtype))
    return jnp.concatenate(parts, axis=0)


def _w_in_grad_chunks(dwt, shard_rows, padded_rows, lay):
    d = dwt.shape[-1]
    ql, kl, gw = lay["ql"], lay["kl"], lay["gw"]
    ko = lay["kr_off"]
    rows = jnp.concatenate([dwt[lay["q_off"]:lay["q_off"] + ql], dwt[lay["kv_off"]:lay["kv_off"] + kl],
                            dwt[ko:ko + ROPE_HALF], dwt[ko + 2 * ROPE_HALF:ko + 3 * ROPE_HALF],
                            dwt[:gw], dwt[gw:2 * gw]], axis=0).reshape(4, shard_rows, d)
    rows = jnp.pad(rows, ((0, 0), (0, padded_rows - shard_rows), (0, 0)))
    return jnp.transpose(rows.reshape(4, 2, padded_rows // 2, d), (1, 0, 2, 3)).astype(BF16)


def _w_uq_padded(w, heads):
    w3 = w.reshape(w.shape[0], heads, NOPE_DIM + ROPE_DIM)
    t = _rope_tile(w3[..., NOPE_DIM:NOPE_DIM + ROPE_HALF], w3[..., NOPE_DIM + ROPE_HALF:])
    return jnp.concatenate([w3[..., :NOPE_DIM], t], axis=-1).reshape(w.shape[0], heads * HEAD_PAD)


def _w_uq_grad_unpadded(dw, heads):
    d3 = dw.reshape(dw.shape[0], heads, HEAD_PAD)
    return jnp.concatenate([d3[..., :NOPE_DIM], d3[..., NOPE_DIM:NOPE_DIM + ROPE_HALF],
                            d3[..., NOPE_DIM + 2 * ROPE_HALF:NOPE_DIM + 3 * ROPE_HALF]],
                           axis=-1).reshape(dw.shape[0], heads * (NOPE_DIM + ROPE_DIM))


def _cols_gathered(g):
    return jnp.transpose(g, (1, 0, 2)).reshape(g.shape[1], 4 * g.shape[2])


def _chunks_of_cols(grad):
    r, c4 = grad.shape
    return jnp.transpose(grad.reshape(2, r // 2, 4, c4 // 4), (0, 2, 1, 3)).astype(BF16)


SMALL = ["pre_mix_norm", "q_norm", "kv_norm", "v_ln_gain", "v_ln_bias", "w_spatial", "b_spatial", "attn_out_norm",
         "gmlp_out_norm", "post_mix_norm", "pre_ffn_norm", "post_ffn_norm"]
BIG = ["w_in", "w_uq", "w_ukv", "w_out", "w_gate", "w_up", "w_down"]
GATHER_NOW = ["w_in"]
GATHER_SOON = ["w_uq", "w_ukv"]
GATHER_LATER = ["w_out", "w_gate", "w_up", "w_down"]
REDUCE_FFN = ["w_gate", "w_up", "w_down"]
REDUCE_OUT = ["w_out"]
REDUCE_LAST = ["w_in", "w_uq", "w_ukv"]
TRANSPOSED = ("w_in", "w_gate", "w_up")
ORDER = ["pre_mix_norm", "w_in", "q_norm", "kv_norm", "w_uq", "w_ukv", "v_ln_gain", "v_ln_bias", "w_spatial",
         "b_spatial", "attn_out_norm", "gmlp_out_norm", "w_out", "post_mix_norm", "pre_ffn_norm", "w_gate", "w_up",
         "w_down", "post_ffn_norm"]


def _pack(arrays):
    flat = jnp.concatenate([a.reshape(-1) for a in arrays])
    n = flat.shape[0]
    total = _round_up(n, 8 * LANES)
    if total > n:
        flat = jnp.concatenate([flat, jnp.zeros((total - n,), F32)])
    return flat.reshape(total // LANES, LANES)


def _unpack(packed, like):
    flat = packed.reshape(-1)
    out, off = [], 0
    for a in like:
        out.append(flat[off:off + a.size].reshape(a.shape))
        off += a.size
    return out


def kernel(x, positions, pre_mix_norm, w_in, q_norm, kv_norm, w_uq, w_ukv, v_ln_gain, v_ln_bias, w_spatial, b_spatial, attn_out_norm, gmlp_out_norm, w_out, post_mix_norm, pre_ffn_norm, w_gate, w_up, w_down, post_ffn_norm, loss_target, m_pre_mix_norm, m_w_in, m_q_norm, m_kv_norm, m_w_uq, m_w_ukv, m_v_ln_gain, m_v_ln_bias, m_w_spatial, m_b_spatial, m_attn_out_norm, m_gmlp_out_norm, m_w_out, m_post_mix_norm, m_pre_ffn_norm, m_w_gate, m_w_up, m_w_down, m_post_ffn_norm, v_pre_mix_norm, v_w_in, v_q_norm, v_kv_norm, v_w_uq, v_w_ukv, v_v_ln_gain, v_v_ln_bias, v_w_spatial, v_b_spatial, v_attn_out_norm, v_gmlp_out_norm, v_w_out, v_post_mix_norm, v_pre_ffn_norm, v_w_gate, v_w_up, v_w_down, v_post_ffn_norm):
    args = dict(locals())
    weights = {n: args[n] for n in ORDER}
    m_in = {n: args["m_" + n] for n in ORDER}
    v_in = {n: args["v_" + n] for n in ORDER}

    lay = _layout(w_in, w_uq, w_ukv, v_ln_gain, q_norm, kv_norm)
    heads, gw = lay["heads"], lay["gw"]
    t, d = x.shape[1], x.shape[2]
    tr = 128 if t % 128 == 0 else t
    xs = x.reshape(t, d)
    target = loss_target.reshape(t, d)

    ffs, ins = w_gate.shape[-1], w_in.shape[-1]
    ffp, inp = _round_up(ffs, LANES), _round_up(ins, LANES)
    padded_rows = {"w_gate": ffp, "w_up": ffp, "w_down": ffp, "w_in": inp}
    shards, halved = {}, {}

    def cast_shard(n, one=None):
        w = jnp.swapaxes(weights[n][0], 0, 1) if n in TRANSPOSED else weights[n][0]
        s = (w if one is None else w * one).astype(BF16)
        if n in padded_rows:
            s = jnp.pad(s, ((0, padded_rows[n] - s.shape[0]), (0, 0)))
        shards[n] = s
        halved[n] = s.reshape(2, s.shape[0] // 2, s.shape[1])

    for n in GATHER_NOW:
        cast_shard(n)
    full = {}

    def pair_sums_of(partial, names, tag):
        from_sibling = _sibling_exchange([partial[n] for n in names], "grads_sibling_exchange_" + tag)
        return [_pair_add(partial[n], r, "pair_add_" + n) for n, r in zip(names, from_sibling)]

    def place(names, lands):
        for n, g in zip(names, lands):
            full[n] = g.reshape((4,) + shards[n].shape)

    flight_0 = _ici_start("gather", [halved[n] for n in GATHER_NOW], "gather_start_0")
    for n in GATHER_SOON + GATHER_LATER:
        cast_shard(n, 1.0 + flight_0[4][0, 0])
    flight_soon = _ici_start("gather", [halved[n] for n in GATHER_SOON], "gather_start_soon", after=flight_0[4])
    flights, last_token = {}, flight_soon[4]
    for n in GATHER_LATER:
        flights[n] = _ici_start("gather", [halved[n]], "gather_start_" + n, after=last_token)
        last_token = flights[n][4]
    xn = _prenorm(xs, pre_mix_norm.reshape(1, -1), tr, last_token)
    _, lands = _ici_wait("gather", *flight_0[:4], xn, "gather_wait_0")
    pass_now = _ici_start("forward", lands, "forward_start_now")
    place(GATHER_NOW, _ici_wait("forward", *pass_now[:4], pass_now[4], "forward_wait_now")[0])
    wt_in = _w_in_rows(full["w_in"], ins, lay)

    inv_freq = 1.0 / (ROPE_THETA ** (jnp.arange(0, ROPE_DIM, 2, dtype=F32) / ROPE_DIM))
    ang = positions.reshape(t).astype(F32)[:, None] * inv_freq
    cos, sin = jnp.cos(ang), jnp.sin(ang)
    cos_t = _rope_tile(cos, cos)
    sin_t = _rope_tile(-sin, sin)

    row = lambda a: a.reshape(1, -1)
    g_pre, g_q, g_kv = row(pre_mix_norm), row(q_norm), row(kv_norm)
    g_a, g_g, g_pm = row(attn_out_norm), row(gmlp_out_norm), row(post_mix_norm)
    g_pf, g_po = row(pre_ffn_norm), row(post_ffn_norm)
    ln_g, ln_b = row(v_ln_gain), row(v_ln_bias)
    ws = w_spatial[0].astype(BF16)
    ws_t = jnp.transpose(ws, (0, 2, 1))
    bs_wide = jnp.broadcast_to(b_spatial[0][:, :, None], b_spatial.shape[1:] + (G_HEAD_DIM,))

    proj = _matmul(xn, wt_in, NT, F32, "proj")
    _, lands = _ici_wait("gather", *flight_soon[:4], proj, "gather_wait_soon")
    pass_soon = _ici_start("forward", lands, "forward_start_soon")
    qn, kvn, kr = _qkv_prep(proj, g_q, g_kv, cos_t, sin_t, lay, tr, pass_soon[4])
    place(GATHER_SOON, _ici_wait("forward", *pass_soon[:4], qn, "forward_wait_soon")[0])
    wb_uq = _w_uq_padded(_cols_gathered(full["w_uq"]), heads)
    wb_ukv = _cols_gathered(full["w_ukv"])
    q = _q_rope(_matmul(qn, wb_uq, NN, F32, "q_up"), cos_t, sin_t, heads, tr)
    kv = _matmul(kvn, wb_ukv, NN, BF16, "kv_up")
    a_out, a_lse = _attn_fwd(q, kv, kr, heads)
    def arrive(n, after):
        _, lands = _ici_wait("gather", *flights[n][:4], after, "gather_wait_" + n)
        return _ici_start("forward", lands, "forward_start_" + n)

    def settle(n, passing, after):
        place([n], _ici_wait("forward", *passing[:4], after, "forward_wait_" + n)[0])

    pass_out = arrive("w_out", a_out)
    gn = _gmlp_fwd(proj, ln_g, ln_b, ws, bs_wide, g_g, lay)
    mixed = _mix_norm(a_out, gn, g_a, tr, pass_out[4])
    settle("w_out", pass_out, mixed)
    pass_gate = arrive("w_gate", mixed)
    wb_out = full["w_out"].reshape(-1, d)
    mix_out = _matmul(mixed, wb_out, NN, F32, "mix_out", token=pass_gate[4])
    h, hn = _post_mix(xs, mix_out, g_pm, g_pf, tr)
    settle("w_gate", pass_gate, hn)
    pass_up = arrive("w_up", hn)
    wt_gate = full["w_gate"].reshape(4 * ffp, d)
    gate = _matmul(hn, wt_gate, NT, BF16, "ffn_gate", token=pass_up[4])
    settle("w_up", pass_up, gate)
    pass_down = arrive("w_down", gate)
    wt_up = full["w_up"].reshape(4 * ffp, d)
    up = _matmul(hn, wt_up, NT, BF16, "ffn_up", token=pass_down[4])
    act = _swiglu(gate, up)
    settle("w_down", pass_down, act)
    wb_down = full["w_down"].reshape(4 * ffp, d)
    ffn = _matmul(act, wb_down, NN, F32, "ffn_down")
    dy, d_ffn, dg_po, loss_vec = _loss_head(h, ffn, target, g_po, tr)

    d_act = _matmul(d_ffn, wb_down, NT, BF16, "d_act")
    d_gate, d_up = _swiglu_bwd(gate, up, d_act)
    partial_ffn = [_matmul(d_gate, hn, TN, BF16, "gw_gate", out_chunks=True),
                   _matmul(d_up, hn, TN, BF16, "gw_up", out_chunks=True),
                   _matmul(act, d_ffn, TN, BF16, "gw_down", out_chunks=True)]
    swap_ffn = _ici_start("sibling", partial_ffn, "sibling_start_ffn")
    d_hn = _matmul(d_up, wt_up, NN, F32, "d_hn",
                   extras=[_matmul(d_gate, wt_gate, NN, F32, "d_hn_gate", token=swap_ffn[4])],
                   epilogue=lambda acc, partial: acc + partial)
    partial_ffn, from_sibling = _ici_wait("sibling", *swap_ffn[:4], d_hn, "sibling_wait_ffn")
    pair_ffn = [_pair_add(p, r, "pair_add_" + n) for n, p, r in zip(REDUCE_FFN, partial_ffn, from_sibling)]
    flight_ffn = _ici_start("scatter", pair_ffn, "scatter_start_ffn")
    dh, d_mo, dg_pf, dg_pm = _post_mix_bwd(mix_out, h, dy, d_hn, g_pm, g_pf, tr, flight_ffn[4])
    d_mixed = _matmul(d_mo, wb_out, NT, F32, "d_mixed")
    gw_out = _matmul(mixed, d_mo, TN, BF16, "gw_out", out_chunks=True)
    pair_out = pair_sums_of({"w_out": gw_out}, REDUCE_OUT, "out")
    flight_out = _ici_start("scatter", pair_out, "scatter_start_out")
    d_a, dg_a = _mix_norm_bwd(a_out, d_mixed, g_a, tr, flight_out[4])
    d_u, d_v, dg_g, d_ln_g, d_ln_b, d_ws, d_bs_wide = _gmlp_bwd(proj, d_mixed, ln_g, ln_b, ws, ws_t, bs_wide, g_g, lay)
    d_bs = _spatial_bias_grad(d_bs_wide)
    d_q, d_kv, d_kr = _attn_bwd(q, kv, kr, a_out, a_lse, d_a, cos_t, sin_t, heads)
    d_qn = _matmul(d_q, wb_uq, NT, F32, "d_qn")
    gw_uq = _matmul(qn, d_q, TN, BF16, "gw_uq")
    d_kvn = _matmul(d_kv, wb_ukv, NT, F32, "d_kvn")
    gw_ukv = _matmul(kvn, d_kv, TN, BF16, "gw_ukv")
    d_qc, d_kvc, d_krt, dg_q, dg_kv = _qkv_bwd(proj, d_qn, d_kvn, d_kr, g_q, g_kv, cos_t, sin_t, lay, tr)
    parts = [d_u, d_v, d_qc, d_kvc, d_krt]
    pad = lay["in_pad"] - (lay["kr_off"] + LANES)
    if pad:
        parts.append(jnp.zeros((t, pad), BF16))
    d_proj = jnp.concatenate(parts, axis=1)
    d_xn = _matmul(d_proj, wt_in, NN, F32, "d_xn")
    gw_in = _matmul(d_proj, xn, TN, BF16, "gw_in")
    grad_x, dg_pre = _prenorm_bwd(xs, d_xn, dh, g_pre, tr)

    pair_mix = pair_sums_of({"w_in": _w_in_grad_chunks(gw_in, ins, inp, lay),
                             "w_uq": _chunks_of_cols(_w_uq_grad_unpadded(gw_uq, heads)),
                             "w_ukv": _chunks_of_cols(gw_ukv)}, REDUCE_LAST, "mix")
    pair_ffn, received_ffn = _ici_wait("scatter", *flight_ffn[:4], pair_mix[-1], "scatter_wait_ffn")
    pair_out, received_out = _ici_wait("scatter", *flight_out[:4], pair_mix[-1], "scatter_wait_out")
    flight_mix = _ici_start("scatter", pair_mix, "scatter_start_mix")
    grads, delta, new_m, new_v = {}, {}, {}, {}

    def finish(names, pair_sums, received, tag, token, in_flight):
        mine, swaps = [], []
        for n, p, r in zip(names, pair_sums, received):
            mine.append(_chip_sum(p, r, "chip_sum_" + n, token))
            if in_flight:
                swaps.append(_ici_start("swap", [mine[-1]], "swap_start_" + n))
                token = swaps[-1][4]
        if not in_flight:
            theirs = _halves_exchange(mine, "grads_halves_exchange_" + tag)
        previous = mine[-1]
        for i, n in enumerate(names):
            if in_flight:
                (g_mine,), (g_theirs,) = _ici_wait("swap", *swaps[i][:4], previous, "swap_wait_" + n)
            else:
                g_mine, g_theirs = mine[i], theirs[i]
            shape = weights[n].shape
            if n in TRANSPOSED:
                view = lambda a: jnp.swapaxes(a[0], 0, 1)
                back = lambda o: jnp.swapaxes(o, 0, 1).reshape(shape)
            else:
                view = lambda a: a[0]
                back = lambda o: o.reshape(shape)
            out = _adamw_halves(view(weights[n]), g_mine, g_theirs, view(m_in[n]), view(v_in[n]), "adamw_" + n)
            grads[n], delta[n], new_m[n], new_v[n] = [back(o) for o in out]
            previous = out[-1]

    finish(REDUCE_FFN + REDUCE_OUT, pair_ffn + pair_out, received_ffn + received_out, "ffn", flight_mix[4], True)
    pair_mix, received = _ici_wait("scatter", *flight_mix[:4], new_v[REDUCE_OUT[-1]], "scatter_wait_mix")
    finish(REDUCE_LAST, pair_mix, received, "mix", None, False)

    small_grads = {"pre_mix_norm": dg_pre, "q_norm": dg_q, "kv_norm": dg_kv, "v_ln_gain": d_ln_g, "v_ln_bias": d_ln_b,
                   "w_spatial": d_ws, "b_spatial": d_bs, "attn_out_norm": dg_a, "gmlp_out_norm": dg_g,
                   "post_mix_norm": dg_pm, "pre_ffn_norm": dg_pf, "post_ffn_norm": dg_po}
    like = [weights[n] for n in SMALL]
    reduced = _small_all_reduce(_pack([small_grads[n] for n in SMALL] + [loss_vec]))
    loss = reduced.reshape(-1)[sum(a.size for a in like)]
    small_g = _pack(_unpack(reduced, like))
    s_delta, s_m, s_v = _adamw(_pack(like), small_g, _pack([m_in[n] for n in SMALL]),
                               _pack([v_in[n] for n in SMALL]), "adamw_small")
    for n, g in zip(SMALL, _unpack(small_g, like)):
        grads[n] = g
    delta.update(zip(SMALL, _unpack(s_delta, like)))
    new_m.update(zip(SMALL, _unpack(s_m, like)))
    new_v.update(zip(SMALL, _unpack(s_v, like)))

    return (loss, grad_x.reshape(x.shape), *[grads[n] for n in ORDER], *[delta[n] for n in ORDER],
            *[new_m[n] for n in ORDER], *[new_v[n] for n in ORDER])
```

```python
import functools
import math

import jax
import jax.numpy as jnp
from jax import lax
from jax.experimental import pallas as pl
from jax.experimental.pallas import tpu as pltpu

F32 = jnp.float32
BF16 = jnp.bfloat16
MESH = pl.DeviceIdType.MESH

NOPE_DIM = 128
ROPE_DIM = 64
ROPE_HALF = ROPE_DIM // 2
V_DIM = 128
HEAD_PAD = 256
G_HEAD_DIM = 128
CHUNK = 128
ROPE_THETA = 10000.0
EPS = 1e-6
ADAM_LR = 0.001
ADAM_B1 = 0.9
ADAM_B2 = 0.999
ADAM_EPS = 1e-08
ADAM_WD = 0.01
ADAM_STEP = 10

LANES = 128
MATMUL_TILE = 1024
WIDE_TILE = 1408
VMEM_LIMIT_BYTES = 48 * 1024 * 1024

NN = (((1,), (0,)), ((), ()))
NT = (((1,), (1,)), ((), ()))
TN = (((0,), (0,)), ((), ()))


def _params(semantics):
    return pltpu.CompilerParams(dimension_semantics=semantics, vmem_limit_bytes=VMEM_LIMIT_BYTES)


def _tile(n, cap=MATMUL_TILE):
    if n <= cap:
        return n
    if cap == MATMUL_TILE and n % WIDE_TILE == 0:
        return WIDE_TILE
    t = cap - cap % LANES
    while n % t:
        t -= LANES
    assert t > 0, n
    return t


def _round_up(n, m):
    return (n + m - 1) // m * m


def _matmul(a, b, dims, out_dtype, name, extras=(), epilogue=None, out_chunks=None, token=None):
    if dims is NN:
        (m, k), (k2, n) = a.shape, b.shape
    elif dims is NT:
        (m, k), (n, k2) = a.shape, b.shape
    else:
        (k, m), (k2, n) = a.shape, b.shape
    assert k == k2, (a.shape, b.shape, name)
    tm, tn, tk = _tile(m // 8 if out_chunks else m), _tile(n), _tile(k, 2 * MATMUL_TILE)
    if len(extras) + (len(out_dtype) if isinstance(out_dtype, tuple) else 1) > 2:
        tm = _tile(m, MATMUL_TILE // 2)
    if not extras and k % (2 * WIDE_TILE) == 0:
        tk = 2 * WIDE_TILE
    nk = k // tk

    out_dtypes = out_dtype if isinstance(out_dtype, tuple) else (out_dtype,)
    n_extra = len(extras)

    def body(*refs):
        a_ref, b_ref = refs[:2]
        extra_refs = refs[2:2 + n_extra]
        out_refs = refs[2 + n_extra + (token is not None):-1]
        acc_ref = refs[-1]
        kk = pl.program_id(2)

        @pl.when(kk == 0)
        def _():
            acc_ref[...] = jnp.zeros_like(acc_ref)

        acc_ref[...] += lax.dot_general(a_ref[...], b_ref[...], dims, preferred_element_type=F32)

        @pl.when(kk == nk - 1)
        def _():
            r = acc_ref[...]
            if epilogue is not None:
                r = epilogue(r, *[e[...] for e in extra_refs])
            for o_ref, val in zip(out_refs, r if isinstance(r, tuple) else (r,)):
                o_ref[...] = val.astype(o_ref.dtype)

    if dims is TN:
        a_spec = pl.BlockSpec((tk, tm), lambda i, j, kk: (kk, i))
    else:
        a_spec = pl.BlockSpec((tm, tk), lambda i, j, kk: (i, kk))
    if dims is NT:
        b_spec = pl.BlockSpec((tn, tk), lambda i, j, kk: (j, kk))
    else:
        b_spec = pl.BlockSpec((tk, tn), lambda i, j, kk: (kk, j))
    if not out_chunks:
        o_spec = pl.BlockSpec((tm, tn), lambda i, j, kk: (i, j))
        o_shape = (m, n)
    else:
        pi = m // 8 // tm
        o_spec = pl.BlockSpec((None, None, tm, tn), lambda i, j, kk: ((i // pi) % 2, i // (2 * pi), i % pi, j))
        o_shape = (2, 4, m // 8, n)
    assert not (extras and out_chunks)
    tokens = [] if token is None else [token]
    out = pl.pallas_call(
        body, name=name, grid=(m // tm, n // tn, nk),
        in_specs=[a_spec, b_spec] + [o_spec] * n_extra + [pl.BlockSpec(memory_space=pl.ANY)] * len(tokens),
        out_specs=[o_spec] * len(out_dtypes),
        out_shape=[jax.ShapeDtypeStruct(o_shape, dt) for dt in out_dtypes],
        scratch_shapes=[pltpu.VMEM((tm, tn), F32)],
        compiler_params=_params(("parallel", "parallel", "arbitrary")),
    )(a, b, *extras, *tokens)
    return tuple(out) if isinstance(out_dtype, tuple) else out[0]


def _row_call(body, name, rows, tr, row_ins, par_ins, row_outs, acc_outs):
    def col(i, cb):
        return (i, cb)

    def whole(i, nd):
        return (0,) * nd

    in_specs = [pl.BlockSpec((tr, w), functools.partial(col, cb=cb)) for (_, w, cb) in row_ins]
    in_specs += [pl.BlockSpec(a.shape, functools.partial(whole, nd=a.ndim)) for a in par_ins]
    out_specs = [pl.BlockSpec((tr, w), lambda i: (i, 0)) for (w, _) in row_outs]
    out_specs += [pl.BlockSpec(s, functools.partial(whole, nd=len(s))) for (s, _) in acc_outs]
    out_shape = [jax.ShapeDtypeStruct((rows, w), dt) for (w, dt) in row_outs]
    out_shape += [jax.ShapeDtypeStruct(s, dt) for (s, dt) in acc_outs]
    return pl.pallas_call(
        body, name=name, grid=(rows // tr,), in_specs=in_specs, out_specs=out_specs, out_shape=out_shape,
        compiler_params=_params(("arbitrary",) if acc_outs else ("parallel",)),
    )(*[a for (a, _, _) in row_ins], *par_ins)


def _accumulate(ref, val):
    i = pl.program_id(0)

    @pl.when(i == 0)
    def _():
        ref[...] = val

    @pl.when(i > 0)
    def _():
        ref[...] += val


def _colsum(v):
    return jnp.sum(v, axis=0, keepdims=True)


def _rms_fwd(x, g):
    r = lax.rsqrt(jnp.mean(x * x, axis=-1, keepdims=True) + EPS)
    return x * r * g


def _rms_bwd(x, g, dy):
    r = lax.rsqrt(jnp.mean(x * x, axis=-1, keepdims=True) + EPS)
    xh = x * r
    dxh = dy * g
    dx = r * (dxh - xh * jnp.mean(dxh * xh, axis=-1, keepdims=True))
    return dx, dy * xh


_GELU_C = math.sqrt(2.0 / math.pi)
_GELU_A = 0.044715


def _gelu(x):
    return 0.5 * x * (1.0 + jnp.tanh(_GELU_C * (x + _GELU_A * (x * x * x))))


def _gelu_and_grad(x):
    t = jnp.tanh(_GELU_C * (x + _GELU_A * (x * x * x)))
    return (0.5 * x * (1.0 + t),
            0.5 * (1.0 + t) + 0.5 * x * (1.0 - t * t) * (_GELU_C * (1.0 + 3.0 * _GELU_A * (x * x))))


def _sigmoid(x):
    return 1.0 / (1.0 + jnp.exp(-x))


def _rope_fwd(t, cos_t, sin_t):
    return t * cos_t + pltpu.roll(t, 2 * ROPE_HALF, 1) * sin_t


def _rope_bwd(dt, cos_t, sin_t):
    return dt * cos_t - pltpu.roll(dt, 2 * ROPE_HALF, 1) * sin_t


def _prenorm(x, g, tr, token):
    def body(x_ref, g_ref, token_ref, o_ref):
        o_ref[...] = _rms_fwd(x_ref[...], g_ref[...]).astype(BF16)

    t, d = x.shape
    return _row_call(body, "prenorm", t, tr, [(x, d, 0)], [g, token], [(d, BF16)], [])[0]


def _qkv_prep(proj, g_q, g_kv, cos_t, sin_t, lay, tr, token):
    ql, kl = lay["ql"], lay["kl"]

    def body(q_ref, kv_ref, kr_ref, cos_ref, sin_ref, gq_ref, gkv_ref, token_ref, qn_ref, kvn_ref, kro_ref):
        qn_ref[...] = _rms_fwd(q_ref[...], gq_ref[...]).astype(BF16)
        kvn_ref[...] = _rms_fwd(kv_ref[...], gkv_ref[...]).astype(BF16)
        kro_ref[...] = _rope_fwd(kr_ref[...], cos_ref[...], sin_ref[...]).astype(BF16)

    t = proj.shape[0]
    return _row_call(
        body, "qkv_prep", t, tr,
        [(proj, ql, lay["q_off"] // ql), (proj, kl, lay["kv_off"] // kl), (proj, LANES, lay["kr_off"] // LANES),
         (cos_t, LANES, 0), (sin_t, LANES, 0)],
        [g_q, g_kv, token], [(ql, BF16), (kl, BF16), (LANES, BF16)], [])


def _q_rope(q, cos_t, sin_t, heads, tr):
    def body(q_ref, cos_ref, sin_ref, o_ref):
        c, s = cos_ref[...], sin_ref[...]
        for h in range(heads):
            lo = h * HEAD_PAD
            o_ref[:, lo:lo + NOPE_DIM] = q_ref[:, lo:lo + NOPE_DIM].astype(BF16)
            o_ref[:, lo + NOPE_DIM:lo + HEAD_PAD] = _rope_fwd(q_ref[:, lo + NOPE_DIM:lo + HEAD_PAD], c, s).astype(BF16)

    t, w = q.shape
    return _row_call(body, "q_rope", t, tr, [(q, w, 0), (cos_t, LANES, 0), (sin_t, LANES, 0)], [], [(w, BF16)], [])[0]


ATTN_SCALE = 1.0 / math.sqrt(NOPE_DIM + ROPE_DIM)
ATTN_EXP2_SCALE = ATTN_SCALE * math.log2(math.e)


def _attn_tile(t, cap):
    tq = cap
    while t % tq:
        tq //= 2
    return tq


def _attn_fwd(q, kv, kr, heads):
    t = q.shape[0]
    tq = _attn_tile(t, 256)

    def body(q_ref, kv_ref, kr_ref, o_ref, lse_ref, kcat):
        @pl.when(pl.program_id(1) == 0)
        def _():
            kcat[:, :NOPE_DIM] = kv_ref[:, :NOPE_DIM]
            kcat[:, NOPE_DIM:] = kr_ref[...]

        s = lax.dot_general(q_ref[...], kcat[...], NT, preferred_element_type=F32)
        m = jnp.max(s, axis=-1, keepdims=True)
        p = jnp.exp2((s - m) * ATTN_EXP2_SCALE)
        l = jnp.sum(p, axis=-1, keepdims=True)
        o_ref[...] = jnp.dot(p.astype(BF16), kv_ref[:, NOPE_DIM:], preferred_element_type=F32) * (1.0 / l)
        lse_ref[...] = jnp.broadcast_to(m * ATTN_EXP2_SCALE + jnp.log(l) * math.log2(math.e), (tq, V_DIM))

    out_spec = pl.BlockSpec((tq, V_DIM), lambda h, i: (i, h))
    out_shape = jax.ShapeDtypeStruct((t, heads * V_DIM), F32)
    return pl.pallas_call(
        body, name="attn_fwd", grid=(heads, t // tq),
        in_specs=[pl.BlockSpec((tq, HEAD_PAD), lambda h, i: (i, h)),
                  pl.BlockSpec((t, HEAD_PAD), lambda h, i: (0, h)),
                  pl.BlockSpec((t, LANES), lambda h, i: (0, 0))],
        out_specs=[out_spec, out_spec], out_shape=[out_shape, out_shape],
        scratch_shapes=[pltpu.VMEM((t, HEAD_PAD), BF16)],
        compiler_params=_params(("arbitrary", "arbitrary")),
    )(q, kv, kr)


def _attn_bwd(q, kv, kr, out, lse, d_out, cos_t, sin_t, heads):
    t = q.shape[0]
    tq = _attn_tile(t, 256)
    nq = t // tq

    def body(q_ref, kv_ref, kr_ref, o_ref, lse_ref, do_ref, cos_ref, sin_ref, dq_ref, dkv_ref, dkr_ref,
             kcat, dk_acc, dv_acc):
        h, i = pl.program_id(0), pl.program_id(1)

        @pl.when(i == 0)
        def _():
            kcat[:, :NOPE_DIM] = kv_ref[:, :NOPE_DIM]
            kcat[:, NOPE_DIM:] = kr_ref[...]
            dk_acc[...] = jnp.zeros_like(dk_acc)
            dv_acc[...] = jnp.zeros_like(dv_acc)

        @pl.when((h == 0) & (i == 0))
        def _():
            dkr_ref[...] = jnp.zeros_like(dkr_ref)

        qb, dob = q_ref[...], do_ref[...]
        row_term = jnp.sum(dob.astype(F32) * o_ref[...], axis=-1, keepdims=True)
        s = lax.dot_general(qb, kcat[...], NT, preferred_element_type=F32)
        dp = lax.dot_general(dob, kv_ref[:, NOPE_DIM:], NT, preferred_element_type=F32)
        p = jnp.exp2(s * ATTN_EXP2_SCALE - lse_ref[:, :1])
        ds = (p * (dp - row_term)).astype(BF16)
        dv_acc[...] += lax.dot_general(p.astype(BF16), dob, TN, preferred_element_type=F32)
        dq = jnp.dot(ds, kcat[...], preferred_element_type=F32) * ATTN_SCALE
        dq_ref[:, :NOPE_DIM] = dq[:, :NOPE_DIM].astype(BF16)
        dq_ref[:, NOPE_DIM:] = _rope_bwd(dq[:, NOPE_DIM:], cos_ref[...], sin_ref[...]).astype(BF16)
        dk_acc[...] += lax.dot_general(ds, qb, TN, preferred_element_type=F32)

        @pl.when(i == nq - 1)
        def _():
            dkv_ref[:, :NOPE_DIM] = (dk_acc[:, :NOPE_DIM] * ATTN_SCALE).astype(BF16)
            dkv_ref[:, NOPE_DIM:] = dv_acc[...].astype(BF16)
            dkr_ref[...] += dk_acc[:, NOPE_DIM:] * ATTN_SCALE

    return pl.pallas_call(
        body, name="attn_bwd", grid=(heads, nq),
        in_specs=[pl.BlockSpec((tq, HEAD_PAD), lambda h, i: (i, h)),
                  pl.BlockSpec((t, HEAD_PAD), lambda h, i: (0, h)),
                  pl.BlockSpec((t, LANES), lambda h, i: (0, 0)),
                  pl.BlockSpec((tq, V_DIM), lambda h, i: (i, h)),
                  pl.BlockSpec((tq, V_DIM), lambda h, i: (i, h)),
                  pl.BlockSpec((tq, V_DIM), lambda h, i: (i, h)),
                  pl.BlockSpec((tq, LANES), lambda h, i: (i, 0)),
                  pl.BlockSpec((tq, LANES), lambda h, i: (i, 0))],
        out_specs=[pl.BlockSpec((tq, HEAD_PAD), lambda h, i: (i, h)),
                   pl.BlockSpec((t, HEAD_PAD), lambda h, i: (0, h)),
                   pl.BlockSpec((t, LANES), lambda h, i: (0, 0))],
        out_shape=[jax.ShapeDtypeStruct((t, heads * HEAD_PAD), BF16),
                   jax.ShapeDtypeStruct((t, heads * HEAD_PAD), BF16),
                   jax.ShapeDtypeStruct((t, LANES), F32)],
        scratch_shapes=[pltpu.VMEM((t, HEAD_PAD), BF16), pltpu.VMEM((t, HEAD_PAD), F32), pltpu.VMEM((t, V_DIM), F32)],
        compiler_params=_params(("arbitrary", "arbitrary")),
    )(q, kv, kr, out, lse, d_out, cos_t, sin_t)


def _layer_norm_parts(x):
    mu = jnp.mean(x, axis=-1, keepdims=True)
    xc = x - mu
    r = lax.rsqrt(jnp.mean(xc * xc, axis=-1, keepdims=True) + EPS)
    return xc * r, r


def _gmlp_fwd(proj, ln_g, ln_b, w_s, b_sb, g_out_norm, lay):
    gw = lay["gw"]
    g_heads = gw // G_HEAD_DIM

    def body(u_ref, v_ref, lng_ref, lnb_ref, ws_ref, bs_ref, gn_ref, o_ref, gate_ref):
        gu = _gelu(u_ref[...])
        vh, _ = _layer_norm_parts(_gelu(v_ref[...]))
        vln = (vh * lng_ref[...] + lnb_ref[...]).astype(BF16)
        for g in range(g_heads):
            cols = slice(g * G_HEAD_DIM, (g + 1) * G_HEAD_DIM)
            s = jnp.dot(ws_ref[g], vln[:, cols], preferred_element_type=F32) + bs_ref[g]
            gate_ref[:, cols] = gu[:, cols] * s
        o_ref[...] = _rms_fwd(gate_ref[...], gn_ref[...]).astype(BF16)

    t = proj.shape[0]
    in_specs = [pl.BlockSpec((CHUNK, gw), lambda i: (i, 0)), pl.BlockSpec((CHUNK, gw), lambda i: (i, 1))]
    pars = [ln_g, ln_b, w_s, b_sb, g_out_norm]
    in_specs += [pl.BlockSpec(a.shape, functools.partial(lambda i, nd: (0,) * nd, nd=a.ndim)) for a in pars]
    return pl.pallas_call(
        body, name="gmlp_fwd", grid=(t // CHUNK,), in_specs=in_specs,
        out_specs=pl.BlockSpec((CHUNK, gw), lambda i: (i, 0)),
        out_shape=jax.ShapeDtypeStruct((t, gw), BF16),
        scratch_shapes=[pltpu.VMEM((CHUNK, gw), F32)],
        compiler_params=_params(("parallel",)),
    )(proj, proj, *pars)


def _gmlp_bwd(proj, d_mixed, ln_g, ln_b, w_s, w_st, b_sb, g_out_norm, lay):
    gw = lay["gw"]
    g_heads = gw // G_HEAD_DIM
    aw_blocks = lay["aw"] // gw

    def body(u_ref, v_ref, dm_ref, lng_ref, lnb_ref, ws_ref, wst_ref, bs_ref, gn_ref,
             du_ref, dv_ref, dgn_ref, dlng_ref, dlnb_ref, dws_ref, dbs_ref, gate_ref, s_ref, dvln_ref):
        i = pl.program_id(0)
        u, v = u_ref[...], v_ref[...]
        (gu, gelu_du), (gv, gelu_dv) = _gelu_and_grad(u), _gelu_and_grad(v)
        vh, r_ln = _layer_norm_parts(gv)
        vln = (vh * lng_ref[...] + lnb_ref[...]).astype(BF16)
        for g in range(g_heads):
            cols = slice(g * G_HEAD_DIM, (g + 1) * G_HEAD_DIM)
            s = jnp.dot(ws_ref[g], vln[:, cols], preferred_element_type=F32) + bs_ref[g]
            s_ref[:, cols] = s
            gate_ref[:, cols] = gu[:, cols] * s
        d_gate, dgn = _rms_bwd(gate_ref[...], gn_ref[...], dm_ref[...])
        _accumulate(dgn_ref, _colsum(dgn))
        du_ref[...] = (d_gate * s_ref[...] * gelu_du).astype(BF16)
        d_s = d_gate * gu
        d_sb = d_s.astype(BF16)
        for g in range(g_heads):
            cols = slice(g * G_HEAD_DIM, (g + 1) * G_HEAD_DIM)
            dw = lax.dot_general(d_sb[:, cols], vln[:, cols], NT, preferred_element_type=F32)

            @pl.when(i == 0)
            def _():
                dws_ref[g] = dw
                dbs_ref[g] = d_s[:, cols]

            @pl.when(i > 0)
            def _():
                dws_ref[g] += dw
                dbs_ref[g] += d_s[:, cols]

            dvln_ref[:, cols] = jnp.dot(wst_ref[g], d_sb[:, cols], preferred_element_type=F32)
        d_vln = dvln_ref[...]
        _accumulate(dlng_ref, _colsum(d_vln * vh))
        _accumulate(dlnb_ref, _colsum(d_vln))
        d_vh = d_vln * lng_ref[...]
        d_gv = r_ln * (d_vh - jnp.mean(d_vh, axis=-1, keepdims=True)
                       - vh * jnp.mean(d_vh * vh, axis=-1, keepdims=True))
        dv_ref[...] = (d_gv * gelu_dv).astype(BF16)

    t = proj.shape[0]
    whole = lambda a: pl.BlockSpec(a.shape, functools.partial(lambda i, nd: (0,) * nd, nd=a.ndim))
    pars = [ln_g, ln_b, w_s, w_st, b_sb, g_out_norm]
    hshape = (g_heads, CHUNK, CHUNK)
    return pl.pallas_call(
        body, name="gmlp_bwd", grid=(t // CHUNK,),
        in_specs=[pl.BlockSpec((CHUNK, gw), lambda i: (i, 0)), pl.BlockSpec((CHUNK, gw), lambda i: (i, 1)),
                  pl.BlockSpec((CHUNK, gw), lambda i: (i, aw_blocks))] + [whole(a) for a in pars],
        out_specs=[pl.BlockSpec((CHUNK, gw), lambda i: (i, 0)), pl.BlockSpec((CHUNK, gw), lambda i: (i, 0)),
                   pl.BlockSpec((1, gw), lambda i: (0, 0)), pl.BlockSpec((1, gw), lambda i: (0, 0)),
                   pl.BlockSpec((1, gw), lambda i: (0, 0)),
                   pl.BlockSpec(hshape, lambda i: (0, 0, 0)), pl.BlockSpec(hshape, lambda i: (0, 0, 0))],
        out_shape=[jax.ShapeDtypeStruct((t, gw), BF16), jax.ShapeDtypeStruct((t, gw), BF16),
                   jax.ShapeDtypeStruct((1, gw), F32), jax.ShapeDtypeStruct((1, gw), F32),
                   jax.ShapeDtypeStruct((1, gw), F32),
                   jax.ShapeDtypeStruct(hshape, F32), jax.ShapeDtypeStruct(hshape, F32)],
        scratch_shapes=[pltpu.VMEM((CHUNK, gw), F32), pltpu.VMEM((CHUNK, gw), F32), pltpu.VMEM((CHUNK, gw), F32)],
        compiler_params=_params(("arbitrary",)),
    )(proj, proj, d_mixed, *pars)


def _spatial_bias_grad(dbs_wide):
    g_heads = dbs_wide.shape[0]

    def body(x_ref, o_ref):
        for g in range(g_heads):
            o_ref[g:g + 1, :] = jnp.sum(x_ref[g].T, axis=0, keepdims=True)

    return pl.pallas_call(
        body, name="spatial_bias_grad", out_shape=jax.ShapeDtypeStruct((g_heads, CHUNK), F32),
        in_specs=[pl.BlockSpec(memory_space=pltpu.VMEM)], out_specs=pl.BlockSpec(memory_space=pltpu.VMEM),
    )(dbs_wide)


def _mix_norm(a_out, gn, g_a, tr, token):
    aw = a_out.shape[1]
    gw = gn.shape[1]

    def body(a_ref, gn_ref, g_ref, token_ref, o_ref):
        o_ref[:, :aw] = _rms_fwd(a_ref[...], g_ref[...]).astype(BF16)
        o_ref[:, aw:] = gn_ref[...]

    t = a_out.shape[0]
    return _row_call(body, "mix_norm", t, tr, [(a_out, aw, 0), (gn, gw, 0)], [g_a, token], [(aw + gw, BF16)], [])[0]


def _mix_norm_bwd(a_out, d_mixed, g_a, tr, token):
    aw = a_out.shape[1]

    def body(a_ref, dm_ref, g_ref, token_ref, da_ref, dg_ref):
        dx, dg = _rms_bwd(a_ref[...], g_ref[...], dm_ref[...])
        da_ref[...] = dx.astype(BF16)
        _accumulate(dg_ref, _colsum(dg))

    t = a_out.shape[0]
    return _row_call(body, "mix_norm_bwd", t, tr, [(a_out, aw, 0), (d_mixed, aw, 0)], [g_a, token],
                     [(aw, BF16)], [((1, aw), F32)])


def _post_mix(x, mix_out, g_pm, g_pf, tr):
    def body(x_ref, mo_ref, gpm_ref, gpf_ref, h_ref, hn_ref):
        h = x_ref[...] + _rms_fwd(mo_ref[...], gpm_ref[...])
        h_ref[...] = h
        hn_ref[...] = _rms_fwd(h, gpf_ref[...]).astype(BF16)

    t, d = x.shape
    return _row_call(body, "post_mix", t, tr, [(x, d, 0), (mix_out, d, 0)], [g_pm, g_pf], [(d, F32), (d, BF16)], [])


def _post_mix_bwd(mix_out, h, dy, d_hn, g_pm, g_pf, tr, token):
    def body(mo_ref, h_ref, dy_ref, dhn_ref, gpm_ref, gpf_ref, token_ref, dh_ref, dmo_ref, dgpf_ref, dgpm_ref):
        dx, dg = _rms_bwd(h_ref[...], gpf_ref[...], dhn_ref[...])
        dh = dy_ref[...] + dx
        dh_ref[...] = dh
        _accumulate(dgpf_ref, _colsum(dg))
        dmo, dg2 = _rms_bwd(mo_ref[...], gpm_ref[...], dh)
        dmo_ref[...] = dmo.astype(BF16)
        _accumulate(dgpm_ref, _colsum(dg2))

    t, d = h.shape
    return _row_call(body, "post_mix_bwd", t, tr, [(mix_out, d, 0), (h, d, 0), (dy, d, 0), (d_hn, d, 0)],
                     [g_pm, g_pf, token], [(d, F32), (d, BF16)], [((1, d), F32), ((1, d), F32)])


def _swiglu(gate, up):
    t, f = gate.shape
    tr, tf = _tile(t, 512), _tile(f, 2048)

    def body(g_ref, u_ref, o_ref):
        g = g_ref[...].astype(F32)
        o_ref[...] = (g * _sigmoid(g) * u_ref[...].astype(F32)).astype(BF16)

    spec = pl.BlockSpec((tr, tf), lambda i, j: (i, j))
    return pl.pallas_call(body, name="swiglu", grid=(t // tr, f // tf), in_specs=[spec, spec], out_specs=spec,
                          out_shape=jax.ShapeDtypeStruct((t, f), BF16),
                          compiler_params=_params(("parallel", "parallel")))(gate, up)


def _swiglu_bwd(gate, up, d_act):
    t, f = gate.shape
    tr, tf = _tile(t, 512), _tile(f, 2048)

    def body(g_ref, u_ref, da_ref, dg_ref, du_ref):
        g, u, da = g_ref[...].astype(F32), u_ref[...].astype(F32), da_ref[...].astype(F32)
        sg = _sigmoid(g)
        du_ref[...] = (da * (g * sg)).astype(BF16)
        dg_ref[...] = (da * u * (sg * (1.0 + g * (1.0 - sg)))).astype(BF16)

    spec = pl.BlockSpec((tr, tf), lambda i, j: (i, j))
    shape = jax.ShapeDtypeStruct((t, f), BF16)
    return pl.pallas_call(body, name="swiglu_bwd", grid=(t // tr, f // tf), in_specs=[spec, spec, spec],
                          out_specs=[spec, spec], out_shape=[shape, shape],
                          compiler_params=_params(("parallel", "parallel")))(gate, up, d_act)


def _loss_head(h, ffn, target, g_po, tr):
    t, d = h.shape

    def body(h_ref, f_ref, t_ref, g_ref, dy_ref, df_ref, dg_ref, loss_ref):
        f = f_ref[...]
        err = h_ref[...] + _rms_fwd(f, g_ref[...]) - t_ref[...]
        dy = err * (1.0 / d)
        dy_ref[...] = dy
        df, dg = _rms_bwd(f, g_ref[...], dy)
        df_ref[...] = df.astype(BF16)
        _accumulate(dg_ref, _colsum(dg))
        sq = jnp.sum(_colsum(err * err), axis=-1, keepdims=True) * (0.5 / d)
        _accumulate(loss_ref, jnp.broadcast_to(sq, (1, LANES)))

    return _row_call(body, "loss_head", t, tr, [(h, d, 0), (ffn, d, 0), (target, d, 0)], [g_po],
                     [(d, F32), (d, BF16)], [((1, d), F32), ((1, LANES), F32)])


def _qkv_bwd(proj, d_qn, d_kvn, d_kr, g_q, g_kv, cos_t, sin_t, lay, tr):
    ql, kl = lay["ql"], lay["kl"]

    def body(q_ref, kv_ref, dqn_ref, dkvn_ref, dkr_ref, cos_ref, sin_ref, gq_ref, gkv_ref,
             dq_ref, dkv_ref, dkt_ref, dgq_ref, dgkv_ref):
        dx, dg = _rms_bwd(q_ref[...], gq_ref[...], dqn_ref[...])
        dq_ref[...] = dx.astype(BF16)
        _accumulate(dgq_ref, _colsum(dg))
        dx, dg = _rms_bwd(kv_ref[...], gkv_ref[...], dkvn_ref[...])
        dkv_ref[...] = dx.astype(BF16)
        _accumulate(dgkv_ref, _colsum(dg))
        dkt_ref[...] = _rope_bwd(dkr_ref[...], cos_ref[...], sin_ref[...]).astype(BF16)

    t = proj.shape[0]
    return _row_call(
        body, "qkv_bwd", t, tr,
        [(proj, ql, lay["q_off"] // ql), (proj, kl, lay["kv_off"] // kl), (d_qn, ql, 0), (d_kvn, kl, 0),
         (d_kr, LANES, 0), (cos_t, LANES, 0), (sin_t, LANES, 0)],
        [g_q, g_kv], [(ql, BF16), (kl, BF16), (LANES, BF16)], [((1, ql), F32), ((1, kl), F32)])


def _prenorm_bwd(x, d_xn, dh, g, tr):
    def body(x_ref, dxn_ref, dh_ref, g_ref, gx_ref, dg_ref):
        dx, dg = _rms_bwd(x_ref[...], g_ref[...], dxn_ref[...])
        gx_ref[...] = dh_ref[...] + dx
        _accumulate(dg_ref, _colsum(dg))

    t, d = x.shape
    return _row_call(body, "prenorm_bwd", t, tr, [(x, d, 0), (d_xn, d, 0), (dh, d, 0)], [g],
                     [(d, F32)], [((1, d), F32)])


def _adam_rows(rows, cols, block_elements=256 * 1024):
    cap = max(8, block_elements // cols // 8 * 8)
    tr = min(rows, cap)
    while rows % tr:
        tr -= 8
    return tr


def _adamw(w, g, m, v, name):
    rows, cols = w.shape
    tr = _adam_rows(rows, cols)

    def body(w_ref, g_ref, m_ref, v_ref, d_ref, mo_ref, vo_ref):
        g = g_ref[...]
        m2 = ADAM_B1 * m_ref[...] + (1.0 - ADAM_B1) * g
        v2 = ADAM_B2 * v_ref[...] + (1.0 - ADAM_B2) * (g * g)
        m_hat = m2 / (1.0 - ADAM_B1 ** ADAM_STEP)
        v_hat = v2 / (1.0 - ADAM_B2 ** ADAM_STEP)
        d_ref[...] = -ADAM_LR * (m_hat / (jnp.sqrt(v_hat) + ADAM_EPS) + ADAM_WD * w_ref[...])
        mo_ref[...] = m2
        vo_ref[...] = v2

    spec = pl.BlockSpec((tr, cols), lambda i: (i, 0))
    shape = jax.ShapeDtypeStruct((rows, cols), F32)
    return pl.pallas_call(body, name=name, grid=(rows // tr,), in_specs=[spec] * 4, out_specs=[spec] * 3,
                          out_shape=[shape] * 3, compiler_params=_params(("parallel",)))(w, g, m, v)


def _adamw_halves(w, g_mine, g_theirs, m, v, name):
    rows, cols = w.shape
    rh = g_mine.shape[0]
    tr = _adam_rows(math.gcd(rows, rh), cols)
    per_half = rh // tr
    my_c = jnp.reshape(lax.axis_index("c"), (1,)).astype(jnp.int32)

    def body(c_ref, w_ref, gm_ref, gt_ref, m_ref, v_ref, g_ref, d_ref, mo_ref, vo_ref):
        mine = (pl.program_id(0) // per_half) == c_ref[0]
        g = jnp.where(mine, gm_ref[...], gt_ref[...])
        m2 = ADAM_B1 * m_ref[...] + (1.0 - ADAM_B1) * g
        v2 = ADAM_B2 * v_ref[...] + (1.0 - ADAM_B2) * (g * g)
        m_hat = m2 / (1.0 - ADAM_B1 ** ADAM_STEP)
        v_hat = v2 / (1.0 - ADAM_B2 ** ADAM_STEP)
        g_ref[...] = g
        d_ref[...] = -ADAM_LR * (m_hat / (jnp.sqrt(v_hat) + ADAM_EPS) + ADAM_WD * w_ref[...])
        mo_ref[...] = m2
        vo_ref[...] = v2

    def half_spec(is_mine):
        def index(i, c_ref):
            used = ((i // per_half) == c_ref[0]) if is_mine else ((i // per_half) != c_ref[0])
            return (jnp.where(used, i % per_half, 0), 0)
        return pl.BlockSpec((tr, cols), index)

    spec = pl.BlockSpec((tr, cols), lambda i, c_ref: (i, 0))
    shape = jax.ShapeDtypeStruct((rows, cols), F32)
    grid_spec = pltpu.PrefetchScalarGridSpec(
        num_scalar_prefetch=1, grid=(rows // tr,),
        in_specs=[spec, half_spec(True), half_spec(False), spec, spec], out_specs=[spec] * 4)
    return pl.pallas_call(body, name=name, grid_spec=grid_spec, out_shape=[shape] * 4,
                          compiler_params=_params(("parallel",)))(my_c, w, g_mine, g_theirs, m, v)


def _pair_add(parts, theirs, name):
    _, n, r, c = parts.shape
    tr = _adam_rows(r, c, 1024 * 1024)
    my_c = jnp.reshape(lax.axis_index("c"), (1,)).astype(jnp.int32)

    def body(c_ref, a_ref, b_ref, o_ref):
        o_ref[0] = (a_ref[0, 0].astype(F32) + b_ref[0].astype(F32)).astype(BF16)

    spec = pl.BlockSpec((1, tr, c), lambda k, i, c_ref: (k, i, 0))
    grid_spec = pltpu.PrefetchScalarGridSpec(
        num_scalar_prefetch=1, grid=(n, r // tr),
        in_specs=[pl.BlockSpec((1, 1, tr, c), lambda k, i, c_ref: (c_ref[0], k, i, 0)), spec], out_specs=spec)
    return pl.pallas_call(body, name=name, grid_spec=grid_spec, out_shape=jax.ShapeDtypeStruct((n, r, c), BF16),
                          compiler_params=_params(("parallel", "parallel")))(my_c, parts, theirs)


def _chip_sum(pair_sums, received, name, token=None):
    _, r, c = pair_sums.shape
    tr = _adam_rows(r, c)
    own = 2 * lax.axis_index("x") + lax.axis_index("y")

    def body(own_ref, p_ref, r0_ref, r1_ref, r2_ref, *rest):
        o_ref = rest[-1]
        acc = p_ref[0].astype(F32) + r0_ref[0].astype(F32)
        acc = acc + r1_ref[0].astype(F32)
        o_ref[...] = acc + r2_ref[0].astype(F32)

    def rspec(j):
        return pl.BlockSpec((1, tr, c), functools.partial(lambda i, own_ref, j: (j, i, 0), j=j))

    extra = [] if token is None else [token]
    grid_spec = pltpu.PrefetchScalarGridSpec(
        num_scalar_prefetch=1, grid=(r // tr,),
        in_specs=[pl.BlockSpec((1, tr, c), lambda i, own_ref: (own_ref[0], i, 0)), rspec(0), rspec(1), rspec(2)]
        + [pl.BlockSpec(memory_space=pl.ANY)] * len(extra),
        out_specs=pl.BlockSpec((tr, c), lambda i, own_ref: (i, 0)))
    return pl.pallas_call(body, name=name, grid_spec=grid_spec, out_shape=jax.ShapeDtypeStruct((r, c), F32),
                          compiler_params=_params(("parallel",)))(
        jnp.reshape(own, (1,)).astype(jnp.int32), pair_sums, received, received, received, *extra)


def _mesh_place():
    x, y, c = lax.axis_index("x"), lax.axis_index("y"), lax.axis_index("c")
    other_chips = [(1 - x, y), (x, 1 - y), (1 - x, 1 - y)]
    return x, y, c, other_chips


def _hbm_specs(n):
    return [pl.BlockSpec(memory_space=pltpu.HBM)] * n


def _sibling_exchange(parts, name):
    n = len(parts)

    def body(*refs):
        ins, outs = refs[:n], refs[n:2 * n]
        send_sems, recv_sems = refs[2 * n:]
        x, y, c, _ = _mesh_place()
        copies = [pltpu.make_async_remote_copy(src_ref=ins[w].at[1 - c], dst_ref=outs[w], send_sem=send_sems.at[w],
                                               recv_sem=recv_sems.at[w], device_id=(x, y, 1 - c), device_id_type=MESH)
                  for w in range(n)]
        for cp in copies:
            cp.start()
        for cp in copies:
            cp.wait()

    return pl.pallas_call(
        body, name=name,
        out_shape=[jax.ShapeDtypeStruct(p.shape[1:], p.dtype) for p in parts],
        in_specs=_hbm_specs(n), out_specs=_hbm_specs(n),
        scratch_shapes=[pltpu.SemaphoreType.DMA((n,)), pltpu.SemaphoreType.DMA((n,))],
    )(*parts)


SEM_SPEC = pl.BlockSpec(memory_space=pltpu.SEMAPHORE)
DATAFLOW_EFFECT = pltpu.SideEffectType.DATAFLOW_SIDE_EFFECTING


def _copies_per_weight(kind):
    return {"gather": 4, "scatter": 3, "sibling": 1, "forward": 3, "swap": 1}[kind]


def _flight_copies(kind, src_refs, land_refs, send_sems, recv_sems, arriving):
    x, y, c, other_chips = _mesh_place()
    own = 2 * x + y
    sibling = (x, y, 1 - c)
    per = _copies_per_weight(kind)
    copies = []
    for w in range(len(src_refs)):
        def remote(src, dst, j, to):
            return pltpu.make_async_remote_copy(src_ref=src, dst_ref=dst, send_sem=send_sems.at[per * w + j],
                                                recv_sem=recv_sems.at[per * w + j], device_id=to, device_id_type=MESH)

        if kind == "sibling":
            copies.append(remote(src_refs[w].at[1 - c], land_refs[w], 0, sibling))
            continue
        if kind == "swap":
            copies.append(remote(src_refs[w], land_refs[w], 0, sibling))
            continue
        for j, chip in enumerate(other_chips):
            theirs = 2 * chip[0] + chip[1]
            if kind == "gather":
                copies.append(remote(src_refs[w].at[c], land_refs[w].at[theirs if arriving else own, c], j, (*chip, c)))
            elif kind == "forward":
                copies.append(remote(src_refs[w].at[theirs, c], src_refs[w].at[theirs, (1 - c) if arriving else c],
                                     j, sibling))
            else:
                copies.append(remote(src_refs[w].at[theirs], land_refs[w].at[j], j, (*chip, c)))
        if kind == "gather":
            copies.append(remote(src_refs[w], land_refs[w].at[own], 3, sibling))
    return copies


def _ici_start(kind, srcs, name, after=None):
    n = len(srcs)
    if kind == "gather":
        lands = [lax.empty((4,) + s.shape, s.dtype) for s in srcs]
    elif kind == "scatter":
        lands = [lax.empty((3,) + s.shape[1:], s.dtype) for s in srcs]
    elif kind == "sibling":
        lands = [lax.empty(s.shape[1:], s.dtype) for s in srcs]
    elif kind == "swap":
        lands = [lax.empty(s.shape, s.dtype) for s in srcs]
    else:
        lands = []
    nb = n + len(lands)
    afters = [] if after is None else [after]

    def body(*refs):
        src_refs, land_refs = refs[:n], refs[n:nb]
        send_sems, recv_sems = refs[nb + len(afters)], refs[nb + len(afters) + 1]
        token = refs[-1]
        for cp in _flight_copies(kind, src_refs, land_refs, send_sems, recv_sems, False):
            cp.start()
        token[...] = jnp.zeros_like(token)

    hbm = lambda a: pltpu.with_memory_space_constraint(a, pltpu.HBM)
    n_sems = _copies_per_weight(kind) * n
    out = pl.pallas_call(
        body, name=name,
        out_shape=(pltpu.SemaphoreType.DMA((n_sems,)), pltpu.SemaphoreType.DMA((n_sems,)),
                   *[pltpu.HBM(a.shape, a.dtype) for a in srcs + lands], jax.ShapeDtypeStruct((8, LANES), F32)),
        in_specs=_hbm_specs(nb) + [pl.BlockSpec(memory_space=pl.ANY)] * len(afters),
        out_specs=(SEM_SPEC, SEM_SPEC, *_hbm_specs(nb), pl.BlockSpec(memory_space=pltpu.VMEM)),
        input_output_aliases={i: 2 + i for i in range(nb)},
        compiler_params=pltpu.CompilerParams(has_side_effects=DATAFLOW_EFFECT),
    )(*[hbm(a) for a in srcs + lands], *afters)
    return out[0], out[1], list(out[2:2 + n]), list(out[2 + n:2 + nb]), out[-1]


def _ici_wait(kind, send_sems, recv_sems, srcs, lands, after, name):
    n = len(srcs)
    nb = n + len(lands)

    def body(*refs):
        src_refs, land_refs = refs[:n], refs[n:nb]
        send_ref, recv_ref = refs[nb], refs[nb + 1]
        for cp in _flight_copies(kind, src_refs, land_refs, send_ref, recv_ref, True):
            cp.wait_send()
            cp.wait_recv()

    out = pl.pallas_call(
        body, name=name, out_shape=tuple(pltpu.HBM(a.shape, a.dtype) for a in srcs + lands),
        in_specs=_hbm_specs(nb) + [SEM_SPEC, SEM_SPEC, pl.BlockSpec(memory_space=pl.ANY)],
        out_specs=tuple(_hbm_specs(nb)), input_output_aliases={i: i for i in range(nb)},
        compiler_params=pltpu.CompilerParams(has_side_effects=DATAFLOW_EFFECT),
    )(*srcs, *lands, send_sems, recv_sems, after)
    return list(out[:n]), list(out[n:])


def _halves_exchange(halves, name):
    n = len(halves)

    def body(*refs):
        ins, outs = refs[:n], refs[n:2 * n]
        send_sems, recv_sems = refs[2 * n:]
        x, y, c, _ = _mesh_place()
        copies = [pltpu.make_async_remote_copy(src_ref=ins[w], dst_ref=outs[w], send_sem=send_sems.at[w],
                                               recv_sem=recv_sems.at[w], device_id=(x, y, 1 - c), device_id_type=MESH)
                  for w in range(n)]
        for cp in copies:
            cp.start()
        for cp in copies:
            cp.wait()

    return pl.pallas_call(
        body, name=name,
        out_shape=[jax.ShapeDtypeStruct(h.shape, h.dtype) for h in halves],
        in_specs=_hbm_specs(n), out_specs=_hbm_specs(n),
        scratch_shapes=[pltpu.SemaphoreType.DMA((n,)), pltpu.SemaphoreType.DMA((n,))],
    )(*halves)


def _small_all_reduce(packed):
    rows = packed.shape[0]

    def body(in_ref, out_ref, from_sibling, pair_sums, send_sems, recv_sems):
        x, y, c, other_chips = _mesh_place()
        own = 2 * x + y

        def remote(src, dst, k, to):
            return pltpu.make_async_remote_copy(src_ref=src, dst_ref=dst, send_sem=send_sems.at[k],
                                                recv_sem=recv_sems.at[k], device_id=to, device_id_type=MESH)

        swap = remote(in_ref, from_sibling, 0, (x, y, 1 - c))
        swap.start()
        swap.wait()
        mine, theirs = in_ref[...], from_sibling[...]
        pair_sums[own] = jnp.where(c == 0, mine, theirs) + jnp.where(c == 0, theirs, mine)
        copies = [remote(pair_sums.at[own], pair_sums.at[own], 1 + j, (*chip, c)) for j, chip in enumerate(other_chips)]
        for cp in copies:
            cp.start()
        for j, chip in enumerate(other_chips):
            landing = pair_sums.at[2 * chip[0] + chip[1]]
            remote(landing, landing, 1 + j, (*chip, c)).wait_recv()
        for cp in copies:
            cp.wait_send()
        out_ref[...] = ((pair_sums[0] + pair_sums[1]) + pair_sums[2]) + pair_sums[3]

    return pl.pallas_call(
        body, name="small_all_reduce", out_shape=jax.ShapeDtypeStruct(packed.shape, F32),
        in_specs=[pl.BlockSpec(memory_space=pltpu.VMEM)], out_specs=pl.BlockSpec(memory_space=pltpu.VMEM),
        scratch_shapes=[pltpu.VMEM((rows, LANES), F32), pltpu.VMEM((4, rows, LANES), F32),
                        pltpu.SemaphoreType.DMA((4,)), pltpu.SemaphoreType.DMA((4,))],
        compiler_params=pltpu.CompilerParams(vmem_limit_bytes=VMEM_LIMIT_BYTES),
    )(packed)


def _layout(w_in, w_uq, w_ukv, v_ln_gain, q_norm, kv_norm):
    heads = 4 * w_uq.shape[-1] // (NOPE_DIM + ROPE_DIM)
    gw = v_ln_gain.shape[-1]
    ql, kl = q_norm.shape[-1], kv_norm.shape[-1]
    lay = dict(heads=heads, gw=gw, ql=ql, kl=kl, aw=heads * V_DIM, u_off=0, v_off=gw, q_off=2 * gw,
               kv_off=2 * gw + ql, kr_off=2 * gw + ql + kl)
    lay["in_pad"] = _round_up(lay["kr_off"] + LANES, 2 * LANES if lay["kr_off"] + LANES <= 2048 else 1024)
    assert lay["q_off"] % ql == 0 and lay["kv_off"] % kl == 0 and lay["aw"] % gw == 0
    assert 4 * w_in.shape[-1] == ql + kl + ROPE_DIM + 2 * gw
    return lay


def _rope_tile(t1, t2, axis=-1):
    z = jnp.zeros_like(t1)
    return jnp.concatenate([t1, z, t2, z], axis=axis)


def _w_in_rows(gathered, shard_rows, lay):
    d = gathered.shape[-1]
    wt = gathered[:, :shard_rows].reshape(4 * shard_rows, d)
    ql, kl, gw = lay["ql"], lay["kl"], lay["gw"]
    q_c, kv_c = wt[:ql], wt[ql:ql + kl]
    r = wt[ql + kl:ql + kl + ROPE_DIM]
    u = wt[ql + kl + ROPE_DIM:ql + kl + ROPE_DIM + gw]
    v = wt[ql + kl + ROPE_DIM + gw:]
    parts = [u, v, q_c, kv_c, _rope_tile(r[:ROPE_HALF], r[ROPE_HALF:], axis=0)]
    pad = lay["in_pad"] - (lay["kr_off"] + LANES)
    if pad:
        parts.append(jnp.zeros((pad, d), wt.dtype))
    return jnp.concatenate(parts, axis=0)


def _w_in_grad_chunks(dwt, shard_rows, padded_rows, lay):
    d = dwt.shape[-1]
    ql, kl, gw = lay["ql"], lay["kl"], lay["gw"]
    ko = lay["kr_off"]
    rows = jnp.concatenate([dwt[lay["q_off"]:lay["q_off"] + ql], dwt[lay["kv_off"]:lay["kv_off"] + kl],
                            dwt[ko:ko + ROPE_HALF], dwt[ko + 2 * ROPE_HALF:ko + 3 * ROPE_HALF],
                            dwt[:gw], dwt[gw:2 * gw]], axis=0).reshape(4, shard_rows, d)
    rows = jnp.pad(rows, ((0, 0), (0, padded_rows - shard_rows), (0, 0)))
    return jnp.transpose(rows.reshape(4, 2, padded_rows // 2, d), (1, 0, 2, 3)).astype(BF16)


def _w_uq_padded(w, heads):
    w3 = w.reshape(w.shape[0], heads, NOPE_DIM + ROPE_DIM)
    t = _rope_tile(w3[..., NOPE_DIM:NOPE_DIM + ROPE_HALF], w3[..., NOPE_DIM + ROPE_HALF:])
    return jnp.concatenate([w3[..., :NOPE_DIM], t], axis=-1).reshape(w.shape[0], heads * HEAD_PAD)


def _w_uq_grad_unpadded(dw, heads):
    d3 = dw.reshape(dw.shape[0], heads, HEAD_PAD)
    return jnp.concatenate([d3[..., :NOPE_DIM], d3[..., NOPE_DIM:NOPE_DIM + ROPE_HALF],
                            d3[..., NOPE_DIM + 2 * ROPE_HALF:NOPE_DIM + 3 * ROPE_HALF]],
                           axis=-1).reshape(dw.shape[0], heads * (NOPE_DIM + ROPE_DIM))


def _cols_gathered(g):
    return jnp.transpose(g, (1, 0, 2)).reshape(g.shape[1], 4 * g.shape[2])


def _chunks_of_cols(grad):
    r, c4 = grad.shape
    return jnp.transpose(grad.reshape(2, r // 2, 4, c4 // 4), (0, 2, 1, 3)).astype(BF16)


SMALL = ["pre_mix_norm", "q_norm", "kv_norm", "v_ln_gain", "v_ln_bias", "w_spatial", "b_spatial", "attn_out_norm",
         "gmlp_out_norm", "post_mix_norm", "pre_ffn_norm", "post_ffn_norm"]
BIG = ["w_in", "w_uq", "w_ukv", "w_out", "w_gate", "w_up", "w_down"]
GATHER_NOW = ["w_in"]
GATHER_SOON = ["w_uq", "w_ukv"]
GATHER_LATER = ["w_out", "w_gate", "w_up", "w_down"]
REDUCE_FFN = ["w_gate", "w_up", "w_down"]
REDUCE_OUT = ["w_out"]
REDUCE_LAST = ["w_in", "w_uq", "w_ukv"]
TRANSPOSED = ("w_in", "w_gate", "w_up")
ORDER = ["pre_mix_norm", "w_in", "q_norm", "kv_norm", "w_uq", "w_ukv", "v_ln_gain", "v_ln_bias", "w_spatial",
         "b_spatial", "attn_out_norm", "gmlp_out_norm", "w_out", "post_mix_norm", "pre_ffn_norm", "w_gate", "w_up",
         "w_down", "post_ffn_norm"]


def _pack(arrays):
    flat = jnp.concatenate([a.reshape(-1) for a in arrays])
    n = flat.shape[0]
    total = _round_up(n, 8 * LANES)
    if total > n:
        flat = jnp.concatenate([flat, jnp.zeros((total - n,), F32)])
    return flat.reshape(total // LANES, LANES)


def _unpack(packed, like):
    flat = packed.reshape(-1)
    out, off = [], 0
    for a in like:
        out.append(flat[off:off + a.size].reshape(a.shape))
        off += a.size
    return out


def kernel(x, positions, pre_mix_norm, w_in, q_norm, kv_norm, w_uq, w_ukv, v_ln_gain, v_ln_bias, w_spatial, b_spatial, attn_out_norm, gmlp_out_norm, w_out, post_mix_norm, pre_ffn_norm, w_gate, w_up, w_down, post_ffn_norm, loss_target, m_pre_mix_norm, m_w_in, m_q_norm, m_kv_norm, m_w_uq, m_w_ukv, m_v_ln_gain, m_v_ln_bias, m_w_spatial, m_b_spatial, m_attn_out_norm, m_gmlp_out_norm, m_w_out, m_post_mix_norm, m_pre_ffn_norm, m_w_gate, m_w_up, m_w_down, m_post_ffn_norm, v_pre_mix_norm, v_w_in, v_q_norm, v_kv_norm, v_w_uq, v_w_ukv, v_v_ln_gain, v_v_ln_bias, v_w_spatial, v_b_spatial, v_attn_out_norm, v_gmlp_out_norm, v_w_out, v_post_mix_norm, v_pre_ffn_norm, v_w_gate, v_w_up, v_w_down, v_post_ffn_norm):
    args = dict(locals())
    weights = {n: args[n] for n in ORDER}
    m_in = {n: args["m_" + n] for n in ORDER}
    v_in = {n: args["v_" + n] for n in ORDER}

    lay = _layout(w_in, w_uq, w_ukv, v_ln_gain, q_norm, kv_norm)
    heads, gw = lay["heads"], lay["gw"]
    t, d = x.shape[1], x.shape[2]
    tr = 128 if t % 128 == 0 else t
    xs = x.reshape(t, d)
    target = loss_target.reshape(t, d)

    ffs, ins = w_gate.shape[-1], w_in.shape[-1]
    ffp, inp = _round_up(ffs, LANES), _round_up(ins, LANES)
    padded_rows = {"w_gate": ffp, "w_up": ffp, "w_down": ffp, "w_in": inp}
    shards, halved = {}, {}

    def cast_shard(n, one=None):
        w = jnp.swapaxes(weights[n][0], 0, 1) if n in TRANSPOSED else weights[n][0]
        s = (w if one is None else w * one).astype(BF16)
        if n in padded_rows:
            s = jnp.pad(s, ((0, padded_rows[n] - s.shape[0]), (0, 0)))
        shards[n] = s
        halved[n] = s.reshape(2, s.shape[0] // 2, s.shape[1])

    for n in GATHER_NOW:
        cast_shard(n)
    full = {}

    def pair_sums_of(partial, names, tag):
        from_sibling = _sibling_exchange([partial[n] for n in names], "grads_sibling_exchange_" + tag)
        return [_pair_add(partial[n], r, "pair_add_" + n) for n, r in zip(names, from_sibling)]

    def place(names, lands):
        for n, g in zip(names, lands):
            full[n] = g.reshape((4,) + shards[n].shape)

    flight_0 = _ici_start("gather", [halved[n] for n in GATHER_NOW], "gather_start_0")
    for n in GATHER_SOON + GATHER_LATER:
        cast_shard(n, 1.0 + flight_0[4][0, 0])
    flight_soon = _ici_start("gather", [halved[n] for n in GATHER_SOON], "gather_start_soon", after=flight_0[4])
    flights, last_token = {}, flight_soon[4]
    for n in GATHER_LATER:
        flights[n] = _ici_start("gather", [halved[n]], "gather_start_" + n, after=last_token)
        last_token = flights[n][4]
    xn = _prenorm(xs, pre_mix_norm.reshape(1, -1), tr, last_token)
    _, lands = _ici_wait("gather", *flight_0[:4], xn, "gather_wait_0")
    pass_now = _ici_start("forward", lands, "forward_start_now")
    place(GATHER_NOW, _ici_wait("forward", *pass_now[:4], pass_now[4], "forward_wait_now")[0])
    wt_in = _w_in_rows(full["w_in"], ins, lay)

    inv_freq = 1.0 / (ROPE_THETA ** (jnp.arange(0, ROPE_DIM, 2, dtype=F32) / ROPE_DIM))
    ang = positions.reshape(t).astype(F32)[:, None] * inv_freq
    cos, sin = jnp.cos(ang), jnp.sin(ang)
    cos_t = _rope_tile(cos, cos)
    sin_t = _rope_tile(-sin, sin)

    row = lambda a: a.reshape(1, -1)
    g_pre, g_q, g_kv = row(pre_mix_norm), row(q_norm), row(kv_norm)
    g_a, g_g, g_pm = row(attn_out_norm), row(gmlp_out_norm), row(post_mix_norm)
    g_pf, g_po = row(pre_ffn_norm), row(post_ffn_norm)
    ln_g, ln_b = row(v_ln_gain), row(v_ln_bias)
    ws = w_spatial[0].astype(BF16)
    ws_t = jnp.transpose(ws, (0, 2, 1))
    bs_wide = jnp.broadcast_to(b_spatial[0][:, :, None], b_spatial.shape[1:] + (G_HEAD_DIM,))

    proj = _matmul(xn, wt_in, NT, F32, "proj")
    _, lands = _ici_wait("gather", *flight_soon[:4], proj, "gather_wait_soon")
    pass_soon = _ici_start("forward", lands, "forward_start_soon")
    qn, kvn, kr = _qkv_prep(proj, g_q, g_kv, cos_t, sin_t, lay, tr, pass_soon[4])
    place(GATHER_SOON, _ici_wait("forward", *pass_soon[:4], qn, "forward_wait_soon")[0])
    wb_uq = _w_uq_padded(_cols_gathered(full["w_uq"]), heads)
    wb_ukv = _cols_gathered(full["w_ukv"])
    q = _q_rope(_matmul(qn, wb_uq, NN, F32, "q_up"), cos_t, sin_t, heads, tr)
    kv = _matmul(kvn, wb_ukv, NN, BF16, "kv_up")
    a_out, a_lse = _attn_fwd(q, kv, kr, heads)
    def arrive(n, after):
        _, lands = _ici_wait("gather", *flights[n][:4], after, "gather_wait_" + n)
        return _ici_start("forward", lands, "forward_start_" + n)

    def settle(n, passing, after):
        place([n], _ici_wait("forward", *passing[:4], after, "forward_wait_" + n)[0])

    pass_out = arrive("w_out", a_out)
    gn = _gmlp_fwd(proj, ln_g, ln_b, ws, bs_wide, g_g, lay)
    mixed = _mix_norm(a_out, gn, g_a, tr, pass_out[4])
    settle("w_out", pass_out, mixed)
    pass_gate = arrive("w_gate", mixed)
    wb_out = full["w_out"].reshape(-1, d)
    mix_out = _matmul(mixed, wb_out, NN, F32, "mix_out", token=pass_gate[4])
    h, hn = _post_mix(xs, mix_out, g_pm, g_pf, tr)
    settle("w_gate", pass_gate, hn)
    pass_up = arrive("w_up", hn)
    wt_gate = full["w_gate"].reshape(4 * ffp, d)
    gate = _matmul(hn, wt_gate, NT, BF16, "ffn_gate", token=pass_up[4])
    settle("w_up", pass_up, gate)
    pass_down = arrive("w_down", gate)
    wt_up = full["w_up"].reshape(4 * ffp, d)
    up = _matmul(hn, wt_up, NT, BF16, "ffn_up", token=pass_down[4])
    act = _swiglu(gate, up)
    settle("w_down", pass_down, act)
    wb_down = full["w_down"].reshape(4 * ffp, d)
    ffn = _matmul(act, wb_down, NN, F32, "ffn_down")
    dy, d_ffn, dg_po, loss_vec = _loss_head(h, ffn, target, g_po, tr)

    d_act = _matmul(d_ffn, wb_down, NT, BF16, "d_act")
    d_gate, d_up = _swiglu_bwd(gate, up, d_act)
    partial_ffn = [_matmul(d_gate, hn, TN, BF16, "gw_gate", out_chunks=True),
                   _matmul(d_up, hn, TN, BF16, "gw_up", out_chunks=True),
                   _matmul(act, d_ffn, TN, BF16, "gw_down", out_chunks=True)]
    swap_ffn = _ici_start("sibling", partial_ffn, "sibling_start_ffn")
    d_hn = _matmul(d_up, wt_up, NN, F32, "d_hn",
                   extras=[_matmul(d_gate, wt_gate, NN, F32, "d_hn_gate", token=swap_ffn[4])],
                   epilogue=lambda acc, partial: acc + partial)
    partial_ffn, from_sibling = _ici_wait("sibling", *swap_ffn[:4], d_hn, "sibling_wait_ffn")
    pair_ffn = [_pair_add(p, r, "pair_add_" + n) for n, p, r in zip(REDUCE_FFN, partial_ffn, from_sibling)]
    flight_ffn = _ici_start("scatter", pair_ffn, "scatter_start_ffn")
    dh, d_mo, dg_pf, dg_pm = _post_mix_bwd(mix_out, h, dy, d_hn, g_pm, g_pf, tr, flight_ffn[4])
    d_mixed = _matmul(d_mo, wb_out, NT, F32, "d_mixed")
    gw_out = _matmul(mixed, d_mo, TN, BF16, "gw_out", out_chunks=True)
    pair_out = pair_sums_of({"w_out": gw_out}, REDUCE_OUT, "out")
    flight_out = _ici_start("scatter", pair_out, "scatter_start_out")
    d_a, dg_a = _mix_norm_bwd(a_out, d_mixed, g_a, tr, flight_out[4])
    d_u, d_v, dg_g, d_ln_g, d_ln_b, d_ws, d_bs_wide = _gmlp_bwd(proj, d_mixed, ln_g, ln_b, ws, ws_t, bs_wide, g_g, lay)
    d_bs = _spatial_bias_grad(d_bs_wide)
    d_q, d_kv, d_kr = _attn_bwd(q, kv, kr, a_out, a_lse, d_a, cos_t, sin_t, heads)
    d_qn = _matmul(d_q, wb_uq, NT, F32, "d_qn")
    gw_uq = _matmul(qn, d_q, TN, BF16, "gw_uq")
    d_kvn = _matmul(d_kv, wb_ukv, NT, F32, "d_kvn")
    gw_ukv = _matmul(kvn, d_kv, TN, BF16, "gw_ukv")
    d_qc, d_kvc, d_krt, dg_q, dg_kv = _qkv_bwd(proj, d_qn, d_kvn, d_kr, g_q, g_kv, cos_t, sin_t, lay, tr)
    parts = [d_u, d_v, d_qc, d_kvc, d_krt]
    pad = lay["in_pad"] - (lay["kr_off"] + LANES)
    if pad:
        parts.append(jnp.zeros((t, pad), BF16))
    d_proj = jnp.concatenate(parts, axis=1)
    d_xn = _matmul(d_proj, wt_in, NN, F32, "d_xn")
    gw_in = _matmul(d_proj, xn, TN, BF16, "gw_in")
    grad_x, dg_pre = _prenorm_bwd(xs, d_xn, dh, g_pre, tr)

    partial_mix = [_w_in_grad_chunks(gw_in, ins, inp, lay), _chunks_of_cols(_w_uq_grad_unpadded(gw_uq, heads)),
                   _chunks_of_cols(gw_ukv)]
    swap_mix = _ici_start("sibling", partial_mix, "sibling_start_mix")
    pair_ffn, received_ffn = _ici_wait("scatter", *flight_ffn[:4], swap_mix[4], "scatter_wait_ffn")
    pair_out, received_out = _ici_wait("scatter", *flight_out[:4], swap_mix[4], "scatter_wait_out")
    grads, delta, new_m, new_v = {}, {}, {}, {}

    def chunk_sums(names, pair_sums, received, in_flight):
        mine, swaps, token = [], [], None
        for n, p, r in zip(names, pair_sums, received):
            mine.append(_chip_sum(p, r, "chip_sum_" + n, token))
            if in_flight:
                swaps.append(_ici_start("swap", [mine[-1]], "swap_start_" + n))
                token = swaps[-1][4]
        return mine, swaps

    def update(names, mine, swaps, tag, previous):
        if not swaps:
            theirs = _halves_exchange(mine, "grads_halves_exchange_" + tag)
        for i, n in enumerate(names):
            if swaps:
                (g_mine,), (g_theirs,) = _ici_wait("swap", *swaps[i][:4], previous, "swap_wait_" + n)
            else:
                g_mine, g_theirs = mine[i], theirs[i]
            shape = weights[n].shape
            if n in TRANSPOSED:
                view = lambda a: jnp.swapaxes(a[0], 0, 1)
                back = lambda o: jnp.swapaxes(o, 0, 1).reshape(shape)
            else:
                view = lambda a: a[0]
                back = lambda o: o.reshape(shape)
            out = _adamw_halves(view(weights[n]), g_mine, g_theirs, view(m_in[n]), view(v_in[n]), "adamw_" + n)
            grads[n], delta[n], new_m[n], new_v[n] = [back(o) for o in out]
            previous = out[-1]

    first = REDUCE_FFN + REDUCE_OUT
    mine, swaps = chunk_sums(first, pair_ffn + pair_out, received_ffn + received_out, True)
    partial_mix, from_sibling = _ici_wait("sibling", *swap_mix[:4], mine[-1], "sibling_wait_mix")
    pair_mix = [_pair_add(p, r, "pair_add_" + n) for n, p, r in zip(REDUCE_LAST, partial_mix, from_sibling)]
    flight_mix = _ici_start("scatter", pair_mix, "scatter_start_mix")
    update(first, mine, swaps, "ffn", flight_mix[4])
    pair_mix, received = _ici_wait("scatter", *flight_mix[:4], new_v[REDUCE_OUT[-1]], "scatter_wait_mix")
    mine, _ = chunk_sums(REDUCE_LAST, pair_mix, received, False)
    update(REDUCE_LAST, mine, [], "mix", None)

    small_grads = {"pre_mix_norm": dg_pre, "q_norm": dg_q, "kv_norm": dg_kv, "v_ln_gain": d_ln_g, "v_ln_bias": d_ln_b,
                   "w_spatial": d_ws, "b_spatial": d_bs, "attn_out_norm": dg_a, "gmlp_out_norm": dg_g,
                   "post_mix_norm": dg_pm, "pre_ffn_norm": dg_pf, "post_ffn_norm": dg_po}
    like = [weights[n] for n in SMALL]
    reduced = _small_all_reduce(_pack([small_grads[n] for n in SMALL] + [loss_vec]))
    loss = reduced.reshape(-1)[sum(a.size for a in like)]
    small_g = _pack(_unpack(reduced, like))
    s_delta, s_m, s_v = _adamw(_pack(like), small_g, _pack([m_in[n] for n in SMALL]),
                               _pack([v_in[n] for n in SMALL]), "adamw_small")
    for n, g in zip(SMALL, _unpack(small_g, like)):
        grads[n] = g
    delta.update(zip(SMALL, _unpack(s_delta, like)))
    new_m.update(zip(SMALL, _unpack(s_m, like)))
    new_v.update(zip(SMALL, _unpack(s_v, like)))

    return (loss, grad_x.reshape(x.shape), *[grads[n] for n in ORDER], *[delta[n] for n in ORDER],
            *[new_m[n] for n in ORDER], *[new_v[n] for n in ORDER])
```

```python
import functools
import math

import jax
import jax.numpy as jnp
from jax import lax
from jax.experimental import pallas as pl
from jax.experimental.pallas import tpu as pltpu

F32 = jnp.float32
BF16 = jnp.bfloat16
MESH = pl.DeviceIdType.MESH

NOPE_DIM = 128
ROPE_DIM = 64
ROPE_HALF = ROPE_DIM // 2
V_DIM = 128
HEAD_PAD = 256
G_HEAD_DIM = 128
CHUNK = 128
ROPE_THETA = 10000.0
EPS = 1e-6
ADAM_LR = 0.001
ADAM_B1 = 0.9
ADAM_B2 = 0.999
ADAM_EPS = 1e-08
ADAM_WD = 0.01
ADAM_STEP = 10

LANES = 128
MATMUL_TILE = 1024
WIDE_TILE = 1408
VMEM_LIMIT_BYTES = 48 * 1024 * 1024

NN = (((1,), (0,)), ((), ()))
NT = (((1,), (1,)), ((), ()))
TN = (((0,), (0,)), ((), ()))


def _params(semantics):
    return pltpu.CompilerParams(dimension_semantics=semantics, vmem_limit_bytes=VMEM_LIMIT_BYTES)


def _tile(n, cap=MATMUL_TILE):
    if n <= cap:
        return n
    if cap == MATMUL_TILE and n % WIDE_TILE == 0:
        return WIDE_TILE
    t = cap - cap % LANES
    while n % t:
        t -= LANES
    assert t > 0, n
    return t


def _round_up(n, m):
    return (n + m - 1) // m * m


def _matmul(a, b, dims, out_dtype, name, extras=(), epilogue=None, out_chunks=None, token=None):
    if dims is NN:
        (m, k), (k2, n) = a.shape, b.shape
    elif dims is NT:
        (m, k), (n, k2) = a.shape, b.shape
    else:
        (k, m), (k2, n) = a.shape, b.shape
    assert k == k2, (a.shape, b.shape, name)
    tm, tn, tk = _tile(m // 8 if out_chunks else m), _tile(n), _tile(k, 2 * MATMUL_TILE)
    if len(extras) + (len(out_dtype) if isinstance(out_dtype, tuple) else 1) > 2:
        tm = _tile(m, MATMUL_TILE // 2)
    if not extras and k % (2 * WIDE_TILE) == 0:
        tk = 2 * WIDE_TILE
    nk = k // tk

    out_dtypes = out_dtype if isinstance(out_dtype, tuple) else (out_dtype,)
    n_extra = len(extras)

    def body(*refs):
        a_ref, b_ref = refs[:2]
        extra_refs = refs[2:2 + n_extra]
        out_refs = refs[2 + n_extra + (token is not None):-1]
        acc_ref = refs[-1]
        kk = pl.program_id(2)

        @pl.when(kk == 0)
        def _():
            acc_ref[...] = jnp.zeros_like(acc_ref)

        acc_ref[...] += lax.dot_general(a_ref[...], b_ref[...], dims, preferred_element_type=F32)

        @pl.when(kk == nk - 1)
        def _():
            r = acc_ref[...]
            if epilogue is not None:
                r = epilogue(r, *[e[...] for e in extra_refs])
            for o_ref, val in zip(out_refs, r if isinstance(r, tuple) else (r,)):
                o_ref[...] = val.astype(o_ref.dtype)

    if dims is TN:
        a_spec = pl.BlockSpec((tk, tm), lambda i, j, kk: (kk, i))
    else:
        a_spec = pl.BlockSpec((tm, tk), lambda i, j, kk: (i, kk))
    if dims is NT:
        b_spec = pl.BlockSpec((tn, tk), lambda i, j, kk: (j, kk))
    else:
        b_spec = pl.BlockSpec((tk, tn), lambda i, j, kk: (kk, j))
    if not out_chunks:
        o_spec = pl.BlockSpec((tm, tn), lambda i, j, kk: (i, j))
        o_shape = (m, n)
    else:
        pi = m // 8 // tm
        o_spec = pl.BlockSpec((None, None, tm, tn), lambda i, j, kk: ((i // pi) % 2, i // (2 * pi), i % pi, j))
        o_shape = (2, 4, m // 8, n)
    assert not (extras and out_chunks)
    tokens = [] if token is None else [token]
    out = pl.pallas_call(
        body, name=name, grid=(m // tm, n // tn, nk),
        in_specs=[a_spec, b_spec] + [o_spec] * n_extra + [pl.BlockSpec(memory_space=pl.ANY)] * len(tokens),
        out_specs=[o_spec] * len(out_dtypes),
        out_shape=[jax.ShapeDtypeStruct(o_shape, dt) for dt in out_dtypes],
        scratch_shapes=[pltpu.VMEM((tm, tn), F32)],
        compiler_params=_params(("parallel", "parallel", "arbitrary")),
    )(a, b, *extras, *tokens)
    return tuple(out) if isinstance(out_dtype, tuple) else out[0]


def _row_call(body, name, rows, tr, row_ins, par_ins, row_outs, acc_outs):
    def col(i, cb):
        return (i, cb)

    def whole(i, nd):
        return (0,) * nd

    in_specs = [pl.BlockSpec((tr, w), functools.partial(col, cb=cb)) for (_, w, cb) in row_ins]
    in_specs += [pl.BlockSpec(a.shape, functools.partial(whole, nd=a.ndim)) for a in par_ins]
    out_specs = [pl.BlockSpec((tr, w), lambda i: (i, 0)) for (w, _) in row_outs]
    out_specs += [pl.BlockSpec(s, functools.partial(whole, nd=len(s))) for (s, _) in acc_outs]
    out_shape = [jax.ShapeDtypeStruct((rows, w), dt) for (w, dt) in row_outs]
    out_shape += [jax.ShapeDtypeStruct(s, dt) for (s, dt) in acc_outs]
    return pl.pallas_call(
        body, name=name, grid=(rows // tr,), in_specs=in_specs, out_specs=out_specs, out_shape=out_shape,
        compiler_params=_params(("arbitrary",) if acc_outs else ("parallel",)),
    )(*[a for (a, _, _) in row_ins], *par_ins)


def _accumulate(ref, val):
    i = pl.program_id(0)

    @pl.when(i == 0)
    def _():
        ref[...] = val

    @pl.when(i > 0)
    def _():
        ref[...] += val


def _colsum(v):
    return jnp.sum(v, axis=0, keepdims=True)


def _rms_fwd(x, g):
    r = lax.rsqrt(jnp.mean(x * x, axis=-1, keepdims=True) + EPS)
    return x * r * g


def _rms_bwd(x, g, dy):
    r = lax.rsqrt(jnp.mean(x * x, axis=-1, keepdims=True) + EPS)
    xh = x * r
    dxh = dy * g
    dx = r * (dxh - xh * jnp.mean(dxh * xh, axis=-1, keepdims=True))
    return dx, dy * xh


_GELU_C = math.sqrt(2.0 / math.pi)
_GELU_A = 0.044715


def _gelu(x):
    return 0.5 * x * (1.0 + jnp.tanh(_GELU_C * (x + _GELU_A * (x * x * x))))


def _gelu_and_grad(x):
    t = jnp.tanh(_GELU_C * (x + _GELU_A * (x * x * x)))
    return (0.5 * x * (1.0 + t),
            0.5 * (1.0 + t) + 0.5 * x * (1.0 - t * t) * (_GELU_C * (1.0 + 3.0 * _GELU_A * (x * x))))


def _sigmoid(x):
    return 1.0 / (1.0 + jnp.exp(-x))


def _rope_fwd(t, cos_t, sin_t):
    return t * cos_t + pltpu.roll(t, 2 * ROPE_HALF, 1) * sin_t


def _rope_bwd(dt, cos_t, sin_t):
    return dt * cos_t - pltpu.roll(dt, 2 * ROPE_HALF, 1) * sin_t


def _prenorm(x, g, tr, token):
    def body(x_ref, g_ref, token_ref, o_ref):
        o_ref[...] = _rms_fwd(x_ref[...], g_ref[...]).astype(BF16)

    t, d = x.shape
    return _row_call(body, "prenorm", t, tr, [(x, d, 0)], [g, token], [(d, BF16)], [])[0]


def _qkv_prep(proj, g_q, g_kv, cos_t, sin_t, lay, tr, token):
    ql, kl = lay["ql"], lay["kl"]

    def body(q_ref, kv_ref, kr_ref, cos_ref, sin_ref, gq_ref, gkv_ref, token_ref, qn_ref, kvn_ref, kro_ref):
        qn_ref[...] = _rms_fwd(q_ref[...], gq_ref[...]).astype(BF16)
        kvn_ref[...] = _rms_fwd(kv_ref[...], gkv_ref[...]).astype(BF16)
        kro_ref[...] = _rope_fwd(kr_ref[...], cos_ref[...], sin_ref[...]).astype(BF16)

    t = proj.shape[0]
    return _row_call(
        body, "qkv_prep", t, tr,
        [(proj, ql, lay["q_off"] // ql), (proj, kl, lay["kv_off"] // kl), (proj, LANES, lay["kr_off"] // LANES),
         (cos_t, LANES, 0), (sin_t, LANES, 0)],
        [g_q, g_kv, token], [(ql, BF16), (kl, BF16), (LANES, BF16)], [])


def _q_rope(q, cos_t, sin_t, heads, tr):
    def body(q_ref, cos_ref, sin_ref, o_ref):
        c, s = cos_ref[...], sin_ref[...]
        for h in range(heads):
            lo = h * HEAD_PAD
            o_ref[:, lo:lo + NOPE_DIM] = q_ref[:, lo:lo + NOPE_DIM].astype(BF16)
            o_ref[:, lo + NOPE_DIM:lo + HEAD_PAD] = _rope_fwd(q_ref[:, lo + NOPE_DIM:lo + HEAD_PAD], c, s).astype(BF16)

    t, w = q.shape
    return _row_call(body, "q_rope", t, tr, [(q, w, 0), (cos_t, LANES, 0), (sin_t, LANES, 0)], [], [(w, BF16)], [])[0]


ATTN_SCALE = 1.0 / math.sqrt(NOPE_DIM + ROPE_DIM)
ATTN_EXP2_SCALE = ATTN_SCALE * math.log2(math.e)


def _attn_tile(t, cap):
    tq = cap
    while t % tq:
        tq //= 2
    return tq


def _attn_fwd(q, kv, kr, heads):
    t = q.shape[0]
    tq = _attn_tile(t, 256)
    kc = _attn_tile(t, 2048)

    def body(q_ref, kv_ref, kr_ref, o_ref, lse_ref, kcat):
        @pl.when(pl.program_id(1) == 0)
        def _():
            kcat[:, :NOPE_DIM] = kv_ref[:, :NOPE_DIM]
            kcat[:, NOPE_DIM:] = kr_ref[...]

        qb = q_ref[...]
        m = l = acc = None
        for j in range(t // kc):
            keys = slice(j * kc, (j + 1) * kc)
            s = lax.dot_general(qb, kcat[keys, :], NT, preferred_element_type=F32)
            m_chunk = jnp.max(s, axis=-1, keepdims=True)
            m_new = m_chunk if m is None else jnp.maximum(m, m_chunk)
            p = jnp.exp2((s - m_new) * ATTN_EXP2_SCALE)
            pv = jnp.dot(p.astype(BF16), kv_ref[keys, NOPE_DIM:], preferred_element_type=F32)
            if m is None:
                l, acc = jnp.sum(p, axis=-1, keepdims=True), pv
            else:
                shrink = jnp.exp2((m - m_new) * ATTN_EXP2_SCALE)
                l = l * shrink + jnp.sum(p, axis=-1, keepdims=True)
                acc = acc * shrink + pv
            m = m_new
        o_ref[...] = acc * (1.0 / l)
        lse_ref[...] = jnp.broadcast_to(m * ATTN_EXP2_SCALE + jnp.log(l) * math.log2(math.e), (tq, V_DIM))

    out_spec = pl.BlockSpec((tq, V_DIM), lambda h, i: (i, h))
    out_shape = jax.ShapeDtypeStruct((t, heads * V_DIM), F32)
    return pl.pallas_call(
        body, name="attn_fwd", grid=(heads, t // tq),
        in_specs=[pl.BlockSpec((tq, HEAD_PAD), lambda h, i: (i, h)),
                  pl.BlockSpec((t, HEAD_PAD), lambda h, i: (0, h)),
                  pl.BlockSpec((t, LANES), lambda h, i: (0, 0))],
        out_specs=[out_spec, out_spec], out_shape=[out_shape, out_shape],
        scratch_shapes=[pltpu.VMEM((t, HEAD_PAD), BF16)],
        compiler_params=_params(("arbitrary", "arbitrary")),
    )(q, kv, kr)


def _attn_bwd(q, kv, kr, out, lse, d_out, cos_t, sin_t, heads):
    t = q.shape[0]
    tq = _attn_tile(t, 256)
    nq = t // tq

    def body(q_ref, kv_ref, kr_ref, o_ref, lse_ref, do_ref, cos_ref, sin_ref, dq_ref, dkv_ref, dkr_ref,
             kcat, dk_acc, dv_acc):
        h, i = pl.program_id(0), pl.program_id(1)

        @pl.when(i == 0)
        def _():
            kcat[:, :NOPE_DIM] = kv_ref[:, :NOPE_DIM]
            kcat[:, NOPE_DIM:] = kr_ref[...]
            dk_acc[...] = jnp.zeros_like(dk_acc)
            dv_acc[...] = jnp.zeros_like(dv_acc)

        @pl.when((h == 0) & (i == 0))
        def _():
            dkr_ref[...] = jnp.zeros_like(dkr_ref)

        qb, dob = q_ref[...], do_ref[...]
        row_term = jnp.sum(dob.astype(F32) * o_ref[...], axis=-1, keepdims=True)
        s = lax.dot_general(qb, kcat[...], NT, preferred_element_type=F32)
        dp = lax.dot_general(dob, kv_ref[:, NOPE_DIM:], NT, preferred_element_type=F32)
        p = jnp.exp2(s * ATTN_EXP2_SCALE - lse_ref[:, :1])
        ds = (p * (dp - row_term)).astype(BF16)
        dv_acc[...] += lax.dot_general(p.astype(BF16), dob, TN, preferred_element_type=F32)
        dq = jnp.dot(ds, kcat[...], preferred_element_type=F32) * ATTN_SCALE
        dq_ref[:, :NOPE_DIM] = dq[:, :NOPE_DIM].astype(BF16)
        dq_ref[:, NOPE_DIM:] = _rope_bwd(dq[:, NOPE_DIM:], cos_ref[...], sin_ref[...]).astype(BF16)
        dk_acc[...] += lax.dot_general(ds, qb, TN, preferred_element_type=F32)

        @pl.when(i == nq - 1)
        def _():
            dkv_ref[:, :NOPE_DIM] = (dk_acc[:, :NOPE_DIM] * ATTN_SCALE).astype(BF16)
            dkv_ref[:, NOPE_DIM:] = dv_acc[...].astype(BF16)
            dkr_ref[...] += dk_acc[:, NOPE_DIM:] * ATTN_SCALE

    return pl.pallas_call(
        body, name="attn_bwd", grid=(heads, nq),
        in_specs=[pl.BlockSpec((tq, HEAD_PAD), lambda h, i: (i, h)),
                  pl.BlockSpec((t, HEAD_PAD), lambda h, i: (0, h)),
                  pl.BlockSpec((t, LANES), lambda h, i: (0, 0)),
                  pl.BlockSpec((tq, V_DIM), lambda h, i: (i, h)),
                  pl.BlockSpec((tq, V_DIM), lambda h, i: (i, h)),
                  pl.BlockSpec((tq, V_DIM), lambda h, i: (i, h)),
                  pl.BlockSpec((tq, LANES), lambda h, i: (i, 0)),
                  pl.BlockSpec((tq, LANES), lambda h, i: (i, 0))],
        out_specs=[pl.BlockSpec((tq, HEAD_PAD), lambda h, i: (i, h)),
                   pl.BlockSpec((t, HEAD_PAD), lambda h, i: (0, h)),
                   pl.BlockSpec((t, LANES), lambda h, i: (0, 0))],
        out_shape=[jax.ShapeDtypeStruct((t, heads * HEAD_PAD), BF16),
                   jax.ShapeDtypeStruct((t, heads * HEAD_PAD), BF16),
                   jax.ShapeDtypeStruct((t, LANES), F32)],
        scratch_shapes=[pltpu.VMEM((t, HEAD_PAD), BF16), pltpu.VMEM((t, HEAD_PAD), F32), pltpu.VMEM((t, V_DIM), F32)],
        compiler_params=_params(("arbitrary", "arbitrary")),
    )(q, kv, kr, out, lse, d_out, cos_t, sin_t)


def _layer_norm_parts(x):
    mu = jnp.mean(x, axis=-1, keepdims=True)
    xc = x - mu
    r = lax.rsqrt(jnp.mean(xc * xc, axis=-1, keepdims=True) + EPS)
    return xc * r, r


def _gmlp_fwd(proj, ln_g, ln_b, w_s, b_sb, g_out_norm, lay):
    gw = lay["gw"]
    g_heads = gw // G_HEAD_DIM

    def body(u_ref, v_ref, lng_ref, lnb_ref, ws_ref, bs_ref, gn_ref, o_ref, gate_ref):
        gu = _gelu(u_ref[...])
        vh, _ = _layer_norm_parts(_gelu(v_ref[...]))
        vln = (vh * lng_ref[...] + lnb_ref[...]).astype(BF16)
        for g in range(g_heads):
            cols = slice(g * G_HEAD_DIM, (g + 1) * G_HEAD_DIM)
            s = jnp.dot(ws_ref[g], vln[:, cols], preferred_element_type=F32) + bs_ref[g]
            gate_ref[:, cols] = gu[:, cols] * s
        o_ref[...] = _rms_fwd(gate_ref[...], gn_ref[...]).astype(BF16)

    t = proj.shape[0]
    in_specs = [pl.BlockSpec((CHUNK, gw), lambda i: (i, 0)), pl.BlockSpec((CHUNK, gw), lambda i: (i, 1))]
    pars = [ln_g, ln_b, w_s, b_sb, g_out_norm]
    in_specs += [pl.BlockSpec(a.shape, functools.partial(lambda i, nd: (0,) * nd, nd=a.ndim)) for a in pars]
    return pl.pallas_call(
        body, name="gmlp_fwd", grid=(t // CHUNK,), in_specs=in_specs,
        out_specs=pl.BlockSpec((CHUNK, gw), lambda i: (i, 0)),
        out_shape=jax.ShapeDtypeStruct((t, gw), BF16),
        scratch_shapes=[pltpu.VMEM((CHUNK, gw), F32)],
        compiler_params=_params(("parallel",)),
    )(proj, proj, *pars)


def _gmlp_bwd(proj, d_mixed, ln_g, ln_b, w_s, w_st, b_sb, g_out_norm, lay):
    gw = lay["gw"]
    g_heads = gw // G_HEAD_DIM
    aw_blocks = lay["aw"] // gw

    def body(u_ref, v_ref, dm_ref, lng_ref, lnb_ref, ws_ref, wst_ref, bs_ref, gn_ref,
             du_ref, dv_ref, dgn_ref, dlng_ref, dlnb_ref, dws_ref, dbs_ref, gate_ref, s_ref, dvln_ref):
        i = pl.program_id(0)
        u, v = u_ref[...], v_ref[...]
        (gu, gelu_du), (gv, gelu_dv) = _gelu_and_grad(u), _gelu_and_grad(v)
        vh, r_ln = _layer_norm_parts(gv)
        vln = (vh * lng_ref[...] + lnb_ref[...]).astype(BF16)
        for g in range(g_heads):
            cols = slice(g * G_HEAD_DIM, (g + 1) * G_HEAD_DIM)
            s = jnp.dot(ws_ref[g], vln[:, cols], preferred_element_type=F32) + bs_ref[g]
            s_ref[:, cols] = s
            gate_ref[:, cols] = gu[:, cols] * s
        d_gate, dgn = _rms_bwd(gate_ref[...], gn_ref[...], dm_ref[...])
        _accumulate(dgn_ref, _colsum(dgn))
        du_ref[...] = (d_gate * s_ref[...] * gelu_du).astype(BF16)
        d_s = d_gate * gu
        d_sb = d_s.astype(BF16)
        for g in range(g_heads):
            cols = slice(g * G_HEAD_DIM, (g + 1) * G_HEAD_DIM)
            dw = lax.dot_general(d_sb[:, cols], vln[:, cols], NT, preferred_element_type=F32)

            @pl.when(i == 0)
            def _():
                dws_ref[g] = dw
                dbs_ref[g] = d_s[:, cols]

            @pl.when(i > 0)
            def _():
                dws_ref[g] += dw
                dbs_ref[g] += d_s[:, cols]

            dvln_ref[:, cols] = jnp.dot(wst_ref[g], d_sb[:, cols], preferred_element_type=F32)
        d_vln = dvln_ref[...]
        _accumulate(dlng_ref, _colsum(d_vln * vh))
        _accumulate(dlnb_ref, _colsum(d_vln))
        d_vh = d_vln * lng_ref[...]
        d_gv = r_ln * (d_vh - jnp.mean(d_vh, axis=-1, keepdims=True)
                       - vh * jnp.mean(d_vh * vh, axis=-1, keepdims=True))
        dv_ref[...] = (d_gv * gelu_dv).astype(BF16)

    t = proj.shape[0]
    whole = lambda a: pl.BlockSpec(a.shape, functools.partial(lambda i, nd: (0,) * nd, nd=a.ndim))
    pars = [ln_g, ln_b, w_s, w_st, b_sb, g_out_norm]
    hshape = (g_heads, CHUNK, CHUNK)
    return pl.pallas_call(
        body, name="gmlp_bwd", grid=(t // CHUNK,),
        in_specs=[pl.BlockSpec((CHUNK, gw), lambda i: (i, 0)), pl.BlockSpec((CHUNK, gw), lambda i: (i, 1)),
                  pl.BlockSpec((CHUNK, gw), lambda i: (i, aw_blocks))] + [whole(a) for a in pars],
        out_specs=[pl.BlockSpec((CHUNK, gw), lambda i: (i, 0)), pl.BlockSpec((CHUNK, gw), lambda i: (i, 0)),
                   pl.BlockSpec((1, gw), lambda i: (0, 0)), pl.BlockSpec((1, gw), lambda i: (0, 0)),
                   pl.BlockSpec((1, gw), lambda i: (0, 0)),
                   pl.BlockSpec(hshape, lambda i: (0, 0, 0)), pl.BlockSpec(hshape, lambda i: (0, 0, 0))],
        out_shape=[jax.ShapeDtypeStruct((t, gw), BF16), jax.ShapeDtypeStruct((t, gw), BF16),
                   jax.ShapeDtypeStruct((1, gw), F32), jax.ShapeDtypeStruct((1, gw), F32),
                   jax.ShapeDtypeStruct((1, gw), F32),
                   jax.ShapeDtypeStruct(hshape, F32), jax.ShapeDtypeStruct(hshape, F32)],
        scratch_shapes=[pltpu.VMEM((CHUNK, gw), F32), pltpu.VMEM((CHUNK, gw), F32), pltpu.VMEM((CHUNK, gw), F32)],
        compiler_params=_params(("arbitrary",)),
    )(proj, proj, d_mixed, *pars)


def _spatial_bias_grad(dbs_wide):
    g_heads = dbs_wide.shape[0]

    def body(x_ref, o_ref):
        for g in range(g_heads):
            o_ref[g:g + 1, :] = jnp.sum(x_ref[g].T, axis=0, keepdims=True)

    return pl.pallas_call(
        body, name="spatial_bias_grad", out_shape=jax.ShapeDtypeStruct((g_heads, CHUNK), F32),
        in_specs=[pl.BlockSpec(memory_space=pltpu.VMEM)], out_specs=pl.BlockSpec(memory_space=pltpu.VMEM),
    )(dbs_wide)


def _mix_norm(a_out, gn, g_a, tr, token):
    aw = a_out.shape[1]
    gw = gn.shape[1]

    def body(a_ref, gn_ref, g_ref, token_ref, o_ref):
        o_ref[:, :aw] = _rms_fwd(a_ref[...], g_ref[...]).astype(BF16)
        o_ref[:, aw:] = gn_ref[...]

    t = a_out.shape[0]
    return _row_call(body, "mix_norm", t, tr, [(a_out, aw, 0), (gn, gw, 0)], [g_a, token], [(aw + gw, BF16)], [])[0]


def _mix_norm_bwd(a_out, d_mixed, g_a, tr, token):
    aw = a_out.shape[1]

    def body(a_ref, dm_ref, g_ref, token_ref, da_ref, dg_ref):
        dx, dg = _rms_bwd(a_ref[...], g_ref[...], dm_ref[...])
        da_ref[...] = dx.astype(BF16)
        _accumulate(dg_ref, _colsum(dg))

    t = a_out.shape[0]
    return _row_call(body, "mix_norm_bwd", t, tr, [(a_out, aw, 0), (d_mixed, aw, 0)], [g_a, token],
                     [(aw, BF16)], [((1, aw), F32)])


def _post_mix(x, mix_out, g_pm, g_pf, tr):
    def body(x_ref, mo_ref, gpm_ref, gpf_ref, h_ref, hn_ref):
        h = x_ref[...] + _rms_fwd(mo_ref[...], gpm_ref[...])
        h_ref[...] = h
        hn_ref[...] = _rms_fwd(h, gpf_ref[...]).astype(BF16)

    t, d = x.shape
    return _row_call(body, "post_mix", t, tr, [(x, d, 0), (mix_out, d, 0)], [g_pm, g_pf], [(d, F32), (d, BF16)], [])


def _post_mix_bwd(mix_out, h, dy, d_hn, g_pm, g_pf, tr, token):
    def body(mo_ref, h_ref, dy_ref, dhn_ref, gpm_ref, gpf_ref, token_ref, dh_ref, dmo_ref, dgpf_ref, dgpm_ref):
        dx, dg = _rms_bwd(h_ref[...], gpf_ref[...], dhn_ref[...])
        dh = dy_ref[...] + dx
        dh_ref[...] = dh
        _accumulate(dgpf_ref, _colsum(dg))
        dmo, dg2 = _rms_bwd(mo_ref[...], gpm_ref[...], dh)
        dmo_ref[...] = dmo.astype(BF16)
        _accumulate(dgpm_ref, _colsum(dg2))

    t, d = h.shape
    return _row_call(body, "post_mix_bwd", t, tr, [(mix_out, d, 0), (h, d, 0), (dy, d, 0), (d_hn, d, 0)],
                     [g_pm, g_pf, token], [(d, F32), (d, BF16)], [((1, d), F32), ((1, d), F32)])


def _swiglu(gate, up):
    t, f = gate.shape
    tr, tf = _tile(t, 512), _tile(f, 2048)

    def body(g_ref, u_ref, o_ref):
        g = g_ref[...].astype(F32)
        o_ref[...] = (g * _sigmoid(g) * u_ref[...].astype(F32)).astype(BF16)

    spec = pl.BlockSpec((tr, tf), lambda i, j: (i, j))
    return pl.pallas_call(body, name="swiglu", grid=(t // tr, f // tf), in_specs=[spec, spec], out_specs=spec,
                          out_shape=jax.ShapeDtypeStruct((t, f), BF16),
                          compiler_params=_params(("parallel", "parallel")))(gate, up)


def _swiglu_bwd(gate, up, d_act):
    t, f = gate.shape
    tr, tf = _tile(t, 512), _tile(f, 2048)

    def body(g_ref, u_ref, da_ref, dg_ref, du_ref):
        g, u, da = g_ref[...].astype(F32), u_ref[...].astype(F32), da_ref[...].astype(F32)
        sg = _sigmoid(g)
        du_ref[...] = (da * (g * sg)).astype(BF16)
        dg_ref[...] = (da * u * (sg * (1.0 + g * (1.0 - sg)))).astype(BF16)

    spec = pl.BlockSpec((tr, tf), lambda i, j: (i, j))
    shape = jax.ShapeDtypeStruct((t, f), BF16)
    return pl.pallas_call(body, name="swiglu_bwd", grid=(t // tr, f // tf), in_specs=[spec, spec, spec],
                          out_specs=[spec, spec], out_shape=[shape, shape],
                          compiler_params=_params(("parallel", "parallel")))(gate, up, d_act)


def _loss_head(h, ffn, target, g_po, tr):
    t, d = h.shape

    def body(h_ref, f_ref, t_ref, g_ref, dy_ref, df_ref, dg_ref, loss_ref):
        f = f_ref[...]
        err = h_ref[...] + _rms_fwd(f, g_ref[...]) - t_ref[...]
        dy = err * (1.0 / d)
        dy_ref[...] = dy
        df, dg = _rms_bwd(f, g_ref[...], dy)
        df_ref[...] = df.astype(BF16)
        _accumulate(dg_ref, _colsum(dg))
        sq = jnp.sum(_colsum(err * err), axis=-1, keepdims=True) * (0.5 / d)
        _accumulate(loss_ref, jnp.broadcast_to(sq, (1, LANES)))

    return _row_call(body, "loss_head", t, tr, [(h, d, 0), (ffn, d, 0), (target, d, 0)], [g_po],
                     [(d, F32), (d, BF16)], [((1, d), F32), ((1, LANES), F32)])


def _qkv_bwd(proj, d_qn, d_kvn, d_kr, g_q, g_kv, cos_t, sin_t, lay, tr):
    ql, kl = lay["ql"], lay["kl"]

    def body(q_ref, kv_ref, dqn_ref, dkvn_ref, dkr_ref, cos_ref, sin_ref, gq_ref, gkv_ref,
             dq_ref, dkv_ref, dkt_ref, dgq_ref, dgkv_ref):
        dx, dg = _rms_bwd(q_ref[...], gq_ref[...], dqn_ref[...])
        dq_ref[...] = dx.astype(BF16)
        _accumulate(dgq_ref, _colsum(dg))
        dx, dg = _rms_bwd(kv_ref[...], gkv_ref[...], dkvn_ref[...])
        dkv_ref[...] = dx.astype(BF16)
        _accumulate(dgkv_ref, _colsum(dg))
        dkt_ref[...] = _rope_bwd(dkr_ref[...], cos_ref[...], sin_ref[...]).astype(BF16)

    t = proj.shape[0]
    return _row_call(
        body, "qkv_bwd", t, tr,
        [(proj, ql, lay["q_off"] // ql), (proj, kl, lay["kv_off"] // kl), (d_qn, ql, 0), (d_kvn, kl, 0),
         (d_kr, LANES, 0), (cos_t, LANES, 0), (sin_t, LANES, 0)],
        [g_q, g_kv], [(ql, BF16), (kl, BF16), (LANES, BF16)], [((1, ql), F32), ((1, kl), F32)])


def _prenorm_bwd(x, d_xn, dh, g, tr):
    def body(x_ref, dxn_ref, dh_ref, g_ref, gx_ref, dg_ref):
        dx, dg = _rms_bwd(x_ref[...], g_ref[...], dxn_ref[...])
        gx_ref[...] = dh_ref[...] + dx
        _accumulate(dg_ref, _colsum(dg))

    t, d = x.shape
    return _row_call(body, "prenorm_bwd", t, tr, [(x, d, 0), (d_xn, d, 0), (dh, d, 0)], [g],
                     [(d, F32)], [((1, d), F32)])


def _adam_rows(rows, cols, block_elements=256 * 1024):
    cap = max(8, block_elements // cols // 8 * 8)
    tr = min(rows, cap)
    while rows % tr:
        tr -= 8
    return tr


def _adamw(w, g, m, v, name):
    rows, cols = w.shape
    tr = _adam_rows(rows, cols)

    def body(w_ref, g_ref, m_ref, v_ref, d_ref, mo_ref, vo_ref):
        g = g_ref[...]
        m2 = ADAM_B1 * m_ref[...] + (1.0 - ADAM_B1) * g
        v2 = ADAM_B2 * v_ref[...] + (1.0 - ADAM_B2) * (g * g)
        m_hat = m2 / (1.0 - ADAM_B1 ** ADAM_STEP)
        v_hat = v2 / (1.0 - ADAM_B2 ** ADAM_STEP)
        d_ref[...] = -ADAM_LR * (m_hat / (jnp.sqrt(v_hat) + ADAM_EPS) + ADAM_WD * w_ref[...])
        mo_ref[...] = m2
        vo_ref[...] = v2

    spec = pl.BlockSpec((tr, cols), lambda i: (i, 0))
    shape = jax.ShapeDtypeStruct((rows, cols), F32)
    return pl.pallas_call(body, name=name, grid=(rows // tr,), in_specs=[spec] * 4, out_specs=[spec] * 3,
                          out_shape=[shape] * 3, compiler_params=_params(("parallel",)))(w, g, m, v)


def _adamw_halves(w, g_mine, g_theirs, m, v, name):
    rows, cols = w.shape
    rh = g_mine.shape[0]
    tr = _adam_rows(math.gcd(rows, rh), cols)
    per_half = rh // tr
    my_c = jnp.reshape(lax.axis_index("c"), (1,)).astype(jnp.int32)

    def body(c_ref, w_ref, gm_ref, gt_ref, m_ref, v_ref, g_ref, d_ref, mo_ref, vo_ref):
        mine = (pl.program_id(0) // per_half) == c_ref[0]
        g = jnp.where(mine, gm_ref[...], gt_ref[...])
        m2 = ADAM_B1 * m_ref[...] + (1.0 - ADAM_B1) * g
        v2 = ADAM_B2 * v_ref[...] + (1.0 - ADAM_B2) * (g * g)
        m_hat = m2 / (1.0 - ADAM_B1 ** ADAM_STEP)
        v_hat = v2 / (1.0 - ADAM_B2 ** ADAM_STEP)
        g_ref[...] = g
        d_ref[...] = -ADAM_LR * (m_hat / (jnp.sqrt(v_hat) + ADAM_EPS) + ADAM_WD * w_ref[...])
        mo_ref[...] = m2
        vo_ref[...] = v2

    def half_spec(is_mine):
        def index(i, c_ref):
            used = ((i // per_half) == c_ref[0]) if is_mine else ((i // per_half) != c_ref[0])
            return (jnp.where(used, i % per_half, 0), 0)
        return pl.BlockSpec((tr, cols), index)

    spec = pl.BlockSpec((tr, cols), lambda i, c_ref: (i, 0))
    shape = jax.ShapeDtypeStruct((rows, cols), F32)
    grid_spec = pltpu.PrefetchScalarGridSpec(
        num_scalar_prefetch=1, grid=(rows // tr,),
        in_specs=[spec, half_spec(True), half_spec(False), spec, spec], out_specs=[spec] * 4)
    return pl.pallas_call(body, name=name, grid_spec=grid_spec, out_shape=[shape] * 4,
                          compiler_params=_params(("parallel",)))(my_c, w, g_mine, g_theirs, m, v)


def _pair_add(parts, theirs, name):
    _, n, r, c = parts.shape
    tr = _adam_rows(r, c, 1024 * 1024)
    my_c = jnp.reshape(lax.axis_index("c"), (1,)).astype(jnp.int32)

    def body(c_ref, a_ref, b_ref, o_ref):
        o_ref[0] = (a_ref[0, 0].astype(F32) + b_ref[0].astype(F32)).astype(BF16)

    spec = pl.BlockSpec((1, tr, c), lambda k, i, c_ref: (k, i, 0))
    grid_spec = pltpu.PrefetchScalarGridSpec(
        num_scalar_prefetch=1, grid=(n, r // tr),
        in_specs=[pl.BlockSpec((1, 1, tr, c), lambda k, i, c_ref: (c_ref[0], k, i, 0)), spec], out_specs=spec)
    return pl.pallas_call(body, name=name, grid_spec=grid_spec, out_shape=jax.ShapeDtypeStruct((n, r, c), BF16),
                          compiler_params=_params(("parallel", "parallel")))(my_c, parts, theirs)


def _chip_sum(pair_sums, received, name, token=None):
    _, r, c = pair_sums.shape
    tr = _adam_rows(r, c)
    own = 2 * lax.axis_index("x") + lax.axis_index("y")

    def body(own_ref, p_ref, r0_ref, r1_ref, r2_ref, *rest):
        o_ref = rest[-1]
        acc = p_ref[0].astype(F32) + r0_ref[0].astype(F32)
        acc = acc + r1_ref[0].astype(F32)
        o_ref[...] = acc + r2_ref[0].astype(F32)

    def rspec(j):
        return pl.BlockSpec((1, tr, c), functools.partial(lambda i, own_ref, j: (j, i, 0), j=j))

    extra = [] if token is None else [token]
    grid_spec = pltpu.PrefetchScalarGridSpec(
        num_scalar_prefetch=1, grid=(r // tr,),
        in_specs=[pl.BlockSpec((1, tr, c), lambda i, own_ref: (own_ref[0], i, 0)), rspec(0), rspec(1), rspec(2)]
        + [pl.BlockSpec(memory_space=pl.ANY)] * len(extra),
        out_specs=pl.BlockSpec((tr, c), lambda i, own_ref: (i, 0)))
    return pl.pallas_call(body, name=name, grid_spec=grid_spec, out_shape=jax.ShapeDtypeStruct((r, c), F32),
                          compiler_params=_params(("parallel",)))(
        jnp.reshape(own, (1,)).astype(jnp.int32), pair_sums, received, received, received, *extra)


def _mesh_place():
    x, y, c = lax.axis_index("x"), lax.axis_index("y"), lax.axis_index("c")
    other_chips = [(1 - x, y), (x, 1 - y), (1 - x, 1 - y)]
    return x, y, c, other_chips


def _hbm_specs(n):
    return [pl.BlockSpec(memory_space=pltpu.HBM)] * n


def _sibling_exchange(parts, name):
    n = len(parts)

    def body(*refs):
        ins, outs = refs[:n], refs[n:2 * n]
        send_sems, recv_sems = refs[2 * n:]
        x, y, c, _ = _mesh_place()
        copies = [pltpu.make_async_remote_copy(src_ref=ins[w].at[1 - c], dst_ref=outs[w], send_sem=send_sems.at[w],
                                               recv_sem=recv_sems.at[w], device_id=(x, y, 1 - c), device_id_type=MESH)
                  for w in range(n)]
        for cp in copies:
            cp.start()
        for cp in copies:
            cp.wait()

    return pl.pallas_call(
        body, name=name,
        out_shape=[jax.ShapeDtypeStruct(p.shape[1:], p.dtype) for p in parts],
        in_specs=_hbm_specs(n), out_specs=_hbm_specs(n),
        scratch_shapes=[pltpu.SemaphoreType.DMA((n,)), pltpu.SemaphoreType.DMA((n,))],
    )(*parts)


SEM_SPEC = pl.BlockSpec(memory_space=pltpu.SEMAPHORE)
DATAFLOW_EFFECT = pltpu.SideEffectType.DATAFLOW_SIDE_EFFECTING


def _copies_per_weight(kind):
    return {"gather": 4, "scatter": 3, "sibling": 1, "forward": 3, "swap": 1}[kind]


def _flight_copies(kind, src_refs, land_refs, send_sems, recv_sems, arriving):
    x, y, c, other_chips = _mesh_place()
    own = 2 * x + y
    sibling = (x, y, 1 - c)
    per = _copies_per_weight(kind)
    copies = []
    for w in range(len(src_refs)):
        def remote(src, dst, j, to):
            return pltpu.make_async_remote_copy(src_ref=src, dst_ref=dst, send_sem=send_sems.at[per * w + j],
                                                recv_sem=recv_sems.at[per * w + j], device_id=to, device_id_type=MESH)

        if kind == "sibling":
            copies.append(remote(src_refs[w].at[1 - c], land_refs[w], 0, sibling))
            continue
        if kind == "swap":
            copies.append(remote(src_refs[w], land_refs[w], 0, sibling))
            continue
        for j, chip in enumerate(other_chips):
            theirs = 2 * chip[0] + chip[1]
            if kind == "gather":
                copies.append(remote(src_refs[w].at[c], land_refs[w].at[theirs if arriving else own, c], j, (*chip, c)))
            elif kind == "forward":
                copies.append(remote(src_refs[w].at[theirs, c], src_refs[w].at[theirs, (1 - c) if arriving else c],
                                     j, sibling))
            else:
                copies.append(remote(src_refs[w].at[theirs], land_refs[w].at[j], j, (*chip, c)))
        if kind == "gather":
            copies.append(remote(src_refs[w], land_refs[w].at[own], 3, sibling))
    return copies


def _ici_start(kind, srcs, name, after=None):
    n = len(srcs)
    if kind == "gather":
        lands = [lax.empty((4,) + s.shape, s.dtype) for s in srcs]
    elif kind == "scatter":
        lands = [lax.empty((3,) + s.shape[1:], s.dtype) for s in srcs]
    elif kind == "sibling":
        lands = [lax.empty(s.shape[1:], s.dtype) for s in srcs]
    elif kind == "swap":
        lands = [lax.empty(s.shape, s.dtype) for s in srcs]
    else:
        lands = []
    nb = n + len(lands)
    afters = [] if after is None else [after]

    def body(*refs):
        src_refs, land_refs = refs[:n], refs[n:nb]
        send_sems, recv_sems = refs[nb + len(afters)], refs[nb + len(afters) + 1]
        token = refs[-1]
        for cp in _flight_copies(kind, src_refs, land_refs, send_sems, recv_sems, False):
            cp.start()
        token[...] = jnp.zeros_like(token)

    hbm = lambda a: pltpu.with_memory_space_constraint(a, pltpu.HBM)
    n_sems = _copies_per_weight(kind) * n
    out = pl.pallas_call(
        body, name=name,
        out_shape=(pltpu.SemaphoreType.DMA((n_sems,)), pltpu.SemaphoreType.DMA((n_sems,)),
                   *[pltpu.HBM(a.shape, a.dtype) for a in srcs + lands], jax.ShapeDtypeStruct((8, LANES), F32)),
        in_specs=_hbm_specs(nb) + [pl.BlockSpec(memory_space=pl.ANY)] * len(afters),
        out_specs=(SEM_SPEC, SEM_SPEC, *_hbm_specs(nb), pl.BlockSpec(memory_space=pltpu.VMEM)),
        input_output_aliases={i: 2 + i for i in range(nb)},
        compiler_params=pltpu.CompilerParams(has_side_effects=DATAFLOW_EFFECT),
    )(*[hbm(a) for a in srcs + lands], *afters)
    return out[0], out[1], list(out[2:2 + n]), list(out[2 + n:2 + nb]), out[-1]


def _ici_wait(kind, send_sems, recv_sems, srcs, lands, after, name):
    n = len(srcs)
    nb = n + len(lands)

    def body(*refs):
        src_refs, land_refs = refs[:n], refs[n:nb]
        send_ref, recv_ref = refs[nb], refs[nb + 1]
        for cp in _flight_copies(kind, src_refs, land_refs, send_ref, recv_ref, True):
            cp.wait_send()
            cp.wait_recv()

    out = pl.pallas_call(
        body, name=name, out_shape=tuple(pltpu.HBM(a.shape, a.dtype) for a in srcs + lands),
        in_specs=_hbm_specs(nb) + [SEM_SPEC, SEM_SPEC, pl.BlockSpec(memory_space=pl.ANY)],
        out_specs=tuple(_hbm_specs(nb)), input_output_aliases={i: i for i in range(nb)},
        compiler_params=pltpu.CompilerParams(has_side_effects=DATAFLOW_EFFECT),
    )(*srcs, *lands, send_sems, recv_sems, after)
    return list(out[:n]), list(out[n:])


def _halves_exchange(halves, name):
    n = len(halves)

    def body(*refs):
        ins, outs = refs[:n], refs[n:2 * n]
        send_sems, recv_sems = refs[2 * n:]
        x, y, c, _ = _mesh_place()
        copies = [pltpu.make_async_remote_copy(src_ref=ins[w], dst_ref=outs[w], send_sem=send_sems.at[w],
                                               recv_sem=recv_sems.at[w], device_id=(x, y, 1 - c), device_id_type=MESH)
                  for w in range(n)]
        for cp in copies:
            cp.start()
        for cp in copies:
            cp.wait()

    return pl.pallas_call(
        body, name=name,
        out_shape=[jax.ShapeDtypeStruct(h.shape, h.dtype) for h in halves],
        in_specs=_hbm_specs(n), out_specs=_hbm_specs(n),
        scratch_shapes=[pltpu.SemaphoreType.DMA((n,)), pltpu.SemaphoreType.DMA((n,))],
    )(*halves)


def _small_all_reduce(packed):
    rows = packed.shape[0]

    def body(in_ref, out_ref, from_sibling, pair_sums, send_sems, recv_sems):
        x, y, c, other_chips = _mesh_place()
        own = 2 * x + y

        def remote(src, dst, k, to):
            return pltpu.make_async_remote_copy(src_ref=src, dst_ref=dst, send_sem=send_sems.at[k],
                                                recv_sem=recv_sems.at[k], device_id=to, device_id_type=MESH)

        swap = remote(in_ref, from_sibling, 0, (x, y, 1 - c))
        swap.start()
        swap.wait()
        mine, theirs = in_ref[...], from_sibling[...]
        pair_sums[own] = jnp.where(c == 0, mine, theirs) + jnp.where(c == 0, theirs, mine)
        copies = [remote(pair_sums.at[own], pair_sums.at[own], 1 + j, (*chip, c)) for j, chip in enumerate(other_chips)]
        for cp in copies:
            cp.start()
        for j, chip in enumerate(other_chips):
            landing = pair_sums.at[2 * chip[0] + chip[1]]
            remote(landing, landing, 1 + j, (*chip, c)).wait_recv()
        for cp in copies:
            cp.wait_send()
        out_ref[...] = ((pair_sums[0] + pair_sums[1]) + pair_sums[2]) + pair_sums[3]

    return pl.pallas_call(
        body, name="small_all_reduce", out_shape=jax.ShapeDtypeStruct(packed.shape, F32),
        in_specs=[pl.BlockSpec(memory_space=pltpu.VMEM)], out_specs=pl.BlockSpec(memory_space=pltpu.VMEM),
        scratch_shapes=[pltpu.VMEM((rows, LANES), F32), pltpu.VMEM((4, rows, LANES), F32),
                        pltpu.SemaphoreType.DMA((4,)), pltpu.SemaphoreType.DMA((4,))],
        compiler_params=pltpu.CompilerParams(vmem_limit_bytes=VMEM_LIMIT_BYTES),
    )(packed)


def _layout(w_in, w_uq, w_ukv, v_ln_gain, q_norm, kv_norm):
    heads = 4 * w_uq.shape[-1] // (NOPE_DIM + ROPE_DIM)
    gw = v_ln_gain.shape[-1]
    ql, kl = q_norm.shape[-1], kv_norm.shape[-1]
    lay = dict(heads=heads, gw=gw, ql=ql, kl=kl, aw=heads * V_DIM, u_off=0, v_off=gw, q_off=2 * gw,
               kv_off=2 * gw + ql, kr_off=2 * gw + ql + kl)
    lay["in_pad"] = _round_up(lay["kr_off"] + LANES, 2 * LANES if lay["kr_off"] + LANES <= 2048 else 1024)
    assert lay["q_off"] % ql == 0 and lay["kv_off"] % kl == 0 and lay["aw"] % gw == 0
    assert 4 * w_in.shape[-1] == ql + kl + ROPE_DIM + 2 * gw
    return lay


def _rope_tile(t1, t2, axis=-1):
    z = jnp.zeros_like(t1)
    return jnp.concatenate([t1, z, t2, z], axis=axis)


def _w_in_rows(gathered, shard_rows, lay):
    d = gathered.shape[-1]
    wt = gathered[:, :shard_rows].reshape(4 * shard_rows, d)
    ql, kl, gw = lay["ql"], lay["kl"], lay["gw"]
    q_c, kv_c = wt[:ql], wt[ql:ql + kl]
    r = wt[ql + kl:ql + kl + ROPE_DIM]
    u = wt[ql + kl + ROPE_DIM:ql + kl + ROPE_DIM + gw]
    v = wt[ql + kl + ROPE_DIM + gw:]
    parts = [u, v, q_c, kv_c, _rope_tile(r[:ROPE_HALF], r[ROPE_HALF:], axis=0)]
    pad = lay["in_pad"] - (lay["kr_off"] + LANES)
    if pad:
        parts.append(jnp.zeros((pad, d), wt.dtype))
    return jnp.concatenate(parts, axis=0)


def _w_in_grad_chunks(dwt, shard_rows, padded_rows, lay):
    d = dwt.shape[-1]
    ql, kl, gw = lay["ql"], lay["kl"], lay["gw"]
    ko = lay["kr_off"]
    rows = jnp.concatenate([dwt[lay["q_off"]:lay["q_off"] + ql], dwt[lay["kv_off"]:lay["kv_off"] + kl],
                            dwt[ko:ko + ROPE_HALF], dwt[ko + 2 * ROPE_HALF:ko + 3 * ROPE_HALF],
                            dwt[:gw], dwt[gw:2 * gw]], axis=0).reshape(4, shard_rows, d)
    rows = jnp.pad(rows, ((0, 0), (0, padded_rows - shard_rows), (0, 0)))
    return jnp.transpose(rows.reshape(4, 2, padded_rows // 2, d), (1, 0, 2, 3)).astype(BF16)


def _w_uq_padded(w, heads):
    w3 = w.reshape(w.shape[0], heads, NOPE_DIM + ROPE_DIM)
    t = _rope_tile(w3[..., NOPE_DIM:NOPE_DIM + ROPE_HALF], w3[..., NOPE_DIM + ROPE_HALF:])
    return jnp.concatenate([w3[..., :NOPE_DIM], t], axis=-1).reshape(w.shape[0], heads * HEAD_PAD)


def _w_uq_grad_unpadded(dw, heads):
    d3 = dw.reshape(dw.shape[0], heads, HEAD_PAD)
    return jnp.concatenate([d3[..., :NOPE_DIM], d3[..., NOPE_DIM:NOPE_DIM + ROPE_HALF],
                            d3[..., NOPE_DIM + 2 * ROPE_HALF:NOPE_DIM + 3 * ROPE_HALF]],
                           axis=-1).reshape(dw.shape[0], heads * (NOPE_DIM + ROPE_DIM))


def _cols_gathered(g):
    return jnp.transpose(g, (1, 0, 2)).reshape(g.shape[1], 4 * g.shape[2])


def _chunks_of_cols(grad):
    r, c4 = grad.shape
    return jnp.transpose(grad.reshape(2, r // 2, 4, c4 // 4), (0, 2, 1, 3)).astype(BF16)


SMALL = ["pre_mix_norm", "q_norm", "kv_norm", "v_ln_gain", "v_ln_bias", "w_spatial", "b_spatial", "attn_out_norm",
         "gmlp_out_norm", "post_mix_norm", "pre_ffn_norm", "post_ffn_norm"]
BIG = ["w_in", "w_uq", "w_ukv", "w_out", "w_gate", "w_up", "w_down"]
GATHER_NOW = ["w_in"]
GATHER_SOON = ["w_uq", "w_ukv"]
GATHER_LATER = ["w_out", "w_gate", "w_up", "w_down"]
REDUCE_FFN = ["w_gate", "w_up", "w_down"]
REDUCE_OUT = ["w_out"]
REDUCE_LAST = ["w_in", "w_uq", "w_ukv"]
TRANSPOSED = ("w_in", "w_gate", "w_up")
ORDER = ["pre_mix_norm", "w_in", "q_norm", "kv_norm", "w_uq", "w_ukv", "v_ln_gain", "v_ln_bias", "w_spatial",
         "b_spatial", "attn_out_norm", "gmlp_out_norm", "w_out", "post_mix_norm", "pre_ffn_norm", "w_gate", "w_up",
         "w_down", "post_ffn_norm"]


def _pack(arrays):
    flat = jnp.concatenate([a.reshape(-1) for a in arrays])
    n = flat.shape[0]
    total = _round_up(n, 8 * LANES)
    if total > n:
        flat = jnp.concatenate([flat, jnp.zeros((total - n,), F32)])
    return flat.reshape(total // LANES, LANES)


def _unpack(packed, like):
    flat = packed.reshape(-1)
    out, off = [], 0
    for a in like:
        out.append(flat[off:off + a.size].reshape(a.shape))
        off += a.size
    return out


def kernel(x, positions, pre_mix_norm, w_in, q_norm, kv_norm, w_uq, w_ukv, v_ln_gain, v_ln_bias, w_spatial, b_spatial, attn_out_norm, gmlp_out_norm, w_out, post_mix_norm, pre_ffn_norm, w_gate, w_up, w_down, post_ffn_norm, loss_target, m_pre_mix_norm, m_w_in, m_q_norm, m_kv_norm, m_w_uq, m_w_ukv, m_v_ln_gain, m_v_ln_bias, m_w_spatial, m_b_spatial, m_attn_out_norm, m_gmlp_out_norm, m_w_out, m_post_mix_norm, m_pre_ffn_norm, m_w_gate, m_w_up, m_w_down, m_post_ffn_norm, v_pre_mix_norm, v_w_in, v_q_norm, v_kv_norm, v_w_uq, v_w_ukv, v_v_ln_gain, v_v_ln_bias, v_w_spatial, v_b_spatial, v_attn_out_norm, v_gmlp_out_norm, v_w_out, v_post_mix_norm, v_pre_ffn_norm, v_w_gate, v_w_up, v_w_down, v_post_ffn_norm):
    args = dict(locals())
    weights = {n: args[n] for n in ORDER}
    m_in = {n: args["m_" + n] for n in ORDER}
    v_in = {n: args["v_" + n] for n in ORDER}

    lay = _layout(w_in, w_uq, w_ukv, v_ln_gain, q_norm, kv_norm)
    heads, gw = lay["heads"], lay["gw"]
    t, d = x.shape[1], x.shape[2]
    tr = 128 if t % 128 == 0 else t
    xs = x.reshape(t, d)
    target = loss_target.reshape(t, d)

    ffs, ins = w_gate.shape[-1], w_in.shape[-1]
    ffp, inp = _round_up(ffs, LANES), _round_up(ins, LANES)
    padded_rows = {"w_gate": ffp, "w_up": ffp, "w_down": ffp, "w_in": inp}
    shards, halved = {}, {}

    def cast_shard(n, one=None):
        w = jnp.swapaxes(weights[n][0], 0, 1) if n in TRANSPOSED else weights[n][0]
        s = (w if one is None else w * one).astype(BF16)
        if n in padded_rows:
            s = jnp.pad(s, ((0, padded_rows[n] - s.shape[0]), (0, 0)))
        shards[n] = s
        halved[n] = s.reshape(2, s.shape[0] // 2, s.shape[1])

    for n in GATHER_NOW:
        cast_shard(n)
    full = {}

    def pair_sums_of(partial, names, tag):
        from_sibling = _sibling_exchange([partial[n] for n in names], "grads_sibling_exchange_" + tag)
        return [_pair_add(partial[n], r, "pair_add_" + n) for n, r in zip(names, from_sibling)]

    def place(names, lands):
        for n, g in zip(names, lands):
            full[n] = g.reshape((4,) + shards[n].shape)

    flight_0 = _ici_start("gather", [halved[n] for n in GATHER_NOW], "gather_start_0")
    for n in GATHER_SOON + GATHER_LATER:
        cast_shard(n, 1.0 + flight_0[4][0, 0])
    flight_soon = _ici_start("gather", [halved[n] for n in GATHER_SOON], "gather_start_soon", after=flight_0[4])
    flights, last_token = {}, flight_soon[4]
    for n in GATHER_LATER:
        flights[n] = _ici_start("gather", [halved[n]], "gather_start_" + n, after=last_token)
        last_token = flights[n][4]
    xn = _prenorm(xs, pre_mix_norm.reshape(1, -1), tr, last_token)
    _, lands = _ici_wait("gather", *flight_0[:4], xn, "gather_wait_0")
    pass_now = _ici_start("forward", lands, "forward_start_now")
    place(GATHER_NOW, _ici_wait("forward", *pass_now[:4], pass_now[4], "forward_wait_now")[0])
    wt_in = _w_in_rows(full["w_in"], ins, lay)

    inv_freq = 1.0 / (ROPE_THETA ** (jnp.arange(0, ROPE_DIM, 2, dtype=F32) / ROPE_DIM))
    ang = positions.reshape(t).astype(F32)[:, None] * inv_freq
    cos, sin = jnp.cos(ang), jnp.sin(ang)
    cos_t = _rope_tile(cos, cos)
    sin_t = _rope_tile(-sin, sin)

    row = lambda a: a.reshape(1, -1)
    g_pre, g_q, g_kv = row(pre_mix_norm), row(q_norm), row(kv_norm)
    g_a, g_g, g_pm = row(attn_out_norm), row(gmlp_out_norm), row(post_mix_norm)
    g_pf, g_po = row(pre_ffn_norm), row(post_ffn_norm)
    ln_g, ln_b = row(v_ln_gain), row(v_ln_bias)
    ws = w_spatial[0].astype(BF16)
    ws_t = jnp.transpose(ws, (0, 2, 1))
    bs_wide = jnp.broadcast_to(b_spatial[0][:, :, None], b_spatial.shape[1:] + (G_HEAD_DIM,))

    proj = _matmul(xn, wt_in, NT, F32, "proj")
    _, lands = _ici_wait("gather", *flight_soon[:4], proj, "gather_wait_soon")
    pass_soon = _ici_start("forward", lands, "forward_start_soon")
    qn, kvn, kr = _qkv_prep(proj, g_q, g_kv, cos_t, sin_t, lay, tr, pass_soon[4])
    place(GATHER_SOON, _ici_wait("forward", *pass_soon[:4], qn, "forward_wait_soon")[0])
    wb_uq = _w_uq_padded(_cols_gathered(full["w_uq"]), heads)
    wb_ukv = _cols_gathered(full["w_ukv"])
    q = _q_rope(_matmul(qn, wb_uq, NN, F32, "q_up"), cos_t, sin_t, heads, tr)
    kv = _matmul(kvn, wb_ukv, NN, BF16, "kv_up")
    a_out, a_lse = _attn_fwd(q, kv, kr, heads)
    def arrive(n, after):
        _, lands = _ici_wait("gather", *flights[n][:4], after, "gather_wait_" + n)
        return _ici_start("forward", lands, "forward_start_" + n)

    def settle(n, passing, after):
        place([n], _ici_wait("forward", *passing[:4], after, "forward_wait_" + n)[0])

    pass_out = arrive("w_out", a_out)
    gn = _gmlp_fwd(proj, ln_g, ln_b, ws, bs_wide, g_g, lay)
    mixed = _mix_norm(a_out, gn, g_a, tr, pass_out[4])
    settle("w_out", pass_out, mixed)
    pass_gate = arrive("w_gate", mixed)
    wb_out = full["w_out"].reshape(-1, d)
    mix_out = _matmul(mixed, wb_out, NN, F32, "mix_out", token=pass_gate[4])
    h, hn = _post_mix(xs, mix_out, g_pm, g_pf, tr)
    settle("w_gate", pass_gate, hn)
    pass_up = arrive("w_up", hn)
    wt_gate = full["w_gate"].reshape(4 * ffp, d)
    gate = _matmul(hn, wt_gate, NT, BF16, "ffn_gate", token=pass_up[4])
    settle("w_up", pass_up, gate)
    pass_down = arrive("w_down", gate)
    wt_up = full["w_up"].reshape(4 * ffp, d)
    up = _matmul(hn, wt_up, NT, BF16, "ffn_up", token=pass_down[4])
    act = _swiglu(gate, up)
    settle("w_down", pass_down, act)
    wb_down = full["w_down"].reshape(4 * ffp, d)
    ffn = _matmul(act, wb_down, NN, F32, "ffn_down")
    dy, d_ffn, dg_po, loss_vec = _loss_head(h, ffn, target, g_po, tr)

    d_act = _matmul(d_ffn, wb_down, NT, BF16, "d_act")
    d_gate, d_up = _swiglu_bwd(gate, up, d_act)
    partial_ffn = [_matmul(d_gate, hn, TN, BF16, "gw_gate", out_chunks=True),
                   _matmul(d_up, hn, TN, BF16, "gw_up", out_chunks=True),
                   _matmul(act, d_ffn, TN, BF16, "gw_down", out_chunks=True)]
    swap_ffn = _ici_start("sibling", partial_ffn, "sibling_start_ffn")
    d_hn = _matmul(d_up, wt_up, NN, F32, "d_hn",
                   extras=[_matmul(d_gate, wt_gate, NN, F32, "d_hn_gate", token=swap_ffn[4])],
                   epilogue=lambda acc, partial: acc + partial)
    partial_ffn, from_sibling = _ici_wait("sibling", *swap_ffn[:4], d_hn, "sibling_wait_ffn")
    pair_ffn = [_pair_add(p, r, "pair_add_" + n) for n, p, r in zip(REDUCE_FFN, partial_ffn, from_sibling)]
    flight_ffn = _ici_start("scatter", pair_ffn, "scatter_start_ffn")
    dh, d_mo, dg_pf, dg_pm = _post_mix_bwd(mix_out, h, dy, d_hn, g_pm, g_pf, tr, flight_ffn[4])
    d_mixed = _matmul(d_mo, wb_out, NT, F32, "d_mixed")
    gw_out = _matmul(mixed, d_mo, TN, BF16, "gw_out", out_chunks=True)
    pair_out = pair_sums_of({"w_out": gw_out}, REDUCE_OUT, "out")
    flight_out = _ici_start("scatter", pair_out, "scatter_start_out")
    d_a, dg_a = _mix_norm_bwd(a_out, d_mixed, g_a, tr, flight_out[4])
    d_u, d_v, dg_g, d_ln_g, d_ln_b, d_ws, d_bs_wide = _gmlp_bwd(proj, d_mixed, ln_g, ln_b, ws, ws_t, bs_wide, g_g, lay)
    d_bs = _spatial_bias_grad(d_bs_wide)
    d_q, d_kv, d_kr = _attn_bwd(q, kv, kr, a_out, a_lse, d_a, cos_t, sin_t, heads)
    d_qn = _matmul(d_q, wb_uq, NT, F32, "d_qn")
    gw_uq = _matmul(qn, d_q, TN, BF16, "gw_uq")
    d_kvn = _matmul(d_kv, wb_ukv, NT, F32, "d_kvn")
    gw_ukv = _matmul(kvn, d_kv, TN, BF16, "gw_ukv")
    d_qc, d_kvc, d_krt, dg_q, dg_kv = _qkv_bwd(proj, d_qn, d_kvn, d_kr, g_q, g_kv, cos_t, sin_t, lay, tr)
    parts = [d_u, d_v, d_qc, d_kvc, d_krt]
    pad = lay["in_pad"] - (lay["kr_off"] + LANES)
    if pad:
        parts.append(jnp.zeros((t, pad), BF16))
    d_proj = jnp.concatenate(parts, axis=1)
    d_xn = _matmul(d_proj, wt_in, NN, F32, "d_xn")
    gw_in = _matmul(d_proj, xn, TN, BF16, "gw_in")
    grad_x, dg_pre = _prenorm_bwd(xs, d_xn, dh, g_pre, tr)

    partial_mix = [_w_in_grad_chunks(gw_in, ins, inp, lay), _chunks_of_cols(_w_uq_grad_unpadded(gw_uq, heads)),
                   _chunks_of_cols(gw_ukv)]
    swap_mix = _ici_start("sibling", partial_mix, "sibling_start_mix")
    pair_ffn, received_ffn = _ici_wait("scatter", *flight_ffn[:4], swap_mix[4], "scatter_wait_ffn")
    pair_out, received_out = _ici_wait("scatter", *flight_out[:4], swap_mix[4], "scatter_wait_out")
    grads, delta, new_m, new_v = {}, {}, {}, {}

    def chunk_sums(names, pair_sums, received, in_flight):
        mine, swaps, token = [], [], None
        for n, p, r in zip(names, pair_sums, received):
            mine.append(_chip_sum(p, r, "chip_sum_" + n, token))
            if in_flight:
                swaps.append(_ici_start("swap", [mine[-1]], "swap_start_" + n))
                token = swaps[-1][4]
        return mine, swaps

    def update(names, mine, swaps, tag, previous):
        if not swaps:
            theirs = _halves_exchange(mine, "grads_halves_exchange_" + tag)
        for i, n in enumerate(names):
            if swaps:
                (g_mine,), (g_theirs,) = _ici_wait("swap", *swaps[i][:4], previous, "swap_wait_" + n)
            else:
                g_mine, g_theirs = mine[i], theirs[i]
            shape = weights[n].shape
            if n in TRANSPOSED:
                view = lambda a: jnp.swapaxes(a[0], 0, 1)
                back = lambda o: jnp.swapaxes(o, 0, 1).reshape(shape)
            else:
                view = lambda a: a[0]
                back = lambda o: o.reshape(shape)
            out = _adamw_halves(view(weights[n]), g_mine, g_theirs, view(m_in[n]), view(v_in[n]), "adamw_" + n)
            grads[n], delta[n], new_m[n], new_v[n] = [back(o) for o in out]
            previous = out[-1]

    first = REDUCE_FFN + REDUCE_OUT
    mine, swaps = chunk_sums(first, pair_ffn + pair_out, received_ffn + received_out, True)
    partial_mix, from_sibling = _ici_wait("sibling", *swap_mix[:4], mine[-1], "sibling_wait_mix")
    pair_mix = [_pair_add(p, r, "pair_add_" + n) for n, p, r in zip(REDUCE_LAST, partial_mix, from_sibling)]
    flight_mix = _ici_start("scatter", pair_mix, "scatter_start_mix")
    update(first, mine, swaps, "ffn", flight_mix[4])
    pair_mix, received = _ici_wait("scatter", *flight_mix[:4], new_v[REDUCE_OUT[-1]], "scatter_wait_mix")
    mine, _ = chunk_sums(REDUCE_LAST, pair_mix, received, False)
    update(REDUCE_LAST, mine, [], "mix", None)

    small_grads = {"pre_mix_norm": dg_pre, "q_norm": dg_q, "kv_norm": dg_kv, "v_ln_gain": d_ln_g, "v_ln_bias": d_ln_b,
                   "w_spatial": d_ws, "b_spatial": d_bs, "attn_out_norm": dg_a, "gmlp_out_norm": dg_g,
                   "post_mix_norm": dg_pm, "pre_ffn_norm": dg_pf, "post_ffn_norm": dg_po}
    like = [weights[n] for n in SMALL]
    reduced = _small_all_reduce(_pack([small_grads[n] for n in SMALL] + [loss_vec]))
    loss = reduced.reshape(-1)[sum(a.size for a in like)]
    small_g = _pack(_unpack(reduced, like))
    s_delta, s_m, s_v = _adamw(_pack(like), small_g, _pack([m_in[n] for n in SMALL]),
                               _pack([v_in[n] for n in SMALL]), "adamw_small")
    for n, g in zip(SMALL, _unpack(small_g, like)):
        grads[n] = g
    delta.update(zip(SMALL, _unpack(s_delta, like)))
    new_m.update(zip(SMALL, _unpack(s_m, like)))
    new_v.update(zip(SMALL, _unpack(s_v, like)))

    return (loss, grad_x.reshape(x.shape), *[grads[n] for n in ORDER], *[delta[n] for n in ORDER],
            *[new_m[n] for n in ORDER], *[new_v[n] for n in ORDER])
```
